```python
import math
import jax, jax.numpy as jnp
from jax import lax
import numpy as np

D_MODEL = 1024
BATCH = 8
SEQ = 16384
DEPTH = 1

CTX_LEN = 256
GRID_W = 64
DN_HEAD_DIM = 128
DN_HEADS = D_MODEL // DN_HEAD_DIM
DN_DIM = DN_HEADS * DN_HEAD_DIM
SHORT_CONV = 5
CHUNK = 64
HEAD_DIM = 128
ATTN_HEADS = D_MODEL // HEAD_DIM
ATTN_KV_HEADS = ATTN_HEADS // 4
ATTN_GROUPS = ATTN_HEADS // ATTN_KV_HEADS
ATTN_DIM = ATTN_HEADS * HEAD_DIM
KV_DIM = ATTN_KV_HEADS * HEAD_DIM
WINDOW = 128
ATTN_BLOCK = 128
ROPE_BASE = 10000.0
ROPE_FREQS = HEAD_DIM // 4
D_FF = ((8 * D_MODEL // 3 + 127) // 128) * 128
FFN_CONV = 3
RMS_EPS = 1e-6
IN_SIZES = (3 * DN_DIM, DN_DIM, 2 * DN_HEADS, 2 * DN_HEADS, ATTN_DIM, KV_DIM, KV_DIM, 2 * D_MODEL)
IN_DIM = 4 * DN_DIM + 4 * DN_HEADS + ATTN_DIM + 2 * KV_DIM + 2 * D_MODEL

kernel_name = 'hybrid_deltanet_swa_dit_layer'


def rms_norm(x, w):
    xf = x.astype(jnp.float32)
    y = xf * lax.rsqrt(jnp.mean(xf * xf, axis=-1, keepdims=True) + RMS_EPS)
    return y.astype(x.dtype) * w


def l2_normalize(x):
    return x * lax.rsqrt(jnp.sum(x * x, axis=-1, keepdims=True) + RMS_EPS)


def modulate(h, shift, scale):
    return h * (1.0 + scale) + shift


def dwconv_centred(x, w):
    width = w.shape[0]
    r = width // 2
    length = x.shape[1]
    xp = jnp.pad(x, ((0, 0), (r, r), (0, 0)))
    out = xp[:, :length] * w[0]
    for j in range(1, width):
        out = out + xp[:, j:j + length] * w[j]
    return out


def split_in(p):
    cuts = np.cumsum(IN_SIZES)[:-1].tolist()
    return jnp.split(p, cuts, axis=-1)


def rope_tables_2d(rows, dtype):
    row = jnp.broadcast_to(jnp.arange(rows, dtype=jnp.float32)[:, None], (rows, GRID_W)).reshape(-1)
    col = jnp.broadcast_to(jnp.arange(GRID_W, dtype=jnp.float32)[None, :], (rows, GRID_W)).reshape(-1)
    inv_freq = ROPE_BASE ** (-jnp.arange(ROPE_FREQS, dtype=jnp.float32) / ROPE_FREQS)
    ang_r = row[:, None] * inv_freq
    ang_c = col[:, None] * inv_freq
    return (jnp.cos(ang_r).astype(dtype), jnp.sin(ang_r).astype(dtype),
            jnp.cos(ang_c).astype(dtype), jnp.sin(ang_c).astype(dtype))


def rope_1d(x, cos, sin):
    x1, x2 = jnp.split(x, 2, axis=-1)
    cos = cos[None, :, None, :]
    sin = sin[None, :, None, :]
    return jnp.concatenate([x1 * cos - x2 * sin, x2 * cos + x1 * sin], axis=-1)


def rope_2d(x, tables):
    cos_r, sin_r, cos_c, sin_c = tables
    half = HEAD_DIM // 2
    return jnp.concatenate([rope_1d(x[..., :half], cos_r, sin_r),
                            rope_1d(x[..., half:], cos_c, sin_c)], axis=-1)


def dn_prepare(qkv, b, a, conv_w, a_log, dt_bias):
    bsz, length = qkv.shape[:2]
    qkv = jax.nn.silu(dwconv_centred(qkv, conv_w)).astype(jnp.float32)
    q, k, v = jnp.split(qkv, 3, axis=-1)
    q = l2_normalize(q.reshape(bsz, length, DN_HEADS, DN_HEAD_DIM)) * (DN_HEAD_DIM ** -0.5)
    k = l2_normalize(k.reshape(bsz, length, DN_HEADS, DN_HEAD_DIM))
    v = v.reshape(bsz, length, DN_HEADS, DN_HEAD_DIM)
    beta = jax.nn.sigmoid(b.astype(jnp.float32)).reshape(bsz, length, 2, DN_HEADS)
    a = a.astype(jnp.float32).reshape(bsz, length, 2, DN_HEADS)
    g = -jnp.exp(a_log.astype(jnp.float32)) * jax.nn.softplus(a + dt_bias.astype(jnp.float32))
    return q, k, v, g, beta


def gated_delta_chunked(q, k, v, g, beta, s0):
    bsz, length, heads, _ = q.shape
    dv = v.shape[-1]
    n = length // CHUNK

    def to_chunks(t):
        t = t.reshape((bsz, n, CHUNK, heads) + t.shape[3:])
        return jnp.moveaxis(t, (1, 3), (0, 2))

    qc, kc, vc, bc = to_chunks(q), to_chunks(k), to_chunks(v), to_chunks(beta)
    gc = jnp.cumsum(to_chunks(g), axis=-1)
    idx = jnp.arange(CHUNK)
    incl = idx[:, None] >= idx[None, :]
    strict = idx[:, None] > idx[None, :]
    diff = gc[..., :, None] - gc[..., None, :]
    decay = jnp.where(incl, jnp.exp(jnp.where(incl, diff, 0.0)), 0.0)
    kk = jnp.einsum('nbhid,nbhjd->nbhij', kc, kc)
    a_mat = jnp.where(strict, bc[..., :, None] * kk * decay, 0.0)
    rhs = jnp.concatenate([vc * bc[..., None], kc * (bc * jnp.exp(gc))[..., None]], axis=-1)
    sol = lax.linalg.triangular_solve(a_mat, rhs, left_side=True, lower=True, unit_diagonal=True)
    u, w = sol[..., :dv], sol[..., dv:]
    qk = jnp.einsum('nbhid,nbhjd->nbhij', qc, kc) * decay
    qg = qc * jnp.exp(gc)[..., None]
    kd = kc * jnp.exp(gc[..., -1:] - gc)[..., None]
    g_last = jnp.exp(gc[..., -1])

    def step(s, xs):
        u_i, w_i, qg_i, qk_i, kd_i, gl_i = xs
        v_new = u_i - jnp.einsum('bhcd,bhde->bhce', w_i, s)
        o_i = jnp.einsum('bhcd,bhde->bhce', qg_i, s) + jnp.einsum('bhij,bhje->bhie', qk_i, v_new)
        s = s * gl_i[..., None, None] + jnp.einsum('bhcd,bhce->bhde', kd_i, v_new)
        return s, o_i

    s_fin, o = lax.scan(step, s0, (u, w, qg, qk, kd, g_last))
    o = jnp.moveaxis(o, (0, 2), (1, 3)).reshape(bsz, length, heads, dv)
    return o, s_fin


def bidirectional_delta(lat, ctx_side):
    qx, kx, vx, gx, bx = lat
    qc, kc, vc, gc, bc = ctx_side
    s0 = jnp.zeros((qc.shape[0], DN_HEADS, DN_HEAD_DIM, DN_HEAD_DIM), jnp.float32)
    rev = lambda t: jnp.flip(t, axis=1)
    oc_f, sc_f = gated_delta_chunked(qc, kc, vc, gc[:, :, 0], bc[:, :, 0], s0)
    ox_f, _ = gated_delta_chunked(qx, kx, vx, gx[:, :, 0], bx[:, :, 0], sc_f)
    oc_b, sc_b = gated_delta_chunked(rev(qc), rev(kc), rev(vc), rev(gc[:, :, 1]), rev(bc[:, :, 1]), s0)
    ox_b, _ = gated_delta_chunked(rev(qx), rev(kx), rev(vx), rev(gx[:, :, 1]), rev(bx[:, :, 1]), sc_b)
    return ox_f + rev(ox_b), oc_f + rev(oc_b)


def gated_head_norm(o, gate, w):
    bsz, length = o.shape[:2]
    y = rms_norm(o, w) * jax.nn.silu(gate.reshape(bsz, length, DN_HEADS, DN_HEAD_DIM).astype(jnp.float32))
    return y.reshape(bsz, length, DN_DIM).astype(gate.dtype)


def attn_heads(q, k, v, q_norm, k_norm):
    bsz, length = q.shape[:2]
    q = rms_norm(q.reshape(bsz, length, ATTN_HEADS, HEAD_DIM), q_norm)
    k = rms_norm(k.reshape(bsz, length, ATTN_KV_HEADS, HEAD_DIM), k_norm)
    v = v.reshape(bsz, length, ATTN_KV_HEADS, HEAD_DIM)
    return q, k, v


def window_attention_latent(q, k, v, k_ctx, v_ctx, sink):
    bsz, length = q.shape[:2]
    n_ctx = k_ctx.shape[1]
    nb = length // ATTN_BLOCK
    band = 3 * ATTN_BLOCK
    scale = HEAD_DIM ** -0.5
    pad = ((0, 0), (ATTN_BLOCK, ATTN_BLOCK), (0, 0), (0, 0))
    kp = jnp.pad(k, pad)
    vp = jnp.pad(v, pad)
    qb = jnp.moveaxis(q.reshape(bsz, nb, ATTN_BLOCK, ATTN_KV_HEADS, ATTN_GROUPS, HEAD_DIM), 1, 0)
    sink_l = jnp.broadcast_to(sink.astype(jnp.float32).reshape(1, ATTN_KV_HEADS, ATTN_GROUPS, 1, 1),
                              (bsz, ATTN_KV_HEADS, ATTN_GROUPS, ATTN_BLOCK, 1))

    def block(args):
        i, q_i = args
        start = i * ATTN_BLOCK
        k_i = lax.dynamic_slice_in_dim(kp, start, band, axis=1)
        v_i = lax.dynamic_slice_in_dim(vp, start, band, axis=1)
        qpos = start + jnp.arange(ATTN_BLOCK)
        kpos = start - ATTN_BLOCK + jnp.arange(band)
        ok = (jnp.abs(qpos[:, None] - kpos[None, :]) <= WINDOW) & (kpos[None, :] >= 0) & (kpos[None, :] < length)
        s_win = jnp.einsum('bqkgd,bskd->bkgqs', q_i, k_i).astype(jnp.float32) * scale
        s_win = jnp.where(ok, s_win, -jnp.inf)
        s_ctx = jnp.einsum('bqkgd,bskd->bkgqs', q_i, k_ctx).astype(jnp.float32) * scale
        p = jax.nn.softmax(jnp.concatenate([s_win, s_ctx, sink_l], axis=-1), axis=-1).astype(v.dtype)
        return (jnp.einsum('bkgqs,bskd->bqkgd', p[..., :band], v_i)
                + jnp.einsum('bkgqs,bskd->bqkgd', p[..., band:band + n_ctx], v_ctx))

    o = lax.map(block, (jnp.arange(nb), qb))
    return jnp.moveaxis(o, 0, 1).reshape(bsz, length, ATTN_DIM)


def context_attention(q, k, v, sink):
    bsz, n_ctx = q.shape[:2]
    qg = q.reshape(bsz, n_ctx, ATTN_KV_HEADS, ATTN_GROUPS, HEAD_DIM)
    s = jnp.einsum('bqkgd,bskd->bkgqs', qg, k).astype(jnp.float32) * (HEAD_DIM ** -0.5)
    sink_c = jnp.broadcast_to(sink.astype(jnp.float32).reshape(1, ATTN_KV_HEADS, ATTN_GROUPS, 1, 1),
                              (bsz, ATTN_KV_HEADS, ATTN_GROUPS, n_ctx, 1))
    p = jax.nn.softmax(jnp.concatenate([s, sink_c], axis=-1), axis=-1).astype(v.dtype)
    o = jnp.einsum('bkgqs,bskd->bqkgd', p[..., :n_ctx], v)
    return o.reshape(bsz, n_ctx, ATTN_DIM)


def merge_branches(y_dn, y_at, gates, w_branch_dn, w_branch_attn, w_out):
    g_dn, g_at = jnp.split(gates, 2, axis=-1)
    merged = jax.nn.sigmoid(g_dn) * (y_dn @ w_branch_dn) + jax.nn.sigmoid(g_at) * (y_at @ w_branch_attn)
    return merged @ w_out


def conv_ffn(h, up, conv_w, conv_b, down):
    u = dwconv_centred(h @ up, conv_w) + conv_b
    u_gate, u_val = jnp.split(u, 2, axis=-1)
    return (jax.nn.silu(u_gate) * u_val) @ down


def hybrid_layer(x, ctx, c, c_ctx, rope, w_ada, b_ada, norm_mix, norm_ffn, w_in, dn_conv, dn_a_log,
                 dn_dt_bias, dn_norm, q_norm, k_norm, attn_sink, w_branch_dn, w_branch_attn, w_out,
                 ffn_up, ffn_conv, ffn_conv_b, ffn_down, update_ctx):
    mod_x = (jax.nn.silu(c) @ w_ada + b_ada)[:, None, :]
    mod_c = (jax.nn.silu(c_ctx) @ w_ada + b_ada)[None, None, :]
    sh_a_x, sc_a_x, g_a_x, sh_f_x, sc_f_x, g_f_x = jnp.split(mod_x, 6, axis=-1)
    sh_a_c, sc_a_c, g_a_c, sh_f_c, sc_f_c, g_f_c = jnp.split(mod_c, 6, axis=-1)

    p_x = modulate(rms_norm(x, norm_mix), sh_a_x, sc_a_x) @ w_in
    p_c = modulate(rms_norm(ctx, norm_mix), sh_a_c, sc_a_c) @ w_in
    qkv_dn_x, gt_dn_x, b_dn_x, a_dn_x, q_at_x, k_at_x, v_at_x, mg_x = split_in(p_x)
    qkv_dn_c, gt_dn_c, b_dn_c, a_dn_c, q_at_c, k_at_c, v_at_c, mg_c = split_in(p_c)

    dn_x = dn_prepare(qkv_dn_x, b_dn_x, a_dn_x, dn_conv, dn_a_log, dn_dt_bias)
    dn_c = dn_prepare(qkv_dn_c, b_dn_c, a_dn_c, dn_conv, dn_a_log, dn_dt_bias)
    o_dn_x, o_dn_c = bidirectional_delta(dn_x, dn_c)
    y_dn_x = gated_head_norm(o_dn_x, gt_dn_x, dn_norm)

    qx, kx, vx = attn_heads(q_at_x, k_at_x, v_at_x, q_norm, k_norm)
    qc, kc, vc = attn_heads(q_at_c, k_at_c, v_at_c, q_norm, k_norm)
    qx = rope_2d(qx, rope)
    kx = rope_2d(kx, rope)
    y_at_x = window_attention_latent(qx, kx, vx, kc, vc, attn_sink)

    x = x + g_a_x * merge_branches(y_dn_x, y_at_x, mg_x, w_branch_dn, w_branch_attn, w_out)
    x = x + g_f_x * conv_ffn(modulate(rms_norm(x, norm_ffn), sh_f_x, sc_f_x), ffn_up, ffn_conv, ffn_conv_b, ffn_down)

    if update_ctx:
        y_dn_c = gated_head_norm(o_dn_c, gt_dn_c, dn_norm)
        y_at_c = context_attention(qc, kc, vc, attn_sink)
        ctx = ctx + g_a_c * merge_branches(y_dn_c, y_at_c, mg_c, w_branch_dn, w_branch_attn, w_out)
        ctx = ctx + g_f_c * conv_ffn(modulate(rms_norm(ctx, norm_ffn), sh_f_c, sc_f_c), ffn_up, ffn_conv, ffn_conv_b, ffn_down)
    return x, ctx


def _fwd_setup_inputs(seed: int = 0) -> dict:
    key = jax.random.key(seed)
    ks = jax.random.split(key, 24)

    def nrm(k, shape, s):
        return jax.random.normal(k, shape, jnp.float32) * s

    x = nrm(ks[0], (BATCH, SEQ, D_MODEL), 1.0)
    c = nrm(ks[1], (BATCH, D_MODEL), 1.0)
    ctx = nrm(ks[2], (BATCH, CTX_LEN, D_MODEL), 1.0)
    c_ctx = nrm(ks[3], (D_MODEL,), 1.0)
    w_ada = nrm(ks[4], (DEPTH, D_MODEL, 6 * D_MODEL), 0.5 * D_MODEL ** -0.5)
    b_ada = nrm(ks[5], (DEPTH, 6 * D_MODEL), 0.02)
    norm_mix = 1.0 + nrm(ks[6], (DEPTH, D_MODEL), 0.1)
    norm_ffn = 1.0 + nrm(ks[7], (DEPTH, D_MODEL), 0.1)
    w_in = nrm(ks[8], (DEPTH, D_MODEL, IN_DIM), D_MODEL ** -0.5)
    dn_conv = nrm(ks[9], (DEPTH, SHORT_CONV, 3 * DN_DIM), SHORT_CONV ** -0.5)
    dn_a_log = jnp.log(jax.random.uniform(ks[10], (DEPTH, 2, DN_HEADS), jnp.float32, minval=1.0, maxval=16.0))
    dt = jnp.exp(jax.random.uniform(ks[11], (DEPTH, 2, DN_HEADS), jnp.float32,
                                    minval=math.log(1e-3), maxval=math.log(1e-1)))
    dn_dt_bias = dt + jnp.log(-jnp.expm1(-dt))
    dn_norm = 1.0 + nrm(ks[12], (DEPTH, DN_HEAD_DIM), 0.1)
    q_norm = 1.0 + nrm(ks[13], (DEPTH, HEAD_DIM), 0.1)
    k_norm = 1.0 + nrm(ks[14], (DEPTH, HEAD_DIM), 0.1)
    attn_sink = nrm(ks[15], (DEPTH, ATTN_HEADS), 0.5)
    w_branch_dn = nrm(ks[16], (DEPTH, DN_DIM, D_MODEL), DN_DIM ** -0.5)
    w_branch_attn = nrm(ks[17], (DEPTH, ATTN_DIM, D_MODEL), ATTN_DIM ** -0.5)
    w_out = nrm(ks[18], (DEPTH, D_MODEL, D_MODEL), D_MODEL ** -0.5)
    ffn_up = nrm(ks[19], (DEPTH, D_MODEL, 2 * D_FF), D_MODEL ** -0.5)
    ffn_conv = nrm(ks[20], (DEPTH, FFN_CONV, 2 * D_FF), FFN_CONV ** -0.5)
    ffn_conv_b = nrm(ks[21], (DEPTH, 2 * D_FF), 0.02)
    ffn_down = nrm(ks[22], (DEPTH, D_FF, D_MODEL), D_FF ** -0.5)
    return {'x': x, 'c': c, 'ctx': ctx, 'c_ctx': c_ctx, 'w_ada': w_ada, 'b_ada': b_ada,
            'norm_mix': norm_mix, 'norm_ffn': norm_ffn, 'w_in': w_in, 'dn_conv': dn_conv,
            'dn_a_log': dn_a_log, 'dn_dt_bias': dn_dt_bias, 'dn_norm': dn_norm, 'q_norm': q_norm,
            'k_norm': k_norm, 'attn_sink': attn_sink, 'w_branch_dn': w_branch_dn,
            'w_branch_attn': w_branch_attn, 'w_out': w_out, 'ffn_up': ffn_up, 'ffn_conv': ffn_conv,
            'ffn_conv_b': ffn_conv_b, 'ffn_down': ffn_down}


def _fwd_reference(x, c, ctx, c_ctx, w_ada, b_ada, norm_mix, norm_ffn, w_in, dn_conv, dn_a_log, dn_dt_bias,
              dn_norm, q_norm, k_norm, attn_sink, w_branch_dn, w_branch_attn, w_out, ffn_up, ffn_conv,
              ffn_conv_b, ffn_down):
    rows = x.shape[1] // GRID_W
    rope = rope_tables_2d(rows, x.dtype)
    for layer in range(DEPTH):
        x, ctx = hybrid_layer(x, ctx, c, c_ctx, rope, w_ada[layer], b_ada[layer], norm_mix[layer],
                              norm_ffn[layer], w_in[layer], dn_conv[layer], dn_a_log[layer],
                              dn_dt_bias[layer], dn_norm[layer], q_norm[layer], k_norm[layer],
                              attn_sink[layer], w_branch_dn[layer], w_branch_attn[layer], w_out[layer],
                              ffn_up[layer], ffn_conv[layer], ffn_conv_b[layer], ffn_down[layer],
                              update_ctx=layer + 1 < DEPTH)
    return x


import jax as _jax
import jax.numpy as _jnp

TWIN_FORMAT = 'train_step'
FWD_PARAMS = ['x', 'c', 'ctx', 'c_ctx', 'w_ada', 'b_ada', 'norm_mix', 'norm_ffn', 'w_in', 'dn_conv', 'dn_a_log', 'dn_dt_bias', 'dn_norm', 'q_norm', 'k_norm', 'attn_sink', 'w_branch_dn', 'w_branch_attn', 'w_out', 'ffn_up', 'ffn_conv', 'ffn_conv_b', 'ffn_down']
TWIN_WEIGHTS = ['c_ctx', 'w_ada', 'b_ada', 'norm_mix', 'norm_ffn', 'w_in', 'dn_conv', 'dn_a_log', 'dn_dt_bias', 'dn_norm', 'q_norm', 'k_norm', 'attn_sink', 'w_branch_dn', 'w_branch_attn', 'w_out', 'ffn_up', 'ffn_conv', 'ffn_conv_b', 'ffn_down']
TWIN_DIFF_INPUT = 'x'
TWIN_INPUTS = ['x', 'c', 'ctx', 'c_ctx', 'w_ada', 'b_ada', 'norm_mix', 'norm_ffn', 'w_in', 'dn_conv', 'dn_a_log', 'dn_dt_bias', 'dn_norm', 'q_norm', 'k_norm', 'attn_sink', 'w_branch_dn', 'w_branch_attn', 'w_out', 'ffn_up', 'ffn_conv', 'ffn_conv_b', 'ffn_down', 'loss_target', 'm_c_ctx', 'm_w_ada', 'm_b_ada', 'm_norm_mix', 'm_norm_ffn', 'm_w_in', 'm_dn_conv', 'm_dn_a_log', 'm_dn_dt_bias', 'm_dn_norm', 'm_q_norm', 'm_k_norm', 'm_attn_sink', 'm_w_branch_dn', 'm_w_branch_attn', 'm_w_out', 'm_ffn_up', 'm_ffn_conv', 'm_ffn_conv_b', 'm_ffn_down', 'v_c_ctx', 'v_w_ada', 'v_b_ada', 'v_norm_mix', 'v_norm_ffn', 'v_w_in', 'v_dn_conv', 'v_dn_a_log', 'v_dn_dt_bias', 'v_dn_norm', 'v_q_norm', 'v_k_norm', 'v_attn_sink', 'v_w_branch_dn', 'v_w_branch_attn', 'v_w_out', 'v_ffn_up', 'v_ffn_conv', 'v_ffn_conv_b', 'v_ffn_down']
TWIN_OUTPUTS = ['loss', 'grad_x', 'grad_c_ctx', 'grad_w_ada', 'grad_b_ada', 'grad_norm_mix', 'grad_norm_ffn', 'grad_w_in', 'grad_dn_conv', 'grad_dn_a_log', 'grad_dn_dt_bias', 'grad_dn_norm', 'grad_q_norm', 'grad_k_norm', 'grad_attn_sink', 'grad_w_branch_dn', 'grad_w_branch_attn', 'grad_w_out', 'grad_ffn_up', 'grad_ffn_conv', 'grad_ffn_conv_b', 'grad_ffn_down', 'delta_c_ctx', 'delta_w_ada', 'delta_b_ada', 'delta_norm_mix', 'delta_norm_ffn', 'delta_w_in', 'delta_dn_conv', 'delta_dn_a_log', 'delta_dn_dt_bias', 'delta_dn_norm', 'delta_q_norm', 'delta_k_norm', 'delta_attn_sink', 'delta_w_branch_dn', 'delta_w_branch_attn', 'delta_w_out', 'delta_ffn_up', 'delta_ffn_conv', 'delta_ffn_conv_b', 'delta_ffn_down', 'new_m_c_ctx', 'new_m_w_ada', 'new_m_b_ada', 'new_m_norm_mix', 'new_m_norm_ffn', 'new_m_w_in', 'new_m_dn_conv', 'new_m_dn_a_log', 'new_m_dn_dt_bias', 'new_m_dn_norm', 'new_m_q_norm', 'new_m_k_norm', 'new_m_attn_sink', 'new_m_w_branch_dn', 'new_m_w_branch_attn', 'new_m_w_out', 'new_m_ffn_up', 'new_m_ffn_conv', 'new_m_ffn_conv_b', 'new_m_ffn_down', 'new_v_c_ctx', 'new_v_w_ada', 'new_v_b_ada', 'new_v_norm_mix', 'new_v_norm_ffn', 'new_v_w_in', 'new_v_dn_conv', 'new_v_dn_a_log', 'new_v_dn_dt_bias', 'new_v_dn_norm', 'new_v_q_norm', 'new_v_k_norm', 'new_v_attn_sink', 'new_v_w_branch_dn', 'new_v_w_branch_attn', 'new_v_w_out', 'new_v_ffn_up', 'new_v_ffn_conv', 'new_v_ffn_conv_b', 'new_v_ffn_down']
TWIN_LEAF_KINDS = {'loss': 'loss', 'grad_x': 'grad_x', 'grad_c_ctx': 'grad_w', 'grad_w_ada': 'grad_w', 'grad_b_ada': 'grad_w', 'grad_norm_mix': 'grad_w', 'grad_norm_ffn': 'grad_w', 'grad_w_in': 'grad_w', 'grad_dn_conv': 'grad_w', 'grad_dn_a_log': 'grad_w', 'grad_dn_dt_bias': 'grad_w', 'grad_dn_norm': 'grad_w', 'grad_q_norm': 'grad_w', 'grad_k_norm': 'grad_w', 'grad_attn_sink': 'grad_w', 'grad_w_branch_dn': 'grad_w', 'grad_w_branch_attn': 'grad_w', 'grad_w_out': 'grad_w', 'grad_ffn_up': 'grad_w', 'grad_ffn_conv': 'grad_w', 'grad_ffn_conv_b': 'grad_w', 'grad_ffn_down': 'grad_w', 'delta_c_ctx': 'delta_w', 'delta_w_ada': 'delta_w', 'delta_b_ada': 'delta_w', 'delta_norm_mix': 'delta_w', 'delta_norm_ffn': 'delta_w', 'delta_w_in': 'delta_w', 'delta_dn_conv': 'delta_w', 'delta_dn_a_log': 'delta_w', 'delta_dn_dt_bias': 'delta_w', 'delta_dn_norm': 'delta_w', 'delta_q_norm': 'delta_w', 'delta_k_norm': 'delta_w', 'delta_attn_sink': 'delta_w', 'delta_w_branch_dn': 'delta_w', 'delta_w_branch_attn': 'delta_w', 'delta_w_out': 'delta_w', 'delta_ffn_up': 'delta_w', 'delta_ffn_conv': 'delta_w', 'delta_ffn_conv_b': 'delta_w', 'delta_ffn_down': 'delta_w', 'new_m_c_ctx': 'new_m', 'new_m_w_ada': 'new_m', 'new_m_b_ada': 'new_m', 'new_m_norm_mix': 'new_m', 'new_m_norm_ffn': 'new_m', 'new_m_w_in': 'new_m', 'new_m_dn_conv': 'new_m', 'new_m_dn_a_log': 'new_m', 'new_m_dn_dt_bias': 'new_m', 'new_m_dn_norm': 'new_m', 'new_m_q_norm': 'new_m', 'new_m_k_norm': 'new_m', 'new_m_attn_sink': 'new_m', 'new_m_w_branch_dn': 'new_m', 'new_m_w_branch_attn': 'new_m', 'new_m_w_out': 'new_m', 'new_m_ffn_up': 'new_m', 'new_m_ffn_conv': 'new_m', 'new_m_ffn_conv_b': 'new_m', 'new_m_ffn_down': 'new_m', 'new_v_c_ctx': 'new_v', 'new_v_w_ada': 'new_v', 'new_v_b_ada': 'new_v', 'new_v_norm_mix': 'new_v', 'new_v_norm_ffn': 'new_v', 'new_v_w_in': 'new_v', 'new_v_dn_conv': 'new_v', 'new_v_dn_a_log': 'new_v', 'new_v_dn_dt_bias': 'new_v', 'new_v_dn_norm': 'new_v', 'new_v_q_norm': 'new_v', 'new_v_k_norm': 'new_v', 'new_v_attn_sink': 'new_v', 'new_v_w_branch_dn': 'new_v', 'new_v_w_branch_attn': 'new_v', 'new_v_w_out': 'new_v', 'new_v_ffn_up': 'new_v', 'new_v_ffn_conv': 'new_v', 'new_v_ffn_conv_b': 'new_v', 'new_v_ffn_down': 'new_v'}


def _forward(args):
    return _fwd_reference(*[args[k] for k in FWD_PARAMS])


def _output_shape():
    def fwd():
        inp = _fwd_setup_inputs(0)
        return _fwd_reference(*[inp[k] for k in FWD_PARAMS])
    out = _jax.eval_shape(fwd)
    return out.shape, out.dtype

N_MICROBATCH = 1
ADAM_LR = 0.001
ADAM_B1 = 0.9
ADAM_B2 = 0.999
ADAM_EPS = 1e-08
ADAM_WD = 0.01
ADAM_STEP = 10
PER_EXAMPLE_BATCH_AXIS = {'x': 0, 'c': 0, 'ctx': 0, 'loss_target': 0}
SHARED_INPUTS = []
_WEIGHT_DTYPES = {'c_ctx': _jnp.float32, 'w_ada': _jnp.float32, 'b_ada': _jnp.float32, 'norm_mix': _jnp.float32, 'norm_ffn': _jnp.float32, 'w_in': _jnp.float32, 'dn_conv': _jnp.float32, 'dn_a_log': _jnp.float32, 'dn_dt_bias': _jnp.float32, 'dn_norm': _jnp.float32, 'q_norm': _jnp.float32, 'k_norm': _jnp.float32, 'attn_sink': _jnp.float32, 'w_branch_dn': _jnp.float32, 'w_branch_attn': _jnp.float32, 'w_out': _jnp.float32, 'ffn_up': _jnp.float32, 'ffn_conv': _jnp.float32, 'ffn_conv_b': _jnp.float32, 'ffn_down': _jnp.float32}
MOMENT_SCALE = {'c_ctx': 2.696453e-01, 'w_ada': 2.410211e+00, 'b_ada': 6.792964e+00, 'norm_mix': 1.956416e+00, 'norm_ffn': 1.272890e+01, 'w_in': 1.286819e-01, 'dn_conv': 1.629476e-01, 'dn_a_log': 4.529947e-01, 'dn_dt_bias': 4.390845e-01, 'dn_norm': 1.357809e+01, 'q_norm': 1.814609e-01, 'k_norm': 1.671758e-01, 'attn_sink': 2.879484e-02, 'w_branch_dn': 1.821823e-01, 'w_branch_attn': 2.080598e-01, 'w_out': 2.044456e-01, 'ffn_up': 2.596694e-01, 'ffn_conv': 1.903285e+00, 'ffn_conv_b': 1.551285e+00, 'ffn_down': 1.750748e-01}


def _to_microbatches(a, axis):
    t = _jnp.moveaxis(a, axis, 0)
    t = t.reshape((N_MICROBATCH, t.shape[0] // N_MICROBATCH) + t.shape[1:])
    return _jnp.moveaxis(t, 1, axis + 1)


def setup_inputs(seed: int = 0) -> dict:
    inp = _fwd_setup_inputs(seed)
    key = _jax.random.fold_in(_jax.random.key(seed), 7919)
    shape, _ = _output_shape()
    out = dict(inp)
    out["loss_target"] = _jax.random.normal(_jax.random.fold_in(key, 0), shape, _jnp.float32)
    for i, name in enumerate(TWIN_WEIGHTS):
        w = inp[name].astype(_jnp.float32)
        if MOMENT_SCALE is None:
            s = _jnp.sqrt(_jnp.mean(_jnp.square(w)) + 1e-30)
        else:
            s = MOMENT_SCALE[name]
        km, kv = _jax.random.split(_jax.random.fold_in(key, i + 1))
        out[name] = w
        out["m_" + name] = s * _jax.random.normal(km, w.shape, _jnp.float32)
        out["v_" + name] = (s * s) * _jax.random.uniform(kv, w.shape, _jnp.float32, 0.5, 1.5)
    if N_MICROBATCH > 1:
        for name, axis in PER_EXAMPLE_BATCH_AXIS.items():
            out[name] = _to_microbatches(out[name], axis)
    return {'x': out['x'], 'c': out['c'], 'ctx': out['ctx'], 'c_ctx': out['c_ctx'], 'w_ada': out['w_ada'], 'b_ada': out['b_ada'], 'norm_mix': out['norm_mix'], 'norm_ffn': out['norm_ffn'], 'w_in': out['w_in'], 'dn_conv': out['dn_conv'], 'dn_a_log': out['dn_a_log'], 'dn_dt_bias': out['dn_dt_bias'], 'dn_norm': out['dn_norm'], 'q_norm': out['q_norm'], 'k_norm': out['k_norm'], 'attn_sink': out['attn_sink'], 'w_branch_dn': out['w_branch_dn'], 'w_branch_attn': out['w_branch_attn'], 'w_out': out['w_out'], 'ffn_up': out['ffn_up'], 'ffn_conv': out['ffn_conv'], 'ffn_conv_b': out['ffn_conv_b'], 'ffn_down': out['ffn_down'], 'loss_target': out['loss_target'], 'm_c_ctx': out['m_c_ctx'], 'm_w_ada': out['m_w_ada'], 'm_b_ada': out['m_b_ada'], 'm_norm_mix': out['m_norm_mix'], 'm_norm_ffn': out['m_norm_ffn'], 'm_w_in': out['m_w_in'], 'm_dn_conv': out['m_dn_conv'], 'm_dn_a_log': out['m_dn_a_log'], 'm_dn_dt_bias': out['m_dn_dt_bias'], 'm_dn_norm': out['m_dn_norm'], 'm_q_norm': out['m_q_norm'], 'm_k_norm': out['m_k_norm'], 'm_attn_sink': out['m_attn_sink'], 'm_w_branch_dn': out['m_w_branch_dn'], 'm_w_branch_attn': out['m_w_branch_attn'], 'm_w_out': out['m_w_out'], 'm_ffn_up': out['m_ffn_up'], 'm_ffn_conv': out['m_ffn_conv'], 'm_ffn_conv_b': out['m_ffn_conv_b'], 'm_ffn_down': out['m_ffn_down'], 'v_c_ctx': out['v_c_ctx'], 'v_w_ada': out['v_w_ada'], 'v_b_ada': out['v_b_ada'], 'v_norm_mix': out['v_norm_mix'], 'v_norm_ffn': out['v_norm_ffn'], 'v_w_in': out['v_w_in'], 'v_dn_conv': out['v_dn_conv'], 'v_dn_a_log': out['v_dn_a_log'], 'v_dn_dt_bias': out['v_dn_dt_bias'], 'v_dn_norm': out['v_dn_norm'], 'v_q_norm': out['v_q_norm'], 'v_k_norm': out['v_k_norm'], 'v_attn_sink': out['v_attn_sink'], 'v_w_branch_dn': out['v_w_branch_dn'], 'v_w_branch_attn': out['v_w_branch_attn'], 'v_w_out': out['v_w_out'], 'v_ffn_up': out['v_ffn_up'], 'v_ffn_conv': out['v_ffn_conv'], 'v_ffn_conv_b': out['v_ffn_conv_b'], 'v_ffn_down': out['v_ffn_down']}


def _loss(weights, diff, rest, loss_target):
    with _jax.named_scope("forward"):
        args = {**rest, TWIN_DIFF_INPUT: diff, **{k: w.astype(_WEIGHT_DTYPES[k]) for k, w in weights.items()}}
        y = _forward(args)
    with _jax.named_scope("loss_head"):
        err = _jnp.square(y.astype(_jnp.float32) - loss_target)
        return 0.5 * _jnp.sum(_jnp.mean(err, axis=-1)) if err.ndim else 0.5 * err


def _adamw(w, g, m, v):
    m = ADAM_B1 * m + (1.0 - ADAM_B1) * g
    v = ADAM_B2 * v + (1.0 - ADAM_B2) * _jnp.square(g)
    m_hat = m / (1.0 - ADAM_B1 ** ADAM_STEP)
    v_hat = v / (1.0 - ADAM_B2 ** ADAM_STEP)
    delta = -ADAM_LR * (m_hat / (_jnp.sqrt(v_hat) + ADAM_EPS) + ADAM_WD * w)
    return delta, m, v


def reference(x, c, ctx, c_ctx, w_ada, b_ada, norm_mix, norm_ffn, w_in, dn_conv, dn_a_log, dn_dt_bias, dn_norm, q_norm, k_norm, attn_sink, w_branch_dn, w_branch_attn, w_out, ffn_up, ffn_conv, ffn_conv_b, ffn_down, loss_target, m_c_ctx, m_w_ada, m_b_ada, m_norm_mix, m_norm_ffn, m_w_in, m_dn_conv, m_dn_a_log, m_dn_dt_bias, m_dn_norm, m_q_norm, m_k_norm, m_attn_sink, m_w_branch_dn, m_w_branch_attn, m_w_out, m_ffn_up, m_ffn_conv, m_ffn_conv_b, m_ffn_down, v_c_ctx, v_w_ada, v_b_ada, v_norm_mix, v_norm_ffn, v_w_in, v_dn_conv, v_dn_a_log, v_dn_dt_bias, v_dn_norm, v_q_norm, v_k_norm, v_attn_sink, v_w_branch_dn, v_w_branch_attn, v_w_out, v_ffn_up, v_ffn_conv, v_ffn_conv_b, v_ffn_down):
    given = dict(x=x, c=c, ctx=ctx, c_ctx=c_ctx, w_ada=w_ada, b_ada=b_ada, norm_mix=norm_mix, norm_ffn=norm_ffn, w_in=w_in, dn_conv=dn_conv, dn_a_log=dn_a_log, dn_dt_bias=dn_dt_bias, dn_norm=dn_norm, q_norm=q_norm, k_norm=k_norm, attn_sink=attn_sink, w_branch_dn=w_branch_dn, w_branch_attn=w_branch_attn, w_out=w_out, ffn_up=ffn_up, ffn_conv=ffn_conv, ffn_conv_b=ffn_conv_b, ffn_down=ffn_down, loss_target=loss_target, m_c_ctx=m_c_ctx, m_w_ada=m_w_ada, m_b_ada=m_b_ada, m_norm_mix=m_norm_mix, m_norm_ffn=m_norm_ffn, m_w_in=m_w_in, m_dn_conv=m_dn_conv, m_dn_a_log=m_dn_a_log, m_dn_dt_bias=m_dn_dt_bias, m_dn_norm=m_dn_norm, m_q_norm=m_q_norm, m_k_norm=m_k_norm, m_attn_sink=m_attn_sink, m_w_branch_dn=m_w_branch_dn, m_w_branch_attn=m_w_branch_attn, m_w_out=m_w_out, m_ffn_up=m_ffn_up, m_ffn_conv=m_ffn_conv, m_ffn_conv_b=m_ffn_conv_b, m_ffn_down=m_ffn_down, v_c_ctx=v_c_ctx, v_w_ada=v_w_ada, v_b_ada=v_b_ada, v_norm_mix=v_norm_mix, v_norm_ffn=v_norm_ffn, v_w_in=v_w_in, v_dn_conv=v_dn_conv, v_dn_a_log=v_dn_a_log, v_dn_dt_bias=v_dn_dt_bias, v_dn_norm=v_dn_norm, v_q_norm=v_q_norm, v_k_norm=v_k_norm, v_attn_sink=v_attn_sink, v_w_branch_dn=v_w_branch_dn, v_w_branch_attn=v_w_branch_attn, v_w_out=v_w_out, v_ffn_up=v_ffn_up, v_ffn_conv=v_ffn_conv, v_ffn_conv_b=v_ffn_conv_b, v_ffn_down=v_ffn_down)
    weights = {n: given[n] for n in TWIN_WEIGHTS}
    shared = {n: given[n] for n in SHARED_INPUTS}
    per_example = {n: given[n] for n in ['x', 'c', 'ctx']}
    grad_fn = _jax.value_and_grad(_loss, argnums=(0, 1))

    def one_microbatch(ex, loss_target):
        ex = dict(ex)
        diff = ex.pop(TWIN_DIFF_INPUT)
        return grad_fn(weights, diff, {**shared, **ex}, loss_target)

    if N_MICROBATCH == 1:
        loss, (grad_w, grad_x) = one_microbatch(per_example, given["loss_target"])
    else:
        def body(carry, xs):
            loss_sum, grad_sum = carry
            l_k, (gw_k, gx_k) = one_microbatch(xs[0], xs[1])
            with _jax.named_scope("update"):
                return (loss_sum + l_k, _jax.tree.map(_jnp.add, grad_sum, gw_k)), gx_k

        init = (_jnp.zeros((), _jnp.float32), _jax.tree.map(_jnp.zeros_like, weights))
        (loss, grad_w), grad_x = _jax.lax.scan(body, init, (per_example, given["loss_target"]))
    with _jax.named_scope("update"):
        delta_w, new_m, new_v = {}, {}, {}
        for n in TWIN_WEIGHTS:
            delta_w[n], new_m[n], new_v[n] = _adamw(weights[n], grad_w[n], given["m_" + n], given["v_" + n])
    return (loss, grad_x, *[grad_w[n] for n in TWIN_WEIGHTS], *[delta_w[n] for n in TWIN_WEIGHTS],
            *[new_m[n] for n in TWIN_WEIGHTS], *[new_v[n] for n in TWIN_WEIGHTS])
```

```python
import functools
import math

import numpy as np
import jax
import jax.numpy as jnp
from jax import lax
from jax.experimental import pallas as pl
from jax.experimental.pallas import tpu as pltpu

F32 = jnp.float32
BF16 = jnp.bfloat16
HI = lax.Precision.HIGHEST

D = 1024
NH = 8
HD = 128
CH = 64
CTX = 256
AB = 128
KVH = 2
GRP = 4
DFF = 2816
EPS = 1e-6
GRID_W = 64
ROPE_BASE = 10000.0
N_DEV = 8
VMEM_LIMIT = 56 * 1024 * 1024

C_QKV, C_GT, C_QAT, C_MG, C_KAT, C_VAT, C_BA, C_PAD = 0, 3072, 4096, 5120, 7168, 7424, 7680, 7808
PW = 8192


def _cparams(sem=None, **kw):
    return pltpu.CompilerParams(dimension_semantics=sem, vmem_limit_bytes=VMEM_LIMIT, **kw)


def _dot(a, b, dims, hi):
    if hi:
        return lax.dot_general(a.astype(F32), b.astype(F32), (dims, ((), ())), precision=HI, preferred_element_type=F32)
    return lax.dot_general(a.astype(BF16), b.astype(BF16), (dims, ((), ())), preferred_element_type=F32)


NN = ((1,), (0,))
NT = ((1,), (1,))
TN = ((0,), (0,))


def _dn_masks():
    i = np.arange(CH)
    lo_incl = (i[:, None] >= i[None, :]).astype(np.float32)
    lo_strict = (i[:, None] > i[None, :]).astype(np.float32)
    return jnp.asarray(np.stack([np.stack([lo_incl, lo_strict]), np.stack([lo_incl.T, lo_strict.T])]))


def _dn_chunk_index(d, i, n_ctx_chunks, n_chunks):
    fwd = i
    bwd = jnp.where(i < n_ctx_chunks, n_ctx_chunks - 1 - i, n_chunks - 1 + n_ctx_chunks - i)
    return jnp.where(d == 0, fwd, bwd)


def _dn_chunk_common(q, k, v, beta, gc, gcr, gtot, s, mi, ms, hi):
    diff = gc - gcr
    decay = jnp.exp(jnp.where(mi > 0, diff, 0.0)) * mi
    e = jnp.exp(gc)
    e_last = jnp.exp(gtot)
    kdfac = jnp.exp(gtot - gc)
    kk = _dot(k, k, NT, hi)
    a = ms * (beta * kk * decay)
    x = -a
    eye = (lax.broadcasted_iota(jnp.int32, (CH, CH), 0) == lax.broadcasted_iota(jnp.int32, (CH, CH), 1)).astype(F32)
    t = eye + x
    p = x
    for _ in range(5):
        p = _dot(p, p, NN, True)
        t = t + _dot(t, p, NN, True)
    vb = beta * v
    kb = (beta * e) * k
    uw = _dot(t, jnp.concatenate([vb, kb], axis=1), NN, hi)
    u, w = uw[:, :HD], uw[:, HD:]
    vn = u - _dot(w, s, NN, hi)
    qk = _dot(q, k, NT, hi)
    pm = qk * decay
    qg = e * q
    kd = kdfac * k
    return dict(decay=decay, e=e, e_last=e_last, kdfac=kdfac, kk=kk, a=a, t=t, vb=vb, kb=kb, u=u, w=w, vn=vn,
                qk=qk, pm=pm, qg=qg, kd=kd)


def _dn_gcum(gb, mi):
    gcum = _dot(mi, gb, NN, True)
    gtot = jnp.sum(gb, axis=0, keepdims=True)
    return gcum, gcum.T, gtot


def _dn_fwd(q, k, v, gb, n_ctx_chunks, hi):
    t_all = q.shape[0]
    n_chunks = t_all // CH
    masks = _dn_masks()

    def body(q_ref, k_ref, v_ref, gb_ref, m_ref, o_ref, sh_ref, s_scr):
        i = pl.program_id(1)

        @pl.when(i == 0)
        def _():
            s_scr[...] = jnp.zeros_like(s_scr)

        mi = m_ref[0, 0]
        ms = m_ref[0, 1]
        gb_blk = gb_ref[0]
        gcum, gcum_t, gtot = _dn_gcum(gb_blk, mi)
        for h in range(NH):
            sl = slice(h * HD, (h + 1) * HD)
            s = s_scr[h]
            sh_ref[0, 0, h] = s
            c = _dn_chunk_common(q_ref[:, sl], k_ref[:, sl], v_ref[:, sl], gb_blk[:, h:h + 1],
                                 gcum[:, NH + h:NH + h + 1], gcum_t[NH + h:NH + h + 1, :], gtot[:, NH + h:NH + h + 1],
                                 s, mi, ms, hi)
            o_ref[0, :, sl] = _dot(c["qg"], s, NN, hi) + _dot(c["pm"], c["vn"], NN, hi)
            s_scr[h] = c["e_last"] * s + _dot(c["kd"], c["vn"], TN, hi)

    cidx = functools.partial(_dn_chunk_index, n_ctx_chunks=n_ctx_chunks, n_chunks=n_chunks)
    tok = pl.BlockSpec((CH, D), lambda d, i: (cidx(d, i), 0))
    return pl.pallas_call(
        body, name="dn_fwd",
        grid=(2, n_chunks),
        in_specs=[tok, tok, tok,
                  pl.BlockSpec((1, CH, 128), lambda d, i: (d, cidx(d, i), 0)),
                  pl.BlockSpec((1, 2, CH, CH), lambda d, i: (d, 0, 0, 0))],
        out_specs=[pl.BlockSpec((1, CH, D), lambda d, i: (d, cidx(d, i), 0)),
                   pl.BlockSpec((1, 1, NH, HD, HD), lambda d, i: (d, cidx(d, i), 0, 0, 0))],
        out_shape=[jax.ShapeDtypeStruct((2, t_all, D), F32),
                   jax.ShapeDtypeStruct((2, n_chunks, NH, HD, HD), F32)],
        scratch_shapes=[pltpu.VMEM((NH, HD, HD), F32)],
        compiler_params=_cparams(("parallel", "arbitrary")),
    )(q, k, v, gb, masks)


def _dn_bwd(q, k, v, gb, s_hist, do, n_ctx_chunks, hi):
    t_all = q.shape[0]
    n_chunks = t_all // CH
    masks = _dn_masks()

    def body(q_ref, k_ref, v_ref, gb_ref, m_ref, sh_ref, do_ref, dq_ref, dk_ref, dv_ref, dgb_ref, ds_scr):
        i = pl.program_id(1)

        @pl.when(i == 0)
        def _():
            ds_scr[...] = jnp.zeros_like(ds_scr)

        mi = m_ref[0, 0]
        ms = m_ref[0, 1]
        gb_blk = gb_ref[0]
        gcum, gcum_t, gtot = _dn_gcum(gb_blk, mi)
        lane = lax.broadcasted_iota(jnp.int32, (1, 128), 1)
        ones = jnp.ones((CH, 128), F32)
        dbeta_all = jnp.zeros((CH, 128), F32)
        dgc_all = jnp.zeros((CH, 128), F32)
        dgtot_all = jnp.zeros((1, 128), F32)
        for h in range(NH):
            sl = slice(h * HD, (h + 1) * HD)
            q_, k_, v_ = q_ref[:, sl], k_ref[:, sl], v_ref[:, sl]
            beta = gb_blk[:, h:h + 1]
            s = sh_ref[0, 0, h]
            dsn = ds_scr[h]
            do_ = do_ref[:, sl]
            c = _dn_chunk_common(q_, k_, v_, beta, gcum[:, NH + h:NH + h + 1], gcum_t[NH + h:NH + h + 1, :],
                                 gtot[:, NH + h:NH + h + 1], s, mi, ms, hi)
            decay, e, e_last, kdfac = c["decay"], c["e"], c["e_last"], c["kdfac"]
            dvn = _dot(c["pm"], do_, TN, hi) + _dot(c["kd"], dsn, NN, hi)
            dpm = _dot(do_, c["vn"], NT, hi)
            dqg = _dot(do_, s, NT, hi)
            dkd = _dot(c["vn"], dsn, NT, hi)
            ds_scr[h] = _dot(c["qg"], do_, TN, hi) + e_last * dsn - _dot(c["w"], dvn, TN, hi)
            de_last = jnp.sum(jnp.sum(s * dsn, axis=1, keepdims=True), axis=0, keepdims=True)
            dw = -_dot(dvn, s, NT, hi)
            dvbkb = _dot(c["t"], jnp.concatenate([dvn, dw], axis=1), TN, hi)
            dvb, dkb = dvbkb[:, :HD], dvbkb[:, HD:]
            da = -ms * _dot(dvbkb, jnp.concatenate([c["u"], c["w"]], axis=1), NT, hi)
            dqk = dpm * decay
            gm = dpm * c["pm"] + da * c["a"]
            dgc = jnp.sum(gm, axis=1, keepdims=True) - _dot(gm, ones, TN, True)[:, 0:1]
            dkk = da * (beta * decay)
            dbeta = jnp.sum(da * c["kk"] * decay, axis=1, keepdims=True)
            dk_ = _dot(dkk, k_, NN, hi) + _dot(dkk, k_, TN, hi) + _dot(dqk, q_, TN, hi)
            dq_ = _dot(dqk, k_, NN, hi) + e * dqg
            de = jnp.sum(dqg * q_, axis=1, keepdims=True)
            dv_ref[0, :, sl] = beta * dvb
            dbeta = dbeta + jnp.sum(dvb * v_, axis=1, keepdims=True)
            skb = jnp.sum(dkb * k_, axis=1, keepdims=True)
            dk_ = dk_ + (beta * e) * dkb + kdfac * dkd
            dbeta = dbeta + e * skb
            de = de + beta * skb
            skd = jnp.sum(dkd * c["kd"], axis=1, keepdims=True)
            dgc = dgc - skd + de * e
            dgtot = jnp.sum(skd, axis=0, keepdims=True) + de_last * e_last
            dq_ref[0, :, sl] = dq_
            dk_ref[0, :, sl] = dk_
            hot_b = (lane == h).astype(F32)
            hot_g = (lane == NH + h).astype(F32)
            dbeta_all = dbeta_all + dbeta * hot_b
            dgc_all = dgc_all + dgc * hot_g
            dgtot_all = dgtot_all + dgtot * hot_g
        dgb_ref[0] = dbeta_all + _dot(mi, dgc_all, TN, True) + dgtot_all

    def cidx(d, i):
        return _dn_chunk_index(d, n_chunks - 1 - i, n_ctx_chunks, n_chunks)

    tok = pl.BlockSpec((CH, D), lambda d, i: (cidx(d, i), 0))
    tok_d = pl.BlockSpec((1, CH, D), lambda d, i: (d, cidx(d, i), 0))
    gbs = pl.BlockSpec((1, CH, 128), lambda d, i: (d, cidx(d, i), 0))
    return pl.pallas_call(
        body, name="dn_bwd",
        grid=(2, n_chunks),
        in_specs=[tok, tok, tok, gbs,
                  pl.BlockSpec((1, 2, CH, CH), lambda d, i: (d, 0, 0, 0)),
                  pl.BlockSpec((1, 1, NH, HD, HD), lambda d, i: (d, cidx(d, i), 0, 0, 0)),
                  tok],
        out_specs=[tok_d, tok_d, tok_d, gbs],
        out_shape=[jax.ShapeDtypeStruct((2, t_all, D), F32)] * 3 + [jax.ShapeDtypeStruct((2, t_all, 128), F32)],
        scratch_shapes=[pltpu.VMEM((NH, HD, HD), F32)],
        compiler_params=_cparams(("parallel", "arbitrary")),
    )(q, k, v, gb, masks, s_hist, do)


ATT_SCALE = HD ** -0.5
NEG = -1e30


def _att_stack(ref, kvh):
    return jnp.concatenate([ref[:, (kvh * GRP + g) * HD:(kvh * GRP + g + 1) * HD] for g in range(GRP)], axis=0)


def _att_col(ref, kvh):
    return jnp.concatenate([ref[:, kvh * GRP + g:kvh * GRP + g + 1] for g in range(GRP)], axis=0)


def _att_sink(sink_ref, kvh):
    return jnp.concatenate([jnp.broadcast_to(sink_ref[:, kvh * GRP + g:kvh * GRP + g + 1], (AB, 1)) for g in range(GRP)],
                           axis=0)


def _att_mask(i, nb):
    r = lax.broadcasted_iota(jnp.int32, (AB, AB), 0)
    c = lax.broadcasted_iota(jnp.int32, (AB, AB), 1)
    okp = jnp.logical_and(c >= r, i > 0)
    okn = jnp.logical_and(c <= r, i < nb - 1)
    m = jnp.concatenate([okp, jnp.ones((AB, AB), jnp.bool_), okn, jnp.ones((AB, CTX), jnp.bool_)], axis=1)
    return jnp.concatenate([m] * GRP, axis=0)


def _att_kspecs(nb):
    nc = CTX // AB
    return [pl.BlockSpec((AB, KVH * HD), lambda i: (jnp.maximum(i - 1, 0) + nc, 0)),
            pl.BlockSpec((AB, KVH * HD), lambda i: (i + nc, 0)),
            pl.BlockSpec((AB, KVH * HD), lambda i: (jnp.minimum(i + 1, nb - 1) + nc, 0)),
            pl.BlockSpec((CTX, KVH * HD), lambda i: (0, 0))]


def _attn_fwd(qr, kr, vv, sink, hi):
    tl = qr.shape[0]
    nb = tl // AB

    def body(q_ref, kp_ref, kc_ref, kn_ref, kx_ref, vp_ref, vc_ref, vn_ref, vx_ref, sink_ref, o_ref, lse_ref):
        i = pl.program_id(0)
        mask = _att_mask(i, nb)
        lane = lax.broadcasted_iota(jnp.int32, (1, 128), 1)
        lse_all = jnp.zeros((AB, 128), F32)
        for kvh in range(KVH):
            ksl = slice(kvh * HD, (kvh + 1) * HD)
            kall = jnp.concatenate([kp_ref[:, ksl], kc_ref[:, ksl], kn_ref[:, ksl], kx_ref[:, ksl]], axis=0)
            vall = jnp.concatenate([vp_ref[:, ksl], vc_ref[:, ksl], vn_ref[:, ksl], vx_ref[:, ksl]], axis=0)
            s = _dot(_att_stack(q_ref, kvh), kall, NT, hi) * ATT_SCALE
            s = jnp.where(mask, s, NEG)
            sk = _att_sink(sink_ref, kvh)
            m = jnp.maximum(jnp.max(s, axis=1, keepdims=True), sk)
            p = jnp.exp(s - m)
            l = jnp.sum(p, axis=1, keepdims=True) + jnp.exp(sk - m)
            o = _dot(p, vall, NN, hi) / l
            lse = m + jnp.log(l)
            for g in range(GRP):
                h = kvh * GRP + g
                o_ref[:, h * HD:(h + 1) * HD] = o[g * AB:(g + 1) * AB]
                lse_all = lse_all + lse[g * AB:(g + 1) * AB] * (lane == h).astype(F32)
        lse_ref[...] = lse_all

    ks = _att_kspecs(nb)
    return pl.pallas_call(
        body, name="attn_fwd", grid=(nb,),
        in_specs=[pl.BlockSpec((AB, D), lambda i: (i, 0))] + ks + ks + [pl.BlockSpec((1, 128), lambda i: (0, 0))],
        out_specs=[pl.BlockSpec((AB, D), lambda i: (i, 0)), pl.BlockSpec((AB, 128), lambda i: (i, 0))],
        out_shape=[jax.ShapeDtypeStruct((tl, D), F32), jax.ShapeDtypeStruct((tl, 128), F32)],
        compiler_params=_cparams(("parallel",)),
    )(qr, kr, kr, kr, kr, vv, vv, vv, vv, sink)


def _attn_delta(o, do):
    tl = o.shape[0]
    tr = min(512, tl)

    def body(o_ref, do_ref, d_ref):
        lane = lax.broadcasted_iota(jnp.int32, (1, 128), 1)
        acc = jnp.zeros((tr, 128), F32)
        for h in range(NH):
            sl = slice(h * HD, (h + 1) * HD)
            acc = acc + jnp.sum(o_ref[:, sl] * do_ref[:, sl], axis=1, keepdims=True) * (lane == h).astype(F32)
        d_ref[...] = acc

    return pl.pallas_call(
        body, name="attn_delta", grid=(tl // tr,),
        in_specs=[pl.BlockSpec((tr, D), lambda i: (i, 0))] * 2,
        out_specs=pl.BlockSpec((tr, 128), lambda i: (i, 0)),
        out_shape=jax.ShapeDtypeStruct((tl, 128), F32),
        compiler_params=_cparams(("parallel",)),
    )(o, do)


def _attn_bwd_q(qr, kr, vv, sink, do, lse, delta, hi):
    tl = qr.shape[0]
    nb = tl // AB

    def body(q_ref, kp_ref, kc_ref, kn_ref, kx_ref, vp_ref, vc_ref, vn_ref, vx_ref, sink_ref, do_ref, lse_ref, dl_ref,
             dq_ref, dkx_ref, dvx_ref, dsink_ref):
        i = pl.program_id(0)

        @pl.when(i == 0)
        def _():
            dkx_ref[...] = jnp.zeros_like(dkx_ref)
            dvx_ref[...] = jnp.zeros_like(dvx_ref)
            dsink_ref[...] = jnp.zeros_like(dsink_ref)

        mask = _att_mask(i, nb)
        lane = lax.broadcasted_iota(jnp.int32, (1, 128), 1)
        dsink = jnp.zeros((1, 128), F32)
        for kvh in range(KVH):
            ksl = slice(kvh * HD, (kvh + 1) * HD)
            kall = jnp.concatenate([kp_ref[:, ksl], kc_ref[:, ksl], kn_ref[:, ksl], kx_ref[:, ksl]], axis=0)
            vall = jnp.concatenate([vp_ref[:, ksl], vc_ref[:, ksl], vn_ref[:, ksl], vx_ref[:, ksl]], axis=0)
            qs = _att_stack(q_ref, kvh)
            dos = _att_stack(do_ref, kvh)
            lse_s = _att_col(lse_ref, kvh)
            dl_s = _att_col(dl_ref, kvh)
            s = _dot(qs, kall, NT, hi) * ATT_SCALE
            p = jnp.where(mask, jnp.exp(jnp.where(mask, s, NEG) - lse_s), 0.0)
            dp = _dot(dos, vall, NT, hi)
            ds = p * (dp - dl_s)
            dq = _dot(ds, kall, NN, hi) * ATT_SCALE
            dkx_ref[:, ksl] += _dot(ds[:, 3 * AB:], qs, TN, hi) * ATT_SCALE
            dvx_ref[:, ksl] += _dot(p[:, 3 * AB:], dos, TN, hi)
            psink = jnp.exp(_att_sink(sink_ref, kvh) - lse_s) * dl_s
            for g in range(GRP):
                h = kvh * GRP + g
                dq_ref[:, h * HD:(h + 1) * HD] = dq[g * AB:(g + 1) * AB]
                dsink = dsink - jnp.sum(psink[g * AB:(g + 1) * AB], axis=0, keepdims=True) * (lane == h).astype(F32)
        dsink_ref[...] += dsink

    ks = _att_kspecs(nb)
    row = pl.BlockSpec((AB, D), lambda i: (i, 0))
    col = pl.BlockSpec((AB, 128), lambda i: (i, 0))
    return pl.pallas_call(
        body, name="attn_bwd_q", grid=(nb,),
        in_specs=[row] + ks + ks + [pl.BlockSpec((1, 128), lambda i: (0, 0)), row, col, col],
        out_specs=[row, pl.BlockSpec((CTX, KVH * HD), lambda i: (0, 0)), pl.BlockSpec((CTX, KVH * HD), lambda i: (0, 0)),
                   pl.BlockSpec((1, 128), lambda i: (0, 0))],
        out_shape=[jax.ShapeDtypeStruct((tl, D), F32), jax.ShapeDtypeStruct((CTX, KVH * HD), F32),
                   jax.ShapeDtypeStruct((CTX, KVH * HD), F32), jax.ShapeDtypeStruct((1, 128), F32)],
        compiler_params=_cparams(("arbitrary",)),
    )(qr, kr, kr, kr, kr, vv, vv, vv, vv, sink, do, lse, delta)


def _attn_bwd_kv(qr, kr, vv, do, lse, delta, hi):
    tl = qr.shape[0]
    nb = tl // AB
    nc = CTX // AB

    def body(k_ref, v_ref, *refs):
        qs_refs, do_refs, lse_refs, dl_refs = refs[0:3], refs[3:6], refs[6:9], refs[9:12]
        dk_ref, dv_ref = refs[12], refs[13]
        j = pl.program_id(0)
        r = lax.broadcasted_iota(jnp.int32, (AB, AB), 0)
        c = lax.broadcasted_iota(jnp.int32, (AB, AB), 1)
        one = jnp.ones((AB, AB), jnp.bool_)
        masks = [jnp.logical_and(c <= r, j > 0), one, jnp.logical_and(c >= r, j < nb - 1)]
        for kvh in range(KVH):
            ksl = slice(kvh * HD, (kvh + 1) * HD)
            k_, v_ = k_ref[:, ksl], v_ref[:, ksl]
            dk = jnp.zeros((AB, HD), F32)
            dv = jnp.zeros((AB, HD), F32)
            for t in range(3):
                mask = jnp.concatenate([masks[t]] * GRP, axis=0)
                qs = _att_stack(qs_refs[t], kvh)
                dos = _att_stack(do_refs[t], kvh)
                lse_s = _att_col(lse_refs[t], kvh)
                dl_s = _att_col(dl_refs[t], kvh)
                s = _dot(qs, k_, NT, hi) * ATT_SCALE
                p = jnp.where(mask, jnp.exp(jnp.where(mask, s, NEG) - lse_s), 0.0)
                dp = _dot(dos, v_, NT, hi)
                ds = p * (dp - dl_s)
                dv = dv + _dot(p, dos, TN, hi)
                dk = dk + _dot(ds, qs, TN, hi) * ATT_SCALE
            dk_ref[:, ksl] = dk
            dv_ref[:, ksl] = dv

    def three(width):
        return [pl.BlockSpec((AB, width), lambda j: (jnp.maximum(j - 1, 0), 0)),
                pl.BlockSpec((AB, width), lambda j: (j, 0)),
                pl.BlockSpec((AB, width), lambda j: (jnp.minimum(j + 1, nb - 1), 0))]

    kv = pl.BlockSpec((AB, KVH * HD), lambda j: (j + nc, 0))
    out = pl.BlockSpec((AB, KVH * HD), lambda j: (j, 0))
    return pl.pallas_call(
        body, name="attn_bwd_kv", grid=(nb,),
        in_specs=[kv, kv] + three(D) + three(D) + three(128) + three(128),
        out_specs=[out, out],
        out_shape=[jax.ShapeDtypeStruct((tl, KVH * HD), F32)] * 2,
        compiler_params=_cparams(("parallel",)),
    )(kr, vv, qr, qr, qr, do, do, do, lse, lse, lse, delta, delta, delta)


def _mm(a, b, ta=False, tb=False, out_dtype=F32, tm=512, tn=1024, tk=1024, name="mm", hi=False):
    m, kd = (a.shape[1], a.shape[0]) if ta else a.shape
    n = b.shape[0] if tb else b.shape[1]
    tm, tn, tk = min(tm, m), min(tn, n), min(tk, kd)
    assert m % tm == 0 and n % tn == 0 and kd % tk == 0, (name, m, n, kd, tm, tn, tk)
    nk = kd // tk
    dims = ((0,) if ta else (1,), (1,) if tb else (0,))

    def body(a_ref, b_ref, o_ref, *scr):
        part = _dot(a_ref[...], b_ref[...], dims, hi)
        if nk == 1:
            o_ref[...] = part.astype(out_dtype)
        else:
            acc = scr[0]
            kk = pl.program_id(2)

            @pl.when(kk == 0)
            def _():
                acc[...] = part

            @pl.when(kk > 0)
            def _():
                acc[...] += part

            @pl.when(kk == nk - 1)
            def _():
                o_ref[...] = acc[...].astype(out_dtype)

    a_spec = pl.BlockSpec((tk, tm), lambda i, j, k: (k, i)) if ta else pl.BlockSpec((tm, tk), lambda i, j, k: (i, k))
    b_spec = pl.BlockSpec((tn, tk), lambda i, j, k: (j, k)) if tb else pl.BlockSpec((tk, tn), lambda i, j, k: (k, j))
    return pl.pallas_call(
        body, name=name, grid=(m // tm, n // tn, nk),
        in_specs=[a_spec, b_spec],
        out_specs=pl.BlockSpec((tm, tn), lambda i, j, k: (i, j)),
        out_shape=jax.ShapeDtypeStruct((m, n), out_dtype),
        scratch_shapes=[] if nk == 1 else [pltpu.VMEM((tm, tn), F32)],
        compiler_params=_cparams(("parallel", "parallel", "arbitrary")),
    )(a, b)


HALO = 8


class _In:
    def __init__(self, arr, w=None, cb=0, roff=0, halo=None):
        self.arr, self.w, self.cb, self.roff, self.halo = arr, w or arr.shape[1], cb, roff, halo


class _Full:
    def __init__(self, arr, w=None, cb=0):
        self.arr, self.w, self.cb = arr, w, cb


class _Out:
    def __init__(self, cols, dtype=F32, w=None, cb=0, acc=False, rows=1, roff=0, nrows=None):
        self.cols, self.dtype, self.w, self.cb, self.acc, self.rows, self.roff, self.nrows = (
            cols, dtype, w or cols, cb, acc, rows, roff, nrows)


def _rowcall(name, fn, nrow_tiles, tile, ins, outs, ncol=1):
    arrays, specs, kinds = [], [], []
    for x in ins:
        if isinstance(x, _Full):
            arrays.append(x.arr)
            if x.w is None:
                specs.append(pl.BlockSpec(x.arr.shape, lambda j, i: (0, 0)))
            else:
                specs.append(pl.BlockSpec((x.arr.shape[0], x.w), lambda j, i, cb=x.cb: (0, cb + j)))
            kinds.append("full")
            continue
        w, cb, roff = x.w, x.cb, x.roff
        cur = pl.BlockSpec((tile, w), lambda j, i, cb=cb, roff=roff: (i + roff, cb + j))
        if x.halo is None:
            arrays.append(x.arr)
            specs.append(cur)
            kinds.append("tile")
        else:
            r8 = tile // HALO
            last = x.arr.shape[0] // HALO - 1
            prev = pl.BlockSpec((HALO, w), lambda j, i, cb=cb, roff=roff, r8=r8: (jnp.maximum((i + roff) * r8 - 1, 0), cb + j))
            nxt = pl.BlockSpec((HALO, w), lambda j, i, cb=cb, roff=roff, r8=r8, last=last:
                               (jnp.minimum((i + roff + 1) * r8, last), cb + j))
            arrays += [x.arr, x.arr, x.arr]
            specs += [prev, cur, nxt]
            kinds.append(("halo", x.halo))
    out_specs, out_shapes = [], []
    for o in outs:
        if o.acc:
            out_specs.append(pl.BlockSpec((o.rows, o.w), lambda j, i, cb=o.cb: (0, cb + j)))
            out_shapes.append(jax.ShapeDtypeStruct((o.rows, o.cols), o.dtype))
        else:
            out_specs.append(pl.BlockSpec((tile, o.w), lambda j, i, cb=o.cb, roff=o.roff: (i + roff, cb + j)))
            out_shapes.append(jax.ShapeDtypeStruct(((o.nrows or nrow_tiles * tile), o.cols), o.dtype))
    n_in = len(arrays)

    def body(*refs):
        j = pl.program_id(0)
        i = pl.program_id(1)
        vals, r = [], 0
        for kind in kinds:
            if kind in ("full", "tile"):
                vals.append(refs[r][...])
                r += 1
            else:
                pok, nok = kind[1]
                p, c, n = refs[r][...], refs[r + 1][...], refs[r + 2][...]
                p = jnp.where(pok(i), p, jnp.zeros_like(p))
                n = jnp.where(nok(i), n, jnp.zeros_like(n))
                vals.append(jnp.concatenate([p, c, n], axis=0))
                r += 3
        res = fn(i, j, *vals)
        for o, ref, val in zip(outs, refs[n_in:], res):
            if o.acc:
                @pl.when(i == 0)
                def _(ref=ref, val=val, o=o):
                    ref[...] = val.astype(o.dtype)

                @pl.when(i > 0)
                def _(ref=ref, val=val, o=o):
                    ref[...] += val.astype(o.dtype)
            else:
                ref[...] = val.astype(o.dtype)

    return pl.pallas_call(
        body, name=name, grid=(ncol, nrow_tiles), in_specs=specs, out_specs=out_specs, out_shape=out_shapes,
        compiler_params=_cparams(("parallel", "arbitrary")),
    )(*arrays)


def _shift(xe, s, tile):
    if s == 0:
        return xe[HALO:HALO + tile]
    return pltpu.roll(xe, (-s) % xe.shape[0], 0)[HALO:HALO + tile]


def _silu(x):
    return x * jax.nn.sigmoid(x)


def _dsilu(x):
    s = jax.nn.sigmoid(x)
    return s * (1.0 + x * (1.0 - s))


def _heads(x, fn):
    return jnp.concatenate([fn(h, x[:, h * HD:(h + 1) * HD]) for h in range(x.shape[1] // HD)], axis=1)


def _colsum(x):
    return jnp.sum(x, axis=0, keepdims=True)


def _rowmean(x):
    return jnp.mean(x, axis=1, keepdims=True)


def _rowsum(x):
    return jnp.sum(x, axis=1, keepdims=True)


TILE = 256
CT = CTX // TILE


def _all_halo(n_tiles):
    return (lambda i: i >= CT + 1, lambda i: jnp.logical_and(i >= CT, i < n_tiles - 1))


def _lat_halo(n_tiles):
    return (lambda i: i >= 1, lambda i: i < n_tiles - 1)


def _rms_mod(x, nm, shift, scale):
    r = lax.rsqrt(_rowmean(x * x) + EPS)
    return (x * r * nm) * (1.0 + scale) + shift


def _rms_mod_bwd(dh, x, nm, scale):
    r = lax.rsqrt(_rowmean(x * x) + EPS)
    xn = x * r
    dz = dh * (1.0 + scale)
    dxn = dz * nm
    dx = r * (dxn - xn * _rowmean(dxn * xn))
    return dx, _colsum(dz * xn), _colsum(dh), _colsum(dh * (xn * nm))


def _norm_mod(xa, nm, mod_c, mod_x):
    n = xa.shape[0] // TILE

    def fn(i, j, x, nm_, mc, mx):
        m = jnp.where(i < CT, mc, mx)
        return (_rms_mod(x, nm_, m[0:1], m[1:2]),)

    return _rowcall("norm_mod", fn, n, TILE, [_In(xa), _Full(nm), _Full(mod_c), _Full(mod_x)], [_Out(D, BF16)])[0]


def _norm_mod_bwd(dh, xa, dres, nm, mod, roff, n):
    ins = [_In(dh, roff=roff), _In(xa, roff=roff), _Full(nm), _Full(mod)] + ([] if dres is None else [_In(dres)])

    def fn(i, j, dh_, x, nm_, m, *rest):
        dx, dn, dsh, dsc = _rms_mod_bwd(dh_, x, nm_, m[1:2])
        if rest:
            return (dx + rest[0], dn, dsh, dsc)
        return (dn, dsh, dsc)

    accs = [_Out(D, acc=True), _Out(D, acc=True), _Out(D, acc=True)]
    return _rowcall("norm_mod_bwd", fn, n, TILE, ins, ([] if dres is None else [_Out(D)]) + accs)


DN_Q_SCALE = HD ** -0.5


def _conv_taps(xe, w, width, rows=None):
    r = width // 2
    acc = None
    for t in range(width):
        s = t - r
        if rows is None:
            sh = xe if s == 0 else pltpu.roll(xe, (-s) % xe.shape[0], 0)
        else:
            sh = _shift(xe, s, rows)
        term = sh * w[t:t + 1]
        acc = term if acc is None else acc + term
    return acc


def _l2n(x, scale):
    rn = lax.rsqrt(_rowsum(x * x) + EPS)
    return x * (rn * scale)


def _l2n_bwd(dy, x, scale):
    rn = lax.rsqrt(_rowsum(x * x) + EPS)
    xu = x * rn
    return (scale * rn) * (dy - xu * _rowsum(dy * xu))


def _softplus(x):
    return jnp.maximum(x, 0.0) + jnp.log(1.0 + jnp.exp(-jnp.abs(x)))


def _lane_mask(lo, hi_):
    lane = lax.broadcasted_iota(jnp.int32, (1, 128), 1)
    return jnp.logical_and(lane >= lo, lane < hi_).astype(F32)


def _dn_prep(p, conv_w, gprm):
    n = p.shape[0] // TILE
    halo = _all_halo(n)

    def fn(i, j, qe, ke, ve, ba, w, gp):
        cq = _conv_taps(qe, w[:, 0:D], 5, TILE)
        ck = _conv_taps(ke, w[:, D:2 * D], 5, TILE)
        cv = _conv_taps(ve, w[:, 2 * D:3 * D], 5, TILE)
        q = _heads(_silu(cq), lambda h, x: _l2n(x, DN_Q_SCALE))
        k = _heads(_silu(ck), lambda h, x: _l2n(x, 1.0))
        v = _silu(cv)
        beta = jax.nn.sigmoid(ba)
        g = -jnp.exp(gp[0:1]) * _softplus(ba + gp[1:2])
        m0, m1 = _lane_mask(0, 8), _lane_mask(8, 16)
        gb_f = beta * m0 + pltpu.roll(g, 128 - 8, 1) * m1
        gb_b = pltpu.roll(beta, 128 - 8, 1) * m0 + pltpu.roll(g, 128 - 16, 1) * m1
        return q, k, v, gb_f, gb_b

    ins = [_In(p, D, 0, halo=halo), _In(p, D, 1, halo=halo), _In(p, D, 2, halo=halo), _In(p, 128, C_BA // 128),
           _Full(conv_w), _Full(gprm)]
    return _rowcall("dn_prep", fn, n, TILE, ins, [_Out(D), _Out(D), _Out(D), _Out(128), _Out(128)])


def _dn_prep_bwd(p, conv_w, gprm, dq2, dk2, dv2, dgb2):
    n = p.shape[0] // TILE
    halo = _all_halo(n)

    def branch(xe, w, dye, scale):
        c = _conv_taps(xe, w, 5)
        sx = _silu(c)
        if scale is None:
            dsx = dye
        else:
            dsx = jnp.concatenate([_l2n_bwd(dye[:, h * HD:(h + 1) * HD], sx[:, h * HD:(h + 1) * HD], scale)
                                   for h in range(NH)], axis=1)
        dc = dsx * _dsilu(c)
        dx = None
        dws = []
        dcc = dc[HALO:HALO + TILE]
        for t in range(5):
            term = _shift(dc, 2 - t, TILE) * w[t:t + 1]
            dx = term if dx is None else dx + term
            dws.append(_colsum(dcc * _shift(xe, t - 2, TILE)))
        dw = jnp.concatenate(dws + [jnp.zeros((3, D), F32)], axis=0)
        return dx, dw

    def fn(i, j, qe, ke, ve, ba, w, gp, dq0, dq1, dk0, dk1, dv0, dv1, dg0, dg1):
        dxq, dwq = branch(qe, w[:, 0:D], dq0 + dq1, DN_Q_SCALE)
        dxk, dwk = branch(ke, w[:, D:2 * D], dk0 + dk1, 1.0)
        dxv, dwv = branch(ve, w[:, 2 * D:3 * D], dv0 + dv1, None)
        m0, m1 = _lane_mask(0, 8), _lane_mask(8, 16)
        dbeta = dg0 * m0 + pltpu.roll(dg1 * m0, 8, 1)
        dg = pltpu.roll(dg0 * m1, 8, 1) + pltpu.roll(dg1 * m1, 16, 1)
        beta = jax.nn.sigmoid(ba)
        ea = jnp.exp(gp[0:1])
        z = ba + gp[1:2]
        g = -ea * _softplus(z)
        mg = _lane_mask(16, 32)
        da = dg * (-ea) * jax.nn.sigmoid(z) * mg
        dba = dbeta * beta * (1.0 - beta) * _lane_mask(0, 16) + da
        dgp = jnp.concatenate([_colsum(dg * g * mg), _colsum(da)], axis=0)
        return (jnp.concatenate([dxq, dxk, dxv], axis=1), dba, jnp.concatenate([dwq, dwk, dwv], axis=1), dgp)

    ins = [_In(p, D, 0, halo=halo), _In(p, D, 1, halo=halo), _In(p, D, 2, halo=halo), _In(p, 128, C_BA // 128),
           _Full(conv_w), _Full(gprm),
           _In(dq2, halo=halo), _In(dq2, roff=n, halo=halo), _In(dk2, halo=halo), _In(dk2, roff=n, halo=halo),
           _In(dv2, halo=halo), _In(dv2, roff=n, halo=halo), _In(dgb2), _In(dgb2, roff=n)]
    return _rowcall("dn_prep_bwd", fn, n, TILE, ins,
                    [_Out(3 * D, BF16), _Out(128, BF16), _Out(3 * D, acc=True, rows=8), _Out(128, acc=True, rows=2)])


def _hnorm(x, w):
    return x * lax.rsqrt(_rowmean(x * x) + EPS) * w


def _hnorm_bwd(dy, x, w):
    r = lax.rsqrt(_rowmean(x * x) + EPS)
    xh = x * r
    dxh = dy * w
    return r * (dxh - xh * _rowmean(dxh * xh)), _colsum(dy * xh)


def _dn_gate(o2, p, dn_norm, n_all):
    n = n_all - CT

    def fn(i, j, of, ob, gt, w):
        o = of + ob
        return (_heads(o, lambda h, x: _hnorm(x, w)) * _silu(gt),)

    ins = [_In(o2, roff=CT), _In(o2, roff=n_all + CT), _In(p, D, C_GT // D, roff=CT), _Full(dn_norm)]
    return _rowcall("dn_gate", fn, n, TILE, ins, [_Out(D, BF16)])[0]


def _dn_gate_bwd(dy, o2, p, dn_norm, n_all):
    n = n_all - CT

    def fn(i, j, dy_, of, ob, gt, w):
        o = of + ob
        sg = _silu(gt)
        dos, dw = [], jnp.zeros((1, HD), F32)
        yn = []
        for h in range(NH):
            sl = slice(h * HD, (h + 1) * HD)
            dx, dwh = _hnorm_bwd(dy_[:, sl] * sg[:, sl], o[:, sl], w)
            dos.append(dx)
            dw = dw + dwh
            yn.append(_hnorm(o[:, sl], w))
        dgt = dy_ * jnp.concatenate(yn, axis=1) * _dsilu(gt)
        return jnp.concatenate(dos, axis=1), dgt, dw

    ins = [_In(dy), _In(o2, roff=CT), _In(o2, roff=n_all + CT), _In(p, D, C_GT // D, roff=CT), _Full(dn_norm)]
    return _rowcall("dn_gate_bwd", fn, n, TILE, ins, [_Out(D), _Out(D, BF16), _Out(HD, acc=True)])


def _rope_shuffle(x):
    lane = lax.broadcasted_iota(jnp.int32, (1, HD), 1)
    return jnp.where((lane % 64) < 32, pltpu.roll(x, HD - 32, 1), pltpu.roll(x, 32, 1))


def _rope(x, cos, sin):
    return x * cos + _rope_shuffle(x) * sin


def _rope_bwd(dy, cos, sin):
    return dy * cos + _rope_shuffle(dy * sin)


def _attn_prep(p, w, cos, sin, width, cb, roff, n, name):
    def fn(i, j, x, w_, c, s):
        return (_heads(x, lambda h, xh: _rope(_hnorm(xh, w_), c, s)),)

    ins = [_In(p, width, cb, roff=roff), _Full(w), _In(cos), _In(sin)]
    return _rowcall(name, fn, n, TILE, ins, [_Out(width)])[0]


def _attn_prep_bwd(dy, p, w, cos, sin, width, cb, roff, n, name):
    def fn(i, j, dy_, x, w_, c, s):
        dxs, dw = [], jnp.zeros((1, HD), F32)
        for h in range(width // HD):
            sl = slice(h * HD, (h + 1) * HD)
            dx, dwh = _hnorm_bwd(_rope_bwd(dy_[:, sl], c, s), x[:, sl], w_)
            dxs.append(dx)
            dw = dw + dwh
        return jnp.concatenate(dxs, axis=1), dw

    ins = [_In(dy), _In(p, width, cb, roff=roff), _Full(w), _In(cos), _In(sin)]
    return _rowcall(name, fn, n, TILE, ins, [_Out(width, BF16), _Out(HD, acc=True)])


def _merge(z_dn, z_at, p, n):
    def fn(i, j, zd, za, gd, ga):
        return (jax.nn.sigmoid(gd) * zd + jax.nn.sigmoid(ga) * za,)

    ins = [_In(z_dn), _In(z_at), _In(p, D, C_MG // D, roff=CT), _In(p, D, C_MG // D + 1, roff=CT)]
    return _rowcall("merge", fn, n, TILE, ins, [_Out(D, BF16)])[0]


def _merge_bwd(dm, z_dn, z_at, p, n):
    def fn(i, j, dm_, zd, za, gd, ga):
        sd, sa = jax.nn.sigmoid(gd), jax.nn.sigmoid(ga)
        dg = jnp.concatenate([dm_ * zd * sd * (1.0 - sd), dm_ * za * sa * (1.0 - sa)], axis=1)
        return dm_ * sd, dm_ * sa, dg

    ins = [_In(dm), _In(z_dn), _In(z_at), _In(p, D, C_MG // D, roff=CT), _In(p, D, C_MG // D + 1, roff=CT)]
    return _rowcall("merge_bwd", fn, n, TILE, ins, [_Out(D, BF16), _Out(D, BF16), _Out(2 * D, BF16)])


def _resid_norm(xa, mo, g_a, nf, mod_f, n):
    def fn(i, j, x, mo_, ga, nf_, m):
        x1 = x + ga * mo_
        return x1, _rms_mod(x1, nf_, m[0:1], m[1:2])

    ins = [_In(xa, roff=CT), _In(mo), _Full(g_a), _Full(nf), _Full(mod_f)]
    return _rowcall("resid_norm", fn, n, TILE, ins, [_Out(D), _Out(D, BF16)])


def _resid_norm_bwd(dy, dh2, x1, mo, g_a, nf, mod_f, n):
    def fn(i, j, dy_, dh_, x1_, mo_, ga, nf_, m):
        dx, dn, dsh, dsc = _rms_mod_bwd(dh_, x1_, nf_, m[1:2])
        dx1 = dy_ + dx
        return dx1, ga * dx1, dn, dsh, dsc, _colsum(dx1 * mo_)

    ins = [_In(dy), _In(dh2), _In(x1), _In(mo), _Full(g_a), _Full(nf), _Full(mod_f)]
    accs = [_Out(D, acc=True) for _ in range(4)]
    return _rowcall("resid_norm_bwd", fn, n, TILE, ins, [_Out(D), _Out(D, BF16)] + accs)


def _loss_head(x1, f, tgt, g_f, n):
    def fn(i, j, x1_, f_, t, gf):
        e = x1_ + gf * f_ - t
        dy = e * (1.0 / D)
        loss = _colsum(_rowsum(e * e)) * (0.5 / D)
        return dy, gf * dy, _colsum(dy * f_), jnp.broadcast_to(loss, (1, 128))

    ins = [_In(x1), _In(f), _In(tgt), _Full(g_f)]
    return _rowcall("loss_head", fn, n, TILE, ins, [_Out(D), _Out(D, BF16), _Out(D, acc=True), _Out(128, acc=True)])


FW = DFF // 2


def _ffn_act(u, conv_w, conv_b, n):
    halo = _lat_halo(n)

    def fn(i, j, ge, ve, wg, wv, bg, bv):
        cg = _conv_taps(ge, wg, 3, TILE) + bg
        cv = _conv_taps(ve, wv, 3, TILE) + bv
        return (_silu(cg) * cv,)

    ins = [_In(u, FW, 0, halo=halo), _In(u, FW, 2, halo=halo), _Full(conv_w, FW, 0), _Full(conv_w, FW, 2),
           _Full(conv_b, FW, 0), _Full(conv_b, FW, 2)]
    return _rowcall("ffn_act", fn, n, TILE, ins, [_Out(DFF, BF16, FW)], ncol=2)[0]


def _ffn_act_bwd(u, da, conv_w, conv_b, n):
    halo = _lat_halo(n)

    def fn(i, j, ge, ve, dae, wg, wv, bg, bv):
        cg = _conv_taps(ge, wg, 3) + bg
        cv = _conv_taps(ve, wv, 3) + bv
        dcg = dae * cv * _dsilu(cg)
        dcv = dae * _silu(cg)
        outs = []
        for dc, xe, w in ((dcg, ge, wg), (dcv, ve, wv)):
            dx, dws = None, []
            dcc = dc[HALO:HALO + TILE]
            for t in range(3):
                term = _shift(dc, 1 - t, TILE) * w[t:t + 1]
                dx = term if dx is None else dx + term
                dws.append(_colsum(dcc * _shift(xe, t - 1, TILE)))
            outs.append((dx, jnp.concatenate(dws + [jnp.zeros((5, FW), F32)], axis=0), _colsum(dcc)))
        return outs[0][0], outs[1][0], outs[0][1], outs[1][1], outs[0][2], outs[1][2]

    ins = [_In(u, FW, 0, halo=halo), _In(u, FW, 2, halo=halo), _In(da, FW, 0, halo=halo),
           _Full(conv_w, FW, 0), _Full(conv_w, FW, 2), _Full(conv_b, FW, 0), _Full(conv_b, FW, 2)]
    outs = [_Out(DFF, BF16, FW), _Out(DFF, BF16, FW), _Out(DFF, w=FW, acc=True, rows=8), _Out(DFF, w=FW, acc=True, rows=8),
            _Out(DFF, w=FW, acc=True), _Out(DFF, w=FW, acc=True)]
    return _rowcall("ffn_act_bwd", fn, n, TILE, ins, outs, ncol=2)


def _rope_tables(tl):
    t = jnp.arange(tl, dtype=jnp.int32)
    row = (t // GRID_W).astype(F32)
    col = (t % GRID_W).astype(F32)
    inv = ROPE_BASE ** (-jnp.arange(32, dtype=F32) / 32)
    ar, ac = row[:, None] * inv, col[:, None] * inv
    cos = jnp.concatenate([jnp.cos(ar), jnp.cos(ar), jnp.cos(ac), jnp.cos(ac)], axis=1)
    sin = jnp.concatenate([-jnp.sin(ar), jnp.sin(ar), -jnp.sin(ac), jnp.sin(ac)], axis=1)
    return cos, sin


def _pad_w_in(w_in):
    z = lambda n: jnp.zeros((D, n), w_in.dtype)
    return jnp.concatenate([w_in[:, 0:4096], w_in[:, 4128:5152], w_in[:, 5664:7712], w_in[:, 5152:5664],
                            w_in[:, 4096:4128], z(96 + PW - C_PAD)], axis=1)


def _unpad_w_in(g):
    return jnp.concatenate([g[:, 0:4096], g[:, C_BA:C_BA + 32], g[:, C_QAT:C_QAT + D], g[:, C_KAT:C_KAT + 512],
                            g[:, C_MG:C_MG + 2 * D]], axis=1)


def _local_step(xa, tgt, mod_x, mod_c, w, hi=False):
    t_all = xa.shape[0]
    tl = t_all - CTX
    n_all, n = t_all // TILE, tl // TILE
    tm_all = 640 if t_all % 640 == 0 else TILE
    tm_lat = 512
    mm = functools.partial(_mm, hi=hi)
    sp = lambda m: [m[:, k * D:(k + 1) * D] for k in range(6)]
    sh_a, sc_a, g_a, sh_f, sc_f, g_f = sp(mod_x)
    sh_ac, sc_ac = sp(mod_c)[:2]
    mod_ax = jnp.concatenate([sh_a, sc_a], axis=0)
    mod_ac = jnp.concatenate([sh_ac, sc_ac], axis=0)
    mod_f = jnp.concatenate([sh_f, sc_f], axis=0)
    nm, nf = w["norm_mix"], w["norm_ffn"]
    cos, sin = _rope_tables(tl)
    cos_all = jnp.concatenate([jnp.ones((CTX, HD), F32), cos], axis=0)
    sin_all = jnp.concatenate([jnp.zeros((CTX, HD), F32), sin], axis=0)
    conv_dn = jnp.concatenate([w["dn_conv"], jnp.zeros((3, 3 * D), F32)], axis=0)
    gprm = jnp.zeros((2, 128), F32).at[0, 16:32].set(w["dn_a_log"].reshape(16)).at[1, 16:32].set(w["dn_dt_bias"].reshape(16))
    conv_ff = jnp.concatenate([w["ffn_conv"], jnp.zeros((5, 2 * DFF), F32)], axis=0)
    sink = jnp.zeros((1, 128), F32).at[0, 0:NH].set(w["attn_sink"].reshape(NH))
    nct = CTX // CH

    h = _norm_mod(xa, nm, mod_ac, mod_ax)
    p = mm(h, w["w_in_p"], tm=tm_all, tn=1024, name="mm_in")
    q, k, v, gb_f, gb_b = _dn_prep(p, conv_dn, gprm)
    gb = jnp.stack([gb_f, gb_b])
    o2, s_hist = _dn_fwd(q, k, v, gb, nct, hi)
    o2 = o2.reshape(2 * t_all, D)
    y_dn = _dn_gate(o2, p, w["dn_norm"], n_all)
    qr = _attn_prep(p, w["q_norm"], cos, sin, D, C_QAT // D, CT, n, "attn_prep_q")
    kr = _attn_prep(p, w["k_norm"], cos_all, sin_all, KVH * HD, C_KAT // (KVH * HD), 0, n_all, "attn_prep_k")
    vv = p[:, C_VAT:C_VAT + KVH * HD]
    o_at, lse = _attn_fwd(qr, kr, vv, sink, hi)
    z_dn = mm(y_dn, w["w_branch_dn"], tm=tm_lat, name="mm_bdn")
    z_at = mm(o_at, w["w_branch_attn"], tm=tm_lat, name="mm_bat")
    merged = _merge(z_dn, z_at, p, n)
    mo = mm(merged, w["w_out"], tm=tm_lat, name="mm_out")
    x1, h2 = _resid_norm(xa, mo, g_a, nf, mod_f, n)
    u = mm(h2, w["ffn_up"], tm=tm_lat, tn=1408, name="mm_up")
    a = _ffn_act(u, conv_ff, w["ffn_conv_b"], n)
    f = mm(a, w["ffn_down"], tm=tm_lat, tk=DFF, name="mm_down")
    dy, df, dg_f, loss = _loss_head(x1, f, tgt, g_f, n)

    g = {}
    da = mm(df, w["ffn_down"], tb=True, tm=tm_lat, tn=1408, name="mm_down_dx")
    g["ffn_down"] = mm(a, df, ta=True, tm=1408, tn=1024, tk=tm_lat, name="mm_down_dw")
    du_g, du_v, dcw_g, dcw_v, dcb_g, dcb_v = _ffn_act_bwd(u, da, conv_ff, w["ffn_conv_b"], n)
    du = jnp.concatenate([du_g, du_v], axis=1)
    g["ffn_conv"] = jnp.concatenate([dcw_g, dcw_v], axis=1)[0:3]
    g["ffn_conv_b"] = jnp.concatenate([dcb_g, dcb_v], axis=1)
    dh2 = mm(du, w["ffn_up"], tb=True, tm=tm_lat, tk=1408, name="mm_up_dx")
    g["ffn_up"] = mm(h2, du, ta=True, tm=512, tn=1408, tk=tm_lat, name="mm_up_dw")
    dx1, dmo, g["norm_ffn"], dsh_f, dsc_f, dg_a = _resid_norm_bwd(dy, dh2, x1, mo, g_a, nf, mod_f, n)
    dmerged = mm(dmo, w["w_out"], tb=True, tm=tm_lat, name="mm_out_dx")
    g["w_out"] = mm(merged, dmo, ta=True, tm=512, tk=tm_lat, name="mm_out_dw")
    dz_dn, dz_at, dmg = _merge_bwd(dmerged, z_dn, z_at, p, n)
    dy_dn = mm(dz_dn, w["w_branch_dn"], tb=True, tm=tm_lat, name="mm_bdn_dx")
    g["w_branch_dn"] = mm(y_dn, dz_dn, ta=True, tm=512, tk=tm_lat, name="mm_bdn_dw")
    do_at = mm(dz_at, w["w_branch_attn"], tb=True, tm=tm_lat, name="mm_bat_dx")
    g["w_branch_attn"] = mm(o_at, dz_at, ta=True, tm=512, tk=tm_lat, name="mm_bat_dw")

    do_dn, dgt, g["dn_norm"] = _dn_gate_bwd(dy_dn, o2, p, w["dn_norm"], n_all)
    do_all = jnp.concatenate([jnp.zeros((CTX, D), F32), do_dn], axis=0)
    dq2, dk2, dv2, dgb2 = _dn_bwd(q, k, v, gb, s_hist, do_all, nct, hi)
    dqkv, dba, dconv, dgprm = _dn_prep_bwd(p, conv_dn, gprm, dq2.reshape(2 * t_all, D), dk2.reshape(2 * t_all, D),
                                           dv2.reshape(2 * t_all, D), dgb2.reshape(2 * t_all, 128))
    g["dn_conv"] = dconv[0:5]
    g["dn_a_log"] = dgprm[0, 16:32].reshape(2, NH)
    g["dn_dt_bias"] = dgprm[1, 16:32].reshape(2, NH)

    delta = _attn_delta(o_at, do_at)
    dqr, dkx, dvx, dsink = _attn_bwd_q(qr, kr, vv, sink, do_at, lse, delta, hi)
    dk_lat, dv_lat = _attn_bwd_kv(qr, kr, vv, do_at, lse, delta, hi)
    g["attn_sink"] = dsink[:, 0:NH]
    dq_at, g["q_norm"] = _attn_prep_bwd(dqr, p, w["q_norm"], cos, sin, D, C_QAT // D, CT, n, "attn_prep_q_bwd")
    dkr = jnp.concatenate([dkx, dk_lat], axis=0)
    dk_at, g["k_norm"] = _attn_prep_bwd(dkr, p, w["k_norm"], cos_all, sin_all, KVH * HD, C_KAT // (KVH * HD), 0, n_all,
                                        "attn_prep_k_bwd")
    dv_at = jnp.concatenate([dvx, dv_lat], axis=0).astype(BF16)

    zc = lambda width: jnp.zeros((CTX, width), BF16)
    dp = jnp.concatenate([
        dqkv,
        jnp.concatenate([zc(D), dgt], axis=0),
        jnp.concatenate([zc(D), dq_at], axis=0),
        jnp.concatenate([zc(2 * D), dmg], axis=0),
        dk_at, dv_at, dba, jnp.zeros((t_all, PW - C_PAD), BF16)], axis=1)
    dh = mm(dp, w["w_in_p"], tb=True, tm=tm_all, tn=1024, tk=2048, name="mm_in_dx")
    g["w_in_p"] = mm(h, dp, ta=True, tm=512, tn=2048, tk=tm_all, name="mm_in_dw")
    dnm_c, dsh_ac, dsc_ac = _norm_mod_bwd(dh, xa, None, nm, mod_ac, 0, CT)
    grad_x, dnm_x, dsh_a, dsc_a = _norm_mod_bwd(dh, xa, dx1, nm, mod_ax, CT, n)
    g["norm_mix"] = dnm_c + dnm_x
    dmod_x = jnp.concatenate([dsh_a, dsc_a, dg_a, dsh_f, dsc_f, dg_f], axis=1)
    dmod_c = jnp.concatenate([dsh_ac, dsc_ac, jnp.zeros((1, 4 * D), F32)], axis=1)
    return loss, grad_x, g, dmod_x, dmod_c


def _sum_slots(buf, n_slots, rows, tile, name, stride=1):
    nt = rows // tile

    def fn(i, j, *vals):
        acc = vals[0]
        for v in vals[1:]:
            acc = acc + v
        return (acc,)

    ins = [_In(buf, roff=k * stride * nt) for k in range(n_slots)]
    return _rowcall(name, fn, nt, tile, ins, [_Out(buf.shape[1])])[0]


ADAM_LR, ADAM_B1, ADAM_B2, ADAM_EPS, ADAM_WD, ADAM_STEP = 0.001, 0.9, 0.999, 1e-08, 0.01, 10


def _row_tile(rows, cols):
    for t in (512, 256, 128, 64, 32, 16, 8):
        if rows % t == 0 and t * cols * 4 * 14 <= 40 * 1024 * 1024:
            return t
    return rows


def _adamw(w, g, m, v, name):
    shape = w.shape
    cols = shape[-1]
    rows = max(1, math.prod(shape[:-1]))
    tile = _row_tile(rows, cols)
    c1 = 1.0 / (1.0 - ADAM_B1 ** ADAM_STEP)
    c2 = 1.0 / (1.0 - ADAM_B2 ** ADAM_STEP)

    def fn(i, j, w_, g_, m_, v_):
        mn = ADAM_B1 * m_ + (1.0 - ADAM_B1) * g_
        vn = ADAM_B2 * v_ + (1.0 - ADAM_B2) * (g_ * g_)
        delta = -ADAM_LR * ((mn * c1) / (jnp.sqrt(vn * c2) + ADAM_EPS) + ADAM_WD * w_)
        return delta, mn, vn

    r2 = lambda a: a.reshape(rows, cols)
    outs = _rowcall(name, fn, rows // tile, tile, [_In(r2(w)), _In(r2(g)), _In(r2(m)), _In(r2(v))],
                    [_Out(cols), _Out(cols), _Out(cols)])
    return [o.reshape(shape) for o in outs]


MESH = pl.DeviceIdType.MESH
ANY = pl.BlockSpec(memory_space=pl.ANY)


def _pos():
    return lax.axis_index("x"), lax.axis_index("y"), lax.axis_index("c")


def _all_gather(blk, name):
    m_per, n = blk.shape

    def body(x_ref, out_ref, send_sems, recv_sems, local_sem):
        x, y, c = _pos()
        me, sibling = (x, y, c), (x, y, 1 - c)
        chips = [(1 - x, y), (x, 1 - y), (1 - x, 1 - y)]

        def rows(px, py, pc):
            return out_ref.at[pl.ds(pl.multiple_of((4 * px + 2 * py + pc) * m_per, 8), m_per), :]

        def copy(k, block, to, src=None):
            return pltpu.make_async_remote_copy(
                src_ref=rows(*block) if src is None else src, dst_ref=rows(*block),
                send_sem=send_sems.at[k], recv_sem=recv_sems.at[k], device_id=to, device_id_type=MESH)

        mine = pltpu.make_async_copy(x_ref, rows(*me), local_sem)
        mine.start()
        first = [copy(0, me, sibling, src=x_ref)]
        first += [copy(1 + j, me, (*chip, c), src=x_ref) for j, chip in enumerate(chips)]
        for cp in first:
            cp.start()
        passed = [copy(4 + j, (*chip, c), sibling) for j, chip in enumerate(chips)]
        for j, chip in enumerate(chips):
            copy(1 + j, (*chip, c), me).wait_recv()
            passed[j].start()
        copy(0, sibling, me).wait_recv()
        for j, chip in enumerate(chips):
            copy(4 + j, (*chip, 1 - c), me).wait_recv()
        for cp in first + passed:
            cp.wait_send()
        mine.wait()

    return pl.pallas_call(
        body, name=name, out_shape=jax.ShapeDtypeStruct((N_DEV * m_per, n), blk.dtype),
        in_specs=[ANY], out_specs=ANY,
        scratch_shapes=[pltpu.SemaphoreType.DMA((7,)), pltpu.SemaphoreType.DMA((7,)), pltpu.SemaphoreType.DMA],
        compiler_params=pltpu.CompilerParams(has_side_effects=True),
    )(blk)


def _flip(v, bit):
    return 1 - v if bit else v


def _exchange_pieces(pieces, rows, name):
    n = pieces.shape[1]

    def body(g_ref, out_ref, send_sems, recv_sems, local_sem):
        x, y, c = _pos()

        def piece(px, py, pc):
            return g_ref.at[pl.ds(pl.multiple_of((4 * px + 2 * py + pc) * rows, 8), rows), :]

        own = pltpu.make_async_copy(piece(x, y, c), out_ref.at[pl.ds(0, rows), :], local_sem)
        own.start()
        copies = []
        for k in range(1, N_DEV):
            px, py, pc = _flip(x, k & 4), _flip(y, k & 2), _flip(c, k & 1)
            cp = pltpu.make_async_remote_copy(
                src_ref=piece(px, py, pc), dst_ref=out_ref.at[pl.ds(k * rows, rows), :],
                send_sem=send_sems.at[k - 1], recv_sem=recv_sems.at[k - 1], device_id=(px, py, pc), device_id_type=MESH)
            cp.start()
            copies.append(cp)
        for cp in copies:
            cp.wait_recv()
        for cp in copies:
            cp.wait_send()
        own.wait()

    return pl.pallas_call(
        body, name=name, out_shape=jax.ShapeDtypeStruct((N_DEV * rows, n), pieces.dtype),
        in_specs=[ANY], out_specs=ANY,
        scratch_shapes=[pltpu.SemaphoreType.DMA((7,)), pltpu.SemaphoreType.DMA((7,)), pltpu.SemaphoreType.DMA],
        compiler_params=pltpu.CompilerParams(has_side_effects=True),
    )(pieces)


def _pair_swap(blk, name):
    r, n = blk.shape

    def body(x_ref, out_ref, send_sem, recv_sem, local_sem):
        x, y, c = _pos()
        mine = out_ref.at[pl.ds(pl.multiple_of(c * r, 8), r), :]
        own = pltpu.make_async_copy(x_ref, mine, local_sem)
        own.start()
        cp = pltpu.make_async_remote_copy(src_ref=x_ref, dst_ref=mine, send_sem=send_sem, recv_sem=recv_sem,
                                          device_id=(x, y, 1 - c), device_id_type=MESH)
        cp.start()
        theirs = out_ref.at[pl.ds(pl.multiple_of((1 - c) * r, 8), r), :]
        pltpu.make_async_remote_copy(src_ref=x_ref, dst_ref=theirs, send_sem=send_sem, recv_sem=recv_sem,
                                     device_id=(x, y, 1 - c), device_id_type=MESH).wait_recv()
        cp.wait_send()
        own.wait()

    return pl.pallas_call(
        body, name=name, out_shape=jax.ShapeDtypeStruct((2 * r, n), blk.dtype),
        in_specs=[ANY], out_specs=ANY,
        scratch_shapes=[pltpu.SemaphoreType.DMA, pltpu.SemaphoreType.DMA, pltpu.SemaphoreType.DMA],
        compiler_params=pltpu.CompilerParams(has_side_effects=True),
    )(blk)


BIG = ("w_in", "w_branch_dn", "w_branch_attn", "w_out", "ffn_up", "ffn_down")
BIG_SHARD = {"w_in": (1024, 1928, True), "w_branch_dn": (256, 1024, False), "w_branch_attn": (256, 1024, False),
             "w_out": (256, 1024, False), "ffn_up": (1024, 1408, True), "ffn_down": (704, 1024, False)}
BIG_ROWS = {k: r * c // 2 // 128 for k, (r, c, _) in BIG_SHARD.items()}
PIECE = 19456
assert sum(BIG_ROWS.values()) <= PIECE


def _pack_half(shards, ci, dtype):
    parts = []
    for k in BIG:
        r, c, _ = BIG_SHARD[k]
        parts.append(lax.dynamic_slice_in_dim(shards[k], ci * (r // 2), r // 2, axis=0).reshape(-1, 128).astype(dtype))
    parts.append(jnp.zeros((PIECE - sum(BIG_ROWS.values()), 128), dtype))
    return jnp.concatenate(parts, axis=0)


def _unpack_full(ag):
    out, off = {}, 0
    for k in BIG:
        r, c, by_col = BIG_SHARD[k]
        blk = ag[:, off:off + BIG_ROWS[k]].reshape(4, r, c)
        out[k] = jnp.transpose(blk, (1, 0, 2)).reshape(r, 4 * c) if by_col else blk.reshape(4 * r, c)
        off += BIG_ROWS[k]
    return out


def _pack_pieces(full):
    parts = []
    for k in BIG:
        r, c, by_col = BIG_SHARD[k]
        a = full[k]
        a = jnp.transpose(a.reshape(r, 4, c), (1, 0, 2)) if by_col else a.reshape(4, r, c)
        parts.append(a.reshape(N_DEV, BIG_ROWS[k], 128))
    parts.append(jnp.zeros((N_DEV, PIECE - sum(BIG_ROWS.values()), 128), F32))
    return jnp.concatenate(parts, axis=1).reshape(N_DEV * PIECE, 128)


def _unpack_shard(two):
    out, off = {}, 0
    for k in BIG:
        r, c, _ = BIG_SHARD[k]
        out[k] = two[:, off:off + BIG_ROWS[k]].reshape(r, c)
        off += BIG_ROWS[k]
    return out


SMALL = (("dn_conv", 120), ("ffn_conv", 132), ("ffn_conv_b", 44), ("norm_mix", 8), ("norm_ffn", 8), ("dn_a_log", 1),
         ("dn_dt_bias", 1), ("dn_norm", 1), ("q_norm", 1), ("k_norm", 1), ("attn_sink", 1), ("dmod_c", 48), ("dmod_x", 48))
SMALL_ROWS = 416


def _rows128(a, rows):
    flat = a.reshape(-1)
    return jnp.concatenate([flat, jnp.zeros((rows * 128 - flat.shape[0],), F32)]).reshape(rows, 128)


def _pack_small(g):
    parts = [_rows128(g[k], r) for k, r in SMALL]
    parts.append(jnp.zeros((SMALL_ROWS - sum(r for _, r in SMALL), 128), F32))
    return jnp.concatenate(parts, axis=0)


def _unpack_small(buf, shapes):
    out, off = {}, 0
    for k, r in SMALL:
        n = math.prod(shapes[k])
        out[k] = buf[off:off + r].reshape(-1)[:n].reshape(shapes[k])
        off += r
    return out


WEIGHTS = ("c_ctx", "w_ada", "b_ada", "norm_mix", "norm_ffn", "w_in", "dn_conv", "dn_a_log", "dn_dt_bias", "dn_norm",
           "q_norm", "k_norm", "attn_sink", "w_branch_dn", "w_branch_attn", "w_out", "ffn_up", "ffn_conv", "ffn_conv_b",
           "ffn_down")


def kernel(x, c, ctx, c_ctx, w_ada, b_ada, norm_mix, norm_ffn, w_in, dn_conv, dn_a_log, dn_dt_bias, dn_norm, q_norm, k_norm, attn_sink, w_branch_dn, w_branch_attn, w_out, ffn_up, ffn_conv, ffn_conv_b, ffn_down, loss_target, m_c_ctx, m_w_ada, m_b_ada, m_norm_mix, m_norm_ffn, m_w_in, m_dn_conv, m_dn_a_log, m_dn_dt_bias, m_dn_norm, m_q_norm, m_k_norm, m_attn_sink, m_w_branch_dn, m_w_branch_attn, m_w_out, m_ffn_up, m_ffn_conv, m_ffn_conv_b, m_ffn_down, v_c_ctx, v_w_ada, v_b_ada, v_norm_mix, v_norm_ffn, v_w_in, v_dn_conv, v_dn_a_log, v_dn_dt_bias, v_dn_norm, v_q_norm, v_k_norm, v_attn_sink, v_w_branch_dn, v_w_branch_attn, v_w_out, v_ffn_up, v_ffn_conv, v_ffn_conv_b, v_ffn_down):
    args = dict(locals())
    xi, yi, ci = _pos()
    dev = 4 * xi + 2 * yi + ci
    shard = 2 * xi + yi
    chips = lambda a: a[0::2]

    blk = jnp.concatenate([_rows128(c, 8), _rows128(dn_conv, 30), _rows128(ffn_conv, 33), jnp.zeros((1, 128), F32)], axis=0)
    ag = _all_gather(blk, "ag_small_in").reshape(N_DEV, 72, 128)
    c_all = ag[:, 0:8].reshape(N_DEV, D)
    dn_conv_full = jnp.transpose(chips(ag)[:, 8:38].reshape(4, 5, 768), (1, 0, 2)).reshape(5, 3 * D)
    ffn_conv_full = jnp.transpose(chips(ag)[:, 38:71].reshape(4, 3, 1408), (1, 0, 2)).reshape(3, 2 * DFF)

    c16 = jnp.concatenate([c_all, c_ctx[None], jnp.zeros((7, D), F32)], axis=0)
    a16 = _rowcall("ada_silu", lambda i, j, v: (_silu(v),), 1, 16, [_In(c16)], [_Out(D)])[0]
    m_sh = _mm(a16, w_ada[0], tm=16, tn=512, tk=D, name="ada_fwd", hi=True)
    mod16 = chips(_all_gather(m_sh, "ag_mod").reshape(N_DEV, 16, 1536))
    mod16 = jnp.transpose(mod16, (1, 0, 2)).reshape(16, 6 * D) + b_ada
    mod_x = lax.dynamic_slice_in_dim(mod16, dev, 1, axis=0)
    mod_c = mod16[8:9]

    shards = {k: args[k][0] for k in BIG}
    wfull = _unpack_full(_all_gather(_pack_half(shards, ci, BF16), "ag_weights").reshape(N_DEV, PIECE, 128))
    w = dict(wfull)
    w["w_in_p"] = _pad_w_in(wfull["w_in"])
    w.update(norm_mix=norm_mix, norm_ffn=norm_ffn, dn_conv=dn_conv_full, dn_a_log=dn_a_log[0], dn_dt_bias=dn_dt_bias[0],
             dn_norm=dn_norm, q_norm=q_norm, k_norm=k_norm, attn_sink=attn_sink, ffn_conv=ffn_conv_full, ffn_conv_b=ffn_conv_b)

    xa = jnp.concatenate([ctx[0], x[0]], axis=0)
    loss_part, grad_x, g, dmod_x, dmod_c = _local_step(xa, loss_target[0], mod_x, mod_c, w)
    loss = lax.psum(loss_part[0, 0], ("x", "y", "c"))

    g["w_in"] = _unpad_w_in(g["w_in_p"])
    recv = _exchange_pieces(_pack_pieces(g), PIECE, "rs_exchange")
    mine = _sum_slots(recv, N_DEV, PIECE, 1024, "rs_sum")
    gshard = _unpack_shard(_pair_swap(mine, "rs_pair").reshape(2, PIECE, 128))

    g["dmod_c"], g["dmod_x"] = dmod_c, dmod_x
    ag_s = _all_gather(_pack_small(g), "ag_small_grads")
    shapes = {k: g[k].shape for k, _ in SMALL}
    gs = _unpack_small(_sum_slots(ag_s, N_DEV, SMALL_ROWS, SMALL_ROWS, "small_sum"), shapes)
    dx_all = ag_s.reshape(N_DEV, SMALL_ROWS, 128)[:, SMALL_ROWS - 50:SMALL_ROWS - 2].reshape(N_DEV, 6 * D)

    d16 = jnp.concatenate([dx_all, gs["dmod_c"], jnp.zeros((7, 6 * D), F32)], axis=0)
    d16_sh = lax.dynamic_slice_in_dim(d16, shard * 1536, 1536, axis=1)
    g_w_ada = _mm(a16, d16_sh, ta=True, tm=D, tn=512, tk=16, name="ada_dw", hi=True)
    g_b_ada = _rowcall("ada_db", lambda i, j, v: (_colsum(v),), 1, 16, [_In(d16)], [_Out(6 * D, acc=True)])[0]
    da_part = _mm(d16_sh, w_ada[0], tb=True, tm=16, tn=D, tk=512, name="ada_dx", hi=True)
    da_all = _all_gather(da_part, "ag_ada_dx")
    da16 = _sum_slots(da_all, 4, 16, 16, "ada_dx_sum", stride=2)
    dc16 = _rowcall("ada_dsilu", lambda i, j, d_, v: (d_ * _dsilu(v),), 1, 16, [_In(da16), _In(c16)], [_Out(D)])[0]

    grads = {
        "c_ctx": dc16[8], "w_ada": g_w_ada[None], "b_ada": g_b_ada, "norm_mix": gs["norm_mix"], "norm_ffn": gs["norm_ffn"],
        "w_in": gshard["w_in"][None],
        "dn_conv": lax.dynamic_slice_in_dim(gs["dn_conv"], shard * 768, 768, axis=1)[None],
        "dn_a_log": gs["dn_a_log"][None], "dn_dt_bias": gs["dn_dt_bias"][None], "dn_norm": gs["dn_norm"],
        "q_norm": gs["q_norm"], "k_norm": gs["k_norm"], "attn_sink": gs["attn_sink"],
        "w_branch_dn": gshard["w_branch_dn"][None], "w_branch_attn": gshard["w_branch_attn"][None],
        "w_out": gshard["w_out"][None], "ffn_up": gshard["ffn_up"][None],
        "ffn_conv": lax.dynamic_slice_in_dim(gs["ffn_conv"], shard * 1408, 1408, axis=1)[None],
        "ffn_conv_b": gs["ffn_conv_b"], "ffn_down": gshard["ffn_down"][None],
    }
    deltas, new_m, new_v = [], [], []
    for k in WEIGHTS:
        d_, m_, v_ = _adamw(args[k], grads[k], args["m_" + k], args["v_" + k], "adamw_" + k)
        deltas.append(d_)
        new_m.append(m_)
        new_v.append(v_)
    return (loss, grad_x[None], *[grads[k] for k in WEIGHTS], *deltas, *new_m, *new_v)
```

```python
import functools
import math

import numpy as np
import jax
import jax.numpy as jnp
from jax import lax
from jax.experimental import pallas as pl
from jax.experimental.pallas import tpu as pltpu

F32 = jnp.float32
BF16 = jnp.bfloat16
HI = lax.Precision.HIGHEST

D = 1024
NH = 8
HD = 128
CH = 64
CTX = 256
AB = 128
KVH = 2
GRP = 4
DFF = 2816
EPS = 1e-6
GRID_W = 64
ROPE_BASE = 10000.0
N_DEV = 8
VMEM_LIMIT = 56 * 1024 * 1024

C_QKV, C_GT, C_QAT, C_MG, C_KAT, C_VAT, C_BA, C_PAD = 0, 3072, 4096, 5120, 7168, 7424, 7680, 7808
PW = 8192


def _cparams(sem=None, **kw):
    return pltpu.CompilerParams(dimension_semantics=sem, vmem_limit_bytes=VMEM_LIMIT, **kw)


def _dot(a, b, dims, hi):
    if hi:
        return lax.dot_general(a.astype(F32), b.astype(F32), (dims, ((), ())), precision=HI, preferred_element_type=F32)
    return lax.dot_general(a.astype(BF16), b.astype(BF16), (dims, ((), ())), preferred_element_type=F32)


NN = ((1,), (0,))
NT = ((1,), (1,))
TN = ((0,), (0,))


def _dn_masks():
    i = np.arange(CH)
    lo_incl = (i[:, None] >= i[None, :]).astype(np.float32)
    lo_strict = (i[:, None] > i[None, :]).astype(np.float32)
    return jnp.asarray(np.stack([np.stack([lo_incl, lo_strict]), np.stack([lo_incl.T, lo_strict.T])]))


def _dn_chunk_index(d, i, n_ctx_chunks, n_chunks):
    fwd = i
    bwd = jnp.where(i < n_ctx_chunks, n_ctx_chunks - 1 - i, n_chunks - 1 + n_ctx_chunks - i)
    return jnp.where(d == 0, fwd, bwd)


BNN = ((2,), (1,))
BNT = ((2,), (2,))
BTN = ((1,), (1,))


def _bdot(a, b, dims, hi):
    dn = (dims, ((0,), (0,)))
    if hi:
        return lax.dot_general(a.astype(F32), b.astype(F32), dn, precision=HI, preferred_element_type=F32)
    return lax.dot_general(a.astype(BF16), b.astype(BF16), dn, preferred_element_type=F32)


def _bdot3(a, b, dims, hi):
    if hi:
        return _bdot(a, b, dims, True)
    ah, bh = a.astype(BF16), b.astype(BF16)
    al, bl = (a - ah.astype(F32)).astype(BF16), (b - bh.astype(F32)).astype(BF16)
    dn = (dims, ((0,), (0,)))
    d = lambda x_, y_: lax.dot_general(x_, y_, dn, preferred_element_type=F32)
    return d(ah, bh) + d(ah, bl) + d(al, bh)


def _dn_heads(ref):
    return jnp.stack([ref[:, h * HD:(h + 1) * HD] for h in range(NH)])


def _dn_scalars(gb, mi):
    gcum, gcum_t, gtot = _dn_gcum(gb, mi)
    beta = jnp.stack([gb[:, h:h + 1] for h in range(NH)])
    gc = jnp.stack([gcum[:, NH + h:NH + h + 1] for h in range(NH)])
    gcr = jnp.stack([gcum_t[NH + h:NH + h + 1, :] for h in range(NH)])
    gt = jnp.stack([gtot[:, NH + h:NH + h + 1] for h in range(NH)])
    return beta, gc, gcr, gt


def _dn_total(gb):
    gtot = jnp.sum(gb, axis=0, keepdims=True)
    return jnp.stack([gtot[:, NH + h:NH + h + 1] for h in range(NH)])


def _dn_gcum(gb, mi):
    gcum = _dot(mi, gb, NN, True)
    gtot = jnp.sum(gb, axis=0, keepdims=True)
    return gcum, gcum.T, gtot


def _dn_specs(n_ctx_chunks, n_chunks, reverse):
    def cidx(d, i):
        return _dn_chunk_index(d, n_chunks - 1 - i if reverse else i, n_ctx_chunks, n_chunks)

    tok = pl.BlockSpec((CH, D), lambda d, i: (cidx(d, i), 0))
    tok_d = pl.BlockSpec((1, CH, D), lambda d, i: (d, cidx(d, i), 0))
    gbs = pl.BlockSpec((1, CH, 128), lambda d, i: (d, cidx(d, i), 0))
    msk = pl.BlockSpec((1, 2, CH, CH), lambda d, i: (d, 0, 0, 0))

    def per_chunk(*tail):
        return pl.BlockSpec((1, 1) + tail, lambda d, i: (d, cidx(d, i)) + (0,) * len(tail))

    return tok, tok_d, gbs, msk, per_chunk


def _dn_intra_fwd(q, k, v, gb, n_ctx_chunks, hi):
    t_all = q.shape[0]
    n_chunks = t_all // CH
    masks = _dn_masks()

    def body(q_ref, k_ref, v_ref, gb_ref, m_ref, u_ref, w_ref, qg_ref, kd_ref, pm_ref, t_ref):
        mi, ms = m_ref[0, 0], m_ref[0, 1]
        beta, gc, gcr, gt = _dn_scalars(gb_ref[0], mi)
        q_, k_, v_ = _dn_heads(q_ref), _dn_heads(k_ref), _dn_heads(v_ref)
        decay = jnp.exp(jnp.where(mi > 0, gc - gcr, 0.0)) * mi
        e = jnp.exp(gc)
        a = ms * (beta * _bdot(k_, k_, BNT, hi) * decay)
        x = -a
        eye = (lax.broadcasted_iota(jnp.int32, (CH, CH), 0) == lax.broadcasted_iota(jnp.int32, (CH, CH), 1)).astype(F32)
        t = eye + x
        p = x
        for _ in range(5):
            p = _bdot3(p, p, BNN, hi)
            t = t + _bdot3(t, p, BNN, hi)
        uw = _bdot(t, jnp.concatenate([beta * v_, (beta * e) * k_], axis=2), BNN, hi)
        u_ref[0, 0] = uw[:, :, :HD]
        w_ref[0, 0] = uw[:, :, HD:].astype(w_ref.dtype)
        qg_ref[0, 0] = (e * q_).astype(qg_ref.dtype)
        kd_ref[0, 0] = (jnp.exp(gt - gc) * k_).astype(kd_ref.dtype)
        pm_ref[0, 0] = (_bdot(q_, k_, BNT, hi) * decay).astype(pm_ref.dtype)
        t_ref[0, 0] = t.astype(t_ref.dtype)

    tok, _, gbs, msk, per_chunk = _dn_specs(n_ctx_chunks, n_chunks, False)
    big = lambda dt: jax.ShapeDtypeStruct((2, n_chunks, NH, CH, HD), dt)
    sq = jax.ShapeDtypeStruct((2, n_chunks, NH, CH, CH), BF16)
    return pl.pallas_call(
        body, name="dn_intra_fwd", grid=(2, n_chunks),
        in_specs=[tok, tok, tok, gbs, msk],
        out_specs=[per_chunk(NH, CH, HD)] * 4 + [per_chunk(NH, CH, CH)] * 2,
        out_shape=[big(F32), big(BF16), big(BF16), big(BF16), sq, sq],
        compiler_params=_cparams(("parallel", "parallel")),
    )(q, k, v, gb, masks)


def _dn_seq_fwd(u, w, qg, kd, pm, gb, n_ctx_chunks, hi):
    n_chunks = u.shape[1]
    t_all = n_chunks * CH

    def body(u_ref, w_ref, qg_ref, kd_ref, pm_ref, gb_ref, o_ref, sh_ref, vn_ref, s_scr):
        @pl.when(pl.program_id(1) == 0)
        def _():
            s_scr[...] = jnp.zeros_like(s_scr)

        s = s_scr[...]
        sh_ref[0, 0] = s
        vn = u_ref[0, 0] - _bdot(w_ref[0, 0], s, BNN, hi)
        o = _bdot(qg_ref[0, 0], s, BNN, hi) + _bdot(pm_ref[0, 0], vn, BNN, hi)
        s_scr[...] = jnp.exp(_dn_total(gb_ref[0])) * s + _bdot(kd_ref[0, 0], vn, BTN, hi)
        vn_ref[0, 0] = vn.astype(vn_ref.dtype)
        for h in range(NH):
            o_ref[0, :, h * HD:(h + 1) * HD] = o[h]

    _, tok_d, gbs, _, per_chunk = _dn_specs(n_ctx_chunks, n_chunks, False)
    big = per_chunk(NH, CH, HD)
    return pl.pallas_call(
        body, name="dn_seq_fwd", grid=(2, n_chunks),
        in_specs=[big, big, big, big, per_chunk(NH, CH, CH), gbs],
        out_specs=[tok_d, per_chunk(NH, HD, HD), big],
        out_shape=[jax.ShapeDtypeStruct((2, t_all, D), F32), jax.ShapeDtypeStruct((2, n_chunks, NH, HD, HD), F32),
                   jax.ShapeDtypeStruct((2, n_chunks, NH, CH, HD), BF16)],
        scratch_shapes=[pltpu.VMEM((NH, HD, HD), F32)],
        compiler_params=_cparams(("parallel", "arbitrary")),
    )(u, w, qg, kd, pm, gb)


def _dn_seq_bwd(w, qg, kd, pm, vn, s_hist, gb, do, n_ctx_chunks, hi):
    n_chunks = w.shape[1]

    def body(w_ref, qg_ref, kd_ref, pm_ref, vn_ref, sh_ref, gb_ref, do_ref, dvn_ref, dw_ref, dqg_ref, dkd_ref, del_ref, ds_scr):
        @pl.when(pl.program_id(1) == 0)
        def _():
            ds_scr[...] = jnp.zeros_like(ds_scr)

        dsn = ds_scr[...]
        s = sh_ref[0, 0]
        do_ = _dn_heads(do_ref)
        dvn = _bdot(pm_ref[0, 0], do_, BTN, hi) + _bdot(kd_ref[0, 0], dsn, BNN, hi)
        ds_scr[...] = (_bdot(qg_ref[0, 0], do_, BTN, hi) + jnp.exp(_dn_total(gb_ref[0])) * dsn
                       - _bdot(w_ref[0, 0], dvn, BTN, hi))
        dvn_ref[0, 0] = dvn.astype(dvn_ref.dtype)
        dw_ref[0, 0] = (-_bdot(dvn, s, BNT, hi)).astype(dw_ref.dtype)
        dqg_ref[0, 0] = _bdot(do_, s, BNT, hi)
        dkd_ref[0, 0] = _bdot(vn_ref[0, 0], dsn, BNT, hi)
        del_ref[0, 0] = jnp.broadcast_to(jnp.sum(jnp.sum(s * dsn, axis=2, keepdims=True), axis=1, keepdims=True),
                                         (NH, 1, 128))

    tok, _, gbs, _, per_chunk = _dn_specs(n_ctx_chunks, n_chunks, True)
    big = per_chunk(NH, CH, HD)
    shp = lambda dt: jax.ShapeDtypeStruct((2, n_chunks, NH, CH, HD), dt)
    return pl.pallas_call(
        body, name="dn_seq_bwd", grid=(2, n_chunks),
        in_specs=[big, big, big, per_chunk(NH, CH, CH), big, per_chunk(NH, HD, HD), gbs, tok],
        out_specs=[big, big, big, big, per_chunk(NH, 1, 128)],
        out_shape=[shp(BF16), shp(BF16), shp(F32), shp(F32), jax.ShapeDtypeStruct((2, n_chunks, NH, 1, 128), F32)],
        scratch_shapes=[pltpu.VMEM((NH, HD, HD), F32)],
        compiler_params=_cparams(("parallel", "arbitrary")),
    )(w, qg, kd, pm, vn, s_hist, gb, do)


def _dn_intra_bwd(q, k, v, gb, u, w, t, vn, dvn, dw, dqg, dkd, de_last, do, n_ctx_chunks, hi):
    t_all = q.shape[0]
    n_chunks = t_all // CH
    masks = _dn_masks()

    def body(q_ref, k_ref, v_ref, gb_ref, m_ref, u_ref, w_ref, t_ref, vn_ref, dvn_ref, dw_ref, dqg_ref, dkd_ref, del_ref,
             do_ref, dq_ref, dk_ref, dv_ref, dgb_ref):
        mi, ms = m_ref[0, 0], m_ref[0, 1]
        beta, gc, gcr, gt = _dn_scalars(gb_ref[0], mi)
        q_, k_, v_, do_ = _dn_heads(q_ref), _dn_heads(k_ref), _dn_heads(v_ref), _dn_heads(do_ref)
        decay = jnp.exp(jnp.where(mi > 0, gc - gcr, 0.0)) * mi
        e = jnp.exp(gc)
        e_last = jnp.exp(gt)
        kdfac = jnp.exp(gt - gc)
        kk = _bdot(k_, k_, BNT, hi)
        a = ms * (beta * kk * decay)
        pm = _bdot(q_, k_, BNT, hi) * decay
        kd = kdfac * k_
        dqg, dkd = dqg_ref[0, 0], dkd_ref[0, 0]
        dpm = _bdot(do_, vn_ref[0, 0], BNT, hi)
        dvbkb = _bdot(t_ref[0, 0], jnp.concatenate([dvn_ref[0, 0], dw_ref[0, 0]], axis=2), BTN, hi)
        dvb, dkb = dvbkb[:, :, :HD], dvbkb[:, :, HD:]
        da = -ms * _bdot(dvbkb, jnp.concatenate([u_ref[0, 0], w_ref[0, 0].astype(F32)], axis=2), BNT, hi)
        dqk = dpm * decay
        gm = dpm * pm + da * a
        dgc = (jnp.sum(gm, axis=2, keepdims=True)
               - _bdot3(gm, jnp.ones((NH, CH, 128), F32), BTN, hi)[:, :, 0:1])
        dkk = da * (beta * decay)
        dbeta = jnp.sum(da * kk * decay, axis=2, keepdims=True)
        dk = _bdot(dkk, k_, BNN, hi) + _bdot(dkk, k_, BTN, hi) + _bdot(dqk, q_, BTN, hi)
        dq = _bdot(dqk, k_, BNN, hi) + e * dqg
        de = jnp.sum(dqg * q_, axis=2, keepdims=True)
        dv = beta * dvb
        dbeta = dbeta + jnp.sum(dvb * v_, axis=2, keepdims=True)
        skb = jnp.sum(dkb * k_, axis=2, keepdims=True)
        dk = dk + (beta * e) * dkb + kdfac * dkd
        dbeta = dbeta + e * skb
        de = de + beta * skb
        skd = jnp.sum(dkd * kd, axis=2, keepdims=True)
        dgc = dgc - skd + de * e
        dgtot = jnp.sum(skd, axis=1, keepdims=True) + del_ref[0, 0][:, :, 0:1] * e_last
        lane = lax.broadcasted_iota(jnp.int32, (1, 128), 1)
        dbeta_all = jnp.zeros((CH, 128), F32)
        dgc_all = jnp.zeros((CH, 128), F32)
        dgtot_all = jnp.zeros((1, 128), F32)
        for h in range(NH):
            sl = slice(h * HD, (h + 1) * HD)
            dq_ref[0, :, sl] = dq[h]
            dk_ref[0, :, sl] = dk[h]
            dv_ref[0, :, sl] = dv[h]
            hot_b = (lane == h).astype(F32)
            hot_g = (lane == NH + h).astype(F32)
            dbeta_all = dbeta_all + dbeta[h] * hot_b
            dgc_all = dgc_all + dgc[h] * hot_g
            dgtot_all = dgtot_all + dgtot[h] * hot_g
        dgb_ref[0] = dbeta_all + _dot(mi, dgc_all, TN, True) + dgtot_all

    tok, tok_d, gbs, msk, per_chunk = _dn_specs(n_ctx_chunks, n_chunks, False)
    big = per_chunk(NH, CH, HD)
    return pl.pallas_call(
        body, name="dn_intra_bwd", grid=(2, n_chunks),
        in_specs=[tok, tok, tok, gbs, msk, big, big, per_chunk(NH, CH, CH), big, big, big, big, big,
                  per_chunk(NH, 1, 128), tok],
        out_specs=[tok_d, tok_d, tok_d, gbs],
        out_shape=[jax.ShapeDtypeStruct((2, t_all, D), F32)] * 3 + [jax.ShapeDtypeStruct((2, t_all, 128), F32)],
        compiler_params=_cparams(("parallel", "parallel")),
    )(q, k, v, gb, masks, u, w, t, vn, dvn, dw, dqg, dkd, de_last, do)


ATT_SCALE = HD ** -0.5
NEG = -1e30


def _att_stack(ref, kvh):
    return jnp.concatenate([ref[:, (kvh * GRP + g) * HD:(kvh * GRP + g + 1) * HD] for g in range(GRP)], axis=0)


def _att_col(ref, kvh):
    return jnp.concatenate([ref[:, kvh * GRP + g:kvh * GRP + g + 1] for g in range(GRP)], axis=0)


def _att_sink(sink_ref, kvh):
    return jnp.concatenate([jnp.broadcast_to(sink_ref[:, kvh * GRP + g:kvh * GRP + g + 1], (AB, 1)) for g in range(GRP)],
                           axis=0)


def _att_mask(i, nb):
    r = lax.broadcasted_iota(jnp.int32, (AB, AB), 0)
    c = lax.broadcasted_iota(jnp.int32, (AB, AB), 1)
    okp = jnp.logical_and(c >= r, i > 0)
    okn = jnp.logical_and(c <= r, i < nb - 1)
    m = jnp.concatenate([okp, jnp.ones((AB, AB), jnp.bool_), okn, jnp.ones((AB, CTX), jnp.bool_)], axis=1)
    return jnp.concatenate([m] * GRP, axis=0)


def _att_kspecs(nb):
    nc = CTX // AB
    return [pl.BlockSpec((AB, KVH * HD), lambda i: (jnp.maximum(i - 1, 0) + nc, 0)),
            pl.BlockSpec((AB, KVH * HD), lambda i: (i + nc, 0)),
            pl.BlockSpec((AB, KVH * HD), lambda i: (jnp.minimum(i + 1, nb - 1) + nc, 0)),
            pl.BlockSpec((CTX, KVH * HD), lambda i: (0, 0))]


def _attn_fwd(qr, kr, vv, sink, hi):
    tl = qr.shape[0]
    nb = tl // AB

    def body(q_ref, kp_ref, kc_ref, kn_ref, kx_ref, vp_ref, vc_ref, vn_ref, vx_ref, sink_ref, o_ref, lse_ref):
        i = pl.program_id(0)
        mask = _att_mask(i, nb)
        lane = lax.broadcasted_iota(jnp.int32, (1, 128), 1)
        lse_all = jnp.zeros((AB, 128), F32)
        for kvh in range(KVH):
            ksl = slice(kvh * HD, (kvh + 1) * HD)
            kall = jnp.concatenate([kp_ref[:, ksl], kc_ref[:, ksl], kn_ref[:, ksl], kx_ref[:, ksl]], axis=0)
            vall = jnp.concatenate([vp_ref[:, ksl], vc_ref[:, ksl], vn_ref[:, ksl], vx_ref[:, ksl]], axis=0)
            s = _dot(_att_stack(q_ref, kvh), kall, NT, hi) * ATT_SCALE
            s = jnp.where(mask, s, NEG)
            sk = _att_sink(sink_ref, kvh)
            m = jnp.maximum(jnp.max(s, axis=1, keepdims=True), sk)
            p = jnp.exp(s - m)
            l = jnp.sum(p, axis=1, keepdims=True) + jnp.exp(sk - m)
            o = _dot(p, vall, NN, hi) / l
            lse = m + jnp.log(l)
            for g in range(GRP):
                h = kvh * GRP + g
                o_ref[:, h * HD:(h + 1) * HD] = o[g * AB:(g + 1) * AB]
                lse_all = lse_all + lse[g * AB:(g + 1) * AB] * (lane == h).astype(F32)
        lse_ref[...] = lse_all

    ks = _att_kspecs(nb)
    return pl.pallas_call(
        body, name="attn_fwd", grid=(nb,),
        in_specs=[pl.BlockSpec((AB, D), lambda i: (i, 0))] + ks + ks + [pl.BlockSpec((1, 128), lambda i: (0, 0))],
        out_specs=[pl.BlockSpec((AB, D), lambda i: (i, 0)), pl.BlockSpec((AB, 128), lambda i: (i, 0))],
        out_shape=[jax.ShapeDtypeStruct((tl, D), F32), jax.ShapeDtypeStruct((tl, 128), F32)],
        compiler_params=_cparams(("parallel",)),
    )(qr, kr, kr, kr, kr, vv, vv, vv, vv, sink)


def _attn_delta(o, do):
    tl = o.shape[0]
    tr = min(512, tl)

    def body(o_ref, do_ref, d_ref):
        lane = lax.broadcasted_iota(jnp.int32, (1, 128), 1)
        acc = jnp.zeros((tr, 128), F32)
        for h in range(NH):
            sl = slice(h * HD, (h + 1) * HD)
            acc = acc + jnp.sum(o_ref[:, sl] * do_ref[:, sl], axis=1, keepdims=True) * (lane == h).astype(F32)
        d_ref[...] = acc

    return pl.pallas_call(
        body, name="attn_delta", grid=(tl // tr,),
        in_specs=[pl.BlockSpec((tr, D), lambda i: (i, 0))] * 2,
        out_specs=pl.BlockSpec((tr, 128), lambda i: (i, 0)),
        out_shape=jax.ShapeDtypeStruct((tl, 128), F32),
        compiler_params=_cparams(("parallel",)),
    )(o, do)


def _attn_bwd_q(qr, kr, vv, sink, do, lse, delta, hi):
    tl = qr.shape[0]
    nb = tl // AB

    def body(q_ref, kp_ref, kc_ref, kn_ref, kx_ref, vp_ref, vc_ref, vn_ref, vx_ref, sink_ref, do_ref, lse_ref, dl_ref,
             dq_ref, dkx_ref, dvx_ref, dsink_ref):
        i = pl.program_id(0)

        @pl.when(i == 0)
        def _():
            dkx_ref[...] = jnp.zeros_like(dkx_ref)
            dvx_ref[...] = jnp.zeros_like(dvx_ref)
            dsink_ref[...] = jnp.zeros_like(dsink_ref)

        mask = _att_mask(i, nb)
        lane = lax.broadcasted_iota(jnp.int32, (1, 128), 1)
        dsink = jnp.zeros((1, 128), F32)
        for kvh in range(KVH):
            ksl = slice(kvh * HD, (kvh + 1) * HD)
            kall = jnp.concatenate([kp_ref[:, ksl], kc_ref[:, ksl], kn_ref[:, ksl], kx_ref[:, ksl]], axis=0)
            vall = jnp.concatenate([vp_ref[:, ksl], vc_ref[:, ksl], vn_ref[:, ksl], vx_ref[:, ksl]], axis=0)
            qs = _att_stack(q_ref, kvh)
            dos = _att_stack(do_ref, kvh)
            lse_s = _att_col(lse_ref, kvh)
            dl_s = _att_col(dl_ref, kvh)
            s = _dot(qs, kall, NT, hi) * ATT_SCALE
            p = jnp.where(mask, jnp.exp(jnp.where(mask, s, NEG) - lse_s), 0.0)
            dp = _dot(dos, vall, NT, hi)
            ds = p * (dp - dl_s)
            dq = _dot(ds, kall, NN, hi) * ATT_SCALE
            dkx_ref[:, ksl] += _dot(ds[:, 3 * AB:], qs, TN, hi) * ATT_SCALE
            dvx_ref[:, ksl] += _dot(p[:, 3 * AB:], dos, TN, hi)
            psink = jnp.exp(_att_sink(sink_ref, kvh) - lse_s) * dl_s
            for g in range(GRP):
                h = kvh * GRP + g
                dq_ref[:, h * HD:(h + 1) * HD] = dq[g * AB:(g + 1) * AB]
                dsink = dsink - jnp.sum(psink[g * AB:(g + 1) * AB], axis=0, keepdims=True) * (lane == h).astype(F32)
        dsink_ref[...] += dsink

    ks = _att_kspecs(nb)
    row = pl.BlockSpec((AB, D), lambda i: (i, 0))
    col = pl.BlockSpec((AB, 128), lambda i: (i, 0))
    return pl.pallas_call(
        body, name="attn_bwd_q", grid=(nb,),
        in_specs=[row] + ks + ks + [pl.BlockSpec((1, 128), lambda i: (0, 0)), row, col, col],
        out_specs=[row, pl.BlockSpec((CTX, KVH * HD), lambda i: (0, 0)), pl.BlockSpec((CTX, KVH * HD), lambda i: (0, 0)),
                   pl.BlockSpec((1, 128), lambda i: (0, 0))],
        out_shape=[jax.ShapeDtypeStruct((tl, D), F32), jax.ShapeDtypeStruct((CTX, KVH * HD), F32),
                   jax.ShapeDtypeStruct((CTX, KVH * HD), F32), jax.ShapeDtypeStruct((1, 128), F32)],
        compiler_params=_cparams(("arbitrary",)),
    )(qr, kr, kr, kr, kr, vv, vv, vv, vv, sink, do, lse, delta)


def _attn_bwd_kv(qr, kr, vv, do, lse, delta, hi):
    tl = qr.shape[0]
    nb = tl // AB
    nc = CTX // AB

    def body(k_ref, v_ref, *refs):
        qs_refs, do_refs, lse_refs, dl_refs = refs[0:3], refs[3:6], refs[6:9], refs[9:12]
        dk_ref, dv_ref = refs[12], refs[13]
        j = pl.program_id(0)
        r = lax.broadcasted_iota(jnp.int32, (AB, AB), 0)
        c = lax.broadcasted_iota(jnp.int32, (AB, AB), 1)
        one = jnp.ones((AB, AB), jnp.bool_)
        masks = [jnp.logical_and(c <= r, j > 0), one, jnp.logical_and(c >= r, j < nb - 1)]
        for kvh in range(KVH):
            ksl = slice(kvh * HD, (kvh + 1) * HD)
            k_, v_ = k_ref[:, ksl], v_ref[:, ksl]
            dk = jnp.zeros((AB, HD), F32)
            dv = jnp.zeros((AB, HD), F32)
            for t in range(3):
                mask = jnp.concatenate([masks[t]] * GRP, axis=0)
                qs = _att_stack(qs_refs[t], kvh)
                dos = _att_stack(do_refs[t], kvh)
                lse_s = _att_col(lse_refs[t], kvh)
                dl_s = _att_col(dl_refs[t], kvh)
                s = _dot(qs, k_, NT, hi) * ATT_SCALE
                p = jnp.where(mask, jnp.exp(jnp.where(mask, s, NEG) - lse_s), 0.0)
                dp = _dot(dos, v_, NT, hi)
                ds = p * (dp - dl_s)
                dv = dv + _dot(p, dos, TN, hi)
                dk = dk + _dot(ds, qs, TN, hi) * ATT_SCALE
            dk_ref[:, ksl] = dk
            dv_ref[:, ksl] = dv

    def three(width):
        return [pl.BlockSpec((AB, width), lambda j: (jnp.maximum(j - 1, 0), 0)),
                pl.BlockSpec((AB, width), lambda j: (j, 0)),
                pl.BlockSpec((AB, width), lambda j: (jnp.minimum(j + 1, nb - 1), 0))]

    kv = pl.BlockSpec((AB, KVH * HD), lambda j: (j + nc, 0))
    out = pl.BlockSpec((AB, KVH * HD), lambda j: (j, 0))
    return pl.pallas_call(
        body, name="attn_bwd_kv", grid=(nb,),
        in_specs=[kv, kv] + three(D) + three(D) + three(128) + three(128),
        out_specs=[out, out],
        out_shape=[jax.ShapeDtypeStruct((tl, KVH * HD), F32)] * 2,
        compiler_params=_cparams(("parallel",)),
    )(kr, vv, qr, qr, qr, do, do, do, lse, lse, lse, delta, delta, delta)


def _mm(a, b, ta=False, tb=False, out_dtype=F32, tm=512, tn=1024, tk=1024, name="mm", hi=False):
    m, kd = (a.shape[1], a.shape[0]) if ta else a.shape
    n = b.shape[0] if tb else b.shape[1]
    tm, tn, tk = min(tm, m), min(tn, n), min(tk, kd)
    assert m % tm == 0 and n % tn == 0 and kd % tk == 0, (name, m, n, kd, tm, tn, tk)
    nk = kd // tk
    dims = ((0,) if ta else (1,), (1,) if tb else (0,))

    def body(a_ref, b_ref, o_ref, *scr):
        part = _dot(a_ref[...], b_ref[...], dims, hi)
        if nk == 1:
            o_ref[...] = part.astype(out_dtype)
        else:
            acc = scr[0]
            kk = pl.program_id(2)

            @pl.when(kk == 0)
            def _():
                acc[...] = part

            @pl.when(kk > 0)
            def _():
                acc[...] += part

            @pl.when(kk == nk - 1)
            def _():
                o_ref[...] = acc[...].astype(out_dtype)

    a_spec = pl.BlockSpec((tk, tm), lambda i, j, k: (k, i)) if ta else pl.BlockSpec((tm, tk), lambda i, j, k: (i, k))
    b_spec = pl.BlockSpec((tn, tk), lambda i, j, k: (j, k)) if tb else pl.BlockSpec((tk, tn), lambda i, j, k: (k, j))
    return pl.pallas_call(
        body, name=name, grid=(m // tm, n // tn, nk),
        in_specs=[a_spec, b_spec],
        out_specs=pl.BlockSpec((tm, tn), lambda i, j, k: (i, j)),
        out_shape=jax.ShapeDtypeStruct((m, n), out_dtype),
        scratch_shapes=[] if nk == 1 else [pltpu.VMEM((tm, tn), F32)],
        compiler_params=_cparams(("parallel", "parallel", "arbitrary")),
    )(a, b)


HALO = 8


class _In:
    def __init__(self, arr, w=None, cb=0, roff=0, halo=None):
        self.arr, self.w, self.cb, self.roff, self.halo = arr, w or arr.shape[1], cb, roff, halo


class _Full:
    def __init__(self, arr, w=None, cb=0):
        self.arr, self.w, self.cb = arr, w, cb


class _Out:
    def __init__(self, cols, dtype=F32, w=None, cb=0, acc=False, rows=1, roff=0, nrows=None):
        self.cols, self.dtype, self.w, self.cb, self.acc, self.rows, self.roff, self.nrows = (
            cols, dtype, w or cols, cb, acc, rows, roff, nrows)


def _rowcall(name, fn, nrow_tiles, tile, ins, outs, ncol=1):
    arrays, specs, kinds = [], [], []
    for x in ins:
        if isinstance(x, _Full):
            arrays.append(x.arr)
            if x.w is None:
                specs.append(pl.BlockSpec(x.arr.shape, lambda j, i: (0, 0)))
            else:
                specs.append(pl.BlockSpec((x.arr.shape[0], x.w), lambda j, i, cb=x.cb: (0, cb + j)))
            kinds.append("full")
            continue
        w, cb, roff = x.w, x.cb, x.roff
        cur = pl.BlockSpec((tile, w), lambda j, i, cb=cb, roff=roff: (i + roff, cb + j))
        if x.halo is None:
            arrays.append(x.arr)
            specs.append(cur)
            kinds.append("tile")
        else:
            r8 = tile // HALO
            last = x.arr.shape[0] // HALO - 1
            prev = pl.BlockSpec((HALO, w), lambda j, i, cb=cb, roff=roff, r8=r8: (jnp.maximum((i + roff) * r8 - 1, 0), cb + j))
            nxt = pl.BlockSpec((HALO, w), lambda j, i, cb=cb, roff=roff, r8=r8, last=last:
                               (jnp.minimum((i + roff + 1) * r8, last), cb + j))
            arrays += [x.arr, x.arr, x.arr]
            specs += [prev, cur, nxt]
            kinds.append(("halo", x.halo))
    out_specs, out_shapes = [], []
    for o in outs:
        if o.acc:
            out_specs.append(pl.BlockSpec((o.rows, o.w), lambda j, i, cb=o.cb: (0, cb + j)))
            out_shapes.append(jax.ShapeDtypeStruct((o.rows, o.cols), o.dtype))
        else:
            out_specs.append(pl.BlockSpec((tile, o.w), lambda j, i, cb=o.cb, roff=o.roff: (i + roff, cb + j)))
            out_shapes.append(jax.ShapeDtypeStruct(((o.nrows or nrow_tiles * tile), o.cols), o.dtype))
    n_in = len(arrays)

    def body(*refs):
        j = pl.program_id(0)
        i = pl.program_id(1)
        vals, r = [], 0
        for kind in kinds:
            if kind in ("full", "tile"):
                vals.append(refs[r][...])
                r += 1
            else:
                pok, nok = kind[1]
                p, c, n = refs[r][...], refs[r + 1][...], refs[r + 2][...]
                p = jnp.where(pok(i), p, jnp.zeros_like(p))
                n = jnp.where(nok(i), n, jnp.zeros_like(n))
                vals.append(jnp.concatenate([p, c, n], axis=0))
                r += 3
        res = fn(i, j, *vals)
        for o, ref, val in zip(outs, refs[n_in:], res):
            if o.acc:
                @pl.when(i == 0)
                def _(ref=ref, val=val, o=o):
                    ref[...] = val.astype(o.dtype)

                @pl.when(i > 0)
                def _(ref=ref, val=val, o=o):
                    ref[...] += val.astype(o.dtype)
            else:
                ref[...] = val.astype(o.dtype)

    return pl.pallas_call(
        body, name=name, grid=(ncol, nrow_tiles), in_specs=specs, out_specs=out_specs, out_shape=out_shapes,
        compiler_params=_cparams(("parallel", "arbitrary")),
    )(*arrays)


def _shift(xe, s, tile):
    if s == 0:
        return xe[HALO:HALO + tile]
    return pltpu.roll(xe, (-s) % xe.shape[0], 0)[HALO:HALO + tile]


def _silu(x):
    return x * jax.nn.sigmoid(x)


def _dsilu(x):
    s = jax.nn.sigmoid(x)
    return s * (1.0 + x * (1.0 - s))


def _heads(x, fn):
    return jnp.concatenate([fn(h, x[:, h * HD:(h + 1) * HD]) for h in range(x.shape[1] // HD)], axis=1)


def _colsum(x):
    return jnp.sum(x, axis=0, keepdims=True)


def _rowmean(x):
    return jnp.mean(x, axis=1, keepdims=True)


def _rowsum(x):
    return jnp.sum(x, axis=1, keepdims=True)


TILE = 256
CT = CTX // TILE


def _all_halo(n_tiles):
    return (lambda i: i >= CT + 1, lambda i: jnp.logical_and(i >= CT, i < n_tiles - 1))


def _lat_halo(n_tiles):
    return (lambda i: i >= 1, lambda i: i < n_tiles - 1)


def _rms_mod(x, nm, shift, scale):
    r = lax.rsqrt(_rowmean(x * x) + EPS)
    return (x * r * nm) * (1.0 + scale) + shift


def _rms_mod_bwd(dh, x, nm, scale):
    r = lax.rsqrt(_rowmean(x * x) + EPS)
    xn = x * r
    dz = dh * (1.0 + scale)
    dxn = dz * nm
    dx = r * (dxn - xn * _rowmean(dxn * xn))
    return dx, _colsum(dz * xn), _colsum(dh), _colsum(dh * (xn * nm))


def _norm_mod(xa, nm, mod_c, mod_x):
    n = xa.shape[0] // TILE

    def fn(i, j, x, nm_, mc, mx):
        m = jnp.where(i < CT, mc, mx)
        return (_rms_mod(x, nm_, m[0:1], m[1:2]),)

    return _rowcall("norm_mod", fn, n, TILE, [_In(xa), _Full(nm), _Full(mod_c), _Full(mod_x)], [_Out(D, BF16)])[0]


def _norm_mod_bwd(dh, xa, dres, nm, mod, roff, n):
    ins = [_In(dh, roff=roff), _In(xa, roff=roff), _Full(nm), _Full(mod)] + ([] if dres is None else [_In(dres)])

    def fn(i, j, dh_, x, nm_, m, *rest):
        dx, dn, dsh, dsc = _rms_mod_bwd(dh_, x, nm_, m[1:2])
        if rest:
            return (dx + rest[0], dn, dsh, dsc)
        return (dn, dsh, dsc)

    accs = [_Out(D, acc=True), _Out(D, acc=True), _Out(D, acc=True)]
    return _rowcall("norm_mod_bwd", fn, n, TILE, ins, ([] if dres is None else [_Out(D)]) + accs)


DN_Q_SCALE = HD ** -0.5


def _conv_taps(xe, w, width, rows=None):
    r = width // 2
    acc = None
    for t in range(width):
        s = t - r
        if rows is None:
            sh = xe if s == 0 else pltpu.roll(xe, (-s) % xe.shape[0], 0)
        else:
            sh = _shift(xe, s, rows)
        term = sh * w[t:t + 1]
        acc = term if acc is None else acc + term
    return acc


def _l2n(x, scale):
    rn = lax.rsqrt(_rowsum(x * x) + EPS)
    return x * (rn * scale)


def _l2n_bwd(dy, x, scale):
    rn = lax.rsqrt(_rowsum(x * x) + EPS)
    xu = x * rn
    return (scale * rn) * (dy - xu * _rowsum(dy * xu))


def _softplus(x):
    return jnp.maximum(x, 0.0) + jnp.log(1.0 + jnp.exp(-jnp.abs(x)))


def _lane_mask(lo, hi_):
    lane = lax.broadcasted_iota(jnp.int32, (1, 128), 1)
    return jnp.logical_and(lane >= lo, lane < hi_).astype(F32)


def _dn_prep(p, conv_w, gprm):
    n = p.shape[0] // TILE
    halo = _all_halo(n)

    def fn(i, j, qe, ke, ve, ba, w, gp):
        cq = _conv_taps(qe, w[:, 0:D], 5, TILE)
        ck = _conv_taps(ke, w[:, D:2 * D], 5, TILE)
        cv = _conv_taps(ve, w[:, 2 * D:3 * D], 5, TILE)
        q = _heads(_silu(cq), lambda h, x: _l2n(x, DN_Q_SCALE))
        k = _heads(_silu(ck), lambda h, x: _l2n(x, 1.0))
        v = _silu(cv)
        beta = jax.nn.sigmoid(ba)
        g = -jnp.exp(gp[0:1]) * _softplus(ba + gp[1:2])
        m0, m1 = _lane_mask(0, 8), _lane_mask(8, 16)
        gb_f = beta * m0 + pltpu.roll(g, 128 - 8, 1) * m1
        gb_b = pltpu.roll(beta, 128 - 8, 1) * m0 + pltpu.roll(g, 128 - 16, 1) * m1
        return q, k, v, gb_f, gb_b

    ins = [_In(p, D, 0, halo=halo), _In(p, D, 1, halo=halo), _In(p, D, 2, halo=halo), _In(p, 128, C_BA // 128),
           _Full(conv_w), _Full(gprm)]
    return _rowcall("dn_prep", fn, n, TILE, ins, [_Out(D), _Out(D), _Out(D), _Out(128), _Out(128)])


def _dn_prep_bwd(p, conv_w, gprm, dq2, dk2, dv2, dgb2):
    n = p.shape[0] // TILE
    halo = _all_halo(n)

    def branch(xe, w, dye, scale):
        c = _conv_taps(xe, w, 5)
        sx = _silu(c)
        if scale is None:
            dsx = dye
        else:
            dsx = jnp.concatenate([_l2n_bwd(dye[:, h * HD:(h + 1) * HD], sx[:, h * HD:(h + 1) * HD], scale)
                                   for h in range(NH)], axis=1)
        dc = dsx * _dsilu(c)
        dx = None
        dws = []
        dcc = dc[HALO:HALO + TILE]
        for t in range(5):
            term = _shift(dc, 2 - t, TILE) * w[t:t + 1]
            dx = term if dx is None else dx + term
            dws.append(_colsum(dcc * _shift(xe, t - 2, TILE)))
        dw = jnp.concatenate(dws + [jnp.zeros((3, D), F32)], axis=0)
        return dx, dw

    def fn(i, j, qe, ke, ve, ba, w, gp, dq0, dq1, dk0, dk1, dv0, dv1, dg0, dg1):
        dxq, dwq = branch(qe, w[:, 0:D], dq0 + dq1, DN_Q_SCALE)
        dxk, dwk = branch(ke, w[:, D:2 * D], dk0 + dk1, 1.0)
        dxv, dwv = branch(ve, w[:, 2 * D:3 * D], dv0 + dv1, None)
        m0, m1 = _lane_mask(0, 8), _lane_mask(8, 16)
        dbeta = dg0 * m0 + pltpu.roll(dg1 * m0, 8, 1)
        dg = pltpu.roll(dg0 * m1, 8, 1) + pltpu.roll(dg1 * m1, 16, 1)
        beta = jax.nn.sigmoid(ba)
        ea = jnp.exp(gp[0:1])
        z = ba + gp[1:2]
        g = -ea * _softplus(z)
        mg = _lane_mask(16, 32)
        da = dg * (-ea) * jax.nn.sigmoid(z) * mg
        dba = dbeta * beta * (1.0 - beta) * _lane_mask(0, 16) + da
        dgp = jnp.concatenate([_colsum(dg * g * mg), _colsum(da)], axis=0)
        return (jnp.concatenate([dxq, dxk, dxv], axis=1), dba, jnp.concatenate([dwq, dwk, dwv], axis=1), dgp)

    ins = [_In(p, D, 0, halo=halo), _In(p, D, 1, halo=halo), _In(p, D, 2, halo=halo), _In(p, 128, C_BA // 128),
           _Full(conv_w), _Full(gprm),
           _In(dq2, halo=halo), _In(dq2, roff=n, halo=halo), _In(dk2, halo=halo), _In(dk2, roff=n, halo=halo),
           _In(dv2, halo=halo), _In(dv2, roff=n, halo=halo), _In(dgb2), _In(dgb2, roff=n)]
    return _rowcall("dn_prep_bwd", fn, n, TILE, ins,
                    [_Out(3 * D, BF16), _Out(128, BF16), _Out(3 * D, acc=True, rows=8), _Out(128, acc=True, rows=2)])


def _hnorm(x, w):
    return x * lax.rsqrt(_rowmean(x * x) + EPS) * w


def _hnorm_bwd(dy, x, w):
    r = lax.rsqrt(_rowmean(x * x) + EPS)
    xh = x * r
    dxh = dy * w
    return r * (dxh - xh * _rowmean(dxh * xh)), _colsum(dy * xh)


def _dn_gate(o2, p, dn_norm, n_all):
    n = n_all - CT

    def fn(i, j, of, ob, gt, w):
        o = of + ob
        return (_heads(o, lambda h, x: _hnorm(x, w)) * _silu(gt),)

    ins = [_In(o2, roff=CT), _In(o2, roff=n_all + CT), _In(p, D, C_GT // D, roff=CT), _Full(dn_norm)]
    return _rowcall("dn_gate", fn, n, TILE, ins, [_Out(D, BF16)])[0]


def _dn_gate_bwd(dy, o2, p, dn_norm, n_all):
    n = n_all - CT

    def fn(i, j, dy_, of, ob, gt, w):
        o = of + ob
        sg = _silu(gt)
        dos, dw = [], jnp.zeros((1, HD), F32)
        yn = []
        for h in range(NH):
            sl = slice(h * HD, (h + 1) * HD)
            dx, dwh = _hnorm_bwd(dy_[:, sl] * sg[:, sl], o[:, sl], w)
            dos.append(dx)
            dw = dw + dwh
            yn.append(_hnorm(o[:, sl], w))
        dgt = dy_ * jnp.concatenate(yn, axis=1) * _dsilu(gt)
        return jnp.concatenate(dos, axis=1), dgt, dw

    ins = [_In(dy), _In(o2, roff=CT), _In(o2, roff=n_all + CT), _In(p, D, C_GT // D, roff=CT), _Full(dn_norm)]
    return _rowcall("dn_gate_bwd", fn, n, TILE, ins, [_Out(D), _Out(D, BF16), _Out(HD, acc=True)])


def _rope_shuffle(x):
    lane = lax.broadcasted_iota(jnp.int32, (1, HD), 1)
    return jnp.where((lane % 64) < 32, pltpu.roll(x, HD - 32, 1), pltpu.roll(x, 32, 1))


def _rope(x, cos, sin):
    return x * cos + _rope_shuffle(x) * sin


def _rope_bwd(dy, cos, sin):
    return dy * cos + _rope_shuffle(dy * sin)


def _attn_prep(p, w, cos, sin, width, cb, roff, n, name):
    def fn(i, j, x, w_, c, s):
        return (_heads(x, lambda h, xh: _rope(_hnorm(xh, w_), c, s)),)

    ins = [_In(p, width, cb, roff=roff), _Full(w), _In(cos), _In(sin)]
    return _rowcall(name, fn, n, TILE, ins, [_Out(width)])[0]


def _attn_prep_bwd(dy, p, w, cos, sin, width, cb, roff, n, name):
    def fn(i, j, dy_, x, w_, c, s):
        dxs, dw = [], jnp.zeros((1, HD), F32)
        for h in range(width // HD):
            sl = slice(h * HD, (h + 1) * HD)
            dx, dwh = _hnorm_bwd(_rope_bwd(dy_[:, sl], c, s), x[:, sl], w_)
            dxs.append(dx)
            dw = dw + dwh
        return jnp.concatenate(dxs, axis=1), dw

    ins = [_In(dy), _In(p, width, cb, roff=roff), _Full(w), _In(cos), _In(sin)]
    return _rowcall(name, fn, n, TILE, ins, [_Out(width, BF16), _Out(HD, acc=True)])


def _merge(z_dn, z_at, p, n):
    def fn(i, j, zd, za, gd, ga):
        return (jax.nn.sigmoid(gd) * zd + jax.nn.sigmoid(ga) * za,)

    ins = [_In(z_dn), _In(z_at), _In(p, D, C_MG // D, roff=CT), _In(p, D, C_MG // D + 1, roff=CT)]
    return _rowcall("merge", fn, n, TILE, ins, [_Out(D, BF16)])[0]


def _merge_bwd(dm, z_dn, z_at, p, n):
    def fn(i, j, dm_, zd, za, gd, ga):
        sd, sa = jax.nn.sigmoid(gd), jax.nn.sigmoid(ga)
        dg = jnp.concatenate([dm_ * zd * sd * (1.0 - sd), dm_ * za * sa * (1.0 - sa)], axis=1)
        return dm_ * sd, dm_ * sa, dg

    ins = [_In(dm), _In(z_dn), _In(z_at), _In(p, D, C_MG // D, roff=CT), _In(p, D, C_MG // D + 1, roff=CT)]
    return _rowcall("merge_bwd", fn, n, TILE, ins, [_Out(D, BF16), _Out(D, BF16), _Out(2 * D, BF16)])


def _resid_norm(xa, mo, g_a, nf, mod_f, n):
    def fn(i, j, x, mo_, ga, nf_, m):
        x1 = x + ga * mo_
        return x1, _rms_mod(x1, nf_, m[0:1], m[1:2])

    ins = [_In(xa, roff=CT), _In(mo), _Full(g_a), _Full(nf), _Full(mod_f)]
    return _rowcall("resid_norm", fn, n, TILE, ins, [_Out(D), _Out(D, BF16)])


def _resid_norm_bwd(dy, dh2, x1, mo, g_a, nf, mod_f, n):
    def fn(i, j, dy_, dh_, x1_, mo_, ga, nf_, m):
        dx, dn, dsh, dsc = _rms_mod_bwd(dh_, x1_, nf_, m[1:2])
        dx1 = dy_ + dx
        return dx1, ga * dx1, dn, dsh, dsc, _colsum(dx1 * mo_)

    ins = [_In(dy), _In(dh2), _In(x1), _In(mo), _Full(g_a), _Full(nf), _Full(mod_f)]
    accs = [_Out(D, acc=True) for _ in range(4)]
    return _rowcall("resid_norm_bwd", fn, n, TILE, ins, [_Out(D), _Out(D, BF16)] + accs)


def _loss_head(x1, f, tgt, g_f, n):
    def fn(i, j, x1_, f_, t, gf):
        e = x1_ + gf * f_ - t
        dy = e * (1.0 / D)
        loss = _colsum(_rowsum(e * e)) * (0.5 / D)
        return dy, gf * dy, _colsum(dy * f_), jnp.broadcast_to(loss, (1, 128))

    ins = [_In(x1), _In(f), _In(tgt), _Full(g_f)]
    return _rowcall("loss_head", fn, n, TILE, ins, [_Out(D), _Out(D, BF16), _Out(D, acc=True), _Out(128, acc=True)])


FW = DFF // 2


def _ffn_act(u, conv_w, conv_b, n):
    halo = _lat_halo(n)

    def fn(i, j, ge, ve, wg, wv, bg, bv):
        cg = _conv_taps(ge, wg, 3, TILE) + bg
        cv = _conv_taps(ve, wv, 3, TILE) + bv
        return (_silu(cg) * cv,)

    ins = [_In(u, FW, 0, halo=halo), _In(u, FW, 2, halo=halo), _Full(conv_w, FW, 0), _Full(conv_w, FW, 2),
           _Full(conv_b, FW, 0), _Full(conv_b, FW, 2)]
    return _rowcall("ffn_act", fn, n, TILE, ins, [_Out(DFF, BF16, FW)], ncol=2)[0]


def _ffn_act_bwd(u, da, conv_w, conv_b, n):
    halo = _lat_halo(n)

    def fn(i, j, ge, ve, dae, wg, wv, bg, bv):
        cg = _conv_taps(ge, wg, 3) + bg
        cv = _conv_taps(ve, wv, 3) + bv
        dcg = dae * cv * _dsilu(cg)
        dcv = dae * _silu(cg)
        outs = []
        for dc, xe, w in ((dcg, ge, wg), (dcv, ve, wv)):
            dx, dws = None, []
            dcc = dc[HALO:HALO + TILE]
            for t in range(3):
                term = _shift(dc, 1 - t, TILE) * w[t:t + 1]
                dx = term if dx is None else dx + term
                dws.append(_colsum(dcc * _shift(xe, t - 1, TILE)))
            outs.append((dx, jnp.concatenate(dws + [jnp.zeros((5, FW), F32)], axis=0), _colsum(dcc)))
        return outs[0][0], outs[1][0], outs[0][1], outs[1][1], outs[0][2], outs[1][2]

    ins = [_In(u, FW, 0, halo=halo), _In(u, FW, 2, halo=halo), _In(da, FW, 0, halo=halo),
           _Full(conv_w, FW, 0), _Full(conv_w, FW, 2), _Full(conv_b, FW, 0), _Full(conv_b, FW, 2)]
    outs = [_Out(DFF, BF16, FW), _Out(DFF, BF16, FW), _Out(DFF, w=FW, acc=True, rows=8), _Out(DFF, w=FW, acc=True, rows=8),
            _Out(DFF, w=FW, acc=True), _Out(DFF, w=FW, acc=True)]
    return _rowcall("ffn_act_bwd", fn, n, TILE, ins, outs, ncol=2)


def _rope_tables(tl):
    t = jnp.arange(tl, dtype=jnp.int32)
    row = (t // GRID_W).astype(F32)
    col = (t % GRID_W).astype(F32)
    inv = ROPE_BASE ** (-jnp.arange(32, dtype=F32) / 32)
    ar, ac = row[:, None] * inv, col[:, None] * inv
    cos = jnp.concatenate([jnp.cos(ar), jnp.cos(ar), jnp.cos(ac), jnp.cos(ac)], axis=1)
    sin = jnp.concatenate([-jnp.sin(ar), jnp.sin(ar), -jnp.sin(ac), jnp.sin(ac)], axis=1)
    return cos, sin


def _pad_w_in(w_in):
    z = lambda n: jnp.zeros((D, n), w_in.dtype)
    return jnp.concatenate([w_in[:, 0:4096], w_in[:, 4128:5152], w_in[:, 5664:7712], w_in[:, 5152:5664],
                            w_in[:, 4096:4128], z(96 + PW - C_PAD)], axis=1)


def _unpad_w_in(g):
    return jnp.concatenate([g[:, 0:4096], g[:, C_BA:C_BA + 32], g[:, C_QAT:C_QAT + D], g[:, C_KAT:C_KAT + 512],
                            g[:, C_MG:C_MG + 2 * D]], axis=1)


def _local_step(xa, tgt, mod_x, mod_c, w, hi=False):
    t_all = xa.shape[0]
    tl = t_all - CTX
    n_all, n = t_all // TILE, tl // TILE
    tm_all = 640 if t_all % 640 == 0 else TILE
    tm_lat = 512
    mm = functools.partial(_mm, hi=hi)
    sp = lambda m: [m[:, k * D:(k + 1) * D] for k in range(6)]
    sh_a, sc_a, g_a, sh_f, sc_f, g_f = sp(mod_x)
    sh_ac, sc_ac = sp(mod_c)[:2]
    mod_ax = jnp.concatenate([sh_a, sc_a], axis=0)
    mod_ac = jnp.concatenate([sh_ac, sc_ac], axis=0)
    mod_f = jnp.concatenate([sh_f, sc_f], axis=0)
    nm, nf = w["norm_mix"], w["norm_ffn"]
    cos, sin = _rope_tables(tl)
    cos_all = jnp.concatenate([jnp.ones((CTX, HD), F32), cos], axis=0)
    sin_all = jnp.concatenate([jnp.zeros((CTX, HD), F32), sin], axis=0)
    conv_dn = jnp.concatenate([w["dn_conv"], jnp.zeros((3, 3 * D), F32)], axis=0)
    gprm = jnp.concatenate([jnp.zeros((2, 16), F32),
                            jnp.concatenate([w["dn_a_log"].reshape(1, 16), w["dn_dt_bias"].reshape(1, 16)], axis=0),
                            jnp.zeros((2, 96), F32)], axis=1)
    conv_ff = jnp.concatenate([w["ffn_conv"], jnp.zeros((5, 2 * DFF), F32)], axis=0)
    sink = jnp.concatenate([w["attn_sink"].reshape(1, NH), jnp.zeros((1, 128 - NH), F32)], axis=1)
    nct = CTX // CH

    h = _norm_mod(xa, nm, mod_ac, mod_ax)
    p = mm(h, w["w_in_p"], tm=tm_all, tn=1024, name="mm_in")
    q, k, v, gb_f, gb_b = _dn_prep(p, conv_dn, gprm)
    gb = jnp.stack([gb_f, gb_b])
    dn_u, dn_w, dn_qg, dn_kd, dn_pm, dn_t = _dn_intra_fwd(q, k, v, gb, nct, hi)
    o2, s_hist, dn_vn = _dn_seq_fwd(dn_u, dn_w, dn_qg, dn_kd, dn_pm, gb, nct, hi)
    o2 = o2.reshape(2 * t_all, D)
    y_dn = _dn_gate(o2, p, w["dn_norm"], n_all)
    qr = _attn_prep(p, w["q_norm"], cos, sin, D, C_QAT // D, CT, n, "attn_prep_q")
    kr = _attn_prep(p, w["k_norm"], cos_all, sin_all, KVH * HD, C_KAT // (KVH * HD), 0, n_all, "attn_prep_k")
    vv = p[:, C_VAT:C_VAT + KVH * HD]
    o_at, lse = _attn_fwd(qr, kr, vv, sink, hi)
    z_dn = mm(y_dn, w["w_branch_dn"], tm=tm_lat, name="mm_bdn")
    z_at = mm(o_at, w["w_branch_attn"], tm=tm_lat, name="mm_bat")
    merged = _merge(z_dn, z_at, p, n)
    mo = mm(merged, w["w_out"], tm=tm_lat, name="mm_out")
    x1, h2 = _resid_norm(xa, mo, g_a, nf, mod_f, n)
    u = mm(h2, w["ffn_up"], tm=tm_lat, tn=1408, name="mm_up")
    a = _ffn_act(u, conv_ff, w["ffn_conv_b"], n)
    f = mm(a, w["ffn_down"], tm=tm_lat, tk=DFF, name="mm_down")
    dy, df, dg_f, loss = _loss_head(x1, f, tgt, g_f, n)

    g = {}
    da = mm(df, w["ffn_down"], tb=True, tm=tm_lat, tn=1408, name="mm_down_dx")
    g["ffn_down"] = mm(a, df, ta=True, tm=1408, tn=1024, tk=tm_lat, name="mm_down_dw")
    du_g, du_v, dcw_g, dcw_v, dcb_g, dcb_v = _ffn_act_bwd(u, da, conv_ff, w["ffn_conv_b"], n)
    du = jnp.concatenate([du_g, du_v], axis=1)
    g["ffn_conv"] = jnp.concatenate([dcw_g, dcw_v], axis=1)[0:3]
    g["ffn_conv_b"] = jnp.concatenate([dcb_g, dcb_v], axis=1)
    dh2 = mm(du, w["ffn_up"], tb=True, tm=tm_lat, tk=1408, name="mm_up_dx")
    g["ffn_up"] = mm(h2, du, ta=True, tm=512, tn=1408, tk=tm_lat, name="mm_up_dw")
    dx1, dmo, g["norm_ffn"], dsh_f, dsc_f, dg_a = _resid_norm_bwd(dy, dh2, x1, mo, g_a, nf, mod_f, n)
    dmerged = mm(dmo, w["w_out"], tb=True, tm=tm_lat, name="mm_out_dx")
    g["w_out"] = mm(merged, dmo, ta=True, tm=512, tk=tm_lat, name="mm_out_dw")
    dz_dn, dz_at, dmg = _merge_bwd(dmerged, z_dn, z_at, p, n)
    dy_dn = mm(dz_dn, w["w_branch_dn"], tb=True, tm=tm_lat, name="mm_bdn_dx")
    g["w_branch_dn"] = mm(y_dn, dz_dn, ta=True, tm=512, tk=tm_lat, name="mm_bdn_dw")
    do_at = mm(dz_at, w["w_branch_attn"], tb=True, tm=tm_lat, name="mm_bat_dx")
    g["w_branch_attn"] = mm(o_at, dz_at, ta=True, tm=512, tk=tm_lat, name="mm_bat_dw")

    do_dn, dgt, g["dn_norm"] = _dn_gate_bwd(dy_dn, o2, p, w["dn_norm"], n_all)
    do_all = jnp.concatenate([jnp.zeros((CTX, D), F32), do_dn], axis=0)
    dn_dvn, dn_dw, dn_dqg, dn_dkd, dn_del = _dn_seq_bwd(dn_w, dn_qg, dn_kd, dn_pm, dn_vn, s_hist, gb, do_all, nct, hi)
    dq2, dk2, dv2, dgb2 = _dn_intra_bwd(q, k, v, gb, dn_u, dn_w, dn_t, dn_vn, dn_dvn, dn_dw, dn_dqg, dn_dkd, dn_del,
                                        do_all, nct, hi)
    dqkv, dba, dconv, dgprm = _dn_prep_bwd(p, conv_dn, gprm, dq2.reshape(2 * t_all, D), dk2.reshape(2 * t_all, D),
                                           dv2.reshape(2 * t_all, D), dgb2.reshape(2 * t_all, 128))
    g["dn_conv"] = dconv[0:5]
    g["dn_a_log"] = dgprm[0, 16:32].reshape(2, NH)
    g["dn_dt_bias"] = dgprm[1, 16:32].reshape(2, NH)

    delta = _attn_delta(o_at, do_at)
    dqr, dkx, dvx, dsink = _attn_bwd_q(qr, kr, vv, sink, do_at, lse, delta, hi)
    dk_lat, dv_lat = _attn_bwd_kv(qr, kr, vv, do_at, lse, delta, hi)
    g["attn_sink"] = dsink[:, 0:NH]
    dq_at, g["q_norm"] = _attn_prep_bwd(dqr, p, w["q_norm"], cos, sin, D, C_QAT // D, CT, n, "attn_prep_q_bwd")
    dkr = jnp.concatenate([dkx, dk_lat], axis=0)
    dk_at, g["k_norm"] = _attn_prep_bwd(dkr, p, w["k_norm"], cos_all, sin_all, KVH * HD, C_KAT // (KVH * HD), 0, n_all,
                                        "attn_prep_k_bwd")
    dv_at = jnp.concatenate([dvx, dv_lat], axis=0).astype(BF16)

    zc = lambda width: jnp.zeros((CTX, width), BF16)
    dp = jnp.concatenate([
        dqkv,
        jnp.concatenate([zc(D), dgt], axis=0),
        jnp.concatenate([zc(D), dq_at], axis=0),
        jnp.concatenate([zc(2 * D), dmg], axis=0),
        dk_at, dv_at, dba, jnp.zeros((t_all, PW - C_PAD), BF16)], axis=1)
    dh = mm(dp, w["w_in_p"], tb=True, tm=tm_all, tn=1024, tk=2048, name="mm_in_dx")
    g["w_in_p"] = mm(h, dp, ta=True, tm=512, tn=2048, tk=tm_all, name="mm_in_dw")
    dnm_c, dsh_ac, dsc_ac = _norm_mod_bwd(dh, xa, None, nm, mod_ac, 0, CT)
    grad_x, dnm_x, dsh_a, dsc_a = _norm_mod_bwd(dh, xa, dx1, nm, mod_ax, CT, n)
    g["norm_mix"] = dnm_c + dnm_x
    dmod_x = jnp.concatenate([dsh_a, dsc_a, dg_a, dsh_f, dsc_f, dg_f], axis=1)
    dmod_c = jnp.concatenate([dsh_ac, dsc_ac, jnp.zeros((1, 4 * D), F32)], axis=1)
    return loss, grad_x, g, dmod_x, dmod_c


def _sum_slots(buf, n_slots, rows, tile, name, stride=1):
    nt = rows // tile

    def fn(i, j, *vals):
        acc = vals[0]
        for v in vals[1:]:
            acc = acc + v
        return (acc,)

    ins = [_In(buf, roff=k * stride * nt) for k in range(n_slots)]
    return _rowcall(name, fn, nt, tile, ins, [_Out(buf.shape[1])])[0]


ADAM_LR, ADAM_B1, ADAM_B2, ADAM_EPS, ADAM_WD, ADAM_STEP = 0.001, 0.9, 0.999, 1e-08, 0.01, 10


def _row_tile(rows, cols):
    for t in (512, 256, 128, 64, 32, 16, 8):
        if rows % t == 0 and t * cols * 4 * 14 <= 40 * 1024 * 1024:
            return t
    return rows


def _adamw(w, g, m, v, name):
    shape = w.shape
    cols = shape[-1]
    rows = max(1, math.prod(shape[:-1]))
    tile = _row_tile(rows, cols)
    c1 = 1.0 / (1.0 - ADAM_B1 ** ADAM_STEP)
    c2 = 1.0 / (1.0 - ADAM_B2 ** ADAM_STEP)

    def fn(i, j, w_, g_, m_, v_):
        mn = ADAM_B1 * m_ + (1.0 - ADAM_B1) * g_
        vn = ADAM_B2 * v_ + (1.0 - ADAM_B2) * (g_ * g_)
        delta = -ADAM_LR * ((mn * c1) / (jnp.sqrt(vn * c2) + ADAM_EPS) + ADAM_WD * w_)
        return delta, mn, vn

    r2 = lambda a: a.reshape(rows, cols)
    outs = _rowcall(name, fn, rows // tile, tile, [_In(r2(w)), _In(r2(g)), _In(r2(m)), _In(r2(v))],
                    [_Out(cols), _Out(cols), _Out(cols)])
    return [o.reshape(shape) for o in outs]


MESH = pl.DeviceIdType.MESH
ANY = pl.BlockSpec(memory_space=pl.ANY)


def _pos():
    return lax.axis_index("x"), lax.axis_index("y"), lax.axis_index("c")


def _all_gather(blk, name):
    m_per, n = blk.shape

    def body(x_ref, out_ref, send_sems, recv_sems, local_sem):
        x, y, c = _pos()
        me, sibling = (x, y, c), (x, y, 1 - c)
        chips = [(1 - x, y), (x, 1 - y), (1 - x, 1 - y)]

        def rows(px, py, pc):
            return out_ref.at[pl.ds(pl.multiple_of((4 * px + 2 * py + pc) * m_per, 8), m_per), :]

        def copy(k, block, to, src=None):
            return pltpu.make_async_remote_copy(
                src_ref=rows(*block) if src is None else src, dst_ref=rows(*block),
                send_sem=send_sems.at[k], recv_sem=recv_sems.at[k], device_id=to, device_id_type=MESH)

        mine = pltpu.make_async_copy(x_ref, rows(*me), local_sem)
        mine.start()
        first = [copy(0, me, sibling, src=x_ref)]
        first += [copy(1 + j, me, (*chip, c), src=x_ref) for j, chip in enumerate(chips)]
        for cp in first:
            cp.start()
        passed = [copy(4 + j, (*chip, c), sibling) for j, chip in enumerate(chips)]
        for j, chip in enumerate(chips):
            copy(1 + j, (*chip, c), me).wait_recv()
            passed[j].start()
        copy(0, sibling, me).wait_recv()
        for j, chip in enumerate(chips):
            copy(4 + j, (*chip, 1 - c), me).wait_recv()
        for cp in first + passed:
            cp.wait_send()
        mine.wait()

    return pl.pallas_call(
        body, name=name, out_shape=jax.ShapeDtypeStruct((N_DEV * m_per, n), blk.dtype),
        in_specs=[ANY], out_specs=ANY,
        scratch_shapes=[pltpu.SemaphoreType.DMA((7,)), pltpu.SemaphoreType.DMA((7,)), pltpu.SemaphoreType.DMA],
        compiler_params=pltpu.CompilerParams(has_side_effects=True),
    )(blk)


def _flip(v, bit):
    return 1 - v if bit else v


def _exchange_pieces(pieces, rows, name):
    n = pieces.shape[1]

    def body(g_ref, out_ref, send_sems, recv_sems, local_sem):
        x, y, c = _pos()

        def piece(px, py, pc):
            return g_ref.at[pl.ds(pl.multiple_of((4 * px + 2 * py + pc) * rows, 8), rows), :]

        own = pltpu.make_async_copy(piece(x, y, c), out_ref.at[pl.ds(0, rows), :], local_sem)
        own.start()
        copies = []
        for k in range(1, N_DEV):
            px, py, pc = _flip(x, k & 4), _flip(y, k & 2), _flip(c, k & 1)
            cp = pltpu.make_async_remote_copy(
                src_ref=piece(px, py, pc), dst_ref=out_ref.at[pl.ds(k * rows, rows), :],
                send_sem=send_sems.at[k - 1], recv_sem=recv_sems.at[k - 1], device_id=(px, py, pc), device_id_type=MESH)
            cp.start()
            copies.append(cp)
        for cp in copies:
            cp.wait_recv()
        for cp in copies:
            cp.wait_send()
        own.wait()

    return pl.pallas_call(
        body, name=name, out_shape=jax.ShapeDtypeStruct((N_DEV * rows, n), pieces.dtype),
        in_specs=[ANY], out_specs=ANY,
        scratch_shapes=[pltpu.SemaphoreType.DMA((7,)), pltpu.SemaphoreType.DMA((7,)), pltpu.SemaphoreType.DMA],
        compiler_params=pltpu.CompilerParams(has_side_effects=True),
    )(pieces)


def _pair_swap(blk, name):
    r, n = blk.shape

    def body(x_ref, out_ref, send_sem, recv_sem, local_sem):
        x, y, c = _pos()
        mine = out_ref.at[pl.ds(pl.multiple_of(c * r, 8), r), :]
        own = pltpu.make_async_copy(x_ref, mine, local_sem)
        own.start()
        cp = pltpu.make_async_remote_copy(src_ref=x_ref, dst_ref=mine, send_sem=send_sem, recv_sem=recv_sem,
                                          device_id=(x, y, 1 - c), device_id_type=MESH)
        cp.start()
        theirs = out_ref.at[pl.ds(pl.multiple_of((1 - c) * r, 8), r), :]
        pltpu.make_async_remote_copy(src_ref=x_ref, dst_ref=theirs, send_sem=send_sem, recv_sem=recv_sem,
                                     device_id=(x, y, 1 - c), device_id_type=MESH).wait_recv()
        cp.wait_send()
        own.wait()

    return pl.pallas_call(
        body, name=name, out_shape=jax.ShapeDtypeStruct((2 * r, n), blk.dtype),
        in_specs=[ANY], out_specs=ANY,
        scratch_shapes=[pltpu.SemaphoreType.DMA, pltpu.SemaphoreType.DMA, pltpu.SemaphoreType.DMA],
        compiler_params=pltpu.CompilerParams(has_side_effects=True),
    )(blk)


BIG = ("w_in", "w_branch_dn", "w_branch_attn", "w_out", "ffn_up", "ffn_down")
BIG_SHARD = {"w_in": (1024, 1928, True), "w_branch_dn": (256, 1024, False), "w_branch_attn": (256, 1024, False),
             "w_out": (256, 1024, False), "ffn_up": (1024, 1408, True), "ffn_down": (704, 1024, False)}
BIG_ROWS = {k: r * c // 2 // 128 for k, (r, c, _) in BIG_SHARD.items()}
PIECE = 19456
assert sum(BIG_ROWS.values()) <= PIECE


def _pack_half(shards, ci, dtype):
    parts = []
    for k in BIG:
        r, c, _ = BIG_SHARD[k]
        parts.append(lax.dynamic_slice_in_dim(shards[k], ci * (r // 2), r // 2, axis=0).reshape(-1, 128).astype(dtype))
    parts.append(jnp.zeros((PIECE - sum(BIG_ROWS.values()), 128), dtype))
    return jnp.concatenate(parts, axis=0)


def _unpack_full(ag):
    out, off = {}, 0
    for k in BIG:
        r, c, by_col = BIG_SHARD[k]
        blk = ag[:, off:off + BIG_ROWS[k]].reshape(4, r, c)
        out[k] = jnp.transpose(blk, (1, 0, 2)).reshape(r, 4 * c) if by_col else blk.reshape(4 * r, c)
        off += BIG_ROWS[k]
    return out


def _pack_pieces(full):
    parts = []
    for k in BIG:
        r, c, by_col = BIG_SHARD[k]
        a = full[k]
        a = jnp.transpose(a.reshape(r, 4, c), (1, 0, 2)) if by_col else a.reshape(4, r, c)
        parts.append(a.reshape(N_DEV, BIG_ROWS[k], 128))
    parts.append(jnp.zeros((N_DEV, PIECE - sum(BIG_ROWS.values()), 128), F32))
    return jnp.concatenate(parts, axis=1).reshape(N_DEV * PIECE, 128)


def _unpack_shard(two):
    out, off = {}, 0
    for k in BIG:
        r, c, _ = BIG_SHARD[k]
        out[k] = two[:, off:off + BIG_ROWS[k]].reshape(r, c)
        off += BIG_ROWS[k]
    return out


SMALL = (("dn_conv", 120), ("ffn_conv", 132), ("ffn_conv_b", 44), ("norm_mix", 8), ("norm_ffn", 8), ("dn_a_log", 1),
         ("dn_dt_bias", 1), ("dn_norm", 1), ("q_norm", 1), ("k_norm", 1), ("attn_sink", 1), ("dmod_c", 48), ("dmod_x", 48))
SMALL_ROWS = 416


def _rows128(a, rows):
    flat = a.reshape(-1)
    return jnp.concatenate([flat, jnp.zeros((rows * 128 - flat.shape[0],), F32)]).reshape(rows, 128)


def _pack_small(g):
    parts = [_rows128(g[k], r) for k, r in SMALL]
    parts.append(jnp.zeros((SMALL_ROWS - sum(r for _, r in SMALL), 128), F32))
    return jnp.concatenate(parts, axis=0)


def _unpack_small(buf, shapes):
    out, off = {}, 0
    for k, r in SMALL:
        n = math.prod(shapes[k])
        out[k] = buf[off:off + r].reshape(-1)[:n].reshape(shapes[k])
        off += r
    return out


WEIGHTS = ("c_ctx", "w_ada", "b_ada", "norm_mix", "norm_ffn", "w_in", "dn_conv", "dn_a_log", "dn_dt_bias", "dn_norm",
           "q_norm", "k_norm", "attn_sink", "w_branch_dn", "w_branch_attn", "w_out", "ffn_up", "ffn_conv", "ffn_conv_b",
           "ffn_down")


def kernel(x, c, ctx, c_ctx, w_ada, b_ada, norm_mix, norm_ffn, w_in, dn_conv, dn_a_log, dn_dt_bias, dn_norm, q_norm, k_norm, attn_sink, w_branch_dn, w_branch_attn, w_out, ffn_up, ffn_conv, ffn_conv_b, ffn_down, loss_target, m_c_ctx, m_w_ada, m_b_ada, m_norm_mix, m_norm_ffn, m_w_in, m_dn_conv, m_dn_a_log, m_dn_dt_bias, m_dn_norm, m_q_norm, m_k_norm, m_attn_sink, m_w_branch_dn, m_w_branch_attn, m_w_out, m_ffn_up, m_ffn_conv, m_ffn_conv_b, m_ffn_down, v_c_ctx, v_w_ada, v_b_ada, v_norm_mix, v_norm_ffn, v_w_in, v_dn_conv, v_dn_a_log, v_dn_dt_bias, v_dn_norm, v_q_norm, v_k_norm, v_attn_sink, v_w_branch_dn, v_w_branch_attn, v_w_out, v_ffn_up, v_ffn_conv, v_ffn_conv_b, v_ffn_down):
    args = dict(locals())
    xi, yi, ci = _pos()
    dev = 4 * xi + 2 * yi + ci
    shard = 2 * xi + yi
    chips = lambda a: a[0::2]

    blk = jnp.concatenate([_rows128(c, 8), _rows128(dn_conv, 30), _rows128(ffn_conv, 33), jnp.zeros((1, 128), F32)], axis=0)
    ag = _all_gather(blk, "ag_small_in").reshape(N_DEV, 72, 128)
    c_all = ag[:, 0:8].reshape(N_DEV, D)
    dn_conv_full = jnp.transpose(chips(ag)[:, 8:38].reshape(4, 5, 768), (1, 0, 2)).reshape(5, 3 * D)
    ffn_conv_full = jnp.transpose(chips(ag)[:, 38:71].reshape(4, 3, 1408), (1, 0, 2)).reshape(3, 2 * DFF)

    c16 = jnp.concatenate([c_all, c_ctx[None], jnp.zeros((7, D), F32)], axis=0)
    a16 = _rowcall("ada_silu", lambda i, j, v: (_silu(v),), 1, 16, [_In(c16)], [_Out(D)])[0]
    m_sh = _mm(a16, w_ada[0], tm=16, tn=512, tk=D, name="ada_fwd", hi=True)
    mod16 = chips(_all_gather(m_sh, "ag_mod").reshape(N_DEV, 16, 1536))
    mod16 = jnp.transpose(mod16, (1, 0, 2)).reshape(16, 6 * D) + b_ada
    mod_x = lax.dynamic_slice_in_dim(mod16, dev, 1, axis=0)
    mod_c = mod16[8:9]

    shards = {k: args[k][0] for k in BIG}
    wfull = _unpack_full(_all_gather(_pack_half(shards, ci, BF16), "ag_weights").reshape(N_DEV, PIECE, 128))
    w = dict(wfull)
    w["w_in_p"] = _pad_w_in(wfull["w_in"])
    w.update(norm_mix=norm_mix, norm_ffn=norm_ffn, dn_conv=dn_conv_full, dn_a_log=dn_a_log[0], dn_dt_bias=dn_dt_bias[0],
             dn_norm=dn_norm, q_norm=q_norm, k_norm=k_norm, attn_sink=attn_sink, ffn_conv=ffn_conv_full, ffn_conv_b=ffn_conv_b)

    xa = jnp.concatenate([ctx[0], x[0]], axis=0)
    loss_part, grad_x, g, dmod_x, dmod_c = _local_step(xa, loss_target[0], mod_x, mod_c, w)
    loss = lax.psum(loss_part[0, 0], ("x", "y", "c"))

    g["w_in"] = _unpad_w_in(g["w_in_p"])
    recv = _exchange_pieces(_pack_pieces(g), PIECE, "rs_exchange")
    mine = _sum_slots(recv, N_DEV, PIECE, 1024, "rs_sum")
    gshard = _unpack_shard(_pair_swap(mine, "rs_pair").reshape(2, PIECE, 128))

    g["dmod_c"], g["dmod_x"] = dmod_c, dmod_x
    ag_s = _all_gather(_pack_small(g), "ag_small_grads")
    shapes = {k: g[k].shape for k, _ in SMALL}
    gs = _unpack_small(_sum_slots(ag_s, N_DEV, SMALL_ROWS, SMALL_ROWS, "small_sum"), shapes)
    dx_all = ag_s.reshape(N_DEV, SMALL_ROWS, 128)[:, SMALL_ROWS - 50:SMALL_ROWS - 2].reshape(N_DEV, 6 * D)

    d16 = jnp.concatenate([dx_all, gs["dmod_c"], jnp.zeros((7, 6 * D), F32)], axis=0)
    d16_sh = lax.dynamic_slice_in_dim(d16, shard * 1536, 1536, axis=1)
    g_w_ada = _mm(a16, d16_sh, ta=True, tm=D, tn=512, tk=16, name="ada_dw", hi=True)
    g_b_ada = _rowcall("ada_db", lambda i, j, v: (_colsum(v),), 1, 16, [_In(d16)], [_Out(6 * D, acc=True)])[0]
    da_part = _mm(d16_sh, w_ada[0], tb=True, tm=16, tn=D, tk=512, name="ada_dx", hi=True)
    da_all = _all_gather(da_part, "ag_ada_dx")
    da16 = _sum_slots(da_all, 4, 16, 16, "ada_dx_sum", stride=2)
    dc16 = _rowcall("ada_dsilu", lambda i, j, d_, v: (d_ * _dsilu(v),), 1, 16, [_In(da16), _In(c16)], [_Out(D)])[0]

    grads = {
        "c_ctx": dc16[8], "w_ada": g_w_ada[None], "b_ada": g_b_ada, "norm_mix": gs["norm_mix"], "norm_ffn": gs["norm_ffn"],
        "w_in": gshard["w_in"][None],
        "dn_conv": lax.dynamic_slice_in_dim(gs["dn_conv"], shard * 768, 768, axis=1)[None],
        "dn_a_log": gs["dn_a_log"][None], "dn_dt_bias": gs["dn_dt_bias"][None], "dn_norm": gs["dn_norm"],
        "q_norm": gs["q_norm"], "k_norm": gs["k_norm"], "attn_sink": gs["attn_sink"],
        "w_branch_dn": gshard["w_branch_dn"][None], "w_branch_attn": gshard["w_branch_attn"][None],
        "w_out": gshard["w_out"][None], "ffn_up": gshard["ffn_up"][None],
        "ffn_conv": lax.dynamic_slice_in_dim(gs["ffn_conv"], shard * 1408, 1408, axis=1)[None],
        "ffn_conv_b": gs["ffn_conv_b"], "ffn_down": gshard["ffn_down"][None],
    }
    deltas, new_m, new_v = [], [], []
    for k in WEIGHTS:
        d_, m_, v_ = _adamw(args[k], grads[k], args["m_" + k], args["v_" + k], "adamw_" + k)
        deltas.append(d_)
        new_m.append(m_)
        new_v.append(v_)
    return (loss, grad_x[None], *[grads[k] for k in WEIGHTS], *deltas, *new_m, *new_v)
```

```python
import functools
import math

import numpy as np
import jax
import jax.numpy as jnp
from jax import lax
from jax.experimental import pallas as pl
from jax.experimental.pallas import tpu as pltpu

F32 = jnp.float32
BF16 = jnp.bfloat16
HI = lax.Precision.HIGHEST

D = 1024
NH = 8
HD = 128
CH = 64
CTX = 256
AB = 128
KVH = 2
GRP = 4
DFF = 2816
EPS = 1e-6
GRID_W = 64
ROPE_BASE = 10000.0
N_DEV = 8
VMEM_LIMIT = 56 * 1024 * 1024

C_QKV, C_GT, C_QAT, C_MG, C_KAT, C_VAT, C_BA, C_PAD = 0, 3072, 4096, 5120, 7168, 7424, 7680, 7808
PW = 8192


def _cparams(sem=None, **kw):
    return pltpu.CompilerParams(dimension_semantics=sem, vmem_limit_bytes=VMEM_LIMIT, **kw)


def _dot(a, b, dims, hi):
    if hi:
        return lax.dot_general(a.astype(F32), b.astype(F32), (dims, ((), ())), precision=HI, preferred_element_type=F32)
    return lax.dot_general(a.astype(BF16), b.astype(BF16), (dims, ((), ())), preferred_element_type=F32)


NN = ((1,), (0,))
NT = ((1,), (1,))
TN = ((0,), (0,))


def _dn_masks():
    i = np.arange(CH)
    lo_incl = (i[:, None] >= i[None, :]).astype(np.float32)
    lo_strict = (i[:, None] > i[None, :]).astype(np.float32)
    return jnp.asarray(np.stack([np.stack([lo_incl, lo_strict]), np.stack([lo_incl.T, lo_strict.T])]))


def _dn_chunk_index(d, i, n_ctx_chunks, n_chunks):
    fwd = i
    bwd = jnp.where(i < n_ctx_chunks, n_ctx_chunks - 1 - i, n_chunks - 1 + n_ctx_chunks - i)
    return jnp.where(d == 0, fwd, bwd)


BNN = ((2,), (1,))
BNT = ((2,), (2,))
BTN = ((1,), (1,))


def _bdot(a, b, dims, hi):
    dn = (dims, ((0,), (0,)))
    if hi:
        return lax.dot_general(a.astype(F32), b.astype(F32), dn, precision=HI, preferred_element_type=F32)
    return lax.dot_general(a.astype(BF16), b.astype(BF16), dn, preferred_element_type=F32)


def _bdot3(a, b, dims, hi):
    if hi:
        return _bdot(a, b, dims, True)
    ah, bh = a.astype(BF16), b.astype(BF16)
    al, bl = (a - ah.astype(F32)).astype(BF16), (b - bh.astype(F32)).astype(BF16)
    dn = (dims, ((0,), (0,)))
    d = lambda x_, y_: lax.dot_general(x_, y_, dn, preferred_element_type=F32)
    return d(ah, bh) + d(ah, bl) + d(al, bh)


def _dn_heads(ref):
    return jnp.stack([ref[:, h * HD:(h + 1) * HD] for h in range(NH)])


def _dn_scalars(gb, mi):
    gcum, gcum_t, gtot = _dn_gcum(gb, mi)
    beta = jnp.stack([gb[:, h:h + 1] for h in range(NH)])
    gc = jnp.stack([gcum[:, NH + h:NH + h + 1] for h in range(NH)])
    gcr = jnp.stack([gcum_t[NH + h:NH + h + 1, :] for h in range(NH)])
    gt = jnp.stack([gtot[:, NH + h:NH + h + 1] for h in range(NH)])
    return beta, gc, gcr, gt


def _dn_total(gb):
    gtot = jnp.sum(gb, axis=0, keepdims=True)
    return jnp.stack([gtot[:, NH + h:NH + h + 1] for h in range(NH)])


def _dn_gcum(gb, mi):
    gcum = _dot(mi, gb, NN, True)
    gtot = jnp.sum(gb, axis=0, keepdims=True)
    return gcum, gcum.T, gtot


def _dn_specs(n_ctx_chunks, n_chunks, reverse):
    def cidx(d, i):
        return _dn_chunk_index(d, n_chunks - 1 - i if reverse else i, n_ctx_chunks, n_chunks)

    tok = pl.BlockSpec((CH, D), lambda d, i: (cidx(d, i), 0))
    tok_d = pl.BlockSpec((1, CH, D), lambda d, i: (d, cidx(d, i), 0))
    gbs = pl.BlockSpec((1, CH, 128), lambda d, i: (d, cidx(d, i), 0))
    msk = pl.BlockSpec((1, 2, CH, CH), lambda d, i: (d, 0, 0, 0))

    def per_chunk(*tail):
        return pl.BlockSpec((1, 1) + tail, lambda d, i: (d, cidx(d, i)) + (0,) * len(tail))

    return tok, tok_d, gbs, msk, per_chunk


def _dn_intra_fwd(q, k, v, gb, n_ctx_chunks, hi):
    t_all = q.shape[0]
    n_chunks = t_all // CH
    masks = _dn_masks()

    def body(q_ref, k_ref, v_ref, gb_ref, m_ref, u_ref, w_ref, qg_ref, kd_ref, pm_ref, t_ref):
        mi, ms = m_ref[0, 0], m_ref[0, 1]
        beta, gc, gcr, gt = _dn_scalars(gb_ref[0], mi)
        q_, k_, v_ = _dn_heads(q_ref), _dn_heads(k_ref), _dn_heads(v_ref)
        decay = jnp.exp(jnp.where(mi > 0, gc - gcr, 0.0)) * mi
        e = jnp.exp(gc)
        a = ms * (beta * _bdot(k_, k_, BNT, hi) * decay)
        x = -a
        eye = (lax.broadcasted_iota(jnp.int32, (CH, CH), 0) == lax.broadcasted_iota(jnp.int32, (CH, CH), 1)).astype(F32)
        t = eye + x
        p = x
        for _ in range(5):
            p = _bdot3(p, p, BNN, hi)
            t = t + _bdot3(t, p, BNN, hi)
        uw = _bdot(t, jnp.concatenate([beta * v_, (beta * e) * k_], axis=2), BNN, hi)
        u_ref[0, 0] = uw[:, :, :HD]
        w_ref[0, 0] = uw[:, :, HD:].astype(w_ref.dtype)
        qg_ref[0, 0] = (e * q_).astype(qg_ref.dtype)
        kd_ref[0, 0] = (jnp.exp(gt - gc) * k_).astype(kd_ref.dtype)
        pm_ref[0, 0] = (_bdot(q_, k_, BNT, hi) * decay).astype(pm_ref.dtype)
        t_ref[0, 0] = t.astype(t_ref.dtype)

    tok, _, gbs, msk, per_chunk = _dn_specs(n_ctx_chunks, n_chunks, False)
    big = lambda dt: jax.ShapeDtypeStruct((2, n_chunks, NH, CH, HD), dt)
    sq = jax.ShapeDtypeStruct((2, n_chunks, NH, CH, CH), BF16)
    return pl.pallas_call(
        body, name="dn_intra_fwd", grid=(2, n_chunks),
        in_specs=[tok, tok, tok, gbs, msk],
        out_specs=[per_chunk(NH, CH, HD)] * 4 + [per_chunk(NH, CH, CH)] * 2,
        out_shape=[big(F32), big(BF16), big(BF16), big(BF16), sq, sq],
        compiler_params=_cparams(("parallel", "parallel")),
    )(q, k, v, gb, masks)


def _dn_seq_fwd(u, w, qg, kd, pm, gb, n_ctx_chunks, hi):
    n_chunks = u.shape[1]
    t_all = n_chunks * CH

    def body(u_ref, w_ref, qg_ref, kd_ref, pm_ref, gb_ref, o_ref, sh_ref, vn_ref, s_scr):
        @pl.when(pl.program_id(1) == 0)
        def _():
            s_scr[...] = jnp.zeros_like(s_scr)

        s = s_scr[...]
        sh_ref[0, 0] = s
        vn = u_ref[0, 0] - _bdot(w_ref[0, 0], s, BNN, hi)
        o = _bdot(qg_ref[0, 0], s, BNN, hi) + _bdot(pm_ref[0, 0], vn, BNN, hi)
        s_scr[...] = jnp.exp(_dn_total(gb_ref[0])) * s + _bdot(kd_ref[0, 0], vn, BTN, hi)
        vn_ref[0, 0] = vn.astype(vn_ref.dtype)
        for h in range(NH):
            o_ref[0, :, h * HD:(h + 1) * HD] = o[h]

    _, tok_d, gbs, _, per_chunk = _dn_specs(n_ctx_chunks, n_chunks, False)
    big = per_chunk(NH, CH, HD)
    return pl.pallas_call(
        body, name="dn_seq_fwd", grid=(2, n_chunks),
        in_specs=[big, big, big, big, per_chunk(NH, CH, CH), gbs],
        out_specs=[tok_d, per_chunk(NH, HD, HD), big],
        out_shape=[jax.ShapeDtypeStruct((2, t_all, D), F32), jax.ShapeDtypeStruct((2, n_chunks, NH, HD, HD), F32),
                   jax.ShapeDtypeStruct((2, n_chunks, NH, CH, HD), BF16)],
        scratch_shapes=[pltpu.VMEM((NH, HD, HD), F32)],
        compiler_params=_cparams(("parallel", "arbitrary")),
    )(u, w, qg, kd, pm, gb)


def _dn_seq_bwd(w, qg, kd, pm, vn, s_hist, gb, do, n_ctx_chunks, hi):
    n_chunks = w.shape[1]

    def body(w_ref, qg_ref, kd_ref, pm_ref, vn_ref, sh_ref, gb_ref, do_ref, dvn_ref, dw_ref, dqg_ref, dkd_ref, del_ref, ds_scr):
        @pl.when(pl.program_id(1) == 0)
        def _():
            ds_scr[...] = jnp.zeros_like(ds_scr)

        dsn = ds_scr[...]
        s = sh_ref[0, 0]
        do_ = _dn_heads(do_ref)
        dvn = _bdot(pm_ref[0, 0], do_, BTN, hi) + _bdot(kd_ref[0, 0], dsn, BNN, hi)
        ds_scr[...] = (_bdot(qg_ref[0, 0], do_, BTN, hi) + jnp.exp(_dn_total(gb_ref[0])) * dsn
                       - _bdot(w_ref[0, 0], dvn, BTN, hi))
        dvn_ref[0, 0] = dvn.astype(dvn_ref.dtype)
        dw_ref[0, 0] = (-_bdot(dvn, s, BNT, hi)).astype(dw_ref.dtype)
        dqg_ref[0, 0] = _bdot(do_, s, BNT, hi)
        dkd_ref[0, 0] = _bdot(vn_ref[0, 0], dsn, BNT, hi)
        del_ref[0, 0] = jnp.broadcast_to(jnp.sum(jnp.sum(s * dsn, axis=2, keepdims=True), axis=1, keepdims=True),
                                         (NH, 1, 128))

    tok, _, gbs, _, per_chunk = _dn_specs(n_ctx_chunks, n_chunks, True)
    big = per_chunk(NH, CH, HD)
    shp = lambda dt: jax.ShapeDtypeStruct((2, n_chunks, NH, CH, HD), dt)
    return pl.pallas_call(
        body, name="dn_seq_bwd", grid=(2, n_chunks),
        in_specs=[big, big, big, per_chunk(NH, CH, CH), big, per_chunk(NH, HD, HD), gbs, tok],
        out_specs=[big, big, big, big, per_chunk(NH, 1, 128)],
        out_shape=[shp(BF16), shp(BF16), shp(F32), shp(F32), jax.ShapeDtypeStruct((2, n_chunks, NH, 1, 128), F32)],
        scratch_shapes=[pltpu.VMEM((NH, HD, HD), F32)],
        compiler_params=_cparams(("parallel", "arbitrary")),
    )(w, qg, kd, pm, vn, s_hist, gb, do)


def _dn_intra_bwd(q, k, v, gb, u, w, t, vn, dvn, dw, dqg, dkd, de_last, do, n_ctx_chunks, hi):
    t_all = q.shape[0]
    n_chunks = t_all // CH
    masks = _dn_masks()

    def body(q_ref, k_ref, v_ref, gb_ref, m_ref, u_ref, w_ref, t_ref, vn_ref, dvn_ref, dw_ref, dqg_ref, dkd_ref, del_ref,
             do_ref, dq_ref, dk_ref, dv_ref, dgb_ref):
        mi, ms = m_ref[0, 0], m_ref[0, 1]
        beta, gc, gcr, gt = _dn_scalars(gb_ref[0], mi)
        q_, k_, v_, do_ = _dn_heads(q_ref), _dn_heads(k_ref), _dn_heads(v_ref), _dn_heads(do_ref)
        decay = jnp.exp(jnp.where(mi > 0, gc - gcr, 0.0)) * mi
        e = jnp.exp(gc)
        e_last = jnp.exp(gt)
        kdfac = jnp.exp(gt - gc)
        kk = _bdot(k_, k_, BNT, hi)
        a = ms * (beta * kk * decay)
        pm = _bdot(q_, k_, BNT, hi) * decay
        kd = kdfac * k_
        dqg, dkd = dqg_ref[0, 0], dkd_ref[0, 0]
        dpm = _bdot(do_, vn_ref[0, 0], BNT, hi)
        dvbkb = _bdot(t_ref[0, 0], jnp.concatenate([dvn_ref[0, 0], dw_ref[0, 0]], axis=2), BTN, hi)
        dvb, dkb = dvbkb[:, :, :HD], dvbkb[:, :, HD:]
        da = -ms * _bdot(dvbkb, jnp.concatenate([u_ref[0, 0], w_ref[0, 0].astype(F32)], axis=2), BNT, hi)
        dqk = dpm * decay
        gm = dpm * pm + da * a
        dgc = (jnp.sum(gm, axis=2, keepdims=True)
               - _bdot3(gm, jnp.ones((NH, CH, 128), F32), BTN, hi)[:, :, 0:1])
        dkk = da * (beta * decay)
        dbeta = jnp.sum(da * kk * decay, axis=2, keepdims=True)
        dk = _bdot(dkk, k_, BNN, hi) + _bdot(dkk, k_, BTN, hi) + _bdot(dqk, q_, BTN, hi)
        dq = _bdot(dqk, k_, BNN, hi) + e * dqg
        de = jnp.sum(dqg * q_, axis=2, keepdims=True)
        dv = beta * dvb
        dbeta = dbeta + jnp.sum(dvb * v_, axis=2, keepdims=True)
        skb = jnp.sum(dkb * k_, axis=2, keepdims=True)
        dk = dk + (beta * e) * dkb + kdfac * dkd
        dbeta = dbeta + e * skb
        de = de + beta * skb
        skd = jnp.sum(dkd * kd, axis=2, keepdims=True)
        dgc = dgc - skd + de * e
        dgtot = jnp.sum(skd, axis=1, keepdims=True) + del_ref[0, 0][:, :, 0:1] * e_last
        lane = lax.broadcasted_iota(jnp.int32, (1, 128), 1)
        dbeta_all = jnp.zeros((CH, 128), F32)
        dgc_all = jnp.zeros((CH, 128), F32)
        dgtot_all = jnp.zeros((1, 128), F32)
        for h in range(NH):
            sl = slice(h * HD, (h + 1) * HD)
            dq_ref[0, :, sl] = dq[h]
            dk_ref[0, :, sl] = dk[h]
            dv_ref[0, :, sl] = dv[h]
            hot_b = (lane == h).astype(F32)
            hot_g = (lane == NH + h).astype(F32)
            dbeta_all = dbeta_all + dbeta[h] * hot_b
            dgc_all = dgc_all + dgc[h] * hot_g
            dgtot_all = dgtot_all + dgtot[h] * hot_g
        dgb_ref[0] = dbeta_all + _dot(mi, dgc_all, TN, True) + dgtot_all

    tok, tok_d, gbs, msk, per_chunk = _dn_specs(n_ctx_chunks, n_chunks, False)
    big = per_chunk(NH, CH, HD)
    return pl.pallas_call(
        body, name="dn_intra_bwd", grid=(2, n_chunks),
        in_specs=[tok, tok, tok, gbs, msk, big, big, per_chunk(NH, CH, CH), big, big, big, big, big,
                  per_chunk(NH, 1, 128), tok],
        out_specs=[tok_d, tok_d, tok_d, gbs],
        out_shape=[jax.ShapeDtypeStruct((2, t_all, D), F32)] * 3 + [jax.ShapeDtypeStruct((2, t_all, 128), F32)],
        compiler_params=_cparams(("parallel", "parallel")),
    )(q, k, v, gb, masks, u, w, t, vn, dvn, dw, dqg, dkd, de_last, do)


ATT_SCALE = HD ** -0.5
NEG = -1e30


def _att_stack(ref, kvh):
    return jnp.concatenate([ref[:, (kvh * GRP + g) * HD:(kvh * GRP + g + 1) * HD] for g in range(GRP)], axis=0)


def _att_col(ref, kvh):
    return jnp.concatenate([ref[:, kvh * GRP + g:kvh * GRP + g + 1] for g in range(GRP)], axis=0)


def _att_sink(sink_ref, kvh):
    return jnp.concatenate([jnp.broadcast_to(sink_ref[:, kvh * GRP + g:kvh * GRP + g + 1], (AB, 1)) for g in range(GRP)],
                           axis=0)


def _att_mask(i, nb):
    r = lax.broadcasted_iota(jnp.int32, (AB, AB), 0)
    c = lax.broadcasted_iota(jnp.int32, (AB, AB), 1)
    okp = jnp.logical_and(c >= r, i > 0)
    okn = jnp.logical_and(c <= r, i < nb - 1)
    m = jnp.concatenate([okp, jnp.ones((AB, AB), jnp.bool_), okn, jnp.ones((AB, CTX), jnp.bool_)], axis=1)
    return jnp.concatenate([m] * GRP, axis=0)


def _att_kspecs(nb):
    nc = CTX // AB
    return [pl.BlockSpec((AB, KVH * HD), lambda i: (jnp.maximum(i - 1, 0) + nc, 0)),
            pl.BlockSpec((AB, KVH * HD), lambda i: (i + nc, 0)),
            pl.BlockSpec((AB, KVH * HD), lambda i: (jnp.minimum(i + 1, nb - 1) + nc, 0)),
            pl.BlockSpec((CTX, KVH * HD), lambda i: (0, 0))]


def _attn_fwd(qr, kr, vv, sink, hi):
    tl = qr.shape[0]
    nb = tl // AB

    def body(q_ref, kp_ref, kc_ref, kn_ref, kx_ref, vp_ref, vc_ref, vn_ref, vx_ref, sink_ref, o_ref, lse_ref):
        i = pl.program_id(0)
        mask = _att_mask(i, nb)
        lane = lax.broadcasted_iota(jnp.int32, (1, 128), 1)
        lse_all = jnp.zeros((AB, 128), F32)
        for kvh in range(KVH):
            ksl = slice(kvh * HD, (kvh + 1) * HD)
            kall = jnp.concatenate([kp_ref[:, ksl], kc_ref[:, ksl], kn_ref[:, ksl], kx_ref[:, ksl]], axis=0)
            vall = jnp.concatenate([vp_ref[:, ksl], vc_ref[:, ksl], vn_ref[:, ksl], vx_ref[:, ksl]], axis=0)
            s = _dot(_att_stack(q_ref, kvh), kall, NT, hi) * ATT_SCALE
            s = jnp.where(mask, s, NEG)
            sk = _att_sink(sink_ref, kvh)
            m = jnp.maximum(jnp.max(s, axis=1, keepdims=True), sk)
            p = jnp.exp(s - m)
            l = jnp.sum(p, axis=1, keepdims=True) + jnp.exp(sk - m)
            o = _dot(p, vall, NN, hi) / l
            lse = m + jnp.log(l)
            for g in range(GRP):
                h = kvh * GRP + g
                o_ref[:, h * HD:(h + 1) * HD] = o[g * AB:(g + 1) * AB]
                lse_all = lse_all + lse[g * AB:(g + 1) * AB] * (lane == h).astype(F32)
        lse_ref[...] = lse_all

    ks = _att_kspecs(nb)
    return pl.pallas_call(
        body, name="attn_fwd", grid=(nb,),
        in_specs=[pl.BlockSpec((AB, D), lambda i: (i, 0))] + ks + ks + [pl.BlockSpec((1, 128), lambda i: (0, 0))],
        out_specs=[pl.BlockSpec((AB, D), lambda i: (i, 0)), pl.BlockSpec((AB, 128), lambda i: (i, 0))],
        out_shape=[jax.ShapeDtypeStruct((tl, D), F32), jax.ShapeDtypeStruct((tl, 128), F32)],
        compiler_params=_cparams(("parallel",)),
    )(qr, kr, kr, kr, kr, vv, vv, vv, vv, sink)


def _attn_delta(o, do):
    tl = o.shape[0]
    tr = min(512, tl)

    def body(o_ref, do_ref, d_ref):
        lane = lax.broadcasted_iota(jnp.int32, (1, 128), 1)
        acc = jnp.zeros((tr, 128), F32)
        for h in range(NH):
            sl = slice(h * HD, (h + 1) * HD)
            acc = acc + jnp.sum(o_ref[:, sl] * do_ref[:, sl], axis=1, keepdims=True) * (lane == h).astype(F32)
        d_ref[...] = acc

    return pl.pallas_call(
        body, name="attn_delta", grid=(tl // tr,),
        in_specs=[pl.BlockSpec((tr, D), lambda i: (i, 0))] * 2,
        out_specs=pl.BlockSpec((tr, 128), lambda i: (i, 0)),
        out_shape=jax.ShapeDtypeStruct((tl, 128), F32),
        compiler_params=_cparams(("parallel",)),
    )(o, do)


def _attn_bwd_q(qr, kr, vv, sink, do, lse, delta, hi):
    tl = qr.shape[0]
    nb = tl // AB

    def body(q_ref, kp_ref, kc_ref, kn_ref, kx_ref, vp_ref, vc_ref, vn_ref, vx_ref, sink_ref, do_ref, lse_ref, dl_ref,
             dq_ref, dkx_ref, dvx_ref, dsink_ref):
        i = pl.program_id(0)

        @pl.when(i == 0)
        def _():
            dkx_ref[...] = jnp.zeros_like(dkx_ref)
            dvx_ref[...] = jnp.zeros_like(dvx_ref)
            dsink_ref[...] = jnp.zeros_like(dsink_ref)

        mask = _att_mask(i, nb)
        lane = lax.broadcasted_iota(jnp.int32, (1, 128), 1)
        dsink = jnp.zeros((1, 128), F32)
        for kvh in range(KVH):
            ksl = slice(kvh * HD, (kvh + 1) * HD)
            kall = jnp.concatenate([kp_ref[:, ksl], kc_ref[:, ksl], kn_ref[:, ksl], kx_ref[:, ksl]], axis=0)
            vall = jnp.concatenate([vp_ref[:, ksl], vc_ref[:, ksl], vn_ref[:, ksl], vx_ref[:, ksl]], axis=0)
            qs = _att_stack(q_ref, kvh)
            dos = _att_stack(do_ref, kvh)
            lse_s = _att_col(lse_ref, kvh)
            dl_s = _att_col(dl_ref, kvh)
            s = _dot(qs, kall, NT, hi) * ATT_SCALE
            p = jnp.where(mask, jnp.exp(jnp.where(mask, s, NEG) - lse_s), 0.0)
            dp = _dot(dos, vall, NT, hi)
            ds = p * (dp - dl_s)
            dq = _dot(ds, kall, NN, hi) * ATT_SCALE
            dkx_ref[:, ksl] += _dot(ds[:, 3 * AB:], qs, TN, hi) * ATT_SCALE
            dvx_ref[:, ksl] += _dot(p[:, 3 * AB:], dos, TN, hi)
            psink = jnp.exp(_att_sink(sink_ref, kvh) - lse_s) * dl_s
            for g in range(GRP):
                h = kvh * GRP + g
                dq_ref[:, h * HD:(h + 1) * HD] = dq[g * AB:(g + 1) * AB]
                dsink = dsink - jnp.sum(psink[g * AB:(g + 1) * AB], axis=0, keepdims=True) * (lane == h).astype(F32)
        dsink_ref[...] += dsink

    ks = _att_kspecs(nb)
    row = pl.BlockSpec((AB, D), lambda i: (i, 0))
    col = pl.BlockSpec((AB, 128), lambda i: (i, 0))
    return pl.pallas_call(
        body, name="attn_bwd_q", grid=(nb,),
        in_specs=[row] + ks + ks + [pl.BlockSpec((1, 128), lambda i: (0, 0)), row, col, col],
        out_specs=[row, pl.BlockSpec((CTX, KVH * HD), lambda i: (0, 0)), pl.BlockSpec((CTX, KVH * HD), lambda i: (0, 0)),
                   pl.BlockSpec((1, 128), lambda i: (0, 0))],
        out_shape=[jax.ShapeDtypeStruct((tl, D), F32), jax.ShapeDtypeStruct((CTX, KVH * HD), F32),
                   jax.ShapeDtypeStruct((CTX, KVH * HD), F32), jax.ShapeDtypeStruct((1, 128), F32)],
        compiler_params=_cparams(("arbitrary",)),
    )(qr, kr, kr, kr, kr, vv, vv, vv, vv, sink, do, lse, delta)


def _attn_bwd_kv(qr, kr, vv, do, lse, delta, hi):
    tl = qr.shape[0]
    nb = tl // AB
    nc = CTX // AB

    def body(k_ref, v_ref, *refs):
        qs_refs, do_refs, lse_refs, dl_refs = refs[0:3], refs[3:6], refs[6:9], refs[9:12]
        dk_ref, dv_ref = refs[12], refs[13]
        j = pl.program_id(0)
        r = lax.broadcasted_iota(jnp.int32, (AB, AB), 0)
        c = lax.broadcasted_iota(jnp.int32, (AB, AB), 1)
        one = jnp.ones((AB, AB), jnp.bool_)
        masks = [jnp.logical_and(c <= r, j > 0), one, jnp.logical_and(c >= r, j < nb - 1)]
        for kvh in range(KVH):
            ksl = slice(kvh * HD, (kvh + 1) * HD)
            k_, v_ = k_ref[:, ksl], v_ref[:, ksl]
            dk = jnp.zeros((AB, HD), F32)
            dv = jnp.zeros((AB, HD), F32)
            for t in range(3):
                mask = jnp.concatenate([masks[t]] * GRP, axis=0)
                qs = _att_stack(qs_refs[t], kvh)
                dos = _att_stack(do_refs[t], kvh)
                lse_s = _att_col(lse_refs[t], kvh)
                dl_s = _att_col(dl_refs[t], kvh)
                s = _dot(qs, k_, NT, hi) * ATT_SCALE
                p = jnp.where(mask, jnp.exp(jnp.where(mask, s, NEG) - lse_s), 0.0)
                dp = _dot(dos, v_, NT, hi)
                ds = p * (dp - dl_s)
                dv = dv + _dot(p, dos, TN, hi)
                dk = dk + _dot(ds, qs, TN, hi) * ATT_SCALE
            dk_ref[:, ksl] = dk
            dv_ref[:, ksl] = dv

    def three(width):
        return [pl.BlockSpec((AB, width), lambda j: (jnp.maximum(j - 1, 0), 0)),
                pl.BlockSpec((AB, width), lambda j: (j, 0)),
                pl.BlockSpec((AB, width), lambda j: (jnp.minimum(j + 1, nb - 1), 0))]

    kv = pl.BlockSpec((AB, KVH * HD), lambda j: (j + nc, 0))
    out = pl.BlockSpec((AB, KVH * HD), lambda j: (j, 0))
    return pl.pallas_call(
        body, name="attn_bwd_kv", grid=(nb,),
        in_specs=[kv, kv] + three(D) + three(D) + three(128) + three(128),
        out_specs=[out, out],
        out_shape=[jax.ShapeDtypeStruct((tl, KVH * HD), F32)] * 2,
        compiler_params=_cparams(("parallel",)),
    )(kr, vv, qr, qr, qr, do, do, do, lse, lse, lse, delta, delta, delta)


def _mm(a, b, ta=False, tb=False, out_dtype=F32, tm=512, tn=1024, tk=1024, name="mm", hi=False):
    m, kd = (a.shape[1], a.shape[0]) if ta else a.shape
    n = b.shape[0] if tb else b.shape[1]
    tm, tn, tk = min(tm, m), min(tn, n), min(tk, kd)
    assert m % tm == 0 and n % tn == 0 and kd % tk == 0, (name, m, n, kd, tm, tn, tk)
    nk = kd // tk
    dims = ((0,) if ta else (1,), (1,) if tb else (0,))

    def body(a_ref, b_ref, o_ref, *scr):
        part = _dot(a_ref[...], b_ref[...], dims, hi)
        if nk == 1:
            o_ref[...] = part.astype(out_dtype)
        else:
            acc = scr[0]
            kk = pl.program_id(2)

            @pl.when(kk == 0)
            def _():
                acc[...] = part

            @pl.when(kk > 0)
            def _():
                acc[...] += part

            @pl.when(kk == nk - 1)
            def _():
                o_ref[...] = acc[...].astype(out_dtype)

    a_spec = pl.BlockSpec((tk, tm), lambda i, j, k: (k, i)) if ta else pl.BlockSpec((tm, tk), lambda i, j, k: (i, k))
    b_spec = pl.BlockSpec((tn, tk), lambda i, j, k: (j, k)) if tb else pl.BlockSpec((tk, tn), lambda i, j, k: (k, j))
    return pl.pallas_call(
        body, name=name, grid=(m // tm, n // tn, nk),
        in_specs=[a_spec, b_spec],
        out_specs=pl.BlockSpec((tm, tn), lambda i, j, k: (i, j)),
        out_shape=jax.ShapeDtypeStruct((m, n), out_dtype),
        scratch_shapes=[] if nk == 1 else [pltpu.VMEM((tm, tn), F32)],
        compiler_params=_cparams(("parallel", "parallel", "arbitrary")),
    )(a, b)


HALO = 8


class _In:
    def __init__(self, arr, w=None, cb=0, roff=0, halo=None):
        self.arr, self.w, self.cb, self.roff, self.halo = arr, w or arr.shape[1], cb, roff, halo


class _Full:
    def __init__(self, arr, w=None, cb=0):
        self.arr, self.w, self.cb = arr, w, cb


class _Out:
    def __init__(self, cols, dtype=F32, w=None, cb=0, acc=False, rows=1, roff=0, nrows=None):
        self.cols, self.dtype, self.w, self.cb, self.acc, self.rows, self.roff, self.nrows = (
            cols, dtype, w or cols, cb, acc, rows, roff, nrows)


def _rowcall(name, fn, nrow_tiles, tile, ins, outs, ncol=1):
    arrays, specs, kinds = [], [], []
    for x in ins:
        if isinstance(x, _Full):
            arrays.append(x.arr)
            if x.w is None:
                specs.append(pl.BlockSpec(x.arr.shape, lambda j, i: (0, 0)))
            else:
                specs.append(pl.BlockSpec((x.arr.shape[0], x.w), lambda j, i, cb=x.cb: (0, cb + j)))
            kinds.append("full")
            continue
        w, cb, roff = x.w, x.cb, x.roff
        cur = pl.BlockSpec((tile, w), lambda j, i, cb=cb, roff=roff: (i + roff, cb + j))
        if x.halo is None:
            arrays.append(x.arr)
            specs.append(cur)
            kinds.append("tile")
        else:
            r8 = tile // HALO
            last = x.arr.shape[0] // HALO - 1
            prev = pl.BlockSpec((HALO, w), lambda j, i, cb=cb, roff=roff, r8=r8: (jnp.maximum((i + roff) * r8 - 1, 0), cb + j))
            nxt = pl.BlockSpec((HALO, w), lambda j, i, cb=cb, roff=roff, r8=r8, last=last:
                               (jnp.minimum((i + roff + 1) * r8, last), cb + j))
            arrays += [x.arr, x.arr, x.arr]
            specs += [prev, cur, nxt]
            kinds.append(("halo", x.halo))
    out_specs, out_shapes = [], []
    for o in outs:
        if o.acc:
            out_specs.append(pl.BlockSpec((o.rows, o.w), lambda j, i, cb=o.cb: (0, cb + j)))
            out_shapes.append(jax.ShapeDtypeStruct((o.rows, o.cols), o.dtype))
        else:
            out_specs.append(pl.BlockSpec((tile, o.w), lambda j, i, cb=o.cb, roff=o.roff: (i + roff, cb + j)))
            out_shapes.append(jax.ShapeDtypeStruct(((o.nrows or nrow_tiles * tile), o.cols), o.dtype))
    n_in = len(arrays)

    def body(*refs):
        j = pl.program_id(0)
        i = pl.program_id(1)
        vals, r = [], 0
        for kind in kinds:
            if kind in ("full", "tile"):
                vals.append(refs[r][...])
                r += 1
            else:
                pok, nok = kind[1]
                p, c, n = refs[r][...], refs[r + 1][...], refs[r + 2][...]
                p = jnp.where(pok(i), p, jnp.zeros_like(p))
                n = jnp.where(nok(i), n, jnp.zeros_like(n))
                vals.append(jnp.concatenate([p, c, n], axis=0))
                r += 3
        res = fn(i, j, *vals)
        for o, ref, val in zip(outs, refs[n_in:], res):
            if o.acc:
                @pl.when(i == 0)
                def _(ref=ref, val=val, o=o):
                    ref[...] = val.astype(o.dtype)

                @pl.when(i > 0)
                def _(ref=ref, val=val, o=o):
                    ref[...] += val.astype(o.dtype)
            else:
                ref[...] = val.astype(o.dtype)

    return pl.pallas_call(
        body, name=name, grid=(ncol, nrow_tiles), in_specs=specs, out_specs=out_specs, out_shape=out_shapes,
        compiler_params=_cparams(("parallel", "arbitrary")),
    )(*arrays)


def _shift(xe, s, tile):
    if s == 0:
        return xe[HALO:HALO + tile]
    return pltpu.roll(xe, (-s) % xe.shape[0], 0)[HALO:HALO + tile]


def _silu(x):
    return x * jax.nn.sigmoid(x)


def _dsilu(x):
    s = jax.nn.sigmoid(x)
    return s * (1.0 + x * (1.0 - s))


def _heads(x, fn):
    return jnp.concatenate([fn(h, x[:, h * HD:(h + 1) * HD]) for h in range(x.shape[1] // HD)], axis=1)


def _colsum(x):
    return jnp.sum(x, axis=0, keepdims=True)


def _rowmean(x):
    return jnp.mean(x, axis=1, keepdims=True)


def _rowsum(x):
    return jnp.sum(x, axis=1, keepdims=True)


TILE = 256
CT = CTX // TILE


def _all_halo(n_tiles):
    return (lambda i: i >= CT + 1, lambda i: jnp.logical_and(i >= CT, i < n_tiles - 1))


def _lat_halo(n_tiles):
    return (lambda i: i >= 1, lambda i: i < n_tiles - 1)


def _rms_mod(x, nm, shift, scale):
    r = lax.rsqrt(_rowmean(x * x) + EPS)
    return (x * r * nm) * (1.0 + scale) + shift


def _rms_mod_bwd(dh, x, nm, scale):
    r = lax.rsqrt(_rowmean(x * x) + EPS)
    xn = x * r
    dz = dh * (1.0 + scale)
    dxn = dz * nm
    dx = r * (dxn - xn * _rowmean(dxn * xn))
    return dx, _colsum(dz * xn), _colsum(dh), _colsum(dh * (xn * nm))


def _norm_mod(xa, nm, mod_c, mod_x):
    n = xa.shape[0] // TILE

    def fn(i, j, x, nm_, mc, mx):
        m = jnp.where(i < CT, mc, mx)
        return (_rms_mod(x, nm_, m[0:1], m[1:2]),)

    return _rowcall("norm_mod", fn, n, TILE, [_In(xa), _Full(nm), _Full(mod_c), _Full(mod_x)], [_Out(D, BF16)])[0]


def _norm_mod_bwd(dh, xa, dres, nm, mod, roff, n):
    ins = [_In(dh, roff=roff), _In(xa, roff=roff), _Full(nm), _Full(mod)] + ([] if dres is None else [_In(dres)])

    def fn(i, j, dh_, x, nm_, m, *rest):
        dx, dn, dsh, dsc = _rms_mod_bwd(dh_, x, nm_, m[1:2])
        if rest:
            return (dx + rest[0], dn, dsh, dsc)
        return (dn, dsh, dsc)

    accs = [_Out(D, acc=True), _Out(D, acc=True), _Out(D, acc=True)]
    return _rowcall("norm_mod_bwd", fn, n, TILE, ins, ([] if dres is None else [_Out(D)]) + accs)


DN_Q_SCALE = HD ** -0.5


def _conv_taps(xe, w, width, rows=None):
    r = width // 2
    acc = None
    for t in range(width):
        s = t - r
        if rows is None:
            sh = xe if s == 0 else pltpu.roll(xe, (-s) % xe.shape[0], 0)
        else:
            sh = _shift(xe, s, rows)
        term = sh * w[t:t + 1]
        acc = term if acc is None else acc + term
    return acc


def _l2n(x, scale):
    rn = lax.rsqrt(_rowsum(x * x) + EPS)
    return x * (rn * scale)


def _l2n_bwd(dy, x, scale):
    rn = lax.rsqrt(_rowsum(x * x) + EPS)
    xu = x * rn
    return (scale * rn) * (dy - xu * _rowsum(dy * xu))


def _softplus(x):
    return jnp.maximum(x, 0.0) + jnp.log(1.0 + jnp.exp(-jnp.abs(x)))


def _lane_mask(lo, hi_):
    lane = lax.broadcasted_iota(jnp.int32, (1, 128), 1)
    return jnp.logical_and(lane >= lo, lane < hi_).astype(F32)


def _dn_prep(p, conv_w, gprm):
    n = p.shape[0] // TILE
    halo = _all_halo(n)

    def fn(i, j, qe, ke, ve, ba, w, gp):
        cq = _conv_taps(qe, w[:, 0:D], 5, TILE)
        ck = _conv_taps(ke, w[:, D:2 * D], 5, TILE)
        cv = _conv_taps(ve, w[:, 2 * D:3 * D], 5, TILE)
        q = _heads(_silu(cq), lambda h, x: _l2n(x, DN_Q_SCALE))
        k = _heads(_silu(ck), lambda h, x: _l2n(x, 1.0))
        v = _silu(cv)
        beta = jax.nn.sigmoid(ba)
        g = -jnp.exp(gp[0:1]) * _softplus(ba + gp[1:2])
        m0, m1 = _lane_mask(0, 8), _lane_mask(8, 16)
        gb_f = beta * m0 + pltpu.roll(g, 128 - 8, 1) * m1
        gb_b = pltpu.roll(beta, 128 - 8, 1) * m0 + pltpu.roll(g, 128 - 16, 1) * m1
        return q, k, v, gb_f, gb_b

    ins = [_In(p, D, 0, halo=halo), _In(p, D, 1, halo=halo), _In(p, D, 2, halo=halo), _In(p, 128, C_BA // 128),
           _Full(conv_w), _Full(gprm)]
    return _rowcall("dn_prep", fn, n, TILE, ins, [_Out(D), _Out(D), _Out(D), _Out(128), _Out(128)])


def _dn_prep_bwd(p, conv_w, gprm, dq2, dk2, dv2, dgb2):
    n = p.shape[0] // TILE
    halo = _all_halo(n)

    def branch(xe, w, dye, scale):
        c = _conv_taps(xe, w, 5)
        sx = _silu(c)
        if scale is None:
            dsx = dye
        else:
            dsx = jnp.concatenate([_l2n_bwd(dye[:, h * HD:(h + 1) * HD], sx[:, h * HD:(h + 1) * HD], scale)
                                   for h in range(NH)], axis=1)
        dc = dsx * _dsilu(c)
        dx = None
        dws = []
        dcc = dc[HALO:HALO + TILE]
        for t in range(5):
            term = _shift(dc, 2 - t, TILE) * w[t:t + 1]
            dx = term if dx is None else dx + term
            dws.append(_colsum(dcc * _shift(xe, t - 2, TILE)))
        dw = jnp.concatenate(dws + [jnp.zeros((3, D), F32)], axis=0)
        return dx, dw

    def fn(i, j, qe, ke, ve, ba, w, gp, dq0, dq1, dk0, dk1, dv0, dv1, dg0, dg1):
        dxq, dwq = branch(qe, w[:, 0:D], dq0 + dq1, DN_Q_SCALE)
        dxk, dwk = branch(ke, w[:, D:2 * D], dk0 + dk1, 1.0)
        dxv, dwv = branch(ve, w[:, 2 * D:3 * D], dv0 + dv1, None)
        m0, m1 = _lane_mask(0, 8), _lane_mask(8, 16)
        dbeta = dg0 * m0 + pltpu.roll(dg1 * m0, 8, 1)
        dg = pltpu.roll(dg0 * m1, 8, 1) + pltpu.roll(dg1 * m1, 16, 1)
        beta = jax.nn.sigmoid(ba)
        ea = jnp.exp(gp[0:1])
        z = ba + gp[1:2]
        g = -ea * _softplus(z)
        mg = _lane_mask(16, 32)
        da = dg * (-ea) * jax.nn.sigmoid(z) * mg
        dba = dbeta * beta * (1.0 - beta) * _lane_mask(0, 16) + da
        dgp = jnp.concatenate([_colsum(dg * g * mg), _colsum(da)], axis=0)
        return (jnp.concatenate([dxq, dxk, dxv], axis=1), dba, jnp.concatenate([dwq, dwk, dwv], axis=1), dgp)

    ins = [_In(p, D, 0, halo=halo), _In(p, D, 1, halo=halo), _In(p, D, 2, halo=halo), _In(p, 128, C_BA // 128),
           _Full(conv_w), _Full(gprm),
           _In(dq2, halo=halo), _In(dq2, roff=n, halo=halo), _In(dk2, halo=halo), _In(dk2, roff=n, halo=halo),
           _In(dv2, halo=halo), _In(dv2, roff=n, halo=halo), _In(dgb2), _In(dgb2, roff=n)]
    return _rowcall("dn_prep_bwd", fn, n, TILE, ins,
                    [_Out(3 * D, BF16), _Out(128, BF16), _Out(3 * D, acc=True, rows=8), _Out(128, acc=True, rows=2)])


def _hnorm(x, w):
    return x * lax.rsqrt(_rowmean(x * x) + EPS) * w


def _hnorm_bwd(dy, x, w):
    r = lax.rsqrt(_rowmean(x * x) + EPS)
    xh = x * r
    dxh = dy * w
    return r * (dxh - xh * _rowmean(dxh * xh)), _colsum(dy * xh)


def _dn_gate(o2, p, dn_norm, n_all):
    n = n_all - CT

    def fn(i, j, of, ob, gt, w):
        o = of + ob
        return (_heads(o, lambda h, x: _hnorm(x, w)) * _silu(gt),)

    ins = [_In(o2, roff=CT), _In(o2, roff=n_all + CT), _In(p, D, C_GT // D, roff=CT), _Full(dn_norm)]
    return _rowcall("dn_gate", fn, n, TILE, ins, [_Out(D, BF16)])[0]


def _dn_gate_bwd(dy, o2, p, dn_norm, n_all):
    n = n_all - CT

    def fn(i, j, dy_, of, ob, gt, w):
        o = of + ob
        sg = _silu(gt)
        dos, dw = [], jnp.zeros((1, HD), F32)
        yn = []
        for h in range(NH):
            sl = slice(h * HD, (h + 1) * HD)
            dx, dwh = _hnorm_bwd(dy_[:, sl] * sg[:, sl], o[:, sl], w)
            dos.append(dx)
            dw = dw + dwh
            yn.append(_hnorm(o[:, sl], w))
        dgt = dy_ * jnp.concatenate(yn, axis=1) * _dsilu(gt)
        return jnp.concatenate(dos, axis=1), dgt, dw

    ins = [_In(dy), _In(o2, roff=CT), _In(o2, roff=n_all + CT), _In(p, D, C_GT // D, roff=CT), _Full(dn_norm)]
    return _rowcall("dn_gate_bwd", fn, n, TILE, ins, [_Out(D), _Out(D, BF16), _Out(HD, acc=True)])


def _rope_shuffle(x):
    lane = lax.broadcasted_iota(jnp.int32, (1, HD), 1)
    return jnp.where((lane % 64) < 32, pltpu.roll(x, HD - 32, 1), pltpu.roll(x, 32, 1))


def _rope(x, cos, sin):
    return x * cos + _rope_shuffle(x) * sin


def _rope_bwd(dy, cos, sin):
    return dy * cos + _rope_shuffle(dy * sin)


def _attn_prep(p, w, cos, sin, width, cb, roff, n, name):
    def fn(i, j, x, w_, c, s):
        return (_heads(x, lambda h, xh: _rope(_hnorm(xh, w_), c, s)),)

    ins = [_In(p, width, cb, roff=roff), _Full(w), _In(cos), _In(sin)]
    return _rowcall(name, fn, n, TILE, ins, [_Out(width)])[0]


def _attn_prep_bwd(dy, p, w, cos, sin, width, cb, roff, n, name):
    def fn(i, j, dy_, x, w_, c, s):
        dxs, dw = [], jnp.zeros((1, HD), F32)
        for h in range(width // HD):
            sl = slice(h * HD, (h + 1) * HD)
            dx, dwh = _hnorm_bwd(_rope_bwd(dy_[:, sl], c, s), x[:, sl], w_)
            dxs.append(dx)
            dw = dw + dwh
        return jnp.concatenate(dxs, axis=1), dw

    ins = [_In(dy), _In(p, width, cb, roff=roff), _Full(w), _In(cos), _In(sin)]
    return _rowcall(name, fn, n, TILE, ins, [_Out(width, BF16), _Out(HD, acc=True)])


def _merge(z_dn, z_at, p, n):
    def fn(i, j, zd, za, gd, ga):
        return (jax.nn.sigmoid(gd) * zd + jax.nn.sigmoid(ga) * za,)

    ins = [_In(z_dn), _In(z_at), _In(p, D, C_MG // D, roff=CT), _In(p, D, C_MG // D + 1, roff=CT)]
    return _rowcall("merge", fn, n, TILE, ins, [_Out(D, BF16)])[0]


def _merge_bwd(dm, z_dn, z_at, p, n):
    def fn(i, j, dm_, zd, za, gd, ga):
        sd, sa = jax.nn.sigmoid(gd), jax.nn.sigmoid(ga)
        dg = jnp.concatenate([dm_ * zd * sd * (1.0 - sd), dm_ * za * sa * (1.0 - sa)], axis=1)
        return dm_ * sd, dm_ * sa, dg

    ins = [_In(dm), _In(z_dn), _In(z_at), _In(p, D, C_MG // D, roff=CT), _In(p, D, C_MG // D + 1, roff=CT)]
    return _rowcall("merge_bwd", fn, n, TILE, ins, [_Out(D, BF16), _Out(D, BF16), _Out(2 * D, BF16)])


def _resid_norm(xa, mo, g_a, nf, mod_f, n):
    def fn(i, j, x, mo_, ga, nf_, m):
        x1 = x + ga * mo_
        return x1, _rms_mod(x1, nf_, m[0:1], m[1:2])

    ins = [_In(xa, roff=CT), _In(mo), _Full(g_a), _Full(nf), _Full(mod_f)]
    return _rowcall("resid_norm", fn, n, TILE, ins, [_Out(D), _Out(D, BF16)])


def _resid_norm_bwd(dy, dh2, x1, mo, g_a, nf, mod_f, n):
    def fn(i, j, dy_, dh_, x1_, mo_, ga, nf_, m):
        dx, dn, dsh, dsc = _rms_mod_bwd(dh_, x1_, nf_, m[1:2])
        dx1 = dy_ + dx
        return dx1, ga * dx1, dn, dsh, dsc, _colsum(dx1 * mo_)

    ins = [_In(dy), _In(dh2), _In(x1), _In(mo), _Full(g_a), _Full(nf), _Full(mod_f)]
    accs = [_Out(D, acc=True) for _ in range(4)]
    return _rowcall("resid_norm_bwd", fn, n, TILE, ins, [_Out(D), _Out(D, BF16)] + accs)


def _loss_head(x1, f, tgt, g_f, n):
    def fn(i, j, x1_, f_, t, gf):
        e = x1_ + gf * f_ - t
        dy = e * (1.0 / D)
        loss = _colsum(_rowsum(e * e)) * (0.5 / D)
        return dy, gf * dy, _colsum(dy * f_), jnp.broadcast_to(loss, (1, 128))

    ins = [_In(x1), _In(f), _In(tgt), _Full(g_f)]
    return _rowcall("loss_head", fn, n, TILE, ins, [_Out(D), _Out(D, BF16), _Out(D, acc=True), _Out(128, acc=True)])


FW = DFF // 2


def _ffn_act(u, conv_w, conv_b, n):
    halo = _lat_halo(n)

    def fn(i, j, ge, ve, wg, wv, bg, bv):
        cg = _conv_taps(ge, wg, 3, TILE) + bg
        cv = _conv_taps(ve, wv, 3, TILE) + bv
        return (_silu(cg) * cv,)

    ins = [_In(u, FW, 0, halo=halo), _In(u, FW, 2, halo=halo), _Full(conv_w, FW, 0), _Full(conv_w, FW, 2),
           _Full(conv_b, FW, 0), _Full(conv_b, FW, 2)]
    return _rowcall("ffn_act", fn, n, TILE, ins, [_Out(DFF, BF16, FW)], ncol=2)[0]


def _ffn_act_bwd(u, da, conv_w, conv_b, n):
    halo = _lat_halo(n)

    def fn(i, j, ge, ve, dae, wg, wv, bg, bv):
        cg = _conv_taps(ge, wg, 3) + bg
        cv = _conv_taps(ve, wv, 3) + bv
        dcg = dae * cv * _dsilu(cg)
        dcv = dae * _silu(cg)
        outs = []
        for dc, xe, w in ((dcg, ge, wg), (dcv, ve, wv)):
            dx, dws = None, []
            dcc = dc[HALO:HALO + TILE]
            for t in range(3):
                term = _shift(dc, 1 - t, TILE) * w[t:t + 1]
                dx = term if dx is None else dx + term
                dws.append(_colsum(dcc * _shift(xe, t - 1, TILE)))
            outs.append((dx, jnp.concatenate(dws + [jnp.zeros((5, FW), F32)], axis=0), _colsum(dcc)))
        return outs[0][0], outs[1][0], outs[0][1], outs[1][1], outs[0][2], outs[1][2]

    ins = [_In(u, FW, 0, halo=halo), _In(u, FW, 2, halo=halo), _In(da, FW, 0, halo=halo),
           _Full(conv_w, FW, 0), _Full(conv_w, FW, 2), _Full(conv_b, FW, 0), _Full(conv_b, FW, 2)]
    outs = [_Out(DFF, BF16, FW), _Out(DFF, BF16, FW), _Out(DFF, w=FW, acc=True, rows=8), _Out(DFF, w=FW, acc=True, rows=8),
            _Out(DFF, w=FW, acc=True), _Out(DFF, w=FW, acc=True)]
    return _rowcall("ffn_act_bwd", fn, n, TILE, ins, outs, ncol=2)


def _rope_tables(tl):
    t = jnp.arange(tl, dtype=jnp.int32)
    row = (t // GRID_W).astype(F32)
    col = (t % GRID_W).astype(F32)
    inv = ROPE_BASE ** (-jnp.arange(32, dtype=F32) / 32)
    ar, ac = row[:, None] * inv, col[:, None] * inv
    cos = jnp.concatenate([jnp.cos(ar), jnp.cos(ar), jnp.cos(ac), jnp.cos(ac)], axis=1)
    sin = jnp.concatenate([-jnp.sin(ar), jnp.sin(ar), -jnp.sin(ac), jnp.sin(ac)], axis=1)
    return cos, sin


def _pad_w_in(w_in):
    z = lambda n: jnp.zeros((D, n), w_in.dtype)
    return jnp.concatenate([w_in[:, 0:4096], w_in[:, 4128:5152], w_in[:, 5664:7712], w_in[:, 5152:5664],
                            w_in[:, 4096:4128], z(96 + PW - C_PAD)], axis=1)


def _unpad_w_in(g):
    return jnp.concatenate([g[:, 0:4096], g[:, C_BA:C_BA + 32], g[:, C_QAT:C_QAT + D], g[:, C_KAT:C_KAT + 512],
                            g[:, C_MG:C_MG + 2 * D]], axis=1)


def _local_step(xa, tgt, mod_x, mod_c, w, hi=False):
    t_all = xa.shape[0]
    tl = t_all - CTX
    n_all, n = t_all // TILE, tl // TILE
    tm_all = 1280 if t_all % 1280 == 0 else TILE
    tm_lat = 1024
    mm = functools.partial(_mm, hi=hi)
    sp = lambda m: [m[:, k * D:(k + 1) * D] for k in range(6)]
    sh_a, sc_a, g_a, sh_f, sc_f, g_f = sp(mod_x)
    sh_ac, sc_ac = sp(mod_c)[:2]
    mod_ax = jnp.concatenate([sh_a, sc_a], axis=0)
    mod_ac = jnp.concatenate([sh_ac, sc_ac], axis=0)
    mod_f = jnp.concatenate([sh_f, sc_f], axis=0)
    nm, nf = w["norm_mix"], w["norm_ffn"]
    cos, sin = _rope_tables(tl)
    cos_all = jnp.concatenate([jnp.ones((CTX, HD), F32), cos], axis=0)
    sin_all = jnp.concatenate([jnp.zeros((CTX, HD), F32), sin], axis=0)
    conv_dn = jnp.concatenate([w["dn_conv"], jnp.zeros((3, 3 * D), F32)], axis=0)
    gprm = jnp.concatenate([jnp.zeros((2, 16), F32),
                            jnp.concatenate([w["dn_a_log"].reshape(1, 16), w["dn_dt_bias"].reshape(1, 16)], axis=0),
                            jnp.zeros((2, 96), F32)], axis=1)
    conv_ff = jnp.concatenate([w["ffn_conv"], jnp.zeros((5, 2 * DFF), F32)], axis=0)
    sink = jnp.concatenate([w["attn_sink"].reshape(1, NH), jnp.zeros((1, 128 - NH), F32)], axis=1)
    nct = CTX // CH

    h = _norm_mod(xa, nm, mod_ac, mod_ax)
    p = mm(h, w["w_in_p"], tm=tm_all, tn=1024, name="mm_in")
    q, k, v, gb_f, gb_b = _dn_prep(p, conv_dn, gprm)
    gb = jnp.stack([gb_f, gb_b])
    dn_u, dn_w, dn_qg, dn_kd, dn_pm, dn_t = _dn_intra_fwd(q, k, v, gb, nct, hi)
    o2, s_hist, dn_vn = _dn_seq_fwd(dn_u, dn_w, dn_qg, dn_kd, dn_pm, gb, nct, hi)
    o2 = o2.reshape(2 * t_all, D)
    y_dn = _dn_gate(o2, p, w["dn_norm"], n_all)
    qr = _attn_prep(p, w["q_norm"], cos, sin, D, C_QAT // D, CT, n, "attn_prep_q")
    kr = _attn_prep(p, w["k_norm"], cos_all, sin_all, KVH * HD, C_KAT // (KVH * HD), 0, n_all, "attn_prep_k")
    vv = p[:, C_VAT:C_VAT + KVH * HD]
    o_at, lse = _attn_fwd(qr, kr, vv, sink, hi)
    z_dn = mm(y_dn, w["w_branch_dn"], tm=tm_lat, name="mm_bdn")
    z_at = mm(o_at, w["w_branch_attn"], tm=tm_lat, name="mm_bat")
    merged = _merge(z_dn, z_at, p, n)
    mo = mm(merged, w["w_out"], tm=tm_lat, name="mm_out")
    x1, h2 = _resid_norm(xa, mo, g_a, nf, mod_f, n)
    u = mm(h2, w["ffn_up"], tm=2 * tm_lat, tn=1408, name="mm_up")
    a = _ffn_act(u, conv_ff, w["ffn_conv_b"], n)
    f = mm(a, w["ffn_down"], tm=tm_lat, tk=DFF, name="mm_down")
    dy, df, dg_f, loss = _loss_head(x1, f, tgt, g_f, n)

    g = {}
    da = mm(df, w["ffn_down"], tb=True, tm=tm_lat, tn=1408, name="mm_down_dx")
    g["ffn_down"] = mm(a, df, ta=True, tm=1408, tn=1024, tk=tm_lat, name="mm_down_dw")
    du_g, du_v, dcw_g, dcw_v, dcb_g, dcb_v = _ffn_act_bwd(u, da, conv_ff, w["ffn_conv_b"], n)
    du = jnp.concatenate([du_g, du_v], axis=1)
    g["ffn_conv"] = jnp.concatenate([dcw_g, dcw_v], axis=1)[0:3]
    g["ffn_conv_b"] = jnp.concatenate([dcb_g, dcb_v], axis=1)
    dh2 = mm(du, w["ffn_up"], tb=True, tm=tm_lat, tk=1408, name="mm_up_dx")
    g["ffn_up"] = mm(h2, du, ta=True, tm=1024, tn=1408, tk=tm_lat, name="mm_up_dw")
    dx1, dmo, g["norm_ffn"], dsh_f, dsc_f, dg_a = _resid_norm_bwd(dy, dh2, x1, mo, g_a, nf, mod_f, n)
    dmerged = mm(dmo, w["w_out"], tb=True, tm=tm_lat, name="mm_out_dx")
    g["w_out"] = mm(merged, dmo, ta=True, tm=1024, tk=tm_lat, name="mm_out_dw")
    dz_dn, dz_at, dmg = _merge_bwd(dmerged, z_dn, z_at, p, n)
    dy_dn = mm(dz_dn, w["w_branch_dn"], tb=True, tm=tm_lat, name="mm_bdn_dx")
    g["w_branch_dn"] = mm(y_dn, dz_dn, ta=True, tm=1024, tk=tm_lat, name="mm_bdn_dw")
    do_at = mm(dz_at, w["w_branch_attn"], tb=True, tm=tm_lat, name="mm_bat_dx")
    g["w_branch_attn"] = mm(o_at, dz_at, ta=True, tm=1024, tk=tm_lat, name="mm_bat_dw")

    do_dn, dgt, g["dn_norm"] = _dn_gate_bwd(dy_dn, o2, p, w["dn_norm"], n_all)
    do_all = jnp.concatenate([jnp.zeros((CTX, D), F32), do_dn], axis=0)
    dn_dvn, dn_dw, dn_dqg, dn_dkd, dn_del = _dn_seq_bwd(dn_w, dn_qg, dn_kd, dn_pm, dn_vn, s_hist, gb, do_all, nct, hi)
    dq2, dk2, dv2, dgb2 = _dn_intra_bwd(q, k, v, gb, dn_u, dn_w, dn_t, dn_vn, dn_dvn, dn_dw, dn_dqg, dn_dkd, dn_del,
                                        do_all, nct, hi)
    dqkv, dba, dconv, dgprm = _dn_prep_bwd(p, conv_dn, gprm, dq2.reshape(2 * t_all, D), dk2.reshape(2 * t_all, D),
                                           dv2.reshape(2 * t_all, D), dgb2.reshape(2 * t_all, 128))
    g["dn_conv"] = dconv[0:5]
    g["dn_a_log"] = dgprm[0, 16:32].reshape(2, NH)
    g["dn_dt_bias"] = dgprm[1, 16:32].reshape(2, NH)

    delta = _attn_delta(o_at, do_at)
    dqr, dkx, dvx, dsink = _attn_bwd_q(qr, kr, vv, sink, do_at, lse, delta, hi)
    dk_lat, dv_lat = _attn_bwd_kv(qr, kr, vv, do_at, lse, delta, hi)
    g["attn_sink"] = dsink[:, 0:NH]
    dq_at, g["q_norm"] = _attn_prep_bwd(dqr, p, w["q_norm"], cos, sin, D, C_QAT // D, CT, n, "attn_prep_q_bwd")
    dkr = jnp.concatenate([dkx, dk_lat], axis=0)
    dk_at, g["k_norm"] = _attn_prep_bwd(dkr, p, w["k_norm"], cos_all, sin_all, KVH * HD, C_KAT // (KVH * HD), 0, n_all,
                                        "attn_prep_k_bwd")
    dv_at = jnp.concatenate([dvx, dv_lat], axis=0).astype(BF16)

    zc = lambda width: jnp.zeros((CTX, width), BF16)
    dp = jnp.concatenate([
        dqkv,
        jnp.concatenate([zc(D), dgt], axis=0),
        jnp.concatenate([zc(D), dq_at], axis=0),
        jnp.concatenate([zc(2 * D), dmg], axis=0),
        dk_at, dv_at, dba, jnp.zeros((t_all, PW - C_PAD), BF16)], axis=1)
    dh = mm(dp, w["w_in_p"], tb=True, tm=tm_all, tn=1024, tk=2048, name="mm_in_dx")
    g["w_in_p"] = mm(h, dp, ta=True, tm=1024, tn=2048, tk=tm_all, name="mm_in_dw")
    dnm_c, dsh_ac, dsc_ac = _norm_mod_bwd(dh, xa, None, nm, mod_ac, 0, CT)
    grad_x, dnm_x, dsh_a, dsc_a = _norm_mod_bwd(dh, xa, dx1, nm, mod_ax, CT, n)
    g["norm_mix"] = dnm_c + dnm_x
    dmod_x = jnp.concatenate([dsh_a, dsc_a, dg_a, dsh_f, dsc_f, dg_f], axis=1)
    dmod_c = jnp.concatenate([dsh_ac, dsc_ac, jnp.zeros((1, 4 * D), F32)], axis=1)
    return loss, grad_x, g, dmod_x, dmod_c


def _sum_slots(buf, n_slots, rows, tile, name, stride=1):
    nt = rows // tile

    def fn(i, j, *vals):
        acc = vals[0]
        for v in vals[1:]:
            acc = acc + v
        return (acc,)

    ins = [_In(buf, roff=k * stride * nt) for k in range(n_slots)]
    return _rowcall(name, fn, nt, tile, ins, [_Out(buf.shape[1])])[0]


ADAM_LR, ADAM_B1, ADAM_B2, ADAM_EPS, ADAM_WD, ADAM_STEP = 0.001, 0.9, 0.999, 1e-08, 0.01, 10


def _row_tile(rows, cols):
    for t in (512, 256, 128, 64, 32, 16, 8):
        if rows % t == 0 and t * cols * 4 * 14 <= 40 * 1024 * 1024:
            return t
    return rows


def _adamw(w, g, m, v, name):
    shape = w.shape
    cols = shape[-1]
    rows = max(1, math.prod(shape[:-1]))
    tile = _row_tile(rows, cols)
    c1 = 1.0 / (1.0 - ADAM_B1 ** ADAM_STEP)
    c2 = 1.0 / (1.0 - ADAM_B2 ** ADAM_STEP)

    def fn(i, j, w_, g_, m_, v_):
        mn = ADAM_B1 * m_ + (1.0 - ADAM_B1) * g_
        vn = ADAM_B2 * v_ + (1.0 - ADAM_B2) * (g_ * g_)
        delta = -ADAM_LR * ((mn * c1) / (jnp.sqrt(vn * c2) + ADAM_EPS) + ADAM_WD * w_)
        return delta, mn, vn

    r2 = lambda a: a.reshape(rows, cols)
    outs = _rowcall(name, fn, rows // tile, tile, [_In(r2(w)), _In(r2(g)), _In(r2(m)), _In(r2(v))],
                    [_Out(cols), _Out(cols), _Out(cols)])
    return [o.reshape(shape) for o in outs]


MESH = pl.DeviceIdType.MESH
ANY = pl.BlockSpec(memory_space=pl.ANY)


def _pos():
    return lax.axis_index("x"), lax.axis_index("y"), lax.axis_index("c")


def _all_gather(blk, name):
    m_per, n = blk.shape

    def body(x_ref, out_ref, send_sems, recv_sems, local_sem):
        x, y, c = _pos()
        me, sibling = (x, y, c), (x, y, 1 - c)
        chips = [(1 - x, y), (x, 1 - y), (1 - x, 1 - y)]

        def rows(px, py, pc):
            return out_ref.at[pl.ds(pl.multiple_of((4 * px + 2 * py + pc) * m_per, 8), m_per), :]

        def copy(k, block, to, src=None):
            return pltpu.make_async_remote_copy(
                src_ref=rows(*block) if src is None else src, dst_ref=rows(*block),
                send_sem=send_sems.at[k], recv_sem=recv_sems.at[k], device_id=to, device_id_type=MESH)

        mine = pltpu.make_async_copy(x_ref, rows(*me), local_sem)
        mine.start()
        first = [copy(0, me, sibling, src=x_ref)]
        first += [copy(1 + j, me, (*chip, c), src=x_ref) for j, chip in enumerate(chips)]
        for cp in first:
            cp.start()
        passed = [copy(4 + j, (*chip, c), sibling) for j, chip in enumerate(chips)]
        for j, chip in enumerate(chips):
            copy(1 + j, (*chip, c), me).wait_recv()
            passed[j].start()
        copy(0, sibling, me).wait_recv()
        for j, chip in enumerate(chips):
            copy(4 + j, (*chip, 1 - c), me).wait_recv()
        for cp in first + passed:
            cp.wait_send()
        mine.wait()

    return pl.pallas_call(
        body, name=name, out_shape=jax.ShapeDtypeStruct((N_DEV * m_per, n), blk.dtype),
        in_specs=[ANY], out_specs=ANY,
        scratch_shapes=[pltpu.SemaphoreType.DMA((7,)), pltpu.SemaphoreType.DMA((7,)), pltpu.SemaphoreType.DMA],
        compiler_params=pltpu.CompilerParams(has_side_effects=True),
    )(blk)


def _flip(v, bit):
    return 1 - v if bit else v


D2D_STREAMS = 8
ICI_STREAMS = 2


def _sibling_exchange(src, rows, other_half, name):
    n = src.shape[1]
    per = rows // D2D_STREAMS
    assert per * D2D_STREAMS == rows and per % 16 == 0

    def body(x_ref, out_ref, send_sems, recv_sems):
        x, y, c = _pos()
        base = (1 - c) * rows if other_half else 0
        copies = []
        for j in range(D2D_STREAMS):
            cp = pltpu.make_async_remote_copy(
                src_ref=x_ref.at[pl.ds(pl.multiple_of(base + j * per, 16), per), :],
                dst_ref=out_ref.at[pl.ds(j * per, per), :],
                send_sem=send_sems.at[j], recv_sem=recv_sems.at[j], device_id=(x, y, 1 - c), device_id_type=MESH)
            cp.start()
            copies.append(cp)
        for cp in copies:
            cp.wait_recv()
        for cp in copies:
            cp.wait_send()

    return pl.pallas_call(
        body, name=name, out_shape=jax.ShapeDtypeStruct((rows, n), src.dtype),
        in_specs=[ANY], out_specs=ANY,
        scratch_shapes=[pltpu.SemaphoreType.DMA((D2D_STREAMS,)), pltpu.SemaphoreType.DMA((D2D_STREAMS,))],
        compiler_params=pltpu.CompilerParams(has_side_effects=True),
    )(src)


def _chip_exchange(buf, rows, name):
    n = buf.shape[1]
    per = rows // ICI_STREAMS
    assert per * ICI_STREAMS == rows and per % 16 == 0

    def body(x_ref, out_ref, send_sems, recv_sems):
        x, y, c = _pos()
        copies = []
        for k in range(1, 4):
            px, py = _flip(x, k & 2), _flip(y, k & 1)
            for j in range(ICI_STREAMS):
                i = (k - 1) * ICI_STREAMS + j
                cp = pltpu.make_async_remote_copy(
                    src_ref=x_ref.at[pl.ds(pl.multiple_of((2 * px + py) * rows + j * per, 16), per), :],
                    dst_ref=out_ref.at[pl.ds((k - 1) * rows + j * per, per), :],
                    send_sem=send_sems.at[i], recv_sem=recv_sems.at[i], device_id=(px, py, c), device_id_type=MESH)
                cp.start()
                copies.append(cp)
        for cp in copies:
            cp.wait_recv()
        for cp in copies:
            cp.wait_send()

    return pl.pallas_call(
        body, name=name, out_shape=jax.ShapeDtypeStruct((3 * rows, n), buf.dtype),
        in_specs=[ANY], out_specs=ANY,
        scratch_shapes=[pltpu.SemaphoreType.DMA((3 * ICI_STREAMS,)), pltpu.SemaphoreType.DMA((3 * ICI_STREAMS,))],
        compiler_params=pltpu.CompilerParams(has_side_effects=True),
    )(buf)


def _add_rows(parts, rows, dtype, name):
    tile = 1024
    ins = [_In(a, roff=r0 // tile) for a, r0 in parts]

    def fn(i, j, *vals):
        acc = vals[0].astype(F32)
        for v_ in vals[1:]:
            acc = acc + v_.astype(F32)
        return (acc,)

    return _rowcall(name, fn, rows // tile, tile, ins, [_Out(parts[0][0].shape[1], dtype)])[0]


BIG = ("w_in", "w_branch_dn", "w_branch_attn", "w_out", "ffn_up", "ffn_down")
BIG_SHARD = {"w_in": (1024, 1928, True), "w_branch_dn": (256, 1024, False), "w_branch_attn": (256, 1024, False),
             "w_out": (256, 1024, False), "ffn_up": (1024, 1408, True), "ffn_down": (704, 1024, False)}
BIG_ROWS = {k: r * c // 2 // 128 for k, (r, c, _) in BIG_SHARD.items()}
PIECE = 19456
assert sum(BIG_ROWS.values()) <= PIECE


def _pack_half(shards, ci, dtype):
    parts = []
    for k in BIG:
        r, c, _ = BIG_SHARD[k]
        parts.append(lax.dynamic_slice_in_dim(shards[k], ci * (r // 2), r // 2, axis=0).reshape(-1, 128).astype(dtype))
    parts.append(jnp.zeros((PIECE - sum(BIG_ROWS.values()), 128), dtype))
    return jnp.concatenate(parts, axis=0)


def _unpack_full(ag):
    out, off = {}, 0
    for k in BIG:
        r, c, by_col = BIG_SHARD[k]
        blk = ag[:, off:off + BIG_ROWS[k]].reshape(4, r, c)
        out[k] = jnp.transpose(blk, (1, 0, 2)).reshape(r, 4 * c) if by_col else blk.reshape(4 * r, c)
        off += BIG_ROWS[k]
    return out


def _pack_pieces(full):
    parts = []
    for k in BIG:
        r, c, by_col = BIG_SHARD[k]
        a = full[k]
        if by_col:
            a = jnp.transpose(a.reshape(2, r // 2, 4, c), (0, 2, 1, 3))
        else:
            a = jnp.transpose(a.reshape(4, 2, r // 2, c), (1, 0, 2, 3))
        parts.append(a.reshape(N_DEV, BIG_ROWS[k], 128).astype(BF16))
    parts.append(jnp.zeros((N_DEV, PIECE - sum(BIG_ROWS.values()), 128), BF16))
    return jnp.concatenate(parts, axis=1).reshape(N_DEV * PIECE, 128)


def _reduce_scatter(pieces, ci, shard):
    half = N_DEV // 2 * PIECE
    theirs = _sibling_exchange(pieces, half, True, "rs_d2d")
    own = lax.dynamic_slice_in_dim(pieces, ci * half, half, axis=0)
    part = _add_rows([(own, 0), (theirs, 0)], half, BF16, "rs_sum_chip")
    recv = _chip_exchange(part, PIECE, "rs_ici")
    own2 = lax.dynamic_slice_in_dim(part, shard * PIECE, PIECE, axis=0)
    mine = _add_rows([(own2, 0), (recv, 0), (recv, PIECE), (recv, 2 * PIECE)], PIECE, F32, "rs_sum_all")
    other = _sibling_exchange(mine, PIECE, False, "rs_pair")
    return jnp.where(ci == 0, jnp.stack([mine, other]), jnp.stack([other, mine]))


def _unpack_shard(two):
    out, off = {}, 0
    for k in BIG:
        r, c, _ = BIG_SHARD[k]
        out[k] = two[:, off:off + BIG_ROWS[k]].reshape(r, c)
        off += BIG_ROWS[k]
    return out


SMALL = (("dn_conv", 120), ("ffn_conv", 132), ("ffn_conv_b", 44), ("norm_mix", 8), ("norm_ffn", 8), ("dn_a_log", 1),
         ("dn_dt_bias", 1), ("dn_norm", 1), ("q_norm", 1), ("k_norm", 1), ("attn_sink", 1), ("dmod_c", 48), ("dmod_x", 48))
SMALL_ROWS = 416


def _rows128(a, rows):
    flat = a.reshape(-1)
    return jnp.concatenate([flat, jnp.zeros((rows * 128 - flat.shape[0],), F32)]).reshape(rows, 128)


def _pack_small(g):
    parts = [_rows128(g[k], r) for k, r in SMALL]
    parts.append(jnp.zeros((SMALL_ROWS - sum(r for _, r in SMALL), 128), F32))
    return jnp.concatenate(parts, axis=0)


def _unpack_small(buf, shapes):
    out, off = {}, 0
    for k, r in SMALL:
        n = math.prod(shapes[k])
        out[k] = buf[off:off + r].reshape(-1)[:n].reshape(shapes[k])
        off += r
    return out


WEIGHTS = ("c_ctx", "w_ada", "b_ada", "norm_mix", "norm_ffn", "w_in", "dn_conv", "dn_a_log", "dn_dt_bias", "dn_norm",
           "q_norm", "k_norm", "attn_sink", "w_branch_dn", "w_branch_attn", "w_out", "ffn_up", "ffn_conv", "ffn_conv_b",
           "ffn_down")


def kernel(x, c, ctx, c_ctx, w_ada, b_ada, norm_mix, norm_ffn, w_in, dn_conv, dn_a_log, dn_dt_bias, dn_norm, q_norm, k_norm, attn_sink, w_branch_dn, w_branch_attn, w_out, ffn_up, ffn_conv, ffn_conv_b, ffn_down, loss_target, m_c_ctx, m_w_ada, m_b_ada, m_norm_mix, m_norm_ffn, m_w_in, m_dn_conv, m_dn_a_log, m_dn_dt_bias, m_dn_norm, m_q_norm, m_k_norm, m_attn_sink, m_w_branch_dn, m_w_branch_attn, m_w_out, m_ffn_up, m_ffn_conv, m_ffn_conv_b, m_ffn_down, v_c_ctx, v_w_ada, v_b_ada, v_norm_mix, v_norm_ffn, v_w_in, v_dn_conv, v_dn_a_log, v_dn_dt_bias, v_dn_norm, v_q_norm, v_k_norm, v_attn_sink, v_w_branch_dn, v_w_branch_attn, v_w_out, v_ffn_up, v_ffn_conv, v_ffn_conv_b, v_ffn_down):
    args = dict(locals())
    xi, yi, ci = _pos()
    dev = 4 * xi + 2 * yi + ci
    shard = 2 * xi + yi
    chips = lambda a: a[0::2]

    blk = jnp.concatenate([_rows128(c, 8), _rows128(dn_conv, 30), _rows128(ffn_conv, 33), jnp.zeros((1, 128), F32)], axis=0)
    ag = _all_gather(blk, "ag_small_in").reshape(N_DEV, 72, 128)
    c_all = ag[:, 0:8].reshape(N_DEV, D)
    dn_conv_full = jnp.transpose(chips(ag)[:, 8:38].reshape(4, 5, 768), (1, 0, 2)).reshape(5, 3 * D)
    ffn_conv_full = jnp.transpose(chips(ag)[:, 38:71].reshape(4, 3, 1408), (1, 0, 2)).reshape(3, 2 * DFF)

    c16 = jnp.concatenate([c_all, c_ctx[None], jnp.zeros((7, D), F32)], axis=0)
    a16 = _rowcall("ada_silu", lambda i, j, v: (_silu(v),), 1, 16, [_In(c16)], [_Out(D)])[0]
    m_sh = _mm(a16, w_ada[0], tm=16, tn=512, tk=D, name="ada_fwd", hi=True)
    mod16 = chips(_all_gather(m_sh, "ag_mod").reshape(N_DEV, 16, 1536))
    mod16 = jnp.transpose(mod16, (1, 0, 2)).reshape(16, 6 * D) + b_ada
    mod_x = lax.dynamic_slice_in_dim(mod16, dev, 1, axis=0)
    mod_c = mod16[8:9]

    shards = {k: args[k][0] for k in BIG}
    wfull = _unpack_full(_all_gather(_pack_half(shards, ci, BF16), "ag_weights").reshape(N_DEV, PIECE, 128))
    w = dict(wfull)
    w["w_in_p"] = _pad_w_in(wfull["w_in"])
    w.update(norm_mix=norm_mix, norm_ffn=norm_ffn, dn_conv=dn_conv_full, dn_a_log=dn_a_log[0], dn_dt_bias=dn_dt_bias[0],
             dn_norm=dn_norm, q_norm=q_norm, k_norm=k_norm, attn_sink=attn_sink, ffn_conv=ffn_conv_full, ffn_conv_b=ffn_conv_b)

    xa = jnp.concatenate([ctx[0], x[0]], axis=0)
    loss_part, grad_x, g, dmod_x, dmod_c = _local_step(xa, loss_target[0], mod_x, mod_c, w)
    loss = lax.psum(loss_part[0, 0], ("x", "y", "c"))

    g["w_in"] = _unpad_w_in(g["w_in_p"])
    gshard = _unpack_shard(_reduce_scatter(_pack_pieces(g), ci, shard))

    g["dmod_c"], g["dmod_x"] = dmod_c, dmod_x
    ag_s = _all_gather(_pack_small(g), "ag_small_grads")
    shapes = {k: g[k].shape for k, _ in SMALL}
    gs = _unpack_small(_sum_slots(ag_s, N_DEV, SMALL_ROWS, SMALL_ROWS, "small_sum"), shapes)
    dx_all = ag_s.reshape(N_DEV, SMALL_ROWS, 128)[:, SMALL_ROWS - 50:SMALL_ROWS - 2].reshape(N_DEV, 6 * D)

    d16 = jnp.concatenate([dx_all, gs["dmod_c"], jnp.zeros((7, 6 * D), F32)], axis=0)
    d16_sh = lax.dynamic_slice_in_dim(d16, shard * 1536, 1536, axis=1)
    g_w_ada = _mm(a16, d16_sh, ta=True, tm=D, tn=512, tk=16, name="ada_dw", hi=True)
    g_b_ada = _rowcall("ada_db", lambda i, j, v: (_colsum(v),), 1, 16, [_In(d16)], [_Out(6 * D, acc=True)])[0]
    da_part = _mm(d16_sh, w_ada[0], tb=True, tm=16, tn=D, tk=512, name="ada_dx", hi=True)
    da_all = _all_gather(da_part, "ag_ada_dx")
    da16 = _sum_slots(da_all, 4, 16, 16, "ada_dx_sum", stride=2)
    dc16 = _rowcall("ada_dsilu", lambda i, j, d_, v: (d_ * _dsilu(v),), 1, 16, [_In(da16), _In(c16)], [_Out(D)])[0]

    grads = {
        "c_ctx": dc16[8], "w_ada": g_w_ada[None], "b_ada": g_b_ada, "norm_mix": gs["norm_mix"], "norm_ffn": gs["norm_ffn"],
        "w_in": gshard["w_in"][None],
        "dn_conv": lax.dynamic_slice_in_dim(gs["dn_conv"], shard * 768, 768, axis=1)[None],
        "dn_a_log": gs["dn_a_log"][None], "dn_dt_bias": gs["dn_dt_bias"][None], "dn_norm": gs["dn_norm"],
        "q_norm": gs["q_norm"], "k_norm": gs["k_norm"], "attn_sink": gs["attn_sink"],
        "w_branch_dn": gshard["w_branch_dn"][None], "w_branch_attn": gshard["w_branch_attn"][None],
        "w_out": gshard["w_out"][None], "ffn_up": gshard["ffn_up"][None],
        "ffn_conv": lax.dynamic_slice_in_dim(gs["ffn_conv"], shard * 1408, 1408, axis=1)[None],
        "ffn_conv_b": gs["ffn_conv_b"], "ffn_down": gshard["ffn_down"][None],
    }
    deltas, new_m, new_v = [], [], []
    for k in WEIGHTS:
        d_, m_, v_ = _adamw(args[k], grads[k], args["m_" + k], args["v_" + k], "adamw_" + k)
        deltas.append(d_)
        new_m.append(m_)
        new_v.append(v_)
    return (loss, grad_x[None], *[grads[k] for k in WEIGHTS], *deltas, *new_m, *new_v)
```

```python
import functools
import math

import numpy as np
import jax
import jax.numpy as jnp
from jax import lax
from jax.experimental import pallas as pl
from jax.experimental.pallas import tpu as pltpu

F32 = jnp.float32
BF16 = jnp.bfloat16
HI = lax.Precision.HIGHEST

D = 1024
NH = 8
HD = 128
CH = 64
CTX = 256
AB = 128
KVH = 2
GRP = 4
DFF = 2816
EPS = 1e-6
GRID_W = 64
ROPE_BASE = 10000.0
N_DEV = 8
VMEM_LIMIT = 56 * 1024 * 1024

C_QKV, C_GT, C_QAT, C_MG, C_KAT, C_VAT, C_BA, C_PAD = 0, 3072, 4096, 5120, 7168, 7424, 7680, 7808
PW = 8192


def _cparams(sem=None, **kw):
    return pltpu.CompilerParams(dimension_semantics=sem, vmem_limit_bytes=VMEM_LIMIT, **kw)


def _dot(a, b, dims, hi):
    if hi:
        return lax.dot_general(a.astype(F32), b.astype(F32), (dims, ((), ())), precision=HI, preferred_element_type=F32)
    return lax.dot_general(a.astype(BF16), b.astype(BF16), (dims, ((), ())), preferred_element_type=F32)


NN = ((1,), (0,))
NT = ((1,), (1,))
TN = ((0,), (0,))


def _dn_masks():
    i = np.arange(CH)
    lo_incl = (i[:, None] >= i[None, :]).astype(np.float32)
    lo_strict = (i[:, None] > i[None, :]).astype(np.float32)
    return jnp.asarray(np.stack([np.stack([lo_incl, lo_strict]), np.stack([lo_incl.T, lo_strict.T])]))


def _dn_chunk_index(d, i, n_ctx_chunks, n_chunks):
    fwd = i
    bwd = jnp.where(i < n_ctx_chunks, n_ctx_chunks - 1 - i, n_chunks - 1 + n_ctx_chunks - i)
    return jnp.where(d == 0, fwd, bwd)


BNN = ((2,), (1,))
BNT = ((2,), (2,))
BTN = ((1,), (1,))


def _bdot(a, b, dims, hi):
    dn = (dims, ((0,), (0,)))
    if hi:
        return lax.dot_general(a.astype(F32), b.astype(F32), dn, precision=HI, preferred_element_type=F32)
    return lax.dot_general(a.astype(BF16), b.astype(BF16), dn, preferred_element_type=F32)


def _bdot3(a, b, dims, hi):
    if hi:
        return _bdot(a, b, dims, True)
    ah, bh = a.astype(BF16), b.astype(BF16)
    al, bl = (a - ah.astype(F32)).astype(BF16), (b - bh.astype(F32)).astype(BF16)
    dn = (dims, ((0,), (0,)))
    d = lambda x_, y_: lax.dot_general(x_, y_, dn, preferred_element_type=F32)
    return d(ah, bh) + d(ah, bl) + d(al, bh)


def _dn_heads(ref):
    return jnp.stack([ref[:, h * HD:(h + 1) * HD] for h in range(NH)])


def _dn_scalars(gb, mi):
    gcum, gcum_t, gtot = _dn_gcum(gb, mi)
    beta = jnp.stack([gb[:, h:h + 1] for h in range(NH)])
    gc = jnp.stack([gcum[:, NH + h:NH + h + 1] for h in range(NH)])
    gcr = jnp.stack([gcum_t[NH + h:NH + h + 1, :] for h in range(NH)])
    gt = jnp.stack([gtot[:, NH + h:NH + h + 1] for h in range(NH)])
    return beta, gc, gcr, gt


def _dn_total(gb):
    gtot = jnp.sum(gb, axis=0, keepdims=True)
    return jnp.stack([gtot[:, NH + h:NH + h + 1] for h in range(NH)])


def _dn_gcum(gb, mi):
    gcum = _dot(mi, gb, NN, True)
    gtot = jnp.sum(gb, axis=0, keepdims=True)
    return gcum, gcum.T, gtot


def _dn_specs(n_ctx_chunks, n_chunks, reverse):
    def cidx(d, i):
        return _dn_chunk_index(d, n_chunks - 1 - i if reverse else i, n_ctx_chunks, n_chunks)

    tok = pl.BlockSpec((CH, D), lambda d, i: (cidx(d, i), 0))
    tok_d = pl.BlockSpec((1, CH, D), lambda d, i: (d, cidx(d, i), 0))
    gbs = pl.BlockSpec((1, CH, 128), lambda d, i: (d, cidx(d, i), 0))
    msk = pl.BlockSpec((1, 2, CH, CH), lambda d, i: (d, 0, 0, 0))

    def per_chunk(*tail):
        return pl.BlockSpec((1, 1) + tail, lambda d, i: (d, cidx(d, i)) + (0,) * len(tail))

    return tok, tok_d, gbs, msk, per_chunk


def _dn_intra_fwd(q, k, v, gb, n_ctx_chunks, hi):
    t_all = q.shape[0]
    n_chunks = t_all // CH
    masks = _dn_masks()

    def body(q_ref, k_ref, v_ref, gb_ref, m_ref, u_ref, w_ref, qg_ref, kd_ref, pm_ref, t_ref):
        mi, ms = m_ref[0, 0], m_ref[0, 1]
        beta, gc, gcr, gt = _dn_scalars(gb_ref[0], mi)
        q_, k_, v_ = _dn_heads(q_ref), _dn_heads(k_ref), _dn_heads(v_ref)
        decay = jnp.exp(jnp.where(mi > 0, gc - gcr, 0.0)) * mi
        e = jnp.exp(gc)
        a = ms * (beta * _bdot(k_, k_, BNT, hi) * decay)
        x = -a
        eye = (lax.broadcasted_iota(jnp.int32, (CH, CH), 0) == lax.broadcasted_iota(jnp.int32, (CH, CH), 1)).astype(F32)
        t = eye + x
        p = x
        for _ in range(5):
            p = _bdot3(p, p, BNN, hi)
            t = t + _bdot3(t, p, BNN, hi)
        uw = _bdot(t, jnp.concatenate([beta * v_, (beta * e) * k_], axis=2), BNN, hi)
        u_ref[0, 0] = uw[:, :, :HD]
        w_ref[0, 0] = uw[:, :, HD:].astype(w_ref.dtype)
        qg_ref[0, 0] = (e * q_).astype(qg_ref.dtype)
        kd_ref[0, 0] = (jnp.exp(gt - gc) * k_).astype(kd_ref.dtype)
        pm_ref[0, 0] = (_bdot(q_, k_, BNT, hi) * decay).astype(pm_ref.dtype)
        t_ref[0, 0] = t.astype(t_ref.dtype)

    tok, _, gbs, msk, per_chunk = _dn_specs(n_ctx_chunks, n_chunks, False)
    big = lambda dt: jax.ShapeDtypeStruct((2, n_chunks, NH, CH, HD), dt)
    sq = jax.ShapeDtypeStruct((2, n_chunks, NH, CH, CH), BF16)
    return pl.pallas_call(
        body, name="dn_intra_fwd", grid=(2, n_chunks),
        in_specs=[tok, tok, tok, gbs, msk],
        out_specs=[per_chunk(NH, CH, HD)] * 4 + [per_chunk(NH, CH, CH)] * 2,
        out_shape=[big(F32), big(BF16), big(BF16), big(BF16), sq, sq],
        compiler_params=_cparams(("parallel", "parallel")),
    )(q, k, v, gb, masks)


def _dn_seq_fwd(u, w, qg, kd, pm, gb, n_ctx_chunks, hi):
    n_chunks = u.shape[1]
    t_all = n_chunks * CH

    def body(u_ref, w_ref, qg_ref, kd_ref, pm_ref, gb_ref, o_ref, sh_ref, vn_ref, s_scr):
        @pl.when(pl.program_id(1) == 0)
        def _():
            s_scr[...] = jnp.zeros_like(s_scr)

        s = s_scr[...]
        sh_ref[0, 0] = s
        vn = u_ref[0, 0] - _bdot(w_ref[0, 0], s, BNN, hi)
        o = _bdot(qg_ref[0, 0], s, BNN, hi) + _bdot(pm_ref[0, 0], vn, BNN, hi)
        s_scr[...] = jnp.exp(_dn_total(gb_ref[0])) * s + _bdot(kd_ref[0, 0], vn, BTN, hi)
        vn_ref[0, 0] = vn.astype(vn_ref.dtype)
        for h in range(NH):
            o_ref[0, :, h * HD:(h + 1) * HD] = o[h]

    _, tok_d, gbs, _, per_chunk = _dn_specs(n_ctx_chunks, n_chunks, False)
    big = per_chunk(NH, CH, HD)
    return pl.pallas_call(
        body, name="dn_seq_fwd", grid=(2, n_chunks),
        in_specs=[big, big, big, big, per_chunk(NH, CH, CH), gbs],
        out_specs=[tok_d, per_chunk(NH, HD, HD), big],
        out_shape=[jax.ShapeDtypeStruct((2, t_all, D), F32), jax.ShapeDtypeStruct((2, n_chunks, NH, HD, HD), F32),
                   jax.ShapeDtypeStruct((2, n_chunks, NH, CH, HD), BF16)],
        scratch_shapes=[pltpu.VMEM((NH, HD, HD), F32)],
        compiler_params=_cparams(("parallel", "arbitrary")),
    )(u, w, qg, kd, pm, gb)


def _dn_seq_bwd(w, qg, kd, pm, vn, s_hist, gb, do, n_ctx_chunks, hi):
    n_chunks = w.shape[1]

    def body(w_ref, qg_ref, kd_ref, pm_ref, vn_ref, sh_ref, gb_ref, do_ref, dvn_ref, dw_ref, dqg_ref, dkd_ref, del_ref, ds_scr):
        @pl.when(pl.program_id(1) == 0)
        def _():
            ds_scr[...] = jnp.zeros_like(ds_scr)

        dsn = ds_scr[...]
        s = sh_ref[0, 0]
        do_ = _dn_heads(do_ref)
        dvn = _bdot(pm_ref[0, 0], do_, BTN, hi) + _bdot(kd_ref[0, 0], dsn, BNN, hi)
        ds_scr[...] = (_bdot(qg_ref[0, 0], do_, BTN, hi) + jnp.exp(_dn_total(gb_ref[0])) * dsn
                       - _bdot(w_ref[0, 0], dvn, BTN, hi))
        dvn_ref[0, 0] = dvn.astype(dvn_ref.dtype)
        dw_ref[0, 0] = (-_bdot(dvn, s, BNT, hi)).astype(dw_ref.dtype)
        dqg_ref[0, 0] = _bdot(do_, s, BNT, hi)
        dkd_ref[0, 0] = _bdot(vn_ref[0, 0], dsn, BNT, hi)
        del_ref[0, 0] = jnp.broadcast_to(jnp.sum(jnp.sum(s * dsn, axis=2, keepdims=True), axis=1, keepdims=True),
                                         (NH, 1, 128))

    tok, _, gbs, _, per_chunk = _dn_specs(n_ctx_chunks, n_chunks, True)
    big = per_chunk(NH, CH, HD)
    shp = lambda dt: jax.ShapeDtypeStruct((2, n_chunks, NH, CH, HD), dt)
    return pl.pallas_call(
        body, name="dn_seq_bwd", grid=(2, n_chunks),
        in_specs=[big, big, big, per_chunk(NH, CH, CH), big, per_chunk(NH, HD, HD), gbs, tok],
        out_specs=[big, big, big, big, per_chunk(NH, 1, 128)],
        out_shape=[shp(BF16), shp(BF16), shp(F32), shp(F32), jax.ShapeDtypeStruct((2, n_chunks, NH, 1, 128), F32)],
        scratch_shapes=[pltpu.VMEM((NH, HD, HD), F32)],
        compiler_params=_cparams(("parallel", "arbitrary")),
    )(w, qg, kd, pm, vn, s_hist, gb, do)


def _dn_intra_bwd(q, k, v, gb, u, w, t, vn, dvn, dw, dqg, dkd, de_last, do, n_ctx_chunks, hi):
    t_all = q.shape[0]
    n_chunks = t_all // CH
    masks = _dn_masks()

    def body(q_ref, k_ref, v_ref, gb_ref, m_ref, u_ref, w_ref, t_ref, vn_ref, dvn_ref, dw_ref, dqg_ref, dkd_ref, del_ref,
             do_ref, dq_ref, dk_ref, dv_ref, dgb_ref):
        mi, ms = m_ref[0, 0], m_ref[0, 1]
        beta, gc, gcr, gt = _dn_scalars(gb_ref[0], mi)
        q_, k_, v_, do_ = _dn_heads(q_ref), _dn_heads(k_ref), _dn_heads(v_ref), _dn_heads(do_ref)
        decay = jnp.exp(jnp.where(mi > 0, gc - gcr, 0.0)) * mi
        e = jnp.exp(gc)
        e_last = jnp.exp(gt)
        kdfac = jnp.exp(gt - gc)
        kk = _bdot(k_, k_, BNT, hi)
        a = ms * (beta * kk * decay)
        pm = _bdot(q_, k_, BNT, hi) * decay
        kd = kdfac * k_
        dqg, dkd = dqg_ref[0, 0], dkd_ref[0, 0]
        dpm = _bdot(do_, vn_ref[0, 0], BNT, hi)
        dvbkb = _bdot(t_ref[0, 0], jnp.concatenate([dvn_ref[0, 0], dw_ref[0, 0]], axis=2), BTN, hi)
        dvb, dkb = dvbkb[:, :, :HD], dvbkb[:, :, HD:]
        da = -ms * _bdot(dvbkb, jnp.concatenate([u_ref[0, 0], w_ref[0, 0].astype(F32)], axis=2), BNT, hi)
        dqk = dpm * decay
        gm = dpm * pm + da * a
        dgc = (jnp.sum(gm, axis=2, keepdims=True)
               - _bdot3(gm, jnp.ones((NH, CH, 128), F32), BTN, hi)[:, :, 0:1])
        dkk = da * (beta * decay)
        dbeta = jnp.sum(da * kk * decay, axis=2, keepdims=True)
        dk = _bdot(dkk, k_, BNN, hi) + _bdot(dkk, k_, BTN, hi) + _bdot(dqk, q_, BTN, hi)
        dq = _bdot(dqk, k_, BNN, hi) + e * dqg
        de = jnp.sum(dqg * q_, axis=2, keepdims=True)
        dv = beta * dvb
        dbeta = dbeta + jnp.sum(dvb * v_, axis=2, keepdims=True)
        skb = jnp.sum(dkb * k_, axis=2, keepdims=True)
        dk = dk + (beta * e) * dkb + kdfac * dkd
        dbeta = dbeta + e * skb
        de = de + beta * skb
        skd = jnp.sum(dkd * kd, axis=2, keepdims=True)
        dgc = dgc - skd + de * e
        dgtot = jnp.sum(skd, axis=1, keepdims=True) + del_ref[0, 0][:, :, 0:1] * e_last
        lane = lax.broadcasted_iota(jnp.int32, (1, 128), 1)
        dbeta_all = jnp.zeros((CH, 128), F32)
        dgc_all = jnp.zeros((CH, 128), F32)
        dgtot_all = jnp.zeros((1, 128), F32)
        for h in range(NH):
            sl = slice(h * HD, (h + 1) * HD)
            dq_ref[0, :, sl] = dq[h]
            dk_ref[0, :, sl] = dk[h]
            dv_ref[0, :, sl] = dv[h]
            hot_b = (lane == h).astype(F32)
            hot_g = (lane == NH + h).astype(F32)
            dbeta_all = dbeta_all + dbeta[h] * hot_b
            dgc_all = dgc_all + dgc[h] * hot_g
            dgtot_all = dgtot_all + dgtot[h] * hot_g
        dgb_ref[0] = dbeta_all + _dot(mi, dgc_all, TN, True) + dgtot_all

    tok, tok_d, gbs, msk, per_chunk = _dn_specs(n_ctx_chunks, n_chunks, False)
    big = per_chunk(NH, CH, HD)
    return pl.pallas_call(
        body, name="dn_intra_bwd", grid=(2, n_chunks),
        in_specs=[tok, tok, tok, gbs, msk, big, big, per_chunk(NH, CH, CH), big, big, big, big, big,
                  per_chunk(NH, 1, 128), tok],
        out_specs=[tok_d, tok_d, tok_d, gbs],
        out_shape=[jax.ShapeDtypeStruct((2, t_all, D), F32)] * 3 + [jax.ShapeDtypeStruct((2, t_all, 128), F32)],
        compiler_params=_cparams(("parallel", "parallel")),
    )(q, k, v, gb, masks, u, w, t, vn, dvn, dw, dqg, dkd, de_last, do)


ATT_SCALE = HD ** -0.5
NEG = -1e30


def _att_stack(ref, kvh):
    return jnp.concatenate([ref[:, (kvh * GRP + g) * HD:(kvh * GRP + g + 1) * HD] for g in range(GRP)], axis=0)


def _att_col(ref, kvh):
    return jnp.concatenate([ref[:, kvh * GRP + g:kvh * GRP + g + 1] for g in range(GRP)], axis=0)


def _att_sink(sink_ref, kvh):
    return jnp.concatenate([jnp.broadcast_to(sink_ref[:, kvh * GRP + g:kvh * GRP + g + 1], (AB, 1)) for g in range(GRP)],
                           axis=0)


def _att_mask(i, nb):
    r = lax.broadcasted_iota(jnp.int32, (AB, AB), 0)
    c = lax.broadcasted_iota(jnp.int32, (AB, AB), 1)
    okp = jnp.logical_and(c >= r, i > 0)
    okn = jnp.logical_and(c <= r, i < nb - 1)
    m = jnp.concatenate([okp, jnp.ones((AB, AB), jnp.bool_), okn, jnp.ones((AB, CTX), jnp.bool_)], axis=1)
    return jnp.concatenate([m] * GRP, axis=0)


def _att_kspecs(nb):
    nc = CTX // AB
    return [pl.BlockSpec((AB, KVH * HD), lambda i: (jnp.maximum(i - 1, 0) + nc, 0)),
            pl.BlockSpec((AB, KVH * HD), lambda i: (i + nc, 0)),
            pl.BlockSpec((AB, KVH * HD), lambda i: (jnp.minimum(i + 1, nb - 1) + nc, 0)),
            pl.BlockSpec((CTX, KVH * HD), lambda i: (0, 0))]


def _attn_fwd(qr, kr, vv, sink, hi):
    tl = qr.shape[0]
    nb = tl // AB

    def body(q_ref, kp_ref, kc_ref, kn_ref, kx_ref, vp_ref, vc_ref, vn_ref, vx_ref, sink_ref, o_ref, lse_ref):
        i = pl.program_id(0)
        mask = _att_mask(i, nb)
        lane = lax.broadcasted_iota(jnp.int32, (1, 128), 1)
        lse_all = jnp.zeros((AB, 128), F32)
        for kvh in range(KVH):
            ksl = slice(kvh * HD, (kvh + 1) * HD)
            kall = jnp.concatenate([kp_ref[:, ksl], kc_ref[:, ksl], kn_ref[:, ksl], kx_ref[:, ksl]], axis=0)
            vall = jnp.concatenate([vp_ref[:, ksl], vc_ref[:, ksl], vn_ref[:, ksl], vx_ref[:, ksl]], axis=0)
            s = _dot(_att_stack(q_ref, kvh), kall, NT, hi) * ATT_SCALE
            s = jnp.where(mask, s, NEG)
            sk = _att_sink(sink_ref, kvh)
            m = jnp.maximum(jnp.max(s, axis=1, keepdims=True), sk)
            p = jnp.exp(s - m)
            l = jnp.sum(p, axis=1, keepdims=True) + jnp.exp(sk - m)
            o = _dot(p, vall, NN, hi) / l
            lse = m + jnp.log(l)
            for g in range(GRP):
                h = kvh * GRP + g
                o_ref[:, h * HD:(h + 1) * HD] = o[g * AB:(g + 1) * AB]
                lse_all = lse_all + lse[g * AB:(g + 1) * AB] * (lane == h).astype(F32)
        lse_ref[...] = lse_all

    ks = _att_kspecs(nb)
    return pl.pallas_call(
        body, name="attn_fwd", grid=(nb,),
        in_specs=[pl.BlockSpec((AB, D), lambda i: (i, 0))] + ks + ks + [pl.BlockSpec((1, 128), lambda i: (0, 0))],
        out_specs=[pl.BlockSpec((AB, D), lambda i: (i, 0)), pl.BlockSpec((AB, 128), lambda i: (i, 0))],
        out_shape=[jax.ShapeDtypeStruct((tl, D), F32), jax.ShapeDtypeStruct((tl, 128), F32)],
        compiler_params=_cparams(("parallel",)),
    )(qr, kr, kr, kr, kr, vv, vv, vv, vv, sink)


def _attn_delta(o, do):
    tl = o.shape[0]
    tr = min(512, tl)

    def body(o_ref, do_ref, d_ref):
        lane = lax.broadcasted_iota(jnp.int32, (1, 128), 1)
        acc = jnp.zeros((tr, 128), F32)
        for h in range(NH):
            sl = slice(h * HD, (h + 1) * HD)
            acc = acc + jnp.sum(o_ref[:, sl] * do_ref[:, sl], axis=1, keepdims=True) * (lane == h).astype(F32)
        d_ref[...] = acc

    return pl.pallas_call(
        body, name="attn_delta", grid=(tl // tr,),
        in_specs=[pl.BlockSpec((tr, D), lambda i: (i, 0))] * 2,
        out_specs=pl.BlockSpec((tr, 128), lambda i: (i, 0)),
        out_shape=jax.ShapeDtypeStruct((tl, 128), F32),
        compiler_params=_cparams(("parallel",)),
    )(o, do)


def _attn_bwd_q(qr, kr, vv, sink, do, lse, delta, hi):
    tl = qr.shape[0]
    nb = tl // AB

    def body(q_ref, kp_ref, kc_ref, kn_ref, kx_ref, vp_ref, vc_ref, vn_ref, vx_ref, sink_ref, do_ref, lse_ref, dl_ref,
             dq_ref, dkx_ref, dvx_ref, dsink_ref):
        i = pl.program_id(0)

        @pl.when(i == 0)
        def _():
            dkx_ref[...] = jnp.zeros_like(dkx_ref)
            dvx_ref[...] = jnp.zeros_like(dvx_ref)
            dsink_ref[...] = jnp.zeros_like(dsink_ref)

        mask = _att_mask(i, nb)
        lane = lax.broadcasted_iota(jnp.int32, (1, 128), 1)
        dsink = jnp.zeros((1, 128), F32)
        for kvh in range(KVH):
            ksl = slice(kvh * HD, (kvh + 1) * HD)
            kall = jnp.concatenate([kp_ref[:, ksl], kc_ref[:, ksl], kn_ref[:, ksl], kx_ref[:, ksl]], axis=0)
            vall = jnp.concatenate([vp_ref[:, ksl], vc_ref[:, ksl], vn_ref[:, ksl], vx_ref[:, ksl]], axis=0)
            qs = _att_stack(q_ref, kvh)
            dos = _att_stack(do_ref, kvh)
            lse_s = _att_col(lse_ref, kvh)
            dl_s = _att_col(dl_ref, kvh)
            s = _dot(qs, kall, NT, hi) * ATT_SCALE
            p = jnp.where(mask, jnp.exp(jnp.where(mask, s, NEG) - lse_s), 0.0)
            dp = _dot(dos, vall, NT, hi)
            ds = p * (dp - dl_s)
            dq = _dot(ds, kall, NN, hi) * ATT_SCALE
            dkx_ref[:, ksl] += _dot(ds[:, 3 * AB:], qs, TN, hi) * ATT_SCALE
            dvx_ref[:, ksl] += _dot(p[:, 3 * AB:], dos, TN, hi)
            psink = jnp.exp(_att_sink(sink_ref, kvh) - lse_s) * dl_s
            for g in range(GRP):
                h = kvh * GRP + g
                dq_ref[:, h * HD:(h + 1) * HD] = dq[g * AB:(g + 1) * AB]
                dsink = dsink - jnp.sum(psink[g * AB:(g + 1) * AB], axis=0, keepdims=True) * (lane == h).astype(F32)
        dsink_ref[...] += dsink

    ks = _att_kspecs(nb)
    row = pl.BlockSpec((AB, D), lambda i: (i, 0))
    col = pl.BlockSpec((AB, 128), lambda i: (i, 0))
    return pl.pallas_call(
        body, name="attn_bwd_q", grid=(nb,),
        in_specs=[row] + ks + ks + [pl.BlockSpec((1, 128), lambda i: (0, 0)), row, col, col],
        out_specs=[row, pl.BlockSpec((CTX, KVH * HD), lambda i: (0, 0)), pl.BlockSpec((CTX, KVH * HD), lambda i: (0, 0)),
                   pl.BlockSpec((1, 128), lambda i: (0, 0))],
        out_shape=[jax.ShapeDtypeStruct((tl, D), F32), jax.ShapeDtypeStruct((CTX, KVH * HD), F32),
                   jax.ShapeDtypeStruct((CTX, KVH * HD), F32), jax.ShapeDtypeStruct((1, 128), F32)],
        compiler_params=_cparams(("arbitrary",)),
    )(qr, kr, kr, kr, kr, vv, vv, vv, vv, sink, do, lse, delta)


def _attn_bwd_kv(qr, kr, vv, do, lse, delta, hi):
    tl = qr.shape[0]
    nb = tl // AB
    nc = CTX // AB

    def body(k_ref, v_ref, *refs):
        qs_refs, do_refs, lse_refs, dl_refs = refs[0:3], refs[3:6], refs[6:9], refs[9:12]
        dk_ref, dv_ref = refs[12], refs[13]
        j = pl.program_id(0)
        r = lax.broadcasted_iota(jnp.int32, (AB, AB), 0)
        c = lax.broadcasted_iota(jnp.int32, (AB, AB), 1)
        one = jnp.ones((AB, AB), jnp.bool_)
        masks = [jnp.logical_and(c <= r, j > 0), one, jnp.logical_and(c >= r, j < nb - 1)]
        for kvh in range(KVH):
            ksl = slice(kvh * HD, (kvh + 1) * HD)
            k_, v_ = k_ref[:, ksl], v_ref[:, ksl]
            dk = jnp.zeros((AB, HD), F32)
            dv = jnp.zeros((AB, HD), F32)
            for t in range(3):
                mask = jnp.concatenate([masks[t]] * GRP, axis=0)
                qs = _att_stack(qs_refs[t], kvh)
                dos = _att_stack(do_refs[t], kvh)
                lse_s = _att_col(lse_refs[t], kvh)
                dl_s = _att_col(dl_refs[t], kvh)
                s = _dot(qs, k_, NT, hi) * ATT_SCALE
                p = jnp.where(mask, jnp.exp(jnp.where(mask, s, NEG) - lse_s), 0.0)
                dp = _dot(dos, v_, NT, hi)
                ds = p * (dp - dl_s)
                dv = dv + _dot(p, dos, TN, hi)
                dk = dk + _dot(ds, qs, TN, hi) * ATT_SCALE
            dk_ref[:, ksl] = dk
            dv_ref[:, ksl] = dv

    def three(width):
        return [pl.BlockSpec((AB, width), lambda j: (jnp.maximum(j - 1, 0), 0)),
                pl.BlockSpec((AB, width), lambda j: (j, 0)),
                pl.BlockSpec((AB, width), lambda j: (jnp.minimum(j + 1, nb - 1), 0))]

    kv = pl.BlockSpec((AB, KVH * HD), lambda j: (j + nc, 0))
    out = pl.BlockSpec((AB, KVH * HD), lambda j: (j, 0))
    return pl.pallas_call(
        body, name="attn_bwd_kv", grid=(nb,),
        in_specs=[kv, kv] + three(D) + three(D) + three(128) + three(128),
        out_specs=[out, out],
        out_shape=[jax.ShapeDtypeStruct((tl, KVH * HD), F32)] * 2,
        compiler_params=_cparams(("parallel",)),
    )(kr, vv, qr, qr, qr, do, do, do, lse, lse, lse, delta, delta, delta)


def _mm(a, b, ta=False, tb=False, out_dtype=F32, tm=512, tn=1024, tk=1024, name="mm", hi=False):
    m, kd = (a.shape[1], a.shape[0]) if ta else a.shape
    n = b.shape[0] if tb else b.shape[1]
    tm, tn, tk = min(tm, m), min(tn, n), min(tk, kd)
    assert m % tm == 0 and n % tn == 0 and kd % tk == 0, (name, m, n, kd, tm, tn, tk)
    nk = kd // tk
    dims = ((0,) if ta else (1,), (1,) if tb else (0,))

    def body(a_ref, b_ref, o_ref, *scr):
        part = _dot(a_ref[...], b_ref[...], dims, hi)
        if nk == 1:
            o_ref[...] = part.astype(out_dtype)
        else:
            acc = scr[0]
            kk = pl.program_id(2)

            @pl.when(kk == 0)
            def _():
                acc[...] = part

            @pl.when(kk > 0)
            def _():
                acc[...] += part

            @pl.when(kk == nk - 1)
            def _():
                o_ref[...] = acc[...].astype(out_dtype)

    a_spec = pl.BlockSpec((tk, tm), lambda i, j, k: (k, i)) if ta else pl.BlockSpec((tm, tk), lambda i, j, k: (i, k))
    b_spec = pl.BlockSpec((tn, tk), lambda i, j, k: (j, k)) if tb else pl.BlockSpec((tk, tn), lambda i, j, k: (k, j))
    return pl.pallas_call(
        body, name=name, grid=(m // tm, n // tn, nk),
        in_specs=[a_spec, b_spec],
        out_specs=pl.BlockSpec((tm, tn), lambda i, j, k: (i, j)),
        out_shape=jax.ShapeDtypeStruct((m, n), out_dtype),
        scratch_shapes=[] if nk == 1 else [pltpu.VMEM((tm, tn), F32)],
        compiler_params=_cparams(("parallel", "parallel", "arbitrary")),
    )(a, b)


HALO = 8


class _In:
    def __init__(self, arr, w=None, cb=0, roff=0, halo=None):
        self.arr, self.w, self.cb, self.roff, self.halo = arr, w or arr.shape[1], cb, roff, halo


class _Full:
    def __init__(self, arr, w=None, cb=0):
        self.arr, self.w, self.cb = arr, w, cb


class _Out:
    def __init__(self, cols, dtype=F32, w=None, cb=0, acc=False, rows=1, roff=0, nrows=None):
        self.cols, self.dtype, self.w, self.cb, self.acc, self.rows, self.roff, self.nrows = (
            cols, dtype, w or cols, cb, acc, rows, roff, nrows)


def _rowcall(name, fn, nrow_tiles, tile, ins, outs, ncol=1):
    arrays, specs, kinds = [], [], []
    for x in ins:
        if isinstance(x, _Full):
            arrays.append(x.arr)
            if x.w is None:
                specs.append(pl.BlockSpec(x.arr.shape, lambda j, i: (0, 0)))
            else:
                specs.append(pl.BlockSpec((x.arr.shape[0], x.w), lambda j, i, cb=x.cb: (0, cb + j)))
            kinds.append("full")
            continue
        w, cb, roff = x.w, x.cb, x.roff
        cur = pl.BlockSpec((tile, w), lambda j, i, cb=cb, roff=roff: (i + roff, cb + j))
        if x.halo is None:
            arrays.append(x.arr)
            specs.append(cur)
            kinds.append("tile")
        else:
            r8 = tile // HALO
            last = x.arr.shape[0] // HALO - 1
            prev = pl.BlockSpec((HALO, w), lambda j, i, cb=cb, roff=roff, r8=r8: (jnp.maximum((i + roff) * r8 - 1, 0), cb + j))
            nxt = pl.BlockSpec((HALO, w), lambda j, i, cb=cb, roff=roff, r8=r8, last=last:
                               (jnp.minimum((i + roff + 1) * r8, last), cb + j))
            arrays += [x.arr, x.arr, x.arr]
            specs += [prev, cur, nxt]
            kinds.append(("halo", x.halo))
    out_specs, out_shapes = [], []
    for o in outs:
        if o.acc:
            out_specs.append(pl.BlockSpec((o.rows, o.w), lambda j, i, cb=o.cb: (0, cb + j)))
            out_shapes.append(jax.ShapeDtypeStruct((o.rows, o.cols), o.dtype))
        else:
            out_specs.append(pl.BlockSpec((tile, o.w), lambda j, i, cb=o.cb, roff=o.roff: (i + roff, cb + j)))
            out_shapes.append(jax.ShapeDtypeStruct(((o.nrows or nrow_tiles * tile), o.cols), o.dtype))
    n_in = len(arrays)

    def body(*refs):
        j = pl.program_id(0)
        i = pl.program_id(1)
        vals, r = [], 0
        for kind in kinds:
            if kind in ("full", "tile"):
                vals.append(refs[r][...])
                r += 1
            else:
                pok, nok = kind[1]
                p, c, n = refs[r][...], refs[r + 1][...], refs[r + 2][...]
                p = jnp.where(pok(i), p, jnp.zeros_like(p))
                n = jnp.where(nok(i), n, jnp.zeros_like(n))
                vals.append(jnp.concatenate([p, c, n], axis=0))
                r += 3
        res = fn(i, j, *vals)
        for o, ref, val in zip(outs, refs[n_in:], res):
            if o.acc:
                @pl.when(i == 0)
                def _(ref=ref, val=val, o=o):
                    ref[...] = val.astype(o.dtype)

                @pl.when(i > 0)
                def _(ref=ref, val=val, o=o):
                    ref[...] += val.astype(o.dtype)
            else:
                ref[...] = val.astype(o.dtype)

    return pl.pallas_call(
        body, name=name, grid=(ncol, nrow_tiles), in_specs=specs, out_specs=out_specs, out_shape=out_shapes,
        compiler_params=_cparams(("parallel", "arbitrary")),
    )(*arrays)


def _shift(xe, s, tile):
    if s == 0:
        return xe[HALO:HALO + tile]
    return pltpu.roll(xe, (-s) % xe.shape[0], 0)[HALO:HALO + tile]


def _silu(x):
    return x * jax.nn.sigmoid(x)


def _dsilu(x):
    s = jax.nn.sigmoid(x)
    return s * (1.0 + x * (1.0 - s))


def _heads(x, fn):
    return jnp.concatenate([fn(h, x[:, h * HD:(h + 1) * HD]) for h in range(x.shape[1] // HD)], axis=1)


def _colsum(x):
    return jnp.sum(x, axis=0, keepdims=True)


def _rowmean(x):
    return jnp.mean(x, axis=1, keepdims=True)


def _rowsum(x):
    return jnp.sum(x, axis=1, keepdims=True)


TILE = 256
CT = CTX // TILE


def _all_halo(n_tiles):
    return (lambda i: i >= CT + 1, lambda i: jnp.logical_and(i >= CT, i < n_tiles - 1))


def _lat_halo(n_tiles):
    return (lambda i: i >= 1, lambda i: i < n_tiles - 1)


def _rms_mod(x, nm, shift, scale):
    r = lax.rsqrt(_rowmean(x * x) + EPS)
    return (x * r * nm) * (1.0 + scale) + shift


def _rms_mod_bwd(dh, x, nm, scale):
    r = lax.rsqrt(_rowmean(x * x) + EPS)
    xn = x * r
    dz = dh * (1.0 + scale)
    dxn = dz * nm
    dx = r * (dxn - xn * _rowmean(dxn * xn))
    return dx, _colsum(dz * xn), _colsum(dh), _colsum(dh * (xn * nm))


def _norm_mod(xa, nm, mod_c, mod_x):
    n = xa.shape[0] // TILE

    def fn(i, j, x, nm_, mc, mx):
        m = jnp.where(i < CT, mc, mx)
        return (_rms_mod(x, nm_, m[0:1], m[1:2]),)

    return _rowcall("norm_mod", fn, n, TILE, [_In(xa), _Full(nm), _Full(mod_c), _Full(mod_x)], [_Out(D, BF16)])[0]


def _norm_mod_bwd(dh, xa, dres, nm, mod, roff, n):
    ins = [_In(dh, roff=roff), _In(xa, roff=roff), _Full(nm), _Full(mod)] + ([] if dres is None else [_In(dres)])

    def fn(i, j, dh_, x, nm_, m, *rest):
        dx, dn, dsh, dsc = _rms_mod_bwd(dh_, x, nm_, m[1:2])
        if rest:
            return (dx + rest[0], dn, dsh, dsc)
        return (dn, dsh, dsc)

    accs = [_Out(D, acc=True), _Out(D, acc=True), _Out(D, acc=True)]
    return _rowcall("norm_mod_bwd", fn, n, TILE, ins, ([] if dres is None else [_Out(D)]) + accs)


DN_Q_SCALE = HD ** -0.5


def _conv_taps(xe, w, width, rows=None):
    r = width // 2
    acc = None
    for t in range(width):
        s = t - r
        if rows is None:
            sh = xe if s == 0 else pltpu.roll(xe, (-s) % xe.shape[0], 0)
        else:
            sh = _shift(xe, s, rows)
        term = sh * w[t:t + 1]
        acc = term if acc is None else acc + term
    return acc


def _l2n(x, scale):
    rn = lax.rsqrt(_rowsum(x * x) + EPS)
    return x * (rn * scale)


def _l2n_bwd(dy, x, scale):
    rn = lax.rsqrt(_rowsum(x * x) + EPS)
    xu = x * rn
    return (scale * rn) * (dy - xu * _rowsum(dy * xu))


def _softplus(x):
    return jnp.maximum(x, 0.0) + jnp.log(1.0 + jnp.exp(-jnp.abs(x)))


def _lane_mask(lo, hi_):
    lane = lax.broadcasted_iota(jnp.int32, (1, 128), 1)
    return jnp.logical_and(lane >= lo, lane < hi_).astype(F32)


def _dn_prep(p, conv_w, gprm):
    n = p.shape[0] // TILE
    halo = _all_halo(n)

    def fn(i, j, qe, ke, ve, ba, w, gp):
        cq = _conv_taps(qe, w[:, 0:D], 5, TILE)
        ck = _conv_taps(ke, w[:, D:2 * D], 5, TILE)
        cv = _conv_taps(ve, w[:, 2 * D:3 * D], 5, TILE)
        q = _heads(_silu(cq), lambda h, x: _l2n(x, DN_Q_SCALE))
        k = _heads(_silu(ck), lambda h, x: _l2n(x, 1.0))
        v = _silu(cv)
        beta = jax.nn.sigmoid(ba)
        g = -jnp.exp(gp[0:1]) * _softplus(ba + gp[1:2])
        m0, m1 = _lane_mask(0, 8), _lane_mask(8, 16)
        gb_f = beta * m0 + pltpu.roll(g, 128 - 8, 1) * m1
        gb_b = pltpu.roll(beta, 128 - 8, 1) * m0 + pltpu.roll(g, 128 - 16, 1) * m1
        return q, k, v, gb_f, gb_b

    ins = [_In(p, D, 0, halo=halo), _In(p, D, 1, halo=halo), _In(p, D, 2, halo=halo), _In(p, 128, C_BA // 128),
           _Full(conv_w), _Full(gprm)]
    return _rowcall("dn_prep", fn, n, TILE, ins, [_Out(D), _Out(D), _Out(D), _Out(128), _Out(128)])


def _dn_prep_bwd(p, conv_w, gprm, dq2, dk2, dv2, dgb2):
    n = p.shape[0] // TILE
    halo = _all_halo(n)

    def branch(xe, w, dye, scale):
        c = _conv_taps(xe, w, 5)
        sx = _silu(c)
        if scale is None:
            dsx = dye
        else:
            dsx = jnp.concatenate([_l2n_bwd(dye[:, h * HD:(h + 1) * HD], sx[:, h * HD:(h + 1) * HD], scale)
                                   for h in range(NH)], axis=1)
        dc = dsx * _dsilu(c)
        dx = None
        dws = []
        dcc = dc[HALO:HALO + TILE]
        for t in range(5):
            term = _shift(dc, 2 - t, TILE) * w[t:t + 1]
            dx = term if dx is None else dx + term
            dws.append(_colsum(dcc * _shift(xe, t - 2, TILE)))
        dw = jnp.concatenate(dws + [jnp.zeros((3, D), F32)], axis=0)
        return dx, dw

    def fn(i, j, qe, ke, ve, ba, w, gp, dq0, dq1, dk0, dk1, dv0, dv1, dg0, dg1):
        dxq, dwq = branch(qe, w[:, 0:D], dq0 + dq1, DN_Q_SCALE)
        dxk, dwk = branch(ke, w[:, D:2 * D], dk0 + dk1, 1.0)
        dxv, dwv = branch(ve, w[:, 2 * D:3 * D], dv0 + dv1, None)
        m0, m1 = _lane_mask(0, 8), _lane_mask(8, 16)
        dbeta = dg0 * m0 + pltpu.roll(dg1 * m0, 8, 1)
        dg = pltpu.roll(dg0 * m1, 8, 1) + pltpu.roll(dg1 * m1, 16, 1)
        beta = jax.nn.sigmoid(ba)
        ea = jnp.exp(gp[0:1])
        z = ba + gp[1:2]
        g = -ea * _softplus(z)
        mg = _lane_mask(16, 32)
        da = dg * (-ea) * jax.nn.sigmoid(z) * mg
        dba = dbeta * beta * (1.0 - beta) * _lane_mask(0, 16) + da
        dgp = jnp.concatenate([_colsum(dg * g * mg), _colsum(da)], axis=0)
        return (jnp.concatenate([dxq, dxk, dxv], axis=1), dba, jnp.concatenate([dwq, dwk, dwv], axis=1), dgp)

    ins = [_In(p, D, 0, halo=halo), _In(p, D, 1, halo=halo), _In(p, D, 2, halo=halo), _In(p, 128, C_BA // 128),
           _Full(conv_w), _Full(gprm),
           _In(dq2, halo=halo), _In(dq2, roff=n, halo=halo), _In(dk2, halo=halo), _In(dk2, roff=n, halo=halo),
           _In(dv2, halo=halo), _In(dv2, roff=n, halo=halo), _In(dgb2), _In(dgb2, roff=n)]
    return _rowcall("dn_prep_bwd", fn, n, TILE, ins,
                    [_Out(3 * D, BF16), _Out(128, BF16), _Out(3 * D, acc=True, rows=8), _Out(128, acc=True, rows=2)])


def _hnorm(x, w):
    return x * lax.rsqrt(_rowmean(x * x) + EPS) * w


def _hnorm_bwd(dy, x, w):
    r = lax.rsqrt(_rowmean(x * x) + EPS)
    xh = x * r
    dxh = dy * w
    return r * (dxh - xh * _rowmean(dxh * xh)), _colsum(dy * xh)


def _dn_gate(o2, p, dn_norm, n_all):
    n = n_all - CT

    def fn(i, j, of, ob, gt, w):
        o = of + ob
        return (_heads(o, lambda h, x: _hnorm(x, w)) * _silu(gt),)

    ins = [_In(o2, roff=CT), _In(o2, roff=n_all + CT), _In(p, D, C_GT // D, roff=CT), _Full(dn_norm)]
    return _rowcall("dn_gate", fn, n, TILE, ins, [_Out(D, BF16)])[0]


def _dn_gate_bwd(dy, o2, p, dn_norm, n_all):
    n = n_all - CT

    def fn(i, j, dy_, of, ob, gt, w):
        o = of + ob
        sg = _silu(gt)
        dos, dw = [], jnp.zeros((1, HD), F32)
        yn = []
        for h in range(NH):
            sl = slice(h * HD, (h + 1) * HD)
            dx, dwh = _hnorm_bwd(dy_[:, sl] * sg[:, sl], o[:, sl], w)
            dos.append(dx)
            dw = dw + dwh
            yn.append(_hnorm(o[:, sl], w))
        dgt = dy_ * jnp.concatenate(yn, axis=1) * _dsilu(gt)
        return jnp.concatenate(dos, axis=1), dgt, dw

    ins = [_In(dy), _In(o2, roff=CT), _In(o2, roff=n_all + CT), _In(p, D, C_GT // D, roff=CT), _Full(dn_norm)]
    return _rowcall("dn_gate_bwd", fn, n, TILE, ins, [_Out(D), _Out(D, BF16), _Out(HD, acc=True)])


def _rope_shuffle(x):
    lane = lax.broadcasted_iota(jnp.int32, (1, HD), 1)
    return jnp.where((lane % 64) < 32, pltpu.roll(x, HD - 32, 1), pltpu.roll(x, 32, 1))


def _rope(x, cos, sin):
    return x * cos + _rope_shuffle(x) * sin


def _rope_bwd(dy, cos, sin):
    return dy * cos + _rope_shuffle(dy * sin)


def _attn_prep(p, w, cos, sin, width, cb, roff, n, name):
    def fn(i, j, x, w_, c, s):
        return (_heads(x, lambda h, xh: _rope(_hnorm(xh, w_), c, s)),)

    ins = [_In(p, width, cb, roff=roff), _Full(w), _In(cos), _In(sin)]
    return _rowcall(name, fn, n, TILE, ins, [_Out(width)])[0]


def _attn_prep_bwd(dy, p, w, cos, sin, width, cb, roff, n, name):
    def fn(i, j, dy_, x, w_, c, s):
        dxs, dw = [], jnp.zeros((1, HD), F32)
        for h in range(width // HD):
            sl = slice(h * HD, (h + 1) * HD)
            dx, dwh = _hnorm_bwd(_rope_bwd(dy_[:, sl], c, s), x[:, sl], w_)
            dxs.append(dx)
            dw = dw + dwh
        return jnp.concatenate(dxs, axis=1), dw

    ins = [_In(dy), _In(p, width, cb, roff=roff), _Full(w), _In(cos), _In(sin)]
    return _rowcall(name, fn, n, TILE, ins, [_Out(width, BF16), _Out(HD, acc=True)])


def _merge(z_dn, z_at, p, n):
    def fn(i, j, zd, za, gd, ga):
        return (jax.nn.sigmoid(gd) * zd + jax.nn.sigmoid(ga) * za,)

    ins = [_In(z_dn), _In(z_at), _In(p, D, C_MG // D, roff=CT), _In(p, D, C_MG // D + 1, roff=CT)]
    return _rowcall("merge", fn, n, TILE, ins, [_Out(D, BF16)])[0]


def _merge_bwd(dm, z_dn, z_at, p, n):
    def fn(i, j, dm_, zd, za, gd, ga):
        sd, sa = jax.nn.sigmoid(gd), jax.nn.sigmoid(ga)
        dg = jnp.concatenate([dm_ * zd * sd * (1.0 - sd), dm_ * za * sa * (1.0 - sa)], axis=1)
        return dm_ * sd, dm_ * sa, dg

    ins = [_In(dm), _In(z_dn), _In(z_at), _In(p, D, C_MG // D, roff=CT), _In(p, D, C_MG // D + 1, roff=CT)]
    return _rowcall("merge_bwd", fn, n, TILE, ins, [_Out(D, BF16), _Out(D, BF16), _Out(2 * D, BF16)])


def _resid_norm(xa, mo, g_a, nf, mod_f, n):
    def fn(i, j, x, mo_, ga, nf_, m):
        x1 = x + ga * mo_
        return x1, _rms_mod(x1, nf_, m[0:1], m[1:2])

    ins = [_In(xa, roff=CT), _In(mo), _Full(g_a), _Full(nf), _Full(mod_f)]
    return _rowcall("resid_norm", fn, n, TILE, ins, [_Out(D), _Out(D, BF16)])


def _resid_norm_bwd(dy, dh2, x1, mo, g_a, nf, mod_f, n):
    def fn(i, j, dy_, dh_, x1_, mo_, ga, nf_, m):
        dx, dn, dsh, dsc = _rms_mod_bwd(dh_, x1_, nf_, m[1:2])
        dx1 = dy_ + dx
        return dx1, ga * dx1, dn, dsh, dsc, _colsum(dx1 * mo_)

    ins = [_In(dy), _In(dh2), _In(x1), _In(mo), _Full(g_a), _Full(nf), _Full(mod_f)]
    accs = [_Out(D, acc=True) for _ in range(4)]
    return _rowcall("resid_norm_bwd", fn, n, TILE, ins, [_Out(D), _Out(D, BF16)] + accs)


def _loss_head(x1, f, tgt, g_f, n):
    def fn(i, j, x1_, f_, t, gf):
        e = x1_ + gf * f_ - t
        dy = e * (1.0 / D)
        loss = _colsum(_rowsum(e * e)) * (0.5 / D)
        return dy, gf * dy, _colsum(dy * f_), jnp.broadcast_to(loss, (1, 128))

    ins = [_In(x1), _In(f), _In(tgt), _Full(g_f)]
    return _rowcall("loss_head", fn, n, TILE, ins, [_Out(D), _Out(D, BF16), _Out(D, acc=True), _Out(128, acc=True)])


FW = DFF // 2


def _ffn_act(u, conv_w, conv_b, n):
    halo = _lat_halo(n)

    def fn(i, j, ge, ve, wg, wv, bg, bv):
        cg = _conv_taps(ge, wg, 3, TILE) + bg
        cv = _conv_taps(ve, wv, 3, TILE) + bv
        return (_silu(cg) * cv,)

    ins = [_In(u, FW, 0, halo=halo), _In(u, FW, 2, halo=halo), _Full(conv_w, FW, 0), _Full(conv_w, FW, 2),
           _Full(conv_b, FW, 0), _Full(conv_b, FW, 2)]
    return _rowcall("ffn_act", fn, n, TILE, ins, [_Out(DFF, BF16, FW)], ncol=2)[0]


def _ffn_act_bwd(u, da, conv_w, conv_b, n):
    halo = _lat_halo(n)

    def fn(i, j, ge, ve, dae, wg, wv, bg, bv):
        cg = _conv_taps(ge, wg, 3) + bg
        cv = _conv_taps(ve, wv, 3) + bv
        dcg = dae * cv * _dsilu(cg)
        dcv = dae * _silu(cg)
        outs = []
        for dc, xe, w in ((dcg, ge, wg), (dcv, ve, wv)):
            dx, dws = None, []
            dcc = dc[HALO:HALO + TILE]
            for t in range(3):
                term = _shift(dc, 1 - t, TILE) * w[t:t + 1]
                dx = term if dx is None else dx + term
                dws.append(_colsum(dcc * _shift(xe, t - 1, TILE)))
            outs.append((dx, jnp.concatenate(dws + [jnp.zeros((5, FW), F32)], axis=0), _colsum(dcc)))
        return outs[0][0], outs[1][0], outs[0][1], outs[1][1], outs[0][2], outs[1][2]

    ins = [_In(u, FW, 0, halo=halo), _In(u, FW, 2, halo=halo), _In(da, FW, 0, halo=halo),
           _Full(conv_w, FW, 0), _Full(conv_w, FW, 2), _Full(conv_b, FW, 0), _Full(conv_b, FW, 2)]
    outs = [_Out(DFF, BF16, FW), _Out(DFF, BF16, FW), _Out(DFF, w=FW, acc=True, rows=8), _Out(DFF, w=FW, acc=True, rows=8),
            _Out(DFF, w=FW, acc=True), _Out(DFF, w=FW, acc=True)]
    return _rowcall("ffn_act_bwd", fn, n, TILE, ins, outs, ncol=2)


def _rope_tables(tl):
    t = jnp.arange(tl, dtype=jnp.int32)
    row = (t // GRID_W).astype(F32)
    col = (t % GRID_W).astype(F32)
    inv = ROPE_BASE ** (-jnp.arange(32, dtype=F32) / 32)
    ar, ac = row[:, None] * inv, col[:, None] * inv
    cos = jnp.concatenate([jnp.cos(ar), jnp.cos(ar), jnp.cos(ac), jnp.cos(ac)], axis=1)
    sin = jnp.concatenate([-jnp.sin(ar), jnp.sin(ar), -jnp.sin(ac), jnp.sin(ac)], axis=1)
    return cos, sin


def _pad_w_in(w_in):
    z = lambda n: jnp.zeros((D, n), w_in.dtype)
    return jnp.concatenate([w_in[:, 0:4096], w_in[:, 4128:5152], w_in[:, 5664:7712], w_in[:, 5152:5664],
                            w_in[:, 4096:4128], z(96 + PW - C_PAD)], axis=1)


def _unpad_w_in(g, axis=1):
    cut = lambda a, b: lax.slice_in_dim(g, a, b, axis=axis)
    return jnp.concatenate([cut(0, 4096), cut(C_BA, C_BA + 32), cut(C_QAT, C_QAT + D), cut(C_KAT, C_KAT + 512),
                            cut(C_MG, C_MG + 2 * D)], axis=axis)


def _local_step(xa, tgt, mod_x, mod_c, w, hi=False):
    t_all = xa.shape[0]
    tl = t_all - CTX
    n_all, n = t_all // TILE, tl // TILE
    tm_all = 1280 if t_all % 1280 == 0 else TILE
    tm_lat = 1024
    mm = functools.partial(_mm, hi=hi)
    sp = lambda m: [m[:, k * D:(k + 1) * D] for k in range(6)]
    sh_a, sc_a, g_a, sh_f, sc_f, g_f = sp(mod_x)
    sh_ac, sc_ac = sp(mod_c)[:2]
    mod_ax = jnp.concatenate([sh_a, sc_a], axis=0)
    mod_ac = jnp.concatenate([sh_ac, sc_ac], axis=0)
    mod_f = jnp.concatenate([sh_f, sc_f], axis=0)
    nm, nf = w["norm_mix"], w["norm_ffn"]
    cos, sin = _rope_tables(tl)
    cos_all = jnp.concatenate([jnp.ones((CTX, HD), F32), cos], axis=0)
    sin_all = jnp.concatenate([jnp.zeros((CTX, HD), F32), sin], axis=0)
    conv_dn = jnp.concatenate([w["dn_conv"], jnp.zeros((3, 3 * D), F32)], axis=0)
    gprm = jnp.concatenate([jnp.zeros((2, 16), F32),
                            jnp.concatenate([w["dn_a_log"].reshape(1, 16), w["dn_dt_bias"].reshape(1, 16)], axis=0),
                            jnp.zeros((2, 96), F32)], axis=1)
    conv_ff = jnp.concatenate([w["ffn_conv"], jnp.zeros((5, 2 * DFF), F32)], axis=0)
    sink = jnp.concatenate([w["attn_sink"].reshape(1, NH), jnp.zeros((1, 128 - NH), F32)], axis=1)
    nct = CTX // CH

    h = _norm_mod(xa, nm, mod_ac, mod_ax)
    p = mm(h, w["w_in_p"], tm=tm_all, tn=1024, name="mm_in")
    q, k, v, gb_f, gb_b = _dn_prep(p, conv_dn, gprm)
    gb = jnp.stack([gb_f, gb_b])
    dn_u, dn_w, dn_qg, dn_kd, dn_pm, dn_t = _dn_intra_fwd(q, k, v, gb, nct, hi)
    o2, s_hist, dn_vn = _dn_seq_fwd(dn_u, dn_w, dn_qg, dn_kd, dn_pm, gb, nct, hi)
    o2 = o2.reshape(2 * t_all, D)
    y_dn = _dn_gate(o2, p, w["dn_norm"], n_all)
    qr = _attn_prep(p, w["q_norm"], cos, sin, D, C_QAT // D, CT, n, "attn_prep_q")
    kr = _attn_prep(p, w["k_norm"], cos_all, sin_all, KVH * HD, C_KAT // (KVH * HD), 0, n_all, "attn_prep_k")
    vv = p[:, C_VAT:C_VAT + KVH * HD]
    o_at, lse = _attn_fwd(qr, kr, vv, sink, hi)
    z_dn = mm(y_dn, w["w_branch_dn"], tm=tm_lat, name="mm_bdn")
    z_at = mm(o_at, w["w_branch_attn"], tm=tm_lat, name="mm_bat")
    merged = _merge(z_dn, z_at, p, n)
    mo = mm(merged, w["w_out"], tm=tm_lat, name="mm_out")
    x1, h2 = _resid_norm(xa, mo, g_a, nf, mod_f, n)
    u = mm(h2, w["ffn_up"], tm=2 * tm_lat, tn=1408, name="mm_up")
    a = _ffn_act(u, conv_ff, w["ffn_conv_b"], n)
    f = mm(a, w["ffn_down"], tm=tm_lat, tk=DFF, name="mm_down")
    dy, df, dg_f, loss = _loss_head(x1, f, tgt, g_f, n)

    g = {}
    da = mm(df, w["ffn_down"], tb=True, tm=tm_lat, tn=1408, name="mm_down_dx")
    g["ffn_down"] = mm(a, df, ta=True, tm=1408, tn=1024, tk=tm_lat, name="mm_down_dw")
    du_g, du_v, dcw_g, dcw_v, dcb_g, dcb_v = _ffn_act_bwd(u, da, conv_ff, w["ffn_conv_b"], n)
    du = jnp.concatenate([du_g, du_v], axis=1)
    g["ffn_conv"] = jnp.concatenate([dcw_g, dcw_v], axis=1)[0:3]
    g["ffn_conv_b"] = jnp.concatenate([dcb_g, dcb_v], axis=1)
    dh2 = mm(du, w["ffn_up"], tb=True, tm=tm_lat, tk=1408, name="mm_up_dx")
    g["ffn_up"] = mm(h2, du, ta=True, tm=1024, tn=1408, tk=tm_lat, name="mm_up_dw")
    dx1, dmo, g["norm_ffn"], dsh_f, dsc_f, dg_a = _resid_norm_bwd(dy, dh2, x1, mo, g_a, nf, mod_f, n)
    dmerged = mm(dmo, w["w_out"], tb=True, tm=tm_lat, name="mm_out_dx")
    g["w_out"] = mm(merged, dmo, ta=True, tm=1024, tk=tm_lat, name="mm_out_dw")
    dz_dn, dz_at, dmg = _merge_bwd(dmerged, z_dn, z_at, p, n)
    dy_dn = mm(dz_dn, w["w_branch_dn"], tb=True, tm=tm_lat, name="mm_bdn_dx")
    g["w_branch_dn"] = mm(y_dn, dz_dn, ta=True, tm=1024, tk=tm_lat, name="mm_bdn_dw")
    do_at = mm(dz_at, w["w_branch_attn"], tb=True, tm=tm_lat, name="mm_bat_dx")
    g["w_branch_attn"] = mm(o_at, dz_at, ta=True, tm=1024, tk=tm_lat, name="mm_bat_dw")

    do_dn, dgt, g["dn_norm"] = _dn_gate_bwd(dy_dn, o2, p, w["dn_norm"], n_all)
    do_all = jnp.concatenate([jnp.zeros((CTX, D), F32), do_dn], axis=0)
    dn_dvn, dn_dw, dn_dqg, dn_dkd, dn_del = _dn_seq_bwd(dn_w, dn_qg, dn_kd, dn_pm, dn_vn, s_hist, gb, do_all, nct, hi)
    dq2, dk2, dv2, dgb2 = _dn_intra_bwd(q, k, v, gb, dn_u, dn_w, dn_t, dn_vn, dn_dvn, dn_dw, dn_dqg, dn_dkd, dn_del,
                                        do_all, nct, hi)
    dqkv, dba, dconv, dgprm = _dn_prep_bwd(p, conv_dn, gprm, dq2.reshape(2 * t_all, D), dk2.reshape(2 * t_all, D),
                                           dv2.reshape(2 * t_all, D), dgb2.reshape(2 * t_all, 128))
    g["dn_conv"] = dconv[0:5]
    g["dn_a_log"] = dgprm[0, 16:32].reshape(2, NH)
    g["dn_dt_bias"] = dgprm[1, 16:32].reshape(2, NH)

    delta = _attn_delta(o_at, do_at)
    dqr, dkx, dvx, dsink = _attn_bwd_q(qr, kr, vv, sink, do_at, lse, delta, hi)
    dk_lat, dv_lat = _attn_bwd_kv(qr, kr, vv, do_at, lse, delta, hi)
    g["attn_sink"] = dsink[:, 0:NH]
    dq_at, g["q_norm"] = _attn_prep_bwd(dqr, p, w["q_norm"], cos, sin, D, C_QAT // D, CT, n, "attn_prep_q_bwd")
    dkr = jnp.concatenate([dkx, dk_lat], axis=0)
    dk_at, g["k_norm"] = _attn_prep_bwd(dkr, p, w["k_norm"], cos_all, sin_all, KVH * HD, C_KAT // (KVH * HD), 0, n_all,
                                        "attn_prep_k_bwd")
    dv_at = jnp.concatenate([dvx, dv_lat], axis=0).astype(BF16)

    zc = lambda width: jnp.zeros((CTX, width), BF16)
    dp = jnp.concatenate([
        dqkv,
        jnp.concatenate([zc(D), dgt], axis=0),
        jnp.concatenate([zc(D), dq_at], axis=0),
        jnp.concatenate([zc(2 * D), dmg], axis=0),
        dk_at, dv_at, dba, jnp.zeros((t_all, PW - C_PAD), BF16)], axis=1)
    dh = mm(dp, w["w_in_p"], tb=True, tm=tm_all, tn=1024, tk=2048, name="mm_in_dx")
    g["w_in_p"] = mm(h, dp, ta=True, tm=1024, tn=2048, tk=tm_all, name="mm_in_dw")
    dnm_c, dsh_ac, dsc_ac = _norm_mod_bwd(dh, xa, None, nm, mod_ac, 0, CT)
    grad_x, dnm_x, dsh_a, dsc_a = _norm_mod_bwd(dh, xa, dx1, nm, mod_ax, CT, n)
    g["norm_mix"] = dnm_c + dnm_x
    dmod_x = jnp.concatenate([dsh_a, dsc_a, dg_a, dsh_f, dsc_f, dg_f], axis=1)
    dmod_c = jnp.concatenate([dsh_ac, dsc_ac, jnp.zeros((1, 4 * D), F32)], axis=1)
    return loss, grad_x, g, dmod_x, dmod_c


def _sum_slots(buf, n_slots, rows, tile, name, stride=1):
    nt = rows // tile

    def fn(i, j, *vals):
        acc = vals[0]
        for v in vals[1:]:
            acc = acc + v
        return (acc,)

    ins = [_In(buf, roff=k * stride * nt) for k in range(n_slots)]
    return _rowcall(name, fn, nt, tile, ins, [_Out(buf.shape[1])])[0]


ADAM_LR, ADAM_B1, ADAM_B2, ADAM_EPS, ADAM_WD, ADAM_STEP = 0.001, 0.9, 0.999, 1e-08, 0.01, 10


def _row_tile(rows, cols):
    for t in (512, 256, 128, 64, 32, 16, 8):
        if rows % t == 0 and t * cols * 4 * 14 <= 40 * 1024 * 1024:
            return t
    return rows


def _adamw(w, g, m, v, name):
    shape = w.shape
    cols = shape[-1]
    rows = max(1, math.prod(shape[:-1]))
    tile = _row_tile(rows, cols)
    c1 = 1.0 / (1.0 - ADAM_B1 ** ADAM_STEP)
    c2 = 1.0 / (1.0 - ADAM_B2 ** ADAM_STEP)

    def fn(i, j, w_, g_, m_, v_):
        mn = ADAM_B1 * m_ + (1.0 - ADAM_B1) * g_
        vn = ADAM_B2 * v_ + (1.0 - ADAM_B2) * (g_ * g_)
        delta = -ADAM_LR * ((mn * c1) / (jnp.sqrt(vn * c2) + ADAM_EPS) + ADAM_WD * w_)
        return delta, mn, vn

    r2 = lambda a: a.reshape(rows, cols)
    outs = _rowcall(name, fn, rows // tile, tile, [_In(r2(w)), _In(r2(g)), _In(r2(m)), _In(r2(v))],
                    [_Out(cols), _Out(cols), _Out(cols)])
    return [o.reshape(shape) for o in outs]


MESH = pl.DeviceIdType.MESH
ANY = pl.BlockSpec(memory_space=pl.ANY)


def _pos():
    return lax.axis_index("x"), lax.axis_index("y"), lax.axis_index("c")


def _all_gather(blk, name):
    m_per, n = blk.shape

    def body(x_ref, out_ref, send_sems, recv_sems, local_sem):
        x, y, c = _pos()
        me, sibling = (x, y, c), (x, y, 1 - c)
        chips = [(1 - x, y), (x, 1 - y), (1 - x, 1 - y)]

        def rows(px, py, pc):
            return out_ref.at[pl.ds(pl.multiple_of((4 * px + 2 * py + pc) * m_per, 8), m_per), :]

        def copy(k, block, to, src=None):
            return pltpu.make_async_remote_copy(
                src_ref=rows(*block) if src is None else src, dst_ref=rows(*block),
                send_sem=send_sems.at[k], recv_sem=recv_sems.at[k], device_id=to, device_id_type=MESH)

        mine = pltpu.make_async_copy(x_ref, rows(*me), local_sem)
        mine.start()
        first = [copy(0, me, sibling, src=x_ref)]
        first += [copy(1 + j, me, (*chip, c), src=x_ref) for j, chip in enumerate(chips)]
        for cp in first:
            cp.start()
        passed = [copy(4 + j, (*chip, c), sibling) for j, chip in enumerate(chips)]
        for j, chip in enumerate(chips):
            copy(1 + j, (*chip, c), me).wait_recv()
            passed[j].start()
        copy(0, sibling, me).wait_recv()
        for j, chip in enumerate(chips):
            copy(4 + j, (*chip, 1 - c), me).wait_recv()
        for cp in first + passed:
            cp.wait_send()
        mine.wait()

    return pl.pallas_call(
        body, name=name, out_shape=jax.ShapeDtypeStruct((N_DEV * m_per, n), blk.dtype),
        in_specs=[ANY], out_specs=ANY,
        scratch_shapes=[pltpu.SemaphoreType.DMA((7,)), pltpu.SemaphoreType.DMA((7,)), pltpu.SemaphoreType.DMA],
        compiler_params=pltpu.CompilerParams(has_side_effects=True),
    )(blk)


def _flip(v, bit):
    return 1 - v if bit else v


D2D_STREAMS = 8
ICI_STREAMS = 2


def _sibling_exchange(src, seg_rows, n_seg, paired, name):
    n = src.shape[1]
    per_seg = D2D_STREAMS // n_seg
    per = seg_rows // per_seg
    assert per_seg * n_seg == D2D_STREAMS and per * per_seg == seg_rows and per % 16 == 0

    def body(x_ref, out_ref, send_sems, recv_sems):
        x, y, c = _pos()
        copies = []
        for s in range(n_seg):
            base = (2 * s + (1 - c)) * seg_rows if paired else s * seg_rows
            for j in range(per_seg):
                i = s * per_seg + j
                cp = pltpu.make_async_remote_copy(
                    src_ref=x_ref.at[pl.ds(pl.multiple_of(base + j * per, 16), per), :],
                    dst_ref=out_ref.at[pl.ds(s * seg_rows + j * per, per), :],
                    send_sem=send_sems.at[i], recv_sem=recv_sems.at[i], device_id=(x, y, 1 - c), device_id_type=MESH)
                cp.start()
                copies.append(cp)
        for cp in copies:
            cp.wait_recv()
        for cp in copies:
            cp.wait_send()

    return pl.pallas_call(
        body, name=name, out_shape=jax.ShapeDtypeStruct((n_seg * seg_rows, n), src.dtype),
        in_specs=[ANY], out_specs=ANY,
        scratch_shapes=[pltpu.SemaphoreType.DMA((D2D_STREAMS,)), pltpu.SemaphoreType.DMA((D2D_STREAMS,))],
        compiler_params=pltpu.CompilerParams(has_side_effects=True),
    )(src)


def _transpose_cast(x, dtype, name):
    r, c = x.shape
    tc = 512

    def body(x_ref, o_ref):
        o_ref[...] = x_ref[...].T.astype(o_ref.dtype)

    return pl.pallas_call(
        body, name=name, grid=(c // tc,),
        in_specs=[pl.BlockSpec((r, tc), lambda j: (0, j))], out_specs=pl.BlockSpec((tc, r), lambda j: (j, 0)),
        out_shape=jax.ShapeDtypeStruct((c, r), dtype), compiler_params=_cparams(("parallel",)),
    )(x)


def _chip_exchange(buf, rows, name):
    n = buf.shape[1]
    per = rows // ICI_STREAMS
    assert per * ICI_STREAMS == rows and per % 16 == 0

    def body(x_ref, out_ref, send_sems, recv_sems):
        x, y, c = _pos()
        copies = []
        for k in range(1, 4):
            px, py = _flip(x, k & 2), _flip(y, k & 1)
            for j in range(ICI_STREAMS):
                i = (k - 1) * ICI_STREAMS + j
                cp = pltpu.make_async_remote_copy(
                    src_ref=x_ref.at[pl.ds(pl.multiple_of((2 * px + py) * rows + j * per, 16), per), :],
                    dst_ref=out_ref.at[pl.ds((k - 1) * rows + j * per, per), :],
                    send_sem=send_sems.at[i], recv_sem=recv_sems.at[i], device_id=(px, py, c), device_id_type=MESH)
                cp.start()
                copies.append(cp)
        for cp in copies:
            cp.wait_recv()
        for cp in copies:
            cp.wait_send()

    return pl.pallas_call(
        body, name=name, out_shape=jax.ShapeDtypeStruct((3 * rows, n), buf.dtype),
        in_specs=[ANY], out_specs=ANY,
        scratch_shapes=[pltpu.SemaphoreType.DMA((3 * ICI_STREAMS,)), pltpu.SemaphoreType.DMA((3 * ICI_STREAMS,))],
        compiler_params=pltpu.CompilerParams(has_side_effects=True),
    )(buf)


def _add_rows(parts, rows, dtype, name):
    tile = 1024
    ins = [_In(a, roff=r0 // tile) for a, r0 in parts]

    def fn(i, j, *vals):
        acc = vals[0].astype(F32)
        for v_ in vals[1:]:
            acc = acc + v_.astype(F32)
        return (acc,)

    return _rowcall(name, fn, rows // tile, tile, ins, [_Out(parts[0][0].shape[1], dtype)])[0]


BIG = ("w_in", "w_branch_dn", "w_branch_attn", "w_out", "ffn_up", "ffn_down")
BIG_SHARD = {"w_in": (1024, 1928, True), "w_branch_dn": (256, 1024, False), "w_branch_attn": (256, 1024, False),
             "w_out": (256, 1024, False), "ffn_up": (1024, 1408, True), "ffn_down": (704, 1024, False)}
BIG_ROWS = {k: r * c // 2 // 128 for k, (r, c, _) in BIG_SHARD.items()}
PIECE = 19456
assert sum(BIG_ROWS.values()) <= PIECE


def _pack_half(shards, ci, dtype):
    parts = []
    for k in BIG:
        r, c, _ = BIG_SHARD[k]
        parts.append(lax.dynamic_slice_in_dim(shards[k], ci * (r // 2), r // 2, axis=0).reshape(-1, 128).astype(dtype))
    parts.append(jnp.zeros((PIECE - sum(BIG_ROWS.values()), 128), dtype))
    return jnp.concatenate(parts, axis=0)


def _unpack_full(ag):
    out, off = {}, 0
    for k in BIG:
        r, c, by_col = BIG_SHARD[k]
        blk = ag[:, off:off + BIG_ROWS[k]].reshape(4, r, c)
        out[k] = jnp.transpose(blk, (1, 0, 2)).reshape(r, 4 * c) if by_col else blk.reshape(4 * r, c)
        off += BIG_ROWS[k]
    return out


def _pack_pieces(full):
    parts = [full["w_in_t"].reshape(N_DEV, BIG_ROWS["w_in"], 128).astype(BF16)]
    for k in BIG[1:]:
        r, c, by_col = BIG_SHARD[k]
        a = full[k]
        if by_col:
            a = jnp.transpose(a.reshape(r, 4, c), (1, 0, 2))
        parts.append(a.reshape(N_DEV, BIG_ROWS[k], 128).astype(BF16))
    parts.append(jnp.zeros((N_DEV, PIECE - sum(BIG_ROWS.values()), 128), BF16))
    return jnp.concatenate(parts, axis=1).reshape(N_DEV * PIECE, 128)


def _reduce_scatter(pieces, ci, shard):
    half = N_DEV // 2 * PIECE
    theirs = _sibling_exchange(pieces, PIECE, N_DEV // 2, True, "rs_d2d")
    own = lax.dynamic_index_in_dim(pieces.reshape(N_DEV // 2, 2, PIECE, 128), ci, axis=1, keepdims=False).reshape(half, 128)
    part = _add_rows([(own, 0), (theirs, 0)], half, BF16, "rs_sum_chip")
    recv = _chip_exchange(part, PIECE, "rs_ici")
    own2 = lax.dynamic_slice_in_dim(part, shard * PIECE, PIECE, axis=0)
    mine = _add_rows([(own2, 0), (recv, 0), (recv, PIECE), (recv, 2 * PIECE)], PIECE, F32, "rs_sum_all")
    other = _sibling_exchange(mine, PIECE, 1, False, "rs_pair")
    return jnp.where(ci == 0, jnp.stack([mine, other]), jnp.stack([other, mine]))


def _unpack_shard(two):
    out, off = {}, 0
    for k in BIG:
        r, c, _ = BIG_SHARD[k]
        blk = two[:, off:off + BIG_ROWS[k]]
        out[k] = blk.reshape(c, r).T if k == "w_in" else blk.reshape(r, c)
        off += BIG_ROWS[k]
    return out


SMALL = (("dn_conv", 120), ("ffn_conv", 132), ("ffn_conv_b", 44), ("norm_mix", 8), ("norm_ffn", 8), ("dn_a_log", 1),
         ("dn_dt_bias", 1), ("dn_norm", 1), ("q_norm", 1), ("k_norm", 1), ("attn_sink", 1), ("dmod_c", 48), ("dmod_x", 48))
SMALL_ROWS = 416


def _rows128(a, rows):
    flat = a.reshape(-1)
    return jnp.concatenate([flat, jnp.zeros((rows * 128 - flat.shape[0],), F32)]).reshape(rows, 128)


def _pack_small(g):
    parts = [_rows128(g[k], r) for k, r in SMALL]
    parts.append(jnp.zeros((SMALL_ROWS - sum(r for _, r in SMALL), 128), F32))
    return jnp.concatenate(parts, axis=0)


def _unpack_small(buf, shapes):
    out, off = {}, 0
    for k, r in SMALL:
        n = math.prod(shapes[k])
        out[k] = buf[off:off + r].reshape(-1)[:n].reshape(shapes[k])
        off += r
    return out


WEIGHTS = ("c_ctx", "w_ada", "b_ada", "norm_mix", "norm_ffn", "w_in", "dn_conv", "dn_a_log", "dn_dt_bias", "dn_norm",
           "q_norm", "k_norm", "attn_sink", "w_branch_dn", "w_branch_attn", "w_out", "ffn_up", "ffn_conv", "ffn_conv_b",
           "ffn_down")


def kernel(x, c, ctx, c_ctx, w_ada, b_ada, norm_mix, norm_ffn, w_in, dn_conv, dn_a_log, dn_dt_bias, dn_norm, q_norm, k_norm, attn_sink, w_branch_dn, w_branch_attn, w_out, ffn_up, ffn_conv, ffn_conv_b, ffn_down, loss_target, m_c_ctx, m_w_ada, m_b_ada, m_norm_mix, m_norm_ffn, m_w_in, m_dn_conv, m_dn_a_log, m_dn_dt_bias, m_dn_norm, m_q_norm, m_k_norm, m_attn_sink, m_w_branch_dn, m_w_branch_attn, m_w_out, m_ffn_up, m_ffn_conv, m_ffn_conv_b, m_ffn_down, v_c_ctx, v_w_ada, v_b_ada, v_norm_mix, v_norm_ffn, v_w_in, v_dn_conv, v_dn_a_log, v_dn_dt_bias, v_dn_norm, v_q_norm, v_k_norm, v_attn_sink, v_w_branch_dn, v_w_branch_attn, v_w_out, v_ffn_up, v_ffn_conv, v_ffn_conv_b, v_ffn_down):
    args = dict(locals())
    xi, yi, ci = _pos()
    dev = 4 * xi + 2 * yi + ci
    shard = 2 * xi + yi
    chips = lambda a: a[0::2]

    blk = jnp.concatenate([_rows128(c, 8), _rows128(dn_conv, 30), _rows128(ffn_conv, 33), jnp.zeros((1, 128), F32)], axis=0)
    ag = _all_gather(blk, "ag_small_in").reshape(N_DEV, 72, 128)
    c_all = ag[:, 0:8].reshape(N_DEV, D)
    dn_conv_full = jnp.transpose(chips(ag)[:, 8:38].reshape(4, 5, 768), (1, 0, 2)).reshape(5, 3 * D)
    ffn_conv_full = jnp.transpose(chips(ag)[:, 38:71].reshape(4, 3, 1408), (1, 0, 2)).reshape(3, 2 * DFF)

    c16 = jnp.concatenate([c_all, c_ctx[None], jnp.zeros((7, D), F32)], axis=0)
    a16 = _rowcall("ada_silu", lambda i, j, v: (_silu(v),), 1, 16, [_In(c16)], [_Out(D)])[0]
    m_sh = _mm(a16, w_ada[0], tm=16, tn=512, tk=D, name="ada_fwd", hi=True)
    mod16 = chips(_all_gather(m_sh, "ag_mod").reshape(N_DEV, 16, 1536))
    mod16 = jnp.transpose(mod16, (1, 0, 2)).reshape(16, 6 * D) + b_ada
    mod_x = lax.dynamic_slice_in_dim(mod16, dev, 1, axis=0)
    mod_c = mod16[8:9]

    shards = {k: args[k][0] for k in BIG}
    wfull = _unpack_full(_all_gather(_pack_half(shards, ci, BF16), "ag_weights").reshape(N_DEV, PIECE, 128))
    w = dict(wfull)
    w["w_in_p"] = _pad_w_in(wfull["w_in"])
    w.update(norm_mix=norm_mix, norm_ffn=norm_ffn, dn_conv=dn_conv_full, dn_a_log=dn_a_log[0], dn_dt_bias=dn_dt_bias[0],
             dn_norm=dn_norm, q_norm=q_norm, k_norm=k_norm, attn_sink=attn_sink, ffn_conv=ffn_conv_full, ffn_conv_b=ffn_conv_b)

    xa = jnp.concatenate([ctx[0], x[0]], axis=0)
    loss_part, grad_x, g, dmod_x, dmod_c = _local_step(xa, loss_target[0], mod_x, mod_c, w)
    loss = lax.psum(loss_part[0, 0], ("x", "y", "c"))

    g["w_in_t"] = _unpad_w_in(_transpose_cast(g["w_in_p"], BF16, "w_in_grad_t"), axis=0)
    gshard = _unpack_shard(_reduce_scatter(_pack_pieces(g), ci, shard))

    g["dmod_c"], g["dmod_x"] = dmod_c, dmod_x
    ag_s = _all_gather(_pack_small(g), "ag_small_grads")
    shapes = {k: g[k].shape for k, _ in SMALL}
    gs = _unpack_small(_sum_slots(ag_s, N_DEV, SMALL_ROWS, SMALL_ROWS, "small_sum"), shapes)
    dx_all = ag_s.reshape(N_DEV, SMALL_ROWS, 128)[:, SMALL_ROWS - 50:SMALL_ROWS - 2].reshape(N_DEV, 6 * D)

    d16 = jnp.concatenate([dx_all, gs["dmod_c"], jnp.zeros((7, 6 * D), F32)], axis=0)
    d16_sh = lax.dynamic_slice_in_dim(d16, shard * 1536, 1536, axis=1)
    g_w_ada = _mm(a16, d16_sh, ta=True, tm=D, tn=512, tk=16, name="ada_dw", hi=True)
    g_b_ada = _rowcall("ada_db", lambda i, j, v: (_colsum(v),), 1, 16, [_In(d16)], [_Out(6 * D, acc=True)])[0]
    da_part = _mm(d16_sh, w_ada[0], tb=True, tm=16, tn=D, tk=512, name="ada_dx", hi=True)
    da_all = _all_gather(da_part, "ag_ada_dx")
    da16 = _sum_slots(da_all, 4, 16, 16, "ada_dx_sum", stride=2)
    dc16 = _rowcall("ada_dsilu", lambda i, j, d_, v: (d_ * _dsilu(v),), 1, 16, [_In(da16), _In(c16)], [_Out(D)])[0]

    grads = {
        "c_ctx": dc16[8], "w_ada": g_w_ada[None], "b_ada": g_b_ada, "norm_mix": gs["norm_mix"], "norm_ffn": gs["norm_ffn"],
        "w_in": gshard["w_in"][None],
        "dn_conv": lax.dynamic_slice_in_dim(gs["dn_conv"], shard * 768, 768, axis=1)[None],
        "dn_a_log": gs["dn_a_log"][None], "dn_dt_bias": gs["dn_dt_bias"][None], "dn_norm": gs["dn_norm"],
        "q_norm": gs["q_norm"], "k_norm": gs["k_norm"], "attn_sink": gs["attn_sink"],
        "w_branch_dn": gshard["w_branch_dn"][None], "w_branch_attn": gshard["w_branch_attn"][None],
        "w_out": gshard["w_out"][None], "ffn_up": gshard["ffn_up"][None],
        "ffn_conv": lax.dynamic_slice_in_dim(gs["ffn_conv"], shard * 1408, 1408, axis=1)[None],
        "ffn_conv_b": gs["ffn_conv_b"], "ffn_down": gshard["ffn_down"][None],
    }
    deltas, new_m, new_v = [], [], []
    for k in WEIGHTS:
        d_, m_, v_ = _adamw(args[k], grads[k], args["m_" + k], args["v_" + k], "adamw_" + k)
        deltas.append(d_)
        new_m.append(m_)
        new_v.append(v_)
    return (loss, grad_x[None], *[grads[k] for k in WEIGHTS], *deltas, *new_m, *new_v)
```

```python
import functools
import math

import numpy as np
import jax
import jax.numpy as jnp
from jax import lax
from jax.experimental import pallas as pl
from jax.experimental.pallas import tpu as pltpu

F32 = jnp.float32
BF16 = jnp.bfloat16
HI = lax.Precision.HIGHEST

D = 1024
NH = 8
HD = 128
CH = 64
CTX = 256
AB = 128
KVH = 2
GRP = 4
DFF = 2816
EPS = 1e-6
GRID_W = 64
ROPE_BASE = 10000.0
N_DEV = 8
VMEM_LIMIT = 56 * 1024 * 1024

C_QKV, C_GT, C_QAT, C_MG, C_KAT, C_VAT, C_BA, C_PAD = 0, 3072, 4096, 5120, 7168, 7424, 7680, 7808
PW = 8192


def _cparams(sem=None, **kw):
    return pltpu.CompilerParams(dimension_semantics=sem, vmem_limit_bytes=VMEM_LIMIT, **kw)


def _dot(a, b, dims, hi):
    if hi:
        return lax.dot_general(a.astype(F32), b.astype(F32), (dims, ((), ())), precision=HI, preferred_element_type=F32)
    return lax.dot_general(a.astype(BF16), b.astype(BF16), (dims, ((), ())), preferred_element_type=F32)


NN = ((1,), (0,))
NT = ((1,), (1,))
TN = ((0,), (0,))


def _dn_masks():
    i = np.arange(CH)
    lo_incl = (i[:, None] >= i[None, :]).astype(np.float32)
    lo_strict = (i[:, None] > i[None, :]).astype(np.float32)
    return jnp.asarray(np.stack([np.stack([lo_incl, lo_strict]), np.stack([lo_incl.T, lo_strict.T])]))


def _dn_chunk_index(d, i, n_ctx_chunks, n_chunks):
    fwd = i
    bwd = jnp.where(i < n_ctx_chunks, n_ctx_chunks - 1 - i, n_chunks - 1 + n_ctx_chunks - i)
    return jnp.where(d == 0, fwd, bwd)


BNN = ((2,), (1,))
BNT = ((2,), (2,))
BTN = ((1,), (1,))


def _bdot(a, b, dims, hi):
    dn = (dims, ((0,), (0,)))
    if hi:
        return lax.dot_general(a.astype(F32), b.astype(F32), dn, precision=HI, preferred_element_type=F32)
    return lax.dot_general(a.astype(BF16), b.astype(BF16), dn, preferred_element_type=F32)


def _bdot3(a, b, dims, hi):
    if hi:
        return _bdot(a, b, dims, True)
    ah, bh = a.astype(BF16), b.astype(BF16)
    al, bl = (a - ah.astype(F32)).astype(BF16), (b - bh.astype(F32)).astype(BF16)
    dn = (dims, ((0,), (0,)))
    d = lambda x_, y_: lax.dot_general(x_, y_, dn, preferred_element_type=F32)
    return d(ah, bh) + d(ah, bl) + d(al, bh)


DN_CB = 4
DN_SEQ_CB = 2


def _dn_heads(ref, cb=1):
    return jnp.stack([ref[t * CH:(t + 1) * CH, h * HD:(h + 1) * HD] for t in range(cb) for h in range(NH)])


def _dn_scalars(gb, mi, cb=1):
    beta, gc, gcr, gt = [], [], [], []
    for t in range(cb):
        g1 = gb[t * CH:(t + 1) * CH]
        gcum, gcum_t, gtot = _dn_gcum(g1, mi)
        beta += [g1[:, h:h + 1] for h in range(NH)]
        gc += [gcum[:, NH + h:NH + h + 1] for h in range(NH)]
        gcr += [gcum_t[NH + h:NH + h + 1, :] for h in range(NH)]
        gt += [gtot[:, NH + h:NH + h + 1] for h in range(NH)]
    return jnp.stack(beta), jnp.stack(gc), jnp.stack(gcr), jnp.stack(gt)


def _dn_total(gb):
    gtot = jnp.sum(gb, axis=0, keepdims=True)
    return jnp.stack([gtot[:, NH + h:NH + h + 1] for h in range(NH)])


def _dn_gcum(gb, mi):
    gcum = _dot(mi, gb, NN, True)
    gtot = jnp.sum(gb, axis=0, keepdims=True)
    return gcum, gcum.T, gtot


def _dn_specs(n_ctx_chunks, n_chunks, reverse, cb):
    assert n_ctx_chunks % cb == 0 and n_chunks % cb == 0

    def grp(d, i):
        first = n_chunks - 1 - cb * i if reverse else cb * i
        return _dn_chunk_index(d, first, n_ctx_chunks, n_chunks) // cb

    def slot(d, t):
        ascending = (d == 1) if reverse else (d == 0)
        return jnp.where(ascending, t, cb - 1 - t)

    ctx_groups = n_ctx_chunks // cb
    tok_lat = pl.BlockSpec((cb * CH, D), lambda d, i: (jnp.maximum(grp(d, i) - ctx_groups, 0), 0))
    is_ctx = lambda d, i: grp(d, i) < ctx_groups
    tok_d = pl.BlockSpec((1, cb * CH, D), lambda d, i: (d, grp(d, i), 0))
    gbs = pl.BlockSpec((1, cb * CH, 128), lambda d, i: (d, grp(d, i), 0))

    def per_chunk(*tail):
        return pl.BlockSpec((1, cb) + tail, lambda d, i: (d, grp(d, i)) + (0,) * len(tail))

    return tok_lat, is_ctx, tok_d, gbs, per_chunk, slot


def _dn_group_specs(cb):
    tok = pl.BlockSpec((cb * CH, D), lambda d, i: (i, 0))
    tok_d = pl.BlockSpec((1, cb * CH, D), lambda d, i: (d, i, 0))
    gbs = pl.BlockSpec((1, cb * CH, 128), lambda d, i: (d, i, 0))
    msk = pl.BlockSpec((1, 2, CH, CH), lambda d, i: (d, 0, 0, 0))

    def per_chunk(*tail):
        return pl.BlockSpec((1, cb) + tail, lambda d, i: (d, i) + (0,) * len(tail))

    return tok, tok_d, gbs, msk, per_chunk


def _dn_intra_fwd(q, k, v, gb, n_ctx_chunks, hi):
    t_all = q.shape[0]
    n_chunks = t_all // CH
    masks = _dn_masks()

    cb = DN_CB

    def put(ref, val):
        for t_ in range(cb):
            ref[0, t_] = val[t_ * NH:(t_ + 1) * NH].astype(ref.dtype)

    def body(q_ref, k_ref, v_ref, gb_ref, m_ref, u_ref, w_ref, qg_ref, kd_ref, pm_ref, t_ref):
        mi, ms = m_ref[0, 0], m_ref[0, 1]
        beta, gc, gcr, gt = _dn_scalars(gb_ref[0], mi, cb)
        q_, k_, v_ = _dn_heads(q_ref, cb), _dn_heads(k_ref, cb), _dn_heads(v_ref, cb)
        decay = jnp.exp(jnp.where(mi > 0, gc - gcr, 0.0)) * mi
        e = jnp.exp(gc)
        a = ms * (beta * _bdot(k_, k_, BNT, hi) * decay)
        x = -a
        eye = (lax.broadcasted_iota(jnp.int32, (CH, CH), 0) == lax.broadcasted_iota(jnp.int32, (CH, CH), 1)).astype(F32)
        t = eye + x
        p = x
        for _ in range(5):
            p = _bdot3(p, p, BNN, hi)
            t = t + _bdot3(t, p, BNN, hi)
        uw = _bdot(t, jnp.concatenate([beta * v_, (beta * e) * k_], axis=2), BNN, hi)
        put(u_ref, uw[:, :, :HD])
        put(w_ref, uw[:, :, HD:])
        put(qg_ref, e * q_)
        put(kd_ref, jnp.exp(gt - gc) * k_)
        put(pm_ref, _bdot(q_, k_, BNT, hi) * decay)
        put(t_ref, t)

    tok, _, gbs, msk, per_chunk = _dn_group_specs(cb)
    big = lambda dt: jax.ShapeDtypeStruct((2, n_chunks, NH, CH, HD), dt)
    sq = jax.ShapeDtypeStruct((2, n_chunks, NH, CH, CH), BF16)
    return pl.pallas_call(
        body, name="dn_intra_fwd", grid=(2, n_chunks // cb),
        in_specs=[tok, tok, tok, gbs, msk],
        out_specs=[per_chunk(NH, CH, HD)] * 4 + [per_chunk(NH, CH, CH)] * 2,
        out_shape=[big(F32), big(BF16), big(BF16), big(BF16), sq, sq],
        compiler_params=_cparams(("parallel", "parallel")),
    )(q, k, v, gb, masks)


def _dn_seq_fwd(u, w, qg, kd, pm, gb, n_ctx_chunks, hi):
    n_chunks = u.shape[1]
    t_all = n_chunks * CH

    cb = DN_SEQ_CB
    _, _, tok_d, gbs, per_chunk, slot = _dn_specs(n_ctx_chunks, n_chunks, False, cb)

    def body(u_ref, w_ref, qg_ref, kd_ref, pm_ref, gb_ref, o_ref, sh_ref, vn_ref, s_scr):
        @pl.when(pl.program_id(1) == 0)
        def _():
            s_scr[...] = jnp.zeros_like(s_scr)

        for t in range(cb):
            j = slot(pl.program_id(0), t)
            rows = pl.ds(pl.multiple_of(j * CH, CH), CH)
            s = s_scr[...]
            sh_ref[0, j] = s
            vn = u_ref[0, j] - _bdot(w_ref[0, j], s, BNN, hi)
            o = _bdot(qg_ref[0, j], s, BNN, hi) + _bdot(pm_ref[0, j], vn, BNN, hi)
            s_scr[...] = jnp.exp(_dn_total(gb_ref[0, rows, :])) * s + _bdot(kd_ref[0, j], vn, BTN, hi)
            vn_ref[0, j] = vn.astype(vn_ref.dtype)
            for h in range(NH):
                o_ref[0, rows, h * HD:(h + 1) * HD] = o[h]

    big = per_chunk(NH, CH, HD)
    return pl.pallas_call(
        body, name="dn_seq_fwd", grid=(2, n_chunks // cb),
        in_specs=[big, big, big, big, per_chunk(NH, CH, CH), gbs],
        out_specs=[tok_d, per_chunk(NH, HD, HD), big],
        out_shape=[jax.ShapeDtypeStruct((2, t_all, D), F32), jax.ShapeDtypeStruct((2, n_chunks, NH, HD, HD), F32),
                   jax.ShapeDtypeStruct((2, n_chunks, NH, CH, HD), BF16)],
        scratch_shapes=[pltpu.VMEM((NH, HD, HD), F32)],
        compiler_params=_cparams(("parallel", "arbitrary")),
    )(u, w, qg, kd, pm, gb)


def _dn_seq_bwd(w, qg, kd, pm, vn, s_hist, gb, do, n_ctx_chunks, hi):
    n_chunks = w.shape[1]

    cb = DN_SEQ_CB
    tok_lat, is_ctx, _, gbs, per_chunk, slot = _dn_specs(n_ctx_chunks, n_chunks, True, cb)

    def body(w_ref, qg_ref, kd_ref, pm_ref, vn_ref, sh_ref, gb_ref, do_ref, dvn_ref, dw_ref, dqg_ref, dkd_ref, del_ref, ds_scr):
        @pl.when(pl.program_id(1) == 0)
        def _():
            ds_scr[...] = jnp.zeros_like(ds_scr)

        for t in range(cb):
            j = slot(pl.program_id(0), t)
            rows = pl.ds(pl.multiple_of(j * CH, CH), CH)
            dsn = ds_scr[...]
            s = sh_ref[0, j]
            do_ = jnp.stack([do_ref[rows, h * HD:(h + 1) * HD] for h in range(NH)])
            do_ = jnp.where(is_ctx(pl.program_id(0), pl.program_id(1)), 0.0, do_)
            dvn =_bdot(pm_ref[0, j], do_, BTN, hi) + _bdot(kd_ref[0, j], dsn, BNN, hi)
            ds_scr[...] = (_bdot(qg_ref[0, j], do_, BTN, hi) + jnp.exp(_dn_total(gb_ref[0, rows, :])) * dsn
                           - _bdot(w_ref[0, j], dvn, BTN, hi))
            dvn_ref[0, j] = dvn.astype(dvn_ref.dtype)
            dw_ref[0, j] = (-_bdot(dvn, s, BNT, hi)).astype(dw_ref.dtype)
            dqg_ref[0, j] = _bdot(do_, s, BNT, hi)
            dkd_ref[0, j] = _bdot(vn_ref[0, j], dsn, BNT, hi)
            del_ref[0, j] = jnp.broadcast_to(jnp.sum(jnp.sum(s * dsn, axis=2, keepdims=True), axis=1, keepdims=True),
                                             (NH, 1, 128))

    big = per_chunk(NH, CH, HD)
    shp = lambda dt: jax.ShapeDtypeStruct((2, n_chunks, NH, CH, HD), dt)
    return pl.pallas_call(
        body, name="dn_seq_bwd", grid=(2, n_chunks // cb),
        in_specs=[big, big, big, per_chunk(NH, CH, CH), big, per_chunk(NH, HD, HD), gbs, tok_lat],
        out_specs=[big, big, big, big, per_chunk(NH, 1, 128)],
        out_shape=[shp(BF16), shp(BF16), shp(F32), shp(F32), jax.ShapeDtypeStruct((2, n_chunks, NH, 1, 128), F32)],
        scratch_shapes=[pltpu.VMEM((NH, HD, HD), F32)],
        compiler_params=_cparams(("parallel", "arbitrary")),
    )(w, qg, kd, pm, vn, s_hist, gb, do)


def _dn_intra_bwd(q, k, v, gb, u, w, t, vn, dvn, dw, dqg, dkd, de_last, do, n_ctx_chunks, hi):
    t_all = q.shape[0]
    n_chunks = t_all // CH
    masks = _dn_masks()

    cb = DN_CB
    assert n_ctx_chunks % cb == 0
    ctx_groups = n_ctx_chunks // cb

    def body(q_ref, k_ref, v_ref, gb_ref, m_ref, u_ref, w_ref, t_ref, vn_ref, dvn_ref, dw_ref, dqg_ref, dkd_ref, del_ref,
             do_ref, dq_ref, dk_ref, dv_ref, dgb_ref):
        mi, ms = m_ref[0, 0], m_ref[0, 1]
        beta, gc, gcr, gt = _dn_scalars(gb_ref[0], mi, cb)
        q_, k_, v_ = _dn_heads(q_ref, cb), _dn_heads(k_ref, cb), _dn_heads(v_ref, cb)
        do_ = jnp.where(pl.program_id(1) < ctx_groups, 0.0, _dn_heads(do_ref, cb))
        get = lambda ref: jnp.concatenate([ref[0, t_] for t_ in range(cb)], axis=0)
        decay = jnp.exp(jnp.where(mi > 0, gc - gcr, 0.0)) * mi
        e = jnp.exp(gc)
        e_last = jnp.exp(gt)
        kdfac = jnp.exp(gt - gc)
        kk = _bdot(k_, k_, BNT, hi)
        a = ms * (beta * kk * decay)
        pm = _bdot(q_, k_, BNT, hi) * decay
        kd = kdfac * k_
        dqg, dkd = get(dqg_ref), get(dkd_ref)
        dpm = _bdot(do_, get(vn_ref), BNT, hi)
        dvbkb = _bdot(get(t_ref), jnp.concatenate([get(dvn_ref), get(dw_ref)], axis=2), BTN, hi)
        dvb, dkb = dvbkb[:, :, :HD], dvbkb[:, :, HD:]
        da = -ms * _bdot(dvbkb, jnp.concatenate([get(u_ref), get(w_ref).astype(F32)], axis=2), BNT, hi)
        dqk = dpm * decay
        gm = dpm * pm + da * a
        dgc = (jnp.sum(gm, axis=2, keepdims=True)
               - _bdot3(gm, jnp.ones((cb * NH, CH, 128), F32), BTN, hi)[:, :, 0:1])
        dkk = da * (beta * decay)
        dbeta = jnp.sum(da * kk * decay, axis=2, keepdims=True)
        dk = _bdot(dkk, k_, BNN, hi) + _bdot(dkk, k_, BTN, hi) + _bdot(dqk, q_, BTN, hi)
        dq = _bdot(dqk, k_, BNN, hi) + e * dqg
        de = jnp.sum(dqg * q_, axis=2, keepdims=True)
        dv = beta * dvb
        dbeta = dbeta + jnp.sum(dvb * v_, axis=2, keepdims=True)
        skb = jnp.sum(dkb * k_, axis=2, keepdims=True)
        dk = dk + (beta * e) * dkb + kdfac * dkd
        dbeta = dbeta + e * skb
        de = de + beta * skb
        skd = jnp.sum(dkd * kd, axis=2, keepdims=True)
        dgc = dgc - skd + de * e
        dgtot = jnp.sum(skd, axis=1, keepdims=True) + get(del_ref)[:, :, 0:1] * e_last
        lane = lax.broadcasted_iota(jnp.int32, (1, 128), 1)
        for t_ in range(cb):
            rows = slice(t_ * CH, (t_ + 1) * CH)
            dbeta_all = jnp.zeros((CH, 128), F32)
            dgc_all = jnp.zeros((CH, 128), F32)
            dgtot_all = jnp.zeros((1, 128), F32)
            for h in range(NH):
                sl = slice(h * HD, (h + 1) * HD)
                b = t_ * NH + h
                dq_ref[0, rows, sl] = dq[b]
                dk_ref[0, rows, sl] = dk[b]
                dv_ref[0, rows, sl] = dv[b]
                hot_b = (lane == h).astype(F32)
                hot_g = (lane == NH + h).astype(F32)
                dbeta_all = dbeta_all + dbeta[b] * hot_b
                dgc_all = dgc_all + dgc[b] * hot_g
                dgtot_all = dgtot_all + dgtot[b] * hot_g
            dgb_ref[0, rows, :] = dbeta_all + _dot(mi, dgc_all, TN, True) + dgtot_all

    tok, tok_d, gbs, msk, per_chunk = _dn_group_specs(cb)
    tok_lat = pl.BlockSpec((cb * CH, D), lambda d, i: (jnp.maximum(i - ctx_groups, 0), 0))
    big = per_chunk(NH, CH, HD)
    return pl.pallas_call(
        body, name="dn_intra_bwd", grid=(2, n_chunks // cb),
        in_specs=[tok, tok, tok, gbs, msk, big, big, per_chunk(NH, CH, CH), big, big, big, big, big,
                  per_chunk(NH, 1, 128), tok_lat],
        out_specs=[tok_d, tok_d, tok_d, gbs],
        out_shape=[jax.ShapeDtypeStruct((2, t_all, D), F32)] * 3 + [jax.ShapeDtypeStruct((2, t_all, 128), F32)],
        compiler_params=_cparams(("parallel", "parallel")),
    )(q, k, v, gb, masks, u, w, t, vn, dvn, dw, dqg, dkd, de_last, do)


ATT_SCALE = HD ** -0.5
NEG = -1e30


def _att_stack(ref, kvh):
    return jnp.concatenate([ref[:, (kvh * GRP + g) * HD:(kvh * GRP + g + 1) * HD] for g in range(GRP)], axis=0)


def _att_col(ref, kvh):
    return jnp.concatenate([ref[:, kvh * GRP + g:kvh * GRP + g + 1] for g in range(GRP)], axis=0)


def _att_sink(sink_ref, kvh):
    return jnp.concatenate([jnp.broadcast_to(sink_ref[:, kvh * GRP + g:kvh * GRP + g + 1], (AB, 1)) for g in range(GRP)],
                           axis=0)


def _att_mask(i, nb):
    r = lax.broadcasted_iota(jnp.int32, (AB, AB), 0)
    c = lax.broadcasted_iota(jnp.int32, (AB, AB), 1)
    okp = jnp.logical_and(c >= r, i > 0)
    okn = jnp.logical_and(c <= r, i < nb - 1)
    m = jnp.concatenate([okp, jnp.ones((AB, AB), jnp.bool_), okn, jnp.ones((AB, CTX), jnp.bool_)], axis=1)
    return jnp.concatenate([m] * GRP, axis=0)


def _att_kspecs(nb):
    nc = CTX // AB
    return [pl.BlockSpec((AB, KVH * HD), lambda i: (jnp.maximum(i - 1, 0) + nc, 0)),
            pl.BlockSpec((AB, KVH * HD), lambda i: (i + nc, 0)),
            pl.BlockSpec((AB, KVH * HD), lambda i: (jnp.minimum(i + 1, nb - 1) + nc, 0)),
            pl.BlockSpec((CTX, KVH * HD), lambda i: (0, 0))]


def _attn_fwd(qr, kr, vv, sink, hi):
    tl = qr.shape[0]
    nb = tl // AB

    def body(q_ref, kp_ref, kc_ref, kn_ref, kx_ref, vp_ref, vc_ref, vn_ref, vx_ref, sink_ref, o_ref, lse_ref):
        i = pl.program_id(0)
        mask = _att_mask(i, nb)
        lane = lax.broadcasted_iota(jnp.int32, (1, 128), 1)
        lse_all = jnp.zeros((AB, 128), F32)
        for kvh in range(KVH):
            ksl = slice(kvh * HD, (kvh + 1) * HD)
            kall = jnp.concatenate([kp_ref[:, ksl], kc_ref[:, ksl], kn_ref[:, ksl], kx_ref[:, ksl]], axis=0)
            vall = jnp.concatenate([vp_ref[:, ksl], vc_ref[:, ksl], vn_ref[:, ksl], vx_ref[:, ksl]], axis=0)
            s = _dot(_att_stack(q_ref, kvh), kall, NT, hi) * ATT_SCALE
            s = jnp.where(mask, s, NEG)
            sk = _att_sink(sink_ref, kvh)
            m = jnp.maximum(jnp.max(s, axis=1, keepdims=True), sk)
            p = jnp.exp(s - m)
            l = jnp.sum(p, axis=1, keepdims=True) + jnp.exp(sk - m)
            o = _dot(p, vall, NN, hi) / l
            lse = m + jnp.log(l)
            for g in range(GRP):
                h = kvh * GRP + g
                o_ref[:, h * HD:(h + 1) * HD] = o[g * AB:(g + 1) * AB]
                lse_all = lse_all + lse[g * AB:(g + 1) * AB] * (lane == h).astype(F32)
        lse_ref[...] = lse_all

    ks = _att_kspecs(nb)
    return pl.pallas_call(
        body, name="attn_fwd", grid=(nb,),
        in_specs=[pl.BlockSpec((AB, D), lambda i: (i, 0))] + ks + ks + [pl.BlockSpec((1, 128), lambda i: (0, 0))],
        out_specs=[pl.BlockSpec((AB, D), lambda i: (i, 0)), pl.BlockSpec((AB, 128), lambda i: (i, 0))],
        out_shape=[jax.ShapeDtypeStruct((tl, D), F32), jax.ShapeDtypeStruct((tl, 128), F32)],
        compiler_params=_cparams(("parallel",)),
    )(qr, kr, kr, kr, kr, vv, vv, vv, vv, sink)


def _attn_delta(o, do):
    tl = o.shape[0]
    tr = min(512, tl)

    def body(o_ref, do_ref, d_ref):
        lane = lax.broadcasted_iota(jnp.int32, (1, 128), 1)
        acc = jnp.zeros((tr, 128), F32)
        for h in range(NH):
            sl = slice(h * HD, (h + 1) * HD)
            acc = acc + jnp.sum(o_ref[:, sl] * do_ref[:, sl], axis=1, keepdims=True) * (lane == h).astype(F32)
        d_ref[...] = acc

    return pl.pallas_call(
        body, name="attn_delta", grid=(tl // tr,),
        in_specs=[pl.BlockSpec((tr, D), lambda i: (i, 0))] * 2,
        out_specs=pl.BlockSpec((tr, 128), lambda i: (i, 0)),
        out_shape=jax.ShapeDtypeStruct((tl, 128), F32),
        compiler_params=_cparams(("parallel",)),
    )(o, do)


def _attn_bwd_q(qr, kr, vv, sink, do, lse, delta, hi):
    tl = qr.shape[0]
    nb = tl // AB

    def body(q_ref, kp_ref, kc_ref, kn_ref, kx_ref, vp_ref, vc_ref, vn_ref, vx_ref, sink_ref, do_ref, lse_ref, dl_ref,
             dq_ref, dkx_ref, dvx_ref, dsink_ref):
        i = pl.program_id(0)

        @pl.when(i == 0)
        def _():
            dkx_ref[...] = jnp.zeros_like(dkx_ref)
            dvx_ref[...] = jnp.zeros_like(dvx_ref)
            dsink_ref[...] = jnp.zeros_like(dsink_ref)

        mask = _att_mask(i, nb)
        lane = lax.broadcasted_iota(jnp.int32, (1, 128), 1)
        dsink = jnp.zeros((1, 128), F32)
        for kvh in range(KVH):
            ksl = slice(kvh * HD, (kvh + 1) * HD)
            kall = jnp.concatenate([kp_ref[:, ksl], kc_ref[:, ksl], kn_ref[:, ksl], kx_ref[:, ksl]], axis=0)
            vall = jnp.concatenate([vp_ref[:, ksl], vc_ref[:, ksl], vn_ref[:, ksl], vx_ref[:, ksl]], axis=0)
            qs = _att_stack(q_ref, kvh)
            dos = _att_stack(do_ref, kvh)
            lse_s = _att_col(lse_ref, kvh)
            dl_s = _att_col(dl_ref, kvh)
            s = _dot(qs, kall, NT, hi) * ATT_SCALE
            p = jnp.where(mask, jnp.exp(jnp.where(mask, s, NEG) - lse_s), 0.0)
            dp = _dot(dos, vall, NT, hi)
            ds = p * (dp - dl_s)
            dq = _dot(ds, kall, NN, hi) * ATT_SCALE
            dkx_ref[:, ksl] += _dot(ds[:, 3 * AB:], qs, TN, hi) * ATT_SCALE
            dvx_ref[:, ksl] += _dot(p[:, 3 * AB:], dos, TN, hi)
            psink = jnp.exp(_att_sink(sink_ref, kvh) - lse_s) * dl_s
            for g in range(GRP):
                h = kvh * GRP + g
                dq_ref[:, h * HD:(h + 1) * HD] = dq[g * AB:(g + 1) * AB]
                dsink = dsink - jnp.sum(psink[g * AB:(g + 1) * AB], axis=0, keepdims=True) * (lane == h).astype(F32)
        dsink_ref[...] += dsink

    ks = _att_kspecs(nb)
    row = pl.BlockSpec((AB, D), lambda i: (i, 0))
    col = pl.BlockSpec((AB, 128), lambda i: (i, 0))
    return pl.pallas_call(
        body, name="attn_bwd_q", grid=(nb,),
        in_specs=[row] + ks + ks + [pl.BlockSpec((1, 128), lambda i: (0, 0)), row, col, col],
        out_specs=[row, pl.BlockSpec((CTX, KVH * HD), lambda i: (0, 0)), pl.BlockSpec((CTX, KVH * HD), lambda i: (0, 0)),
                   pl.BlockSpec((1, 128), lambda i: (0, 0))],
        out_shape=[jax.ShapeDtypeStruct((tl, D), F32), jax.ShapeDtypeStruct((CTX, KVH * HD), F32),
                   jax.ShapeDtypeStruct((CTX, KVH * HD), F32), jax.ShapeDtypeStruct((1, 128), F32)],
        compiler_params=_cparams(("arbitrary",)),
    )(qr, kr, kr, kr, kr, vv, vv, vv, vv, sink, do, lse, delta)


def _attn_bwd_kv(qr, kr, vv, do, lse, delta, hi):
    tl = qr.shape[0]
    nb = tl // AB
    nc = CTX // AB

    def body(k_ref, v_ref, *refs):
        qs_refs, do_refs, lse_refs, dl_refs = refs[0:3], refs[3:6], refs[6:9], refs[9:12]
        dk_ref, dv_ref = refs[12], refs[13]
        j = pl.program_id(0)
        r = lax.broadcasted_iota(jnp.int32, (AB, AB), 0)
        c = lax.broadcasted_iota(jnp.int32, (AB, AB), 1)
        one = jnp.ones((AB, AB), jnp.bool_)
        masks = [jnp.logical_and(c <= r, j > 0), one, jnp.logical_and(c >= r, j < nb - 1)]
        for kvh in range(KVH):
            ksl = slice(kvh * HD, (kvh + 1) * HD)
            k_, v_ = k_ref[:, ksl], v_ref[:, ksl]
            dk = jnp.zeros((AB, HD), F32)
            dv = jnp.zeros((AB, HD), F32)
            for t in range(3):
                mask = jnp.concatenate([masks[t]] * GRP, axis=0)
                qs = _att_stack(qs_refs[t], kvh)
                dos = _att_stack(do_refs[t], kvh)
                lse_s = _att_col(lse_refs[t], kvh)
                dl_s = _att_col(dl_refs[t], kvh)
                s = _dot(qs, k_, NT, hi) * ATT_SCALE
                p = jnp.where(mask, jnp.exp(jnp.where(mask, s, NEG) - lse_s), 0.0)
                dp = _dot(dos, v_, NT, hi)
                ds = p * (dp - dl_s)
                dv = dv + _dot(p, dos, TN, hi)
                dk = dk + _dot(ds, qs, TN, hi) * ATT_SCALE
            dk_ref[:, ksl] = dk
            dv_ref[:, ksl] = dv

    def three(width):
        return [pl.BlockSpec((AB, width), lambda j: (jnp.maximum(j - 1, 0), 0)),
                pl.BlockSpec((AB, width), lambda j: (j, 0)),
                pl.BlockSpec((AB, width), lambda j: (jnp.minimum(j + 1, nb - 1), 0))]

    kv = pl.BlockSpec((AB, KVH * HD), lambda j: (j + nc, 0))
    out = pl.BlockSpec((AB, KVH * HD), lambda j: (j, 0))
    return pl.pallas_call(
        body, name="attn_bwd_kv", grid=(nb,),
        in_specs=[kv, kv] + three(D) + three(D) + three(128) + three(128),
        out_specs=[out, out],
        out_shape=[jax.ShapeDtypeStruct((tl, KVH * HD), F32)] * 2,
        compiler_params=_cparams(("parallel",)),
    )(kr, vv, qr, qr, qr, do, do, do, lse, lse, lse, delta, delta, delta)


def _mm(a, b, ta=False, tb=False, out_dtype=F32, tm=512, tn=1024, tk=1024, name="mm", hi=False):
    a_parts = a.shape[0] if a.ndim == 3 else 0
    b_parts = b.shape[0] if b.ndim == 3 else 0
    assert not (a_parts and ta) and not (b_parts and tb)
    if a_parts:
        m, kd = a.shape[1], a_parts * a.shape[2]
    else:
        m, kd = (a.shape[1], a.shape[0]) if ta else a.shape
    n = b_parts * b.shape[2] if b_parts else (b.shape[0] if tb else b.shape[1])
    tm, tn, tk = min(tm, m), min(tn, n), min(tk, kd)
    assert m % tm == 0 and n % tn == 0 and kd % tk == 0, (name, m, n, kd, tm, tn, tk)
    nk = kd // tk
    dims = ((0,) if ta else (1,), (1,) if tb else (0,))

    def body(a_ref, b_ref, o_ref, *scr):
        part = _dot(a_ref[0] if a_parts else a_ref[...], b_ref[0] if b_parts else b_ref[...], dims, hi)
        if nk == 1:
            o_ref[...] = part.astype(out_dtype)
        else:
            acc = scr[0]
            kk = pl.program_id(2)

            @pl.when(kk == 0)
            def _():
                acc[...] = part

            @pl.when(kk > 0)
            def _():
                acc[...] += part

            @pl.when(kk == nk - 1)
            def _():
                o_ref[...] = acc[...].astype(out_dtype)

    a_spec = pl.BlockSpec((tk, tm), lambda i, j, k: (k, i)) if ta else pl.BlockSpec((tm, tk), lambda i, j, k: (i, k))
    b_spec = pl.BlockSpec((tn, tk), lambda i, j, k: (j, k)) if tb else pl.BlockSpec((tk, tn), lambda i, j, k: (k, j))
    if a_parts:
        per = a.shape[2] // tk
        assert per * tk == a.shape[2]
        a_spec = pl.BlockSpec((1, tm, tk), lambda i, j, k: (k // per, i, k % per))
    if b_parts:
        per_n = b.shape[2] // tn
        assert per_n * tn == b.shape[2]
        b_spec = pl.BlockSpec((1, tk, tn), lambda i, j, k: (j // per_n, k, j % per_n))
    return pl.pallas_call(
        body, name=name, grid=(m // tm, n // tn, nk),
        in_specs=[a_spec, b_spec],
        out_specs=pl.BlockSpec((tm, tn), lambda i, j, k: (i, j)),
        out_shape=jax.ShapeDtypeStruct((m, n), out_dtype),
        scratch_shapes=[] if nk == 1 else [pltpu.VMEM((tm, tn), F32)],
        compiler_params=_cparams(("parallel", "parallel", "arbitrary")),
    )(a, b)


HALO = 8


class _In:
    def __init__(self, arr, w=None, cb=0, roff=0, halo=None, ridx=None):
        self.arr, self.w, self.cb, self.roff, self.halo = arr, w or arr.shape[1], cb, roff, halo
        self.ridx = ridx or (lambda i, roff=roff: i + roff)


class _Full:
    def __init__(self, arr, w=None, cb=0):
        self.arr, self.w, self.cb = arr, w, cb


class _Out:
    def __init__(self, cols, dtype=F32, w=None, cb=0, acc=False, rows=1, roff=0, nrows=None, stack=0):
        self.cols, self.dtype, self.w, self.cb, self.acc, self.rows, self.roff, self.nrows, self.stack = (
            cols, dtype, w or cols, cb, acc, rows, roff, nrows, stack)


def _rowcall(name, fn, nrow_tiles, tile, ins, outs, ncol=1):
    arrays, specs, kinds = [], [], []
    for x in ins:
        if isinstance(x, _Full):
            arrays.append(x.arr)
            if x.w is None:
                specs.append(pl.BlockSpec(x.arr.shape, lambda j, i: (0, 0)))
            else:
                specs.append(pl.BlockSpec((x.arr.shape[0], x.w), lambda j, i, cb=x.cb: (0, cb + j)))
            kinds.append("full")
            continue
        w, cb, roff = x.w, x.cb, x.roff
        cur = pl.BlockSpec((tile, w), lambda j, i, cb=cb, ridx=x.ridx: (ridx(i), cb + j))
        if x.halo is None:
            arrays.append(x.arr)
            specs.append(cur)
            kinds.append("tile")
        else:
            r8 = tile // HALO
            last = x.arr.shape[0] // HALO - 1
            prev = pl.BlockSpec((HALO, w), lambda j, i, cb=cb, roff=roff, r8=r8: (jnp.maximum((i + roff) * r8 - 1, 0), cb + j))
            nxt = pl.BlockSpec((HALO, w), lambda j, i, cb=cb, roff=roff, r8=r8, last=last:
                               (jnp.minimum((i + roff + 1) * r8, last), cb + j))
            arrays += [x.arr, x.arr, x.arr]
            specs += [prev, cur, nxt]
            kinds.append(("halo", x.halo))
    out_specs, out_shapes = [], []
    for o in outs:
        if o.acc:
            out_specs.append(pl.BlockSpec((o.rows, o.w), lambda j, i, cb=o.cb: (0, cb + j)))
            out_shapes.append(jax.ShapeDtypeStruct((o.rows, o.cols), o.dtype))
        elif o.stack:
            out_specs.append(pl.BlockSpec((o.stack, tile, o.w), lambda j, i, cb=o.cb: (0, i, cb + j)))
            out_shapes.append(jax.ShapeDtypeStruct((o.stack, nrow_tiles * tile, o.cols), o.dtype))
        else:
            out_specs.append(pl.BlockSpec((tile, o.w), lambda j, i, cb=o.cb, roff=o.roff: (i + roff, cb + j)))
            out_shapes.append(jax.ShapeDtypeStruct(((o.nrows or nrow_tiles * tile), o.cols), o.dtype))
    n_in = len(arrays)

    def body(*refs):
        j = pl.program_id(0)
        i = pl.program_id(1)
        vals, r = [], 0
        for kind in kinds:
            if kind in ("full", "tile"):
                vals.append(refs[r][...])
                r += 1
            else:
                pok, nok = kind[1]
                p, c, n = refs[r][...], refs[r + 1][...], refs[r + 2][...]
                p = jnp.where(pok(i), p, jnp.zeros_like(p))
                n = jnp.where(nok(i), n, jnp.zeros_like(n))
                vals.append(jnp.concatenate([p, c, n], axis=0))
                r += 3
        res = fn(i, j, *vals)
        for o, ref, val in zip(outs, refs[n_in:], res):
            if o.acc:
                @pl.when(i == 0)
                def _(ref=ref, val=val, o=o):
                    ref[...] = val.astype(o.dtype)

                @pl.when(i > 0)
                def _(ref=ref, val=val, o=o):
                    ref[...] += val.astype(o.dtype)
            elif o.stack:
                for s_ in range(o.stack):
                    ref[s_] = val[s_].astype(o.dtype)
            else:
                ref[...] = val.astype(o.dtype)

    return pl.pallas_call(
        body, name=name, grid=(ncol, nrow_tiles), in_specs=specs, out_specs=out_specs, out_shape=out_shapes,
        compiler_params=_cparams(("parallel", "arbitrary")),
    )(*arrays)


def _shift(xe, s, tile):
    if s == 0:
        return xe[HALO:HALO + tile]
    return pltpu.roll(xe, (-s) % xe.shape[0], 0)[HALO:HALO + tile]


def _silu(x):
    return x * jax.nn.sigmoid(x)


def _dsilu(x):
    s = jax.nn.sigmoid(x)
    return s * (1.0 + x * (1.0 - s))


def _heads(x, fn):
    return jnp.concatenate([fn(h, x[:, h * HD:(h + 1) * HD]) for h in range(x.shape[1] // HD)], axis=1)


def _colsum(x):
    return jnp.sum(x, axis=0, keepdims=True)


def _rowmean(x):
    return jnp.mean(x, axis=1, keepdims=True)


def _rowsum(x):
    return jnp.sum(x, axis=1, keepdims=True)


TILE = 256
CT = CTX // TILE


def _all_halo(n_tiles):
    return (lambda i: i >= CT + 1, lambda i: jnp.logical_and(i >= CT, i < n_tiles - 1))


def _lat_halo(n_tiles):
    return (lambda i: i >= 1, lambda i: i < n_tiles - 1)


def _rms_mod(x, nm, shift, scale):
    r = lax.rsqrt(_rowmean(x * x) + EPS)
    return (x * r * nm) * (1.0 + scale) + shift


def _rms_mod_bwd(dh, x, nm, scale):
    r = lax.rsqrt(_rowmean(x * x) + EPS)
    xn = x * r
    dz = dh * (1.0 + scale)
    dxn = dz * nm
    dx = r * (dxn - xn * _rowmean(dxn * xn))
    return dx, _colsum(dz * xn), _colsum(dh), _colsum(dh * (xn * nm))


def _norm_mod(x, ctx, nm, mod_c, mod_x):
    n = (x.shape[0] + ctx.shape[0]) // TILE

    def fn(i, j, c_, x_, nm_, mc, mx):
        m = jnp.where(i < CT, mc, mx)
        return (_rms_mod(jnp.where(i < CT, c_, x_), nm_, m[0:1], m[1:2]),)

    ins = [_In(ctx, ridx=lambda i: jnp.minimum(i, CT - 1)), _In(x, ridx=lambda i: jnp.maximum(i - CT, 0)),
           _Full(nm), _Full(mod_c), _Full(mod_x)]
    return _rowcall("norm_mod", fn, n, TILE, ins, [_Out(D, BF16)])[0]


def _norm_mod_bwd(dh, xs, dres, nm, mod, roff, n):
    ins = [_In(dh, roff=roff), _In(xs), _Full(nm), _Full(mod)] + ([] if dres is None else [_In(dres)])

    def fn(i, j, dh_, x, nm_, m, *rest):
        dx, dn, dsh, dsc = _rms_mod_bwd(dh_, x, nm_, m[1:2])
        if rest:
            return (dx + rest[0], dn, dsh, dsc)
        return (dn, dsh, dsc)

    accs = [_Out(D, acc=True), _Out(D, acc=True), _Out(D, acc=True)]
    return _rowcall("norm_mod_bwd", fn, n, TILE, ins, ([] if dres is None else [_Out(D)]) + accs)


DN_Q_SCALE = HD ** -0.5


def _conv_taps(xe, w, width, rows=None):
    r = width // 2
    acc = None
    for t in range(width):
        s = t - r
        if rows is None:
            sh = xe if s == 0 else pltpu.roll(xe, (-s) % xe.shape[0], 0)
        else:
            sh = _shift(xe, s, rows)
        term = sh * w[t:t + 1]
        acc = term if acc is None else acc + term
    return acc


def _l2n(x, scale):
    rn = lax.rsqrt(_rowsum(x * x) + EPS)
    return x * (rn * scale)


def _l2n_bwd(dy, x, scale):
    rn = lax.rsqrt(_rowsum(x * x) + EPS)
    xu = x * rn
    return (scale * rn) * (dy - xu * _rowsum(dy * xu))


def _softplus(x):
    return jnp.maximum(x, 0.0) + jnp.log(1.0 + jnp.exp(-jnp.abs(x)))


def _lane_mask(lo, hi_):
    lane = lax.broadcasted_iota(jnp.int32, (1, 128), 1)
    return jnp.logical_and(lane >= lo, lane < hi_).astype(F32)


def _dn_prep(p, conv_w, gprm):
    n = p.shape[0] // TILE
    halo = _all_halo(n)

    def fn(i, j, qe, ke, ve, ba, w, gp):
        cq = _conv_taps(qe, w[:, 0:D], 5, TILE)
        ck = _conv_taps(ke, w[:, D:2 * D], 5, TILE)
        cv = _conv_taps(ve, w[:, 2 * D:3 * D], 5, TILE)
        q = _heads(_silu(cq), lambda h, x: _l2n(x, DN_Q_SCALE))
        k = _heads(_silu(ck), lambda h, x: _l2n(x, 1.0))
        v = _silu(cv)
        beta = jax.nn.sigmoid(ba)
        g = -jnp.exp(gp[0:1]) * _softplus(ba + gp[1:2])
        m0, m1 = _lane_mask(0, 8), _lane_mask(8, 16)
        gb_f = beta * m0 + pltpu.roll(g, 128 - 8, 1) * m1
        gb_b = pltpu.roll(beta, 128 - 8, 1) * m0 + pltpu.roll(g, 128 - 16, 1) * m1
        return q, k, v, gb_f, gb_b

    ins = [_In(p, D, 0, halo=halo), _In(p, D, 1, halo=halo), _In(p, D, 2, halo=halo), _In(p, 128, C_BA // 128),
           _Full(conv_w), _Full(gprm)]
    return _rowcall("dn_prep", fn, n, TILE, ins, [_Out(D), _Out(D), _Out(D), _Out(128), _Out(128)])


def _dn_prep_bwd(p, conv_w, gprm, dq2, dk2, dv2, dgb2):
    n = p.shape[0] // TILE
    halo = _all_halo(n)

    def branch(xe, w, dye, scale):
        c = _conv_taps(xe, w, 5)
        sx = _silu(c)
        if scale is None:
            dsx = dye
        else:
            dsx = jnp.concatenate([_l2n_bwd(dye[:, h * HD:(h + 1) * HD], sx[:, h * HD:(h + 1) * HD], scale)
                                   for h in range(NH)], axis=1)
        dc = dsx * _dsilu(c)
        dx = None
        dws = []
        dcc = dc[HALO:HALO + TILE]
        for t in range(5):
            term = _shift(dc, 2 - t, TILE) * w[t:t + 1]
            dx = term if dx is None else dx + term
            dws.append(_colsum(dcc * _shift(xe, t - 2, TILE)))
        dw = jnp.concatenate(dws + [jnp.zeros((3, D), F32)], axis=0)
        return dx, dw

    def fn(i, j, qe, ke, ve, ba, w, gp, dq0, dq1, dk0, dk1, dv0, dv1, dg0, dg1):
        dxq, dwq = branch(qe, w[:, 0:D], dq0 + dq1, DN_Q_SCALE)
        dxk, dwk = branch(ke, w[:, D:2 * D], dk0 + dk1, 1.0)
        dxv, dwv = branch(ve, w[:, 2 * D:3 * D], dv0 + dv1, None)
        m0, m1 = _lane_mask(0, 8), _lane_mask(8, 16)
        dbeta = dg0 * m0 + pltpu.roll(dg1 * m0, 8, 1)
        dg = pltpu.roll(dg0 * m1, 8, 1) + pltpu.roll(dg1 * m1, 16, 1)
        beta = jax.nn.sigmoid(ba)
        ea = jnp.exp(gp[0:1])
        z = ba + gp[1:2]
        g = -ea * _softplus(z)
        mg = _lane_mask(16, 32)
        da = dg * (-ea) * jax.nn.sigmoid(z) * mg
        dba = dbeta * beta * (1.0 - beta) * _lane_mask(0, 16) + da
        dgp = jnp.concatenate([_colsum(dg * g * mg), _colsum(da)], axis=0)
        return (jnp.concatenate([dxq, dxk, dxv], axis=1), dba, jnp.concatenate([dwq, dwk, dwv], axis=1), dgp)

    ins = [_In(p, D, 0, halo=halo), _In(p, D, 1, halo=halo), _In(p, D, 2, halo=halo), _In(p, 128, C_BA // 128),
           _Full(conv_w), _Full(gprm),
           _In(dq2, halo=halo), _In(dq2, roff=n, halo=halo), _In(dk2, halo=halo), _In(dk2, roff=n, halo=halo),
           _In(dv2, halo=halo), _In(dv2, roff=n, halo=halo), _In(dgb2), _In(dgb2, roff=n)]
    return _rowcall("dn_prep_bwd", fn, n, TILE, ins,
                    [_Out(3 * D, BF16), _Out(128, BF16), _Out(3 * D, acc=True, rows=8), _Out(128, acc=True, rows=2)])


def _hnorm(x, w):
    return x * lax.rsqrt(_rowmean(x * x) + EPS) * w


def _hnorm_bwd(dy, x, w):
    r = lax.rsqrt(_rowmean(x * x) + EPS)
    xh = x * r
    dxh = dy * w
    return r * (dxh - xh * _rowmean(dxh * xh)), _colsum(dy * xh)


def _dn_gate(o2, p, dn_norm, n_all):
    n = n_all - CT

    def fn(i, j, of, ob, gt, w):
        o = of + ob
        return (_heads(o, lambda h, x: _hnorm(x, w)) * _silu(gt),)

    ins = [_In(o2, roff=CT), _In(o2, roff=n_all + CT), _In(p, D, C_GT // D, roff=CT), _Full(dn_norm)]
    return _rowcall("dn_gate", fn, n, TILE, ins, [_Out(D, BF16)])[0]


def _dn_gate_bwd(dy, o2, p, dn_norm, n_all):
    n = n_all - CT

    def fn(i, j, dy_, of, ob, gt, w):
        o = of + ob
        sg = _silu(gt)
        dos, dw = [], jnp.zeros((1, HD), F32)
        yn = []
        for h in range(NH):
            sl = slice(h * HD, (h + 1) * HD)
            dx, dwh = _hnorm_bwd(dy_[:, sl] * sg[:, sl], o[:, sl], w)
            dos.append(dx)
            dw = dw + dwh
            yn.append(_hnorm(o[:, sl], w))
        dgt = dy_ * jnp.concatenate(yn, axis=1) * _dsilu(gt)
        return jnp.concatenate(dos, axis=1), dgt, dw

    ins = [_In(dy), _In(o2, roff=CT), _In(o2, roff=n_all + CT), _In(p, D, C_GT // D, roff=CT), _Full(dn_norm)]
    return _rowcall("dn_gate_bwd", fn, n, TILE, ins, [_Out(D), _Out(D, BF16), _Out(HD, acc=True)])


def _rope_shuffle(x):
    lane = lax.broadcasted_iota(jnp.int32, (1, HD), 1)
    return jnp.where((lane % 64) < 32, pltpu.roll(x, HD - 32, 1), pltpu.roll(x, 32, 1))


def _rope(x, cos, sin):
    return x * cos + _rope_shuffle(x) * sin


def _rope_bwd(dy, cos, sin):
    return dy * cos + _rope_shuffle(dy * sin)


def _attn_prep(p, w, cos, sin, width, cb, roff, n, name):
    def fn(i, j, x, w_, c, s):
        return (_heads(x, lambda h, xh: _rope(_hnorm(xh, w_), c, s)),)

    ins = [_In(p, width, cb, roff=roff), _Full(w), _In(cos), _In(sin)]
    return _rowcall(name, fn, n, TILE, ins, [_Out(width)])[0]


def _attn_prep_bwd(dy, p, w, cos, sin, width, cb, roff, n, name):
    def fn(i, j, dy_, x, w_, c, s):
        dxs, dw = [], jnp.zeros((1, HD), F32)
        for h in range(width // HD):
            sl = slice(h * HD, (h + 1) * HD)
            dx, dwh = _hnorm_bwd(_rope_bwd(dy_[:, sl], c, s), x[:, sl], w_)
            dxs.append(dx)
            dw = dw + dwh
        return jnp.concatenate(dxs, axis=1), dw

    ins = [_In(dy), _In(p, width, cb, roff=roff), _Full(w), _In(cos), _In(sin)]
    return _rowcall(name, fn, n, TILE, ins, [_Out(width, BF16), _Out(HD, acc=True)])


def _merge(z_dn, z_at, p, n):
    def fn(i, j, zd, za, gd, ga):
        return (jax.nn.sigmoid(gd) * zd + jax.nn.sigmoid(ga) * za,)

    ins = [_In(z_dn), _In(z_at), _In(p, D, C_MG // D, roff=CT), _In(p, D, C_MG // D + 1, roff=CT)]
    return _rowcall("merge", fn, n, TILE, ins, [_Out(D, BF16)])[0]


def _merge_bwd(dm, z_dn, z_at, p, n):
    def fn(i, j, dm_, zd, za, gd, ga):
        sd, sa = jax.nn.sigmoid(gd), jax.nn.sigmoid(ga)
        dg = jnp.concatenate([dm_ * zd * sd * (1.0 - sd), dm_ * za * sa * (1.0 - sa)], axis=1)
        return dm_ * sd, dm_ * sa, dg

    ins = [_In(dm), _In(z_dn), _In(z_at), _In(p, D, C_MG // D, roff=CT), _In(p, D, C_MG // D + 1, roff=CT)]
    return _rowcall("merge_bwd", fn, n, TILE, ins, [_Out(D, BF16), _Out(D, BF16), _Out(2 * D, BF16)])


def _resid_norm(xa, mo, g_a, nf, mod_f, n):
    def fn(i, j, x, mo_, ga, nf_, m):
        x1 = x + ga * mo_
        return x1, _rms_mod(x1, nf_, m[0:1], m[1:2])

    ins = [_In(xa), _In(mo), _Full(g_a), _Full(nf), _Full(mod_f)]
    return _rowcall("resid_norm", fn, n, TILE, ins, [_Out(D), _Out(D, BF16)])


def _resid_norm_bwd(dy, dh2, x1, mo, g_a, nf, mod_f, n):
    def fn(i, j, dy_, dh_, x1_, mo_, ga, nf_, m):
        dx, dn, dsh, dsc = _rms_mod_bwd(dh_, x1_, nf_, m[1:2])
        dx1 = dy_ + dx
        return dx1, ga * dx1, dn, dsh, dsc, _colsum(dx1 * mo_)

    ins = [_In(dy), _In(dh2), _In(x1), _In(mo), _Full(g_a), _Full(nf), _Full(mod_f)]
    accs = [_Out(D, acc=True) for _ in range(4)]
    return _rowcall("resid_norm_bwd", fn, n, TILE, ins, [_Out(D), _Out(D, BF16)] + accs)


def _loss_head(x1, f, tgt, g_f, n):
    def fn(i, j, x1_, f_, t, gf):
        e = x1_ + gf * f_ - t
        dy = e * (1.0 / D)
        loss = _colsum(_rowsum(e * e)) * (0.5 / D)
        return dy, gf * dy, _colsum(dy * f_), jnp.broadcast_to(loss, (1, 128))

    ins = [_In(x1), _In(f), _In(tgt), _Full(g_f)]
    return _rowcall("loss_head", fn, n, TILE, ins, [_Out(D), _Out(D, BF16), _Out(D, acc=True), _Out(128, acc=True)])


FW = DFF // 2


def _ffn_act(u, conv_w, conv_b, n):
    halo = _lat_halo(n)

    def fn(i, j, ge, ve, wg, wv, bg, bv):
        cg = _conv_taps(ge, wg, 3, TILE) + bg
        cv = _conv_taps(ve, wv, 3, TILE) + bv
        return (_silu(cg) * cv,)

    ins = [_In(u, FW, 0, halo=halo), _In(u, FW, 2, halo=halo), _Full(conv_w, FW, 0), _Full(conv_w, FW, 2),
           _Full(conv_b, FW, 0), _Full(conv_b, FW, 2)]
    return _rowcall("ffn_act", fn, n, TILE, ins, [_Out(DFF, BF16, FW)], ncol=2)[0]


def _ffn_act_bwd(u, da, conv_w, conv_b, n):
    halo = _lat_halo(n)

    def fn(i, j, ge, ve, dae, wg, wv, bg, bv):
        cg = _conv_taps(ge, wg, 3) + bg
        cv = _conv_taps(ve, wv, 3) + bv
        dcg = dae * cv * _dsilu(cg)
        dcv = dae * _silu(cg)
        outs = []
        for dc, xe, w in ((dcg, ge, wg), (dcv, ve, wv)):
            dx, dws = None, []
            dcc = dc[HALO:HALO + TILE]
            for t in range(3):
                term = _shift(dc, 1 - t, TILE) * w[t:t + 1]
                dx = term if dx is None else dx + term
                dws.append(_colsum(dcc * _shift(xe, t - 1, TILE)))
            outs.append((dx, jnp.concatenate(dws + [jnp.zeros((5, FW), F32)], axis=0), _colsum(dcc)))
        return (outs[0][0], outs[1][0]), outs[0][1], outs[1][1], outs[0][2], outs[1][2]

    ins = [_In(u, FW, 0, halo=halo), _In(u, FW, 2, halo=halo), _In(da, FW, 0, halo=halo),
           _Full(conv_w, FW, 0), _Full(conv_w, FW, 2), _Full(conv_b, FW, 0), _Full(conv_b, FW, 2)]
    outs = [_Out(DFF, BF16, FW, stack=2), _Out(DFF, w=FW, acc=True, rows=8), _Out(DFF, w=FW, acc=True, rows=8),
            _Out(DFF, w=FW, acc=True), _Out(DFF, w=FW, acc=True)]
    return _rowcall("ffn_act_bwd", fn, n, TILE, ins, outs, ncol=2)


def _rope_tables(tl):
    rows = tl // GRID_W
    inv = np.float32(ROPE_BASE) ** (-np.arange(32, dtype=np.float32) / np.float32(32))
    ar = np.arange(rows, dtype=np.float32)[:, None] * inv
    ac = np.arange(GRID_W, dtype=np.float32)[:, None] * inv

    def table(r, c):
        full = (rows, GRID_W, HD // 2)
        return jnp.concatenate([jnp.broadcast_to(jnp.asarray(r)[:, None, :], full),
                                jnp.broadcast_to(jnp.asarray(c)[None, :, :], full)], axis=2).reshape(tl, HD)

    two = lambda a, b: np.concatenate([a, b], axis=1).astype(np.float32)
    cos = table(two(np.cos(ar), np.cos(ar)), two(np.cos(ac), np.cos(ac)))
    sin = table(two(-np.sin(ar), np.sin(ar)), two(-np.sin(ac), np.sin(ac)))
    return cos, sin


def _pad_w_in(w_in):
    z = lambda n: jnp.zeros((D, n), w_in.dtype)
    return jnp.concatenate([w_in[:, 0:4096], w_in[:, 4128:5152], w_in[:, 5664:7712], w_in[:, 5152:5664],
                            w_in[:, 4096:4128], z(96 + PW - C_PAD)], axis=1)


def _unpad_w_in(g, axis=1):
    cut = lambda a, b: lax.slice_in_dim(g, a, b, axis=axis)
    return jnp.concatenate([cut(0, 4096), cut(C_BA, C_BA + 32), cut(C_QAT, C_QAT + D), cut(C_KAT, C_KAT + 512),
                            cut(C_MG, C_MG + 2 * D)], axis=axis)


def _local_step(x, ctx, tgt, mod_x, mod_c, w, hi=False):
    tl = x.shape[0]
    t_all = tl + CTX
    n_all, n = t_all // TILE, tl // TILE
    tm_all = 1280 if t_all % 1280 == 0 else TILE
    tm_lat = 1024
    mm = functools.partial(_mm, hi=hi)
    sp = lambda m: [m[:, k * D:(k + 1) * D] for k in range(6)]
    sh_a, sc_a, g_a, sh_f, sc_f, g_f = sp(mod_x)
    sh_ac, sc_ac = sp(mod_c)[:2]
    mod_ax = jnp.concatenate([sh_a, sc_a], axis=0)
    mod_ac = jnp.concatenate([sh_ac, sc_ac], axis=0)
    mod_f = jnp.concatenate([sh_f, sc_f], axis=0)
    nm, nf = w["norm_mix"], w["norm_ffn"]
    cos, sin = _rope_tables(tl)
    cos_all = jnp.concatenate([jnp.ones((CTX, HD), F32), cos], axis=0)
    sin_all = jnp.concatenate([jnp.zeros((CTX, HD), F32), sin], axis=0)
    conv_dn = jnp.concatenate([w["dn_conv"], jnp.zeros((3, 3 * D), F32)], axis=0)
    gprm = jnp.concatenate([jnp.zeros((2, 16), F32),
                            jnp.concatenate([w["dn_a_log"].reshape(1, 16), w["dn_dt_bias"].reshape(1, 16)], axis=0),
                            jnp.zeros((2, 96), F32)], axis=1)
    conv_ff = jnp.concatenate([w["ffn_conv"], jnp.zeros((5, 2 * DFF), F32)], axis=0)
    sink = jnp.concatenate([w["attn_sink"].reshape(1, NH), jnp.zeros((1, 128 - NH), F32)], axis=1)
    nct = CTX // CH

    h = _norm_mod(x, ctx, nm, mod_ac, mod_ax)
    p = mm(h, w["w_in_p"], tm=tm_all, tn=1024, name="mm_in")
    q, k, v, gb_f, gb_b = _dn_prep(p, conv_dn, gprm)
    gb = jnp.stack([gb_f, gb_b])
    dn_u, dn_w, dn_qg, dn_kd, dn_pm, dn_t = _dn_intra_fwd(q, k, v, gb, nct, hi)
    o2, s_hist, dn_vn = _dn_seq_fwd(dn_u, dn_w, dn_qg, dn_kd, dn_pm, gb, nct, hi)
    o2 = o2.reshape(2 * t_all, D)
    y_dn = _dn_gate(o2, p, w["dn_norm"], n_all)
    qr = _attn_prep(p, w["q_norm"], cos, sin, D, C_QAT // D, CT, n, "attn_prep_q")
    kr = _attn_prep(p, w["k_norm"], cos_all, sin_all, KVH * HD, C_KAT // (KVH * HD), 0, n_all, "attn_prep_k")
    vv = p[:, C_VAT:C_VAT + KVH * HD]
    o_at, lse = _attn_fwd(qr, kr, vv, sink, hi)
    z_dn = mm(y_dn, w["w_branch_dn"], tm=tm_lat, name="mm_bdn")
    z_at = mm(o_at, w["w_branch_attn"], tm=tm_lat, name="mm_bat")
    merged = _merge(z_dn, z_at, p, n)
    mo = mm(merged, w["w_out"], tm=tm_lat, name="mm_out")
    x1, h2 = _resid_norm(x, mo, g_a, nf, mod_f, n)
    u = mm(h2, w["ffn_up"], tm=2 * tm_lat, tn=1408, name="mm_up")
    a = _ffn_act(u, conv_ff, w["ffn_conv_b"], n)
    f = mm(a, w["ffn_down"], tm=tm_lat, tk=DFF, name="mm_down")
    dy, df, dg_f, loss = _loss_head(x1, f, tgt, g_f, n)

    g = {}
    da = mm(df, w["ffn_down"], tb=True, tm=tm_lat, tn=1408, name="mm_down_dx")
    g["ffn_down"] = mm(a, df, ta=True, tm=1408, tn=1024, tk=tm_lat, name="mm_down_dw")
    du, dcw_g, dcw_v, dcb_g, dcb_v = _ffn_act_bwd(u, da, conv_ff, w["ffn_conv_b"], n)
    g["ffn_conv"] = jnp.concatenate([dcw_g, dcw_v], axis=1)[0:3]
    g["ffn_conv_b"] = jnp.concatenate([dcb_g, dcb_v], axis=1)
    dh2 = mm(du, w["ffn_up"], tb=True, tm=tm_lat, tk=1408, name="mm_up_dx")
    g["ffn_up"] = mm(h2, du, ta=True, tm=1024, tn=1408, tk=tm_lat, name="mm_up_dw")
    dx1, dmo, g["norm_ffn"], dsh_f, dsc_f, dg_a = _resid_norm_bwd(dy, dh2, x1, mo, g_a, nf, mod_f, n)
    dmerged = mm(dmo, w["w_out"], tb=True, tm=tm_lat, name="mm_out_dx")
    g["w_out"] = mm(merged, dmo, ta=True, tm=1024, tk=tm_lat, name="mm_out_dw")
    dz_dn, dz_at, dmg = _merge_bwd(dmerged, z_dn, z_at, p, n)
    dy_dn = mm(dz_dn, w["w_branch_dn"], tb=True, tm=tm_lat, name="mm_bdn_dx")
    g["w_branch_dn"] = mm(y_dn, dz_dn, ta=True, tm=1024, tk=tm_lat, name="mm_bdn_dw")
    do_at = mm(dz_at, w["w_branch_attn"], tb=True, tm=tm_lat, name="mm_bat_dx")
    g["w_branch_attn"] = mm(o_at, dz_at, ta=True, tm=1024, tk=tm_lat, name="mm_bat_dw")

    do_dn, dgt, g["dn_norm"] = _dn_gate_bwd(dy_dn, o2, p, w["dn_norm"], n_all)
    do_all = do_dn
    dn_dvn, dn_dw, dn_dqg, dn_dkd, dn_del = _dn_seq_bwd(dn_w, dn_qg, dn_kd, dn_pm, dn_vn, s_hist, gb, do_all, nct, hi)
    dq2, dk2, dv2, dgb2 = _dn_intra_bwd(q, k, v, gb, dn_u, dn_w, dn_t, dn_vn, dn_dvn, dn_dw, dn_dqg, dn_dkd, dn_del,
                                        do_all, nct, hi)
    dqkv, dba, dconv, dgprm = _dn_prep_bwd(p, conv_dn, gprm, dq2.reshape(2 * t_all, D), dk2.reshape(2 * t_all, D),
                                           dv2.reshape(2 * t_all, D), dgb2.reshape(2 * t_all, 128))
    g["dn_conv"] = dconv[0:5]
    g["dn_a_log"] = dgprm[0, 16:32].reshape(2, NH)
    g["dn_dt_bias"] = dgprm[1, 16:32].reshape(2, NH)

    delta = _attn_delta(o_at, do_at)
    dqr, dkx, dvx, dsink = _attn_bwd_q(qr, kr, vv, sink, do_at, lse, delta, hi)
    dk_lat, dv_lat = _attn_bwd_kv(qr, kr, vv, do_at, lse, delta, hi)
    g["attn_sink"] = dsink[:, 0:NH]
    dq_at, g["q_norm"] = _attn_prep_bwd(dqr, p, w["q_norm"], cos, sin, D, C_QAT // D, CT, n, "attn_prep_q_bwd")
    dkr = jnp.concatenate([dkx, dk_lat], axis=0)
    dk_at, g["k_norm"] = _attn_prep_bwd(dkr, p, w["k_norm"], cos_all, sin_all, KVH * HD, C_KAT // (KVH * HD), 0, n_all,
                                        "attn_prep_k_bwd")
    dv_at = jnp.concatenate([dvx, dv_lat], axis=0).astype(BF16)

    zc = lambda width: jnp.zeros((CTX, width), BF16)
    dp = jnp.concatenate([
        dqkv,
        jnp.concatenate([zc(D), dgt], axis=0),
        jnp.concatenate([zc(D), dq_at], axis=0),
        jnp.concatenate([zc(2 * D), dmg], axis=0),
        dk_at, dv_at, dba, jnp.zeros((t_all, PW - C_PAD), BF16)], axis=1)
    dh = mm(dp, w["w_in_p"], tb=True, tm=tm_all, tn=1024, tk=2048, name="mm_in_dx")
    g["w_in_p"] = mm(h, dp, ta=True, tm=1024, tn=2048, tk=tm_all, name="mm_in_dw")
    dnm_c, dsh_ac, dsc_ac = _norm_mod_bwd(dh, ctx, None, nm, mod_ac, 0, CT)
    grad_x, dnm_x, dsh_a, dsc_a = _norm_mod_bwd(dh, x, dx1, nm, mod_ax, CT, n)
    g["norm_mix"] = dnm_c + dnm_x
    dmod_x = jnp.concatenate([dsh_a, dsc_a, dg_a, dsh_f, dsc_f, dg_f], axis=1)
    dmod_c = jnp.concatenate([dsh_ac, dsc_ac, jnp.zeros((1, 4 * D), F32)], axis=1)
    return loss, grad_x, g, dmod_x, dmod_c


def _sum_slots(buf, n_slots, rows, tile, name, stride=1):
    nt = rows // tile

    def fn(i, j, *vals):
        acc = vals[0]
        for v in vals[1:]:
            acc = acc + v
        return (acc,)

    ins = [_In(buf, roff=k * stride * nt) for k in range(n_slots)]
    return _rowcall(name, fn, nt, tile, ins, [_Out(buf.shape[1])])[0]


ADAM_LR, ADAM_B1, ADAM_B2, ADAM_EPS, ADAM_WD, ADAM_STEP = 0.001, 0.9, 0.999, 1e-08, 0.01, 10


def _row_tile(rows, cols):
    for t in (512, 256, 128, 64, 32, 16, 8):
        if rows % t == 0 and t * cols * 4 * 14 <= 40 * 1024 * 1024:
            return t
    return rows


def _adamw(w, g, m, v, name):
    shape = w.shape
    cols = shape[-1]
    rows = max(1, math.prod(shape[:-1]))
    tile = _row_tile(rows, cols)
    c1 = 1.0 / (1.0 - ADAM_B1 ** ADAM_STEP)
    c2 = 1.0 / (1.0 - ADAM_B2 ** ADAM_STEP)

    def fn(i, j, w_, g_, m_, v_):
        mn = ADAM_B1 * m_ + (1.0 - ADAM_B1) * g_
        vn = ADAM_B2 * v_ + (1.0 - ADAM_B2) * (g_ * g_)
        delta = -ADAM_LR * ((mn * c1) / (jnp.sqrt(vn * c2) + ADAM_EPS) + ADAM_WD * w_)
        return delta, mn, vn

    r2 = lambda a: a.reshape(rows, cols)
    outs = _rowcall(name, fn, rows // tile, tile, [_In(r2(w)), _In(r2(g)), _In(r2(m)), _In(r2(v))],
                    [_Out(cols), _Out(cols), _Out(cols)])
    return [o.reshape(shape) for o in outs]


MESH = pl.DeviceIdType.MESH
ANY = pl.BlockSpec(memory_space=pl.ANY)


def _pos():
    return lax.axis_index("x"), lax.axis_index("y"), lax.axis_index("c")


def _all_gather_many(blks, name):
    na = len(blks)

    def body(*refs):
        x_refs, out_refs = refs[:na], refs[na:2 * na]
        send_sems, recv_sems, local_sems = refs[2 * na:]
        x, y, c = _pos()
        me, sibling = (x, y, c), (x, y, 1 - c)
        chips = [(1 - x, y), (x, 1 - y), (1 - x, 1 - y)]

        def rows(a, px, py, pc):
            m_per = blks[a].shape[0]
            return out_refs[a].at[pl.ds(pl.multiple_of((4 * px + 2 * py + pc) * m_per, 8), m_per), :]

        def copy(a, k, block, to, src=None):
            return pltpu.make_async_remote_copy(
                src_ref=rows(a, *block) if src is None else src, dst_ref=rows(a, *block),
                send_sem=send_sems.at[7 * a + k], recv_sem=recv_sems.at[7 * a + k], device_id=to, device_id_type=MESH)

        every = range(na)
        mine = [pltpu.make_async_copy(x_refs[a], rows(a, *me), local_sems.at[a]) for a in every]
        for cp in mine:
            cp.start()
        first = [copy(a, 0, me, sibling, src=x_refs[a]) for a in every]
        first += [copy(a, 1 + j, me, (*chip, c), src=x_refs[a]) for j, chip in enumerate(chips) for a in every]
        for cp in first:
            cp.start()
        passed = []
        for j, chip in enumerate(chips):
            for a in every:
                copy(a, 1 + j, (*chip, c), me).wait_recv()
                passed.append(copy(a, 4 + j, (*chip, c), sibling))
                passed[-1].start()
        for a in every:
            copy(a, 0, sibling, me).wait_recv()
        for j, chip in enumerate(chips):
            for a in every:
                copy(a, 4 + j, (*chip, 1 - c), me).wait_recv()
        for cp in first + passed:
            cp.wait_send()
        for cp in mine:
            cp.wait()

    return pl.pallas_call(
        body, name=name,
        out_shape=[jax.ShapeDtypeStruct((N_DEV * b.shape[0], b.shape[1]), b.dtype) for b in blks],
        in_specs=[ANY] * na, out_specs=[ANY] * na,
        scratch_shapes=[pltpu.SemaphoreType.DMA((7 * na,)), pltpu.SemaphoreType.DMA((7 * na,)), pltpu.SemaphoreType.DMA((na,))],
        compiler_params=pltpu.CompilerParams(has_side_effects=True),
    )(*blks)


def _all_gather(blk, name):
    return _all_gather_many([blk], name)[0]


def _flip(v, bit):
    return 1 - v if bit else v


D2D_STREAMS = 8
ICI_STREAMS = 2


def _sibling_exchange(src, seg_rows, n_seg, paired, name):
    n = src.shape[1]
    per_seg = D2D_STREAMS // n_seg
    per = seg_rows // per_seg
    assert per_seg * n_seg == D2D_STREAMS and per * per_seg == seg_rows and per % 16 == 0

    def body(x_ref, out_ref, send_sems, recv_sems):
        x, y, c = _pos()
        copies = []
        for s in range(n_seg):
            base = (2 * s + (1 - c)) * seg_rows if paired else s * seg_rows
            for j in range(per_seg):
                i = s * per_seg + j
                cp = pltpu.make_async_remote_copy(
                    src_ref=x_ref.at[pl.ds(pl.multiple_of(base + j * per, 16), per), :],
                    dst_ref=out_ref.at[pl.ds(s * seg_rows + j * per, per), :],
                    send_sem=send_sems.at[i], recv_sem=recv_sems.at[i], device_id=(x, y, 1 - c), device_id_type=MESH)
                cp.start()
                copies.append(cp)
        for cp in copies:
            cp.wait_recv()
        for cp in copies:
            cp.wait_send()

    return pl.pallas_call(
        body, name=name, out_shape=jax.ShapeDtypeStruct((n_seg * seg_rows, n), src.dtype),
        in_specs=[ANY], out_specs=ANY,
        scratch_shapes=[pltpu.SemaphoreType.DMA((D2D_STREAMS,)), pltpu.SemaphoreType.DMA((D2D_STREAMS,))],
        compiler_params=pltpu.CompilerParams(has_side_effects=True),
    )(src)


def _transpose_cast(x, dtype, name):
    r, c = x.shape
    tc = 512

    def body(x_ref, o_ref):
        o_ref[...] = x_ref[...].T.astype(o_ref.dtype)

    return pl.pallas_call(
        body, name=name, grid=(c // tc,),
        in_specs=[pl.BlockSpec((r, tc), lambda j: (0, j))], out_specs=pl.BlockSpec((tc, r), lambda j: (j, 0)),
        out_shape=jax.ShapeDtypeStruct((c, r), dtype), compiler_params=_cparams(("parallel",)),
    )(x)


def _chip_exchange(buf, rows, name):
    n = buf.shape[1]
    per = rows // ICI_STREAMS
    assert per * ICI_STREAMS == rows and per % 16 == 0

    def body(x_ref, out_ref, send_sems, recv_sems):
        x, y, c = _pos()
        copies = []
        for k in range(1, 4):
            px, py = _flip(x, k & 2), _flip(y, k & 1)
            for j in range(ICI_STREAMS):
                i = (k - 1) * ICI_STREAMS + j
                cp = pltpu.make_async_remote_copy(
                    src_ref=x_ref.at[pl.ds(pl.multiple_of((2 * px + py) * rows + j * per, 16), per), :],
                    dst_ref=out_ref.at[pl.ds((k - 1) * rows + j * per, per), :],
                    send_sem=send_sems.at[i], recv_sem=recv_sems.at[i], device_id=(px, py, c), device_id_type=MESH)
                cp.start()
                copies.append(cp)
        for cp in copies:
            cp.wait_recv()
        for cp in copies:
            cp.wait_send()

    return pl.pallas_call(
        body, name=name, out_shape=jax.ShapeDtypeStruct((3 * rows, n), buf.dtype),
        in_specs=[ANY], out_specs=ANY,
        scratch_shapes=[pltpu.SemaphoreType.DMA((3 * ICI_STREAMS,)), pltpu.SemaphoreType.DMA((3 * ICI_STREAMS,))],
        compiler_params=pltpu.CompilerParams(has_side_effects=True),
    )(buf)


def _add_rows(parts, rows, dtype, name):
    tile = 1024
    ins = [_In(a, roff=r0 // tile) for a, r0 in parts]

    def fn(i, j, *vals):
        acc = vals[0].astype(F32)
        for v_ in vals[1:]:
            acc = acc + v_.astype(F32)
        return (acc,)

    return _rowcall(name, fn, rows // tile, tile, ins, [_Out(parts[0][0].shape[1], dtype)])[0]


BIG = ("w_in", "w_branch_dn", "w_branch_attn", "w_out", "ffn_up", "ffn_down")
BIG_SHARD = {"w_in": (1024, 1928, True), "w_branch_dn": (256, 1024, False), "w_branch_attn": (256, 1024, False),
             "w_out": (256, 1024, False), "ffn_up": (1024, 1408, True), "ffn_down": (704, 1024, False)}
BIG_ROWS = {k: r * c // 2 // 128 for k, (r, c, _) in BIG_SHARD.items()}
PIECE = 19456
assert sum(BIG_ROWS.values()) <= PIECE


def _gather_weights(shards, ci):
    halves = []
    for k in BIG:
        r, c, _ = BIG_SHARD[k]
        halves.append(lax.dynamic_slice_in_dim(shards[k], ci * (r // 2), r // 2, axis=0).astype(BF16))
    out = {}
    for k, ag in zip(BIG, _all_gather_many(halves, "ag_weights")):
        r, c, by_col = BIG_SHARD[k]
        blk = ag.reshape(4, r, c)
        out[k] = jnp.transpose(blk, (1, 0, 2)).reshape(r, 4 * c) if by_col else blk.reshape(4 * r, c)
    return out


def _pack_pieces(full):
    parts = [full["w_in_t"].reshape(N_DEV, BIG_ROWS["w_in"], 128).astype(BF16)]
    for k in BIG[1:]:
        r, c, by_col = BIG_SHARD[k]
        a = full[k]
        if by_col:
            a = jnp.transpose(a.reshape(r, 4, c), (1, 0, 2))
        parts.append(a.reshape(N_DEV, BIG_ROWS[k], 128).astype(BF16))
    parts.append(jnp.zeros((N_DEV, PIECE - sum(BIG_ROWS.values()), 128), BF16))
    return jnp.concatenate(parts, axis=1).reshape(N_DEV * PIECE, 128)


def _reduce_scatter(pieces, ci, shard):
    half = N_DEV // 2 * PIECE
    theirs = _sibling_exchange(pieces, PIECE, N_DEV // 2, True, "rs_d2d")
    own = lax.dynamic_index_in_dim(pieces.reshape(N_DEV // 2, 2, PIECE, 128), ci, axis=1, keepdims=False).reshape(half, 128)
    part = _add_rows([(own, 0), (theirs, 0)], half, BF16, "rs_sum_chip")
    recv = _chip_exchange(part, PIECE, "rs_ici")
    own2 = lax.dynamic_slice_in_dim(part, shard * PIECE, PIECE, axis=0)
    mine = _add_rows([(own2, 0), (recv, 0), (recv, PIECE), (recv, 2 * PIECE)], PIECE, F32, "rs_sum_all")
    other = _sibling_exchange(mine, PIECE, 1, False, "rs_pair")
    return jnp.where(ci == 0, jnp.stack([mine, other]), jnp.stack([other, mine]))


def _unpack_shard(two):
    out, off = {}, 0
    for k in BIG:
        r, c, _ = BIG_SHARD[k]
        blk = two[:, off:off + BIG_ROWS[k]]
        out[k] = blk.reshape(c, r).T if k == "w_in" else blk.reshape(r, c)
        off += BIG_ROWS[k]
    return out


SMALL = (("dn_conv", 120), ("ffn_conv", 132), ("ffn_conv_b", 44), ("norm_mix", 8), ("norm_ffn", 8), ("dn_a_log", 1),
         ("dn_dt_bias", 1), ("dn_norm", 1), ("q_norm", 1), ("k_norm", 1), ("attn_sink", 1), ("dmod_c", 48), ("dmod_x", 48))
SMALL_ROWS = 416


def _rows128(a, rows):
    flat = a.reshape(-1)
    return jnp.concatenate([flat, jnp.zeros((rows * 128 - flat.shape[0],), F32)]).reshape(rows, 128)


def _pack_small(g):
    parts = [_rows128(g[k], r) for k, r in SMALL]
    parts.append(jnp.zeros((SMALL_ROWS - sum(r for _, r in SMALL), 128), F32))
    return jnp.concatenate(parts, axis=0)


def _unpack_small(buf, shapes):
    out, off = {}, 0
    for k, r in SMALL:
        n = math.prod(shapes[k])
        out[k] = buf[off:off + r].reshape(-1)[:n].reshape(shapes[k])
        off += r
    return out


WEIGHTS = ("c_ctx", "w_ada", "b_ada", "norm_mix", "norm_ffn", "w_in", "dn_conv", "dn_a_log", "dn_dt_bias", "dn_norm",
           "q_norm", "k_norm", "attn_sink", "w_branch_dn", "w_branch_attn", "w_out", "ffn_up", "ffn_conv", "ffn_conv_b",
           "ffn_down")


def kernel(x, c, ctx, c_ctx, w_ada, b_ada, norm_mix, norm_ffn, w_in, dn_conv, dn_a_log, dn_dt_bias, dn_norm, q_norm, k_norm, attn_sink, w_branch_dn, w_branch_attn, w_out, ffn_up, ffn_conv, ffn_conv_b, ffn_down, loss_target, m_c_ctx, m_w_ada, m_b_ada, m_norm_mix, m_norm_ffn, m_w_in, m_dn_conv, m_dn_a_log, m_dn_dt_bias, m_dn_norm, m_q_norm, m_k_norm, m_attn_sink, m_w_branch_dn, m_w_branch_attn, m_w_out, m_ffn_up, m_ffn_conv, m_ffn_conv_b, m_ffn_down, v_c_ctx, v_w_ada, v_b_ada, v_norm_mix, v_norm_ffn, v_w_in, v_dn_conv, v_dn_a_log, v_dn_dt_bias, v_dn_norm, v_q_norm, v_k_norm, v_attn_sink, v_w_branch_dn, v_w_branch_attn, v_w_out, v_ffn_up, v_ffn_conv, v_ffn_conv_b, v_ffn_down):
    args = dict(locals())
    xi, yi, ci = _pos()
    dev = 4 * xi + 2 * yi + ci
    shard = 2 * xi + yi
    chips = lambda a: a[0::2]

    blk = jnp.concatenate([_rows128(c, 8), _rows128(dn_conv, 30), _rows128(ffn_conv, 33), jnp.zeros((1, 128), F32)], axis=0)
    ag = _all_gather(blk, "ag_small_in").reshape(N_DEV, 72, 128)
    c_all = ag[:, 0:8].reshape(N_DEV, D)
    dn_conv_full = jnp.transpose(chips(ag)[:, 8:38].reshape(4, 5, 768), (1, 0, 2)).reshape(5, 3 * D)
    ffn_conv_full = jnp.transpose(chips(ag)[:, 38:71].reshape(4, 3, 1408), (1, 0, 2)).reshape(3, 2 * DFF)

    c16 = jnp.concatenate([c_all, c_ctx[None], jnp.zeros((7, D), F32)], axis=0)
    a16 = _rowcall("ada_silu", lambda i, j, v: (_silu(v),), 1, 16, [_In(c16)], [_Out(D)])[0]
    m_sh = _mm(a16, w_ada[0], tm=16, tn=512, tk=D, name="ada_fwd", hi=True)
    mod16 = chips(_all_gather(m_sh, "ag_mod").reshape(N_DEV, 16, 1536))
    mod16 = jnp.transpose(mod16, (1, 0, 2)).reshape(16, 6 * D) + b_ada
    mod_x = lax.dynamic_slice_in_dim(mod16, dev, 1, axis=0)
    mod_c = mod16[8:9]

    shards = {k: args[k][0] for k in BIG}
    wfull = _gather_weights(shards, ci)
    w = dict(wfull)
    w["w_in_p"] = _pad_w_in(wfull["w_in"])
    w.update(norm_mix=norm_mix, norm_ffn=norm_ffn, dn_conv=dn_conv_full, dn_a_log=dn_a_log[0], dn_dt_bias=dn_dt_bias[0],
             dn_norm=dn_norm, q_norm=q_norm, k_norm=k_norm, attn_sink=attn_sink, ffn_conv=ffn_conv_full, ffn_conv_b=ffn_conv_b)

    loss_part, grad_x, g, dmod_x, dmod_c = _local_step(x[0], ctx[0], loss_target[0], mod_x, mod_c, w)
    loss = lax.psum(loss_part[0, 0], ("x", "y", "c"))

    g["w_in_t"] = _unpad_w_in(_transpose_cast(g["w_in_p"], BF16, "w_in_grad_t"), axis=0)
    gshard = _unpack_shard(_reduce_scatter(_pack_pieces(g), ci, shard))

    g["dmod_c"], g["dmod_x"] = dmod_c, dmod_x
    ag_s = _all_gather(_pack_small(g), "ag_small_grads")
    shapes = {k: g[k].shape for k, _ in SMALL}
    gs = _unpack_small(_sum_slots(ag_s, N_DEV, SMALL_ROWS, SMALL_ROWS, "small_sum"), shapes)
    dx_all = ag_s.reshape(N_DEV, SMALL_ROWS, 128)[:, SMALL_ROWS - 50:SMALL_ROWS - 2].reshape(N_DEV, 6 * D)

    d16 = jnp.concatenate([dx_all, gs["dmod_c"], jnp.zeros((7, 6 * D), F32)], axis=0)
    d16_sh = lax.dynamic_slice_in_dim(d16, shard * 1536, 1536, axis=1)
    g_w_ada = _mm(a16, d16_sh, ta=True, tm=D, tn=512, tk=16, name="ada_dw", hi=True)
    g_b_ada = _rowcall("ada_db", lambda i, j, v: (_colsum(v),), 1, 16, [_In(d16)], [_Out(6 * D, acc=True)])[0]
    da_part = _mm(d16_sh, w_ada[0], tb=True, tm=16, tn=D, tk=512, name="ada_dx", hi=True)
    da_all = _all_gather(da_part, "ag_ada_dx")
    da16 = _sum_slots(da_all, 4, 16, 16, "ada_dx_sum", stride=2)
    dc16 = _rowcall("ada_dsilu", lambda i, j, d_, v: (d_ * _dsilu(v),), 1, 16, [_In(da16), _In(c16)], [_Out(D)])[0]

    grads = {
        "c_ctx": dc16[8], "w_ada": g_w_ada[None], "b_ada": g_b_ada, "norm_mix": gs["norm_mix"], "norm_ffn": gs["norm_ffn"],
        "w_in": gshard["w_in"][None],
        "dn_conv": lax.dynamic_slice_in_dim(gs["dn_conv"], shard * 768, 768, axis=1)[None],
        "dn_a_log": gs["dn_a_log"][None], "dn_dt_bias": gs["dn_dt_bias"][None], "dn_norm": gs["dn_norm"],
        "q_norm": gs["q_norm"], "k_norm": gs["k_norm"], "attn_sink": gs["attn_sink"],
        "w_branch_dn": gshard["w_branch_dn"][None], "w_branch_attn": gshard["w_branch_attn"][None],
        "w_out": gshard["w_out"][None], "ffn_up": gshard["ffn_up"][None],
        "ffn_conv": lax.dynamic_slice_in_dim(gs["ffn_conv"], shard * 1408, 1408, axis=1)[None],
        "ffn_conv_b": gs["ffn_conv_b"], "ffn_down": gshard["ffn_down"][None],
    }
    deltas, new_m, new_v = [], [], []
    for k in WEIGHTS:
        d_, m_, v_ = _adamw(args[k], grads[k], args["m_" + k], args["v_" + k], "adamw_" + k)
        deltas.append(d_)
        new_m.append(m_)
        new_v.append(v_)
    return (loss, grad_x[None], *[grads[k] for k in WEIGHTS], *deltas, *new_m, *new_v)
```

```python
import functools
import math

import numpy as np
import jax
import jax.numpy as jnp
from jax import lax
from jax.experimental import pallas as pl
from jax.experimental.pallas import tpu as pltpu

F32 = jnp.float32
BF16 = jnp.bfloat16
HI = lax.Precision.HIGHEST

D = 1024
NH = 8
HD = 128
CH = 64
CTX = 256
AB = 128
KVH = 2
GRP = 4
DFF = 2816
EPS = 1e-6
GRID_W = 64
ROPE_BASE = 10000.0
N_DEV = 8
VMEM_LIMIT = 56 * 1024 * 1024

C_QKV, C_GT, C_QAT, C_MG, C_KAT, C_VAT, C_BA, C_PAD = 0, 3072, 4096, 5120, 7168, 7424, 7680, 7808
PW = 8192


def _cparams(sem=None, **kw):
    return pltpu.CompilerParams(dimension_semantics=sem, vmem_limit_bytes=VMEM_LIMIT, **kw)


def _dot(a, b, dims, hi):
    if hi:
        return lax.dot_general(a.astype(F32), b.astype(F32), (dims, ((), ())), precision=HI, preferred_element_type=F32)
    return lax.dot_general(a.astype(BF16), b.astype(BF16), (dims, ((), ())), preferred_element_type=F32)


NN = ((1,), (0,))
NT = ((1,), (1,))
TN = ((0,), (0,))


def _dn_masks():
    i = np.arange(CH)
    lo_incl = (i[:, None] >= i[None, :]).astype(np.float32)
    lo_strict = (i[:, None] > i[None, :]).astype(np.float32)
    return jnp.asarray(np.stack([np.stack([lo_incl, lo_strict]), np.stack([lo_incl.T, lo_strict.T])]))


def _dn_chunk_index(d, i, n_ctx_chunks, n_chunks):
    fwd = i
    bwd = jnp.where(i < n_ctx_chunks, n_ctx_chunks - 1 - i, n_chunks - 1 + n_ctx_chunks - i)
    return jnp.where(d == 0, fwd, bwd)


BNN = ((2,), (1,))
BNT = ((2,), (2,))
BTN = ((1,), (1,))


def _bdot(a, b, dims, hi):
    dn = (dims, ((0,), (0,)))
    if hi:
        return lax.dot_general(a.astype(F32), b.astype(F32), dn, precision=HI, preferred_element_type=F32)
    return lax.dot_general(a.astype(BF16), b.astype(BF16), dn, preferred_element_type=F32)


def _bdot3(a, b, dims, hi):
    if hi:
        return _bdot(a, b, dims, True)
    ah, bh = a.astype(BF16), b.astype(BF16)
    al, bl = (a - ah.astype(F32)).astype(BF16), (b - bh.astype(F32)).astype(BF16)
    dn = (dims, ((0,), (0,)))
    d = lambda x_, y_: lax.dot_general(x_, y_, dn, preferred_element_type=F32)
    return d(ah, bh) + d(ah, bl) + d(al, bh)


DN_CB = 4
DN_SEQ_CB = 4


def _dn_heads(ref, cb=1):
    return jnp.stack([ref[t * CH:(t + 1) * CH, h * HD:(h + 1) * HD] for t in range(cb) for h in range(NH)])


def _dn_scalars(gb, mi, cb=1):
    beta, gc, gcr, gt = [], [], [], []
    for t in range(cb):
        g1 = gb[t * CH:(t + 1) * CH]
        gcum, gcum_t, gtot = _dn_gcum(g1, mi)
        beta += [g1[:, h:h + 1] for h in range(NH)]
        gc += [gcum[:, NH + h:NH + h + 1] for h in range(NH)]
        gcr += [gcum_t[NH + h:NH + h + 1, :] for h in range(NH)]
        gt += [gtot[:, NH + h:NH + h + 1] for h in range(NH)]
    return jnp.stack(beta), jnp.stack(gc), jnp.stack(gcr), jnp.stack(gt)


DN_NEWTON = 1


def _dn_inverse(a, hi):
    eye = (lax.broadcasted_iota(jnp.int32, (CH, CH), 0) == lax.broadcasted_iota(jnp.int32, (CH, CH), 1)).astype(F32)
    x = -a
    t = eye + x
    p = x
    if hi:
        for _ in range(5):
            p = _bdot(p, p, BNN, True)
            t = t + _bdot(t, p, BNN, True)
        return t
    for _ in range(5):
        p = _bdot(p, p, BNN, False)
        t = t + _bdot(t, p, BNN, False)
    for _ in range(DN_NEWTON):
        r = eye - t - _bdot3(a, t, BNN, False)
        t = t + _bdot(t, r, BNN, False)
    return t


def _dn_total(gb):
    gtot = jnp.sum(gb, axis=0, keepdims=True)
    return jnp.stack([gtot[:, NH + h:NH + h + 1] for h in range(NH)])


def _dn_gcum(gb, mi):
    gcum = _dot(mi, gb, NN, True)
    gtot = jnp.sum(gb, axis=0, keepdims=True)
    return gcum, gcum.T, gtot


def _dn_specs(n_ctx_chunks, n_chunks, reverse, cb):
    assert n_ctx_chunks % cb == 0 and n_chunks % cb == 0

    def grp(d, i):
        first = n_chunks - 1 - cb * i if reverse else cb * i
        return _dn_chunk_index(d, first, n_ctx_chunks, n_chunks) // cb

    def slot(d, t):
        ascending = (d == 1) if reverse else (d == 0)
        return jnp.where(ascending, t, cb - 1 - t)

    ctx_groups = n_ctx_chunks // cb
    tok_lat = pl.BlockSpec((cb * CH, D), lambda d, i: (jnp.maximum(grp(d, i) - ctx_groups, 0), 0))
    is_ctx = lambda d, i: grp(d, i) < ctx_groups
    tok_d = pl.BlockSpec((1, cb * CH, D), lambda d, i: (d, grp(d, i), 0))
    gbs = pl.BlockSpec((1, cb * CH, 128), lambda d, i: (d, grp(d, i), 0))

    def per_chunk(*tail):
        return pl.BlockSpec((1, cb) + tail, lambda d, i: (d, grp(d, i)) + (0,) * len(tail))

    return tok_lat, is_ctx, tok_d, gbs, per_chunk, slot


def _dn_group_specs(cb):
    tok = pl.BlockSpec((cb * CH, D), lambda d, i: (i, 0))
    tok_d = pl.BlockSpec((1, cb * CH, D), lambda d, i: (d, i, 0))
    gbs = pl.BlockSpec((1, cb * CH, 128), lambda d, i: (d, i, 0))
    msk = pl.BlockSpec((1, 2, CH, CH), lambda d, i: (d, 0, 0, 0))

    def per_chunk(*tail):
        return pl.BlockSpec((1, cb) + tail, lambda d, i: (d, i) + (0,) * len(tail))

    return tok, tok_d, gbs, msk, per_chunk


def _dn_intra_fwd(q, k, v, gb, n_ctx_chunks, hi):
    t_all = q.shape[0]
    n_chunks = t_all // CH
    masks = _dn_masks()

    cb = DN_CB

    def put(ref, val):
        for t_ in range(cb):
            ref[0, t_] = val[t_ * NH:(t_ + 1) * NH].astype(ref.dtype)

    def body(q_ref, k_ref, v_ref, gb_ref, m_ref, u_ref, w_ref, qg_ref, kd_ref, pm_ref, t_ref):
        mi, ms = m_ref[0, 0], m_ref[0, 1]
        beta, gc, gcr, gt = _dn_scalars(gb_ref[0], mi, cb)
        q_, k_, v_ = _dn_heads(q_ref, cb), _dn_heads(k_ref, cb), _dn_heads(v_ref, cb)
        decay = jnp.exp(jnp.where(mi > 0, gc - gcr, 0.0)) * mi
        e = jnp.exp(gc)
        a = ms * (beta * _bdot(k_, k_, BNT, hi) * decay)
        t = _dn_inverse(a, hi)
        uw =_bdot(t, jnp.concatenate([beta * v_, (beta * e) * k_], axis=2), BNN, hi)
        put(u_ref, uw[:, :, :HD])
        put(w_ref, uw[:, :, HD:])
        put(qg_ref, e * q_)
        put(kd_ref, jnp.exp(gt - gc) * k_)
        put(pm_ref, _bdot(q_, k_, BNT, hi) * decay)
        put(t_ref, t)

    tok, _, gbs, msk, per_chunk = _dn_group_specs(cb)
    big = lambda dt: jax.ShapeDtypeStruct((2, n_chunks, NH, CH, HD), dt)
    sq = jax.ShapeDtypeStruct((2, n_chunks, NH, CH, CH), BF16)
    return pl.pallas_call(
        body, name="dn_intra_fwd", grid=(2, n_chunks // cb),
        in_specs=[tok, tok, tok, gbs, msk],
        out_specs=[per_chunk(NH, CH, HD)] * 4 + [per_chunk(NH, CH, CH)] * 2,
        out_shape=[big(F32), big(BF16), big(BF16), big(BF16), sq, sq],
        compiler_params=_cparams(("parallel", "parallel")),
    )(q, k, v, gb, masks)


def _dn_seq_fwd(u, w, qg, kd, pm, gb, n_ctx_chunks, hi):
    n_chunks = u.shape[1]
    t_all = n_chunks * CH

    cb = DN_SEQ_CB
    _, _, tok_d, gbs, per_chunk, slot = _dn_specs(n_ctx_chunks, n_chunks, False, cb)

    def body(u_ref, w_ref, qg_ref, kd_ref, pm_ref, gb_ref, o_ref, sh_ref, vn_ref, s_scr):
        @pl.when(pl.program_id(1) == 0)
        def _():
            s_scr[...] = jnp.zeros_like(s_scr)

        for t in range(cb):
            j = slot(pl.program_id(0), t)
            rows = pl.ds(pl.multiple_of(j * CH, CH), CH)
            s = s_scr[...]
            sh_ref[0, j] = s.astype(sh_ref.dtype)
            vn = u_ref[0, j] - _bdot(w_ref[0, j], s, BNN, hi)
            o = _bdot(qg_ref[0, j], s, BNN, hi) + _bdot(pm_ref[0, j], vn, BNN, hi)
            s_scr[...] = jnp.exp(_dn_total(gb_ref[0, rows, :])) * s + _bdot(kd_ref[0, j], vn, BTN, hi)
            vn_ref[0, j] = vn.astype(vn_ref.dtype)
            for h in range(NH):
                o_ref[0, rows, h * HD:(h + 1) * HD] = o[h]

    big = per_chunk(NH, CH, HD)
    return pl.pallas_call(
        body, name="dn_seq_fwd", grid=(2, n_chunks // cb),
        in_specs=[big, big, big, big, per_chunk(NH, CH, CH), gbs],
        out_specs=[tok_d, per_chunk(NH, HD, HD), big],
        out_shape=[jax.ShapeDtypeStruct((2, t_all, D), F32), jax.ShapeDtypeStruct((2, n_chunks, NH, HD, HD), BF16),
                   jax.ShapeDtypeStruct((2, n_chunks, NH, CH, HD), BF16)],
        scratch_shapes=[pltpu.VMEM((NH, HD, HD), F32)],
        compiler_params=_cparams(("parallel", "arbitrary")),
    )(u, w, qg, kd, pm, gb)


def _dn_seq_bwd(w, qg, kd, pm, vn, s_hist, gb, do, n_ctx_chunks, hi):
    n_chunks = w.shape[1]

    cb = DN_SEQ_CB
    tok_lat, is_ctx, _, gbs, per_chunk, slot = _dn_specs(n_ctx_chunks, n_chunks, True, cb)

    def body(w_ref, qg_ref, kd_ref, pm_ref, vn_ref, sh_ref, gb_ref, do_ref, dvn_ref, dw_ref, dqg_ref, dkd_ref, del_ref, ds_scr):
        @pl.when(pl.program_id(1) == 0)
        def _():
            ds_scr[...] = jnp.zeros_like(ds_scr)

        for t in range(cb):
            j = slot(pl.program_id(0), t)
            rows = pl.ds(pl.multiple_of(j * CH, CH), CH)
            dsn = ds_scr[...]
            s = sh_ref[0, j]
            do_ = jnp.stack([do_ref[rows, h * HD:(h + 1) * HD] for h in range(NH)])
            do_ = jnp.where(is_ctx(pl.program_id(0), pl.program_id(1)), 0.0, do_)
            dvn =_bdot(pm_ref[0, j], do_, BTN, hi) + _bdot(kd_ref[0, j], dsn, BNN, hi)
            ds_scr[...] = (_bdot(qg_ref[0, j], do_, BTN, hi) + jnp.exp(_dn_total(gb_ref[0, rows, :])) * dsn
                           - _bdot(w_ref[0, j], dvn, BTN, hi))
            dvn_ref[0, j] = dvn.astype(dvn_ref.dtype)
            dw_ref[0, j] = (-_bdot(dvn, s, BNT, hi)).astype(dw_ref.dtype)
            dqg_ref[0, j] = _bdot(do_, s, BNT, hi)
            dkd_ref[0, j] = _bdot(vn_ref[0, j], dsn, BNT, hi)
            del_ref[0, j] = jnp.broadcast_to(jnp.sum(jnp.sum(s * dsn, axis=2, keepdims=True), axis=1, keepdims=True),
                                             (NH, 1, 128))

    big = per_chunk(NH, CH, HD)
    shp = lambda dt: jax.ShapeDtypeStruct((2, n_chunks, NH, CH, HD), dt)
    return pl.pallas_call(
        body, name="dn_seq_bwd", grid=(2, n_chunks // cb),
        in_specs=[big, big, big, per_chunk(NH, CH, CH), big, per_chunk(NH, HD, HD), gbs, tok_lat],
        out_specs=[big, big, big, big, per_chunk(NH, 1, 128)],
        out_shape=[shp(BF16), shp(BF16), shp(F32), shp(F32), jax.ShapeDtypeStruct((2, n_chunks, NH, 1, 128), F32)],
        scratch_shapes=[pltpu.VMEM((NH, HD, HD), F32)],
        compiler_params=_cparams(("parallel", "arbitrary")),
    )(w, qg, kd, pm, vn, s_hist, gb, do)


def _dn_intra_bwd(q, k, v, gb, u, w, t, vn, dvn, dw, dqg, dkd, de_last, do, n_ctx_chunks, hi):
    t_all = q.shape[0]
    n_chunks = t_all // CH
    masks = _dn_masks()

    cb = DN_CB
    assert n_ctx_chunks % cb == 0
    ctx_groups = n_ctx_chunks // cb

    def body(q_ref, k_ref, v_ref, gb_ref, m_ref, u_ref, w_ref, t_ref, vn_ref, dvn_ref, dw_ref, dqg_ref, dkd_ref, del_ref,
             do_ref, dq_ref, dk_ref, dv_ref, dgb_ref):
        mi, ms = m_ref[0, 0], m_ref[0, 1]
        beta, gc, gcr, gt = _dn_scalars(gb_ref[0], mi, cb)
        q_, k_, v_ = _dn_heads(q_ref, cb), _dn_heads(k_ref, cb), _dn_heads(v_ref, cb)
        do_ = jnp.where(pl.program_id(1) < ctx_groups, 0.0, _dn_heads(do_ref, cb))
        get = lambda ref: jnp.concatenate([ref[0, t_] for t_ in range(cb)], axis=0)
        decay = jnp.exp(jnp.where(mi > 0, gc - gcr, 0.0)) * mi
        e = jnp.exp(gc)
        e_last = jnp.exp(gt)
        kdfac = jnp.exp(gt - gc)
        kk = _bdot(k_, k_, BNT, hi)
        a = ms * (beta * kk * decay)
        pm = _bdot(q_, k_, BNT, hi) * decay
        kd = kdfac * k_
        dqg, dkd = get(dqg_ref), get(dkd_ref)
        dpm = _bdot(do_, get(vn_ref), BNT, hi)
        dvbkb = _bdot(get(t_ref), jnp.concatenate([get(dvn_ref), get(dw_ref)], axis=2), BTN, hi)
        dvb, dkb = dvbkb[:, :, :HD], dvbkb[:, :, HD:]
        da = -ms * _bdot(dvbkb, jnp.concatenate([get(u_ref), get(w_ref).astype(F32)], axis=2), BNT, hi)
        dqk = dpm * decay
        gm = dpm * pm + da * a
        dgc = (jnp.sum(gm, axis=2, keepdims=True)
               - _bdot3(gm, jnp.ones((cb * NH, CH, 128), F32), BTN, hi)[:, :, 0:1])
        dkk = da * (beta * decay)
        dbeta = jnp.sum(da * kk * decay, axis=2, keepdims=True)
        dk = _bdot(dkk, k_, BNN, hi) + _bdot(dkk, k_, BTN, hi) + _bdot(dqk, q_, BTN, hi)
        dq = _bdot(dqk, k_, BNN, hi) + e * dqg
        de = jnp.sum(dqg * q_, axis=2, keepdims=True)
        dv = beta * dvb
        dbeta = dbeta + jnp.sum(dvb * v_, axis=2, keepdims=True)
        skb = jnp.sum(dkb * k_, axis=2, keepdims=True)
        dk = dk + (beta * e) * dkb + kdfac * dkd
        dbeta = dbeta + e * skb
        de = de + beta * skb
        skd = jnp.sum(dkd * kd, axis=2, keepdims=True)
        dgc = dgc - skd + de * e
        dgtot = jnp.sum(skd, axis=1, keepdims=True) + get(del_ref)[:, :, 0:1] * e_last
        lane = lax.broadcasted_iota(jnp.int32, (1, 128), 1)
        for t_ in range(cb):
            rows = slice(t_ * CH, (t_ + 1) * CH)
            dbeta_all = jnp.zeros((CH, 128), F32)
            dgc_all = jnp.zeros((CH, 128), F32)
            dgtot_all = jnp.zeros((1, 128), F32)
            for h in range(NH):
                sl = slice(h * HD, (h + 1) * HD)
                b = t_ * NH + h
                dq_ref[0, rows, sl] = dq[b]
                dk_ref[0, rows, sl] = dk[b]
                dv_ref[0, rows, sl] = dv[b]
                hot_b = (lane == h).astype(F32)
                hot_g = (lane == NH + h).astype(F32)
                dbeta_all = dbeta_all + dbeta[b] * hot_b
                dgc_all = dgc_all + dgc[b] * hot_g
                dgtot_all = dgtot_all + dgtot[b] * hot_g
            dgb_ref[0, rows, :] = dbeta_all + _dot(mi, dgc_all, TN, True) + dgtot_all

    tok, tok_d, gbs, msk, per_chunk = _dn_group_specs(cb)
    tok_lat = pl.BlockSpec((cb * CH, D), lambda d, i: (jnp.maximum(i - ctx_groups, 0), 0))
    big = per_chunk(NH, CH, HD)
    return pl.pallas_call(
        body, name="dn_intra_bwd", grid=(2, n_chunks // cb),
        in_specs=[tok, tok, tok, gbs, msk, big, big, per_chunk(NH, CH, CH), big, big, big, big, big,
                  per_chunk(NH, 1, 128), tok_lat],
        out_specs=[tok_d, tok_d, tok_d, gbs],
        out_shape=[jax.ShapeDtypeStruct((2, t_all, D), F32)] * 3 + [jax.ShapeDtypeStruct((2, t_all, 128), F32)],
        compiler_params=_cparams(("parallel", "parallel")),
    )(q, k, v, gb, masks, u, w, t, vn, dvn, dw, dqg, dkd, de_last, do)


ATT_SCALE = HD ** -0.5
NEG = -1e30


def _att_stack(ref, kvh):
    return jnp.concatenate([ref[:, (kvh * GRP + g) * HD:(kvh * GRP + g + 1) * HD] for g in range(GRP)], axis=0)


def _att_col(ref, kvh):
    return jnp.concatenate([ref[:, kvh * GRP + g:kvh * GRP + g + 1] for g in range(GRP)], axis=0)


def _att_sink(sink_ref, kvh):
    return jnp.concatenate([jnp.broadcast_to(sink_ref[:, kvh * GRP + g:kvh * GRP + g + 1], (AB, 1)) for g in range(GRP)],
                           axis=0)


def _att_mask(i, nb):
    r = lax.broadcasted_iota(jnp.int32, (AB, AB), 0)
    c = lax.broadcasted_iota(jnp.int32, (AB, AB), 1)
    okp = jnp.logical_and(c >= r, i > 0)
    okn = jnp.logical_and(c <= r, i < nb - 1)
    return jnp.concatenate([okp] * GRP, axis=0), jnp.concatenate([okn] * GRP, axis=0)


def _att_masked(s, mask):
    mp, mn = mask
    return jnp.concatenate([jnp.where(mp, s[:, 0:AB], NEG), s[:, AB:2 * AB], jnp.where(mn, s[:, 2 * AB:3 * AB], NEG),
                            s[:, 3 * AB:]], axis=1)


def _att_kspecs(nb):
    nc = CTX // AB
    return [pl.BlockSpec((AB, KVH * HD), lambda i: (jnp.maximum(i - 1, 0) + nc, 0)),
            pl.BlockSpec((AB, KVH * HD), lambda i: (i + nc, 0)),
            pl.BlockSpec((AB, KVH * HD), lambda i: (jnp.minimum(i + 1, nb - 1) + nc, 0)),
            pl.BlockSpec((CTX, KVH * HD), lambda i: (0, 0))]


def _attn_fwd(qr, kr, vv, sink, hi):
    tl = qr.shape[0]
    nb = tl // AB

    def body(q_ref, kp_ref, kc_ref, kn_ref, kx_ref, vp_ref, vc_ref, vn_ref, vx_ref, sink_ref, o_ref, lse_ref):
        i = pl.program_id(0)
        mask = _att_mask(i, nb)
        lane = lax.broadcasted_iota(jnp.int32, (1, 128), 1)
        lse_all = jnp.zeros((AB, 128), F32)
        for kvh in range(KVH):
            ksl = slice(kvh * HD, (kvh + 1) * HD)
            kall = jnp.concatenate([kp_ref[:, ksl], kc_ref[:, ksl], kn_ref[:, ksl], kx_ref[:, ksl]], axis=0)
            vall = jnp.concatenate([vp_ref[:, ksl], vc_ref[:, ksl], vn_ref[:, ksl], vx_ref[:, ksl]], axis=0)
            s = _dot(_att_stack(q_ref, kvh), kall, NT, hi) * ATT_SCALE
            s = _att_masked(s, mask)
            sk = _att_sink(sink_ref, kvh)
            m = jnp.maximum(jnp.max(s, axis=1, keepdims=True), sk)
            p = jnp.exp(s - m)
            l = jnp.sum(p, axis=1, keepdims=True) + jnp.exp(sk - m)
            o = _dot(p, vall, NN, hi) / l
            lse = m + jnp.log(l)
            for g in range(GRP):
                h = kvh * GRP + g
                o_ref[:, h * HD:(h + 1) * HD] = o[g * AB:(g + 1) * AB]
                lse_all = lse_all + lse[g * AB:(g + 1) * AB] * (lane == h).astype(F32)
        lse_ref[...] = lse_all

    ks = _att_kspecs(nb)
    return pl.pallas_call(
        body, name="attn_fwd", grid=(nb,),
        in_specs=[pl.BlockSpec((AB, D), lambda i: (i, 0))] + ks + ks + [pl.BlockSpec((1, 128), lambda i: (0, 0))],
        out_specs=[pl.BlockSpec((AB, D), lambda i: (i, 0)), pl.BlockSpec((AB, 128), lambda i: (i, 0))],
        out_shape=[jax.ShapeDtypeStruct((tl, D), F32), jax.ShapeDtypeStruct((tl, 128), F32)],
        compiler_params=_cparams(("parallel",)),
    )(qr, kr, kr, kr, kr, vv, vv, vv, vv, sink)


def _attn_delta(o, do):
    tl = o.shape[0]
    tr = min(512, tl)

    def body(o_ref, do_ref, d_ref):
        lane = lax.broadcasted_iota(jnp.int32, (1, 128), 1)
        acc = jnp.zeros((tr, 128), F32)
        for h in range(NH):
            sl = slice(h * HD, (h + 1) * HD)
            acc = acc + jnp.sum(o_ref[:, sl] * do_ref[:, sl], axis=1, keepdims=True) * (lane == h).astype(F32)
        d_ref[...] = acc

    return pl.pallas_call(
        body, name="attn_delta", grid=(tl // tr,),
        in_specs=[pl.BlockSpec((tr, D), lambda i: (i, 0))] * 2,
        out_specs=pl.BlockSpec((tr, 128), lambda i: (i, 0)),
        out_shape=jax.ShapeDtypeStruct((tl, 128), F32),
        compiler_params=_cparams(("parallel",)),
    )(o, do)


def _attn_bwd_q(qr, kr, vv, sink, do, lse, delta, hi):
    tl = qr.shape[0]
    nb = tl // AB

    def body(q_ref, kp_ref, kc_ref, kn_ref, kx_ref, vp_ref, vc_ref, vn_ref, vx_ref, sink_ref, do_ref, lse_ref, dl_ref,
             dq_ref, dkx_ref, dvx_ref, dsink_ref):
        i = pl.program_id(0)

        @pl.when(i == 0)
        def _():
            dkx_ref[...] = jnp.zeros_like(dkx_ref)
            dvx_ref[...] = jnp.zeros_like(dvx_ref)
            dsink_ref[...] = jnp.zeros_like(dsink_ref)

        mask = _att_mask(i, nb)
        lane = lax.broadcasted_iota(jnp.int32, (1, 128), 1)
        dsink = jnp.zeros((1, 128), F32)
        for kvh in range(KVH):
            ksl = slice(kvh * HD, (kvh + 1) * HD)
            kall = jnp.concatenate([kp_ref[:, ksl], kc_ref[:, ksl], kn_ref[:, ksl], kx_ref[:, ksl]], axis=0)
            vall = jnp.concatenate([vp_ref[:, ksl], vc_ref[:, ksl], vn_ref[:, ksl], vx_ref[:, ksl]], axis=0)
            qs = _att_stack(q_ref, kvh)
            dos = _att_stack(do_ref, kvh)
            lse_s = _att_col(lse_ref, kvh)
            dl_s = _att_col(dl_ref, kvh)
            s = _dot(qs, kall, NT, hi) * ATT_SCALE
            p = jnp.exp(_att_masked(s, mask) - lse_s)
            dp = _dot(dos, vall, NT, hi)
            ds = p * (dp - dl_s)
            dq = _dot(ds, kall, NN, hi) * ATT_SCALE
            dkx_ref[:, ksl] += _dot(ds[:, 3 * AB:], qs, TN, hi) * ATT_SCALE
            dvx_ref[:, ksl] += _dot(p[:, 3 * AB:], dos, TN, hi)
            psink = jnp.exp(_att_sink(sink_ref, kvh) - lse_s) * dl_s
            for g in range(GRP):
                h = kvh * GRP + g
                dq_ref[:, h * HD:(h + 1) * HD] = dq[g * AB:(g + 1) * AB]
                dsink = dsink - jnp.sum(psink[g * AB:(g + 1) * AB], axis=0, keepdims=True) * (lane == h).astype(F32)
        dsink_ref[...] += dsink

    ks = _att_kspecs(nb)
    row = pl.BlockSpec((AB, D), lambda i: (i, 0))
    col = pl.BlockSpec((AB, 128), lambda i: (i, 0))
    return pl.pallas_call(
        body, name="attn_bwd_q", grid=(nb,),
        in_specs=[row] + ks + ks + [pl.BlockSpec((1, 128), lambda i: (0, 0)), row, col, col],
        out_specs=[row, pl.BlockSpec((CTX, KVH * HD), lambda i: (0, 0)), pl.BlockSpec((CTX, KVH * HD), lambda i: (0, 0)),
                   pl.BlockSpec((1, 128), lambda i: (0, 0))],
        out_shape=[jax.ShapeDtypeStruct((tl, D), F32), jax.ShapeDtypeStruct((CTX, KVH * HD), F32),
                   jax.ShapeDtypeStruct((CTX, KVH * HD), F32), jax.ShapeDtypeStruct((1, 128), F32)],
        compiler_params=_cparams(("arbitrary",)),
    )(qr, kr, kr, kr, kr, vv, vv, vv, vv, sink, do, lse, delta)


def _attn_bwd_kv(qr, kr, vv, do, lse, delta, hi):
    tl = qr.shape[0]
    nb = tl // AB
    nc = CTX // AB

    def body(k_ref, v_ref, *refs):
        qs_refs, do_refs, lse_refs, dl_refs = refs[0:3], refs[3:6], refs[6:9], refs[9:12]
        dk_ref, dv_ref = refs[12], refs[13]
        j = pl.program_id(0)
        r = lax.broadcasted_iota(jnp.int32, (AB, AB), 0)
        c = lax.broadcasted_iota(jnp.int32, (AB, AB), 1)
        masks = [jnp.concatenate([jnp.logical_and(c <= r, j > 0)] * GRP, axis=0), None,
                 jnp.concatenate([jnp.logical_and(c >= r, j < nb - 1)] * GRP, axis=0)]
        for kvh in range(KVH):
            ksl = slice(kvh * HD, (kvh + 1) * HD)
            k_, v_ = k_ref[:, ksl], v_ref[:, ksl]
            dk = jnp.zeros((AB, HD), F32)
            dv = jnp.zeros((AB, HD), F32)
            for t in range(3):
                qs = _att_stack(qs_refs[t], kvh)
                dos = _att_stack(do_refs[t], kvh)
                lse_s = _att_col(lse_refs[t], kvh)
                dl_s = _att_col(dl_refs[t], kvh)
                s = _dot(qs, k_, NT, hi) * ATT_SCALE
                if masks[t] is not None:
                    s = jnp.where(masks[t], s, NEG)
                p = jnp.exp(s - lse_s)
                dp = _dot(dos, v_, NT, hi)
                ds = p * (dp - dl_s)
                dv = dv + _dot(p, dos, TN, hi)
                dk = dk + _dot(ds, qs, TN, hi) * ATT_SCALE
            dk_ref[:, ksl] = dk
            dv_ref[:, ksl] = dv

    def three(width):
        return [pl.BlockSpec((AB, width), lambda j: (jnp.maximum(j - 1, 0), 0)),
                pl.BlockSpec((AB, width), lambda j: (j, 0)),
                pl.BlockSpec((AB, width), lambda j: (jnp.minimum(j + 1, nb - 1), 0))]

    kv = pl.BlockSpec((AB, KVH * HD), lambda j: (j + nc, 0))
    out = pl.BlockSpec((AB, KVH * HD), lambda j: (j, 0))
    return pl.pallas_call(
        body, name="attn_bwd_kv", grid=(nb,),
        in_specs=[kv, kv] + three(D) + three(D) + three(128) + three(128),
        out_specs=[out, out],
        out_shape=[jax.ShapeDtypeStruct((tl, KVH * HD), F32)] * 2,
        compiler_params=_cparams(("parallel",)),
    )(kr, vv, qr, qr, qr, do, do, do, lse, lse, lse, delta, delta, delta)


def _mm(a, b, ta=False, tb=False, out_dtype=F32, tm=512, tn=1024, tk=1024, name="mm", hi=False):
    a_parts = a.shape[0] if a.ndim == 3 else 0
    b_parts = b.shape[0] if b.ndim == 3 else 0
    assert not (a_parts and ta) and not (b_parts and tb)
    if a_parts:
        m, kd = a.shape[1], a_parts * a.shape[2]
    else:
        m, kd = (a.shape[1], a.shape[0]) if ta else a.shape
    n = b_parts * b.shape[2] if b_parts else (b.shape[0] if tb else b.shape[1])
    tm, tn, tk = min(tm, m), min(tn, n), min(tk, kd)
    assert m % tm == 0 and n % tn == 0 and kd % tk == 0, (name, m, n, kd, tm, tn, tk)
    nk = kd // tk
    dims = ((0,) if ta else (1,), (1,) if tb else (0,))

    def body(a_ref, b_ref, o_ref, *scr):
        part = _dot(a_ref[0] if a_parts else a_ref[...], b_ref[0] if b_parts else b_ref[...], dims, hi)
        if nk == 1:
            o_ref[...] = part.astype(out_dtype)
        else:
            acc = scr[0]
            kk = pl.program_id(2)

            @pl.when(kk == 0)
            def _():
                acc[...] = part

            @pl.when(kk > 0)
            def _():
                acc[...] += part

            @pl.when(kk == nk - 1)
            def _():
                o_ref[...] = acc[...].astype(out_dtype)

    a_spec = pl.BlockSpec((tk, tm), lambda i, j, k: (k, i)) if ta else pl.BlockSpec((tm, tk), lambda i, j, k: (i, k))
    b_spec = pl.BlockSpec((tn, tk), lambda i, j, k: (j, k)) if tb else pl.BlockSpec((tk, tn), lambda i, j, k: (k, j))
    if a_parts:
        per = a.shape[2] // tk
        assert per * tk == a.shape[2]
        a_spec = pl.BlockSpec((1, tm, tk), lambda i, j, k: (k // per, i, k % per))
    if b_parts:
        per_n = b.shape[2] // tn
        assert per_n * tn == b.shape[2]
        b_spec = pl.BlockSpec((1, tk, tn), lambda i, j, k: (j // per_n, k, j % per_n))
    return pl.pallas_call(
        body, name=name, grid=(m // tm, n // tn, nk),
        in_specs=[a_spec, b_spec],
        out_specs=pl.BlockSpec((tm, tn), lambda i, j, k: (i, j)),
        out_shape=jax.ShapeDtypeStruct((m, n), out_dtype),
        scratch_shapes=[] if nk == 1 else [pltpu.VMEM((tm, tn), F32)],
        compiler_params=_cparams(("parallel", "parallel", "arbitrary")),
    )(a, b)


HALO = 8


class _In:
    def __init__(self, arr, w=None, cb=0, roff=0, halo=None, ridx=None):
        self.arr, self.w, self.cb, self.roff, self.halo = arr, w or arr.shape[1], cb, roff, halo
        self.ridx = ridx or (lambda i, roff=roff: i + roff)


class _Full:
    def __init__(self, arr, w=None, cb=0):
        self.arr, self.w, self.cb = arr, w, cb


class _Out:
    def __init__(self, cols, dtype=F32, w=None, cb=0, acc=False, rows=1, roff=0, nrows=None, stack=0):
        self.cols, self.dtype, self.w, self.cb, self.acc, self.rows, self.roff, self.nrows, self.stack = (
            cols, dtype, w or cols, cb, acc, rows, roff, nrows, stack)


def _rowcall(name, fn, nrow_tiles, tile, ins, outs, ncol=1):
    arrays, specs, kinds = [], [], []
    for x in ins:
        if isinstance(x, _Full):
            arrays.append(x.arr)
            if x.w is None:
                specs.append(pl.BlockSpec(x.arr.shape, lambda j, i: (0, 0)))
            else:
                specs.append(pl.BlockSpec((x.arr.shape[0], x.w), lambda j, i, cb=x.cb: (0, cb + j)))
            kinds.append("full")
            continue
        w, cb, roff = x.w, x.cb, x.roff
        cur = pl.BlockSpec((tile, w), lambda j, i, cb=cb, ridx=x.ridx: (ridx(i), cb + j))
        if x.halo is None:
            arrays.append(x.arr)
            specs.append(cur)
            kinds.append("tile")
        else:
            r8 = tile // HALO
            last = x.arr.shape[0] // HALO - 1
            prev = pl.BlockSpec((HALO, w), lambda j, i, cb=cb, roff=roff, r8=r8: (jnp.maximum((i + roff) * r8 - 1, 0), cb + j))
            nxt = pl.BlockSpec((HALO, w), lambda j, i, cb=cb, roff=roff, r8=r8, last=last:
                               (jnp.minimum((i + roff + 1) * r8, last), cb + j))
            arrays += [x.arr, x.arr, x.arr]
            specs += [prev, cur, nxt]
            kinds.append(("halo", x.halo))
    out_specs, out_shapes = [], []
    for o in outs:
        if o.acc:
            out_specs.append(pl.BlockSpec((o.rows, o.w), lambda j, i, cb=o.cb: (0, cb + j)))
            out_shapes.append(jax.ShapeDtypeStruct((o.rows, o.cols), o.dtype))
        elif o.stack:
            out_specs.append(pl.BlockSpec((o.stack, tile, o.w), lambda j, i, cb=o.cb: (0, i, cb + j)))
            out_shapes.append(jax.ShapeDtypeStruct((o.stack, nrow_tiles * tile, o.cols), o.dtype))
        else:
            out_specs.append(pl.BlockSpec((tile, o.w), lambda j, i, cb=o.cb, roff=o.roff: (i + roff, cb + j)))
            out_shapes.append(jax.ShapeDtypeStruct(((o.nrows or nrow_tiles * tile), o.cols), o.dtype))
    n_in = len(arrays)

    def body(*refs):
        j = pl.program_id(0)
        i = pl.program_id(1)
        vals, r = [], 0
        for kind in kinds:
            if kind in ("full", "tile"):
                vals.append(refs[r][...])
                r += 1
            else:
                pok, nok = kind[1]
                p, c, n = refs[r][...], refs[r + 1][...], refs[r + 2][...]
                p = jnp.where(pok(i), p, jnp.zeros_like(p))
                n = jnp.where(nok(i), n, jnp.zeros_like(n))
                vals.append(jnp.concatenate([p, c, n], axis=0))
                r += 3
        res = fn(i, j, *vals)
        for o, ref, val in zip(outs, refs[n_in:], res):
            if o.acc:
                @pl.when(i == 0)
                def _(ref=ref, val=val, o=o):
                    ref[...] = val.astype(o.dtype)

                @pl.when(i > 0)
                def _(ref=ref, val=val, o=o):
                    ref[...] += val.astype(o.dtype)
            elif o.stack:
                for s_ in range(o.stack):
                    ref[s_] = val[s_].astype(o.dtype)
            else:
                ref[...] = val.astype(o.dtype)

    return pl.pallas_call(
        body, name=name, grid=(ncol, nrow_tiles), in_specs=specs, out_specs=out_specs, out_shape=out_shapes,
        compiler_params=_cparams(("parallel", "arbitrary")),
    )(*arrays)


def _shift(xe, s, tile):
    if s == 0:
        return xe[HALO:HALO + tile]
    return pltpu.roll(xe, (-s) % xe.shape[0], 0)[HALO:HALO + tile]


def _silu(x):
    return x * jax.nn.sigmoid(x)


def _dsilu(x):
    s = jax.nn.sigmoid(x)
    return s * (1.0 + x * (1.0 - s))


def _heads(x, fn):
    return jnp.concatenate([fn(h, x[:, h * HD:(h + 1) * HD]) for h in range(x.shape[1] // HD)], axis=1)


def _colsum(x):
    return jnp.sum(x, axis=0, keepdims=True)


def _rowmean(x):
    return jnp.mean(x, axis=1, keepdims=True)


def _rowsum(x):
    return jnp.sum(x, axis=1, keepdims=True)


TILE = 256
CT = CTX // TILE


def _all_halo(n_tiles):
    return (lambda i: i >= CT + 1, lambda i: jnp.logical_and(i >= CT, i < n_tiles - 1))


def _lat_halo(n_tiles):
    return (lambda i: i >= 1, lambda i: i < n_tiles - 1)


def _rms_mod(x, nm, shift, scale):
    r = lax.rsqrt(_rowmean(x * x) + EPS)
    return (x * r * nm) * (1.0 + scale) + shift


def _rms_mod_bwd(dh, x, nm, scale):
    r = lax.rsqrt(_rowmean(x * x) + EPS)
    xn = x * r
    dz = dh * (1.0 + scale)
    dxn = dz * nm
    dx = r * (dxn - xn * _rowmean(dxn * xn))
    return dx, _colsum(dz * xn), _colsum(dh), _colsum(dh * (xn * nm))


def _norm_mod(x, ctx, nm, mod_c, mod_x):
    n = (x.shape[0] + ctx.shape[0]) // TILE

    def fn(i, j, c_, x_, nm_, mc, mx):
        m = jnp.where(i < CT, mc, mx)
        return (_rms_mod(jnp.where(i < CT, c_, x_), nm_, m[0:1], m[1:2]),)

    ins = [_In(ctx, ridx=lambda i: jnp.minimum(i, CT - 1)), _In(x, ridx=lambda i: jnp.maximum(i - CT, 0)),
           _Full(nm), _Full(mod_c), _Full(mod_x)]
    return _rowcall("norm_mod", fn, n, TILE, ins, [_Out(D, BF16)])[0]


def _norm_mod_bwd(dh, xs, dres, nm, mod, roff, n):
    ins = [_In(dh, roff=roff), _In(xs), _Full(nm), _Full(mod)] + ([] if dres is None else [_In(dres)])

    def fn(i, j, dh_, x, nm_, m, *rest):
        dx, dn, dsh, dsc = _rms_mod_bwd(dh_, x, nm_, m[1:2])
        if rest:
            return (dx + rest[0], dn, dsh, dsc)
        return (dn, dsh, dsc)

    accs = [_Out(D, acc=True), _Out(D, acc=True), _Out(D, acc=True)]
    return _rowcall("norm_mod_bwd", fn, n, TILE, ins, ([] if dres is None else [_Out(D)]) + accs)


DN_Q_SCALE = HD ** -0.5


def _conv_taps(xe, w, width, rows=None):
    r = width // 2
    acc = None
    for t in range(width):
        s = t - r
        if rows is None:
            sh = xe if s == 0 else pltpu.roll(xe, (-s) % xe.shape[0], 0)
        else:
            sh = _shift(xe, s, rows)
        term = sh * w[t:t + 1]
        acc = term if acc is None else acc + term
    return acc


def _rolled(xe, width):
    r = width // 2
    return [xe if t == r else pltpu.roll(xe, (r - t) % xe.shape[0], 0) for t in range(width)]


def _conv_bwd(rolled, w, c_grad, width):
    r = width // 2
    cc = c_grad[HALO:HALO + TILE]
    dx, dws = None, []
    for t in range(width):
        term = _shift(c_grad, r - t, TILE) * w[t:t + 1]
        dx = term if dx is None else dx + term
        dws.append(_colsum(cc * rolled[t][HALO:HALO + TILE]))
    return dx, jnp.concatenate(dws + [jnp.zeros((8 - width, cc.shape[1]), F32)], axis=0)


def _silu_both(x):
    s = jax.nn.sigmoid(x)
    return x * s, s * (1.0 + x * (1.0 - s))


def _l2n(x, scale):
    rn = lax.rsqrt(_rowsum(x * x) + EPS)
    return x * (rn * scale)


def _l2n_bwd(dy, x, scale):
    rn = lax.rsqrt(_rowsum(x * x) + EPS)
    xu = x * rn
    return (scale * rn) * (dy - xu * _rowsum(dy * xu))


def _softplus(x):
    return jnp.maximum(x, 0.0) + jnp.log(1.0 + jnp.exp(-jnp.abs(x)))


def _lane_mask(lo, hi_):
    lane = lax.broadcasted_iota(jnp.int32, (1, 128), 1)
    return jnp.logical_and(lane >= lo, lane < hi_).astype(F32)


def _dn_prep(p, conv_w, gprm):
    n = p.shape[0] // TILE
    halo = _all_halo(n)

    def fn(i, j, qe, ke, ve, ba, w, gp):
        cq = _conv_taps(qe, w[:, 0:D], 5, TILE)
        ck = _conv_taps(ke, w[:, D:2 * D], 5, TILE)
        cv = _conv_taps(ve, w[:, 2 * D:3 * D], 5, TILE)
        q = _heads(_silu(cq), lambda h, x: _l2n(x, DN_Q_SCALE))
        k = _heads(_silu(ck), lambda h, x: _l2n(x, 1.0))
        v = _silu(cv)
        beta = jax.nn.sigmoid(ba)
        g = -jnp.exp(gp[0:1]) * _softplus(ba + gp[1:2])
        m0, m1 = _lane_mask(0, 8), _lane_mask(8, 16)
        gb_f = beta * m0 + pltpu.roll(g, 128 - 8, 1) * m1
        gb_b = pltpu.roll(beta, 128 - 8, 1) * m0 + pltpu.roll(g, 128 - 16, 1) * m1
        return q, k, v, gb_f, gb_b

    ins = [_In(p, D, 0, halo=halo), _In(p, D, 1, halo=halo), _In(p, D, 2, halo=halo), _In(p, 128, C_BA // 128),
           _Full(conv_w), _Full(gprm)]
    return _rowcall("dn_prep", fn, n, TILE, ins, [_Out(D), _Out(D), _Out(D), _Out(128), _Out(128)])


def _dn_prep_bwd(p, conv_w, gprm, dq2, dk2, dv2, dgb2):
    n = p.shape[0] // TILE
    halo = _all_halo(n)

    def branch(xe, w, dye, scale):
        rolled = _rolled(xe, 5)
        c = rolled[0] * w[0:1]
        for t in range(1, 5):
            c = c + rolled[t] * w[t:t + 1]
        sx, dsilu = _silu_both(c)
        if scale is None:
            dsx = dye
        else:
            dsx = jnp.concatenate([_l2n_bwd(dye[:, h * HD:(h + 1) * HD], sx[:, h * HD:(h + 1) * HD], scale)
                                   for h in range(NH)], axis=1)
        return _conv_bwd(rolled, w, dsx * dsilu, 5)

    def fn(i, j, qe, ke, ve, ba, w, gp, dq0, dq1, dk0, dk1, dv0, dv1, dg0, dg1):
        dxq, dwq = branch(qe, w[:, 0:D], dq0 + dq1, DN_Q_SCALE)
        dxk, dwk = branch(ke, w[:, D:2 * D], dk0 + dk1, 1.0)
        dxv, dwv = branch(ve, w[:, 2 * D:3 * D], dv0 + dv1, None)
        m0, m1 = _lane_mask(0, 8), _lane_mask(8, 16)
        dbeta = dg0 * m0 + pltpu.roll(dg1 * m0, 8, 1)
        dg = pltpu.roll(dg0 * m1, 8, 1) + pltpu.roll(dg1 * m1, 16, 1)
        beta = jax.nn.sigmoid(ba)
        ea = jnp.exp(gp[0:1])
        z = ba + gp[1:2]
        g = -ea * _softplus(z)
        mg = _lane_mask(16, 32)
        da = dg * (-ea) * jax.nn.sigmoid(z) * mg
        dba = dbeta * beta * (1.0 - beta) * _lane_mask(0, 16) + da
        dgp = jnp.concatenate([_colsum(dg * g * mg), _colsum(da)], axis=0)
        return (jnp.concatenate([dxq, dxk, dxv], axis=1), dba, jnp.concatenate([dwq, dwk, dwv], axis=1), dgp)

    ins = [_In(p, D, 0, halo=halo), _In(p, D, 1, halo=halo), _In(p, D, 2, halo=halo), _In(p, 128, C_BA // 128),
           _Full(conv_w), _Full(gprm),
           _In(dq2, halo=halo), _In(dq2, roff=n, halo=halo), _In(dk2, halo=halo), _In(dk2, roff=n, halo=halo),
           _In(dv2, halo=halo), _In(dv2, roff=n, halo=halo), _In(dgb2), _In(dgb2, roff=n)]
    return _rowcall("dn_prep_bwd", fn, n, TILE, ins,
                    [_Out(3 * D, BF16), _Out(128, BF16), _Out(3 * D, acc=True, rows=8), _Out(128, acc=True, rows=2)])


def _hnorm(x, w):
    return x * lax.rsqrt(_rowmean(x * x) + EPS) * w


def _hnorm_bwd(dy, x, w):
    r = lax.rsqrt(_rowmean(x * x) + EPS)
    xh = x * r
    dxh = dy * w
    return r * (dxh - xh * _rowmean(dxh * xh)), _colsum(dy * xh)


def _dn_gate(o2, p, dn_norm, n_all):
    n = n_all - CT

    def fn(i, j, of, ob, gt, w):
        o = of + ob
        return (_heads(o, lambda h, x: _hnorm(x, w)) * _silu(gt),)

    ins = [_In(o2, roff=CT), _In(o2, roff=n_all + CT), _In(p, D, C_GT // D, roff=CT), _Full(dn_norm)]
    return _rowcall("dn_gate", fn, n, TILE, ins, [_Out(D, BF16)])[0]


def _dn_gate_bwd(dy, o2, p, dn_norm, n_all):
    n = n_all - CT

    def fn(i, j, dy_, of, ob, gt, w):
        o = of + ob
        sg, dsg = _silu_both(gt)
        dos, dw = [], jnp.zeros((1, HD), F32)
        yn = []
        for h in range(NH):
            sl = slice(h * HD, (h + 1) * HD)
            dx, dwh = _hnorm_bwd(dy_[:, sl] * sg[:, sl], o[:, sl], w)
            dos.append(dx)
            dw = dw + dwh
            yn.append(_hnorm(o[:, sl], w))
        dgt = dy_ * jnp.concatenate(yn, axis=1) * dsg
        return jnp.concatenate(dos, axis=1), dgt, dw

    ins = [_In(dy), _In(o2, roff=CT), _In(o2, roff=n_all + CT), _In(p, D, C_GT // D, roff=CT), _Full(dn_norm)]
    return _rowcall("dn_gate_bwd", fn, n, TILE, ins, [_Out(D), _Out(D, BF16), _Out(HD, acc=True)])


def _rope_shuffle(x):
    lane = lax.broadcasted_iota(jnp.int32, (1, HD), 1)
    return jnp.where((lane % 64) < 32, pltpu.roll(x, HD - 32, 1), pltpu.roll(x, 32, 1))


def _rope(x, cos, sin):
    return x * cos + _rope_shuffle(x) * sin


def _rope_bwd(dy, cos, sin):
    return dy * cos + _rope_shuffle(dy * sin)


def _attn_prep(p, w, cos, sin, width, cb, roff, n, name):
    def fn(i, j, x, w_, c, s):
        return (_heads(x, lambda h, xh: _rope(_hnorm(xh, w_), c, s)),)

    ins = [_In(p, width, cb, roff=roff), _Full(w), _In(cos), _In(sin)]
    return _rowcall(name, fn, n, TILE, ins, [_Out(width)])[0]


def _attn_prep_bwd(dy, p, w, cos, sin, width, cb, roff, n, name):
    def fn(i, j, dy_, x, w_, c, s):
        dxs, dw = [], jnp.zeros((1, HD), F32)
        for h in range(width // HD):
            sl = slice(h * HD, (h + 1) * HD)
            dx, dwh = _hnorm_bwd(_rope_bwd(dy_[:, sl], c, s), x[:, sl], w_)
            dxs.append(dx)
            dw = dw + dwh
        return jnp.concatenate(dxs, axis=1), dw

    ins = [_In(dy), _In(p, width, cb, roff=roff), _Full(w), _In(cos), _In(sin)]
    return _rowcall(name, fn, n, TILE, ins, [_Out(width, BF16), _Out(HD, acc=True)])


def _merge(z_dn, z_at, p, n):
    def fn(i, j, zd, za, gd, ga):
        return (jax.nn.sigmoid(gd) * zd + jax.nn.sigmoid(ga) * za,)

    ins = [_In(z_dn), _In(z_at), _In(p, D, C_MG // D, roff=CT), _In(p, D, C_MG // D + 1, roff=CT)]
    return _rowcall("merge", fn, n, TILE, ins, [_Out(D, BF16)])[0]


def _merge_bwd(dm, z_dn, z_at, p, n):
    def fn(i, j, dm_, zd, za, gd, ga):
        sd, sa = jax.nn.sigmoid(gd), jax.nn.sigmoid(ga)
        dg = jnp.concatenate([dm_ * zd * sd * (1.0 - sd), dm_ * za * sa * (1.0 - sa)], axis=1)
        return dm_ * sd, dm_ * sa, dg

    ins = [_In(dm), _In(z_dn), _In(z_at), _In(p, D, C_MG // D, roff=CT), _In(p, D, C_MG // D + 1, roff=CT)]
    return _rowcall("merge_bwd", fn, n, TILE, ins, [_Out(D, BF16), _Out(D, BF16), _Out(2 * D, BF16)])


def _resid_norm(xa, mo, g_a, nf, mod_f, n):
    def fn(i, j, x, mo_, ga, nf_, m):
        x1 = x + ga * mo_
        return x1, _rms_mod(x1, nf_, m[0:1], m[1:2])

    ins = [_In(xa), _In(mo), _Full(g_a), _Full(nf), _Full(mod_f)]
    return _rowcall("resid_norm", fn, n, TILE, ins, [_Out(D), _Out(D, BF16)])


def _resid_norm_bwd(dy, dh2, x1, mo, g_a, nf, mod_f, n):
    def fn(i, j, dy_, dh_, x1_, mo_, ga, nf_, m):
        dx, dn, dsh, dsc = _rms_mod_bwd(dh_, x1_, nf_, m[1:2])
        dx1 = dy_ + dx
        return dx1, ga * dx1, dn, dsh, dsc, _colsum(dx1 * mo_)

    ins = [_In(dy), _In(dh2), _In(x1), _In(mo), _Full(g_a), _Full(nf), _Full(mod_f)]
    accs = [_Out(D, acc=True) for _ in range(4)]
    return _rowcall("resid_norm_bwd", fn, n, TILE, ins, [_Out(D), _Out(D, BF16)] + accs)


def _loss_head(x1, f, tgt, g_f, n):
    def fn(i, j, x1_, f_, t, gf):
        e = x1_ + gf * f_ - t
        dy = e * (1.0 / D)
        loss = _colsum(_rowsum(e * e)) * (0.5 / D)
        return dy, gf * dy, _colsum(dy * f_), jnp.broadcast_to(loss, (1, 128))

    ins = [_In(x1), _In(f), _In(tgt), _Full(g_f)]
    return _rowcall("loss_head", fn, n, TILE, ins, [_Out(D), _Out(D, BF16), _Out(D, acc=True), _Out(128, acc=True)])


FW = DFF // 2


def _ffn_act(u, conv_w, conv_b, n):
    halo = _lat_halo(n)

    def fn(i, j, ge, ve, wg, wv, bg, bv):
        cg = _conv_taps(ge, wg, 3, TILE) + bg
        cv = _conv_taps(ve, wv, 3, TILE) + bv
        return (_silu(cg) * cv,)

    ins = [_In(u, FW, 0, halo=halo), _In(u, FW, 2, halo=halo), _Full(conv_w, FW, 0), _Full(conv_w, FW, 2),
           _Full(conv_b, FW, 0), _Full(conv_b, FW, 2)]
    return _rowcall("ffn_act", fn, n, TILE, ins, [_Out(DFF, BF16, FW)], ncol=2)[0]


def _ffn_act_bwd(u, da, conv_w, conv_b, n):
    halo = _lat_halo(n)

    def fn(i, j, ge, ve, dae, wg, wv, bg, bv):
        rg, rv = _rolled(ge, 3), _rolled(ve, 3)
        cg = rg[0] * wg[0:1] + rg[1] * wg[1:2] + rg[2] * wg[2:3] + bg
        cv = rv[0] * wv[0:1] + rv[1] * wv[1:2] + rv[2] * wv[2:3] + bv
        sg, dsg = _silu_both(cg)
        dcg = dae * cv * dsg
        dcv = dae * sg
        dxg, dwg = _conv_bwd(rg, wg, dcg, 3)
        dxv, dwv = _conv_bwd(rv, wv, dcv, 3)
        return (dxg, dxv), dwg, dwv, _colsum(dcg[HALO:HALO + TILE]), _colsum(dcv[HALO:HALO + TILE])

    ins = [_In(u, FW, 0, halo=halo), _In(u, FW, 2, halo=halo), _In(da, FW, 0, halo=halo),
           _Full(conv_w, FW, 0), _Full(conv_w, FW, 2), _Full(conv_b, FW, 0), _Full(conv_b, FW, 2)]
    outs = [_Out(DFF, BF16, FW, stack=2), _Out(DFF, w=FW, acc=True, rows=8), _Out(DFF, w=FW, acc=True, rows=8),
            _Out(DFF, w=FW, acc=True), _Out(DFF, w=FW, acc=True)]
    return _rowcall("ffn_act_bwd", fn, n, TILE, ins, outs, ncol=2)


def _rope_tables(tl):
    rows = tl // GRID_W
    inv = np.float32(ROPE_BASE) ** (-np.arange(32, dtype=np.float32) / np.float32(32))
    ar = np.arange(rows, dtype=np.float32)[:, None] * inv
    ac = np.arange(GRID_W, dtype=np.float32)[:, None] * inv

    def table(r, c):
        full = (rows, GRID_W, HD // 2)
        return jnp.concatenate([jnp.broadcast_to(jnp.asarray(r)[:, None, :], full),
                                jnp.broadcast_to(jnp.asarray(c)[None, :, :], full)], axis=2).reshape(tl, HD)

    two = lambda a, b: np.concatenate([a, b], axis=1).astype(np.float32)
    cos = table(two(np.cos(ar), np.cos(ar)), two(np.cos(ac), np.cos(ac)))
    sin = table(two(-np.sin(ar), np.sin(ar)), two(-np.sin(ac), np.sin(ac)))
    return cos, sin


def _pad_w_in(w_in):
    z = lambda n: jnp.zeros((D, n), w_in.dtype)
    return jnp.concatenate([w_in[:, 0:4096], w_in[:, 4128:5152], w_in[:, 5664:7712], w_in[:, 5152:5664],
                            w_in[:, 4096:4128], z(96 + PW - C_PAD)], axis=1)


def _unpad_w_in(g, axis=1):
    cut = lambda a, b: lax.slice_in_dim(g, a, b, axis=axis)
    return jnp.concatenate([cut(0, 4096), cut(C_BA, C_BA + 32), cut(C_QAT, C_QAT + D), cut(C_KAT, C_KAT + 512),
                            cut(C_MG, C_MG + 2 * D)], axis=axis)


def _local_step(x, ctx, tgt, mod_x, mod_c, w, hi=False):
    tl = x.shape[0]
    t_all = tl + CTX
    n_all, n = t_all // TILE, tl // TILE
    tm_all = 1280 if t_all % 1280 == 0 else TILE
    tm_lat = 1024
    mm = functools.partial(_mm, hi=hi)
    sp = lambda m: [m[:, k * D:(k + 1) * D] for k in range(6)]
    sh_a, sc_a, g_a, sh_f, sc_f, g_f = sp(mod_x)
    sh_ac, sc_ac = sp(mod_c)[:2]
    mod_ax = jnp.concatenate([sh_a, sc_a], axis=0)
    mod_ac = jnp.concatenate([sh_ac, sc_ac], axis=0)
    mod_f = jnp.concatenate([sh_f, sc_f], axis=0)
    nm, nf = w["norm_mix"], w["norm_ffn"]
    cos, sin = _rope_tables(tl)
    cos_all = jnp.concatenate([jnp.ones((CTX, HD), F32), cos], axis=0)
    sin_all = jnp.concatenate([jnp.zeros((CTX, HD), F32), sin], axis=0)
    conv_dn = jnp.concatenate([w["dn_conv"], jnp.zeros((3, 3 * D), F32)], axis=0)
    gprm = jnp.concatenate([jnp.zeros((2, 16), F32),
                            jnp.concatenate([w["dn_a_log"].reshape(1, 16), w["dn_dt_bias"].reshape(1, 16)], axis=0),
                            jnp.zeros((2, 96), F32)], axis=1)
    conv_ff = jnp.concatenate([w["ffn_conv"], jnp.zeros((5, 2 * DFF), F32)], axis=0)
    sink = jnp.concatenate([w["attn_sink"].reshape(1, NH), jnp.zeros((1, 128 - NH), F32)], axis=1)
    nct = CTX // CH

    h = _norm_mod(x, ctx, nm, mod_ac, mod_ax)
    p = mm(h, w["w_in_p"], tm=tm_all, tn=1024, name="mm_in")
    q, k, v, gb_f, gb_b = _dn_prep(p, conv_dn, gprm)
    gb = jnp.stack([gb_f, gb_b])
    dn_u, dn_w, dn_qg, dn_kd, dn_pm, dn_t = _dn_intra_fwd(q, k, v, gb, nct, hi)
    o2, s_hist, dn_vn = _dn_seq_fwd(dn_u, dn_w, dn_qg, dn_kd, dn_pm, gb, nct, hi)
    o2 = o2.reshape(2 * t_all, D)
    y_dn = _dn_gate(o2, p, w["dn_norm"], n_all)
    qr = _attn_prep(p, w["q_norm"], cos, sin, D, C_QAT // D, CT, n, "attn_prep_q")
    kr = _attn_prep(p, w["k_norm"], cos_all, sin_all, KVH * HD, C_KAT // (KVH * HD), 0, n_all, "attn_prep_k")
    vv = p[:, C_VAT:C_VAT + KVH * HD]
    o_at, lse = _attn_fwd(qr, kr, vv, sink, hi)
    z_dn = mm(y_dn, w["w_branch_dn"], tm=tm_lat, name="mm_bdn")
    z_at = mm(o_at, w["w_branch_attn"], tm=tm_lat, name="mm_bat")
    merged = _merge(z_dn, z_at, p, n)
    mo = mm(merged, w["w_out"], tm=tm_lat, name="mm_out")
    x1, h2 = _resid_norm(x, mo, g_a, nf, mod_f, n)
    u = mm(h2, w["ffn_up"], tm=2 * tm_lat, tn=1408, name="mm_up")
    a = _ffn_act(u, conv_ff, w["ffn_conv_b"], n)
    f = mm(a, w["ffn_down"], tm=tm_lat, tk=DFF, name="mm_down")
    dy, df, dg_f, loss = _loss_head(x1, f, tgt, g_f, n)

    g = {}
    da = mm(df, w["ffn_down"], tb=True, tm=tm_lat, tn=1408, name="mm_down_dx")
    g["ffn_down"] = mm(a, df, ta=True, tm=1408, tn=1024, tk=tm_lat, name="mm_down_dw")
    du, dcw_g, dcw_v, dcb_g, dcb_v = _ffn_act_bwd(u, da, conv_ff, w["ffn_conv_b"], n)
    g["ffn_conv"] = jnp.concatenate([dcw_g, dcw_v], axis=1)[0:3]
    g["ffn_conv_b"] = jnp.concatenate([dcb_g, dcb_v], axis=1)
    dh2 = mm(du, w["ffn_up"], tb=True, tm=tm_lat, tk=1408, name="mm_up_dx")
    g["ffn_up"] = mm(h2, du, ta=True, tm=1024, tn=1408, tk=tm_lat, name="mm_up_dw")
    dx1, dmo, g["norm_ffn"], dsh_f, dsc_f, dg_a = _resid_norm_bwd(dy, dh2, x1, mo, g_a, nf, mod_f, n)
    dmerged = mm(dmo, w["w_out"], tb=True, tm=tm_lat, name="mm_out_dx")
    g["w_out"] = mm(merged, dmo, ta=True, tm=1024, tk=tm_lat, name="mm_out_dw")
    dz_dn, dz_at, dmg = _merge_bwd(dmerged, z_dn, z_at, p, n)
    dy_dn = mm(dz_dn, w["w_branch_dn"], tb=True, tm=tm_lat, name="mm_bdn_dx")
    g["w_branch_dn"] = mm(y_dn, dz_dn, ta=True, tm=1024, tk=tm_lat, name="mm_bdn_dw")
    do_at = mm(dz_at, w["w_branch_attn"], tb=True, tm=tm_lat, name="mm_bat_dx")
    g["w_branch_attn"] = mm(o_at, dz_at, ta=True, tm=1024, tk=tm_lat, name="mm_bat_dw")

    do_dn, dgt, g["dn_norm"] = _dn_gate_bwd(dy_dn, o2, p, w["dn_norm"], n_all)
    do_all = do_dn
    dn_dvn, dn_dw, dn_dqg, dn_dkd, dn_del = _dn_seq_bwd(dn_w, dn_qg, dn_kd, dn_pm, dn_vn, s_hist, gb, do_all, nct, hi)
    dq2, dk2, dv2, dgb2 = _dn_intra_bwd(q, k, v, gb, dn_u, dn_w, dn_t, dn_vn, dn_dvn, dn_dw, dn_dqg, dn_dkd, dn_del,
                                        do_all, nct, hi)
    dqkv, dba, dconv, dgprm = _dn_prep_bwd(p, conv_dn, gprm, dq2.reshape(2 * t_all, D), dk2.reshape(2 * t_all, D),
                                           dv2.reshape(2 * t_all, D), dgb2.reshape(2 * t_all, 128))
    g["dn_conv"] = dconv[0:5]
    g["dn_a_log"] = dgprm[0, 16:32].reshape(2, NH)
    g["dn_dt_bias"] = dgprm[1, 16:32].reshape(2, NH)

    delta = _attn_delta(o_at, do_at)
    dqr, dkx, dvx, dsink = _attn_bwd_q(qr, kr, vv, sink, do_at, lse, delta, hi)
    dk_lat, dv_lat = _attn_bwd_kv(qr, kr, vv, do_at, lse, delta, hi)
    g["attn_sink"] = dsink[:, 0:NH]
    dq_at, g["q_norm"] = _attn_prep_bwd(dqr, p, w["q_norm"], cos, sin, D, C_QAT // D, CT, n, "attn_prep_q_bwd")
    dkr = jnp.concatenate([dkx, dk_lat], axis=0)
    dk_at, g["k_norm"] = _attn_prep_bwd(dkr, p, w["k_norm"], cos_all, sin_all, KVH * HD, C_KAT // (KVH * HD), 0, n_all,
                                        "attn_prep_k_bwd")
    dv_at = jnp.concatenate([dvx, dv_lat], axis=0).astype(BF16)

    zc = lambda width: jnp.zeros((CTX, width), BF16)
    dp = jnp.concatenate([
        dqkv,
        jnp.concatenate([zc(D), dgt], axis=0),
        jnp.concatenate([zc(D), dq_at], axis=0),
        jnp.concatenate([zc(2 * D), dmg], axis=0),
        dk_at, dv_at, dba, jnp.zeros((t_all, PW - C_PAD), BF16)], axis=1)
    dh = mm(dp, w["w_in_p"], tb=True, tm=tm_all, tn=1024, tk=2048, name="mm_in_dx")
    g["w_in_p"] = mm(h, dp, ta=True, tm=1024, tn=2048, tk=tm_all, name="mm_in_dw")
    dnm_c, dsh_ac, dsc_ac = _norm_mod_bwd(dh, ctx, None, nm, mod_ac, 0, CT)
    grad_x, dnm_x, dsh_a, dsc_a = _norm_mod_bwd(dh, x, dx1, nm, mod_ax, CT, n)
    g["norm_mix"] = dnm_c + dnm_x
    dmod_x = jnp.concatenate([dsh_a, dsc_a, dg_a, dsh_f, dsc_f, dg_f], axis=1)
    dmod_c = jnp.concatenate([dsh_ac, dsc_ac, jnp.zeros((1, 4 * D), F32)], axis=1)
    return loss, grad_x, g, dmod_x, dmod_c


def _sum_slots(buf, n_slots, rows, tile, name, stride=1):
    nt = rows // tile

    def fn(i, j, *vals):
        acc = vals[0]
        for v in vals[1:]:
            acc = acc + v
        return (acc,)

    ins = [_In(buf, roff=k * stride * nt) for k in range(n_slots)]
    return _rowcall(name, fn, nt, tile, ins, [_Out(buf.shape[1])])[0]


ADAM_LR, ADAM_B1, ADAM_B2, ADAM_EPS, ADAM_WD, ADAM_STEP = 0.001, 0.9, 0.999, 1e-08, 0.01, 10


def _row_tile(rows, cols):
    for t in (512, 256, 128, 64, 32, 16, 8):
        if rows % t == 0 and t * cols * 4 * 14 <= 40 * 1024 * 1024:
            return t
    return rows


def _adamw(w, g, m, v, name):
    shape = w.shape
    cols = shape[-1]
    rows = max(1, math.prod(shape[:-1]))
    tile = _row_tile(rows, cols)
    c1 = 1.0 / (1.0 - ADAM_B1 ** ADAM_STEP)
    c2 = 1.0 / (1.0 - ADAM_B2 ** ADAM_STEP)

    def fn(i, j, w_, g_, m_, v_):
        mn = ADAM_B1 * m_ + (1.0 - ADAM_B1) * g_
        vn = ADAM_B2 * v_ + (1.0 - ADAM_B2) * (g_ * g_)
        delta = -ADAM_LR * ((mn * c1) / (jnp.sqrt(vn * c2) + ADAM_EPS) + ADAM_WD * w_)
        return delta, mn, vn

    r2 = lambda a: a.reshape(rows, cols)
    outs = _rowcall(name, fn, rows // tile, tile, [_In(r2(w)), _In(r2(g)), _In(r2(m)), _In(r2(v))],
                    [_Out(cols), _Out(cols), _Out(cols)])
    return [o.reshape(shape) for o in outs]


MESH = pl.DeviceIdType.MESH
ANY = pl.BlockSpec(memory_space=pl.ANY)


def _pos():
    return lax.axis_index("x"), lax.axis_index("y"), lax.axis_index("c")


def _all_gather_many(blks, name):
    na = len(blks)

    def body(*refs):
        x_refs, out_refs = refs[:na], refs[na:2 * na]
        send_sems, recv_sems, local_sems = refs[2 * na:]
        x, y, c = _pos()
        me, sibling = (x, y, c), (x, y, 1 - c)
        chips = [(1 - x, y), (x, 1 - y), (1 - x, 1 - y)]

        def rows(a, px, py, pc):
            m_per = blks[a].shape[0]
            return out_refs[a].at[pl.ds(pl.multiple_of((4 * px + 2 * py + pc) * m_per, 8), m_per), :]

        def copy(a, k, block, to, src=None):
            return pltpu.make_async_remote_copy(
                src_ref=rows(a, *block) if src is None else src, dst_ref=rows(a, *block),
                send_sem=send_sems.at[7 * a + k], recv_sem=recv_sems.at[7 * a + k], device_id=to, device_id_type=MESH)

        every = range(na)
        mine = [pltpu.make_async_copy(x_refs[a], rows(a, *me), local_sems.at[a]) for a in every]
        for cp in mine:
            cp.start()
        first = [copy(a, 0, me, sibling, src=x_refs[a]) for a in every]
        first += [copy(a, 1 + j, me, (*chip, c), src=x_refs[a]) for j, chip in enumerate(chips) for a in every]
        for cp in first:
            cp.start()
        passed = []
        for j, chip in enumerate(chips):
            for a in every:
                copy(a, 1 + j, (*chip, c), me).wait_recv()
                passed.append(copy(a, 4 + j, (*chip, c), sibling))
                passed[-1].start()
        for a in every:
            copy(a, 0, sibling, me).wait_recv()
        for j, chip in enumerate(chips):
            for a in every:
                copy(a, 4 + j, (*chip, 1 - c), me).wait_recv()
        for cp in first + passed:
            cp.wait_send()
        for cp in mine:
            cp.wait()

    return pl.pallas_call(
        body, name=name,
        out_shape=[jax.ShapeDtypeStruct((N_DEV * b.shape[0], b.shape[1]), b.dtype) for b in blks],
        in_specs=[ANY] * na, out_specs=[ANY] * na,
        scratch_shapes=[pltpu.SemaphoreType.DMA((7 * na,)), pltpu.SemaphoreType.DMA((7 * na,)), pltpu.SemaphoreType.DMA((na,))],
        compiler_params=pltpu.CompilerParams(has_side_effects=True),
    )(*blks)


def _all_gather(blk, name):
    return _all_gather_many([blk], name)[0]


def _flip(v, bit):
    return 1 - v if bit else v


D2D_STREAMS = 8
ICI_STREAMS = 2


def _sibling_exchange(src, seg_rows, n_seg, paired, name):
    n = src.shape[1]
    per_seg = D2D_STREAMS // n_seg
    per = seg_rows // per_seg
    assert per_seg * n_seg == D2D_STREAMS and per * per_seg == seg_rows and per % 16 == 0

    def body(x_ref, out_ref, send_sems, recv_sems):
        x, y, c = _pos()
        copies = []
        for s in range(n_seg):
            base = (2 * s + (1 - c)) * seg_rows if paired else s * seg_rows
            for j in range(per_seg):
                i = s * per_seg + j
                cp = pltpu.make_async_remote_copy(
                    src_ref=x_ref.at[pl.ds(pl.multiple_of(base + j * per, 16), per), :],
                    dst_ref=out_ref.at[pl.ds(s * seg_rows + j * per, per), :],
                    send_sem=send_sems.at[i], recv_sem=recv_sems.at[i], device_id=(x, y, 1 - c), device_id_type=MESH)
                cp.start()
                copies.append(cp)
        for cp in copies:
            cp.wait_recv()
        for cp in copies:
            cp.wait_send()

    return pl.pallas_call(
        body, name=name, out_shape=jax.ShapeDtypeStruct((n_seg * seg_rows, n), src.dtype),
        in_specs=[ANY], out_specs=ANY,
        scratch_shapes=[pltpu.SemaphoreType.DMA((D2D_STREAMS,)), pltpu.SemaphoreType.DMA((D2D_STREAMS,))],
        compiler_params=pltpu.CompilerParams(has_side_effects=True),
    )(src)


def _transpose_cast(x, dtype, name):
    r, c = x.shape
    tc = 512

    def body(x_ref, o_ref):
        o_ref[...] = x_ref[...].T.astype(o_ref.dtype)

    return pl.pallas_call(
        body, name=name, grid=(c // tc,),
        in_specs=[pl.BlockSpec((r, tc), lambda j: (0, j))], out_specs=pl.BlockSpec((tc, r), lambda j: (j, 0)),
        out_shape=jax.ShapeDtypeStruct((c, r), dtype), compiler_params=_cparams(("parallel",)),
    )(x)


def _chip_exchange(buf, rows, name):
    n = buf.shape[1]
    per = rows // ICI_STREAMS
    assert per * ICI_STREAMS == rows and per % 16 == 0

    def body(x_ref, out_ref, send_sems, recv_sems):
        x, y, c = _pos()
        copies = []
        for k in range(1, 4):
            px, py = _flip(x, k & 2), _flip(y, k & 1)
            for j in range(ICI_STREAMS):
                i = (k - 1) * ICI_STREAMS + j
                cp = pltpu.make_async_remote_copy(
                    src_ref=x_ref.at[pl.ds(pl.multiple_of((2 * px + py) * rows + j * per, 16), per), :],
                    dst_ref=out_ref.at[pl.ds((k - 1) * rows + j * per, per), :],
                    send_sem=send_sems.at[i], recv_sem=recv_sems.at[i], device_id=(px, py, c), device_id_type=MESH)
                cp.start()
                copies.append(cp)
        for cp in copies:
            cp.wait_recv()
        for cp in copies:
            cp.wait_send()

    return pl.pallas_call(
        body, name=name, out_shape=jax.ShapeDtypeStruct((3 * rows, n), buf.dtype),
        in_specs=[ANY], out_specs=ANY,
        scratch_shapes=[pltpu.SemaphoreType.DMA((3 * ICI_STREAMS,)), pltpu.SemaphoreType.DMA((3 * ICI_STREAMS,))],
        compiler_params=pltpu.CompilerParams(has_side_effects=True),
    )(buf)


def _add_rows(parts, rows, dtype, name):
    tile = 1024
    ins = [_In(a, roff=r0 // tile) for a, r0 in parts]

    def fn(i, j, *vals):
        acc = vals[0].astype(F32)
        for v_ in vals[1:]:
            acc = acc + v_.astype(F32)
        return (acc,)

    return _rowcall(name, fn, rows // tile, tile, ins, [_Out(parts[0][0].shape[1], dtype)])[0]


BIG = ("w_in", "w_branch_dn", "w_branch_attn", "w_out", "ffn_up", "ffn_down")
BIG_SHARD = {"w_in": (1024, 1928, True), "w_branch_dn": (256, 1024, False), "w_branch_attn": (256, 1024, False),
             "w_out": (256, 1024, False), "ffn_up": (1024, 1408, True), "ffn_down": (704, 1024, False)}
BIG_ROWS = {k: r * c // 2 // 128 for k, (r, c, _) in BIG_SHARD.items()}
PIECE = 19456
assert sum(BIG_ROWS.values()) <= PIECE


def _gather_weights(shards, ci):
    halves = []
    for k in BIG:
        r, c, _ = BIG_SHARD[k]
        halves.append(lax.dynamic_slice_in_dim(shards[k], ci * (r // 2), r // 2, axis=0).astype(BF16))
    out = {}
    for k, ag in zip(BIG, _all_gather_many(halves, "ag_weights")):
        r, c, by_col = BIG_SHARD[k]
        blk = ag.reshape(4, r, c)
        out[k] = jnp.transpose(blk, (1, 0, 2)).reshape(r, 4 * c) if by_col else blk.reshape(4 * r, c)
    return out


def _pack_pieces(full):
    parts = [full["w_in_t"].reshape(N_DEV, BIG_ROWS["w_in"], 128).astype(BF16)]
    for k in BIG[1:]:
        r, c, by_col = BIG_SHARD[k]
        a = full[k]
        if by_col:
            a = jnp.transpose(a.reshape(r, 4, c), (1, 0, 2))
        parts.append(a.reshape(N_DEV, BIG_ROWS[k], 128).astype(BF16))
    parts.append(jnp.zeros((N_DEV, PIECE - sum(BIG_ROWS.values()), 128), BF16))
    return jnp.concatenate(parts, axis=1).reshape(N_DEV * PIECE, 128)


def _reduce_scatter(pieces, ci, shard):
    half = N_DEV // 2 * PIECE
    theirs = _sibling_exchange(pieces, PIECE, N_DEV // 2, True, "rs_d2d")
    own = lax.dynamic_index_in_dim(pieces.reshape(N_DEV // 2, 2, PIECE, 128), ci, axis=1, keepdims=False).reshape(half, 128)
    part = _add_rows([(own, 0), (theirs, 0)], half, BF16, "rs_sum_chip")
    recv = _chip_exchange(part, PIECE, "rs_ici")
    own2 = lax.dynamic_slice_in_dim(part, shard * PIECE, PIECE, axis=0)
    mine = _add_rows([(own2, 0), (recv, 0), (recv, PIECE), (recv, 2 * PIECE)], PIECE, F32, "rs_sum_all")
    other = _sibling_exchange(mine, PIECE, 1, False, "rs_pair")
    return jnp.where(ci == 0, jnp.stack([mine, other]), jnp.stack([other, mine]))


def _unpack_shard(two):
    out, off = {}, 0
    for k in BIG:
        r, c, _ = BIG_SHARD[k]
        blk = two[:, off:off + BIG_ROWS[k]]
        out[k] = blk.reshape(c, r).T if k == "w_in" else blk.reshape(r, c)
        off += BIG_ROWS[k]
    return out


SMALL = (("dn_conv", 120), ("ffn_conv", 132), ("ffn_conv_b", 44), ("norm_mix", 8), ("norm_ffn", 8), ("dn_a_log", 1),
         ("dn_dt_bias", 1), ("dn_norm", 1), ("q_norm", 1), ("k_norm", 1), ("attn_sink", 1), ("dmod_c", 48), ("dmod_x", 48))
SMALL_ROWS = 416


def _rows128(a, rows):
    flat = a.reshape(-1)
    return jnp.concatenate([flat, jnp.zeros((rows * 128 - flat.shape[0],), F32)]).reshape(rows, 128)


def _pack_small(g):
    parts = [_rows128(g[k], r) for k, r in SMALL]
    parts.append(jnp.zeros((SMALL_ROWS - sum(r for _, r in SMALL), 128), F32))
    return jnp.concatenate(parts, axis=0)


def _unpack_small(buf, shapes):
    out, off = {}, 0
    for k, r in SMALL:
        n = math.prod(shapes[k])
        out[k] = buf[off:off + r].reshape(-1)[:n].reshape(shapes[k])
        off += r
    return out


WEIGHTS = ("c_ctx", "w_ada", "b_ada", "norm_mix", "norm_ffn", "w_in", "dn_conv", "dn_a_log", "dn_dt_bias", "dn_norm",
           "q_norm", "k_norm", "attn_sink", "w_branch_dn", "w_branch_attn", "w_out", "ffn_up", "ffn_conv", "ffn_conv_b",
           "ffn_down")


def kernel(x, c, ctx, c_ctx, w_ada, b_ada, norm_mix, norm_ffn, w_in, dn_conv, dn_a_log, dn_dt_bias, dn_norm, q_norm, k_norm, attn_sink, w_branch_dn, w_branch_attn, w_out, ffn_up, ffn_conv, ffn_conv_b, ffn_down, loss_target, m_c_ctx, m_w_ada, m_b_ada, m_norm_mix, m_norm_ffn, m_w_in, m_dn_conv, m_dn_a_log, m_dn_dt_bias, m_dn_norm, m_q_norm, m_k_norm, m_attn_sink, m_w_branch_dn, m_w_branch_attn, m_w_out, m_ffn_up, m_ffn_conv, m_ffn_conv_b, m_ffn_down, v_c_ctx, v_w_ada, v_b_ada, v_norm_mix, v_norm_ffn, v_w_in, v_dn_conv, v_dn_a_log, v_dn_dt_bias, v_dn_norm, v_q_norm, v_k_norm, v_attn_sink, v_w_branch_dn, v_w_branch_attn, v_w_out, v_ffn_up, v_ffn_conv, v_ffn_conv_b, v_ffn_down):
    args = dict(locals())
    xi, yi, ci = _pos()
    dev = 4 * xi + 2 * yi + ci
    shard = 2 * xi + yi
    chips = lambda a: a[0::2]

    blk = jnp.concatenate([_rows128(c, 8), _rows128(dn_conv, 30), _rows128(ffn_conv, 33), jnp.zeros((1, 128), F32)], axis=0)
    ag = _all_gather(blk, "ag_small_in").reshape(N_DEV, 72, 128)
    c_all = ag[:, 0:8].reshape(N_DEV, D)
    dn_conv_full = jnp.transpose(chips(ag)[:, 8:38].reshape(4, 5, 768), (1, 0, 2)).reshape(5, 3 * D)
    ffn_conv_full = jnp.transpose(chips(ag)[:, 38:71].reshape(4, 3, 1408), (1, 0, 2)).reshape(3, 2 * DFF)

    c16 = jnp.concatenate([c_all, c_ctx[None], jnp.zeros((7, D), F32)], axis=0)
    a16 = _rowcall("ada_silu", lambda i, j, v: (_silu(v),), 1, 16, [_In(c16)], [_Out(D)])[0]
    m_sh = _mm(a16, w_ada[0], tm=16, tn=512, tk=D, name="ada_fwd", hi=True)
    mod16 = chips(_all_gather(m_sh, "ag_mod").reshape(N_DEV, 16, 1536))
    mod16 = jnp.transpose(mod16, (1, 0, 2)).reshape(16, 6 * D) + b_ada
    mod_x = lax.dynamic_slice_in_dim(mod16, dev, 1, axis=0)
    mod_c = mod16[8:9]

    shards = {k: args[k][0] for k in BIG}
    wfull = _gather_weights(shards, ci)
    w = dict(wfull)
    w["w_in_p"] = _pad_w_in(wfull["w_in"])
    w.update(norm_mix=norm_mix, norm_ffn=norm_ffn, dn_conv=dn_conv_full, dn_a_log=dn_a_log[0], dn_dt_bias=dn_dt_bias[0],
             dn_norm=dn_norm, q_norm=q_norm, k_norm=k_norm, attn_sink=attn_sink, ffn_conv=ffn_conv_full, ffn_conv_b=ffn_conv_b)

    loss_part, grad_x, g, dmod_x, dmod_c = _local_step(x[0], ctx[0], loss_target[0], mod_x, mod_c, w)
    loss = lax.psum(loss_part[0, 0], ("x", "y", "c"))

    g["w_in_t"] = _unpad_w_in(_transpose_cast(g["w_in_p"], BF16, "w_in_grad_t"), axis=0)
    gshard = _unpack_shard(_reduce_scatter(_pack_pieces(g), ci, shard))

    g["dmod_c"], g["dmod_x"] = dmod_c, dmod_x
    ag_s = _all_gather(_pack_small(g), "ag_small_grads")
    shapes = {k: g[k].shape for k, _ in SMALL}
    gs = _unpack_small(_sum_slots(ag_s, N_DEV, SMALL_ROWS, SMALL_ROWS, "small_sum"), shapes)
    dx_all = ag_s.reshape(N_DEV, SMALL_ROWS, 128)[:, SMALL_ROWS - 50:SMALL_ROWS - 2].reshape(N_DEV, 6 * D)

    d16 = jnp.concatenate([dx_all, gs["dmod_c"], jnp.zeros((7, 6 * D), F32)], axis=0)
    d16_sh = lax.dynamic_slice_in_dim(d16, shard * 1536, 1536, axis=1)
    g_w_ada = _mm(a16, d16_sh, ta=True, tm=D, tn=512, tk=16, name="ada_dw", hi=True)
    g_b_ada = _rowcall("ada_db", lambda i, j, v: (_colsum(v),), 1, 16, [_In(d16)], [_Out(6 * D, acc=True)])[0]
    da_part = _mm(d16_sh, w_ada[0], tb=True, tm=16, tn=D, tk=512, name="ada_dx", hi=True)
    da_all = _all_gather(da_part, "ag_ada_dx")
    da16 = _sum_slots(da_all, 4, 16, 16, "ada_dx_sum", stride=2)
    dc16 = _rowcall("ada_dsilu", lambda i, j, d_, v: (d_ * _dsilu(v),), 1, 16, [_In(da16), _In(c16)], [_Out(D)])[0]

    grads = {
        "c_ctx": dc16[8], "w_ada": g_w_ada[None], "b_ada": g_b_ada, "norm_mix": gs["norm_mix"], "norm_ffn": gs["norm_ffn"],
        "w_in": gshard["w_in"][None],
        "dn_conv": lax.dynamic_slice_in_dim(gs["dn_conv"], shard * 768, 768, axis=1)[None],
        "dn_a_log": gs["dn_a_log"][None], "dn_dt_bias": gs["dn_dt_bias"][None], "dn_norm": gs["dn_norm"],
        "q_norm": gs["q_norm"], "k_norm": gs["k_norm"], "attn_sink": gs["attn_sink"],
        "w_branch_dn": gshard["w_branch_dn"][None], "w_branch_attn": gshard["w_branch_attn"][None],
        "w_out": gshard["w_out"][None], "ffn_up": gshard["ffn_up"][None],
        "ffn_conv": lax.dynamic_slice_in_dim(gs["ffn_conv"], shard * 1408, 1408, axis=1)[None],
        "ffn_conv_b": gs["ffn_conv_b"], "ffn_down": gshard["ffn_down"][None],
    }
    deltas, new_m, new_v = [], [], []
    for k in WEIGHTS:
        d_, m_, v_ = _adamw(args[k], grads[k], args["m_" + k], args["v_" + k], "adamw_" + k)
        deltas.append(d_)
        new_m.append(m_)
        new_v.append(v_)
    return (loss, grad_x[None], *[grads[k] for k in WEIGHTS], *deltas, *new_m, *new_v)
```

```python
import functools
import math

import numpy as np
import jax
import jax.numpy as jnp
from jax import lax
from jax.experimental import pallas as pl
from jax.experimental.pallas import tpu as pltpu

F32 = jnp.float32
BF16 = jnp.bfloat16
HI = lax.Precision.HIGHEST

D = 1024
NH = 8
HD = 128
CH = 64
CTX = 256
AB = 128
KVH = 2
GRP = 4
DFF = 2816
EPS = 1e-6
GRID_W = 64
ROPE_BASE = 10000.0
N_DEV = 8
VMEM_LIMIT = 56 * 1024 * 1024

C_QKV, C_GT, C_QAT, C_MG, C_KAT, C_VAT, C_BA, C_PAD = 0, 3072, 4096, 5120, 7168, 7424, 7680, 7808
PW = 8192


def _cparams(sem=None, **kw):
    return pltpu.CompilerParams(dimension_semantics=sem, vmem_limit_bytes=VMEM_LIMIT, **kw)


def _dot(a, b, dims, hi):
    if hi:
        return lax.dot_general(a.astype(F32), b.astype(F32), (dims, ((), ())), precision=HI, preferred_element_type=F32)
    return lax.dot_general(a.astype(BF16), b.astype(BF16), (dims, ((), ())), preferred_element_type=F32)


NN = ((1,), (0,))
NT = ((1,), (1,))
TN = ((0,), (0,))


def _dn_masks():
    i = np.arange(CH)
    lo_incl = (i[:, None] >= i[None, :]).astype(np.float32)
    lo_strict = (i[:, None] > i[None, :]).astype(np.float32)
    return jnp.asarray(np.stack([np.stack([lo_incl, lo_strict]), np.stack([lo_incl.T, lo_strict.T])]))


def _dn_chunk_index(d, i, n_ctx_chunks, n_chunks):
    fwd = i
    bwd = jnp.where(i < n_ctx_chunks, n_ctx_chunks - 1 - i, n_chunks - 1 + n_ctx_chunks - i)
    return jnp.where(d == 0, fwd, bwd)


BNN = ((2,), (1,))
BNT = ((2,), (2,))
BTN = ((1,), (1,))


def _bdot(a, b, dims, hi):
    dn = (dims, ((0,), (0,)))
    if hi:
        return lax.dot_general(a.astype(F32), b.astype(F32), dn, precision=HI, preferred_element_type=F32)
    return lax.dot_general(a.astype(BF16), b.astype(BF16), dn, preferred_element_type=F32)


def _bdot3(a, b, dims, hi):
    if hi:
        return _bdot(a, b, dims, True)
    ah, bh = a.astype(BF16), b.astype(BF16)
    al, bl = (a - ah.astype(F32)).astype(BF16), (b - bh.astype(F32)).astype(BF16)
    dn = (dims, ((0,), (0,)))
    d = lambda x_, y_: lax.dot_general(x_, y_, dn, preferred_element_type=F32)
    return d(ah, bh) + d(ah, bl) + d(al, bh)


DN_CB = 4
DN_SEQ_CB = 4


def _dn_heads(ref, cb=1):
    return jnp.stack([ref[t * CH:(t + 1) * CH, h * HD:(h + 1) * HD] for t in range(cb) for h in range(NH)])


def _dn_scalars(gb, mi, cb=1):
    beta, gc, gcr, gt = [], [], [], []
    for t in range(cb):
        g1 = gb[t * CH:(t + 1) * CH]
        gcum, gcum_t, gtot = _dn_gcum(g1, mi)
        beta += [g1[:, h:h + 1] for h in range(NH)]
        gc += [gcum[:, NH + h:NH + h + 1] for h in range(NH)]
        gcr += [gcum_t[NH + h:NH + h + 1, :] for h in range(NH)]
        gt += [gtot[:, NH + h:NH + h + 1] for h in range(NH)]
    return jnp.stack(beta), jnp.stack(gc), jnp.stack(gcr), jnp.stack(gt)


DN_NEWTON = 1


def _dn_inverse(a, hi):
    eye = (lax.broadcasted_iota(jnp.int32, (CH, CH), 0) == lax.broadcasted_iota(jnp.int32, (CH, CH), 1)).astype(F32)
    x = -a
    t = eye + x
    p = x
    if hi:
        for _ in range(5):
            p = _bdot(p, p, BNN, True)
            t = t + _bdot(t, p, BNN, True)
        return t
    for _ in range(5):
        p = _bdot(p, p, BNN, False)
        t = t + _bdot(t, p, BNN, False)
    for _ in range(DN_NEWTON):
        r = eye - t - _bdot3(a, t, BNN, False)
        t = t + _bdot(t, r, BNN, False)
    return t


def _dn_total(gb):
    gtot = jnp.sum(gb, axis=0, keepdims=True)
    return jnp.stack([gtot[:, NH + h:NH + h + 1] for h in range(NH)])


def _dn_gcum(gb, mi):
    gcum = _dot(mi, gb, NN, True)
    gtot = jnp.sum(gb, axis=0, keepdims=True)
    return gcum, gcum.T, gtot


def _dn_specs(n_ctx_chunks, n_chunks, reverse, cb):
    assert n_ctx_chunks % cb == 0 and n_chunks % cb == 0

    def grp(d, i):
        first = n_chunks - 1 - cb * i if reverse else cb * i
        return _dn_chunk_index(d, first, n_ctx_chunks, n_chunks) // cb

    def slot(d, t):
        ascending = (d == 1) if reverse else (d == 0)
        return jnp.where(ascending, t, cb - 1 - t)

    ctx_groups = n_ctx_chunks // cb
    tok_lat = pl.BlockSpec((cb * CH, D), lambda d, i: (jnp.maximum(grp(d, i) - ctx_groups, 0), 0))
    is_ctx = lambda d, i: grp(d, i) < ctx_groups
    tok_d = pl.BlockSpec((1, cb * CH, D), lambda d, i: (d, grp(d, i), 0))
    gbs = pl.BlockSpec((1, cb * CH, 128), lambda d, i: (d, grp(d, i), 0))

    def per_chunk(*tail):
        return pl.BlockSpec((1, cb) + tail, lambda d, i: (d, grp(d, i)) + (0,) * len(tail))

    return tok_lat, is_ctx, tok_d, gbs, per_chunk, slot


def _dn_group_specs(cb):
    tok = pl.BlockSpec((cb * CH, D), lambda d, i: (i, 0))
    tok_d = pl.BlockSpec((1, cb * CH, D), lambda d, i: (d, i, 0))
    gbs = pl.BlockSpec((1, cb * CH, 128), lambda d, i: (d, i, 0))
    msk = pl.BlockSpec((1, 2, CH, CH), lambda d, i: (d, 0, 0, 0))

    def per_chunk(*tail):
        return pl.BlockSpec((1, cb) + tail, lambda d, i: (d, i) + (0,) * len(tail))

    return tok, tok_d, gbs, msk, per_chunk


def _dn_intra_fwd(q, k, v, gb, n_ctx_chunks, hi):
    t_all = q.shape[0]
    n_chunks = t_all // CH
    masks = _dn_masks()

    cb = DN_CB

    def put(ref, val):
        for t_ in range(cb):
            ref[0, t_] = val[t_ * NH:(t_ + 1) * NH].astype(ref.dtype)

    def body(q_ref, k_ref, v_ref, gb_ref, m_ref, u_ref, w_ref, qg_ref, kd_ref, pm_ref, t_ref):
        mi, ms = m_ref[0, 0], m_ref[0, 1]
        beta, gc, gcr, gt = _dn_scalars(gb_ref[0], mi, cb)
        q_, k_, v_ = _dn_heads(q_ref, cb), _dn_heads(k_ref, cb), _dn_heads(v_ref, cb)
        decay = jnp.exp(jnp.where(mi > 0, gc - gcr, 0.0)) * mi
        e = jnp.exp(gc)
        a = ms * (beta * _bdot(k_, k_, BNT, hi) * decay)
        t = _dn_inverse(a, hi)
        uw =_bdot(t, jnp.concatenate([beta * v_, (beta * e) * k_], axis=2), BNN, hi)
        put(u_ref, uw[:, :, :HD])
        put(w_ref, uw[:, :, HD:])
        put(qg_ref, e * q_)
        put(kd_ref, jnp.exp(gt - gc) * k_)
        put(pm_ref, _bdot(q_, k_, BNT, hi) * decay)
        put(t_ref, t)

    tok, _, gbs, msk, per_chunk = _dn_group_specs(cb)
    big = lambda dt: jax.ShapeDtypeStruct((2, n_chunks, NH, CH, HD), dt)
    sq = jax.ShapeDtypeStruct((2, n_chunks, NH, CH, CH), BF16)
    return pl.pallas_call(
        body, name="dn_intra_fwd", grid=(2, n_chunks // cb),
        in_specs=[tok, tok, tok, gbs, msk],
        out_specs=[per_chunk(NH, CH, HD)] * 4 + [per_chunk(NH, CH, CH)] * 2,
        out_shape=[big(F32), big(BF16), big(BF16), big(BF16), sq, sq],
        compiler_params=_cparams(("parallel", "parallel")),
    )(q, k, v, gb, masks)


def _dn_seq_fwd(u, w, qg, kd, pm, gb, n_ctx_chunks, hi):
    n_chunks = u.shape[1]
    t_all = n_chunks * CH

    cb = DN_SEQ_CB
    _, _, tok_d, gbs, per_chunk, slot = _dn_specs(n_ctx_chunks, n_chunks, False, cb)

    def body(u_ref, w_ref, qg_ref, kd_ref, pm_ref, gb_ref, o_ref, sh_ref, vn_ref, s_scr):
        @pl.when(pl.program_id(1) == 0)
        def _():
            s_scr[...] = jnp.zeros_like(s_scr)

        for t in range(cb):
            j = slot(pl.program_id(0), t)
            rows = pl.ds(pl.multiple_of(j * CH, CH), CH)
            s = s_scr[...]
            sh_ref[0, j] = s.astype(sh_ref.dtype)
            vn = u_ref[0, j] - _bdot(w_ref[0, j], s, BNN, hi)
            o = _bdot(qg_ref[0, j], s, BNN, hi) + _bdot(pm_ref[0, j], vn, BNN, hi)
            s_scr[...] = jnp.exp(_dn_total(gb_ref[0, rows, :])) * s + _bdot(kd_ref[0, j], vn, BTN, hi)
            vn_ref[0, j] = vn.astype(vn_ref.dtype)
            for h in range(NH):
                o_ref[0, rows, h * HD:(h + 1) * HD] = o[h]

    big = per_chunk(NH, CH, HD)
    return pl.pallas_call(
        body, name="dn_seq_fwd", grid=(2, n_chunks // cb),
        in_specs=[big, big, big, big, per_chunk(NH, CH, CH), gbs],
        out_specs=[tok_d, per_chunk(NH, HD, HD), big],
        out_shape=[jax.ShapeDtypeStruct((2, t_all, D), F32), jax.ShapeDtypeStruct((2, n_chunks, NH, HD, HD), BF16),
                   jax.ShapeDtypeStruct((2, n_chunks, NH, CH, HD), BF16)],
        scratch_shapes=[pltpu.VMEM((NH, HD, HD), F32)],
        compiler_params=_cparams(("parallel", "arbitrary")),
    )(u, w, qg, kd, pm, gb)


def _dn_seq_bwd(w, qg, kd, pm, vn, s_hist, gb, do, n_ctx_chunks, hi):
    n_chunks = w.shape[1]

    cb = DN_SEQ_CB
    tok_lat, is_ctx, _, gbs, per_chunk, slot = _dn_specs(n_ctx_chunks, n_chunks, True, cb)

    def body(w_ref, qg_ref, kd_ref, pm_ref, vn_ref, sh_ref, gb_ref, do_ref, dvn_ref, dw_ref, dqg_ref, dkd_ref, del_ref, ds_scr):
        @pl.when(pl.program_id(1) == 0)
        def _():
            ds_scr[...] = jnp.zeros_like(ds_scr)

        for t in range(cb):
            j = slot(pl.program_id(0), t)
            rows = pl.ds(pl.multiple_of(j * CH, CH), CH)
            dsn = ds_scr[...]
            s = sh_ref[0, j]
            do_ = jnp.stack([do_ref[rows, h * HD:(h + 1) * HD] for h in range(NH)])
            do_ = jnp.where(is_ctx(pl.program_id(0), pl.program_id(1)), 0.0, do_)
            dvn =_bdot(pm_ref[0, j], do_, BTN, hi) + _bdot(kd_ref[0, j], dsn, BNN, hi)
            ds_scr[...] = (_bdot(qg_ref[0, j], do_, BTN, hi) + jnp.exp(_dn_total(gb_ref[0, rows, :])) * dsn
                           - _bdot(w_ref[0, j], dvn, BTN, hi))
            dvn_ref[0, j] = dvn.astype(dvn_ref.dtype)
            dw_ref[0, j] = (-_bdot(dvn, s, BNT, hi)).astype(dw_ref.dtype)
            dqg_ref[0, j] = _bdot(do_, s, BNT, hi)
            dkd_ref[0, j] = _bdot(vn_ref[0, j], dsn, BNT, hi)
            del_ref[0, j] = jnp.broadcast_to(jnp.sum(jnp.sum(s * dsn, axis=2, keepdims=True), axis=1, keepdims=True),
                                             (NH, 1, 128))

    big = per_chunk(NH, CH, HD)
    shp = lambda dt: jax.ShapeDtypeStruct((2, n_chunks, NH, CH, HD), dt)
    return pl.pallas_call(
        body, name="dn_seq_bwd", grid=(2, n_chunks // cb),
        in_specs=[big, big, big, per_chunk(NH, CH, CH), big, per_chunk(NH, HD, HD), gbs, tok_lat],
        out_specs=[big, big, big, big, per_chunk(NH, 1, 128)],
        out_shape=[shp(BF16), shp(BF16), shp(F32), shp(F32), jax.ShapeDtypeStruct((2, n_chunks, NH, 1, 128), F32)],
        scratch_shapes=[pltpu.VMEM((NH, HD, HD), F32)],
        compiler_params=_cparams(("parallel", "arbitrary")),
    )(w, qg, kd, pm, vn, s_hist, gb, do)


def _dn_intra_bwd(q, k, v, gb, u, w, t, vn, dvn, dw, dqg, dkd, de_last, do, n_ctx_chunks, hi):
    t_all = q.shape[0]
    n_chunks = t_all // CH
    masks = _dn_masks()

    cb = DN_CB
    assert n_ctx_chunks % cb == 0
    ctx_groups = n_ctx_chunks // cb

    def body(q_ref, k_ref, v_ref, gb_ref, m_ref, u_ref, w_ref, t_ref, vn_ref, dvn_ref, dw_ref, dqg_ref, dkd_ref, del_ref,
             do_ref, dq_ref, dk_ref, dv_ref, dgb_ref):
        mi, ms = m_ref[0, 0], m_ref[0, 1]
        beta, gc, gcr, gt = _dn_scalars(gb_ref[0], mi, cb)
        q_, k_, v_ = _dn_heads(q_ref, cb), _dn_heads(k_ref, cb), _dn_heads(v_ref, cb)
        do_ = jnp.where(pl.program_id(1) < ctx_groups, 0.0, _dn_heads(do_ref, cb))
        get = lambda ref: jnp.concatenate([ref[0, t_] for t_ in range(cb)], axis=0)
        decay = jnp.exp(jnp.where(mi > 0, gc - gcr, 0.0)) * mi
        e = jnp.exp(gc)
        e_last = jnp.exp(gt)
        kdfac = jnp.exp(gt - gc)
        kk = _bdot(k_, k_, BNT, hi)
        a = ms * (beta * kk * decay)
        pm = _bdot(q_, k_, BNT, hi) * decay
        kd = kdfac * k_
        dqg, dkd = get(dqg_ref), get(dkd_ref)
        dpm = _bdot(do_, get(vn_ref), BNT, hi)
        dvbkb = _bdot(get(t_ref), jnp.concatenate([get(dvn_ref), get(dw_ref)], axis=2), BTN, hi)
        dvb, dkb = dvbkb[:, :, :HD], dvbkb[:, :, HD:]
        da = -ms * _bdot(dvbkb, jnp.concatenate([get(u_ref), get(w_ref).astype(F32)], axis=2), BNT, hi)
        dqk = dpm * decay
        gm = dpm * pm + da * a
        dgc = (jnp.sum(gm, axis=2, keepdims=True)
               - _bdot3(gm, jnp.ones((cb * NH, CH, 128), F32), BTN, hi)[:, :, 0:1])
        dkk = da * (beta * decay)
        dbeta = jnp.sum(da * kk * decay, axis=2, keepdims=True)
        dk = _bdot(dkk, k_, BNN, hi) + _bdot(dkk, k_, BTN, hi) + _bdot(dqk, q_, BTN, hi)
        dq = _bdot(dqk, k_, BNN, hi) + e * dqg
        de = jnp.sum(dqg * q_, axis=2, keepdims=True)
        dv = beta * dvb
        dbeta = dbeta + jnp.sum(dvb * v_, axis=2, keepdims=True)
        skb = jnp.sum(dkb * k_, axis=2, keepdims=True)
        dk = dk + (beta * e) * dkb + kdfac * dkd
        dbeta = dbeta + e * skb
        de = de + beta * skb
        skd = jnp.sum(dkd * kd, axis=2, keepdims=True)
        dgc = dgc - skd + de * e
        dgtot = jnp.sum(skd, axis=1, keepdims=True) + get(del_ref)[:, :, 0:1] * e_last
        lane = lax.broadcasted_iota(jnp.int32, (1, 128), 1)
        for t_ in range(cb):
            rows = slice(t_ * CH, (t_ + 1) * CH)
            dbeta_all = jnp.zeros((CH, 128), F32)
            dgc_all = jnp.zeros((CH, 128), F32)
            dgtot_all = jnp.zeros((1, 128), F32)
            for h in range(NH):
                sl = slice(h * HD, (h + 1) * HD)
                b = t_ * NH + h
                dq_ref[0, rows, sl] = dq[b]
                dk_ref[0, rows, sl] = dk[b]
                dv_ref[0, rows, sl] = dv[b]
                hot_b = (lane == h).astype(F32)
                hot_g = (lane == NH + h).astype(F32)
                dbeta_all = dbeta_all + dbeta[b] * hot_b
                dgc_all = dgc_all + dgc[b] * hot_g
                dgtot_all = dgtot_all + dgtot[b] * hot_g
            dgb_ref[0, rows, :] = dbeta_all + _dot(mi, dgc_all, TN, True) + dgtot_all

    tok, tok_d, gbs, msk, per_chunk = _dn_group_specs(cb)
    tok_lat = pl.BlockSpec((cb * CH, D), lambda d, i: (jnp.maximum(i - ctx_groups, 0), 0))
    big = per_chunk(NH, CH, HD)
    return pl.pallas_call(
        body, name="dn_intra_bwd", grid=(2, n_chunks // cb),
        in_specs=[tok, tok, tok, gbs, msk, big, big, per_chunk(NH, CH, CH), big, big, big, big, big,
                  per_chunk(NH, 1, 128), tok_lat],
        out_specs=[tok_d, tok_d, tok_d, gbs],
        out_shape=[jax.ShapeDtypeStruct((2, t_all, D), F32)] * 3 + [jax.ShapeDtypeStruct((2, t_all, 128), F32)],
        compiler_params=_cparams(("parallel", "parallel")),
    )(q, k, v, gb, masks, u, w, t, vn, dvn, dw, dqg, dkd, de_last, do)


ATT_SCALE = HD ** -0.5
NEG = -1e30


def _att_stack(ref, kvh):
    return jnp.concatenate([ref[:, (kvh * GRP + g) * HD:(kvh * GRP + g + 1) * HD] for g in range(GRP)], axis=0)


def _att_col(ref, kvh):
    return jnp.concatenate([ref[:, kvh * GRP + g:kvh * GRP + g + 1] for g in range(GRP)], axis=0)


def _att_sink(sink_ref, kvh):
    return jnp.concatenate([jnp.broadcast_to(sink_ref[:, kvh * GRP + g:kvh * GRP + g + 1], (AB, 1)) for g in range(GRP)],
                           axis=0)


def _att_mask(i, nb):
    r = lax.broadcasted_iota(jnp.int32, (AB, AB), 0)
    c = lax.broadcasted_iota(jnp.int32, (AB, AB), 1)
    okp = jnp.logical_and(c >= r, i > 0)
    okn = jnp.logical_and(c <= r, i < nb - 1)
    return jnp.concatenate([okp] * GRP, axis=0), jnp.concatenate([okn] * GRP, axis=0)


def _att_masked(s, mask):
    mp, mn = mask
    return jnp.concatenate([jnp.where(mp, s[:, 0:AB], NEG), s[:, AB:2 * AB], jnp.where(mn, s[:, 2 * AB:3 * AB], NEG),
                            s[:, 3 * AB:]], axis=1)


def _att_kspecs(nb):
    nc = CTX // AB
    return [pl.BlockSpec((AB, KVH * HD), lambda i: (jnp.maximum(i - 1, 0) + nc, 0)),
            pl.BlockSpec((AB, KVH * HD), lambda i: (i + nc, 0)),
            pl.BlockSpec((AB, KVH * HD), lambda i: (jnp.minimum(i + 1, nb - 1) + nc, 0)),
            pl.BlockSpec((CTX, KVH * HD), lambda i: (0, 0))]


def _attn_fwd(qr, kr, vv, sink, hi):
    tl = qr.shape[0]
    nb = tl // AB

    def body(q_ref, kp_ref, kc_ref, kn_ref, kx_ref, vp_ref, vc_ref, vn_ref, vx_ref, sink_ref, o_ref, lse_ref):
        i = pl.program_id(0)
        mask = _att_mask(i, nb)
        lane = lax.broadcasted_iota(jnp.int32, (1, 128), 1)
        lse_all = jnp.zeros((AB, 128), F32)
        for kvh in range(KVH):
            ksl = slice(kvh * HD, (kvh + 1) * HD)
            kall = jnp.concatenate([kp_ref[:, ksl], kc_ref[:, ksl], kn_ref[:, ksl], kx_ref[:, ksl]], axis=0)
            vall = jnp.concatenate([vp_ref[:, ksl], vc_ref[:, ksl], vn_ref[:, ksl], vx_ref[:, ksl]], axis=0)
            s = _dot(_att_stack(q_ref, kvh), kall, NT, hi) * ATT_SCALE
            s = _att_masked(s, mask)
            sk = _att_sink(sink_ref, kvh)
            m = jnp.maximum(jnp.max(s, axis=1, keepdims=True), sk)
            p = jnp.exp(s - m)
            l = jnp.sum(p, axis=1, keepdims=True) + jnp.exp(sk - m)
            o = _dot(p, vall, NN, hi) / l
            lse = m + jnp.log(l)
            for g in range(GRP):
                h = kvh * GRP + g
                o_ref[:, h * HD:(h + 1) * HD] = o[g * AB:(g + 1) * AB]
                lse_all = lse_all + lse[g * AB:(g + 1) * AB] * (lane == h).astype(F32)
        lse_ref[...] = lse_all

    ks = _att_kspecs(nb)
    return pl.pallas_call(
        body, name="attn_fwd", grid=(nb,),
        in_specs=[pl.BlockSpec((AB, D), lambda i: (i, 0))] + ks + ks + [pl.BlockSpec((1, 128), lambda i: (0, 0))],
        out_specs=[pl.BlockSpec((AB, D), lambda i: (i, 0)), pl.BlockSpec((AB, 128), lambda i: (i, 0))],
        out_shape=[jax.ShapeDtypeStruct((tl, D), F32), jax.ShapeDtypeStruct((tl, 128), F32)],
        compiler_params=_cparams(("parallel",)),
    )(qr, kr, kr, kr, kr, vv, vv, vv, vv, sink)


def _mm_bat_dx_delta(dz_at, w_bat, o, hi):
    def fn(i, do_, o_):
        lane = lax.broadcasted_iota(jnp.int32, (1, 128), 1)
        acc = jnp.zeros((do_.shape[0], 128), F32)
        for h in range(NH):
            sl = slice(h * HD, (h + 1) * HD)
            acc = acc + jnp.sum(o_[:, sl] * do_[:, sl], axis=1, keepdims=True) * (lane == h).astype(F32)
        return do_, acc

    return _mm_ep("mm_bat_dx_delta", dz_at, w_bat, True, min(512, o.shape[0]), D, fn, [_In(o)], [_Out(D), _Out(128)], hi)


def _attn_bwd_q(qr, kr, vv, sink, do, lse, delta, hi):
    tl = qr.shape[0]
    nb = tl // AB

    def body(q_ref, kp_ref, kc_ref, kn_ref, kx_ref, vp_ref, vc_ref, vn_ref, vx_ref, sink_ref, do_ref, lse_ref, dl_ref,
             dq_ref, dkx_ref, dvx_ref, dsink_ref):
        i = pl.program_id(0)

        @pl.when(i == 0)
        def _():
            dkx_ref[...] = jnp.zeros_like(dkx_ref)
            dvx_ref[...] = jnp.zeros_like(dvx_ref)
            dsink_ref[...] = jnp.zeros_like(dsink_ref)

        mask = _att_mask(i, nb)
        lane = lax.broadcasted_iota(jnp.int32, (1, 128), 1)
        dsink = jnp.zeros((1, 128), F32)
        for kvh in range(KVH):
            ksl = slice(kvh * HD, (kvh + 1) * HD)
            kall = jnp.concatenate([kp_ref[:, ksl], kc_ref[:, ksl], kn_ref[:, ksl], kx_ref[:, ksl]], axis=0)
            vall = jnp.concatenate([vp_ref[:, ksl], vc_ref[:, ksl], vn_ref[:, ksl], vx_ref[:, ksl]], axis=0)
            qs = _att_stack(q_ref, kvh)
            dos = _att_stack(do_ref, kvh)
            lse_s = _att_col(lse_ref, kvh)
            dl_s = _att_col(dl_ref, kvh)
            s = _dot(qs, kall, NT, hi) * ATT_SCALE
            p = jnp.exp(_att_masked(s, mask) - lse_s)
            dp = _dot(dos, vall, NT, hi)
            ds = p * (dp - dl_s)
            dq = _dot(ds, kall, NN, hi) * ATT_SCALE
            dkx_ref[:, ksl] += _dot(ds[:, 3 * AB:], qs, TN, hi) * ATT_SCALE
            dvx_ref[:, ksl] += _dot(p[:, 3 * AB:], dos, TN, hi)
            psink = jnp.exp(_att_sink(sink_ref, kvh) - lse_s) * dl_s
            for g in range(GRP):
                h = kvh * GRP + g
                dq_ref[:, h * HD:(h + 1) * HD] = dq[g * AB:(g + 1) * AB]
                dsink = dsink - jnp.sum(psink[g * AB:(g + 1) * AB], axis=0, keepdims=True) * (lane == h).astype(F32)
        dsink_ref[...] += dsink

    ks = _att_kspecs(nb)
    row = pl.BlockSpec((AB, D), lambda i: (i, 0))
    col = pl.BlockSpec((AB, 128), lambda i: (i, 0))
    return pl.pallas_call(
        body, name="attn_bwd_q", grid=(nb,),
        in_specs=[row] + ks + ks + [pl.BlockSpec((1, 128), lambda i: (0, 0)), row, col, col],
        out_specs=[row, pl.BlockSpec((CTX, KVH * HD), lambda i: (0, 0)), pl.BlockSpec((CTX, KVH * HD), lambda i: (0, 0)),
                   pl.BlockSpec((1, 128), lambda i: (0, 0))],
        out_shape=[jax.ShapeDtypeStruct((tl, D), F32), jax.ShapeDtypeStruct((CTX, KVH * HD), F32),
                   jax.ShapeDtypeStruct((CTX, KVH * HD), F32), jax.ShapeDtypeStruct((1, 128), F32)],
        compiler_params=_cparams(("arbitrary",)),
    )(qr, kr, kr, kr, kr, vv, vv, vv, vv, sink, do, lse, delta)


def _attn_bwd_kv(qr, kr, vv, do, lse, delta, hi):
    tl = qr.shape[0]
    nb = tl // AB
    nc = CTX // AB

    def body(k_ref, v_ref, *refs):
        qs_refs, do_refs, lse_refs, dl_refs = refs[0:3], refs[3:6], refs[6:9], refs[9:12]
        dk_ref, dv_ref = refs[12], refs[13]
        j = pl.program_id(0)
        r = lax.broadcasted_iota(jnp.int32, (AB, AB), 0)
        c = lax.broadcasted_iota(jnp.int32, (AB, AB), 1)
        masks = [jnp.concatenate([jnp.logical_and(c <= r, j > 0)] * GRP, axis=0), None,
                 jnp.concatenate([jnp.logical_and(c >= r, j < nb - 1)] * GRP, axis=0)]
        for kvh in range(KVH):
            ksl = slice(kvh * HD, (kvh + 1) * HD)
            k_, v_ = k_ref[:, ksl], v_ref[:, ksl]
            dk = jnp.zeros((AB, HD), F32)
            dv = jnp.zeros((AB, HD), F32)
            for t in range(3):
                qs = _att_stack(qs_refs[t], kvh)
                dos = _att_stack(do_refs[t], kvh)
                lse_s = _att_col(lse_refs[t], kvh)
                dl_s = _att_col(dl_refs[t], kvh)
                s = _dot(qs, k_, NT, hi) * ATT_SCALE
                if masks[t] is not None:
                    s = jnp.where(masks[t], s, NEG)
                p = jnp.exp(s - lse_s)
                dp = _dot(dos, v_, NT, hi)
                ds = p * (dp - dl_s)
                dv = dv + _dot(p, dos, TN, hi)
                dk = dk + _dot(ds, qs, TN, hi) * ATT_SCALE
            dk_ref[:, ksl] = dk
            dv_ref[:, ksl] = dv

    def three(width):
        return [pl.BlockSpec((AB, width), lambda j: (jnp.maximum(j - 1, 0), 0)),
                pl.BlockSpec((AB, width), lambda j: (j, 0)),
                pl.BlockSpec((AB, width), lambda j: (jnp.minimum(j + 1, nb - 1), 0))]

    kv = pl.BlockSpec((AB, KVH * HD), lambda j: (j + nc, 0))
    out = pl.BlockSpec((AB, KVH * HD), lambda j: (j, 0))
    return pl.pallas_call(
        body, name="attn_bwd_kv", grid=(nb,),
        in_specs=[kv, kv] + three(D) + three(D) + three(128) + three(128),
        out_specs=[out, out],
        out_shape=[jax.ShapeDtypeStruct((tl, KVH * HD), F32)] * 2,
        compiler_params=_cparams(("parallel",)),
    )(kr, vv, qr, qr, qr, do, do, do, lse, lse, lse, delta, delta, delta)


def _mm(a, b, ta=False, tb=False, out_dtype=F32, tm=512, tn=1024, tk=1024, name="mm", hi=False):
    a_parts = a.shape[0] if a.ndim == 3 else 0
    b_parts = b.shape[0] if b.ndim == 3 else 0
    assert not (a_parts and ta) and not (b_parts and tb)
    if a_parts:
        m, kd = a.shape[1], a_parts * a.shape[2]
    else:
        m, kd = (a.shape[1], a.shape[0]) if ta else a.shape
    n = b_parts * b.shape[2] if b_parts else (b.shape[0] if tb else b.shape[1])
    tm, tn, tk = min(tm, m), min(tn, n), min(tk, kd)
    assert m % tm == 0 and n % tn == 0 and kd % tk == 0, (name, m, n, kd, tm, tn, tk)
    nk = kd // tk
    dims = ((0,) if ta else (1,), (1,) if tb else (0,))

    def body(a_ref, b_ref, o_ref, *scr):
        part = _dot(a_ref[0] if a_parts else a_ref[...], b_ref[0] if b_parts else b_ref[...], dims, hi)
        if nk == 1:
            o_ref[...] = part.astype(out_dtype)
        else:
            acc = scr[0]
            kk = pl.program_id(2)

            @pl.when(kk == 0)
            def _():
                acc[...] = part

            @pl.when(kk > 0)
            def _():
                acc[...] += part

            @pl.when(kk == nk - 1)
            def _():
                o_ref[...] = acc[...].astype(out_dtype)

    a_spec = pl.BlockSpec((tk, tm), lambda i, j, k: (k, i)) if ta else pl.BlockSpec((tm, tk), lambda i, j, k: (i, k))
    b_spec = pl.BlockSpec((tn, tk), lambda i, j, k: (j, k)) if tb else pl.BlockSpec((tk, tn), lambda i, j, k: (k, j))
    if a_parts:
        per = a.shape[2] // tk
        assert per * tk == a.shape[2]
        a_spec = pl.BlockSpec((1, tm, tk), lambda i, j, k: (k // per, i, k % per))
    if b_parts:
        per_n = b.shape[2] // tn
        assert per_n * tn == b.shape[2]
        b_spec = pl.BlockSpec((1, tk, tn), lambda i, j, k: (j // per_n, k, j % per_n))
    return pl.pallas_call(
        body, name=name, grid=(m // tm, n // tn, nk),
        in_specs=[a_spec, b_spec],
        out_specs=pl.BlockSpec((tm, tn), lambda i, j, k: (i, j)),
        out_shape=jax.ShapeDtypeStruct((m, n), out_dtype),
        scratch_shapes=[] if nk == 1 else [pltpu.VMEM((tm, tn), F32)],
        compiler_params=_cparams(("parallel", "parallel", "arbitrary")),
    )(a, b)


HALO = 8


class _In:
    def __init__(self, arr, w=None, cb=0, roff=0, halo=None, ridx=None):
        self.arr, self.w, self.cb, self.roff, self.halo = arr, w or arr.shape[1], cb, roff, halo
        self.ridx = ridx or (lambda i, roff=roff: i + roff)


class _Full:
    def __init__(self, arr, w=None, cb=0):
        self.arr, self.w, self.cb = arr, w, cb


class _Out:
    def __init__(self, cols, dtype=F32, w=None, cb=0, acc=False, rows=1, roff=0, nrows=None, stack=0):
        self.cols, self.dtype, self.w, self.cb, self.acc, self.rows, self.roff, self.nrows, self.stack = (
            cols, dtype, w or cols, cb, acc, rows, roff, nrows, stack)


def _rowcall(name, fn, nrow_tiles, tile, ins, outs, ncol=1):
    arrays, specs, kinds = [], [], []
    for x in ins:
        if isinstance(x, _Full):
            arrays.append(x.arr)
            if x.w is None:
                specs.append(pl.BlockSpec(x.arr.shape, lambda j, i: (0, 0)))
            else:
                specs.append(pl.BlockSpec((x.arr.shape[0], x.w), lambda j, i, cb=x.cb: (0, cb + j)))
            kinds.append("full")
            continue
        w, cb, roff = x.w, x.cb, x.roff
        cur = pl.BlockSpec((tile, w), lambda j, i, cb=cb, ridx=x.ridx: (ridx(i), cb + j))
        if x.halo is None:
            arrays.append(x.arr)
            specs.append(cur)
            kinds.append("tile")
        else:
            r8 = tile // HALO
            last = x.arr.shape[0] // HALO - 1
            prev = pl.BlockSpec((HALO, w), lambda j, i, cb=cb, roff=roff, r8=r8: (jnp.maximum((i + roff) * r8 - 1, 0), cb + j))
            nxt = pl.BlockSpec((HALO, w), lambda j, i, cb=cb, roff=roff, r8=r8, last=last:
                               (jnp.minimum((i + roff + 1) * r8, last), cb + j))
            arrays += [x.arr, x.arr, x.arr]
            specs += [prev, cur, nxt]
            kinds.append(("halo", x.halo))
    out_specs, out_shapes = [], []
    for o in outs:
        if o.acc:
            out_specs.append(pl.BlockSpec((o.rows, o.w), lambda j, i, cb=o.cb: (0, cb + j)))
            out_shapes.append(jax.ShapeDtypeStruct((o.rows, o.cols), o.dtype))
        elif o.stack:
            out_specs.append(pl.BlockSpec((o.stack, tile, o.w), lambda j, i, cb=o.cb: (0, i, cb + j)))
            out_shapes.append(jax.ShapeDtypeStruct((o.stack, nrow_tiles * tile, o.cols), o.dtype))
        else:
            out_specs.append(pl.BlockSpec((tile, o.w), lambda j, i, cb=o.cb, roff=o.roff: (i + roff, cb + j)))
            out_shapes.append(jax.ShapeDtypeStruct(((o.nrows or nrow_tiles * tile), o.cols), o.dtype))
    n_in = len(arrays)

    def body(*refs):
        j = pl.program_id(0)
        i = pl.program_id(1)
        vals, r = [], 0
        for kind in kinds:
            if kind in ("full", "tile"):
                vals.append(refs[r][...])
                r += 1
            else:
                pok, nok = kind[1]
                p, c, n = refs[r][...], refs[r + 1][...], refs[r + 2][...]
                p = jnp.where(pok(i), p, jnp.zeros_like(p))
                n = jnp.where(nok(i), n, jnp.zeros_like(n))
                vals.append(jnp.concatenate([p, c, n], axis=0))
                r += 3
        res = fn(i, j, *vals)
        for o, ref, val in zip(outs, refs[n_in:], res):
            if o.acc:
                @pl.when(i == 0)
                def _(ref=ref, val=val, o=o):
                    ref[...] = val.astype(o.dtype)

                @pl.when(i > 0)
                def _(ref=ref, val=val, o=o):
                    ref[...] += val.astype(o.dtype)
            elif o.stack:
                for s_ in range(o.stack):
                    ref[s_] = val[s_].astype(o.dtype)
            else:
                ref[...] = val.astype(o.dtype)

    return pl.pallas_call(
        body, name=name, grid=(ncol, nrow_tiles), in_specs=specs, out_specs=out_specs, out_shape=out_shapes,
        compiler_params=_cparams(("parallel", "arbitrary")),
    )(*arrays)


def _mm_ep(name, a, b, tb, tm, tk, fn, ins, outs, hi=False):
    a_parts = a.shape[0] if a.ndim == 3 else 0
    m, kd = (a.shape[1], a_parts * a.shape[2]) if a_parts else a.shape
    n = b.shape[0] if tb else b.shape[1]
    tk = min(tk, kd)
    assert m % tm == 0 and kd % tk == 0, (name, m, kd, tm, tk)
    nk = kd // tk
    dims = ((1,), (1,) if tb else (0,))
    if a_parts:
        per = a.shape[2] // tk
        arrays, specs = [a], [pl.BlockSpec((1, tm, tk), lambda i, k: (k // per, i, k % per))]
    else:
        arrays, specs = [a], [pl.BlockSpec((tm, tk), lambda i, k: (i, k))]
    arrays.append(b)
    specs.append(pl.BlockSpec((n, tk), lambda i, k: (0, k)) if tb else pl.BlockSpec((tk, n), lambda i, k: (k, 0)))
    for x in ins:
        arrays.append(x.arr)
        if isinstance(x, _Full):
            specs.append(pl.BlockSpec(x.arr.shape, lambda i, k: (0, 0)))
        else:
            specs.append(pl.BlockSpec((tm, x.w), lambda i, k, cb=x.cb, ridx=x.ridx: (ridx(i), cb)))
    out_specs, out_shapes = [], []
    for o in outs:
        if o.acc:
            out_specs.append(pl.BlockSpec((o.rows, o.w), lambda i, k, cb=o.cb: (0, cb)))
            out_shapes.append(jax.ShapeDtypeStruct((o.rows, o.cols), o.dtype))
        else:
            out_specs.append(pl.BlockSpec((tm, o.w), lambda i, k, cb=o.cb, roff=o.roff: (i + roff, cb)))
            out_shapes.append(jax.ShapeDtypeStruct((o.nrows or m, o.cols), o.dtype))
    n_in = len(arrays)

    def body(*refs):
        i, kk = pl.program_id(0), pl.program_id(1)
        a_ref, b_ref = refs[0], refs[1]
        acc_ref = refs[-1]
        part = _dot(a_ref[0] if a_parts else a_ref[...], b_ref[...], dims, hi)

        @pl.when(kk == 0)
        def _():
            acc_ref[...] = part

        @pl.when(kk > 0)
        def _():
            acc_ref[...] += part

        @pl.when(kk == nk - 1)
        def _():
            res = fn(i, acc_ref[...], *[r[...] for r in refs[2:n_in]])
            for o, ref, val in zip(outs, refs[n_in:-1], res):
                if o.acc:
                    @pl.when(i == 0)
                    def _(ref=ref, val=val, o=o):
                        ref[...] = val.astype(o.dtype)

                    @pl.when(i > 0)
                    def _(ref=ref, val=val, o=o):
                        ref[...] += val.astype(o.dtype)
                else:
                    ref[...] = val.astype(o.dtype)

    return pl.pallas_call(
        body, name=name, grid=(m // tm, nk), in_specs=specs, out_specs=out_specs, out_shape=out_shapes,
        scratch_shapes=[pltpu.VMEM((tm, n), F32)],
        compiler_params=_cparams(("arbitrary", "arbitrary")),
    )(*arrays)


def _shift(xe, s, tile):
    if s == 0:
        return xe[HALO:HALO + tile]
    return pltpu.roll(xe, (-s) % xe.shape[0], 0)[HALO:HALO + tile]


def _silu(x):
    return x * jax.nn.sigmoid(x)


def _dsilu(x):
    s = jax.nn.sigmoid(x)
    return s * (1.0 + x * (1.0 - s))


def _heads(x, fn):
    return jnp.concatenate([fn(h, x[:, h * HD:(h + 1) * HD]) for h in range(x.shape[1] // HD)], axis=1)


def _colsum(x):
    return jnp.sum(x, axis=0, keepdims=True)


def _rowmean(x):
    return jnp.mean(x, axis=1, keepdims=True)


def _rowsum(x):
    return jnp.sum(x, axis=1, keepdims=True)


TILE = 256
CT = CTX // TILE


def _all_halo(n_tiles):
    return (lambda i: i >= CT + 1, lambda i: jnp.logical_and(i >= CT, i < n_tiles - 1))


def _lat_halo(n_tiles):
    return (lambda i: i >= 1, lambda i: i < n_tiles - 1)


def _rms_mod(x, nm, shift, scale):
    r = lax.rsqrt(_rowmean(x * x) + EPS)
    return (x * r * nm) * (1.0 + scale) + shift


def _rms_mod_bwd(dh, x, nm, scale):
    r = lax.rsqrt(_rowmean(x * x) + EPS)
    xn = x * r
    dz = dh * (1.0 + scale)
    dxn = dz * nm
    dx = r * (dxn - xn * _rowmean(dxn * xn))
    return dx, _colsum(dz * xn), _colsum(dh), _colsum(dh * (xn * nm))


def _norm_mod(x, ctx, nm, mod_c, mod_x):
    n = (x.shape[0] + ctx.shape[0]) // TILE

    def fn(i, j, c_, x_, nm_, mc, mx):
        m = jnp.where(i < CT, mc, mx)
        return (_rms_mod(jnp.where(i < CT, c_, x_), nm_, m[0:1], m[1:2]),)

    ins = [_In(ctx, ridx=lambda i: jnp.minimum(i, CT - 1)), _In(x, ridx=lambda i: jnp.maximum(i - CT, 0)),
           _Full(nm), _Full(mod_c), _Full(mod_x)]
    return _rowcall("norm_mod", fn, n, TILE, ins, [_Out(D, BF16)])[0]


def _norm_mod_bwd(dh, xs, dres, nm, mod, roff, n):
    ins = [_In(dh, roff=roff), _In(xs), _Full(nm), _Full(mod)] + ([] if dres is None else [_In(dres)])

    def fn(i, j, dh_, x, nm_, m, *rest):
        dx, dn, dsh, dsc = _rms_mod_bwd(dh_, x, nm_, m[1:2])
        if rest:
            return (dx + rest[0], dn, dsh, dsc)
        return (dn, dsh, dsc)

    accs = [_Out(D, acc=True), _Out(D, acc=True), _Out(D, acc=True)]
    return _rowcall("norm_mod_bwd", fn, n, TILE, ins, ([] if dres is None else [_Out(D)]) + accs)


DN_Q_SCALE = HD ** -0.5


def _conv_taps(xe, w, width, rows=None):
    r = width // 2
    acc = None
    for t in range(width):
        s = t - r
        if rows is None:
            sh = xe if s == 0 else pltpu.roll(xe, (-s) % xe.shape[0], 0)
        else:
            sh = _shift(xe, s, rows)
        term = sh * w[t:t + 1]
        acc = term if acc is None else acc + term
    return acc


def _rolled(xe, width):
    r = width // 2
    return [xe if t == r else pltpu.roll(xe, (r - t) % xe.shape[0], 0) for t in range(width)]


def _conv_bwd(rolled, w, c_grad, width):
    r = width // 2
    cc = c_grad[HALO:HALO + TILE]
    dx, dws = None, []
    for t in range(width):
        term = _shift(c_grad, r - t, TILE) * w[t:t + 1]
        dx = term if dx is None else dx + term
        dws.append(_colsum(cc * rolled[t][HALO:HALO + TILE]))
    return dx, jnp.concatenate(dws + [jnp.zeros((8 - width, cc.shape[1]), F32)], axis=0)


def _silu_both(x):
    s = jax.nn.sigmoid(x)
    return x * s, s * (1.0 + x * (1.0 - s))


def _l2n(x, scale):
    rn = lax.rsqrt(_rowsum(x * x) + EPS)
    return x * (rn * scale)


def _l2n_bwd(dy, x, scale):
    rn = lax.rsqrt(_rowsum(x * x) + EPS)
    xu = x * rn
    return (scale * rn) * (dy - xu * _rowsum(dy * xu))


def _softplus(x):
    return jnp.maximum(x, 0.0) + jnp.log(1.0 + jnp.exp(-jnp.abs(x)))


def _lane_mask(lo, hi_):
    lane = lax.broadcasted_iota(jnp.int32, (1, 128), 1)
    return jnp.logical_and(lane >= lo, lane < hi_).astype(F32)


def _dn_prep(p, conv_w, gprm):
    n = p.shape[0] // TILE
    halo = _all_halo(n)

    def fn(i, j, qe, ke, ve, ba, w, gp):
        cq = _conv_taps(qe, w[:, 0:D], 5, TILE)
        ck = _conv_taps(ke, w[:, D:2 * D], 5, TILE)
        cv = _conv_taps(ve, w[:, 2 * D:3 * D], 5, TILE)
        q = _heads(_silu(cq), lambda h, x: _l2n(x, DN_Q_SCALE))
        k = _heads(_silu(ck), lambda h, x: _l2n(x, 1.0))
        v = _silu(cv)
        beta = jax.nn.sigmoid(ba)
        g = -jnp.exp(gp[0:1]) * _softplus(ba + gp[1:2])
        m0, m1 = _lane_mask(0, 8), _lane_mask(8, 16)
        gb_f = beta * m0 + pltpu.roll(g, 128 - 8, 1) * m1
        gb_b = pltpu.roll(beta, 128 - 8, 1) * m0 + pltpu.roll(g, 128 - 16, 1) * m1
        return q, k, v, gb_f, gb_b

    ins = [_In(p, D, 0, halo=halo), _In(p, D, 1, halo=halo), _In(p, D, 2, halo=halo), _In(p, 128, C_BA // 128),
           _Full(conv_w), _Full(gprm)]
    return _rowcall("dn_prep", fn, n, TILE, ins, [_Out(D), _Out(D), _Out(D), _Out(128), _Out(128)])


def _dn_prep_bwd(p, conv_w, gprm, dq2, dk2, dv2, dgb2):
    n = p.shape[0] // TILE
    halo = _all_halo(n)

    def branch(xe, w, dye, scale):
        rolled = _rolled(xe, 5)
        c = rolled[0] * w[0:1]
        for t in range(1, 5):
            c = c + rolled[t] * w[t:t + 1]
        sx, dsilu = _silu_both(c)
        if scale is None:
            dsx = dye
        else:
            dsx = jnp.concatenate([_l2n_bwd(dye[:, h * HD:(h + 1) * HD], sx[:, h * HD:(h + 1) * HD], scale)
                                   for h in range(NH)], axis=1)
        return _conv_bwd(rolled, w, dsx * dsilu, 5)

    def fn(i, j, qe, ke, ve, ba, w, gp, dq0, dq1, dk0, dk1, dv0, dv1, dg0, dg1):
        dxq, dwq = branch(qe, w[:, 0:D], dq0 + dq1, DN_Q_SCALE)
        dxk, dwk = branch(ke, w[:, D:2 * D], dk0 + dk1, 1.0)
        dxv, dwv = branch(ve, w[:, 2 * D:3 * D], dv0 + dv1, None)
        m0, m1 = _lane_mask(0, 8), _lane_mask(8, 16)
        dbeta = dg0 * m0 + pltpu.roll(dg1 * m0, 8, 1)
        dg = pltpu.roll(dg0 * m1, 8, 1) + pltpu.roll(dg1 * m1, 16, 1)
        beta = jax.nn.sigmoid(ba)
        ea = jnp.exp(gp[0:1])
        z = ba + gp[1:2]
        g = -ea * _softplus(z)
        mg = _lane_mask(16, 32)
        da = dg * (-ea) * jax.nn.sigmoid(z) * mg
        dba = dbeta * beta * (1.0 - beta) * _lane_mask(0, 16) + da
        dgp = jnp.concatenate([_colsum(dg * g * mg), _colsum(da)], axis=0)
        return (jnp.concatenate([dxq, dxk, dxv], axis=1), dba, jnp.concatenate([dwq, dwk, dwv], axis=1), dgp)

    ins = [_In(p, D, 0, halo=halo), _In(p, D, 1, halo=halo), _In(p, D, 2, halo=halo), _In(p, 128, C_BA // 128),
           _Full(conv_w), _Full(gprm),
           _In(dq2, halo=halo), _In(dq2, roff=n, halo=halo), _In(dk2, halo=halo), _In(dk2, roff=n, halo=halo),
           _In(dv2, halo=halo), _In(dv2, roff=n, halo=halo), _In(dgb2), _In(dgb2, roff=n)]
    return _rowcall("dn_prep_bwd", fn, n, TILE, ins,
                    [_Out(3 * D, BF16), _Out(128, BF16), _Out(3 * D, acc=True, rows=8), _Out(128, acc=True, rows=2)])


def _hnorm(x, w):
    return x * lax.rsqrt(_rowmean(x * x) + EPS) * w


def _hnorm_bwd(dy, x, w):
    r = lax.rsqrt(_rowmean(x * x) + EPS)
    xh = x * r
    dxh = dy * w
    return r * (dxh - xh * _rowmean(dxh * xh)), _colsum(dy * xh)


def _dn_gate(o2, p, dn_norm, n_all):
    n = n_all - CT

    def fn(i, j, of, ob, gt, w):
        o = of + ob
        return (_heads(o, lambda h, x: _hnorm(x, w)) * _silu(gt),)

    ins = [_In(o2, roff=CT), _In(o2, roff=n_all + CT), _In(p, D, C_GT // D, roff=CT), _Full(dn_norm)]
    return _rowcall("dn_gate", fn, n, TILE, ins, [_Out(D, BF16)])[0]


def _mm_bdn_dx_gate(dz_dn, w_bdn, o2, p, dn_norm, n_all, hi):
    def fn(i, dy_, of, ob, gt, w):
        o = of + ob
        sg, dsg = _silu_both(gt)
        dos, dw = [], jnp.zeros((1, HD), F32)
        yn = []
        for h in range(NH):
            sl = slice(h * HD, (h + 1) * HD)
            dx, dwh = _hnorm_bwd(dy_[:, sl] * sg[:, sl], o[:, sl], w)
            dos.append(dx)
            dw = dw + dwh
            yn.append(_hnorm(o[:, sl], w))
        dgt = dy_ * jnp.concatenate(yn, axis=1) * dsg
        return jnp.concatenate(dos, axis=1), dgt, dw

    ins = [_In(o2, roff=CT), _In(o2, roff=n_all + CT), _In(p, D, C_GT // D, roff=CT), _Full(dn_norm)]
    outs = [_Out(D), _Out(D, BF16), _Out(HD, acc=True)]
    return _mm_ep("mm_bdn_dx_gate", dz_dn, w_bdn, True, TILE, D, fn, ins, outs, hi)


def _rope_shuffle(x):
    lane = lax.broadcasted_iota(jnp.int32, (1, HD), 1)
    return jnp.where((lane % 64) < 32, pltpu.roll(x, HD - 32, 1), pltpu.roll(x, 32, 1))


def _rope(x, cos, sin):
    return x * cos + _rope_shuffle(x) * sin


def _rope_bwd(dy, cos, sin):
    return dy * cos + _rope_shuffle(dy * sin)


def _attn_prep(p, w, cos, sin, width, cb, roff, n, name):
    def fn(i, j, x, w_, c, s):
        return (_heads(x, lambda h, xh: _rope(_hnorm(xh, w_), c, s)),)

    ins = [_In(p, width, cb, roff=roff), _Full(w), _In(cos), _In(sin)]
    return _rowcall(name, fn, n, TILE, ins, [_Out(width)])[0]


def _attn_prep_bwd(dy, p, w, cos, sin, width, cb, roff, n, name):
    def fn(i, j, dy_, x, w_, c, s):
        dxs, dw = [], jnp.zeros((1, HD), F32)
        for h in range(width // HD):
            sl = slice(h * HD, (h + 1) * HD)
            dx, dwh = _hnorm_bwd(_rope_bwd(dy_[:, sl], c, s), x[:, sl], w_)
            dxs.append(dx)
            dw = dw + dwh
        return jnp.concatenate(dxs, axis=1), dw

    ins = [_In(dy), _In(p, width, cb, roff=roff), _Full(w), _In(cos), _In(sin)]
    return _rowcall(name, fn, n, TILE, ins, [_Out(width, BF16), _Out(HD, acc=True)])


def _mm_bat_merge(o_at, w_bat, z_dn, p, hi):
    def fn(i, za, zd, gd, ga):
        return za, jax.nn.sigmoid(gd) * zd + jax.nn.sigmoid(ga) * za

    ins = [_In(z_dn), _In(p, D, C_MG // D, roff=CT), _In(p, D, C_MG // D + 1, roff=CT)]
    return _mm_ep("mm_bat_merge", o_at, w_bat, False, TILE, D, fn, ins, [_Out(D), _Out(D, BF16)], hi)


def _mm_out_dx_merge(dmo, w_out, z_dn, z_at, p, hi):
    def fn(i, dm_, zd, za, gd, ga):
        sd, sa = jax.nn.sigmoid(gd), jax.nn.sigmoid(ga)
        dg = jnp.concatenate([dm_ * zd * sd * (1.0 - sd), dm_ * za * sa * (1.0 - sa)], axis=1)
        return dm_ * sd, dm_ * sa, dg

    ins = [_In(z_dn), _In(z_at), _In(p, D, C_MG // D, roff=CT), _In(p, D, C_MG // D + 1, roff=CT)]
    outs = [_Out(D, BF16), _Out(D, BF16), _Out(2 * D, BF16)]
    return _mm_ep("mm_out_dx_merge", dmo, w_out, True, TILE, D, fn, ins, outs, hi)


def _mm_out_resid(merged, w_out, x, g_a, nf, mod_f, hi):
    def fn(i, mo_, x_, ga, nf_, m):
        x1 = x_ + ga * mo_
        return mo_, x1, _rms_mod(x1, nf_, m[0:1], m[1:2])

    ins = [_In(x), _Full(g_a), _Full(nf), _Full(mod_f)]
    return _mm_ep("mm_out_resid", merged, w_out, False, min(512, x.shape[0]), D, fn, ins, [_Out(D), _Out(D), _Out(D, BF16)], hi)


def _mm_up_dx_norm(du, ffn_up, dy, x1, mo, g_a, nf, mod_f, hi):
    def fn(i, dh_, dy_, x1_, mo_, ga, nf_, m):
        dx, dn, dsh, dsc = _rms_mod_bwd(dh_, x1_, nf_, m[1:2])
        dx1 = dy_ + dx
        return dx1, ga * dx1, dn, dsh, dsc, _colsum(dx1 * mo_)

    ins = [_In(dy), _In(x1), _In(mo), _Full(g_a), _Full(nf), _Full(mod_f)]
    accs = [_Out(D, acc=True) for _ in range(4)]
    return _mm_ep("mm_up_dx_norm", du, ffn_up, True, min(512, x1.shape[0]), 1408, fn, ins, [_Out(D), _Out(D, BF16)] + accs, hi)


def _mm_down_loss(a, ffn_down, x1, tgt, g_f, hi):
    def fn(i, f_, x1_, t, gf):
        e = x1_ + gf * f_ - t
        dy = e * (1.0 / D)
        loss = _colsum(_rowsum(e * e)) * (0.5 / D)
        return dy, gf * dy, _colsum(dy * f_), jnp.broadcast_to(loss, (1, 128))

    ins = [_In(x1), _In(tgt), _Full(g_f)]
    outs = [_Out(D), _Out(D, BF16), _Out(D, acc=True), _Out(128, acc=True)]
    return _mm_ep("mm_down_loss", a, ffn_down, False, min(512, x1.shape[0]), DFF, fn, ins, outs, hi)


FW = DFF // 2


def _ffn_act(u, conv_w, conv_b, n):
    halo = _lat_halo(n)

    def fn(i, j, ge, ve, wg, wv, bg, bv):
        cg = _conv_taps(ge, wg, 3, TILE) + bg
        cv = _conv_taps(ve, wv, 3, TILE) + bv
        return (_silu(cg) * cv,)

    ins = [_In(u, FW, 0, halo=halo), _In(u, FW, 2, halo=halo), _Full(conv_w, FW, 0), _Full(conv_w, FW, 2),
           _Full(conv_b, FW, 0), _Full(conv_b, FW, 2)]
    return _rowcall("ffn_act", fn, n, TILE, ins, [_Out(DFF, BF16, FW)], ncol=2)[0]


def _ffn_act_bwd(u, da, conv_w, conv_b, n):
    halo = _lat_halo(n)

    def fn(i, j, ge, ve, dae, wg, wv, bg, bv):
        rg, rv = _rolled(ge, 3), _rolled(ve, 3)
        cg = rg[0] * wg[0:1] + rg[1] * wg[1:2] + rg[2] * wg[2:3] + bg
        cv = rv[0] * wv[0:1] + rv[1] * wv[1:2] + rv[2] * wv[2:3] + bv
        sg, dsg = _silu_both(cg)
        dcg = dae * cv * dsg
        dcv = dae * sg
        dxg, dwg = _conv_bwd(rg, wg, dcg, 3)
        dxv, dwv = _conv_bwd(rv, wv, dcv, 3)
        return (dxg, dxv), dwg, dwv, _colsum(dcg[HALO:HALO + TILE]), _colsum(dcv[HALO:HALO + TILE])

    ins = [_In(u, FW, 0, halo=halo), _In(u, FW, 2, halo=halo), _In(da, FW, 0, halo=halo),
           _Full(conv_w, FW, 0), _Full(conv_w, FW, 2), _Full(conv_b, FW, 0), _Full(conv_b, FW, 2)]
    outs = [_Out(DFF, BF16, FW, stack=2), _Out(DFF, w=FW, acc=True, rows=8), _Out(DFF, w=FW, acc=True, rows=8),
            _Out(DFF, w=FW, acc=True), _Out(DFF, w=FW, acc=True)]
    return _rowcall("ffn_act_bwd", fn, n, TILE, ins, outs, ncol=2)


def _rope_tables(tl):
    rows = tl // GRID_W
    inv = np.float32(ROPE_BASE) ** (-np.arange(32, dtype=np.float32) / np.float32(32))
    ar = np.arange(rows, dtype=np.float32)[:, None] * inv
    ac = np.arange(GRID_W, dtype=np.float32)[:, None] * inv

    def table(r, c):
        full = (rows, GRID_W, HD // 2)
        return jnp.concatenate([jnp.broadcast_to(jnp.asarray(r)[:, None, :], full),
                                jnp.broadcast_to(jnp.asarray(c)[None, :, :], full)], axis=2).reshape(tl, HD)

    two = lambda a, b: np.concatenate([a, b], axis=1).astype(np.float32)
    cos = table(two(np.cos(ar), np.cos(ar)), two(np.cos(ac), np.cos(ac)))
    sin = table(two(-np.sin(ar), np.sin(ar)), two(-np.sin(ac), np.sin(ac)))
    return cos, sin


def _pad_w_in(w_in):
    z = lambda n: jnp.zeros((D, n), w_in.dtype)
    return jnp.concatenate([w_in[:, 0:4096], w_in[:, 4128:5152], w_in[:, 5664:7712], w_in[:, 5152:5664],
                            w_in[:, 4096:4128], z(96 + PW - C_PAD)], axis=1)


def _unpad_w_in(g, axis=1):
    cut = lambda a, b: lax.slice_in_dim(g, a, b, axis=axis)
    return jnp.concatenate([cut(0, 4096), cut(C_BA, C_BA + 32), cut(C_QAT, C_QAT + D), cut(C_KAT, C_KAT + 512),
                            cut(C_MG, C_MG + 2 * D)], axis=axis)


def _local_step(x, ctx, tgt, mod_x, mod_c, w, hi=False):
    tl = x.shape[0]
    t_all = tl + CTX
    n_all, n = t_all // TILE, tl // TILE
    tm_all = 1280 if t_all % 1280 == 0 else TILE
    tm_lat = 1024
    mm = functools.partial(_mm, hi=hi)
    sp = lambda m: [m[:, k * D:(k + 1) * D] for k in range(6)]
    sh_a, sc_a, g_a, sh_f, sc_f, g_f = sp(mod_x)
    sh_ac, sc_ac = sp(mod_c)[:2]
    mod_ax = jnp.concatenate([sh_a, sc_a], axis=0)
    mod_ac = jnp.concatenate([sh_ac, sc_ac], axis=0)
    mod_f = jnp.concatenate([sh_f, sc_f], axis=0)
    nm, nf = w["norm_mix"], w["norm_ffn"]
    cos, sin = _rope_tables(tl)
    cos_all = jnp.concatenate([jnp.ones((CTX, HD), F32), cos], axis=0)
    sin_all = jnp.concatenate([jnp.zeros((CTX, HD), F32), sin], axis=0)
    conv_dn = jnp.concatenate([w["dn_conv"], jnp.zeros((3, 3 * D), F32)], axis=0)
    gprm = jnp.concatenate([jnp.zeros((2, 16), F32),
                            jnp.concatenate([w["dn_a_log"].reshape(1, 16), w["dn_dt_bias"].reshape(1, 16)], axis=0),
                            jnp.zeros((2, 96), F32)], axis=1)
    conv_ff = jnp.concatenate([w["ffn_conv"], jnp.zeros((5, 2 * DFF), F32)], axis=0)
    sink = jnp.concatenate([w["attn_sink"].reshape(1, NH), jnp.zeros((1, 128 - NH), F32)], axis=1)
    nct = CTX // CH

    h = _norm_mod(x, ctx, nm, mod_ac, mod_ax)
    p = mm(h, w["w_in_p"], tm=tm_all, tn=1024, name="mm_in")
    q, k, v, gb_f, gb_b = _dn_prep(p, conv_dn, gprm)
    gb = jnp.stack([gb_f, gb_b])
    dn_u, dn_w, dn_qg, dn_kd, dn_pm, dn_t = _dn_intra_fwd(q, k, v, gb, nct, hi)
    o2, s_hist, dn_vn = _dn_seq_fwd(dn_u, dn_w, dn_qg, dn_kd, dn_pm, gb, nct, hi)
    o2 = o2.reshape(2 * t_all, D)
    y_dn = _dn_gate(o2, p, w["dn_norm"], n_all)
    qr = _attn_prep(p, w["q_norm"], cos, sin, D, C_QAT // D, CT, n, "attn_prep_q")
    kr = _attn_prep(p, w["k_norm"], cos_all, sin_all, KVH * HD, C_KAT // (KVH * HD), 0, n_all, "attn_prep_k")
    vv = p[:, C_VAT:C_VAT + KVH * HD]
    o_at, lse = _attn_fwd(qr, kr, vv, sink, hi)
    z_dn = mm(y_dn, w["w_branch_dn"], tm=tm_lat, name="mm_bdn")
    z_at, merged = _mm_bat_merge(o_at, w["w_branch_attn"], z_dn, p, hi)
    mo, x1, h2 = _mm_out_resid(merged, w["w_out"], x, g_a, nf, mod_f, hi)
    u = mm(h2, w["ffn_up"], tm=2 * tm_lat, tn=1408, name="mm_up")
    a = _ffn_act(u, conv_ff, w["ffn_conv_b"], n)
    dy, df, dg_f, loss = _mm_down_loss(a, w["ffn_down"], x1, tgt, g_f, hi)

    g = {}
    da = mm(df, w["ffn_down"], tb=True, tm=tm_lat, tn=1408, name="mm_down_dx")
    g["ffn_down"] = mm(a, df, ta=True, tm=1408, tn=1024, tk=tm_lat, name="mm_down_dw")
    du, dcw_g, dcw_v, dcb_g, dcb_v = _ffn_act_bwd(u, da, conv_ff, w["ffn_conv_b"], n)
    g["ffn_conv"] = jnp.concatenate([dcw_g, dcw_v], axis=1)[0:3]
    g["ffn_conv_b"] = jnp.concatenate([dcb_g, dcb_v], axis=1)
    g["ffn_up"] = mm(h2, du, ta=True, tm=1024, tn=1408, tk=tm_lat, name="mm_up_dw")
    dx1, dmo, g["norm_ffn"], dsh_f, dsc_f, dg_a = _mm_up_dx_norm(du, w["ffn_up"], dy, x1, mo, g_a, nf, mod_f, hi)
    g["w_out"] = mm(merged, dmo, ta=True, tm=1024, tk=tm_lat, name="mm_out_dw")
    dz_dn, dz_at, dmg = _mm_out_dx_merge(dmo, w["w_out"], z_dn, z_at, p, hi)
    g["w_branch_dn"] = mm(y_dn, dz_dn, ta=True, tm=1024, tk=tm_lat, name="mm_bdn_dw")
    do_at, delta = _mm_bat_dx_delta(dz_at, w["w_branch_attn"], o_at, hi)
    g["w_branch_attn"] = mm(o_at, dz_at, ta=True, tm=1024, tk=tm_lat, name="mm_bat_dw")

    do_dn, dgt, g["dn_norm"] = _mm_bdn_dx_gate(dz_dn, w["w_branch_dn"], o2, p, w["dn_norm"], n_all, hi)
    do_all = do_dn
    dn_dvn, dn_dw, dn_dqg, dn_dkd, dn_del = _dn_seq_bwd(dn_w, dn_qg, dn_kd, dn_pm, dn_vn, s_hist, gb, do_all, nct, hi)
    dq2, dk2, dv2, dgb2 = _dn_intra_bwd(q, k, v, gb, dn_u, dn_w, dn_t, dn_vn, dn_dvn, dn_dw, dn_dqg, dn_dkd, dn_del,
                                        do_all, nct, hi)
    dqkv, dba, dconv, dgprm = _dn_prep_bwd(p, conv_dn, gprm, dq2.reshape(2 * t_all, D), dk2.reshape(2 * t_all, D),
                                           dv2.reshape(2 * t_all, D), dgb2.reshape(2 * t_all, 128))
    g["dn_conv"] = dconv[0:5]
    g["dn_a_log"] = dgprm[0, 16:32].reshape(2, NH)
    g["dn_dt_bias"] = dgprm[1, 16:32].reshape(2, NH)

    dqr, dkx, dvx, dsink = _attn_bwd_q(qr, kr, vv, sink, do_at, lse, delta, hi)
    dk_lat, dv_lat = _attn_bwd_kv(qr, kr, vv, do_at, lse, delta, hi)
    g["attn_sink"] = dsink[:, 0:NH]
    dq_at, g["q_norm"] = _attn_prep_bwd(dqr, p, w["q_norm"], cos, sin, D, C_QAT // D, CT, n, "attn_prep_q_bwd")
    dkr = jnp.concatenate([dkx, dk_lat], axis=0)
    dk_at, g["k_norm"] = _attn_prep_bwd(dkr, p, w["k_norm"], cos_all, sin_all, KVH * HD, C_KAT // (KVH * HD), 0, n_all,
                                        "attn_prep_k_bwd")
    dv_at = jnp.concatenate([dvx, dv_lat], axis=0).astype(BF16)

    zc = lambda width: jnp.zeros((CTX, width), BF16)
    dp = jnp.concatenate([
        dqkv,
        jnp.concatenate([zc(D), dgt], axis=0),
        jnp.concatenate([zc(D), dq_at], axis=0),
        jnp.concatenate([zc(2 * D), dmg], axis=0),
        dk_at, dv_at, dba, jnp.zeros((t_all, PW - C_PAD), BF16)], axis=1)
    dh = mm(dp, w["w_in_p"], tb=True, tm=tm_all, tn=1024, tk=2048, name="mm_in_dx")
    g["w_in_p"] = mm(h, dp, ta=True, tm=1024, tn=2048, tk=tm_all, name="mm_in_dw")
    dnm_c, dsh_ac, dsc_ac = _norm_mod_bwd(dh, ctx, None, nm, mod_ac, 0, CT)
    grad_x, dnm_x, dsh_a, dsc_a = _norm_mod_bwd(dh, x, dx1, nm, mod_ax, CT, n)
    g["norm_mix"] = dnm_c + dnm_x
    dmod_x = jnp.concatenate([dsh_a, dsc_a, dg_a, dsh_f, dsc_f, dg_f], axis=1)
    dmod_c = jnp.concatenate([dsh_ac, dsc_ac, jnp.zeros((1, 4 * D), F32)], axis=1)
    return loss, grad_x, g, dmod_x, dmod_c


def _sum_slots(buf, n_slots, rows, tile, name, stride=1):
    nt = rows // tile

    def fn(i, j, *vals):
        acc = vals[0]
        for v in vals[1:]:
            acc = acc + v
        return (acc,)

    ins = [_In(buf, roff=k * stride * nt) for k in range(n_slots)]
    return _rowcall(name, fn, nt, tile, ins, [_Out(buf.shape[1])])[0]


ADAM_LR, ADAM_B1, ADAM_B2, ADAM_EPS, ADAM_WD, ADAM_STEP = 0.001, 0.9, 0.999, 1e-08, 0.01, 10


def _row_tile(rows, cols):
    for t in (512, 256, 128, 64, 32, 16, 8):
        if rows % t == 0 and t * cols * 4 * 14 <= 40 * 1024 * 1024:
            return t
    return rows


def _adamw(w, g, m, v, name):
    shape = w.shape
    cols = shape[-1]
    rows = max(1, math.prod(shape[:-1]))
    tile = _row_tile(rows, cols)
    c1 = 1.0 / (1.0 - ADAM_B1 ** ADAM_STEP)
    c2 = 1.0 / (1.0 - ADAM_B2 ** ADAM_STEP)

    def fn(i, j, w_, g_, m_, v_):
        mn = ADAM_B1 * m_ + (1.0 - ADAM_B1) * g_
        vn = ADAM_B2 * v_ + (1.0 - ADAM_B2) * (g_ * g_)
        delta = -ADAM_LR * ((mn * c1) / (jnp.sqrt(vn * c2) + ADAM_EPS) + ADAM_WD * w_)
        return delta, mn, vn

    r2 = lambda a: a.reshape(rows, cols)
    outs = _rowcall(name, fn, rows // tile, tile, [_In(r2(w)), _In(r2(g)), _In(r2(m)), _In(r2(v))],
                    [_Out(cols), _Out(cols), _Out(cols)])
    return [o.reshape(shape) for o in outs]


MESH = pl.DeviceIdType.MESH
ANY = pl.BlockSpec(memory_space=pl.ANY)


def _pos():
    return lax.axis_index("x"), lax.axis_index("y"), lax.axis_index("c")


def _all_gather_many(blks, name):
    na = len(blks)

    def body(*refs):
        x_refs, out_refs = refs[:na], refs[na:2 * na]
        send_sems, recv_sems, local_sems = refs[2 * na:]
        x, y, c = _pos()
        me, sibling = (x, y, c), (x, y, 1 - c)
        chips = [(1 - x, y), (x, 1 - y), (1 - x, 1 - y)]

        def rows(a, px, py, pc):
            m_per = blks[a].shape[0]
            return out_refs[a].at[pl.ds(pl.multiple_of((4 * px + 2 * py + pc) * m_per, 8), m_per), :]

        def copy(a, k, block, to, src=None):
            return pltpu.make_async_remote_copy(
                src_ref=rows(a, *block) if src is None else src, dst_ref=rows(a, *block),
                send_sem=send_sems.at[7 * a + k], recv_sem=recv_sems.at[7 * a + k], device_id=to, device_id_type=MESH)

        every = range(na)
        mine = [pltpu.make_async_copy(x_refs[a], rows(a, *me), local_sems.at[a]) for a in every]
        for cp in mine:
            cp.start()
        first = [copy(a, 0, me, sibling, src=x_refs[a]) for a in every]
        first += [copy(a, 1 + j, me, (*chip, c), src=x_refs[a]) for j, chip in enumerate(chips) for a in every]
        for cp in first:
            cp.start()
        passed = []
        for j, chip in enumerate(chips):
            for a in every:
                copy(a, 1 + j, (*chip, c), me).wait_recv()
                passed.append(copy(a, 4 + j, (*chip, c), sibling))
                passed[-1].start()
        for a in every:
            copy(a, 0, sibling, me).wait_recv()
        for j, chip in enumerate(chips):
            for a in every:
                copy(a, 4 + j, (*chip, 1 - c), me).wait_recv()
        for cp in first + passed:
            cp.wait_send()
        for cp in mine:
            cp.wait()

    return pl.pallas_call(
        body, name=name,
        out_shape=[jax.ShapeDtypeStruct((N_DEV * b.shape[0], b.shape[1]), b.dtype) for b in blks],
        in_specs=[ANY] * na, out_specs=[ANY] * na,
        scratch_shapes=[pltpu.SemaphoreType.DMA((7 * na,)), pltpu.SemaphoreType.DMA((7 * na,)), pltpu.SemaphoreType.DMA((na,))],
        compiler_params=pltpu.CompilerParams(has_side_effects=True),
    )(*blks)


def _all_gather(blk, name):
    return _all_gather_many([blk], name)[0]


def _flip(v, bit):
    return 1 - v if bit else v


D2D_STREAMS = 8
ICI_STREAMS = 2


def _sibling_exchange(src, seg_rows, n_seg, paired, name):
    n = src.shape[1]
    per_seg = D2D_STREAMS // n_seg
    per = seg_rows // per_seg
    assert per_seg * n_seg == D2D_STREAMS and per * per_seg == seg_rows and per % 16 == 0

    def body(x_ref, out_ref, send_sems, recv_sems):
        x, y, c = _pos()
        copies = []
        for s in range(n_seg):
            base = (2 * s + (1 - c)) * seg_rows if paired else s * seg_rows
            for j in range(per_seg):
                i = s * per_seg + j
                cp = pltpu.make_async_remote_copy(
                    src_ref=x_ref.at[pl.ds(pl.multiple_of(base + j * per, 16), per), :],
                    dst_ref=out_ref.at[pl.ds(s * seg_rows + j * per, per), :],
                    send_sem=send_sems.at[i], recv_sem=recv_sems.at[i], device_id=(x, y, 1 - c), device_id_type=MESH)
                cp.start()
                copies.append(cp)
        for cp in copies:
            cp.wait_recv()
        for cp in copies:
            cp.wait_send()

    return pl.pallas_call(
        body, name=name, out_shape=jax.ShapeDtypeStruct((n_seg * seg_rows, n), src.dtype),
        in_specs=[ANY], out_specs=ANY,
        scratch_shapes=[pltpu.SemaphoreType.DMA((D2D_STREAMS,)), pltpu.SemaphoreType.DMA((D2D_STREAMS,))],
        compiler_params=pltpu.CompilerParams(has_side_effects=True),
    )(src)


def _transpose_cast(x, dtype, name):
    r, c = x.shape
    tc = 512

    def body(x_ref, o_ref):
        o_ref[...] = x_ref[...].T.astype(o_ref.dtype)

    return pl.pallas_call(
        body, name=name, grid=(c // tc,),
        in_specs=[pl.BlockSpec((r, tc), lambda j: (0, j))], out_specs=pl.BlockSpec((tc, r), lambda j: (j, 0)),
        out_shape=jax.ShapeDtypeStruct((c, r), dtype), compiler_params=_cparams(("parallel",)),
    )(x)


def _chip_exchange(buf, rows, name):
    n = buf.shape[1]
    per = rows // ICI_STREAMS
    assert per * ICI_STREAMS == rows and per % 16 == 0

    def body(x_ref, out_ref, send_sems, recv_sems):
        x, y, c = _pos()
        copies = []
        for k in range(1, 4):
            px, py = _flip(x, k & 2), _flip(y, k & 1)
            for j in range(ICI_STREAMS):
                i = (k - 1) * ICI_STREAMS + j
                cp = pltpu.make_async_remote_copy(
                    src_ref=x_ref.at[pl.ds(pl.multiple_of((2 * px + py) * rows + j * per, 16), per), :],
                    dst_ref=out_ref.at[pl.ds((k - 1) * rows + j * per, per), :],
                    send_sem=send_sems.at[i], recv_sem=recv_sems.at[i], device_id=(px, py, c), device_id_type=MESH)
                cp.start()
                copies.append(cp)
        for cp in copies:
            cp.wait_recv()
        for cp in copies:
            cp.wait_send()

    return pl.pallas_call(
        body, name=name, out_shape=jax.ShapeDtypeStruct((3 * rows, n), buf.dtype),
        in_specs=[ANY], out_specs=ANY,
        scratch_shapes=[pltpu.SemaphoreType.DMA((3 * ICI_STREAMS,)), pltpu.SemaphoreType.DMA((3 * ICI_STREAMS,))],
        compiler_params=pltpu.CompilerParams(has_side_effects=True),
    )(buf)


def _add_rows(parts, rows, dtype, name):
    tile = 1024
    ins = [_In(a, roff=r0 // tile) for a, r0 in parts]

    def fn(i, j, *vals):
        acc = vals[0].astype(F32)
        for v_ in vals[1:]:
            acc = acc + v_.astype(F32)
        return (acc,)

    return _rowcall(name, fn, rows // tile, tile, ins, [_Out(parts[0][0].shape[1], dtype)])[0]


BIG = ("w_in", "w_branch_dn", "w_branch_attn", "w_out", "ffn_up", "ffn_down")
BIG_SHARD = {"w_in": (1024, 1928, True), "w_branch_dn": (256, 1024, False), "w_branch_attn": (256, 1024, False),
             "w_out": (256, 1024, False), "ffn_up": (1024, 1408, True), "ffn_down": (704, 1024, False)}
BIG_ROWS = {k: r * c // 2 // 128 for k, (r, c, _) in BIG_SHARD.items()}
PIECE = 19456
assert sum(BIG_ROWS.values()) <= PIECE


def _gather_weights(shards, ci):
    halves = []
    for k in BIG:
        r, c, _ = BIG_SHARD[k]
        halves.append(lax.dynamic_slice_in_dim(shards[k], ci * (r // 2), r // 2, axis=0).astype(BF16))
    out = {}
    for k, ag in zip(BIG, _all_gather_many(halves, "ag_weights")):
        r, c, by_col = BIG_SHARD[k]
        blk = ag.reshape(4, r, c)
        out[k] = jnp.transpose(blk, (1, 0, 2)).reshape(r, 4 * c) if by_col else blk.reshape(4 * r, c)
    return out


def _pack_pieces(full):
    parts = [full["w_in_t"].reshape(N_DEV, BIG_ROWS["w_in"], 128).astype(BF16)]
    for k in BIG[1:]:
        r, c, by_col = BIG_SHARD[k]
        a = full[k]
        if by_col:
            a = jnp.transpose(a.reshape(r, 4, c), (1, 0, 2))
        parts.append(a.reshape(N_DEV, BIG_ROWS[k], 128).astype(BF16))
    parts.append(jnp.zeros((N_DEV, PIECE - sum(BIG_ROWS.values()), 128), BF16))
    return jnp.concatenate(parts, axis=1).reshape(N_DEV * PIECE, 128)


def _reduce_scatter(pieces, ci, shard):
    half = N_DEV // 2 * PIECE
    theirs = _sibling_exchange(pieces, PIECE, N_DEV // 2, True, "rs_d2d")
    own = lax.dynamic_index_in_dim(pieces.reshape(N_DEV // 2, 2, PIECE, 128), ci, axis=1, keepdims=False).reshape(half, 128)
    part = _add_rows([(own, 0), (theirs, 0)], half, BF16, "rs_sum_chip")
    recv = _chip_exchange(part, PIECE, "rs_ici")
    own2 = lax.dynamic_slice_in_dim(part, shard * PIECE, PIECE, axis=0)
    mine = _add_rows([(own2, 0), (recv, 0), (recv, PIECE), (recv, 2 * PIECE)], PIECE, F32, "rs_sum_all")
    other = _sibling_exchange(mine, PIECE, 1, False, "rs_pair")
    return jnp.where(ci == 0, jnp.stack([mine, other]), jnp.stack([other, mine]))


def _unpack_shard(two):
    out, off = {}, 0
    for k in BIG:
        r, c, _ = BIG_SHARD[k]
        blk = two[:, off:off + BIG_ROWS[k]]
        out[k] = blk.reshape(c, r).T if k == "w_in" else blk.reshape(r, c)
        off += BIG_ROWS[k]
    return out


SMALL = (("dn_conv", 120), ("ffn_conv", 132), ("ffn_conv_b", 44), ("norm_mix", 8), ("norm_ffn", 8), ("dn_a_log", 1),
         ("dn_dt_bias", 1), ("dn_norm", 1), ("q_norm", 1), ("k_norm", 1), ("attn_sink", 1), ("dmod_c", 48), ("dmod_x", 48))
SMALL_ROWS = 416


def _rows128(a, rows):
    flat = a.reshape(-1)
    return jnp.concatenate([flat, jnp.zeros((rows * 128 - flat.shape[0],), F32)]).reshape(rows, 128)


def _pack_small(g):
    parts = [_rows128(g[k], r) for k, r in SMALL]
    parts.append(jnp.zeros((SMALL_ROWS - sum(r for _, r in SMALL), 128), F32))
    return jnp.concatenate(parts, axis=0)


def _unpack_small(buf, shapes):
    out, off = {}, 0
    for k, r in SMALL:
        n = math.prod(shapes[k])
        out[k] = buf[off:off + r].reshape(-1)[:n].reshape(shapes[k])
        off += r
    return out


WEIGHTS = ("c_ctx", "w_ada", "b_ada", "norm_mix", "norm_ffn", "w_in", "dn_conv", "dn_a_log", "dn_dt_bias", "dn_norm",
           "q_norm", "k_norm", "attn_sink", "w_branch_dn", "w_branch_attn", "w_out", "ffn_up", "ffn_conv", "ffn_conv_b",
           "ffn_down")


def kernel(x, c, ctx, c_ctx, w_ada, b_ada, norm_mix, norm_ffn, w_in, dn_conv, dn_a_log, dn_dt_bias, dn_norm, q_norm, k_norm, attn_sink, w_branch_dn, w_branch_attn, w_out, ffn_up, ffn_conv, ffn_conv_b, ffn_down, loss_target, m_c_ctx, m_w_ada, m_b_ada, m_norm_mix, m_norm_ffn, m_w_in, m_dn_conv, m_dn_a_log, m_dn_dt_bias, m_dn_norm, m_q_norm, m_k_norm, m_attn_sink, m_w_branch_dn, m_w_branch_attn, m_w_out, m_ffn_up, m_ffn_conv, m_ffn_conv_b, m_ffn_down, v_c_ctx, v_w_ada, v_b_ada, v_norm_mix, v_norm_ffn, v_w_in, v_dn_conv, v_dn_a_log, v_dn_dt_bias, v_dn_norm, v_q_norm, v_k_norm, v_attn_sink, v_w_branch_dn, v_w_branch_attn, v_w_out, v_ffn_up, v_ffn_conv, v_ffn_conv_b, v_ffn_down):
    args = dict(locals())
    xi, yi, ci = _pos()
    dev = 4 * xi + 2 * yi + ci
    shard = 2 * xi + yi
    chips = lambda a: a[0::2]

    blk = jnp.concatenate([_rows128(c, 8), _rows128(dn_conv, 30), _rows128(ffn_conv, 33), jnp.zeros((1, 128), F32)], axis=0)
    ag = _all_gather(blk, "ag_small_in").reshape(N_DEV, 72, 128)
    c_all = ag[:, 0:8].reshape(N_DEV, D)
    dn_conv_full = jnp.transpose(chips(ag)[:, 8:38].reshape(4, 5, 768), (1, 0, 2)).reshape(5, 3 * D)
    ffn_conv_full = jnp.transpose(chips(ag)[:, 38:71].reshape(4, 3, 1408), (1, 0, 2)).reshape(3, 2 * DFF)

    c16 = jnp.concatenate([c_all, c_ctx[None], jnp.zeros((7, D), F32)], axis=0)
    a16 = _rowcall("ada_silu", lambda i, j, v: (_silu(v),), 1, 16, [_In(c16)], [_Out(D)])[0]
    m_sh = _mm(a16, w_ada[0], tm=16, tn=512, tk=D, name="ada_fwd", hi=True)
    mod16 = chips(_all_gather(m_sh, "ag_mod").reshape(N_DEV, 16, 1536))
    mod16 = jnp.transpose(mod16, (1, 0, 2)).reshape(16, 6 * D) + b_ada
    mod_x = lax.dynamic_slice_in_dim(mod16, dev, 1, axis=0)
    mod_c = mod16[8:9]

    shards = {k: args[k][0] for k in BIG}
    wfull = _gather_weights(shards, ci)
    w = dict(wfull)
    w["w_in_p"] = _pad_w_in(wfull["w_in"])
    w.update(norm_mix=norm_mix, norm_ffn=norm_ffn, dn_conv=dn_conv_full, dn_a_log=dn_a_log[0], dn_dt_bias=dn_dt_bias[0],
             dn_norm=dn_norm, q_norm=q_norm, k_norm=k_norm, attn_sink=attn_sink, ffn_conv=ffn_conv_full, ffn_conv_b=ffn_conv_b)

    loss_part, grad_x, g, dmod_x, dmod_c = _local_step(x[0], ctx[0], loss_target[0], mod_x, mod_c, w)
    loss = lax.psum(loss_part[0, 0], ("x", "y", "c"))

    g["w_in_t"] = _unpad_w_in(_transpose_cast(g["w_in_p"], BF16, "w_in_grad_t"), axis=0)
    gshard = _unpack_shard(_reduce_scatter(_pack_pieces(g), ci, shard))

    g["dmod_c"], g["dmod_x"] = dmod_c, dmod_x
    ag_s = _all_gather(_pack_small(g), "ag_small_grads")
    shapes = {k: g[k].shape for k, _ in SMALL}
    gs = _unpack_small(_sum_slots(ag_s, N_DEV, SMALL_ROWS, SMALL_ROWS, "small_sum"), shapes)
    dx_all = ag_s.reshape(N_DEV, SMALL_ROWS, 128)[:, SMALL_ROWS - 50:SMALL_ROWS - 2].reshape(N_DEV, 6 * D)

    d16 = jnp.concatenate([dx_all, gs["dmod_c"], jnp.zeros((7, 6 * D), F32)], axis=0)
    d16_sh = lax.dynamic_slice_in_dim(d16, shard * 1536, 1536, axis=1)
    g_w_ada = _mm(a16, d16_sh, ta=True, tm=D, tn=512, tk=16, name="ada_dw", hi=True)
    g_b_ada = _rowcall("ada_db", lambda i, j, v: (_colsum(v),), 1, 16, [_In(d16)], [_Out(6 * D, acc=True)])[0]
    da_part = _mm(d16_sh, w_ada[0], tb=True, tm=16, tn=D, tk=512, name="ada_dx", hi=True)
    da_all = _all_gather(da_part, "ag_ada_dx")
    da16 = _sum_slots(da_all, 4, 16, 16, "ada_dx_sum", stride=2)
    dc16 = _rowcall("ada_dsilu", lambda i, j, d_, v: (d_ * _dsilu(v),), 1, 16, [_In(da16), _In(c16)], [_Out(D)])[0]

    grads = {
        "c_ctx": dc16[8], "w_ada": g_w_ada[None], "b_ada": g_b_ada, "norm_mix": gs["norm_mix"], "norm_ffn": gs["norm_ffn"],
        "w_in": gshard["w_in"][None],
        "dn_conv": lax.dynamic_slice_in_dim(gs["dn_conv"], shard * 768, 768, axis=1)[None],
        "dn_a_log": gs["dn_a_log"][None], "dn_dt_bias": gs["dn_dt_bias"][None], "dn_norm": gs["dn_norm"],
        "q_norm": gs["q_norm"], "k_norm": gs["k_norm"], "attn_sink": gs["attn_sink"],
        "w_branch_dn": gshard["w_branch_dn"][None], "w_branch_attn": gshard["w_branch_attn"][None],
        "w_out": gshard["w_out"][None], "ffn_up": gshard["ffn_up"][None],
        "ffn_conv": lax.dynamic_slice_in_dim(gs["ffn_conv"], shard * 1408, 1408, axis=1)[None],
        "ffn_conv_b": gs["ffn_conv_b"], "ffn_down": gshard["ffn_down"][None],
    }
    deltas, new_m, new_v = [], [], []
    for k in WEIGHTS:
        d_, m_, v_ = _adamw(args[k], grads[k], args["m_" + k], args["v_" + k], "adamw_" + k)
        deltas.append(d_)
        new_m.append(m_)
        new_v.append(v_)
    return (loss, grad_x[None], *[grads[k] for k in WEIGHTS], *deltas, *new_m, *new_v)
```

```python
import functools
import math

import numpy as np
import jax
import jax.numpy as jnp
from jax import lax
from jax.experimental import pallas as pl
from jax.experimental.pallas import tpu as pltpu

F32 = jnp.float32
BF16 = jnp.bfloat16
HI = lax.Precision.HIGHEST

D = 1024
NH = 8
HD = 128
CH = 64
CTX = 256
AB = 128
KVH = 2
GRP = 4
DFF = 2816
EPS = 1e-6
GRID_W = 64
ROPE_BASE = 10000.0
N_DEV = 8
VMEM_LIMIT = 56 * 1024 * 1024

C_QKV, C_KAT, C_VAT, C_BA, C_PAD, C_GT, C_QAT, C_MG = 0, 3072, 3328, 3584, 3712, 4096, 5120, 6144
PW = 8192
PH = PW // 2


def _cparams(sem=None, **kw):
    return pltpu.CompilerParams(dimension_semantics=sem, vmem_limit_bytes=VMEM_LIMIT, **kw)


def _dot(a, b, dims, hi):
    if hi:
        return lax.dot_general(a.astype(F32), b.astype(F32), (dims, ((), ())), precision=HI, preferred_element_type=F32)
    return lax.dot_general(a.astype(BF16), b.astype(BF16), (dims, ((), ())), preferred_element_type=F32)


NN = ((1,), (0,))
NT = ((1,), (1,))
TN = ((0,), (0,))


def _dn_masks():
    i = np.arange(CH)
    lo_incl = (i[:, None] >= i[None, :]).astype(np.float32)
    lo_strict = (i[:, None] > i[None, :]).astype(np.float32)
    return jnp.asarray(np.stack([np.stack([lo_incl, lo_strict]), np.stack([lo_incl.T, lo_strict.T])]))


def _dn_chunk_index(d, i, n_ctx_chunks, n_chunks):
    fwd = i
    bwd = jnp.where(i < n_ctx_chunks, n_ctx_chunks - 1 - i, n_chunks - 1 + n_ctx_chunks - i)
    return jnp.where(d == 0, fwd, bwd)


BNN = ((2,), (1,))
BNT = ((2,), (2,))
BTN = ((1,), (1,))


def _bdot(a, b, dims, hi):
    dn = (dims, ((0,), (0,)))
    if hi:
        return lax.dot_general(a.astype(F32), b.astype(F32), dn, precision=HI, preferred_element_type=F32)
    return lax.dot_general(a.astype(BF16), b.astype(BF16), dn, preferred_element_type=F32)


def _bdot3(a, b, dims, hi):
    if hi:
        return _bdot(a, b, dims, True)
    ah, bh = a.astype(BF16), b.astype(BF16)
    al, bl = (a - ah.astype(F32)).astype(BF16), (b - bh.astype(F32)).astype(BF16)
    dn = (dims, ((0,), (0,)))
    d = lambda x_, y_: lax.dot_general(x_, y_, dn, preferred_element_type=F32)
    return d(ah, bh) + d(ah, bl) + d(al, bh)


DN_CB = 4
DN_SEQ_CB = 4


def _dn_heads(ref, cb=1):
    return jnp.stack([ref[t * CH:(t + 1) * CH, h * HD:(h + 1) * HD] for t in range(cb) for h in range(NH)])


def _dn_scalars(gb, mi, cb=1):
    beta, gc, gcr, gt = [], [], [], []
    for t in range(cb):
        g1 = gb[t * CH:(t + 1) * CH]
        gcum, gcum_t, gtot = _dn_gcum(g1, mi)
        beta += [g1[:, h:h + 1] for h in range(NH)]
        gc += [gcum[:, NH + h:NH + h + 1] for h in range(NH)]
        gcr += [gcum_t[NH + h:NH + h + 1, :] for h in range(NH)]
        gt += [gtot[:, NH + h:NH + h + 1] for h in range(NH)]
    return jnp.stack(beta), jnp.stack(gc), jnp.stack(gcr), jnp.stack(gt)


DN_NEWTON = 1


def _dn_inverse(a, hi):
    eye = (lax.broadcasted_iota(jnp.int32, (CH, CH), 0) == lax.broadcasted_iota(jnp.int32, (CH, CH), 1)).astype(F32)
    x = -a
    t = eye + x
    p = x
    if hi:
        for _ in range(5):
            p = _bdot(p, p, BNN, True)
            t = t + _bdot(t, p, BNN, True)
        return t
    for _ in range(5):
        p = _bdot(p, p, BNN, False)
        t = t + _bdot(t, p, BNN, False)
    for _ in range(DN_NEWTON):
        r = eye - t - _bdot3(a, t, BNN, False)
        t = t + _bdot(t, r, BNN, False)
    return t


def _dn_total(gb):
    gtot = jnp.sum(gb, axis=0, keepdims=True)
    return jnp.stack([gtot[:, NH + h:NH + h + 1] for h in range(NH)])


def _dn_gcum(gb, mi):
    gcum = _dot(mi, gb, NN, True)
    gtot = jnp.sum(gb, axis=0, keepdims=True)
    return gcum, gcum.T, gtot


def _dn_specs(n_ctx_chunks, n_chunks, reverse, cb):
    assert n_ctx_chunks % cb == 0 and n_chunks % cb == 0

    def grp(d, i):
        first = n_chunks - 1 - cb * i if reverse else cb * i
        return _dn_chunk_index(d, first, n_ctx_chunks, n_chunks) // cb

    def slot(d, t):
        ascending = (d == 1) if reverse else (d == 0)
        return jnp.where(ascending, t, cb - 1 - t)

    ctx_groups = n_ctx_chunks // cb
    tok_lat = pl.BlockSpec((cb * CH, D), lambda d, i: (jnp.maximum(grp(d, i) - ctx_groups, 0), 0))
    is_ctx = lambda d, i: grp(d, i) < ctx_groups
    tok_d = pl.BlockSpec((1, cb * CH, D), lambda d, i: (d, grp(d, i), 0))
    gbs = pl.BlockSpec((1, cb * CH, 128), lambda d, i: (d, grp(d, i), 0))

    def per_chunk(*tail):
        return pl.BlockSpec((1, cb) + tail, lambda d, i: (d, grp(d, i)) + (0,) * len(tail))

    return tok_lat, is_ctx, tok_d, gbs, per_chunk, slot


def _dn_group_specs(cb):
    tok = pl.BlockSpec((cb * CH, D), lambda d, i: (i, 0))
    tok_d = pl.BlockSpec((1, cb * CH, D), lambda d, i: (d, i, 0))
    gbs = pl.BlockSpec((1, cb * CH, 128), lambda d, i: (d, i, 0))
    msk = pl.BlockSpec((1, 2, CH, CH), lambda d, i: (d, 0, 0, 0))

    def per_chunk(*tail):
        return pl.BlockSpec((1, cb) + tail, lambda d, i: (d, i) + (0,) * len(tail))

    return tok, tok_d, gbs, msk, per_chunk


def _dn_intra_fwd(q, k, v, gb, n_ctx_chunks, hi):
    t_all = q.shape[0]
    n_chunks = t_all // CH
    masks = _dn_masks()

    cb = DN_CB

    def put(ref, val):
        for t_ in range(cb):
            ref[0, t_] = val[t_ * NH:(t_ + 1) * NH].astype(ref.dtype)

    def body(q_ref, k_ref, v_ref, gb_ref, m_ref, u_ref, w_ref, qg_ref, kd_ref, pm_ref, t_ref):
        mi, ms = m_ref[0, 0], m_ref[0, 1]
        beta, gc, gcr, gt = _dn_scalars(gb_ref[0], mi, cb)
        q_, k_, v_ = _dn_heads(q_ref, cb), _dn_heads(k_ref, cb), _dn_heads(v_ref, cb)
        decay = jnp.exp(jnp.where(mi > 0, gc - gcr, 0.0)) * mi
        e = jnp.exp(gc)
        a = ms * (beta * _bdot(k_, k_, BNT, hi) * decay)
        t = _dn_inverse(a, hi)
        uw =_bdot(t, jnp.concatenate([beta * v_, (beta * e) * k_], axis=2), BNN, hi)
        put(u_ref, uw[:, :, :HD])
        put(w_ref, uw[:, :, HD:])
        put(qg_ref, e * q_)
        put(kd_ref, jnp.exp(gt - gc) * k_)
        put(pm_ref, _bdot(q_, k_, BNT, hi) * decay)
        put(t_ref, t)

    tok, _, gbs, msk, per_chunk = _dn_group_specs(cb)
    big = lambda dt: jax.ShapeDtypeStruct((2, n_chunks, NH, CH, HD), dt)
    sq = jax.ShapeDtypeStruct((2, n_chunks, NH, CH, CH), BF16)
    return pl.pallas_call(
        body, name="dn_intra_fwd", grid=(2, n_chunks // cb),
        in_specs=[tok, tok, tok, gbs, msk],
        out_specs=[per_chunk(NH, CH, HD)] * 4 + [per_chunk(NH, CH, CH)] * 2,
        out_shape=[big(F32), big(BF16), big(BF16), big(BF16), sq, sq],
        compiler_params=_cparams(("parallel", "parallel")),
    )(q, k, v, gb, masks)


def _dn_seq_fwd(u, w, qg, kd, pm, gb, n_ctx_chunks, hi):
    n_chunks = u.shape[1]
    t_all = n_chunks * CH

    cb = DN_SEQ_CB
    _, _, tok_d, gbs, per_chunk, slot = _dn_specs(n_ctx_chunks, n_chunks, False, cb)

    def body(u_ref, w_ref, qg_ref, kd_ref, pm_ref, gb_ref, o_ref, sh_ref, vn_ref, s_scr):
        @pl.when(pl.program_id(1) == 0)
        def _():
            s_scr[...] = jnp.zeros_like(s_scr)

        for t in range(cb):
            j = slot(pl.program_id(0), t)
            rows = pl.ds(pl.multiple_of(j * CH, CH), CH)
            s = s_scr[...]
            sh_ref[0, j] = s.astype(sh_ref.dtype)
            vn = u_ref[0, j] - _bdot(w_ref[0, j], s, BNN, hi)
            o = _bdot(qg_ref[0, j], s, BNN, hi) + _bdot(pm_ref[0, j], vn, BNN, hi)
            s_scr[...] = jnp.exp(_dn_total(gb_ref[0, rows, :])) * s + _bdot(kd_ref[0, j], vn, BTN, hi)
            vn_ref[0, j] = vn.astype(vn_ref.dtype)
            for h in range(NH):
                o_ref[0, rows, h * HD:(h + 1) * HD] = o[h]

    big = per_chunk(NH, CH, HD)
    return pl.pallas_call(
        body, name="dn_seq_fwd", grid=(2, n_chunks // cb),
        in_specs=[big, big, big, big, per_chunk(NH, CH, CH), gbs],
        out_specs=[tok_d, per_chunk(NH, HD, HD), big],
        out_shape=[jax.ShapeDtypeStruct((2, t_all, D), F32), jax.ShapeDtypeStruct((2, n_chunks, NH, HD, HD), BF16),
                   jax.ShapeDtypeStruct((2, n_chunks, NH, CH, HD), BF16)],
        scratch_shapes=[pltpu.VMEM((NH, HD, HD), F32)],
        compiler_params=_cparams(("parallel", "arbitrary")),
    )(u, w, qg, kd, pm, gb)


def _dn_seq_bwd(w, qg, kd, pm, vn, s_hist, gb, do, n_ctx_chunks, hi):
    n_chunks = w.shape[1]

    cb = DN_SEQ_CB
    tok_lat, is_ctx, _, gbs, per_chunk, slot = _dn_specs(n_ctx_chunks, n_chunks, True, cb)

    def body(w_ref, qg_ref, kd_ref, pm_ref, vn_ref, sh_ref, gb_ref, do_ref, dvn_ref, dw_ref, dqg_ref, dkd_ref, del_ref, ds_scr):
        @pl.when(pl.program_id(1) == 0)
        def _():
            ds_scr[...] = jnp.zeros_like(ds_scr)

        for t in range(cb):
            j = slot(pl.program_id(0), t)
            rows = pl.ds(pl.multiple_of(j * CH, CH), CH)
            dsn = ds_scr[...]
            s = sh_ref[0, j]
            do_ = jnp.stack([do_ref[rows, h * HD:(h + 1) * HD] for h in range(NH)])
            do_ = jnp.where(is_ctx(pl.program_id(0), pl.program_id(1)), 0.0, do_)
            dvn =_bdot(pm_ref[0, j], do_, BTN, hi) + _bdot(kd_ref[0, j], dsn, BNN, hi)
            ds_scr[...] = (_bdot(qg_ref[0, j], do_, BTN, hi) + jnp.exp(_dn_total(gb_ref[0, rows, :])) * dsn
                           - _bdot(w_ref[0, j], dvn, BTN, hi))
            dvn_ref[0, j] = dvn.astype(dvn_ref.dtype)
            dw_ref[0, j] = (-_bdot(dvn, s, BNT, hi)).astype(dw_ref.dtype)
            dqg_ref[0, j] = _bdot(do_, s, BNT, hi)
            dkd_ref[0, j] = _bdot(vn_ref[0, j], dsn, BNT, hi)
            del_ref[0, j] = jnp.broadcast_to(jnp.sum(jnp.sum(s * dsn, axis=2, keepdims=True), axis=1, keepdims=True),
                                             (NH, 1, 128))

    big = per_chunk(NH, CH, HD)
    shp = lambda dt: jax.ShapeDtypeStruct((2, n_chunks, NH, CH, HD), dt)
    return pl.pallas_call(
        body, name="dn_seq_bwd", grid=(2, n_chunks // cb),
        in_specs=[big, big, big, per_chunk(NH, CH, CH), big, per_chunk(NH, HD, HD), gbs, tok_lat],
        out_specs=[big, big, big, big, per_chunk(NH, 1, 128)],
        out_shape=[shp(BF16), shp(BF16), shp(F32), shp(F32), jax.ShapeDtypeStruct((2, n_chunks, NH, 1, 128), F32)],
        scratch_shapes=[pltpu.VMEM((NH, HD, HD), F32)],
        compiler_params=_cparams(("parallel", "arbitrary")),
    )(w, qg, kd, pm, vn, s_hist, gb, do)


def _dn_intra_bwd(q, k, v, gb, u, w, t, vn, dvn, dw, dqg, dkd, de_last, do, n_ctx_chunks, hi):
    t_all = q.shape[0]
    n_chunks = t_all // CH
    masks = _dn_masks()

    cb = DN_CB
    assert n_ctx_chunks % cb == 0
    ctx_groups = n_ctx_chunks // cb

    def body(q_ref, k_ref, v_ref, gb_ref, m_ref, u_ref, w_ref, t_ref, vn_ref, dvn_ref, dw_ref, dqg_ref, dkd_ref, del_ref,
             do_ref, dq_ref, dk_ref, dv_ref, dgb_ref):
        mi, ms = m_ref[0, 0], m_ref[0, 1]
        beta, gc, gcr, gt = _dn_scalars(gb_ref[0], mi, cb)
        q_, k_, v_ = _dn_heads(q_ref, cb), _dn_heads(k_ref, cb), _dn_heads(v_ref, cb)
        do_ = jnp.where(pl.program_id(1) < ctx_groups, 0.0, _dn_heads(do_ref, cb))
        get = lambda ref: jnp.concatenate([ref[0, t_] for t_ in range(cb)], axis=0)
        decay = jnp.exp(jnp.where(mi > 0, gc - gcr, 0.0)) * mi
        e = jnp.exp(gc)
        e_last = jnp.exp(gt)
        kdfac = jnp.exp(gt - gc)
        kk = _bdot(k_, k_, BNT, hi)
        a = ms * (beta * kk * decay)
        pm = _bdot(q_, k_, BNT, hi) * decay
        kd = kdfac * k_
        dqg, dkd = get(dqg_ref), get(dkd_ref)
        dpm = _bdot(do_, get(vn_ref), BNT, hi)
        dvbkb = _bdot(get(t_ref), jnp.concatenate([get(dvn_ref), get(dw_ref)], axis=2), BTN, hi)
        dvb, dkb = dvbkb[:, :, :HD], dvbkb[:, :, HD:]
        da = -ms * _bdot(dvbkb, jnp.concatenate([get(u_ref), get(w_ref).astype(F32)], axis=2), BNT, hi)
        dqk = dpm * decay
        gm = dpm * pm + da * a
        dgc = (jnp.sum(gm, axis=2, keepdims=True)
               - _bdot3(gm, jnp.ones((cb * NH, CH, 128), F32), BTN, hi)[:, :, 0:1])
        dkk = da * (beta * decay)
        dbeta = jnp.sum(da * kk * decay, axis=2, keepdims=True)
        dk = _bdot(dkk, k_, BNN, hi) + _bdot(dkk, k_, BTN, hi) + _bdot(dqk, q_, BTN, hi)
        dq = _bdot(dqk, k_, BNN, hi) + e * dqg
        de = jnp.sum(dqg * q_, axis=2, keepdims=True)
        dv = beta * dvb
        dbeta = dbeta + jnp.sum(dvb * v_, axis=2, keepdims=True)
        skb = jnp.sum(dkb * k_, axis=2, keepdims=True)
        dk = dk + (beta * e) * dkb + kdfac * dkd
        dbeta = dbeta + e * skb
        de = de + beta * skb
        skd = jnp.sum(dkd * kd, axis=2, keepdims=True)
        dgc = dgc - skd + de * e
        dgtot = jnp.sum(skd, axis=1, keepdims=True) + get(del_ref)[:, :, 0:1] * e_last
        lane = lax.broadcasted_iota(jnp.int32, (1, 128), 1)
        for t_ in range(cb):
            rows = slice(t_ * CH, (t_ + 1) * CH)
            dbeta_all = jnp.zeros((CH, 128), F32)
            dgc_all = jnp.zeros((CH, 128), F32)
            dgtot_all = jnp.zeros((1, 128), F32)
            for h in range(NH):
                sl = slice(h * HD, (h + 1) * HD)
                b = t_ * NH + h
                dq_ref[0, rows, sl] = dq[b]
                dk_ref[0, rows, sl] = dk[b]
                dv_ref[0, rows, sl] = dv[b]
                hot_b = (lane == h).astype(F32)
                hot_g = (lane == NH + h).astype(F32)
                dbeta_all = dbeta_all + dbeta[b] * hot_b
                dgc_all = dgc_all + dgc[b] * hot_g
                dgtot_all = dgtot_all + dgtot[b] * hot_g
            dgb_ref[0, rows, :] = dbeta_all + _dot(mi, dgc_all, TN, True) + dgtot_all

    tok, tok_d, gbs, msk, per_chunk = _dn_group_specs(cb)
    tok_lat = pl.BlockSpec((cb * CH, D), lambda d, i: (jnp.maximum(i - ctx_groups, 0), 0))
    big = per_chunk(NH, CH, HD)
    return pl.pallas_call(
        body, name="dn_intra_bwd", grid=(2, n_chunks // cb),
        in_specs=[tok, tok, tok, gbs, msk, big, big, per_chunk(NH, CH, CH), big, big, big, big, big,
                  per_chunk(NH, 1, 128), tok_lat],
        out_specs=[tok_d, tok_d, tok_d, gbs],
        out_shape=[jax.ShapeDtypeStruct((2, t_all, D), F32)] * 3 + [jax.ShapeDtypeStruct((2, t_all, 128), F32)],
        compiler_params=_cparams(("parallel", "parallel")),
    )(q, k, v, gb, masks, u, w, t, vn, dvn, dw, dqg, dkd, de_last, do)


ATT_SCALE = HD ** -0.5
NEG = -1e30


def _att_stack(ref, kvh):
    return jnp.concatenate([ref[:, (kvh * GRP + g) * HD:(kvh * GRP + g + 1) * HD] for g in range(GRP)], axis=0)


def _att_col(ref, kvh):
    return jnp.concatenate([ref[:, kvh * GRP + g:kvh * GRP + g + 1] for g in range(GRP)], axis=0)


def _att_sink(sink_ref, kvh):
    return jnp.concatenate([jnp.broadcast_to(sink_ref[:, kvh * GRP + g:kvh * GRP + g + 1], (AB, 1)) for g in range(GRP)],
                           axis=0)


def _att_mask(i, nb):
    r = lax.broadcasted_iota(jnp.int32, (AB, AB), 0)
    c = lax.broadcasted_iota(jnp.int32, (AB, AB), 1)
    okp = jnp.logical_and(c >= r, i > 0)
    okn = jnp.logical_and(c <= r, i < nb - 1)
    return jnp.concatenate([okp] * GRP, axis=0), jnp.concatenate([okn] * GRP, axis=0)


def _att_masked(s, mask):
    mp, mn = mask
    return jnp.concatenate([jnp.where(mp, s[:, 0:AB], NEG), s[:, AB:2 * AB], jnp.where(mn, s[:, 2 * AB:3 * AB], NEG),
                            s[:, 3 * AB:]], axis=1)


def _att_kspecs(nb):
    nc = CTX // AB
    return [pl.BlockSpec((AB, KVH * HD), lambda i: (jnp.maximum(i - 1, 0) + nc, 0)),
            pl.BlockSpec((AB, KVH * HD), lambda i: (i + nc, 0)),
            pl.BlockSpec((AB, KVH * HD), lambda i: (jnp.minimum(i + 1, nb - 1) + nc, 0)),
            pl.BlockSpec((CTX, KVH * HD), lambda i: (0, 0))]


def _attn_fwd(qr, kr, vv, sink, hi):
    tl = qr.shape[0]
    nb = tl // AB

    def body(q_ref, kp_ref, kc_ref, kn_ref, kx_ref, vp_ref, vc_ref, vn_ref, vx_ref, sink_ref, o_ref, lse_ref):
        i = pl.program_id(0)
        mask = _att_mask(i, nb)
        lane = lax.broadcasted_iota(jnp.int32, (1, 128), 1)
        lse_all = jnp.zeros((AB, 128), F32)
        for kvh in range(KVH):
            ksl = slice(kvh * HD, (kvh + 1) * HD)
            kall = jnp.concatenate([kp_ref[:, ksl], kc_ref[:, ksl], kn_ref[:, ksl], kx_ref[:, ksl]], axis=0)
            vall = jnp.concatenate([vp_ref[:, ksl], vc_ref[:, ksl], vn_ref[:, ksl], vx_ref[:, ksl]], axis=0)
            s = _dot(_att_stack(q_ref, kvh), kall, NT, hi) * ATT_SCALE
            s = _att_masked(s, mask)
            sk = _att_sink(sink_ref, kvh)
            m = jnp.maximum(jnp.max(s, axis=1, keepdims=True), sk)
            p = jnp.exp(s - m)
            l = jnp.sum(p, axis=1, keepdims=True) + jnp.exp(sk - m)
            o = _dot(p, vall, NN, hi) / l
            lse = m + jnp.log(l)
            for g in range(GRP):
                h = kvh * GRP + g
                o_ref[:, h * HD:(h + 1) * HD] = o[g * AB:(g + 1) * AB]
                lse_all = lse_all + lse[g * AB:(g + 1) * AB] * (lane == h).astype(F32)
        lse_ref[...] = lse_all

    ks = _att_kspecs(nb)
    return pl.pallas_call(
        body, name="attn_fwd", grid=(nb,),
        in_specs=[pl.BlockSpec((AB, D), lambda i: (i, 0))] + ks + ks + [pl.BlockSpec((1, 128), lambda i: (0, 0))],
        out_specs=[pl.BlockSpec((AB, D), lambda i: (i, 0)), pl.BlockSpec((AB, 128), lambda i: (i, 0))],
        out_shape=[jax.ShapeDtypeStruct((tl, D), F32), jax.ShapeDtypeStruct((tl, 128), F32)],
        compiler_params=_cparams(("parallel",)),
    )(qr, kr, kr, kr, kr, vv, vv, vv, vv, sink)


def _mm_bat_dx_delta(dz_at, w_bat, o, hi):
    def fn(i, do_, o_):
        lane = lax.broadcasted_iota(jnp.int32, (1, 128), 1)
        acc = jnp.zeros((do_.shape[0], 128), F32)
        for h in range(NH):
            sl = slice(h * HD, (h + 1) * HD)
            acc = acc + jnp.sum(o_[:, sl] * do_[:, sl], axis=1, keepdims=True) * (lane == h).astype(F32)
        return do_, acc

    return _mm_ep("mm_bat_dx_delta", dz_at, w_bat, True, min(512, o.shape[0]), D, fn, [_In(o)], [_Out(D), _Out(128)], hi)


def _attn_bwd_q(qr, kr, vv, sink, do, lse, delta, hi):
    tl = qr.shape[0]
    nb = tl // AB

    def body(q_ref, kp_ref, kc_ref, kn_ref, kx_ref, vp_ref, vc_ref, vn_ref, vx_ref, sink_ref, do_ref, lse_ref, dl_ref,
             dq_ref, dkx_ref, dvx_ref, dsink_ref):
        i = pl.program_id(0)

        @pl.when(i == 0)
        def _():
            dkx_ref[...] = jnp.zeros_like(dkx_ref)
            dvx_ref[...] = jnp.zeros_like(dvx_ref)
            dsink_ref[...] = jnp.zeros_like(dsink_ref)

        mask = _att_mask(i, nb)
        lane = lax.broadcasted_iota(jnp.int32, (1, 128), 1)
        dsink = jnp.zeros((1, 128), F32)
        for kvh in range(KVH):
            ksl = slice(kvh * HD, (kvh + 1) * HD)
            kall = jnp.concatenate([kp_ref[:, ksl], kc_ref[:, ksl], kn_ref[:, ksl], kx_ref[:, ksl]], axis=0)
            vall = jnp.concatenate([vp_ref[:, ksl], vc_ref[:, ksl], vn_ref[:, ksl], vx_ref[:, ksl]], axis=0)
            qs = _att_stack(q_ref, kvh)
            dos = _att_stack(do_ref, kvh)
            lse_s = _att_col(lse_ref, kvh)
            dl_s = _att_col(dl_ref, kvh)
            s = _dot(qs, kall, NT, hi) * ATT_SCALE
            p = jnp.exp(_att_masked(s, mask) - lse_s)
            dp = _dot(dos, vall, NT, hi)
            ds = p * (dp - dl_s)
            dq = _dot(ds, kall, NN, hi) * ATT_SCALE
            dkx_ref[:, ksl] += _dot(ds[:, 3 * AB:], qs, TN, hi) * ATT_SCALE
            dvx_ref[:, ksl] += _dot(p[:, 3 * AB:], dos, TN, hi)
            psink = jnp.exp(_att_sink(sink_ref, kvh) - lse_s) * dl_s
            for g in range(GRP):
                h = kvh * GRP + g
                dq_ref[:, h * HD:(h + 1) * HD] = dq[g * AB:(g + 1) * AB]
                dsink = dsink - jnp.sum(psink[g * AB:(g + 1) * AB], axis=0, keepdims=True) * (lane == h).astype(F32)
        dsink_ref[...] += dsink

    ks = _att_kspecs(nb)
    row = pl.BlockSpec((AB, D), lambda i: (i, 0))
    col = pl.BlockSpec((AB, 128), lambda i: (i, 0))
    return pl.pallas_call(
        body, name="attn_bwd_q", grid=(nb,),
        in_specs=[row] + ks + ks + [pl.BlockSpec((1, 128), lambda i: (0, 0)), row, col, col],
        out_specs=[row, pl.BlockSpec((CTX, KVH * HD), lambda i: (0, 0)), pl.BlockSpec((CTX, KVH * HD), lambda i: (0, 0)),
                   pl.BlockSpec((1, 128), lambda i: (0, 0))],
        out_shape=[jax.ShapeDtypeStruct((tl, D), F32), jax.ShapeDtypeStruct((CTX, KVH * HD), F32),
                   jax.ShapeDtypeStruct((CTX, KVH * HD), F32), jax.ShapeDtypeStruct((1, 128), F32)],
        compiler_params=_cparams(("arbitrary",)),
    )(qr, kr, kr, kr, kr, vv, vv, vv, vv, sink, do, lse, delta)


def _attn_bwd_kv(qr, kr, vv, do, lse, delta, hi):
    tl = qr.shape[0]
    nb = tl // AB
    nc = CTX // AB

    def body(k_ref, v_ref, *refs):
        qs_refs, do_refs, lse_refs, dl_refs = refs[0:3], refs[3:6], refs[6:9], refs[9:12]
        dk_ref, dv_ref = refs[12], refs[13]
        j = pl.program_id(0)
        r = lax.broadcasted_iota(jnp.int32, (AB, AB), 0)
        c = lax.broadcasted_iota(jnp.int32, (AB, AB), 1)
        masks = [jnp.concatenate([jnp.logical_and(c <= r, j > 0)] * GRP, axis=0), None,
                 jnp.concatenate([jnp.logical_and(c >= r, j < nb - 1)] * GRP, axis=0)]
        for kvh in range(KVH):
            ksl = slice(kvh * HD, (kvh + 1) * HD)
            k_, v_ = k_ref[:, ksl], v_ref[:, ksl]
            dk = jnp.zeros((AB, HD), F32)
            dv = jnp.zeros((AB, HD), F32)
            for t in range(3):
                qs = _att_stack(qs_refs[t], kvh)
                dos = _att_stack(do_refs[t], kvh)
                lse_s = _att_col(lse_refs[t], kvh)
                dl_s = _att_col(dl_refs[t], kvh)
                s = _dot(qs, k_, NT, hi) * ATT_SCALE
                if masks[t] is not None:
                    s = jnp.where(masks[t], s, NEG)
                p = jnp.exp(s - lse_s)
                dp = _dot(dos, v_, NT, hi)
                ds = p * (dp - dl_s)
                dv = dv + _dot(p, dos, TN, hi)
                dk = dk + _dot(ds, qs, TN, hi) * ATT_SCALE
            dk_ref[:, ksl] = dk
            dv_ref[:, ksl] = dv

    def three(width):
        return [pl.BlockSpec((AB, width), lambda j: (jnp.maximum(j - 1, 0), 0)),
                pl.BlockSpec((AB, width), lambda j: (j, 0)),
                pl.BlockSpec((AB, width), lambda j: (jnp.minimum(j + 1, nb - 1), 0))]

    kv = pl.BlockSpec((AB, KVH * HD), lambda j: (j + nc, 0))
    out = pl.BlockSpec((AB, KVH * HD), lambda j: (j, 0))
    return pl.pallas_call(
        body, name="attn_bwd_kv", grid=(nb,),
        in_specs=[kv, kv] + three(D) + three(D) + three(128) + three(128),
        out_specs=[out, out],
        out_shape=[jax.ShapeDtypeStruct((tl, KVH * HD), F32)] * 2,
        compiler_params=_cparams(("parallel",)),
    )(kr, vv, qr, qr, qr, do, do, do, lse, lse, lse, delta, delta, delta)


def _mm(a, b, ta=False, tb=False, out_dtype=F32, tm=512, tn=1024, tk=1024, name="mm", hi=False):
    a_parts = a.shape[0] if a.ndim == 3 else 0
    b_parts = b.shape[0] if b.ndim == 3 else 0
    assert not (a_parts and ta) and not (b_parts and tb)
    if a_parts:
        m, kd = a.shape[1], a_parts * a.shape[2]
    else:
        m, kd = (a.shape[1], a.shape[0]) if ta else a.shape
    n = b_parts * b.shape[2] if b_parts else (b.shape[0] if tb else b.shape[1])
    tm, tn, tk = min(tm, m), min(tn, n), min(tk, kd)
    assert m % tm == 0 and n % tn == 0 and kd % tk == 0, (name, m, n, kd, tm, tn, tk)
    nk = kd // tk
    dims = ((0,) if ta else (1,), (1,) if tb else (0,))

    def body(a_ref, b_ref, o_ref, *scr):
        part = _dot(a_ref[0] if a_parts else a_ref[...], b_ref[0] if b_parts else b_ref[...], dims, hi)
        if nk == 1:
            o_ref[...] = part.astype(out_dtype)
        else:
            acc = scr[0]
            kk = pl.program_id(2)

            @pl.when(kk == 0)
            def _():
                acc[...] = part

            @pl.when(kk > 0)
            def _():
                acc[...] += part

            @pl.when(kk == nk - 1)
            def _():
                o_ref[...] = acc[...].astype(out_dtype)

    a_spec = pl.BlockSpec((tk, tm), lambda i, j, k: (k, i)) if ta else pl.BlockSpec((tm, tk), lambda i, j, k: (i, k))
    b_spec = pl.BlockSpec((tn, tk), lambda i, j, k: (j, k)) if tb else pl.BlockSpec((tk, tn), lambda i, j, k: (k, j))
    if a_parts:
        per = a.shape[2] // tk
        assert per * tk == a.shape[2]
        a_spec = pl.BlockSpec((1, tm, tk), lambda i, j, k: (k // per, i, k % per))
    if b_parts:
        per_n = b.shape[2] // tn
        assert per_n * tn == b.shape[2]
        b_spec = pl.BlockSpec((1, tk, tn), lambda i, j, k: (j // per_n, k, j % per_n))
    return pl.pallas_call(
        body, name=name, grid=(m // tm, n // tn, nk),
        in_specs=[a_spec, b_spec],
        out_specs=pl.BlockSpec((tm, tn), lambda i, j, k: (i, j)),
        out_shape=jax.ShapeDtypeStruct((m, n), out_dtype),
        scratch_shapes=[] if nk == 1 else [pltpu.VMEM((tm, tn), F32)],
        compiler_params=_cparams(("parallel", "parallel", "arbitrary")),
    )(a, b)


HALO = 8


class _In:
    def __init__(self, arr, w=None, cb=0, roff=0, halo=None, ridx=None):
        self.arr, self.w, self.cb, self.roff, self.halo = arr, w or arr.shape[1], cb, roff, halo
        self.ridx = ridx or (lambda i, roff=roff: i + roff)


class _Full:
    def __init__(self, arr, w=None, cb=0):
        self.arr, self.w, self.cb = arr, w, cb


class _Out:
    def __init__(self, cols, dtype=F32, w=None, cb=0, acc=False, rows=1, roff=0, nrows=None, stack=0, into=None):
        self.cols, self.dtype, self.w, self.cb, self.acc, self.rows, self.roff, self.nrows, self.stack = (
            cols, dtype, w or cols, cb, acc, rows, roff, nrows, stack)
        self.into = into


def _alias_outs(arrays, specs, outs):
    aliases = {}
    for k, o in enumerate(outs):
        if o.into is not None:
            aliases[len(arrays)] = k
            arrays.append(o.into)
            specs.append(pl.BlockSpec(memory_space=pl.ANY))
    return aliases


def _rowcall(name, fn, nrow_tiles, tile, ins, outs, ncol=1):
    arrays, specs, kinds = [], [], []
    for x in ins:
        if isinstance(x, _Full):
            arrays.append(x.arr)
            if x.w is None:
                specs.append(pl.BlockSpec(x.arr.shape, lambda j, i: (0, 0)))
            else:
                specs.append(pl.BlockSpec((x.arr.shape[0], x.w), lambda j, i, cb=x.cb: (0, cb + j)))
            kinds.append("full")
            continue
        w, cb, roff = x.w, x.cb, x.roff
        cur = pl.BlockSpec((tile, w), lambda j, i, cb=cb, ridx=x.ridx: (ridx(i), cb + j))
        if x.halo is None:
            arrays.append(x.arr)
            specs.append(cur)
            kinds.append("tile")
        else:
            r8 = tile // HALO
            last = x.arr.shape[0] // HALO - 1
            prev = pl.BlockSpec((HALO, w), lambda j, i, cb=cb, roff=roff, r8=r8: (jnp.maximum((i + roff) * r8 - 1, 0), cb + j))
            nxt = pl.BlockSpec((HALO, w), lambda j, i, cb=cb, roff=roff, r8=r8, last=last:
                               (jnp.minimum((i + roff + 1) * r8, last), cb + j))
            arrays += [x.arr, x.arr, x.arr]
            specs += [prev, cur, nxt]
            kinds.append(("halo", x.halo))
    out_specs, out_shapes = [], []
    for o in outs:
        if o.acc:
            out_specs.append(pl.BlockSpec((o.rows, o.w), lambda j, i, cb=o.cb: (0, cb + j)))
            out_shapes.append(jax.ShapeDtypeStruct((o.rows, o.cols), o.dtype))
        elif o.stack:
            out_specs.append(pl.BlockSpec((o.stack, tile, o.w), lambda j, i, cb=o.cb: (0, i, cb + j)))
            out_shapes.append(jax.ShapeDtypeStruct((o.stack, nrow_tiles * tile, o.cols), o.dtype))
        else:
            out_specs.append(pl.BlockSpec((tile, o.w), lambda j, i, cb=o.cb, roff=o.roff: (i + roff, cb + j)))
            out_shapes.append(jax.ShapeDtypeStruct(((o.nrows or nrow_tiles * tile), o.cols), o.dtype))
    aliases = _alias_outs(arrays, specs, outs)
    n_in = len(arrays)

    def body(*refs):
        j = pl.program_id(0)
        i = pl.program_id(1)
        vals, r = [], 0
        for kind in kinds:
            if kind in ("full", "tile"):
                vals.append(refs[r][...])
                r += 1
            else:
                pok, nok = kind[1]
                p, c, n = refs[r][...], refs[r + 1][...], refs[r + 2][...]
                p = jnp.where(pok(i), p, jnp.zeros_like(p))
                n = jnp.where(nok(i), n, jnp.zeros_like(n))
                vals.append(jnp.concatenate([p, c, n], axis=0))
                r += 3
        res = fn(i, j, *vals)
        for o, ref, val in zip(outs, refs[n_in:], res):
            if o.acc:
                @pl.when(i == 0)
                def _(ref=ref, val=val, o=o):
                    ref[...] = val.astype(o.dtype)

                @pl.when(i > 0)
                def _(ref=ref, val=val, o=o):
                    ref[...] += val.astype(o.dtype)
            elif o.stack:
                for s_ in range(o.stack):
                    ref[s_] = val[s_].astype(o.dtype)
            else:
                ref[...] = val.astype(o.dtype)

    return pl.pallas_call(
        body, name=name, grid=(ncol, nrow_tiles), in_specs=specs, out_specs=out_specs, out_shape=out_shapes,
        input_output_aliases=aliases, compiler_params=_cparams(("parallel", "arbitrary")),
    )(*arrays)


def _mm_ep(name, a, b, tb, tm, tk, fn, ins, outs, hi=False):
    a_parts = a.shape[0] if a.ndim == 3 else 0
    m, kd = (a.shape[1], a_parts * a.shape[2]) if a_parts else a.shape
    n = b.shape[0] if tb else b.shape[1]
    tk = min(tk, kd)
    assert m % tm == 0 and kd % tk == 0, (name, m, kd, tm, tk)
    nk = kd // tk
    dims = ((1,), (1,) if tb else (0,))
    if a_parts:
        per = a.shape[2] // tk
        arrays, specs = [a], [pl.BlockSpec((1, tm, tk), lambda i, k: (k // per, i, k % per))]
    else:
        arrays, specs = [a], [pl.BlockSpec((tm, tk), lambda i, k: (i, k))]
    arrays.append(b)
    specs.append(pl.BlockSpec((n, tk), lambda i, k: (0, k)) if tb else pl.BlockSpec((tk, n), lambda i, k: (k, 0)))
    for x in ins:
        arrays.append(x.arr)
        if isinstance(x, _Full):
            specs.append(pl.BlockSpec(x.arr.shape, lambda i, k: (0, 0)))
        else:
            specs.append(pl.BlockSpec((tm, x.w), lambda i, k, cb=x.cb, ridx=x.ridx: (ridx(i), cb)))
    out_specs, out_shapes = [], []
    for o in outs:
        if o.acc:
            out_specs.append(pl.BlockSpec((o.rows, o.w), lambda i, k, cb=o.cb: (0, cb)))
            out_shapes.append(jax.ShapeDtypeStruct((o.rows, o.cols), o.dtype))
        else:
            out_specs.append(pl.BlockSpec((tm, o.w), lambda i, k, cb=o.cb, roff=o.roff: (i + roff, cb)))
            out_shapes.append(jax.ShapeDtypeStruct((o.nrows or m, o.cols), o.dtype))
    n_vals = len(arrays)
    aliases = _alias_outs(arrays, specs, outs)
    n_in = len(arrays)

    def body(*refs):
        i, kk = pl.program_id(0), pl.program_id(1)
        a_ref, b_ref = refs[0], refs[1]
        acc_ref = refs[-1]
        part = _dot(a_ref[0] if a_parts else a_ref[...], b_ref[...], dims, hi)

        @pl.when(kk == 0)
        def _():
            acc_ref[...] = part

        @pl.when(kk > 0)
        def _():
            acc_ref[...] += part

        @pl.when(kk == nk - 1)
        def _():
            res = fn(i, acc_ref[...], *[r[...] for r in refs[2:n_vals]])
            for o, ref, val in zip(outs, refs[n_in:-1], res):
                if o.acc:
                    @pl.when(i == 0)
                    def _(ref=ref, val=val, o=o):
                        ref[...] = val.astype(o.dtype)

                    @pl.when(i > 0)
                    def _(ref=ref, val=val, o=o):
                        ref[...] += val.astype(o.dtype)
                else:
                    ref[...] = val.astype(o.dtype)

    return pl.pallas_call(
        body, name=name, grid=(m // tm, nk), in_specs=specs, out_specs=out_specs, out_shape=out_shapes,
        scratch_shapes=[pltpu.VMEM((tm, n), F32)], input_output_aliases=aliases,
        compiler_params=_cparams(("arbitrary", "arbitrary")),
    )(*arrays)


def _shift(xe, s, tile):
    if s == 0:
        return xe[HALO:HALO + tile]
    return pltpu.roll(xe, (-s) % xe.shape[0], 0)[HALO:HALO + tile]


def _silu(x):
    return x * jax.nn.sigmoid(x)


def _dsilu(x):
    s = jax.nn.sigmoid(x)
    return s * (1.0 + x * (1.0 - s))


def _heads(x, fn):
    return jnp.concatenate([fn(h, x[:, h * HD:(h + 1) * HD]) for h in range(x.shape[1] // HD)], axis=1)


def _colsum(x):
    return jnp.sum(x, axis=0, keepdims=True)


def _rowmean(x):
    return jnp.mean(x, axis=1, keepdims=True)


def _rowsum(x):
    return jnp.sum(x, axis=1, keepdims=True)


TILE = 256
CT = CTX // TILE


def _all_halo(n_tiles):
    return (lambda i: i >= CT + 1, lambda i: jnp.logical_and(i >= CT, i < n_tiles - 1))


def _lat_halo(n_tiles):
    return (lambda i: i >= 1, lambda i: i < n_tiles - 1)


def _rms_mod(x, nm, shift, scale):
    r = lax.rsqrt(_rowmean(x * x) + EPS)
    return (x * r * nm) * (1.0 + scale) + shift


def _rms_mod_bwd(dh, x, nm, scale):
    r = lax.rsqrt(_rowmean(x * x) + EPS)
    xn = x * r
    dz = dh * (1.0 + scale)
    dxn = dz * nm
    dx = r * (dxn - xn * _rowmean(dxn * xn))
    return dx, _colsum(dz * xn), _colsum(dh), _colsum(dh * (xn * nm))


def _norm_mod(x, ctx, nm, mod_c, mod_x):
    n = (x.shape[0] + ctx.shape[0]) // TILE

    def fn(i, j, c_, x_, nm_, mc, mx):
        m = jnp.where(i < CT, mc, mx)
        return (_rms_mod(jnp.where(i < CT, c_, x_), nm_, m[0:1], m[1:2]),)

    ins = [_In(ctx, ridx=lambda i: jnp.minimum(i, CT - 1)), _In(x, ridx=lambda i: jnp.maximum(i - CT, 0)),
           _Full(nm), _Full(mod_c), _Full(mod_x)]
    return _rowcall("norm_mod", fn, n, TILE, ins, [_Out(D, BF16)])[0]


def _norm_mod_bwd(dh, xs, dres, nm, mod, roff, n):
    ins = [_In(dh, roff=roff), _In(xs), _Full(nm), _Full(mod)] + ([] if dres is None else [_In(dres)])

    def fn(i, j, dh_, x, nm_, m, *rest):
        dx, dn, dsh, dsc = _rms_mod_bwd(dh_, x, nm_, m[1:2])
        if rest:
            return (dx + rest[0], dn, dsh, dsc)
        return (dn, dsh, dsc)

    accs = [_Out(D, acc=True), _Out(D, acc=True), _Out(D, acc=True)]
    return _rowcall("norm_mod_bwd", fn, n, TILE, ins, ([] if dres is None else [_Out(D)]) + accs)


DN_Q_SCALE = HD ** -0.5


def _conv_taps(xe, w, width, rows=None):
    r = width // 2
    acc = None
    for t in range(width):
        s = t - r
        if rows is None:
            sh = xe if s == 0 else pltpu.roll(xe, (-s) % xe.shape[0], 0)
        else:
            sh = _shift(xe, s, rows)
        term = sh * w[t:t + 1]
        acc = term if acc is None else acc + term
    return acc


def _rolled(xe, width):
    r = width // 2
    return [xe if t == r else pltpu.roll(xe, (r - t) % xe.shape[0], 0) for t in range(width)]


def _conv_bwd(rolled, w, c_grad, width):
    r = width // 2
    cc = c_grad[HALO:HALO + TILE]
    dx, dws = None, []
    for t in range(width):
        term = _shift(c_grad, r - t, TILE) * w[t:t + 1]
        dx = term if dx is None else dx + term
        dws.append(_colsum(cc * rolled[t][HALO:HALO + TILE]))
    return dx, jnp.concatenate(dws + [jnp.zeros((8 - width, cc.shape[1]), F32)], axis=0)


def _silu_both(x):
    s = jax.nn.sigmoid(x)
    return x * s, s * (1.0 + x * (1.0 - s))


def _l2n(x, scale):
    rn = lax.rsqrt(_rowsum(x * x) + EPS)
    return x * (rn * scale)


def _l2n_bwd(dy, x, scale):
    rn = lax.rsqrt(_rowsum(x * x) + EPS)
    xu = x * rn
    return (scale * rn) * (dy - xu * _rowsum(dy * xu))


def _softplus(x):
    return jnp.maximum(x, 0.0) + jnp.log(1.0 + jnp.exp(-jnp.abs(x)))


def _lane_mask(lo, hi_):
    lane = lax.broadcasted_iota(jnp.int32, (1, 128), 1)
    return jnp.logical_and(lane >= lo, lane < hi_).astype(F32)


def _dn_prep(p, conv_w, gprm):
    n = p.shape[0] // TILE
    halo = _all_halo(n)

    def fn(i, j, qe, ke, ve, ba, w, gp):
        cq = _conv_taps(qe, w[:, 0:D], 5, TILE)
        ck = _conv_taps(ke, w[:, D:2 * D], 5, TILE)
        cv = _conv_taps(ve, w[:, 2 * D:3 * D], 5, TILE)
        q = _heads(_silu(cq), lambda h, x: _l2n(x, DN_Q_SCALE))
        k = _heads(_silu(ck), lambda h, x: _l2n(x, 1.0))
        v = _silu(cv)
        beta = jax.nn.sigmoid(ba)
        g = -jnp.exp(gp[0:1]) * _softplus(ba + gp[1:2])
        m0, m1 = _lane_mask(0, 8), _lane_mask(8, 16)
        gb_f = beta * m0 + pltpu.roll(g, 128 - 8, 1) * m1
        gb_b = pltpu.roll(beta, 128 - 8, 1) * m0 + pltpu.roll(g, 128 - 16, 1) * m1
        return q, k, v, gb_f, gb_b

    ins = [_In(p, D, 0, halo=halo), _In(p, D, 1, halo=halo), _In(p, D, 2, halo=halo), _In(p, 128, C_BA // 128),
           _Full(conv_w), _Full(gprm)]
    return _rowcall("dn_prep", fn, n, TILE, ins, [_Out(D), _Out(D), _Out(D), _Out(128), _Out(128)])


def _dn_prep_bwd(p, conv_w, gprm, dq2, dk2, dv2, dgb2, dk_at, dv_at, dp):
    n = p.shape[0] // TILE
    halo = _all_halo(n)

    def branch(xe, w, dye, scale):
        rolled = _rolled(xe, 5)
        c = rolled[0] * w[0:1]
        for t in range(1, 5):
            c = c + rolled[t] * w[t:t + 1]
        sx, dsilu = _silu_both(c)
        if scale is None:
            dsx = dye
        else:
            dsx = jnp.concatenate([_l2n_bwd(dye[:, h * HD:(h + 1) * HD], sx[:, h * HD:(h + 1) * HD], scale)
                                   for h in range(NH)], axis=1)
        return _conv_bwd(rolled, w, dsx * dsilu, 5)

    def fn(i, j, qe, ke, ve, ba, w, gp, dq0, dq1, dk0, dk1, dv0, dv1, dg0, dg1, dka, dva):
        dxq, dwq = branch(qe, w[:, 0:D], dq0 + dq1, DN_Q_SCALE)
        dxk, dwk = branch(ke, w[:, D:2 * D], dk0 + dk1, 1.0)
        dxv, dwv = branch(ve, w[:, 2 * D:3 * D], dv0 + dv1, None)
        m0, m1 = _lane_mask(0, 8), _lane_mask(8, 16)
        dbeta = dg0 * m0 + pltpu.roll(dg1 * m0, 8, 1)
        dg = pltpu.roll(dg0 * m1, 8, 1) + pltpu.roll(dg1 * m1, 16, 1)
        beta = jax.nn.sigmoid(ba)
        ea = jnp.exp(gp[0:1])
        z = ba + gp[1:2]
        g = -ea * _softplus(z)
        mg = _lane_mask(16, 32)
        da = dg * (-ea) * jax.nn.sigmoid(z) * mg
        dba = dbeta * beta * (1.0 - beta) * _lane_mask(0, 16) + da
        dgp = jnp.concatenate([_colsum(dg * g * mg), _colsum(da)], axis=0)
        half = jnp.concatenate([dxq, dxk, dxv, dka.astype(F32), dva.astype(F32), dba, jnp.zeros((TILE, PH - C_PAD), F32)],
                               axis=1)
        return (half, jnp.concatenate([dwq, dwk, dwv], axis=1), dgp)

    ins = [_In(p, D, 0, halo=halo), _In(p, D, 1, halo=halo), _In(p, D, 2, halo=halo), _In(p, 128, C_BA // 128),
           _Full(conv_w), _Full(gprm),
           _In(dq2, halo=halo), _In(dq2, roff=n, halo=halo), _In(dk2, halo=halo), _In(dk2, roff=n, halo=halo),
           _In(dv2, halo=halo), _In(dv2, roff=n, halo=halo), _In(dgb2), _In(dgb2, roff=n), _In(dk_at), _In(dv_at)]
    return _rowcall("dn_prep_bwd", fn, n, TILE, ins,
                    [_Out(PW, BF16, w=PH, cb=0, into=dp), _Out(3 * D, acc=True, rows=8), _Out(128, acc=True, rows=2)])


def _hnorm(x, w):
    return x * lax.rsqrt(_rowmean(x * x) + EPS) * w


def _hnorm_bwd(dy, x, w):
    r = lax.rsqrt(_rowmean(x * x) + EPS)
    xh = x * r
    dxh = dy * w
    return r * (dxh - xh * _rowmean(dxh * xh)), _colsum(dy * xh)


def _dn_gate(o2, p, dn_norm, n_all):
    n = n_all - CT

    def fn(i, j, of, ob, gt, w):
        o = of + ob
        return (_heads(o, lambda h, x: _hnorm(x, w)) * _silu(gt),)

    ins = [_In(o2, roff=CT), _In(o2, roff=n_all + CT), _In(p, D, C_GT // D, roff=CT), _Full(dn_norm)]
    return _rowcall("dn_gate", fn, n, TILE, ins, [_Out(D, BF16)])[0]


def _mm_bdn_dx_gate(dz_dn, w_bdn, o2, p, dn_norm, n_all, dp, hi):
    def fn(i, dy_, of, ob, gt, w):
        o = of + ob
        sg, dsg = _silu_both(gt)
        dos, dw = [], jnp.zeros((1, HD), F32)
        yn = []
        for h in range(NH):
            sl = slice(h * HD, (h + 1) * HD)
            dx, dwh = _hnorm_bwd(dy_[:, sl] * sg[:, sl], o[:, sl], w)
            dos.append(dx)
            dw = dw + dwh
            yn.append(_hnorm(o[:, sl], w))
        dgt = dy_ * jnp.concatenate(yn, axis=1) * dsg
        return jnp.concatenate(dos, axis=1), dgt, dw

    ins = [_In(o2, roff=CT), _In(o2, roff=n_all + CT), _In(p, D, C_GT // D, roff=CT), _Full(dn_norm)]
    outs = [_Out(D), _Out(PW, BF16, w=D, cb=C_GT // D, roff=CT, nrows=p.shape[0], into=dp), _Out(HD, acc=True)]
    return _mm_ep("mm_bdn_dx_gate", dz_dn, w_bdn, True, TILE, D, fn, ins, outs, hi)


def _rope_shuffle(x):
    lane = lax.broadcasted_iota(jnp.int32, (1, HD), 1)
    return jnp.where((lane % 64) < 32, pltpu.roll(x, HD - 32, 1), pltpu.roll(x, 32, 1))


def _rope(x, cos, sin):
    return x * cos + _rope_shuffle(x) * sin


def _rope_bwd(dy, cos, sin):
    return dy * cos + _rope_shuffle(dy * sin)


def _attn_prep(p, w, cos, sin, width, cb, roff, n, name):
    def fn(i, j, x, w_, c, s):
        return (_heads(x, lambda h, xh: _rope(_hnorm(xh, w_), c, s)),)

    ins = [_In(p, width, cb, roff=roff), _Full(w), _In(cos), _In(sin)]
    return _rowcall(name, fn, n, TILE, ins, [_Out(width)])[0]


def _attn_prep_bwd(dy, p, w, cos, sin, width, cb, roff, n, name, dx_out):
    def fn(i, j, dy_, x, w_, c, s):
        dxs, dw = [], jnp.zeros((1, HD), F32)
        for h in range(width // HD):
            sl = slice(h * HD, (h + 1) * HD)
            dx, dwh = _hnorm_bwd(_rope_bwd(dy_[:, sl], c, s), x[:, sl], w_)
            dxs.append(dx)
            dw = dw + dwh
        return jnp.concatenate(dxs, axis=1), dw

    ins = [_In(dy), _In(p, width, cb, roff=roff), _Full(w), _In(cos), _In(sin)]
    return _rowcall(name, fn, n, TILE, ins, [dx_out, _Out(HD, acc=True)])


def _mm_bat_merge(o_at, w_bat, z_dn, p, hi):
    def fn(i, za, zd, gd, ga):
        return za, jax.nn.sigmoid(gd) * zd + jax.nn.sigmoid(ga) * za

    ins = [_In(z_dn), _In(p, D, C_MG // D, roff=CT), _In(p, D, C_MG // D + 1, roff=CT)]
    return _mm_ep("mm_bat_merge", o_at, w_bat, False, TILE, D, fn, ins, [_Out(D), _Out(D, BF16)], hi)


def _mm_out_dx_merge(dmo, w_out, z_dn, z_at, p, hi):
    def fn(i, dm_, zd, za, gd, ga):
        sd, sa = jax.nn.sigmoid(gd), jax.nn.sigmoid(ga)
        dg = jnp.concatenate([dm_ * zd * sd * (1.0 - sd), dm_ * za * sa * (1.0 - sa)], axis=1)
        return dm_ * sd, dm_ * sa, dg

    ins = [_In(z_dn), _In(z_at), _In(p, D, C_MG // D, roff=CT), _In(p, D, C_MG // D + 1, roff=CT)]
    outs = [_Out(D, BF16), _Out(D, BF16), _Out(PW, BF16, w=2 * D, cb=C_MG // (2 * D), roff=CT, nrows=p.shape[0])]
    return _mm_ep("mm_out_dx_merge", dmo, w_out, True, TILE, D, fn, ins, outs, hi)


def _mm_out_resid(merged, w_out, x, g_a, nf, mod_f, hi):
    def fn(i, mo_, x_, ga, nf_, m):
        x1 = x_ + ga * mo_
        return mo_, x1, _rms_mod(x1, nf_, m[0:1], m[1:2])

    ins = [_In(x), _Full(g_a), _Full(nf), _Full(mod_f)]
    return _mm_ep("mm_out_resid", merged, w_out, False, min(512, x.shape[0]), D, fn, ins, [_Out(D), _Out(D), _Out(D, BF16)], hi)


def _mm_up_dx_norm(du, ffn_up, dy, x1, mo, g_a, nf, mod_f, hi):
    def fn(i, dh_, dy_, x1_, mo_, ga, nf_, m):
        dx, dn, dsh, dsc = _rms_mod_bwd(dh_, x1_, nf_, m[1:2])
        dx1 = dy_ + dx
        return dx1, ga * dx1, dn, dsh, dsc, _colsum(dx1 * mo_)

    ins = [_In(dy), _In(x1), _In(mo), _Full(g_a), _Full(nf), _Full(mod_f)]
    accs = [_Out(D, acc=True) for _ in range(4)]
    return _mm_ep("mm_up_dx_norm", du, ffn_up, True, min(512, x1.shape[0]), 1408, fn, ins, [_Out(D), _Out(D, BF16)] + accs, hi)


def _mm_down_loss(a, ffn_down, x1, tgt, g_f, hi):
    def fn(i, f_, x1_, t, gf):
        e = x1_ + gf * f_ - t
        dy = e * (1.0 / D)
        loss = _colsum(_rowsum(e * e)) * (0.5 / D)
        return dy, gf * dy, _colsum(dy * f_), jnp.broadcast_to(loss, (1, 128))

    ins = [_In(x1), _In(tgt), _Full(g_f)]
    outs = [_Out(D), _Out(D, BF16), _Out(D, acc=True), _Out(128, acc=True)]
    return _mm_ep("mm_down_loss", a, ffn_down, False, min(512, x1.shape[0]), DFF, fn, ins, outs, hi)


FW = DFF // 2


def _ffn_act(u, conv_w, conv_b, n):
    halo = _lat_halo(n)

    def fn(i, j, ge, ve, wg, wv, bg, bv):
        cg = _conv_taps(ge, wg, 3, TILE) + bg
        cv = _conv_taps(ve, wv, 3, TILE) + bv
        return (_silu(cg) * cv,)

    ins = [_In(u, FW, 0, halo=halo), _In(u, FW, 2, halo=halo), _Full(conv_w, FW, 0), _Full(conv_w, FW, 2),
           _Full(conv_b, FW, 0), _Full(conv_b, FW, 2)]
    return _rowcall("ffn_act", fn, n, TILE, ins, [_Out(DFF, BF16, FW)], ncol=2)[0]


def _ffn_act_bwd(u, da, conv_w, conv_b, n):
    halo = _lat_halo(n)

    def fn(i, j, ge, ve, dae, wg, wv, bg, bv):
        rg, rv = _rolled(ge, 3), _rolled(ve, 3)
        cg = rg[0] * wg[0:1] + rg[1] * wg[1:2] + rg[2] * wg[2:3] + bg
        cv = rv[0] * wv[0:1] + rv[1] * wv[1:2] + rv[2] * wv[2:3] + bv
        sg, dsg = _silu_both(cg)
        dcg = dae * cv * dsg
        dcv = dae * sg
        dxg, dwg = _conv_bwd(rg, wg, dcg, 3)
        dxv, dwv = _conv_bwd(rv, wv, dcv, 3)
        return (dxg, dxv), dwg, dwv, _colsum(dcg[HALO:HALO + TILE]), _colsum(dcv[HALO:HALO + TILE])

    ins = [_In(u, FW, 0, halo=halo), _In(u, FW, 2, halo=halo), _In(da, FW, 0, halo=halo),
           _Full(conv_w, FW, 0), _Full(conv_w, FW, 2), _Full(conv_b, FW, 0), _Full(conv_b, FW, 2)]
    outs = [_Out(DFF, BF16, FW, stack=2), _Out(DFF, w=FW, acc=True, rows=8), _Out(DFF, w=FW, acc=True, rows=8),
            _Out(DFF, w=FW, acc=True), _Out(DFF, w=FW, acc=True)]
    return _rowcall("ffn_act_bwd", fn, n, TILE, ins, outs, ncol=2)


def _rope_tables(tl):
    rows = tl // GRID_W
    inv = np.float32(ROPE_BASE) ** (-np.arange(32, dtype=np.float32) / np.float32(32))
    ar = np.arange(rows, dtype=np.float32)[:, None] * inv
    ac = np.arange(GRID_W, dtype=np.float32)[:, None] * inv

    def table(r, c):
        full = (rows, GRID_W, HD // 2)
        return jnp.concatenate([jnp.broadcast_to(jnp.asarray(r)[:, None, :], full),
                                jnp.broadcast_to(jnp.asarray(c)[None, :, :], full)], axis=2).reshape(tl, HD)

    two = lambda a, b: np.concatenate([a, b], axis=1).astype(np.float32)
    cos = table(two(np.cos(ar), np.cos(ar)), two(np.cos(ac), np.cos(ac)))
    sin = table(two(-np.sin(ar), np.sin(ar)), two(-np.sin(ac), np.sin(ac)))
    return cos, sin


def _pad_w_in(w_in):
    return jnp.concatenate([w_in[:, 0:3072], w_in[:, 5152:5664], w_in[:, 4096:4128], jnp.zeros((D, 96 + C_GT - C_PAD), w_in.dtype),
                            w_in[:, 3072:4096], w_in[:, 4128:5152], w_in[:, 5664:7712]], axis=1)


def _unpad_w_in(g, axis=1):
    cut = lambda a, b: lax.slice_in_dim(g, a, b, axis=axis)
    return jnp.concatenate([cut(0, 3072), cut(C_GT, C_GT + D), cut(C_BA, C_BA + 32), cut(C_QAT, C_QAT + D),
                            cut(C_KAT, C_KAT + 512), cut(C_MG, C_MG + 2 * D)], axis=axis)


def _local_step(x, ctx, tgt, mod_x, mod_c, w, hi=False):
    tl = x.shape[0]
    t_all = tl + CTX
    n_all, n = t_all // TILE, tl // TILE
    tm_all = 1280 if t_all % 1280 == 0 else TILE
    tm_lat = 1024
    mm = functools.partial(_mm, hi=hi)
    sp = lambda m: [m[:, k * D:(k + 1) * D] for k in range(6)]
    sh_a, sc_a, g_a, sh_f, sc_f, g_f = sp(mod_x)
    sh_ac, sc_ac = sp(mod_c)[:2]
    mod_ax = jnp.concatenate([sh_a, sc_a], axis=0)
    mod_ac = jnp.concatenate([sh_ac, sc_ac], axis=0)
    mod_f = jnp.concatenate([sh_f, sc_f], axis=0)
    nm, nf = w["norm_mix"], w["norm_ffn"]
    cos, sin = _rope_tables(tl)
    cos_all = jnp.concatenate([jnp.ones((CTX, HD), F32), cos], axis=0)
    sin_all = jnp.concatenate([jnp.zeros((CTX, HD), F32), sin], axis=0)
    conv_dn = jnp.concatenate([w["dn_conv"], jnp.zeros((3, 3 * D), F32)], axis=0)
    gprm = jnp.concatenate([jnp.zeros((2, 16), F32),
                            jnp.concatenate([w["dn_a_log"].reshape(1, 16), w["dn_dt_bias"].reshape(1, 16)], axis=0),
                            jnp.zeros((2, 96), F32)], axis=1)
    conv_ff = jnp.concatenate([w["ffn_conv"], jnp.zeros((5, 2 * DFF), F32)], axis=0)
    sink = jnp.concatenate([w["attn_sink"].reshape(1, NH), jnp.zeros((1, 128 - NH), F32)], axis=1)
    nct = CTX // CH

    h = _norm_mod(x, ctx, nm, mod_ac, mod_ax)
    p = mm(h, w["w_in_p"], tm=tm_all, tn=1024, name="mm_in")
    q, k, v, gb_f, gb_b = _dn_prep(p, conv_dn, gprm)
    gb = jnp.stack([gb_f, gb_b])
    dn_u, dn_w, dn_qg, dn_kd, dn_pm, dn_t = _dn_intra_fwd(q, k, v, gb, nct, hi)
    o2, s_hist, dn_vn = _dn_seq_fwd(dn_u, dn_w, dn_qg, dn_kd, dn_pm, gb, nct, hi)
    o2 = o2.reshape(2 * t_all, D)
    y_dn = _dn_gate(o2, p, w["dn_norm"], n_all)
    qr = _attn_prep(p, w["q_norm"], cos, sin, D, C_QAT // D, CT, n, "attn_prep_q")
    kr = _attn_prep(p, w["k_norm"], cos_all, sin_all, KVH * HD, C_KAT // (KVH * HD), 0, n_all, "attn_prep_k")
    vv = p[:, C_VAT:C_VAT + KVH * HD]
    o_at, lse = _attn_fwd(qr, kr, vv, sink, hi)
    z_dn = mm(y_dn, w["w_branch_dn"], tm=tm_lat, name="mm_bdn")
    z_at, merged = _mm_bat_merge(o_at, w["w_branch_attn"], z_dn, p, hi)
    mo, x1, h2 = _mm_out_resid(merged, w["w_out"], x, g_a, nf, mod_f, hi)
    u = mm(h2, w["ffn_up"], tm=2 * tm_lat, tn=1408, name="mm_up")
    a = _ffn_act(u, conv_ff, w["ffn_conv_b"], n)
    dy, df, dg_f, loss = _mm_down_loss(a, w["ffn_down"], x1, tgt, g_f, hi)

    g = {}
    da = mm(df, w["ffn_down"], tb=True, tm=tm_lat, tn=1408, name="mm_down_dx")
    g["ffn_down"] = mm(a, df, ta=True, tm=1408, tn=1024, tk=tm_lat, name="mm_down_dw")
    du, dcw_g, dcw_v, dcb_g, dcb_v = _ffn_act_bwd(u, da, conv_ff, w["ffn_conv_b"], n)
    g["ffn_conv"] = jnp.concatenate([dcw_g, dcw_v], axis=1)[0:3]
    g["ffn_conv_b"] = jnp.concatenate([dcb_g, dcb_v], axis=1)
    g["ffn_up"] = mm(h2, du, ta=True, tm=1024, tn=1408, tk=tm_lat, name="mm_up_dw")
    dx1, dmo, g["norm_ffn"], dsh_f, dsc_f, dg_a = _mm_up_dx_norm(du, w["ffn_up"], dy, x1, mo, g_a, nf, mod_f, hi)
    g["w_out"] = mm(merged, dmo, ta=True, tm=1024, tk=tm_lat, name="mm_out_dw")
    dz_dn, dz_at, dmg = _mm_out_dx_merge(dmo, w["w_out"], z_dn, z_at, p, hi)
    g["w_branch_dn"] = mm(y_dn, dz_dn, ta=True, tm=1024, tk=tm_lat, name="mm_bdn_dw")
    do_at, delta = _mm_bat_dx_delta(dz_at, w["w_branch_attn"], o_at, hi)
    g["w_branch_attn"] = mm(o_at, dz_at, ta=True, tm=1024, tk=tm_lat, name="mm_bat_dw")

    do_dn, dp, g["dn_norm"] = _mm_bdn_dx_gate(dz_dn, w["w_branch_dn"], o2, p, w["dn_norm"], n_all, dmg, hi)
    do_all = do_dn
    dn_dvn, dn_dw, dn_dqg, dn_dkd, dn_del = _dn_seq_bwd(dn_w, dn_qg, dn_kd, dn_pm, dn_vn, s_hist, gb, do_all, nct, hi)
    dq2, dk2, dv2, dgb2 = _dn_intra_bwd(q, k, v, gb, dn_u, dn_w, dn_t, dn_vn, dn_dvn, dn_dw, dn_dqg, dn_dkd, dn_del,
                                        do_all, nct, hi)

    dqr, dkx, dvx, dsink = _attn_bwd_q(qr, kr, vv, sink, do_at, lse, delta, hi)
    dk_lat, dv_lat = _attn_bwd_kv(qr, kr, vv, do_at, lse, delta, hi)
    g["attn_sink"] = dsink[:, 0:NH]
    q_out = _Out(PW, BF16, w=D, cb=C_QAT // D, roff=CT, nrows=t_all, into=dp)
    dp, g["q_norm"] = _attn_prep_bwd(dqr, p, w["q_norm"], cos, sin, D, C_QAT // D, CT, n, "attn_prep_q_bwd", q_out)
    dkr = jnp.concatenate([dkx, dk_lat], axis=0)
    dk_at, g["k_norm"] = _attn_prep_bwd(dkr, p, w["k_norm"], cos_all, sin_all, KVH * HD, C_KAT // (KVH * HD), 0, n_all,
                                        "attn_prep_k_bwd", _Out(KVH * HD, BF16))
    dv_at = jnp.concatenate([dvx, dv_lat], axis=0).astype(BF16)

    dp, dconv, dgprm = _dn_prep_bwd(p, conv_dn, gprm, dq2.reshape(2 * t_all, D), dk2.reshape(2 * t_all, D),
                                    dv2.reshape(2 * t_all, D), dgb2.reshape(2 * t_all, 128), dk_at, dv_at, dp)
    g["dn_conv"] = dconv[0:5]
    g["dn_a_log"] = dgprm[0, 16:32].reshape(2, NH)
    g["dn_dt_bias"] = dgprm[1, 16:32].reshape(2, NH)
    dp = lax.dynamic_update_slice(dp, jnp.zeros((CTX, PH), BF16), (0, PH))
    dh = mm(dp, w["w_in_p"], tb=True, tm=tm_all, tn=1024, tk=2048, name="mm_in_dx")
    g["w_in_p"] = mm(h, dp, ta=True, tm=1024, tn=2048, tk=tm_all, name="mm_in_dw")
    dnm_c, dsh_ac, dsc_ac = _norm_mod_bwd(dh, ctx, None, nm, mod_ac, 0, CT)
    grad_x, dnm_x, dsh_a, dsc_a = _norm_mod_bwd(dh, x, dx1, nm, mod_ax, CT, n)
    g["norm_mix"] = dnm_c + dnm_x
    dmod_x = jnp.concatenate([dsh_a, dsc_a, dg_a, dsh_f, dsc_f, dg_f], axis=1)
    dmod_c = jnp.concatenate([dsh_ac, dsc_ac, jnp.zeros((1, 4 * D), F32)], axis=1)
    return loss, grad_x, g, dmod_x, dmod_c


def _sum_slots(buf, n_slots, rows, tile, name, stride=1):
    nt = rows // tile

    def fn(i, j, *vals):
        acc = vals[0]
        for v in vals[1:]:
            acc = acc + v
        return (acc,)

    ins = [_In(buf, roff=k * stride * nt) for k in range(n_slots)]
    return _rowcall(name, fn, nt, tile, ins, [_Out(buf.shape[1])])[0]


ADAM_LR, ADAM_B1, ADAM_B2, ADAM_EPS, ADAM_WD, ADAM_STEP = 0.001, 0.9, 0.999, 1e-08, 0.01, 10


def _row_tile(rows, cols):
    for t in (512, 256, 128, 64, 32, 16, 8):
        if rows % t == 0 and t * cols * 4 * 14 <= 40 * 1024 * 1024:
            return t
    return rows


def _adamw(w, g, m, v, name):
    shape = w.shape
    cols = shape[-1]
    rows = max(1, math.prod(shape[:-1]))
    tile = _row_tile(rows, cols)
    c1 = 1.0 / (1.0 - ADAM_B1 ** ADAM_STEP)
    c2 = 1.0 / (1.0 - ADAM_B2 ** ADAM_STEP)

    def fn(i, j, w_, g_, m_, v_):
        mn = ADAM_B1 * m_ + (1.0 - ADAM_B1) * g_
        vn = ADAM_B2 * v_ + (1.0 - ADAM_B2) * (g_ * g_)
        delta = -ADAM_LR * ((mn * c1) / (jnp.sqrt(vn * c2) + ADAM_EPS) + ADAM_WD * w_)
        return delta, mn, vn

    r2 = lambda a: a.reshape(rows, cols)
    outs = _rowcall(name, fn, rows // tile, tile, [_In(r2(w)), _In(r2(g)), _In(r2(m)), _In(r2(v))],
                    [_Out(cols), _Out(cols), _Out(cols)])
    return [o.reshape(shape) for o in outs]


MESH = pl.DeviceIdType.MESH
ANY = pl.BlockSpec(memory_space=pl.ANY)


def _pos():
    return lax.axis_index("x"), lax.axis_index("y"), lax.axis_index("c")


def _all_gather_many(blks, name):
    na = len(blks)

    def body(*refs):
        x_refs, out_refs = refs[:na], refs[na:2 * na]
        send_sems, recv_sems, local_sems = refs[2 * na:]
        x, y, c = _pos()
        me, sibling = (x, y, c), (x, y, 1 - c)
        chips = [(1 - x, y), (x, 1 - y), (1 - x, 1 - y)]

        def rows(a, px, py, pc):
            m_per = blks[a].shape[0]
            return out_refs[a].at[pl.ds(pl.multiple_of((4 * px + 2 * py + pc) * m_per, 8), m_per), :]

        def copy(a, k, block, to, src=None):
            return pltpu.make_async_remote_copy(
                src_ref=rows(a, *block) if src is None else src, dst_ref=rows(a, *block),
                send_sem=send_sems.at[7 * a + k], recv_sem=recv_sems.at[7 * a + k], device_id=to, device_id_type=MESH)

        every = range(na)
        mine = [pltpu.make_async_copy(x_refs[a], rows(a, *me), local_sems.at[a]) for a in every]
        for cp in mine:
            cp.start()
        first = [copy(a, 0, me, sibling, src=x_refs[a]) for a in every]
        first += [copy(a, 1 + j, me, (*chip, c), src=x_refs[a]) for j, chip in enumerate(chips) for a in every]
        for cp in first:
            cp.start()
        passed = []
        for j, chip in enumerate(chips):
            for a in every:
                copy(a, 1 + j, (*chip, c), me).wait_recv()
                passed.append(copy(a, 4 + j, (*chip, c), sibling))
                passed[-1].start()
        for a in every:
            copy(a, 0, sibling, me).wait_recv()
        for j, chip in enumerate(chips):
            for a in every:
                copy(a, 4 + j, (*chip, 1 - c), me).wait_recv()
        for cp in first + passed:
            cp.wait_send()
        for cp in mine:
            cp.wait()

    return pl.pallas_call(
        body, name=name,
        out_shape=[jax.ShapeDtypeStruct((N_DEV * b.shape[0], b.shape[1]), b.dtype) for b in blks],
        in_specs=[ANY] * na, out_specs=[ANY] * na,
        scratch_shapes=[pltpu.SemaphoreType.DMA((7 * na,)), pltpu.SemaphoreType.DMA((7 * na,)), pltpu.SemaphoreType.DMA((na,))],
        compiler_params=pltpu.CompilerParams(has_side_effects=True),
    )(*blks)


def _all_gather(blk, name):
    return _all_gather_many([blk], name)[0]


def _flip(v, bit):
    return 1 - v if bit else v


D2D_STREAMS = 8
ICI_STREAMS = 2


def _sibling_exchange(src, seg_rows, n_seg, paired, name):
    n = src.shape[1]
    per_seg = D2D_STREAMS // n_seg
    per = seg_rows // per_seg
    assert per_seg * n_seg == D2D_STREAMS and per * per_seg == seg_rows and per % 16 == 0

    def body(x_ref, out_ref, send_sems, recv_sems):
        x, y, c = _pos()
        copies = []
        for s in range(n_seg):
            base = (2 * s + (1 - c)) * seg_rows if paired else s * seg_rows
            for j in range(per_seg):
                i = s * per_seg + j
                cp = pltpu.make_async_remote_copy(
                    src_ref=x_ref.at[pl.ds(pl.multiple_of(base + j * per, 16), per), :],
                    dst_ref=out_ref.at[pl.ds(s * seg_rows + j * per, per), :],
                    send_sem=send_sems.at[i], recv_sem=recv_sems.at[i], device_id=(x, y, 1 - c), device_id_type=MESH)
                cp.start()
                copies.append(cp)
        for cp in copies:
            cp.wait_recv()
        for cp in copies:
            cp.wait_send()

    return pl.pallas_call(
        body, name=name, out_shape=jax.ShapeDtypeStruct((n_seg * seg_rows, n), src.dtype),
        in_specs=[ANY], out_specs=ANY,
        scratch_shapes=[pltpu.SemaphoreType.DMA((D2D_STREAMS,)), pltpu.SemaphoreType.DMA((D2D_STREAMS,))],
        compiler_params=pltpu.CompilerParams(has_side_effects=True),
    )(src)


def _transpose_cast(x, dtype, name):
    r, c = x.shape
    tc = 512

    def body(x_ref, o_ref):
        o_ref[...] = x_ref[...].T.astype(o_ref.dtype)

    return pl.pallas_call(
        body, name=name, grid=(c // tc,),
        in_specs=[pl.BlockSpec((r, tc), lambda j: (0, j))], out_specs=pl.BlockSpec((tc, r), lambda j: (j, 0)),
        out_shape=jax.ShapeDtypeStruct((c, r), dtype), compiler_params=_cparams(("parallel",)),
    )(x)


def _chip_exchange(buf, rows, name):
    n = buf.shape[1]
    per = rows // ICI_STREAMS
    assert per * ICI_STREAMS == rows and per % 16 == 0

    def body(x_ref, out_ref, send_sems, recv_sems):
        x, y, c = _pos()
        copies = []
        for k in range(1, 4):
            px, py = _flip(x, k & 2), _flip(y, k & 1)
            for j in range(ICI_STREAMS):
                i = (k - 1) * ICI_STREAMS + j
                cp = pltpu.make_async_remote_copy(
                    src_ref=x_ref.at[pl.ds(pl.multiple_of((2 * px + py) * rows + j * per, 16), per), :],
                    dst_ref=out_ref.at[pl.ds((k - 1) * rows + j * per, per), :],
                    send_sem=send_sems.at[i], recv_sem=recv_sems.at[i], device_id=(px, py, c), device_id_type=MESH)
                cp.start()
                copies.append(cp)
        for cp in copies:
            cp.wait_recv()
        for cp in copies:
            cp.wait_send()

    return pl.pallas_call(
        body, name=name, out_shape=jax.ShapeDtypeStruct((3 * rows, n), buf.dtype),
        in_specs=[ANY], out_specs=ANY,
        scratch_shapes=[pltpu.SemaphoreType.DMA((3 * ICI_STREAMS,)), pltpu.SemaphoreType.DMA((3 * ICI_STREAMS,))],
        compiler_params=pltpu.CompilerParams(has_side_effects=True),
    )(buf)


def _add_rows(parts, rows, dtype, name):
    tile = 1024
    ins = [_In(a, roff=r0 // tile) for a, r0 in parts]

    def fn(i, j, *vals):
        acc = vals[0].astype(F32)
        for v_ in vals[1:]:
            acc = acc + v_.astype(F32)
        return (acc,)

    return _rowcall(name, fn, rows // tile, tile, ins, [_Out(parts[0][0].shape[1], dtype)])[0]


BIG = ("w_in", "w_branch_dn", "w_branch_attn", "w_out", "ffn_up", "ffn_down")
BIG_SHARD = {"w_in": (1024, 1928, True), "w_branch_dn": (256, 1024, False), "w_branch_attn": (256, 1024, False),
             "w_out": (256, 1024, False), "ffn_up": (1024, 1408, True), "ffn_down": (704, 1024, False)}
BIG_ROWS = {k: r * c // 2 // 128 for k, (r, c, _) in BIG_SHARD.items()}
PIECE = 19456
assert sum(BIG_ROWS.values()) <= PIECE


def _gather_weights(shards, ci):
    halves = []
    for k in BIG:
        r, c, _ = BIG_SHARD[k]
        halves.append(lax.dynamic_slice_in_dim(shards[k], ci * (r // 2), r // 2, axis=0).astype(BF16))
    out = {}
    for k, ag in zip(BIG, _all_gather_many(halves, "ag_weights")):
        r, c, by_col = BIG_SHARD[k]
        blk = ag.reshape(4, r, c)
        out[k] = jnp.transpose(blk, (1, 0, 2)).reshape(r, 4 * c) if by_col else blk.reshape(4 * r, c)
    return out


def _pack_pieces(full):
    parts = [full["w_in_t"].reshape(N_DEV, BIG_ROWS["w_in"], 128).astype(BF16)]
    for k in BIG[1:]:
        r, c, by_col = BIG_SHARD[k]
        a = full[k]
        if by_col:
            a = jnp.transpose(a.reshape(r, 4, c), (1, 0, 2))
        parts.append(a.reshape(N_DEV, BIG_ROWS[k], 128).astype(BF16))
    parts.append(jnp.zeros((N_DEV, PIECE - sum(BIG_ROWS.values()), 128), BF16))
    return jnp.concatenate(parts, axis=1).reshape(N_DEV * PIECE, 128)


def _reduce_scatter(pieces, ci, shard):
    half = N_DEV // 2 * PIECE
    theirs = _sibling_exchange(pieces, PIECE, N_DEV // 2, True, "rs_d2d")
    own = lax.dynamic_index_in_dim(pieces.reshape(N_DEV // 2, 2, PIECE, 128), ci, axis=1, keepdims=False).reshape(half, 128)
    part = _add_rows([(own, 0), (theirs, 0)], half, BF16, "rs_sum_chip")
    recv = _chip_exchange(part, PIECE, "rs_ici")
    own2 = lax.dynamic_slice_in_dim(part, shard * PIECE, PIECE, axis=0)
    mine = _add_rows([(own2, 0), (recv, 0), (recv, PIECE), (recv, 2 * PIECE)], PIECE, F32, "rs_sum_all")
    other = _sibling_exchange(mine, PIECE, 1, False, "rs_pair")
    return jnp.where(ci == 0, jnp.stack([mine, other]), jnp.stack([other, mine]))


def _unpack_shard(two):
    out, off = {}, 0
    for k in BIG:
        r, c, _ = BIG_SHARD[k]
        blk = two[:, off:off + BIG_ROWS[k]]
        out[k] = blk.reshape(c, r).T if k == "w_in" else blk.reshape(r, c)
        off += BIG_ROWS[k]
    return out


SMALL = (("dn_conv", 120), ("ffn_conv", 132), ("ffn_conv_b", 44), ("norm_mix", 8), ("norm_ffn", 8), ("dn_a_log", 1),
         ("dn_dt_bias", 1), ("dn_norm", 1), ("q_norm", 1), ("k_norm", 1), ("attn_sink", 1), ("dmod_c", 48), ("dmod_x", 48))
SMALL_ROWS = 416


def _rows128(a, rows):
    flat = a.reshape(-1)
    return jnp.concatenate([flat, jnp.zeros((rows * 128 - flat.shape[0],), F32)]).reshape(rows, 128)


def _pack_small(g):
    parts = [_rows128(g[k], r) for k, r in SMALL]
    parts.append(jnp.zeros((SMALL_ROWS - sum(r for _, r in SMALL), 128), F32))
    return jnp.concatenate(parts, axis=0)


def _unpack_small(buf, shapes):
    out, off = {}, 0
    for k, r in SMALL:
        n = math.prod(shapes[k])
        out[k] = buf[off:off + r].reshape(-1)[:n].reshape(shapes[k])
        off += r
    return out


WEIGHTS = ("c_ctx", "w_ada", "b_ada", "norm_mix", "norm_ffn", "w_in", "dn_conv", "dn_a_log", "dn_dt_bias", "dn_norm",
           "q_norm", "k_norm", "attn_sink", "w_branch_dn", "w_branch_attn", "w_out", "ffn_up", "ffn_conv", "ffn_conv_b",
           "ffn_down")


def kernel(x, c, ctx, c_ctx, w_ada, b_ada, norm_mix, norm_ffn, w_in, dn_conv, dn_a_log, dn_dt_bias, dn_norm, q_norm, k_norm, attn_sink, w_branch_dn, w_branch_attn, w_out, ffn_up, ffn_conv, ffn_conv_b, ffn_down, loss_target, m_c_ctx, m_w_ada, m_b_ada, m_norm_mix, m_norm_ffn, m_w_in, m_dn_conv, m_dn_a_log, m_dn_dt_bias, m_dn_norm, m_q_norm, m_k_norm, m_attn_sink, m_w_branch_dn, m_w_branch_attn, m_w_out, m_ffn_up, m_ffn_conv, m_ffn_conv_b, m_ffn_down, v_c_ctx, v_w_ada, v_b_ada, v_norm_mix, v_norm_ffn, v_w_in, v_dn_conv, v_dn_a_log, v_dn_dt_bias, v_dn_norm, v_q_norm, v_k_norm, v_attn_sink, v_w_branch_dn, v_w_branch_attn, v_w_out, v_ffn_up, v_ffn_conv, v_ffn_conv_b, v_ffn_down):
    args = dict(locals())
    xi, yi, ci = _pos()
    dev = 4 * xi + 2 * yi + ci
    shard = 2 * xi + yi
    chips = lambda a: a[0::2]

    blk = jnp.concatenate([_rows128(c, 8), _rows128(dn_conv, 30), _rows128(ffn_conv, 33), jnp.zeros((1, 128), F32)], axis=0)
    ag = _all_gather(blk, "ag_small_in").reshape(N_DEV, 72, 128)
    c_all = ag[:, 0:8].reshape(N_DEV, D)
    dn_conv_full = jnp.transpose(chips(ag)[:, 8:38].reshape(4, 5, 768), (1, 0, 2)).reshape(5, 3 * D)
    ffn_conv_full = jnp.transpose(chips(ag)[:, 38:71].reshape(4, 3, 1408), (1, 0, 2)).reshape(3, 2 * DFF)

    c16 = jnp.concatenate([c_all, c_ctx[None], jnp.zeros((7, D), F32)], axis=0)
    a16 = _rowcall("ada_silu", lambda i, j, v: (_silu(v),), 1, 16, [_In(c16)], [_Out(D)])[0]
    m_sh = _mm(a16, w_ada[0], tm=16, tn=512, tk=D, name="ada_fwd", hi=True)
    mod16 = chips(_all_gather(m_sh, "ag_mod").reshape(N_DEV, 16, 1536))
    mod16 = jnp.transpose(mod16, (1, 0, 2)).reshape(16, 6 * D) + b_ada
    mod_x = lax.dynamic_slice_in_dim(mod16, dev, 1, axis=0)
    mod_c = mod16[8:9]

    shards = {k: args[k][0] for k in BIG}
    wfull = _gather_weights(shards, ci)
    w = dict(wfull)
    w["w_in_p"] = _pad_w_in(wfull["w_in"])
    w.update(norm_mix=norm_mix, norm_ffn=norm_ffn, dn_conv=dn_conv_full, dn_a_log=dn_a_log[0], dn_dt_bias=dn_dt_bias[0],
             dn_norm=dn_norm, q_norm=q_norm, k_norm=k_norm, attn_sink=attn_sink, ffn_conv=ffn_conv_full, ffn_conv_b=ffn_conv_b)

    loss_part, grad_x, g, dmod_x, dmod_c = _local_step(x[0], ctx[0], loss_target[0], mod_x, mod_c, w)
    loss = lax.psum(loss_part[0, 0], ("x", "y", "c"))

    g["w_in_t"] = _unpad_w_in(_transpose_cast(g["w_in_p"], BF16, "w_in_grad_t"), axis=0)
    gshard = _unpack_shard(_reduce_scatter(_pack_pieces(g), ci, shard))

    g["dmod_c"], g["dmod_x"] = dmod_c, dmod_x
    ag_s = _all_gather(_pack_small(g), "ag_small_grads")
    shapes = {k: g[k].shape for k, _ in SMALL}
    gs = _unpack_small(_sum_slots(ag_s, N_DEV, SMALL_ROWS, SMALL_ROWS, "small_sum"), shapes)
    dx_all = ag_s.reshape(N_DEV, SMALL_ROWS, 128)[:, SMALL_ROWS - 50:SMALL_ROWS - 2].reshape(N_DEV, 6 * D)

    d16 = jnp.concatenate([dx_all, gs["dmod_c"], jnp.zeros((7, 6 * D), F32)], axis=0)
    d16_sh = lax.dynamic_slice_in_dim(d16, shard * 1536, 1536, axis=1)
    g_w_ada = _mm(a16, d16_sh, ta=True, tm=D, tn=512, tk=16, name="ada_dw", hi=True)
    g_b_ada = _rowcall("ada_db", lambda i, j, v: (_colsum(v),), 1, 16, [_In(d16)], [_Out(6 * D, acc=True)])[0]
    da_part = _mm(d16_sh, w_ada[0], tb=True, tm=16, tn=D, tk=512, name="ada_dx", hi=True)
    da_all = _all_gather(da_part, "ag_ada_dx")
    da16 = _sum_slots(da_all, 4, 16, 16, "ada_dx_sum", stride=2)
    dc16 = _rowcall("ada_dsilu", lambda i, j, d_, v: (d_ * _dsilu(v),), 1, 16, [_In(da16), _In(c16)], [_Out(D)])[0]

    grads = {
        "c_ctx": dc16[8], "w_ada": g_w_ada[None], "b_ada": g_b_ada, "norm_mix": gs["norm_mix"], "norm_ffn": gs["norm_ffn"],
        "w_in": gshard["w_in"][None],
        "dn_conv": lax.dynamic_slice_in_dim(gs["dn_conv"], shard * 768, 768, axis=1)[None],
        "dn_a_log": gs["dn_a_log"][None], "dn_dt_bias": gs["dn_dt_bias"][None], "dn_norm": gs["dn_norm"],
        "q_norm": gs["q_norm"], "k_norm": gs["k_norm"], "attn_sink": gs["attn_sink"],
        "w_branch_dn": gshard["w_branch_dn"][None], "w_branch_attn": gshard["w_branch_attn"][None],
        "w_out": gshard["w_out"][None], "ffn_up": gshard["ffn_up"][None],
        "ffn_conv": lax.dynamic_slice_in_dim(gs["ffn_conv"], shard * 1408, 1408, axis=1)[None],
        "ffn_conv_b": gs["ffn_conv_b"], "ffn_down": gshard["ffn_down"][None],
    }
    deltas, new_m, new_v = [], [], []
    for k in WEIGHTS:
        d_, m_, v_ = _adamw(args[k], grads[k], args["m_" + k], args["v_" + k], "adamw_" + k)
        deltas.append(d_)
        new_m.append(m_)
        new_v.append(v_)
    return (loss, grad_x[None], *[grads[k] for k in WEIGHTS], *deltas, *new_m, *new_v)
```

```python
import functools
import math

import numpy as np
import jax
import jax.numpy as jnp
from jax import lax
from jax.experimental import pallas as pl
from jax.experimental.pallas import tpu as pltpu

F32 = jnp.float32
BF16 = jnp.bfloat16
HI = lax.Precision.HIGHEST

D = 1024
NH = 8
HD = 128
CH = 64
CTX = 256
AB = 128
KVH = 2
GRP = 4
DFF = 2816
EPS = 1e-6
GRID_W = 64
ROPE_BASE = 10000.0
N_DEV = 8
VMEM_LIMIT = 56 * 1024 * 1024

C_QKV, C_KAT, C_VAT, C_BA, C_PAD, C_GT, C_QAT, C_MG = 0, 3072, 3328, 3584, 3712, 4096, 5120, 6144
PW = 8192
PH = PW // 2


def _cparams(sem=None, **kw):
    return pltpu.CompilerParams(dimension_semantics=sem, vmem_limit_bytes=VMEM_LIMIT, **kw)


def _dot(a, b, dims, hi):
    if hi:
        return lax.dot_general(a.astype(F32), b.astype(F32), (dims, ((), ())), precision=HI, preferred_element_type=F32)
    return lax.dot_general(a.astype(BF16), b.astype(BF16), (dims, ((), ())), preferred_element_type=F32)


NN = ((1,), (0,))
NT = ((1,), (1,))
TN = ((0,), (0,))


def _dn_masks():
    i = np.arange(CH)
    lo_incl = (i[:, None] >= i[None, :]).astype(np.float32)
    lo_strict = (i[:, None] > i[None, :]).astype(np.float32)
    return jnp.asarray(np.stack([np.stack([lo_incl, lo_strict]), np.stack([lo_incl.T, lo_strict.T])]))


def _dn_chunk_index(d, i, n_ctx_chunks, n_chunks):
    fwd = i
    bwd = jnp.where(i < n_ctx_chunks, n_ctx_chunks - 1 - i, n_chunks - 1 + n_ctx_chunks - i)
    return jnp.where(d == 0, fwd, bwd)


BNN = ((2,), (1,))
BNT = ((2,), (2,))
BTN = ((1,), (1,))


def _bdot(a, b, dims, hi):
    dn = (dims, ((0,), (0,)))
    if hi:
        return lax.dot_general(a.astype(F32), b.astype(F32), dn, precision=HI, preferred_element_type=F32)
    return lax.dot_general(a.astype(BF16), b.astype(BF16), dn, preferred_element_type=F32)


def _bdot3(a, b, dims, hi):
    if hi:
        return _bdot(a, b, dims, True)
    ah, bh = a.astype(BF16), b.astype(BF16)
    al, bl = (a - ah.astype(F32)).astype(BF16), (b - bh.astype(F32)).astype(BF16)
    dn = (dims, ((0,), (0,)))
    d = lambda x_, y_: lax.dot_general(x_, y_, dn, preferred_element_type=F32)
    return d(ah, bh) + d(ah, bl) + d(al, bh)


DN_CB = 4
DN_SEQ_CB = 4


def _dn_heads(ref, cb=1):
    return jnp.stack([ref[t * CH:(t + 1) * CH, h * HD:(h + 1) * HD] for t in range(cb) for h in range(NH)])


def _dn_scalars(gb, mi, cb=1):
    beta, gc, gcr, gt = [], [], [], []
    for t in range(cb):
        g1 = gb[t * CH:(t + 1) * CH]
        gcum, gcum_t, gtot = _dn_gcum(g1, mi)
        beta += [g1[:, h:h + 1] for h in range(NH)]
        gc += [gcum[:, NH + h:NH + h + 1] for h in range(NH)]
        gcr += [gcum_t[NH + h:NH + h + 1, :] for h in range(NH)]
        gt += [gtot[:, NH + h:NH + h + 1] for h in range(NH)]
    return jnp.stack(beta), jnp.stack(gc), jnp.stack(gcr), jnp.stack(gt)


DN_NEWTON = 1


def _dn_inverse(a, hi):
    eye = (lax.broadcasted_iota(jnp.int32, (CH, CH), 0) == lax.broadcasted_iota(jnp.int32, (CH, CH), 1)).astype(F32)
    x = -a
    t = eye + x
    p = x
    if hi:
        for _ in range(5):
            p = _bdot(p, p, BNN, True)
            t = t + _bdot(t, p, BNN, True)
        return t
    for _ in range(5):
        p = _bdot(p, p, BNN, False)
        t = t + _bdot(t, p, BNN, False)
    for _ in range(DN_NEWTON):
        r = eye - t - _bdot3(a, t, BNN, False)
        t = t + _bdot(t, r, BNN, False)
    return t


def _dn_total(gb):
    gtot = jnp.sum(gb, axis=0, keepdims=True)
    return jnp.stack([gtot[:, NH + h:NH + h + 1] for h in range(NH)])


def _dn_gcum(gb, mi):
    gcum = _dot(mi, gb, NN, True)
    gtot = jnp.sum(gb, axis=0, keepdims=True)
    return gcum, gcum.T, gtot


def _dn_specs(n_ctx_chunks, n_chunks, reverse, cb):
    assert n_ctx_chunks % cb == 0 and n_chunks % cb == 0

    def grp(d, i):
        first = n_chunks - 1 - cb * i if reverse else cb * i
        return _dn_chunk_index(d, first, n_ctx_chunks, n_chunks) // cb

    def slot(d, t):
        ascending = (d == 1) if reverse else (d == 0)
        return jnp.where(ascending, t, cb - 1 - t)

    ctx_groups = n_ctx_chunks // cb
    tok_lat = pl.BlockSpec((cb * CH, D), lambda d, i: (jnp.maximum(grp(d, i) - ctx_groups, 0), 0))
    is_ctx = lambda d, i: grp(d, i) < ctx_groups
    tok_d = pl.BlockSpec((1, cb * CH, D), lambda d, i: (d, grp(d, i), 0))
    gbs = pl.BlockSpec((1, cb * CH, 128), lambda d, i: (d, grp(d, i), 0))

    def per_chunk(*tail):
        return pl.BlockSpec((1, cb) + tail, lambda d, i: (d, grp(d, i)) + (0,) * len(tail))

    return tok_lat, is_ctx, tok_d, gbs, per_chunk, slot


def _dn_group_specs(cb):
    tok = pl.BlockSpec((cb * CH, D), lambda d, i: (i, 0))
    tok_d = pl.BlockSpec((1, cb * CH, D), lambda d, i: (d, i, 0))
    gbs = pl.BlockSpec((1, cb * CH, 128), lambda d, i: (d, i, 0))
    msk = pl.BlockSpec((1, 2, CH, CH), lambda d, i: (d, 0, 0, 0))

    def per_chunk(*tail):
        return pl.BlockSpec((1, cb) + tail, lambda d, i: (d, i) + (0,) * len(tail))

    return tok, tok_d, gbs, msk, per_chunk


def _dn_intra_fwd(q, k, v, gb, n_ctx_chunks, hi):
    t_all = q.shape[0]
    n_chunks = t_all // CH
    masks = _dn_masks()

    cb = DN_CB

    def put(ref, val):
        for t_ in range(cb):
            ref[0, t_] = val[t_ * NH:(t_ + 1) * NH].astype(ref.dtype)

    def body(q_ref, k_ref, v_ref, gb_ref, m_ref, u_ref, w_ref, qg_ref, kd_ref, pm_ref, t_ref):
        mi, ms = m_ref[0, 0], m_ref[0, 1]
        beta, gc, gcr, gt = _dn_scalars(gb_ref[0], mi, cb)
        q_, k_, v_ = _dn_heads(q_ref, cb), _dn_heads(k_ref, cb), _dn_heads(v_ref, cb)
        decay = jnp.exp(jnp.where(mi > 0, gc - gcr, 0.0)) * mi
        e = jnp.exp(gc)
        a = ms * (beta * _bdot(k_, k_, BNT, hi) * decay)
        t = _dn_inverse(a, hi)
        uw =_bdot(t, jnp.concatenate([beta * v_, (beta * e) * k_], axis=2), BNN, hi)
        put(u_ref, uw[:, :, :HD])
        put(w_ref, uw[:, :, HD:])
        put(qg_ref, e * q_)
        put(kd_ref, jnp.exp(gt - gc) * k_)
        put(pm_ref, _bdot(q_, k_, BNT, hi) * decay)
        put(t_ref, t)

    tok, _, gbs, msk, per_chunk = _dn_group_specs(cb)
    big = lambda dt: jax.ShapeDtypeStruct((2, n_chunks, NH, CH, HD), dt)
    sq = jax.ShapeDtypeStruct((2, n_chunks, NH, CH, CH), BF16)
    return pl.pallas_call(
        body, name="dn_intra_fwd", grid=(2, n_chunks // cb),
        in_specs=[tok, tok, tok, gbs, msk],
        out_specs=[per_chunk(NH, CH, HD)] * 4 + [per_chunk(NH, CH, CH)] * 2,
        out_shape=[big(F32), big(BF16), big(BF16), big(BF16), sq, sq],
        compiler_params=_cparams(("parallel", "parallel")),
    )(q, k, v, gb, masks)


def _dn_seq_fwd(u, w, qg, kd, pm, gb, n_ctx_chunks, hi):
    n_chunks = u.shape[1]
    t_all = n_chunks * CH

    cb = DN_SEQ_CB
    _, _, tok_d, gbs, per_chunk, slot = _dn_specs(n_ctx_chunks, n_chunks, False, cb)

    def body(u_ref, w_ref, qg_ref, kd_ref, pm_ref, gb_ref, o_ref, sh_ref, vn_ref, s_scr):
        @pl.when(pl.program_id(1) == 0)
        def _():
            s_scr[...] = jnp.zeros_like(s_scr)

        for t in range(cb):
            j = slot(pl.program_id(0), t)
            rows = pl.ds(pl.multiple_of(j * CH, CH), CH)
            s = s_scr[...]
            sh_ref[0, j] = s.astype(sh_ref.dtype)
            vn = u_ref[0, j] - _bdot(w_ref[0, j], s, BNN, hi)
            o = _bdot(qg_ref[0, j], s, BNN, hi) + _bdot(pm_ref[0, j], vn, BNN, hi)
            s_scr[...] = jnp.exp(_dn_total(gb_ref[0, rows, :])) * s + _bdot(kd_ref[0, j], vn, BTN, hi)
            vn_ref[0, j] = vn.astype(vn_ref.dtype)
            for h in range(NH):
                o_ref[0, rows, h * HD:(h + 1) * HD] = o[h]

    big = per_chunk(NH, CH, HD)
    return pl.pallas_call(
        body, name="dn_seq_fwd", grid=(2, n_chunks // cb),
        in_specs=[big, big, big, big, per_chunk(NH, CH, CH), gbs],
        out_specs=[tok_d, per_chunk(NH, HD, HD), big],
        out_shape=[jax.ShapeDtypeStruct((2, t_all, D), F32), jax.ShapeDtypeStruct((2, n_chunks, NH, HD, HD), BF16),
                   jax.ShapeDtypeStruct((2, n_chunks, NH, CH, HD), BF16)],
        scratch_shapes=[pltpu.VMEM((NH, HD, HD), F32)],
        compiler_params=_cparams(("parallel", "arbitrary")),
    )(u, w, qg, kd, pm, gb)


def _dn_seq_bwd(w, qg, kd, pm, vn, s_hist, gb, do, n_ctx_chunks, hi):
    n_chunks = w.shape[1]

    cb = DN_SEQ_CB
    tok_lat, is_ctx, _, gbs, per_chunk, slot = _dn_specs(n_ctx_chunks, n_chunks, True, cb)

    def body(w_ref, qg_ref, kd_ref, pm_ref, vn_ref, sh_ref, gb_ref, do_ref, dvn_ref, dw_ref, dqg_ref, dkd_ref, del_ref, ds_scr):
        @pl.when(pl.program_id(1) == 0)
        def _():
            ds_scr[...] = jnp.zeros_like(ds_scr)

        for t in range(cb):
            j = slot(pl.program_id(0), t)
            rows = pl.ds(pl.multiple_of(j * CH, CH), CH)
            dsn = ds_scr[...]
            s = sh_ref[0, j]
            do_ = jnp.stack([do_ref[rows, h * HD:(h + 1) * HD] for h in range(NH)])
            do_ = jnp.where(is_ctx(pl.program_id(0), pl.program_id(1)), 0.0, do_)
            dvn =_bdot(pm_ref[0, j], do_, BTN, hi) + _bdot(kd_ref[0, j], dsn, BNN, hi)
            ds_scr[...] = (_bdot(qg_ref[0, j], do_, BTN, hi) + jnp.exp(_dn_total(gb_ref[0, rows, :])) * dsn
                           - _bdot(w_ref[0, j], dvn, BTN, hi))
            dvn_ref[0, j] = dvn.astype(dvn_ref.dtype)
            dw_ref[0, j] = (-_bdot(dvn, s, BNT, hi)).astype(dw_ref.dtype)
            dqg_ref[0, j] = _bdot(do_, s, BNT, hi)
            dkd_ref[0, j] = _bdot(vn_ref[0, j], dsn, BNT, hi)
            del_ref[0, j] = jnp.broadcast_to(jnp.sum(jnp.sum(s * dsn, axis=2, keepdims=True), axis=1, keepdims=True),
                                             (NH, 1, 128))

    big = per_chunk(NH, CH, HD)
    shp = lambda dt: jax.ShapeDtypeStruct((2, n_chunks, NH, CH, HD), dt)
    return pl.pallas_call(
        body, name="dn_seq_bwd", grid=(2, n_chunks // cb),
        in_specs=[big, big, big, per_chunk(NH, CH, CH), big, per_chunk(NH, HD, HD), gbs, tok_lat],
        out_specs=[big, big, big, big, per_chunk(NH, 1, 128)],
        out_shape=[shp(BF16), shp(BF16), shp(F32), shp(F32), jax.ShapeDtypeStruct((2, n_chunks, NH, 1, 128), F32)],
        scratch_shapes=[pltpu.VMEM((NH, HD, HD), F32)],
        compiler_params=_cparams(("parallel", "arbitrary")),
    )(w, qg, kd, pm, vn, s_hist, gb, do)


def _dn_intra_bwd(q, k, v, gb, u, w, t, vn, dvn, dw, dqg, dkd, de_last, do, n_ctx_chunks, hi):
    t_all = q.shape[0]
    n_chunks = t_all // CH
    masks = _dn_masks()

    cb = DN_CB
    assert n_ctx_chunks % cb == 0
    ctx_groups = n_ctx_chunks // cb

    def body(q_ref, k_ref, v_ref, gb_ref, m_ref, u_ref, w_ref, t_ref, vn_ref, dvn_ref, dw_ref, dqg_ref, dkd_ref, del_ref,
             do_ref, dq_ref, dk_ref, dv_ref, dgb_ref):
        mi, ms = m_ref[0, 0], m_ref[0, 1]
        beta, gc, gcr, gt = _dn_scalars(gb_ref[0], mi, cb)
        q_, k_, v_ = _dn_heads(q_ref, cb), _dn_heads(k_ref, cb), _dn_heads(v_ref, cb)
        do_ = jnp.where(pl.program_id(1) < ctx_groups, 0.0, _dn_heads(do_ref, cb))
        get = lambda ref: jnp.concatenate([ref[0, t_] for t_ in range(cb)], axis=0)
        decay = jnp.exp(jnp.where(mi > 0, gc - gcr, 0.0)) * mi
        e = jnp.exp(gc)
        e_last = jnp.exp(gt)
        kdfac = jnp.exp(gt - gc)
        kk = _bdot(k_, k_, BNT, hi)
        a = ms * (beta * kk * decay)
        pm = _bdot(q_, k_, BNT, hi) * decay
        kd = kdfac * k_
        dqg, dkd = get(dqg_ref), get(dkd_ref)
        dpm = _bdot(do_, get(vn_ref), BNT, hi)
        dvbkb = _bdot(get(t_ref), jnp.concatenate([get(dvn_ref), get(dw_ref)], axis=2), BTN, hi)
        dvb, dkb = dvbkb[:, :, :HD], dvbkb[:, :, HD:]
        da = -ms * _bdot(dvbkb, jnp.concatenate([get(u_ref), get(w_ref).astype(F32)], axis=2), BNT, hi)
        dqk = dpm * decay
        gm = dpm * pm + da * a
        dgc = (jnp.sum(gm, axis=2, keepdims=True)
               - _bdot3(gm, jnp.ones((cb * NH, CH, 128), F32), BTN, hi)[:, :, 0:1])
        dkk = da * (beta * decay)
        dbeta = jnp.sum(da * kk * decay, axis=2, keepdims=True)
        dk = _bdot(dkk, k_, BNN, hi) + _bdot(dkk, k_, BTN, hi) + _bdot(dqk, q_, BTN, hi)
        dq = _bdot(dqk, k_, BNN, hi) + e * dqg
        de = jnp.sum(dqg * q_, axis=2, keepdims=True)
        dv = beta * dvb
        dbeta = dbeta + jnp.sum(dvb * v_, axis=2, keepdims=True)
        skb = jnp.sum(dkb * k_, axis=2, keepdims=True)
        dk = dk + (beta * e) * dkb + kdfac * dkd
        dbeta = dbeta + e * skb
        de = de + beta * skb
        skd = jnp.sum(dkd * kd, axis=2, keepdims=True)
        dgc = dgc - skd + de * e
        dgtot = jnp.sum(skd, axis=1, keepdims=True) + get(del_ref)[:, :, 0:1] * e_last
        lane = lax.broadcasted_iota(jnp.int32, (1, 128), 1)
        for t_ in range(cb):
            rows = slice(t_ * CH, (t_ + 1) * CH)
            dbeta_all = jnp.zeros((CH, 128), F32)
            dgc_all = jnp.zeros((CH, 128), F32)
            dgtot_all = jnp.zeros((1, 128), F32)
            for h in range(NH):
                sl = slice(h * HD, (h + 1) * HD)
                b = t_ * NH + h
                dq_ref[0, rows, sl] = dq[b]
                dk_ref[0, rows, sl] = dk[b]
                dv_ref[0, rows, sl] = dv[b]
                hot_b = (lane == h).astype(F32)
                hot_g = (lane == NH + h).astype(F32)
                dbeta_all = dbeta_all + dbeta[b] * hot_b
                dgc_all = dgc_all + dgc[b] * hot_g
                dgtot_all = dgtot_all + dgtot[b] * hot_g
            dgb_ref[0, rows, :] = dbeta_all + _dot(mi, dgc_all, TN, True) + dgtot_all

    tok, tok_d, gbs, msk, per_chunk = _dn_group_specs(cb)
    tok_lat = pl.BlockSpec((cb * CH, D), lambda d, i: (jnp.maximum(i - ctx_groups, 0), 0))
    big = per_chunk(NH, CH, HD)
    return pl.pallas_call(
        body, name="dn_intra_bwd", grid=(2, n_chunks // cb),
        in_specs=[tok, tok, tok, gbs, msk, big, big, per_chunk(NH, CH, CH), big, big, big, big, big,
                  per_chunk(NH, 1, 128), tok_lat],
        out_specs=[tok_d, tok_d, tok_d, gbs],
        out_shape=[jax.ShapeDtypeStruct((2, t_all, D), F32)] * 3 + [jax.ShapeDtypeStruct((2, t_all, 128), F32)],
        compiler_params=_cparams(("parallel", "parallel")),
    )(q, k, v, gb, masks, u, w, t, vn, dvn, dw, dqg, dkd, de_last, do)


ATT_SCALE = HD ** -0.5
NEG = -1e30


def _att_stack(ref, kvh):
    return jnp.concatenate([ref[:, (kvh * GRP + g) * HD:(kvh * GRP + g + 1) * HD] for g in range(GRP)], axis=0)


def _att_col(ref, kvh):
    return jnp.concatenate([ref[:, kvh * GRP + g:kvh * GRP + g + 1] for g in range(GRP)], axis=0)


def _att_sink(sink_ref, kvh):
    return jnp.concatenate([jnp.broadcast_to(sink_ref[:, kvh * GRP + g:kvh * GRP + g + 1], (AB, 1)) for g in range(GRP)],
                           axis=0)


def _att_mask(i, nb):
    r = lax.broadcasted_iota(jnp.int32, (AB, AB), 0)
    c = lax.broadcasted_iota(jnp.int32, (AB, AB), 1)
    okp = jnp.logical_and(c >= r, i > 0)
    okn = jnp.logical_and(c <= r, i < nb - 1)
    return jnp.concatenate([okp] * GRP, axis=0), jnp.concatenate([okn] * GRP, axis=0)


def _att_masked(s, mask):
    mp, mn = mask
    return jnp.concatenate([jnp.where(mp, s[:, 0:AB], NEG), s[:, AB:2 * AB], jnp.where(mn, s[:, 2 * AB:3 * AB], NEG),
                            s[:, 3 * AB:]], axis=1)


def _att_kspecs(nb):
    nc = CTX // AB
    return [pl.BlockSpec((AB, KVH * HD), lambda i: (jnp.maximum(i - 1, 0) + nc, 0)),
            pl.BlockSpec((AB, KVH * HD), lambda i: (i + nc, 0)),
            pl.BlockSpec((AB, KVH * HD), lambda i: (jnp.minimum(i + 1, nb - 1) + nc, 0)),
            pl.BlockSpec((CTX, KVH * HD), lambda i: (0, 0))]


def _attn_fwd(qr, kr, vv, sink, hi):
    tl = qr.shape[0]
    nb = tl // AB

    def body(q_ref, kp_ref, kc_ref, kn_ref, kx_ref, vp_ref, vc_ref, vn_ref, vx_ref, sink_ref, o_ref, lse_ref):
        i = pl.program_id(0)
        mask = _att_mask(i, nb)
        lane = lax.broadcasted_iota(jnp.int32, (1, 128), 1)
        lse_all = jnp.zeros((AB, 128), F32)
        for kvh in range(KVH):
            ksl = slice(kvh * HD, (kvh + 1) * HD)
            kall = jnp.concatenate([kp_ref[:, ksl], kc_ref[:, ksl], kn_ref[:, ksl], kx_ref[:, ksl]], axis=0)
            vall = jnp.concatenate([vp_ref[:, ksl], vc_ref[:, ksl], vn_ref[:, ksl], vx_ref[:, ksl]], axis=0)
            s = _dot(_att_stack(q_ref, kvh), kall, NT, hi) * ATT_SCALE
            s = _att_masked(s, mask)
            sk = _att_sink(sink_ref, kvh)
            m = jnp.maximum(jnp.max(s, axis=1, keepdims=True), sk)
            p = jnp.exp(s - m)
            l = jnp.sum(p, axis=1, keepdims=True) + jnp.exp(sk - m)
            o = _dot(p, vall, NN, hi) / l
            lse = m + jnp.log(l)
            for g in range(GRP):
                h = kvh * GRP + g
                o_ref[:, h * HD:(h + 1) * HD] = o[g * AB:(g + 1) * AB]
                lse_all = lse_all + lse[g * AB:(g + 1) * AB] * (lane == h).astype(F32)
        lse_ref[...] = lse_all

    ks = _att_kspecs(nb)
    return pl.pallas_call(
        body, name="attn_fwd", grid=(nb,),
        in_specs=[pl.BlockSpec((AB, D), lambda i: (i, 0))] + ks + ks + [pl.BlockSpec((1, 128), lambda i: (0, 0))],
        out_specs=[pl.BlockSpec((AB, D), lambda i: (i, 0)), pl.BlockSpec((AB, 128), lambda i: (i, 0))],
        out_shape=[jax.ShapeDtypeStruct((tl, D), F32), jax.ShapeDtypeStruct((tl, 128), F32)],
        compiler_params=_cparams(("parallel",)),
    )(qr, kr, kr, kr, kr, vv, vv, vv, vv, sink)


def _mm_bat_dx_delta(dz_at, w_bat, o, hi):
    def fn(i, do_, o_):
        lane = lax.broadcasted_iota(jnp.int32, (1, 128), 1)
        acc = jnp.zeros((do_.shape[0], 128), F32)
        for h in range(NH):
            sl = slice(h * HD, (h + 1) * HD)
            acc = acc + jnp.sum(o_[:, sl] * do_[:, sl], axis=1, keepdims=True) * (lane == h).astype(F32)
        return do_, acc

    return _mm_ep("mm_bat_dx_delta", dz_at, w_bat, True, min(512, o.shape[0]), D, fn, [_In(o)], [_Out(D), _Out(128)], hi)


def _attn_bwd(qr, kr, vv, sink, do, lse, delta, hi):
    tl = qr.shape[0]
    nb = tl // AB
    nc = CTX // AB

    def body(q_ref, kp_ref, kc_ref, kn_ref, kx_ref, vp_ref, vc_ref, vn_ref, vx_ref, sink_ref, do_ref, lse_ref, dl_ref,
             dq_ref, dk_ref, dv_ref, dkx_ref, dvx_ref, dsink_ref, dk_acc, dv_acc):
        i = pl.program_id(0)

        @pl.when(i == 0)
        def _():
            dkx_ref[...] = jnp.zeros_like(dkx_ref)
            dvx_ref[...] = jnp.zeros_like(dvx_ref)
            dsink_ref[...] = jnp.zeros_like(dsink_ref)
            dk_acc[...] = jnp.zeros_like(dk_acc)
            dv_acc[...] = jnp.zeros_like(dv_acc)

        @pl.when(i < nb)
        def _():
            mask = _att_mask(i, nb)
            lane = lax.broadcasted_iota(jnp.int32, (1, 128), 1)
            s_prev, s_cur, s_next = (i + 2) % 3, i % 3, (i + 1) % 3
            dsink = jnp.zeros((1, 128), F32)
            for kvh in range(KVH):
                ksl = slice(kvh * HD, (kvh + 1) * HD)
                kall = jnp.concatenate([kp_ref[:, ksl], kc_ref[:, ksl], kn_ref[:, ksl], kx_ref[:, ksl]], axis=0)
                vall = jnp.concatenate([vp_ref[:, ksl], vc_ref[:, ksl], vn_ref[:, ksl], vx_ref[:, ksl]], axis=0)
                qs = _att_stack(q_ref, kvh)
                dos = _att_stack(do_ref, kvh)
                lse_s = _att_col(lse_ref, kvh)
                dl_s = _att_col(dl_ref, kvh)
                s = _dot(qs, kall, NT, hi) * ATT_SCALE
                p = jnp.exp(_att_masked(s, mask) - lse_s)
                dp = _dot(dos, vall, NT, hi)
                ds = p * (dp - dl_s)
                dq = _dot(ds, kall, NN, hi) * ATT_SCALE
                dk_all = _dot(ds, qs, TN, hi) * ATT_SCALE
                dv_all = _dot(p, dos, TN, hi)
                dkx_ref[:, ksl] += dk_all[3 * AB:]
                dvx_ref[:, ksl] += dv_all[3 * AB:]
                dk_acc[s_prev, :, ksl] += dk_all[0:AB]
                dv_acc[s_prev, :, ksl] += dv_all[0:AB]
                dk_acc[s_cur, :, ksl] += dk_all[AB:2 * AB]
                dv_acc[s_cur, :, ksl] += dv_all[AB:2 * AB]
                dk_acc[s_next, :, ksl] = dk_all[2 * AB:3 * AB]
                dv_acc[s_next, :, ksl] = dv_all[2 * AB:3 * AB]
                psink = jnp.exp(_att_sink(sink_ref, kvh) - lse_s) * dl_s
                for g in range(GRP):
                    h = kvh * GRP + g
                    dq_ref[:, h * HD:(h + 1) * HD] = dq[g * AB:(g + 1) * AB]
                    dsink = dsink - jnp.sum(psink[g * AB:(g + 1) * AB], axis=0, keepdims=True) * (lane == h).astype(F32)
            dsink_ref[...] += dsink

        @pl.when(i >= 1)
        def _():
            dk_ref[...] = dk_acc[(i + 2) % 3]
            dv_ref[...] = dv_acc[(i + 2) % 3]

    blk = lambda i: jnp.minimum(i, nb - 1)
    row = pl.BlockSpec((AB, D), lambda i: (blk(i), 0))
    col = pl.BlockSpec((AB, 128), lambda i: (blk(i), 0))
    ks = [pl.BlockSpec((AB, KVH * HD), lambda i: (jnp.maximum(blk(i) - 1, 0) + nc, 0)),
          pl.BlockSpec((AB, KVH * HD), lambda i: (blk(i) + nc, 0)),
          pl.BlockSpec((AB, KVH * HD), lambda i: (jnp.minimum(i + 1, nb - 1) + nc, 0)),
          pl.BlockSpec((CTX, KVH * HD), lambda i: (0, 0))]
    kv_out = pl.BlockSpec((AB, KVH * HD), lambda i: (jnp.maximum(i - 1, 0), 0))
    ctx_out = pl.BlockSpec((CTX, KVH * HD), lambda i: (0, 0))
    return pl.pallas_call(
        body, name="attn_bwd", grid=(nb + 1,),
        in_specs=[row] + ks + ks + [pl.BlockSpec((1, 128), lambda i: (0, 0)), row, col, col],
        out_specs=[row, kv_out, kv_out, ctx_out, ctx_out, pl.BlockSpec((1, 128), lambda i: (0, 0))],
        out_shape=[jax.ShapeDtypeStruct((tl, D), F32), jax.ShapeDtypeStruct((tl, KVH * HD), F32),
                   jax.ShapeDtypeStruct((tl, KVH * HD), F32), jax.ShapeDtypeStruct((CTX, KVH * HD), F32),
                   jax.ShapeDtypeStruct((CTX, KVH * HD), F32), jax.ShapeDtypeStruct((1, 128), F32)],
        scratch_shapes=[pltpu.VMEM((3, AB, KVH * HD), F32), pltpu.VMEM((3, AB, KVH * HD), F32)],
        compiler_params=_cparams(("arbitrary",)),
    )(qr, kr, kr, kr, kr, vv, vv, vv, vv, sink, do, lse, delta)


def _mm(a, b, ta=False, tb=False, out_dtype=F32, tm=512, tn=1024, tk=1024, name="mm", hi=False):
    a_parts = a.shape[0] if a.ndim == 3 else 0
    b_parts = b.shape[0] if b.ndim == 3 else 0
    assert not (a_parts and ta) and not (b_parts and tb)
    if a_parts:
        m, kd = a.shape[1], a_parts * a.shape[2]
    else:
        m, kd = (a.shape[1], a.shape[0]) if ta else a.shape
    n = b_parts * b.shape[2] if b_parts else (b.shape[0] if tb else b.shape[1])
    tm, tn, tk = min(tm, m), min(tn, n), min(tk, kd)
    assert m % tm == 0 and n % tn == 0 and kd % tk == 0, (name, m, n, kd, tm, tn, tk)
    nk = kd // tk
    dims = ((0,) if ta else (1,), (1,) if tb else (0,))

    def body(a_ref, b_ref, o_ref, *scr):
        part = _dot(a_ref[0] if a_parts else a_ref[...], b_ref[0] if b_parts else b_ref[...], dims, hi)
        if nk == 1:
            o_ref[...] = part.astype(out_dtype)
        else:
            acc = scr[0]
            kk = pl.program_id(2)

            @pl.when(kk == 0)
            def _():
                acc[...] = part

            @pl.when(kk > 0)
            def _():
                acc[...] += part

            @pl.when(kk == nk - 1)
            def _():
                o_ref[...] = acc[...].astype(out_dtype)

    a_spec = pl.BlockSpec((tk, tm), lambda i, j, k: (k, i)) if ta else pl.BlockSpec((tm, tk), lambda i, j, k: (i, k))
    b_spec = pl.BlockSpec((tn, tk), lambda i, j, k: (j, k)) if tb else pl.BlockSpec((tk, tn), lambda i, j, k: (k, j))
    if a_parts:
        per = a.shape[2] // tk
        assert per * tk == a.shape[2]
        a_spec = pl.BlockSpec((1, tm, tk), lambda i, j, k: (k // per, i, k % per))
    if b_parts:
        per_n = b.shape[2] // tn
        assert per_n * tn == b.shape[2]
        b_spec = pl.BlockSpec((1, tk, tn), lambda i, j, k: (j // per_n, k, j % per_n))
    return pl.pallas_call(
        body, name=name, grid=(m // tm, n // tn, nk),
        in_specs=[a_spec, b_spec],
        out_specs=pl.BlockSpec((tm, tn), lambda i, j, k: (i, j)),
        out_shape=jax.ShapeDtypeStruct((m, n), out_dtype),
        scratch_shapes=[] if nk == 1 else [pltpu.VMEM((tm, tn), F32)],
        compiler_params=_cparams(("parallel", "parallel", "arbitrary")),
    )(a, b)


HALO = 8


class _In:
    def __init__(self, arr, w=None, cb=0, roff=0, halo=None, ridx=None):
        self.arr, self.w, self.cb, self.roff, self.halo = arr, w or arr.shape[1], cb, roff, halo
        self.ridx = ridx or (lambda i, roff=roff: i + roff)


class _Full:
    def __init__(self, arr, w=None, cb=0):
        self.arr, self.w, self.cb = arr, w, cb


class _Out:
    def __init__(self, cols, dtype=F32, w=None, cb=0, acc=False, rows=1, roff=0, nrows=None, stack=0, into=None):
        self.cols, self.dtype, self.w, self.cb, self.acc, self.rows, self.roff, self.nrows, self.stack = (
            cols, dtype, w or cols, cb, acc, rows, roff, nrows, stack)
        self.into = into


def _alias_outs(arrays, specs, outs):
    aliases = {}
    for k, o in enumerate(outs):
        if o.into is not None:
            aliases[len(arrays)] = k
            arrays.append(o.into)
            specs.append(pl.BlockSpec(memory_space=pl.ANY))
    return aliases


def _rowcall(name, fn, nrow_tiles, tile, ins, outs, ncol=1):
    arrays, specs, kinds = [], [], []
    for x in ins:
        if isinstance(x, _Full):
            arrays.append(x.arr)
            if x.w is None:
                specs.append(pl.BlockSpec(x.arr.shape, lambda j, i: (0, 0)))
            else:
                specs.append(pl.BlockSpec((x.arr.shape[0], x.w), lambda j, i, cb=x.cb: (0, cb + j)))
            kinds.append("full")
            continue
        w, cb, roff = x.w, x.cb, x.roff
        cur = pl.BlockSpec((tile, w), lambda j, i, cb=cb, ridx=x.ridx: (ridx(i), cb + j))
        if x.halo is None:
            arrays.append(x.arr)
            specs.append(cur)
            kinds.append("tile")
        else:
            r8 = tile // HALO
            last = x.arr.shape[0] // HALO - 1
            prev = pl.BlockSpec((HALO, w), lambda j, i, cb=cb, roff=roff, r8=r8: (jnp.maximum((i + roff) * r8 - 1, 0), cb + j))
            nxt = pl.BlockSpec((HALO, w), lambda j, i, cb=cb, roff=roff, r8=r8, last=last:
                               (jnp.minimum((i + roff + 1) * r8, last), cb + j))
            arrays += [x.arr, x.arr, x.arr]
            specs += [prev, cur, nxt]
            kinds.append(("halo", x.halo))
    out_specs, out_shapes = [], []
    for o in outs:
        if o.acc:
            out_specs.append(pl.BlockSpec((o.rows, o.w), lambda j, i, cb=o.cb: (0, cb + j)))
            out_shapes.append(jax.ShapeDtypeStruct((o.rows, o.cols), o.dtype))
        elif o.stack:
            out_specs.append(pl.BlockSpec((o.stack, tile, o.w), lambda j, i, cb=o.cb: (0, i, cb + j)))
            out_shapes.append(jax.ShapeDtypeStruct((o.stack, nrow_tiles * tile, o.cols), o.dtype))
        else:
            out_specs.append(pl.BlockSpec((tile, o.w), lambda j, i, cb=o.cb, roff=o.roff: (i + roff, cb + j)))
            out_shapes.append(jax.ShapeDtypeStruct(((o.nrows or nrow_tiles * tile), o.cols), o.dtype))
    aliases = _alias_outs(arrays, specs, outs)
    n_in = len(arrays)

    def body(*refs):
        j = pl.program_id(0)
        i = pl.program_id(1)
        vals, r = [], 0
        for kind in kinds:
            if kind in ("full", "tile"):
                vals.append(refs[r][...])
                r += 1
            else:
                pok, nok = kind[1]
                p, c, n = refs[r][...], refs[r + 1][...], refs[r + 2][...]
                p = jnp.where(pok(i), p, jnp.zeros_like(p))
                n = jnp.where(nok(i), n, jnp.zeros_like(n))
                vals.append(jnp.concatenate([p, c, n], axis=0))
                r += 3
        res = fn(i, j, *vals)
        for o, ref, val in zip(outs, refs[n_in:], res):
            if o.acc:
                @pl.when(i == 0)
                def _(ref=ref, val=val, o=o):
                    ref[...] = val.astype(o.dtype)

                @pl.when(i > 0)
                def _(ref=ref, val=val, o=o):
                    ref[...] += val.astype(o.dtype)
            elif o.stack:
                for s_ in range(o.stack):
                    ref[s_] = val[s_].astype(o.dtype)
            else:
                ref[...] = val.astype(o.dtype)

    return pl.pallas_call(
        body, name=name, grid=(ncol, nrow_tiles), in_specs=specs, out_specs=out_specs, out_shape=out_shapes,
        input_output_aliases=aliases, compiler_params=_cparams(("parallel", "arbitrary")),
    )(*arrays)


def _mm_ep(name, a, b, tb, tm, tk, fn, ins, outs, hi=False):
    a_parts = a.shape[0] if a.ndim == 3 else 0
    m, kd = (a.shape[1], a_parts * a.shape[2]) if a_parts else a.shape
    n = b.shape[0] if tb else b.shape[1]
    tk = min(tk, kd)
    assert m % tm == 0 and kd % tk == 0, (name, m, kd, tm, tk)
    nk = kd // tk
    dims = ((1,), (1,) if tb else (0,))
    if a_parts:
        per = a.shape[2] // tk
        arrays, specs = [a], [pl.BlockSpec((1, tm, tk), lambda i, k: (k // per, i, k % per))]
    else:
        arrays, specs = [a], [pl.BlockSpec((tm, tk), lambda i, k: (i, k))]
    arrays.append(b)
    specs.append(pl.BlockSpec((n, tk), lambda i, k: (0, k)) if tb else pl.BlockSpec((tk, n), lambda i, k: (k, 0)))
    for x in ins:
        arrays.append(x.arr)
        if isinstance(x, _Full):
            specs.append(pl.BlockSpec(x.arr.shape, lambda i, k: (0, 0)))
        else:
            specs.append(pl.BlockSpec((tm, x.w), lambda i, k, cb=x.cb, ridx=x.ridx: (ridx(i), cb)))
    out_specs, out_shapes = [], []
    for o in outs:
        if o.acc:
            out_specs.append(pl.BlockSpec((o.rows, o.w), lambda i, k, cb=o.cb: (0, cb)))
            out_shapes.append(jax.ShapeDtypeStruct((o.rows, o.cols), o.dtype))
        else:
            out_specs.append(pl.BlockSpec((tm, o.w), lambda i, k, cb=o.cb, roff=o.roff: (i + roff, cb)))
            out_shapes.append(jax.ShapeDtypeStruct((o.nrows or m, o.cols), o.dtype))
    n_vals = len(arrays)
    aliases = _alias_outs(arrays, specs, outs)
    n_in = len(arrays)

    def body(*refs):
        i, kk = pl.program_id(0), pl.program_id(1)
        a_ref, b_ref = refs[0], refs[1]
        acc_ref = refs[-1]
        part = _dot(a_ref[0] if a_parts else a_ref[...], b_ref[...], dims, hi)

        @pl.when(kk == 0)
        def _():
            acc_ref[...] = part

        @pl.when(kk > 0)
        def _():
            acc_ref[...] += part

        @pl.when(kk == nk - 1)
        def _():
            res = fn(i, acc_ref[...], *[r[...] for r in refs[2:n_vals]])
            for o, ref, val in zip(outs, refs[n_in:-1], res):
                if o.acc:
                    @pl.when(i == 0)
                    def _(ref=ref, val=val, o=o):
                        ref[...] = val.astype(o.dtype)

                    @pl.when(i > 0)
                    def _(ref=ref, val=val, o=o):
                        ref[...] += val.astype(o.dtype)
                else:
                    ref[...] = val.astype(o.dtype)

    return pl.pallas_call(
        body, name=name, grid=(m // tm, nk), in_specs=specs, out_specs=out_specs, out_shape=out_shapes,
        scratch_shapes=[pltpu.VMEM((tm, n), F32)], input_output_aliases=aliases,
        compiler_params=_cparams(("arbitrary", "arbitrary")),
    )(*arrays)


def _shift(xe, s, tile):
    if s == 0:
        return xe[HALO:HALO + tile]
    return pltpu.roll(xe, (-s) % xe.shape[0], 0)[HALO:HALO + tile]


def _silu(x):
    return x * jax.nn.sigmoid(x)


def _dsilu(x):
    s = jax.nn.sigmoid(x)
    return s * (1.0 + x * (1.0 - s))


def _heads(x, fn):
    return jnp.concatenate([fn(h, x[:, h * HD:(h + 1) * HD]) for h in range(x.shape[1] // HD)], axis=1)


def _colsum(x):
    return jnp.sum(x, axis=0, keepdims=True)


def _rowmean(x):
    return jnp.mean(x, axis=1, keepdims=True)


def _rowsum(x):
    return jnp.sum(x, axis=1, keepdims=True)


TILE = 256
CT = CTX // TILE


def _all_halo(n_tiles):
    return (lambda i: i >= CT + 1, lambda i: jnp.logical_and(i >= CT, i < n_tiles - 1))


def _lat_halo(n_tiles):
    return (lambda i: i >= 1, lambda i: i < n_tiles - 1)


def _rms_mod(x, nm, shift, scale):
    r = lax.rsqrt(_rowmean(x * x) + EPS)
    return (x * r * nm) * (1.0 + scale) + shift


def _rms_mod_bwd(dh, x, nm, scale):
    r = lax.rsqrt(_rowmean(x * x) + EPS)
    xn = x * r
    dz = dh * (1.0 + scale)
    dxn = dz * nm
    dx = r * (dxn - xn * _rowmean(dxn * xn))
    return dx, _colsum(dz * xn), _colsum(dh), _colsum(dh * (xn * nm))


def _norm_mod(x, ctx, nm, mod_c, mod_x):
    n = (x.shape[0] + ctx.shape[0]) // TILE

    def fn(i, j, c_, x_, nm_, mc, mx):
        m = jnp.where(i < CT, mc, mx)
        return (_rms_mod(jnp.where(i < CT, c_, x_), nm_, m[0:1], m[1:2]),)

    ins = [_In(ctx, ridx=lambda i: jnp.minimum(i, CT - 1)), _In(x, ridx=lambda i: jnp.maximum(i - CT, 0)),
           _Full(nm), _Full(mod_c), _Full(mod_x)]
    return _rowcall("norm_mod", fn, n, TILE, ins, [_Out(D, BF16)])[0]


def _norm_mod_bwd(dh, xs, dres, nm, mod, roff, n):
    ins = [_In(dh, roff=roff), _In(xs), _Full(nm), _Full(mod)] + ([] if dres is None else [_In(dres)])

    def fn(i, j, dh_, x, nm_, m, *rest):
        dx, dn, dsh, dsc = _rms_mod_bwd(dh_, x, nm_, m[1:2])
        if rest:
            return (dx + rest[0], dn, dsh, dsc)
        return (dn, dsh, dsc)

    accs = [_Out(D, acc=True), _Out(D, acc=True), _Out(D, acc=True)]
    return _rowcall("norm_mod_bwd", fn, n, TILE, ins, ([] if dres is None else [_Out(D)]) + accs)


DN_Q_SCALE = HD ** -0.5


def _conv_taps(xe, w, width, rows=None):
    r = width // 2
    acc = None
    for t in range(width):
        s = t - r
        if rows is None:
            sh = xe if s == 0 else pltpu.roll(xe, (-s) % xe.shape[0], 0)
        else:
            sh = _shift(xe, s, rows)
        term = sh * w[t:t + 1]
        acc = term if acc is None else acc + term
    return acc


def _rolled(xe, width):
    r = width // 2
    return [xe if t == r else pltpu.roll(xe, (r - t) % xe.shape[0], 0) for t in range(width)]


def _conv_bwd(rolled, w, c_grad, width):
    r = width // 2
    cc = c_grad[HALO:HALO + TILE]
    dx, dws = None, []
    for t in range(width):
        term = _shift(c_grad, r - t, TILE) * w[t:t + 1]
        dx = term if dx is None else dx + term
        dws.append(_colsum(cc * rolled[t][HALO:HALO + TILE]))
    return dx, jnp.concatenate(dws + [jnp.zeros((8 - width, cc.shape[1]), F32)], axis=0)


def _silu_both(x):
    s = jax.nn.sigmoid(x)
    return x * s, s * (1.0 + x * (1.0 - s))


def _l2n(x, scale):
    rn = lax.rsqrt(_rowsum(x * x) + EPS)
    return x * (rn * scale)


def _l2n_bwd(dy, x, scale):
    rn = lax.rsqrt(_rowsum(x * x) + EPS)
    xu = x * rn
    return (scale * rn) * (dy - xu * _rowsum(dy * xu))


def _softplus(x):
    return jnp.maximum(x, 0.0) + jnp.log(1.0 + jnp.exp(-jnp.abs(x)))


def _lane_mask(lo, hi_):
    lane = lax.broadcasted_iota(jnp.int32, (1, 128), 1)
    return jnp.logical_and(lane >= lo, lane < hi_).astype(F32)


def _dn_prep(p, conv_w, gprm):
    n = p.shape[0] // TILE
    halo = _all_halo(n)

    def fn(i, j, qe, ke, ve, ba, w, gp):
        cq = _conv_taps(qe, w[:, 0:D], 5, TILE)
        ck = _conv_taps(ke, w[:, D:2 * D], 5, TILE)
        cv = _conv_taps(ve, w[:, 2 * D:3 * D], 5, TILE)
        q = _heads(_silu(cq), lambda h, x: _l2n(x, DN_Q_SCALE))
        k = _heads(_silu(ck), lambda h, x: _l2n(x, 1.0))
        v = _silu(cv)
        beta = jax.nn.sigmoid(ba)
        g = -jnp.exp(gp[0:1]) * _softplus(ba + gp[1:2])
        m0, m1 = _lane_mask(0, 8), _lane_mask(8, 16)
        gb_f = beta * m0 + pltpu.roll(g, 128 - 8, 1) * m1
        gb_b = pltpu.roll(beta, 128 - 8, 1) * m0 + pltpu.roll(g, 128 - 16, 1) * m1
        return q, k, v, gb_f, gb_b

    ins = [_In(p, D, 0, halo=halo), _In(p, D, 1, halo=halo), _In(p, D, 2, halo=halo), _In(p, 128, C_BA // 128),
           _Full(conv_w), _Full(gprm)]
    return _rowcall("dn_prep", fn, n, TILE, ins, [_Out(D), _Out(D), _Out(D), _Out(128), _Out(128)])


def _dn_prep_bwd(p, conv_w, gprm, dq2, dk2, dv2, dgb2, dk_at, dv_at, dp):
    n = p.shape[0] // TILE
    halo = _all_halo(n)

    def branch(xe, w, dye, scale):
        rolled = _rolled(xe, 5)
        c = rolled[0] * w[0:1]
        for t in range(1, 5):
            c = c + rolled[t] * w[t:t + 1]
        sx, dsilu = _silu_both(c)
        if scale is None:
            dsx = dye
        else:
            dsx = jnp.concatenate([_l2n_bwd(dye[:, h * HD:(h + 1) * HD], sx[:, h * HD:(h + 1) * HD], scale)
                                   for h in range(NH)], axis=1)
        return _conv_bwd(rolled, w, dsx * dsilu, 5)

    def fn(i, j, qe, ke, ve, ba, w, gp, dq0, dq1, dk0, dk1, dv0, dv1, dg0, dg1, dka, dva):
        dxq, dwq = branch(qe, w[:, 0:D], dq0 + dq1, DN_Q_SCALE)
        dxk, dwk = branch(ke, w[:, D:2 * D], dk0 + dk1, 1.0)
        dxv, dwv = branch(ve, w[:, 2 * D:3 * D], dv0 + dv1, None)
        m0, m1 = _lane_mask(0, 8), _lane_mask(8, 16)
        dbeta = dg0 * m0 + pltpu.roll(dg1 * m0, 8, 1)
        dg = pltpu.roll(dg0 * m1, 8, 1) + pltpu.roll(dg1 * m1, 16, 1)
        beta = jax.nn.sigmoid(ba)
        ea = jnp.exp(gp[0:1])
        z = ba + gp[1:2]
        g = -ea * _softplus(z)
        mg = _lane_mask(16, 32)
        da = dg * (-ea) * jax.nn.sigmoid(z) * mg
        dba = dbeta * beta * (1.0 - beta) * _lane_mask(0, 16) + da
        dgp = jnp.concatenate([_colsum(dg * g * mg), _colsum(da)], axis=0)
        half = jnp.concatenate([dxq, dxk, dxv, dka.astype(F32), dva.astype(F32), dba, jnp.zeros((TILE, PH - C_PAD), F32)],
                               axis=1)
        return (half, jnp.concatenate([dwq, dwk, dwv], axis=1), dgp)

    ins = [_In(p, D, 0, halo=halo), _In(p, D, 1, halo=halo), _In(p, D, 2, halo=halo), _In(p, 128, C_BA // 128),
           _Full(conv_w), _Full(gprm),
           _In(dq2, halo=halo), _In(dq2, roff=n, halo=halo), _In(dk2, halo=halo), _In(dk2, roff=n, halo=halo),
           _In(dv2, halo=halo), _In(dv2, roff=n, halo=halo), _In(dgb2), _In(dgb2, roff=n), _In(dk_at), _In(dv_at)]
    return _rowcall("dn_prep_bwd", fn, n, TILE, ins,
                    [_Out(PW, BF16, w=PH, cb=0, into=dp), _Out(3 * D, acc=True, rows=8), _Out(128, acc=True, rows=2)])


def _hnorm(x, w):
    return x * lax.rsqrt(_rowmean(x * x) + EPS) * w


def _hnorm_bwd(dy, x, w):
    r = lax.rsqrt(_rowmean(x * x) + EPS)
    xh = x * r
    dxh = dy * w
    return r * (dxh - xh * _rowmean(dxh * xh)), _colsum(dy * xh)


def _dn_gate(o2, p, dn_norm, n_all):
    n = n_all - CT

    def fn(i, j, of, ob, gt, w):
        o = of + ob
        return (_heads(o, lambda h, x: _hnorm(x, w)) * _silu(gt),)

    ins = [_In(o2, roff=CT), _In(o2, roff=n_all + CT), _In(p, D, C_GT // D, roff=CT), _Full(dn_norm)]
    return _rowcall("dn_gate", fn, n, TILE, ins, [_Out(D, BF16)])[0]


def _mm_bdn_dx_gate(dz_dn, w_bdn, o2, p, dn_norm, n_all, dp, hi):
    def fn(i, dy_, of, ob, gt, w):
        o = of + ob
        sg, dsg = _silu_both(gt)
        dos, dw = [], jnp.zeros((1, HD), F32)
        yn = []
        for h in range(NH):
            sl = slice(h * HD, (h + 1) * HD)
            dx, dwh = _hnorm_bwd(dy_[:, sl] * sg[:, sl], o[:, sl], w)
            dos.append(dx)
            dw = dw + dwh
            yn.append(_hnorm(o[:, sl], w))
        dgt = dy_ * jnp.concatenate(yn, axis=1) * dsg
        return jnp.concatenate(dos, axis=1), dgt, dw

    ins = [_In(o2, roff=CT), _In(o2, roff=n_all + CT), _In(p, D, C_GT // D, roff=CT), _Full(dn_norm)]
    outs = [_Out(D), _Out(PW, BF16, w=D, cb=C_GT // D, roff=CT, nrows=p.shape[0], into=dp), _Out(HD, acc=True)]
    return _mm_ep("mm_bdn_dx_gate", dz_dn, w_bdn, True, TILE, D, fn, ins, outs, hi)


def _rope_shuffle(x):
    lane = lax.broadcasted_iota(jnp.int32, (1, HD), 1)
    return jnp.where((lane % 64) < 32, pltpu.roll(x, HD - 32, 1), pltpu.roll(x, 32, 1))


def _rope(x, cos, sin):
    return x * cos + _rope_shuffle(x) * sin


def _rope_bwd(dy, cos, sin):
    return dy * cos + _rope_shuffle(dy * sin)


def _attn_prep(p, w, cos, sin, width, cb, roff, n, name):
    def fn(i, j, x, w_, c, s):
        return (_heads(x, lambda h, xh: _rope(_hnorm(xh, w_), c, s)),)

    ins = [_In(p, width, cb, roff=roff), _Full(w), _In(cos), _In(sin)]
    return _rowcall(name, fn, n, TILE, ins, [_Out(width)])[0]


def _attn_prep_bwd(dy, p, w, cos, sin, width, cb, roff, n, name, dx_out):
    def fn(i, j, dy_, x, w_, c, s):
        dxs, dw = [], jnp.zeros((1, HD), F32)
        for h in range(width // HD):
            sl = slice(h * HD, (h + 1) * HD)
            dx, dwh = _hnorm_bwd(_rope_bwd(dy_[:, sl], c, s), x[:, sl], w_)
            dxs.append(dx)
            dw = dw + dwh
        return jnp.concatenate(dxs, axis=1), dw

    ins = [_In(dy), _In(p, width, cb, roff=roff), _Full(w), _In(cos), _In(sin)]
    return _rowcall(name, fn, n, TILE, ins, [dx_out, _Out(HD, acc=True)])


def _mm_bat_merge(o_at, w_bat, z_dn, p, hi):
    def fn(i, za, zd, gd, ga):
        return za, jax.nn.sigmoid(gd) * zd + jax.nn.sigmoid(ga) * za

    ins = [_In(z_dn), _In(p, D, C_MG // D, roff=CT), _In(p, D, C_MG // D + 1, roff=CT)]
    return _mm_ep("mm_bat_merge", o_at, w_bat, False, TILE, D, fn, ins, [_Out(D), _Out(D, BF16)], hi)


def _mm_out_dx_merge(dmo, w_out, z_dn, z_at, p, hi):
    def fn(i, dm_, zd, za, gd, ga):
        sd, sa = jax.nn.sigmoid(gd), jax.nn.sigmoid(ga)
        dg = jnp.concatenate([dm_ * zd * sd * (1.0 - sd), dm_ * za * sa * (1.0 - sa)], axis=1)
        return dm_ * sd, dm_ * sa, dg

    ins = [_In(z_dn), _In(z_at), _In(p, D, C_MG // D, roff=CT), _In(p, D, C_MG // D + 1, roff=CT)]
    outs = [_Out(D, BF16), _Out(D, BF16), _Out(PW, BF16, w=2 * D, cb=C_MG // (2 * D), roff=CT, nrows=p.shape[0])]
    return _mm_ep("mm_out_dx_merge", dmo, w_out, True, TILE, D, fn, ins, outs, hi)


def _mm_out_resid(merged, w_out, x, g_a, nf, mod_f, hi):
    def fn(i, mo_, x_, ga, nf_, m):
        x1 = x_ + ga * mo_
        return mo_, x1, _rms_mod(x1, nf_, m[0:1], m[1:2])

    ins = [_In(x), _Full(g_a), _Full(nf), _Full(mod_f)]
    return _mm_ep("mm_out_resid", merged, w_out, False, min(512, x.shape[0]), D, fn, ins, [_Out(D), _Out(D), _Out(D, BF16)], hi)


def _mm_up_dx_norm(du, ffn_up, dy, x1, mo, g_a, nf, mod_f, hi):
    def fn(i, dh_, dy_, x1_, mo_, ga, nf_, m):
        dx, dn, dsh, dsc = _rms_mod_bwd(dh_, x1_, nf_, m[1:2])
        dx1 = dy_ + dx
        return dx1, ga * dx1, dn, dsh, dsc, _colsum(dx1 * mo_)

    ins = [_In(dy), _In(x1), _In(mo), _Full(g_a), _Full(nf), _Full(mod_f)]
    accs = [_Out(D, acc=True) for _ in range(4)]
    return _mm_ep("mm_up_dx_norm", du, ffn_up, True, min(512, x1.shape[0]), 1408, fn, ins, [_Out(D), _Out(D, BF16)] + accs, hi)


def _mm_down_loss(a, ffn_down, x1, tgt, g_f, hi):
    def fn(i, f_, x1_, t, gf):
        e = x1_ + gf * f_ - t
        dy = e * (1.0 / D)
        loss = _colsum(_rowsum(e * e)) * (0.5 / D)
        return dy, gf * dy, _colsum(dy * f_), jnp.broadcast_to(loss, (1, 128))

    ins = [_In(x1), _In(tgt), _Full(g_f)]
    outs = [_Out(D), _Out(D, BF16), _Out(D, acc=True), _Out(128, acc=True)]
    return _mm_ep("mm_down_loss", a, ffn_down, False, min(512, x1.shape[0]), DFF, fn, ins, outs, hi)


FW = DFF // 2


def _ffn_act(u, conv_w, conv_b, n):
    halo = _lat_halo(n)

    def fn(i, j, ge, ve, wg, wv, bg, bv):
        cg = _conv_taps(ge, wg, 3, TILE) + bg
        cv = _conv_taps(ve, wv, 3, TILE) + bv
        return (_silu(cg) * cv,)

    ins = [_In(u, FW, 0, halo=halo), _In(u, FW, 2, halo=halo), _Full(conv_w, FW, 0), _Full(conv_w, FW, 2),
           _Full(conv_b, FW, 0), _Full(conv_b, FW, 2)]
    return _rowcall("ffn_act", fn, n, TILE, ins, [_Out(DFF, BF16, FW)], ncol=2)[0]


def _ffn_act_bwd(u, da, conv_w, conv_b, n):
    halo = _lat_halo(n)

    def fn(i, j, ge, ve, dae, wg, wv, bg, bv):
        rg, rv = _rolled(ge, 3), _rolled(ve, 3)
        cg = rg[0] * wg[0:1] + rg[1] * wg[1:2] + rg[2] * wg[2:3] + bg
        cv = rv[0] * wv[0:1] + rv[1] * wv[1:2] + rv[2] * wv[2:3] + bv
        sg, dsg = _silu_both(cg)
        dcg = dae * cv * dsg
        dcv = dae * sg
        dxg, dwg = _conv_bwd(rg, wg, dcg, 3)
        dxv, dwv = _conv_bwd(rv, wv, dcv, 3)
        return (dxg, dxv), dwg, dwv, _colsum(dcg[HALO:HALO + TILE]), _colsum(dcv[HALO:HALO + TILE])

    ins = [_In(u, FW, 0, halo=halo), _In(u, FW, 2, halo=halo), _In(da, FW, 0, halo=halo),
           _Full(conv_w, FW, 0), _Full(conv_w, FW, 2), _Full(conv_b, FW, 0), _Full(conv_b, FW, 2)]
    outs = [_Out(DFF, BF16, FW, stack=2), _Out(DFF, w=FW, acc=True, rows=8), _Out(DFF, w=FW, acc=True, rows=8),
            _Out(DFF, w=FW, acc=True), _Out(DFF, w=FW, acc=True)]
    return _rowcall("ffn_act_bwd", fn, n, TILE, ins, outs, ncol=2)


def _rope_tables(tl):
    rows = tl // GRID_W
    inv = np.float32(ROPE_BASE) ** (-np.arange(32, dtype=np.float32) / np.float32(32))
    ar = np.arange(rows, dtype=np.float32)[:, None] * inv
    ac = np.arange(GRID_W, dtype=np.float32)[:, None] * inv

    def table(r, c):
        full = (rows, GRID_W, HD // 2)
        return jnp.concatenate([jnp.broadcast_to(jnp.asarray(r)[:, None, :], full),
                                jnp.broadcast_to(jnp.asarray(c)[None, :, :], full)], axis=2).reshape(tl, HD)

    two = lambda a, b: np.concatenate([a, b], axis=1).astype(np.float32)
    cos = table(two(np.cos(ar), np.cos(ar)), two(np.cos(ac), np.cos(ac)))
    sin = table(two(-np.sin(ar), np.sin(ar)), two(-np.sin(ac), np.sin(ac)))
    return cos, sin


def _pad_w_in(w_in):
    return jnp.concatenate([w_in[:, 0:3072], w_in[:, 5152:5664], w_in[:, 4096:4128], jnp.zeros((D, 96 + C_GT - C_PAD), w_in.dtype),
                            w_in[:, 3072:4096], w_in[:, 4128:5152], w_in[:, 5664:7712]], axis=1)


def _unpad_w_in(g, axis=1):
    cut = lambda a, b: lax.slice_in_dim(g, a, b, axis=axis)
    return jnp.concatenate([cut(0, 3072), cut(C_GT, C_GT + D), cut(C_BA, C_BA + 32), cut(C_QAT, C_QAT + D),
                            cut(C_KAT, C_KAT + 512), cut(C_MG, C_MG + 2 * D)], axis=axis)


def _local_step(x, ctx, tgt, mod_x, mod_c, w, hi=False):
    tl = x.shape[0]
    t_all = tl + CTX
    n_all, n = t_all // TILE, tl // TILE
    tm_all = 1280 if t_all % 1280 == 0 else TILE
    tm_lat = 1024
    mm = functools.partial(_mm, hi=hi)
    sp = lambda m: [m[:, k * D:(k + 1) * D] for k in range(6)]
    sh_a, sc_a, g_a, sh_f, sc_f, g_f = sp(mod_x)
    sh_ac, sc_ac = sp(mod_c)[:2]
    mod_ax = jnp.concatenate([sh_a, sc_a], axis=0)
    mod_ac = jnp.concatenate([sh_ac, sc_ac], axis=0)
    mod_f = jnp.concatenate([sh_f, sc_f], axis=0)
    nm, nf = w["norm_mix"], w["norm_ffn"]
    cos, sin = _rope_tables(tl)
    cos_all = jnp.concatenate([jnp.ones((CTX, HD), F32), cos], axis=0)
    sin_all = jnp.concatenate([jnp.zeros((CTX, HD), F32), sin], axis=0)
    conv_dn = jnp.concatenate([w["dn_conv"], jnp.zeros((3, 3 * D), F32)], axis=0)
    gprm = jnp.concatenate([jnp.zeros((2, 16), F32),
                            jnp.concatenate([w["dn_a_log"].reshape(1, 16), w["dn_dt_bias"].reshape(1, 16)], axis=0),
                            jnp.zeros((2, 96), F32)], axis=1)
    conv_ff = jnp.concatenate([w["ffn_conv"], jnp.zeros((5, 2 * DFF), F32)], axis=0)
    sink = jnp.concatenate([w["attn_sink"].reshape(1, NH), jnp.zeros((1, 128 - NH), F32)], axis=1)
    nct = CTX // CH

    h = _norm_mod(x, ctx, nm, mod_ac, mod_ax)
    p = mm(h, w["w_in_p"], tm=tm_all, tn=1024, name="mm_in")
    q, k, v, gb_f, gb_b = _dn_prep(p, conv_dn, gprm)
    gb = jnp.stack([gb_f, gb_b])
    dn_u, dn_w, dn_qg, dn_kd, dn_pm, dn_t = _dn_intra_fwd(q, k, v, gb, nct, hi)
    o2, s_hist, dn_vn = _dn_seq_fwd(dn_u, dn_w, dn_qg, dn_kd, dn_pm, gb, nct, hi)
    o2 = o2.reshape(2 * t_all, D)
    y_dn = _dn_gate(o2, p, w["dn_norm"], n_all)
    qr = _attn_prep(p, w["q_norm"], cos, sin, D, C_QAT // D, CT, n, "attn_prep_q")
    kr = _attn_prep(p, w["k_norm"], cos_all, sin_all, KVH * HD, C_KAT // (KVH * HD), 0, n_all, "attn_prep_k")
    vv = p[:, C_VAT:C_VAT + KVH * HD]
    o_at, lse = _attn_fwd(qr, kr, vv, sink, hi)
    z_dn = mm(y_dn, w["w_branch_dn"], tm=tm_lat, name="mm_bdn")
    z_at, merged = _mm_bat_merge(o_at, w["w_branch_attn"], z_dn, p, hi)
    mo, x1, h2 = _mm_out_resid(merged, w["w_out"], x, g_a, nf, mod_f, hi)
    u = mm(h2, w["ffn_up"], tm=2 * tm_lat, tn=1408, name="mm_up")
    a = _ffn_act(u, conv_ff, w["ffn_conv_b"], n)
    dy, df, dg_f, loss = _mm_down_loss(a, w["ffn_down"], x1, tgt, g_f, hi)

    g = {}
    da = mm(df, w["ffn_down"], tb=True, tm=tm_lat, tn=1408, name="mm_down_dx")
    g["ffn_down"] = mm(a, df, ta=True, tm=1408, tn=1024, tk=tm_lat, name="mm_down_dw")
    du, dcw_g, dcw_v, dcb_g, dcb_v = _ffn_act_bwd(u, da, conv_ff, w["ffn_conv_b"], n)
    g["ffn_conv"] = jnp.concatenate([dcw_g, dcw_v], axis=1)[0:3]
    g["ffn_conv_b"] = jnp.concatenate([dcb_g, dcb_v], axis=1)
    g["ffn_up"] = mm(h2, du, ta=True, tm=1024, tn=1408, tk=tm_lat, name="mm_up_dw")
    dx1, dmo, g["norm_ffn"], dsh_f, dsc_f, dg_a = _mm_up_dx_norm(du, w["ffn_up"], dy, x1, mo, g_a, nf, mod_f, hi)
    g["w_out"] = mm(merged, dmo, ta=True, tm=1024, tk=tm_lat, name="mm_out_dw")
    dz_dn, dz_at, dmg = _mm_out_dx_merge(dmo, w["w_out"], z_dn, z_at, p, hi)
    g["w_branch_dn"] = mm(y_dn, dz_dn, ta=True, tm=1024, tk=tm_lat, name="mm_bdn_dw")
    do_at, delta = _mm_bat_dx_delta(dz_at, w["w_branch_attn"], o_at, hi)
    g["w_branch_attn"] = mm(o_at, dz_at, ta=True, tm=1024, tk=tm_lat, name="mm_bat_dw")

    do_dn, dp, g["dn_norm"] = _mm_bdn_dx_gate(dz_dn, w["w_branch_dn"], o2, p, w["dn_norm"], n_all, dmg, hi)
    do_all = do_dn
    dn_dvn, dn_dw, dn_dqg, dn_dkd, dn_del = _dn_seq_bwd(dn_w, dn_qg, dn_kd, dn_pm, dn_vn, s_hist, gb, do_all, nct, hi)
    dq2, dk2, dv2, dgb2 = _dn_intra_bwd(q, k, v, gb, dn_u, dn_w, dn_t, dn_vn, dn_dvn, dn_dw, dn_dqg, dn_dkd, dn_del,
                                        do_all, nct, hi)

    dqr, dk_lat, dv_lat, dkx, dvx, dsink = _attn_bwd(qr, kr, vv, sink, do_at, lse, delta, hi)
    g["attn_sink"] = dsink[:, 0:NH]
    q_out = _Out(PW, BF16, w=D, cb=C_QAT // D, roff=CT, nrows=t_all, into=dp)
    dp, g["q_norm"] = _attn_prep_bwd(dqr, p, w["q_norm"], cos, sin, D, C_QAT // D, CT, n, "attn_prep_q_bwd", q_out)
    dkr = jnp.concatenate([dkx, dk_lat], axis=0)
    dk_at, g["k_norm"] = _attn_prep_bwd(dkr, p, w["k_norm"], cos_all, sin_all, KVH * HD, C_KAT // (KVH * HD), 0, n_all,
                                        "attn_prep_k_bwd", _Out(KVH * HD, BF16))
    dv_at = jnp.concatenate([dvx, dv_lat], axis=0).astype(BF16)

    dp, dconv, dgprm = _dn_prep_bwd(p, conv_dn, gprm, dq2.reshape(2 * t_all, D), dk2.reshape(2 * t_all, D),
                                    dv2.reshape(2 * t_all, D), dgb2.reshape(2 * t_all, 128), dk_at, dv_at, dp)
    g["dn_conv"] = dconv[0:5]
    g["dn_a_log"] = dgprm[0, 16:32].reshape(2, NH)
    g["dn_dt_bias"] = dgprm[1, 16:32].reshape(2, NH)
    dp = lax.dynamic_update_slice(dp, jnp.zeros((CTX, PH), BF16), (0, PH))
    dh = mm(dp, w["w_in_p"], tb=True, tm=tm_all, tn=1024, tk=2048, name="mm_in_dx")
    g["w_in_p"] = mm(h, dp, ta=True, tm=1024, tn=2048, tk=tm_all, name="mm_in_dw")
    dnm_c, dsh_ac, dsc_ac = _norm_mod_bwd(dh, ctx, None, nm, mod_ac, 0, CT)
    grad_x, dnm_x, dsh_a, dsc_a = _norm_mod_bwd(dh, x, dx1, nm, mod_ax, CT, n)
    g["norm_mix"] = dnm_c + dnm_x
    dmod_x = jnp.concatenate([dsh_a, dsc_a, dg_a, dsh_f, dsc_f, dg_f], axis=1)
    dmod_c = jnp.concatenate([dsh_ac, dsc_ac, jnp.zeros((1, 4 * D), F32)], axis=1)
    return loss, grad_x, g, dmod_x, dmod_c


def _sum_slots(buf, n_slots, rows, tile, name, stride=1):
    nt = rows // tile

    def fn(i, j, *vals):
        acc = vals[0]
        for v in vals[1:]:
            acc = acc + v
        return (acc,)

    ins = [_In(buf, roff=k * stride * nt) for k in range(n_slots)]
    return _rowcall(name, fn, nt, tile, ins, [_Out(buf.shape[1])])[0]


ADAM_LR, ADAM_B1, ADAM_B2, ADAM_EPS, ADAM_WD, ADAM_STEP = 0.001, 0.9, 0.999, 1e-08, 0.01, 10


def _row_tile(rows, cols):
    for t in (512, 256, 128, 64, 32, 16, 8):
        if rows % t == 0 and t * cols * 4 * 14 <= 40 * 1024 * 1024:
            return t
    return rows


def _adamw(w, g, m, v, name):
    shape = w.shape
    cols = shape[-1]
    rows = max(1, math.prod(shape[:-1]))
    tile = _row_tile(rows, cols)
    c1 = 1.0 / (1.0 - ADAM_B1 ** ADAM_STEP)
    c2 = 1.0 / (1.0 - ADAM_B2 ** ADAM_STEP)

    def fn(i, j, w_, g_, m_, v_):
        mn = ADAM_B1 * m_ + (1.0 - ADAM_B1) * g_
        vn = ADAM_B2 * v_ + (1.0 - ADAM_B2) * (g_ * g_)
        delta = -ADAM_LR * ((mn * c1) / (jnp.sqrt(vn * c2) + ADAM_EPS) + ADAM_WD * w_)
        return delta, mn, vn

    r2 = lambda a: a.reshape(rows, cols)
    outs = _rowcall(name, fn, rows // tile, tile, [_In(r2(w)), _In(r2(g)), _In(r2(m)), _In(r2(v))],
                    [_Out(cols), _Out(cols), _Out(cols)])
    return [o.reshape(shape) for o in outs]


MESH = pl.DeviceIdType.MESH
ANY = pl.BlockSpec(memory_space=pl.ANY)


def _pos():
    return lax.axis_index("x"), lax.axis_index("y"), lax.axis_index("c")


def _all_gather_many(blks, name):
    na = len(blks)

    def body(*refs):
        x_refs, out_refs = refs[:na], refs[na:2 * na]
        send_sems, recv_sems, local_sems = refs[2 * na:]
        x, y, c = _pos()
        me, sibling = (x, y, c), (x, y, 1 - c)
        chips = [(1 - x, y), (x, 1 - y), (1 - x, 1 - y)]

        def rows(a, px, py, pc):
            m_per = blks[a].shape[0]
            return out_refs[a].at[pl.ds(pl.multiple_of((4 * px + 2 * py + pc) * m_per, 8), m_per), :]

        def copy(a, k, block, to, src=None):
            return pltpu.make_async_remote_copy(
                src_ref=rows(a, *block) if src is None else src, dst_ref=rows(a, *block),
                send_sem=send_sems.at[7 * a + k], recv_sem=recv_sems.at[7 * a + k], device_id=to, device_id_type=MESH)

        every = range(na)
        mine = [pltpu.make_async_copy(x_refs[a], rows(a, *me), local_sems.at[a]) for a in every]
        for cp in mine:
            cp.start()
        first = [copy(a, 0, me, sibling, src=x_refs[a]) for a in every]
        first += [copy(a, 1 + j, me, (*chip, c), src=x_refs[a]) for j, chip in enumerate(chips) for a in every]
        for cp in first:
            cp.start()
        passed = []
        for j, chip in enumerate(chips):
            for a in every:
                copy(a, 1 + j, (*chip, c), me).wait_recv()
                passed.append(copy(a, 4 + j, (*chip, c), sibling))
                passed[-1].start()
        for a in every:
            copy(a, 0, sibling, me).wait_recv()
        for j, chip in enumerate(chips):
            for a in every:
                copy(a, 4 + j, (*chip, 1 - c), me).wait_recv()
        for cp in first + passed:
            cp.wait_send()
        for cp in mine:
            cp.wait()

    return pl.pallas_call(
        body, name=name,
        out_shape=[jax.ShapeDtypeStruct((N_DEV * b.shape[0], b.shape[1]), b.dtype) for b in blks],
        in_specs=[ANY] * na, out_specs=[ANY] * na,
        scratch_shapes=[pltpu.SemaphoreType.DMA((7 * na,)), pltpu.SemaphoreType.DMA((7 * na,)), pltpu.SemaphoreType.DMA((na,))],
        compiler_params=pltpu.CompilerParams(has_side_effects=True),
    )(*blks)


def _all_gather(blk, name):
    return _all_gather_many([blk], name)[0]


def _flip(v, bit):
    return 1 - v if bit else v


D2D_STREAMS = 8
ICI_STREAMS = 2


def _sibling_exchange(src, seg_rows, n_seg, paired, name):
    n = src.shape[1]
    per_seg = D2D_STREAMS // n_seg
    per = seg_rows // per_seg
    assert per_seg * n_seg == D2D_STREAMS and per * per_seg == seg_rows and per % 16 == 0

    def body(x_ref, out_ref, send_sems, recv_sems):
        x, y, c = _pos()
        copies = []
        for s in range(n_seg):
            base = (2 * s + (1 - c)) * seg_rows if paired else s * seg_rows
            for j in range(per_seg):
                i = s * per_seg + j
                cp = pltpu.make_async_remote_copy(
                    src_ref=x_ref.at[pl.ds(pl.multiple_of(base + j * per, 16), per), :],
                    dst_ref=out_ref.at[pl.ds(s * seg_rows + j * per, per), :],
                    send_sem=send_sems.at[i], recv_sem=recv_sems.at[i], device_id=(x, y, 1 - c), device_id_type=MESH)
                cp.start()
                copies.append(cp)
        for cp in copies:
            cp.wait_recv()
        for cp in copies:
            cp.wait_send()

    return pl.pallas_call(
        body, name=name, out_shape=jax.ShapeDtypeStruct((n_seg * seg_rows, n), src.dtype),
        in_specs=[ANY], out_specs=ANY,
        scratch_shapes=[pltpu.SemaphoreType.DMA((D2D_STREAMS,)), pltpu.SemaphoreType.DMA((D2D_STREAMS,))],
        compiler_params=pltpu.CompilerParams(has_side_effects=True),
    )(src)


def _transpose_cast(x, dtype, name):
    r, c = x.shape
    tc = 512

    def body(x_ref, o_ref):
        o_ref[...] = x_ref[...].T.astype(o_ref.dtype)

    return pl.pallas_call(
        body, name=name, grid=(c // tc,),
        in_specs=[pl.BlockSpec((r, tc), lambda j: (0, j))], out_specs=pl.BlockSpec((tc, r), lambda j: (j, 0)),
        out_shape=jax.ShapeDtypeStruct((c, r), dtype), compiler_params=_cparams(("parallel",)),
    )(x)


def _chip_exchange(buf, rows, name):
    n = buf.shape[1]
    per = rows // ICI_STREAMS
    assert per * ICI_STREAMS == rows and per % 16 == 0

    def body(x_ref, out_ref, send_sems, recv_sems):
        x, y, c = _pos()
        copies = []
        for k in range(1, 4):
            px, py = _flip(x, k & 2), _flip(y, k & 1)
            for j in range(ICI_STREAMS):
                i = (k - 1) * ICI_STREAMS + j
                cp = pltpu.make_async_remote_copy(
                    src_ref=x_ref.at[pl.ds(pl.multiple_of((2 * px + py) * rows + j * per, 16), per), :],
                    dst_ref=out_ref.at[pl.ds((k - 1) * rows + j * per, per), :],
                    send_sem=send_sems.at[i], recv_sem=recv_sems.at[i], device_id=(px, py, c), device_id_type=MESH)
                cp.start()
                copies.append(cp)
        for cp in copies:
            cp.wait_recv()
        for cp in copies:
            cp.wait_send()

    return pl.pallas_call(
        body, name=name, out_shape=jax.ShapeDtypeStruct((3 * rows, n), buf.dtype),
        in_specs=[ANY], out_specs=ANY,
        scratch_shapes=[pltpu.SemaphoreType.DMA((3 * ICI_STREAMS,)), pltpu.SemaphoreType.DMA((3 * ICI_STREAMS,))],
        compiler_params=pltpu.CompilerParams(has_side_effects=True),
    )(buf)


def _add_rows(parts, rows, dtype, name):
    tile = 1024
    ins = [_In(a, roff=r0 // tile) for a, r0 in parts]

    def fn(i, j, *vals):
        acc = vals[0].astype(F32)
        for v_ in vals[1:]:
            acc = acc + v_.astype(F32)
        return (acc,)

    return _rowcall(name, fn, rows // tile, tile, ins, [_Out(parts[0][0].shape[1], dtype)])[0]


BIG = ("w_in", "w_branch_dn", "w_branch_attn", "w_out", "ffn_up", "ffn_down")
BIG_SHARD = {"w_in": (1024, 1928, True), "w_branch_dn": (256, 1024, False), "w_branch_attn": (256, 1024, False),
             "w_out": (256, 1024, False), "ffn_up": (1024, 1408, True), "ffn_down": (704, 1024, False)}
BIG_ROWS = {k: r * c // 2 // 128 for k, (r, c, _) in BIG_SHARD.items()}
PIECE = 19456
assert sum(BIG_ROWS.values()) <= PIECE


def _gather_weights(shards, ci):
    halves = []
    for k in BIG:
        r, c, _ = BIG_SHARD[k]
        halves.append(lax.dynamic_slice_in_dim(shards[k], ci * (r // 2), r // 2, axis=0).astype(BF16))
    out = {}
    for k, ag in zip(BIG, _all_gather_many(halves, "ag_weights")):
        r, c, by_col = BIG_SHARD[k]
        blk = ag.reshape(4, r, c)
        out[k] = jnp.transpose(blk, (1, 0, 2)).reshape(r, 4 * c) if by_col else blk.reshape(4 * r, c)
    return out


def _pack_pieces(full):
    parts = [full["w_in_t"].reshape(N_DEV, BIG_ROWS["w_in"], 128).astype(BF16)]
    for k in BIG[1:]:
        r, c, by_col = BIG_SHARD[k]
        a = full[k]
        if by_col:
            a = jnp.transpose(a.reshape(r, 4, c), (1, 0, 2))
        parts.append(a.reshape(N_DEV, BIG_ROWS[k], 128).astype(BF16))
    parts.append(jnp.zeros((N_DEV, PIECE - sum(BIG_ROWS.values()), 128), BF16))
    return jnp.concatenate(parts, axis=1).reshape(N_DEV * PIECE, 128)


def _reduce_scatter(pieces, ci, shard):
    half = N_DEV // 2 * PIECE
    theirs = _sibling_exchange(pieces, PIECE, N_DEV // 2, True, "rs_d2d")
    own = lax.dynamic_index_in_dim(pieces.reshape(N_DEV // 2, 2, PIECE, 128), ci, axis=1, keepdims=False).reshape(half, 128)
    part = _add_rows([(own, 0), (theirs, 0)], half, BF16, "rs_sum_chip")
    recv = _chip_exchange(part, PIECE, "rs_ici")
    own2 = lax.dynamic_slice_in_dim(part, shard * PIECE, PIECE, axis=0)
    mine = _add_rows([(own2, 0), (recv, 0), (recv, PIECE), (recv, 2 * PIECE)], PIECE, F32, "rs_sum_all")
    other = _sibling_exchange(mine, PIECE, 1, False, "rs_pair")
    return jnp.where(ci == 0, jnp.stack([mine, other]), jnp.stack([other, mine]))


def _unpack_shard(two):
    out, off = {}, 0
    for k in BIG:
        r, c, _ = BIG_SHARD[k]
        blk = two[:, off:off + BIG_ROWS[k]]
        out[k] = blk.reshape(c, r).T if k == "w_in" else blk.reshape(r, c)
        off += BIG_ROWS[k]
    return out


SMALL = (("dn_conv", 120), ("ffn_conv", 132), ("ffn_conv_b", 44), ("norm_mix", 8), ("norm_ffn", 8), ("dn_a_log", 1),
         ("dn_dt_bias", 1), ("dn_norm", 1), ("q_norm", 1), ("k_norm", 1), ("attn_sink", 1), ("dmod_c", 48), ("dmod_x", 48))
SMALL_ROWS = 416


def _rows128(a, rows):
    flat = a.reshape(-1)
    return jnp.concatenate([flat, jnp.zeros((rows * 128 - flat.shape[0],), F32)]).reshape(rows, 128)


def _pack_small(g):
    parts = [_rows128(g[k], r) for k, r in SMALL]
    parts.append(jnp.zeros((SMALL_ROWS - sum(r for _, r in SMALL), 128), F32))
    return jnp.concatenate(parts, axis=0)


def _unpack_small(buf, shapes):
    out, off = {}, 0
    for k, r in SMALL:
        n = math.prod(shapes[k])
        out[k] = buf[off:off + r].reshape(-1)[:n].reshape(shapes[k])
        off += r
    return out


WEIGHTS = ("c_ctx", "w_ada", "b_ada", "norm_mix", "norm_ffn", "w_in", "dn_conv", "dn_a_log", "dn_dt_bias", "dn_norm",
           "q_norm", "k_norm", "attn_sink", "w_branch_dn", "w_branch_attn", "w_out", "ffn_up", "ffn_conv", "ffn_conv_b",
           "ffn_down")


def kernel(x, c, ctx, c_ctx, w_ada, b_ada, norm_mix, norm_ffn, w_in, dn_conv, dn_a_log, dn_dt_bias, dn_norm, q_norm, k_norm, attn_sink, w_branch_dn, w_branch_attn, w_out, ffn_up, ffn_conv, ffn_conv_b, ffn_down, loss_target, m_c_ctx, m_w_ada, m_b_ada, m_norm_mix, m_norm_ffn, m_w_in, m_dn_conv, m_dn_a_log, m_dn_dt_bias, m_dn_norm, m_q_norm, m_k_norm, m_attn_sink, m_w_branch_dn, m_w_branch_attn, m_w_out, m_ffn_up, m_ffn_conv, m_ffn_conv_b, m_ffn_down, v_c_ctx, v_w_ada, v_b_ada, v_norm_mix, v_norm_ffn, v_w_in, v_dn_conv, v_dn_a_log, v_dn_dt_bias, v_dn_norm, v_q_norm, v_k_norm, v_attn_sink, v_w_branch_dn, v_w_branch_attn, v_w_out, v_ffn_up, v_ffn_conv, v_ffn_conv_b, v_ffn_down):
    args = dict(locals())
    xi, yi, ci = _pos()
    dev = 4 * xi + 2 * yi + ci
    shard = 2 * xi + yi
    chips = lambda a: a[0::2]

    blk = jnp.concatenate([_rows128(c, 8), _rows128(dn_conv, 30), _rows128(ffn_conv, 33), jnp.zeros((1, 128), F32)], axis=0)
    ag = _all_gather(blk, "ag_small_in").reshape(N_DEV, 72, 128)
    c_all = ag[:, 0:8].reshape(N_DEV, D)
    dn_conv_full = jnp.transpose(chips(ag)[:, 8:38].reshape(4, 5, 768), (1, 0, 2)).reshape(5, 3 * D)
    ffn_conv_full = jnp.transpose(chips(ag)[:, 38:71].reshape(4, 3, 1408), (1, 0, 2)).reshape(3, 2 * DFF)

    c16 = jnp.concatenate([c_all, c_ctx[None], jnp.zeros((7, D), F32)], axis=0)
    a16 = _rowcall("ada_silu", lambda i, j, v: (_silu(v),), 1, 16, [_In(c16)], [_Out(D)])[0]
    m_sh = _mm(a16, w_ada[0], tm=16, tn=512, tk=D, name="ada_fwd", hi=True)
    mod16 = chips(_all_gather(m_sh, "ag_mod").reshape(N_DEV, 16, 1536))
    mod16 = jnp.transpose(mod16, (1, 0, 2)).reshape(16, 6 * D) + b_ada
    mod_x = lax.dynamic_slice_in_dim(mod16, dev, 1, axis=0)
    mod_c = mod16[8:9]

    shards = {k: args[k][0] for k in BIG}
    wfull = _gather_weights(shards, ci)
    w = dict(wfull)
    w["w_in_p"] = _pad_w_in(wfull["w_in"])
    w.update(norm_mix=norm_mix, norm_ffn=norm_ffn, dn_conv=dn_conv_full, dn_a_log=dn_a_log[0], dn_dt_bias=dn_dt_bias[0],
             dn_norm=dn_norm, q_norm=q_norm, k_norm=k_norm, attn_sink=attn_sink, ffn_conv=ffn_conv_full, ffn_conv_b=ffn_conv_b)

    loss_part, grad_x, g, dmod_x, dmod_c = _local_step(x[0], ctx[0], loss_target[0], mod_x, mod_c, w)
    loss = lax.psum(loss_part[0, 0], ("x", "y", "c"))

    g["w_in_t"] = _unpad_w_in(_transpose_cast(g["w_in_p"], BF16, "w_in_grad_t"), axis=0)
    gshard = _unpack_shard(_reduce_scatter(_pack_pieces(g), ci, shard))

    g["dmod_c"], g["dmod_x"] = dmod_c, dmod_x
    ag_s = _all_gather(_pack_small(g), "ag_small_grads")
    shapes = {k: g[k].shape for k, _ in SMALL}
    gs = _unpack_small(_sum_slots(ag_s, N_DEV, SMALL_ROWS, SMALL_ROWS, "small_sum"), shapes)
    dx_all = ag_s.reshape(N_DEV, SMALL_ROWS, 128)[:, SMALL_ROWS - 50:SMALL_ROWS - 2].reshape(N_DEV, 6 * D)

    d16 = jnp.concatenate([dx_all, gs["dmod_c"], jnp.zeros((7, 6 * D), F32)], axis=0)
    d16_sh = lax.dynamic_slice_in_dim(d16, shard * 1536, 1536, axis=1)
    g_w_ada = _mm(a16, d16_sh, ta=True, tm=D, tn=512, tk=16, name="ada_dw", hi=True)
    g_b_ada = _rowcall("ada_db", lambda i, j, v: (_colsum(v),), 1, 16, [_In(d16)], [_Out(6 * D, acc=True)])[0]
    da_part = _mm(d16_sh, w_ada[0], tb=True, tm=16, tn=D, tk=512, name="ada_dx", hi=True)
    da_all = _all_gather(da_part, "ag_ada_dx")
    da16 = _sum_slots(da_all, 4, 16, 16, "ada_dx_sum", stride=2)
    dc16 = _rowcall("ada_dsilu", lambda i, j, d_, v: (d_ * _dsilu(v),), 1, 16, [_In(da16), _In(c16)], [_Out(D)])[0]

    grads = {
        "c_ctx": dc16[8], "w_ada": g_w_ada[None], "b_ada": g_b_ada, "norm_mix": gs["norm_mix"], "norm_ffn": gs["norm_ffn"],
        "w_in": gshard["w_in"][None],
        "dn_conv": lax.dynamic_slice_in_dim(gs["dn_conv"], shard * 768, 768, axis=1)[None],
        "dn_a_log": gs["dn_a_log"][None], "dn_dt_bias": gs["dn_dt_bias"][None], "dn_norm": gs["dn_norm"],
        "q_norm": gs["q_norm"], "k_norm": gs["k_norm"], "attn_sink": gs["attn_sink"],
        "w_branch_dn": gshard["w_branch_dn"][None], "w_branch_attn": gshard["w_branch_attn"][None],
        "w_out": gshard["w_out"][None], "ffn_up": gshard["ffn_up"][None],
        "ffn_conv": lax.dynamic_slice_in_dim(gs["ffn_conv"], shard * 1408, 1408, axis=1)[None],
        "ffn_conv_b": gs["ffn_conv_b"], "ffn_down": gshard["ffn_down"][None],
    }
    deltas, new_m, new_v = [], [], []
    for k in WEIGHTS:
        d_, m_, v_ = _adamw(args[k], grads[k], args["m_" + k], args["v_" + k], "adamw_" + k)
        deltas.append(d_)
        new_m.append(m_)
        new_v.append(v_)
    return (loss, grad_x[None], *[grads[k] for k in WEIGHTS], *deltas, *new_m, *new_v)
```

```python
import functools
import math

import numpy as np
import jax
import jax.numpy as jnp
from jax import lax
from jax.experimental import pallas as pl
from jax.experimental.pallas import tpu as pltpu

F32 = jnp.float32
BF16 = jnp.bfloat16
HI = lax.Precision.HIGHEST

D = 1024
NH = 8
HD = 128
CH = 64
CTX = 256
AB = 128
KVH = 2
GRP = 4
DFF = 2816
EPS = 1e-6
GRID_W = 64
ROPE_BASE = 10000.0
N_DEV = 8
VMEM_LIMIT = 56 * 1024 * 1024

C_QKV, C_KAT, C_VAT, C_BA, C_PAD, C_GT, C_QAT, C_MG = 0, 3072, 3328, 3584, 3712, 4096, 5120, 6144
PW = 8192
PH = PW // 2


def _cparams(sem=None, **kw):
    return pltpu.CompilerParams(dimension_semantics=sem, vmem_limit_bytes=VMEM_LIMIT, **kw)


def _dot(a, b, dims, hi):
    if hi:
        return lax.dot_general(a.astype(F32), b.astype(F32), (dims, ((), ())), precision=HI, preferred_element_type=F32)
    return lax.dot_general(a.astype(BF16), b.astype(BF16), (dims, ((), ())), preferred_element_type=F32)


NN = ((1,), (0,))
NT = ((1,), (1,))
TN = ((0,), (0,))


def _dn_masks():
    i = np.arange(CH)
    lo_incl = (i[:, None] >= i[None, :]).astype(np.float32)
    lo_strict = (i[:, None] > i[None, :]).astype(np.float32)
    return jnp.asarray(np.stack([np.stack([lo_incl, lo_strict]), np.stack([lo_incl.T, lo_strict.T])]))


def _dn_chunk_index(d, i, n_ctx_chunks, n_chunks):
    fwd = i
    bwd = jnp.where(i < n_ctx_chunks, n_ctx_chunks - 1 - i, n_chunks - 1 + n_ctx_chunks - i)
    return jnp.where(d == 0, fwd, bwd)


BNN = ((2,), (1,))
BNT = ((2,), (2,))
BTN = ((1,), (1,))


def _bdot(a, b, dims, hi):
    dn = (dims, ((0,), (0,)))
    if hi:
        return lax.dot_general(a.astype(F32), b.astype(F32), dn, precision=HI, preferred_element_type=F32)
    return lax.dot_general(a.astype(BF16), b.astype(BF16), dn, preferred_element_type=F32)


def _bdot3(a, b, dims, hi):
    if hi:
        return _bdot(a, b, dims, True)
    ah, bh = a.astype(BF16), b.astype(BF16)
    al, bl = (a - ah.astype(F32)).astype(BF16), (b - bh.astype(F32)).astype(BF16)
    dn = (dims, ((0,), (0,)))
    d = lambda x_, y_: lax.dot_general(x_, y_, dn, preferred_element_type=F32)
    return d(ah, bh) + d(ah, bl) + d(al, bh)


DN_CB = 4
DN_SEQ_CB = 4


def _dn_heads(ref, cb=1):
    return jnp.stack([ref[t * CH:(t + 1) * CH, h * HD:(h + 1) * HD] for t in range(cb) for h in range(NH)])


def _dn_scalars(gb, mi, cb=1):
    beta, gc, gcr, gt = [], [], [], []
    for t in range(cb):
        g1 = gb[t * CH:(t + 1) * CH]
        gcum, gcum_t, gtot = _dn_gcum(g1, mi)
        beta += [g1[:, h:h + 1] for h in range(NH)]
        gc += [gcum[:, NH + h:NH + h + 1] for h in range(NH)]
        gcr += [gcum_t[NH + h:NH + h + 1, :] for h in range(NH)]
        gt += [gtot[:, NH + h:NH + h + 1] for h in range(NH)]
    return jnp.stack(beta), jnp.stack(gc), jnp.stack(gcr), jnp.stack(gt)


DN_NEWTON = 1


def _dn_inverse(a, hi):
    eye = (lax.broadcasted_iota(jnp.int32, (CH, CH), 0) == lax.broadcasted_iota(jnp.int32, (CH, CH), 1)).astype(F32)
    x = -a
    t = eye + x
    p = x
    if hi:
        for _ in range(5):
            p = _bdot(p, p, BNN, True)
            t = t + _bdot(t, p, BNN, True)
        return t
    for _ in range(5):
        p = _bdot(p, p, BNN, False)
        t = t + _bdot(t, p, BNN, False)
    for _ in range(DN_NEWTON):
        r = eye - t - _bdot3(a, t, BNN, False)
        t = t + _bdot(t, r, BNN, False)
    return t


def _dn_total(gb):
    gtot = jnp.sum(gb, axis=0, keepdims=True)
    return jnp.stack([gtot[:, NH + h:NH + h + 1] for h in range(NH)])


def _dn_gcum(gb, mi):
    gcum = _dot(mi, gb, NN, True)
    gtot = jnp.sum(gb, axis=0, keepdims=True)
    return gcum, gcum.T, gtot


def _dn_specs(n_ctx_chunks, n_chunks, reverse, cb):
    assert n_ctx_chunks % cb == 0 and n_chunks % cb == 0

    def grp(d, i):
        first = n_chunks - 1 - cb * i if reverse else cb * i
        return _dn_chunk_index(d, first, n_ctx_chunks, n_chunks) // cb

    def slot(d, t):
        ascending = (d == 1) if reverse else (d == 0)
        return jnp.where(ascending, t, cb - 1 - t)

    ctx_groups = n_ctx_chunks // cb
    tok_lat = pl.BlockSpec((cb * CH, D), lambda d, i: (jnp.maximum(grp(d, i) - ctx_groups, 0), 0))
    is_ctx = lambda d, i: grp(d, i) < ctx_groups
    tok_d = pl.BlockSpec((1, cb * CH, D), lambda d, i: (d, grp(d, i), 0))
    gbs = pl.BlockSpec((1, cb * CH, 128), lambda d, i: (d, grp(d, i), 0))

    def per_chunk(*tail):
        return pl.BlockSpec((1, cb) + tail, lambda d, i: (d, grp(d, i)) + (0,) * len(tail))

    return tok_lat, is_ctx, tok_d, gbs, per_chunk, slot


def _dn_group_specs(cb):
    tok = pl.BlockSpec((cb * CH, D), lambda d, i: (i, 0))
    tok_d = pl.BlockSpec((1, cb * CH, D), lambda d, i: (d, i, 0))
    gbs = pl.BlockSpec((1, cb * CH, 128), lambda d, i: (d, i, 0))
    msk = pl.BlockSpec((1, 2, CH, CH), lambda d, i: (d, 0, 0, 0))

    def per_chunk(*tail):
        return pl.BlockSpec((1, cb) + tail, lambda d, i: (d, i) + (0,) * len(tail))

    return tok, tok_d, gbs, msk, per_chunk


def _dn_intra_fwd(q, k, v, gb, n_ctx_chunks, hi):
    t_all = q.shape[0]
    n_chunks = t_all // CH
    masks = _dn_masks()

    cb = DN_CB

    def put(ref, val):
        for t_ in range(cb):
            ref[0, t_] = val[t_ * NH:(t_ + 1) * NH].astype(ref.dtype)

    def body(q_ref, k_ref, v_ref, gb_ref, m_ref, u_ref, w_ref, qg_ref, kd_ref, pm_ref, t_ref):
        mi, ms = m_ref[0, 0], m_ref[0, 1]
        beta, gc, gcr, gt = _dn_scalars(gb_ref[0], mi, cb)
        q_, k_, v_ = _dn_heads(q_ref, cb), _dn_heads(k_ref, cb), _dn_heads(v_ref, cb)
        decay = jnp.exp(jnp.where(mi > 0, gc - gcr, 0.0)) * mi
        e = jnp.exp(gc)
        a = ms * (beta * _bdot(k_, k_, BNT, hi) * decay)
        t = _dn_inverse(a, hi)
        uw =_bdot(t, jnp.concatenate([beta * v_, (beta * e) * k_], axis=2), BNN, hi)
        put(u_ref, uw[:, :, :HD])
        put(w_ref, uw[:, :, HD:])
        put(qg_ref, e * q_)
        put(kd_ref, jnp.exp(gt - gc) * k_)
        put(pm_ref, _bdot(q_, k_, BNT, hi) * decay)
        put(t_ref, t)

    tok, _, gbs, msk, per_chunk = _dn_group_specs(cb)
    big = lambda dt: jax.ShapeDtypeStruct((2, n_chunks, NH, CH, HD), dt)
    sq = jax.ShapeDtypeStruct((2, n_chunks, NH, CH, CH), BF16)
    return pl.pallas_call(
        body, name="dn_intra_fwd", grid=(2, n_chunks // cb),
        in_specs=[tok, tok, tok, gbs, msk],
        out_specs=[per_chunk(NH, CH, HD)] * 4 + [per_chunk(NH, CH, CH)] * 2,
        out_shape=[big(BF16), big(BF16), big(BF16), big(BF16), sq, sq],
        compiler_params=_cparams(("parallel", "parallel")),
    )(q, k, v, gb, masks)


def _dn_seq_fwd(u, w, qg, kd, pm, gb, n_ctx_chunks, hi):
    n_chunks = u.shape[1]
    t_all = n_chunks * CH

    cb = DN_SEQ_CB
    _, _, tok_d, gbs, per_chunk, slot = _dn_specs(n_ctx_chunks, n_chunks, False, cb)

    def body(u_ref, w_ref, qg_ref, kd_ref, pm_ref, gb_ref, o_ref, sh_ref, vn_ref, s_scr):
        @pl.when(pl.program_id(1) == 0)
        def _():
            s_scr[...] = jnp.zeros_like(s_scr)

        for t in range(cb):
            j = slot(pl.program_id(0), t)
            rows = pl.ds(pl.multiple_of(j * CH, CH), CH)
            s = s_scr[...]
            sh_ref[0, j] = s.astype(sh_ref.dtype)
            vn = u_ref[0, j] - _bdot(w_ref[0, j], s, BNN, hi)
            o = _bdot(qg_ref[0, j], s, BNN, hi) + _bdot(pm_ref[0, j], vn, BNN, hi)
            s_scr[...] = jnp.exp(_dn_total(gb_ref[0, rows, :])) * s + _bdot(kd_ref[0, j], vn, BTN, hi)
            vn_ref[0, j] = vn.astype(vn_ref.dtype)
            for h in range(NH):
                o_ref[0, rows, h * HD:(h + 1) * HD] = o[h]

    big = per_chunk(NH, CH, HD)
    return pl.pallas_call(
        body, name="dn_seq_fwd", grid=(2, n_chunks // cb),
        in_specs=[big, big, big, big, per_chunk(NH, CH, CH), gbs],
        out_specs=[tok_d, per_chunk(NH, HD, HD), big],
        out_shape=[jax.ShapeDtypeStruct((2, t_all, D), F32), jax.ShapeDtypeStruct((2, n_chunks, NH, HD, HD), BF16),
                   jax.ShapeDtypeStruct((2, n_chunks, NH, CH, HD), BF16)],
        scratch_shapes=[pltpu.VMEM((NH, HD, HD), F32)],
        compiler_params=_cparams(("parallel", "arbitrary")),
    )(u, w, qg, kd, pm, gb)


def _dn_seq_bwd(w, qg, kd, pm, vn, s_hist, gb, do, n_ctx_chunks, hi):
    n_chunks = w.shape[1]

    cb = DN_SEQ_CB
    tok_lat, is_ctx, _, gbs, per_chunk, slot = _dn_specs(n_ctx_chunks, n_chunks, True, cb)

    def body(w_ref, qg_ref, kd_ref, pm_ref, vn_ref, sh_ref, gb_ref, do_ref, dvn_ref, dw_ref, dqg_ref, dkd_ref, del_ref, ds_scr):
        @pl.when(pl.program_id(1) == 0)
        def _():
            ds_scr[...] = jnp.zeros_like(ds_scr)

        for t in range(cb):
            j = slot(pl.program_id(0), t)
            rows = pl.ds(pl.multiple_of(j * CH, CH), CH)
            dsn = ds_scr[...]
            s = sh_ref[0, j]
            do_ = jnp.stack([do_ref[rows, h * HD:(h + 1) * HD] for h in range(NH)])
            do_ = jnp.where(is_ctx(pl.program_id(0), pl.program_id(1)), 0.0, do_)
            dvn =_bdot(pm_ref[0, j], do_, BTN, hi) + _bdot(kd_ref[0, j], dsn, BNN, hi)
            ds_scr[...] = (_bdot(qg_ref[0, j], do_, BTN, hi) + jnp.exp(_dn_total(gb_ref[0, rows, :])) * dsn
                           - _bdot(w_ref[0, j], dvn, BTN, hi))
            dvn_ref[0, j] = dvn.astype(dvn_ref.dtype)
            dw_ref[0, j] = (-_bdot(dvn, s, BNT, hi)).astype(dw_ref.dtype)
            dqg_ref[0, j] = _bdot(do_, s, BNT, hi).astype(dqg_ref.dtype)
            dkd_ref[0, j] = _bdot(vn_ref[0, j], dsn, BNT, hi).astype(dkd_ref.dtype)
            del_ref[0, j] = jnp.broadcast_to(jnp.sum(jnp.sum(s * dsn, axis=2, keepdims=True), axis=1, keepdims=True),
                                             (NH, 1, 128))

    big = per_chunk(NH, CH, HD)
    shp = lambda dt: jax.ShapeDtypeStruct((2, n_chunks, NH, CH, HD), dt)
    return pl.pallas_call(
        body, name="dn_seq_bwd", grid=(2, n_chunks // cb),
        in_specs=[big, big, big, per_chunk(NH, CH, CH), big, per_chunk(NH, HD, HD), gbs, tok_lat],
        out_specs=[big, big, big, big, per_chunk(NH, 1, 128)],
        out_shape=[shp(BF16), shp(BF16), shp(BF16), shp(BF16), jax.ShapeDtypeStruct((2, n_chunks, NH, 1, 128), F32)],
        scratch_shapes=[pltpu.VMEM((NH, HD, HD), F32)],
        compiler_params=_cparams(("parallel", "arbitrary")),
    )(w, qg, kd, pm, vn, s_hist, gb, do)


def _dn_intra_bwd(q, k, v, gb, u, w, t, vn, dvn, dw, dqg, dkd, de_last, do, n_ctx_chunks, hi):
    t_all = q.shape[0]
    n_chunks = t_all // CH
    masks = _dn_masks()

    cb = DN_CB
    assert n_ctx_chunks % cb == 0
    ctx_groups = n_ctx_chunks // cb

    def body(q_ref, k_ref, v_ref, gb_ref, m_ref, u_ref, w_ref, t_ref, vn_ref, dvn_ref, dw_ref, dqg_ref, dkd_ref, del_ref,
             do_ref, dq_ref, dk_ref, dv_ref, dgb_ref):
        mi, ms = m_ref[0, 0], m_ref[0, 1]
        beta, gc, gcr, gt = _dn_scalars(gb_ref[0], mi, cb)
        q_, k_, v_ = _dn_heads(q_ref, cb), _dn_heads(k_ref, cb), _dn_heads(v_ref, cb)
        do_ = jnp.where(pl.program_id(1) < ctx_groups, 0.0, _dn_heads(do_ref, cb))
        get = lambda ref: jnp.concatenate([ref[0, t_] for t_ in range(cb)], axis=0)
        decay = jnp.exp(jnp.where(mi > 0, gc - gcr, 0.0)) * mi
        e = jnp.exp(gc)
        e_last = jnp.exp(gt)
        kdfac = jnp.exp(gt - gc)
        kk = _bdot(k_, k_, BNT, hi)
        a = ms * (beta * kk * decay)
        pm = _bdot(q_, k_, BNT, hi) * decay
        kd = kdfac * k_
        dqg, dkd = get(dqg_ref), get(dkd_ref)
        dpm = _bdot(do_, get(vn_ref), BNT, hi)
        dvbkb = _bdot(get(t_ref), jnp.concatenate([get(dvn_ref), get(dw_ref)], axis=2), BTN, hi)
        dvb, dkb = dvbkb[:, :, :HD], dvbkb[:, :, HD:]
        da = -ms * _bdot(dvbkb, jnp.concatenate([get(u_ref), get(w_ref).astype(F32)], axis=2), BNT, hi)
        dqk = dpm * decay
        gm = dpm * pm + da * a
        dgc = (jnp.sum(gm, axis=2, keepdims=True)
               - _bdot3(gm, jnp.ones((cb * NH, CH, 128), F32), BTN, hi)[:, :, 0:1])
        dkk = da * (beta * decay)
        dbeta = jnp.sum(da * kk * decay, axis=2, keepdims=True)
        dk = _bdot(dkk, k_, BNN, hi) + _bdot(dkk, k_, BTN, hi) + _bdot(dqk, q_, BTN, hi)
        dq = _bdot(dqk, k_, BNN, hi) + e * dqg
        de = jnp.sum(dqg * q_, axis=2, keepdims=True)
        dv = beta * dvb
        dbeta = dbeta + jnp.sum(dvb * v_, axis=2, keepdims=True)
        skb = jnp.sum(dkb * k_, axis=2, keepdims=True)
        dk = dk + (beta * e) * dkb + kdfac * dkd
        dbeta = dbeta + e * skb
        de = de + beta * skb
        skd = jnp.sum(dkd * kd, axis=2, keepdims=True)
        dgc = dgc - skd + de * e
        dgtot = jnp.sum(skd, axis=1, keepdims=True) + get(del_ref)[:, :, 0:1] * e_last
        lane = lax.broadcasted_iota(jnp.int32, (1, 128), 1)
        for t_ in range(cb):
            rows = slice(t_ * CH, (t_ + 1) * CH)
            dbeta_all = jnp.zeros((CH, 128), F32)
            dgc_all = jnp.zeros((CH, 128), F32)
            dgtot_all = jnp.zeros((1, 128), F32)
            for h in range(NH):
                sl = slice(h * HD, (h + 1) * HD)
                b = t_ * NH + h
                dq_ref[0, rows, sl] = dq[b]
                dk_ref[0, rows, sl] = dk[b]
                dv_ref[0, rows, sl] = dv[b]
                hot_b = (lane == h).astype(F32)
                hot_g = (lane == NH + h).astype(F32)
                dbeta_all = dbeta_all + dbeta[b] * hot_b
                dgc_all = dgc_all + dgc[b] * hot_g
                dgtot_all = dgtot_all + dgtot[b] * hot_g
            dgb_ref[0, rows, :] = dbeta_all + _dot(mi, dgc_all, TN, True) + dgtot_all

    tok, tok_d, gbs, msk, per_chunk = _dn_group_specs(cb)
    tok_lat = pl.BlockSpec((cb * CH, D), lambda d, i: (jnp.maximum(i - ctx_groups, 0), 0))
    big = per_chunk(NH, CH, HD)
    return pl.pallas_call(
        body, name="dn_intra_bwd", grid=(2, n_chunks // cb),
        in_specs=[tok, tok, tok, gbs, msk, big, big, per_chunk(NH, CH, CH), big, big, big, big, big,
                  per_chunk(NH, 1, 128), tok_lat],
        out_specs=[tok_d, tok_d, tok_d, gbs],
        out_shape=[jax.ShapeDtypeStruct((2, t_all, D), F32)] * 3 + [jax.ShapeDtypeStruct((2, t_all, 128), F32)],
        compiler_params=_cparams(("parallel", "parallel")),
    )(q, k, v, gb, masks, u, w, t, vn, dvn, dw, dqg, dkd, de_last, do)


ATT_SCALE = HD ** -0.5
NEG = -1e30


def _att_stack(ref, kvh):
    return jnp.concatenate([ref[:, (kvh * GRP + g) * HD:(kvh * GRP + g + 1) * HD] for g in range(GRP)], axis=0)


def _att_col(ref, kvh):
    return jnp.concatenate([ref[:, kvh * GRP + g:kvh * GRP + g + 1] for g in range(GRP)], axis=0)


def _att_sink(sink_ref, kvh):
    return jnp.concatenate([jnp.broadcast_to(sink_ref[:, kvh * GRP + g:kvh * GRP + g + 1], (AB, 1)) for g in range(GRP)],
                           axis=0)


def _att_mask(i, nb):
    r = lax.broadcasted_iota(jnp.int32, (AB, AB), 0)
    c = lax.broadcasted_iota(jnp.int32, (AB, AB), 1)
    okp = jnp.logical_and(c >= r, i > 0)
    okn = jnp.logical_and(c <= r, i < nb - 1)
    return jnp.concatenate([okp] * GRP, axis=0), jnp.concatenate([okn] * GRP, axis=0)


def _att_masked(s, mask):
    mp, mn = mask
    return jnp.concatenate([jnp.where(mp, s[:, 0:AB], NEG), s[:, AB:2 * AB], jnp.where(mn, s[:, 2 * AB:3 * AB], NEG),
                            s[:, 3 * AB:]], axis=1)


def _att_kspecs(nb):
    nc = CTX // AB
    return [pl.BlockSpec((AB, KVH * HD), lambda i: (jnp.maximum(i - 1, 0) + nc, 0)),
            pl.BlockSpec((AB, KVH * HD), lambda i: (i + nc, 0)),
            pl.BlockSpec((AB, KVH * HD), lambda i: (jnp.minimum(i + 1, nb - 1) + nc, 0)),
            pl.BlockSpec((CTX, KVH * HD), lambda i: (0, 0))]


def _attn_fwd(qr, kr, vv, sink, hi):
    tl = qr.shape[0]
    nb = tl // AB

    def body(q_ref, kp_ref, kc_ref, kn_ref, kx_ref, vp_ref, vc_ref, vn_ref, vx_ref, sink_ref, o_ref, lse_ref):
        i = pl.program_id(0)
        mask = _att_mask(i, nb)
        lane = lax.broadcasted_iota(jnp.int32, (1, 128), 1)
        lse_all = jnp.zeros((AB, 128), F32)
        for kvh in range(KVH):
            ksl = slice(kvh * HD, (kvh + 1) * HD)
            kall = jnp.concatenate([kp_ref[:, ksl], kc_ref[:, ksl], kn_ref[:, ksl], kx_ref[:, ksl]], axis=0)
            vall = jnp.concatenate([vp_ref[:, ksl], vc_ref[:, ksl], vn_ref[:, ksl], vx_ref[:, ksl]], axis=0)
            s = _dot(_att_stack(q_ref, kvh), kall, NT, hi) * ATT_SCALE
            s = _att_masked(s, mask)
            sk = _att_sink(sink_ref, kvh)
            m = jnp.maximum(jnp.max(s, axis=1, keepdims=True), sk)
            p = jnp.exp(s - m)
            l = jnp.sum(p, axis=1, keepdims=True) + jnp.exp(sk - m)
            o = _dot(p, vall, NN, hi) / l
            lse = m + jnp.log(l)
            for g in range(GRP):
                h = kvh * GRP + g
                o_ref[:, h * HD:(h + 1) * HD] = o[g * AB:(g + 1) * AB]
                lse_all = lse_all + lse[g * AB:(g + 1) * AB] * (lane == h).astype(F32)
        lse_ref[...] = lse_all

    ks = _att_kspecs(nb)
    return pl.pallas_call(
        body, name="attn_fwd", grid=(nb,),
        in_specs=[pl.BlockSpec((AB, D), lambda i: (i, 0))] + ks + ks + [pl.BlockSpec((1, 128), lambda i: (0, 0))],
        out_specs=[pl.BlockSpec((AB, D), lambda i: (i, 0)), pl.BlockSpec((AB, 128), lambda i: (i, 0))],
        out_shape=[jax.ShapeDtypeStruct((tl, D), F32), jax.ShapeDtypeStruct((tl, 128), F32)],
        compiler_params=_cparams(("parallel",)),
    )(qr, kr, kr, kr, kr, vv, vv, vv, vv, sink)


def _mm_bat_dx_delta(dz_at, w_bat, o, hi):
    def fn(i, do_, o_):
        lane = lax.broadcasted_iota(jnp.int32, (1, 128), 1)
        acc = jnp.zeros((do_.shape[0], 128), F32)
        for h in range(NH):
            sl = slice(h * HD, (h + 1) * HD)
            acc = acc + jnp.sum(o_[:, sl] * do_[:, sl], axis=1, keepdims=True) * (lane == h).astype(F32)
        return do_, acc

    return _mm_ep("mm_bat_dx_delta", dz_at, w_bat, True, min(512, o.shape[0]), D, fn, [_In(o)], [_Out(D), _Out(128)], hi)


def _attn_bwd(qr, kr, vv, sink, do, lse, delta, hi):
    tl = qr.shape[0]
    nb = tl // AB
    nc = CTX // AB

    def body(q_ref, kp_ref, kc_ref, kn_ref, kx_ref, vp_ref, vc_ref, vn_ref, vx_ref, sink_ref, do_ref, lse_ref, dl_ref,
             dq_ref, dk_ref, dv_ref, dkx_ref, dvx_ref, dsink_ref, dk_acc, dv_acc):
        i = pl.program_id(0)

        @pl.when(i == 0)
        def _():
            dkx_ref[...] = jnp.zeros_like(dkx_ref)
            dvx_ref[...] = jnp.zeros_like(dvx_ref)
            dsink_ref[...] = jnp.zeros_like(dsink_ref)
            dk_acc[...] = jnp.zeros_like(dk_acc)
            dv_acc[...] = jnp.zeros_like(dv_acc)

        @pl.when(i < nb)
        def _():
            mask = _att_mask(i, nb)
            lane = lax.broadcasted_iota(jnp.int32, (1, 128), 1)
            s_prev, s_cur, s_next = (i + 2) % 3, i % 3, (i + 1) % 3
            dsink = jnp.zeros((1, 128), F32)
            for kvh in range(KVH):
                ksl = slice(kvh * HD, (kvh + 1) * HD)
                kall = jnp.concatenate([kp_ref[:, ksl], kc_ref[:, ksl], kn_ref[:, ksl], kx_ref[:, ksl]], axis=0)
                vall = jnp.concatenate([vp_ref[:, ksl], vc_ref[:, ksl], vn_ref[:, ksl], vx_ref[:, ksl]], axis=0)
                qs = _att_stack(q_ref, kvh)
                dos = _att_stack(do_ref, kvh)
                lse_s = _att_col(lse_ref, kvh)
                dl_s = _att_col(dl_ref, kvh)
                s = _dot(qs, kall, NT, hi) * ATT_SCALE
                p = jnp.exp(_att_masked(s, mask) - lse_s)
                dp = _dot(dos, vall, NT, hi)
                ds = p * (dp - dl_s)
                dq = _dot(ds, kall, NN, hi) * ATT_SCALE
                dk_all = _dot(ds, qs, TN, hi) * ATT_SCALE
                dv_all = _dot(p, dos, TN, hi)
                dkx_ref[:, ksl] += dk_all[3 * AB:]
                dvx_ref[:, ksl] += dv_all[3 * AB:]
                dk_acc[s_prev, :, ksl] += dk_all[0:AB]
                dv_acc[s_prev, :, ksl] += dv_all[0:AB]
                dk_acc[s_cur, :, ksl] += dk_all[AB:2 * AB]
                dv_acc[s_cur, :, ksl] += dv_all[AB:2 * AB]
                dk_acc[s_next, :, ksl] = dk_all[2 * AB:3 * AB]
                dv_acc[s_next, :, ksl] = dv_all[2 * AB:3 * AB]
                psink = jnp.exp(_att_sink(sink_ref, kvh) - lse_s) * dl_s
                for g in range(GRP):
                    h = kvh * GRP + g
                    dq_ref[:, h * HD:(h + 1) * HD] = dq[g * AB:(g + 1) * AB]
                    dsink = dsink - jnp.sum(psink[g * AB:(g + 1) * AB], axis=0, keepdims=True) * (lane == h).astype(F32)
            dsink_ref[...] += dsink

        @pl.when(i >= 1)
        def _():
            dk_ref[...] = dk_acc[(i + 2) % 3]
            dv_ref[...] = dv_acc[(i + 2) % 3]

    blk = lambda i: jnp.minimum(i, nb - 1)
    row = pl.BlockSpec((AB, D), lambda i: (blk(i), 0))
    col = pl.BlockSpec((AB, 128), lambda i: (blk(i), 0))
    ks = [pl.BlockSpec((AB, KVH * HD), lambda i: (jnp.maximum(blk(i) - 1, 0) + nc, 0)),
          pl.BlockSpec((AB, KVH * HD), lambda i: (blk(i) + nc, 0)),
          pl.BlockSpec((AB, KVH * HD), lambda i: (jnp.minimum(i + 1, nb - 1) + nc, 0)),
          pl.BlockSpec((CTX, KVH * HD), lambda i: (0, 0))]
    kv_out = pl.BlockSpec((AB, KVH * HD), lambda i: (jnp.maximum(i - 1, 0), 0))
    ctx_out = pl.BlockSpec((CTX, KVH * HD), lambda i: (0, 0))
    return pl.pallas_call(
        body, name="attn_bwd", grid=(nb + 1,),
        in_specs=[row] + ks + ks + [pl.BlockSpec((1, 128), lambda i: (0, 0)), row, col, col],
        out_specs=[row, kv_out, kv_out, ctx_out, ctx_out, pl.BlockSpec((1, 128), lambda i: (0, 0))],
        out_shape=[jax.ShapeDtypeStruct((tl, D), F32), jax.ShapeDtypeStruct((tl, KVH * HD), F32),
                   jax.ShapeDtypeStruct((tl, KVH * HD), F32), jax.ShapeDtypeStruct((CTX, KVH * HD), F32),
                   jax.ShapeDtypeStruct((CTX, KVH * HD), F32), jax.ShapeDtypeStruct((1, 128), F32)],
        scratch_shapes=[pltpu.VMEM((3, AB, KVH * HD), F32), pltpu.VMEM((3, AB, KVH * HD), F32)],
        compiler_params=_cparams(("arbitrary",)),
    )(qr, kr, kr, kr, kr, vv, vv, vv, vv, sink, do, lse, delta)


def _mm(a, b, ta=False, tb=False, out_dtype=F32, tm=512, tn=1024, tk=1024, name="mm", hi=False):
    a_parts = a.shape[0] if a.ndim == 3 else 0
    b_parts = b.shape[0] if b.ndim == 3 else 0
    assert not (a_parts and ta) and not (b_parts and tb)
    if a_parts:
        m, kd = a.shape[1], a_parts * a.shape[2]
    else:
        m, kd = (a.shape[1], a.shape[0]) if ta else a.shape
    n = b_parts * b.shape[2] if b_parts else (b.shape[0] if tb else b.shape[1])
    tm, tn, tk = min(tm, m), min(tn, n), min(tk, kd)
    assert m % tm == 0 and n % tn == 0 and kd % tk == 0, (name, m, n, kd, tm, tn, tk)
    nk = kd // tk
    dims = ((0,) if ta else (1,), (1,) if tb else (0,))

    def body(a_ref, b_ref, o_ref, *scr):
        part = _dot(a_ref[0] if a_parts else a_ref[...], b_ref[0] if b_parts else b_ref[...], dims, hi)
        if nk == 1:
            o_ref[...] = part.astype(out_dtype)
        else:
            acc = scr[0]
            kk = pl.program_id(2)

            @pl.when(kk == 0)
            def _():
                acc[...] = part

            @pl.when(kk > 0)
            def _():
                acc[...] += part

            @pl.when(kk == nk - 1)
            def _():
                o_ref[...] = acc[...].astype(out_dtype)

    a_spec = pl.BlockSpec((tk, tm), lambda i, j, k: (k, i)) if ta else pl.BlockSpec((tm, tk), lambda i, j, k: (i, k))
    b_spec = pl.BlockSpec((tn, tk), lambda i, j, k: (j, k)) if tb else pl.BlockSpec((tk, tn), lambda i, j, k: (k, j))
    if a_parts:
        per = a.shape[2] // tk
        assert per * tk == a.shape[2]
        a_spec = pl.BlockSpec((1, tm, tk), lambda i, j, k: (k // per, i, k % per))
    if b_parts:
        per_n = b.shape[2] // tn
        assert per_n * tn == b.shape[2]
        b_spec = pl.BlockSpec((1, tk, tn), lambda i, j, k: (j // per_n, k, j % per_n))
    return pl.pallas_call(
        body, name=name, grid=(m // tm, n // tn, nk),
        in_specs=[a_spec, b_spec],
        out_specs=pl.BlockSpec((tm, tn), lambda i, j, k: (i, j)),
        out_shape=jax.ShapeDtypeStruct((m, n), out_dtype),
        scratch_shapes=[] if nk == 1 else [pltpu.VMEM((tm, tn), F32)],
        compiler_params=_cparams(("parallel", "parallel", "arbitrary")),
    )(a, b)


HALO = 8


class _In:
    def __init__(self, arr, w=None, cb=0, roff=0, halo=None, ridx=None):
        self.arr, self.w, self.cb, self.roff, self.halo = arr, w or arr.shape[1], cb, roff, halo
        self.ridx = ridx or (lambda i, roff=roff: i + roff)


class _Full:
    def __init__(self, arr, w=None, cb=0):
        self.arr, self.w, self.cb = arr, w, cb


class _Out:
    def __init__(self, cols, dtype=F32, w=None, cb=0, acc=False, rows=1, roff=0, nrows=None, stack=0, into=None):
        self.cols, self.dtype, self.w, self.cb, self.acc, self.rows, self.roff, self.nrows, self.stack = (
            cols, dtype, w or cols, cb, acc, rows, roff, nrows, stack)
        self.into = into


def _alias_outs(arrays, specs, outs):
    aliases = {}
    for k, o in enumerate(outs):
        if o.into is not None:
            aliases[len(arrays)] = k
            arrays.append(o.into)
            specs.append(pl.BlockSpec(memory_space=pl.ANY))
    return aliases


def _rowcall(name, fn, nrow_tiles, tile, ins, outs, ncol=1):
    arrays, specs, kinds = [], [], []
    for x in ins:
        if isinstance(x, _Full):
            arrays.append(x.arr)
            if x.w is None:
                specs.append(pl.BlockSpec(x.arr.shape, lambda j, i: (0, 0)))
            else:
                specs.append(pl.BlockSpec((x.arr.shape[0], x.w), lambda j, i, cb=x.cb: (0, cb + j)))
            kinds.append("full")
            continue
        w, cb, roff = x.w, x.cb, x.roff
        cur = pl.BlockSpec((tile, w), lambda j, i, cb=cb, ridx=x.ridx: (ridx(i), cb + j))
        if x.halo is None:
            arrays.append(x.arr)
            specs.append(cur)
            kinds.append("tile")
        else:
            r8 = tile // HALO
            last = x.arr.shape[0] // HALO - 1
            prev = pl.BlockSpec((HALO, w), lambda j, i, cb=cb, roff=roff, r8=r8: (jnp.maximum((i + roff) * r8 - 1, 0), cb + j))
            nxt = pl.BlockSpec((HALO, w), lambda j, i, cb=cb, roff=roff, r8=r8, last=last:
                               (jnp.minimum((i + roff + 1) * r8, last), cb + j))
            arrays += [x.arr, x.arr, x.arr]
            specs += [prev, cur, nxt]
            kinds.append(("halo", x.halo))
    out_specs, out_shapes = [], []
    for o in outs:
        if o.acc:
            out_specs.append(pl.BlockSpec((o.rows, o.w), lambda j, i, cb=o.cb: (0, cb + j)))
            out_shapes.append(jax.ShapeDtypeStruct((o.rows, o.cols), o.dtype))
        elif o.stack:
            out_specs.append(pl.BlockSpec((o.stack, tile, o.w), lambda j, i, cb=o.cb: (0, i, cb + j)))
            out_shapes.append(jax.ShapeDtypeStruct((o.stack, nrow_tiles * tile, o.cols), o.dtype))
        else:
            out_specs.append(pl.BlockSpec((tile, o.w), lambda j, i, cb=o.cb, roff=o.roff: (i + roff, cb + j)))
            out_shapes.append(jax.ShapeDtypeStruct(((o.nrows or nrow_tiles * tile), o.cols), o.dtype))
    aliases = _alias_outs(arrays, specs, outs)
    n_in = len(arrays)

    def body(*refs):
        j = pl.program_id(0)
        i = pl.program_id(1)
        vals, r = [], 0
        for kind in kinds:
            if kind in ("full", "tile"):
                vals.append(refs[r][...])
                r += 1
            else:
                pok, nok = kind[1]
                p, c, n = refs[r][...], refs[r + 1][...], refs[r + 2][...]
                p = jnp.where(pok(i), p, jnp.zeros_like(p))
                n = jnp.where(nok(i), n, jnp.zeros_like(n))
                vals.append(jnp.concatenate([p, c, n], axis=0))
                r += 3
        res = fn(i, j, *vals)
        for o, ref, val in zip(outs, refs[n_in:], res):
            if o.acc:
                @pl.when(i == 0)
                def _(ref=ref, val=val, o=o):
                    ref[...] = val.astype(o.dtype)

                @pl.when(i > 0)
                def _(ref=ref, val=val, o=o):
                    ref[...] += val.astype(o.dtype)
            elif o.stack:
                for s_ in range(o.stack):
                    ref[s_] = val[s_].astype(o.dtype)
            else:
                ref[...] = val.astype(o.dtype)

    return pl.pallas_call(
        body, name=name, grid=(ncol, nrow_tiles), in_specs=specs, out_specs=out_specs, out_shape=out_shapes,
        input_output_aliases=aliases, compiler_params=_cparams(("parallel", "arbitrary")),
    )(*arrays)


def _mm_ep(name, a, b, tb, tm, tk, fn, ins, outs, hi=False):
    a_parts = a.shape[0] if a.ndim == 3 else 0
    m, kd = (a.shape[1], a_parts * a.shape[2]) if a_parts else a.shape
    n = b.shape[0] if tb else b.shape[1]
    tk = min(tk, kd)
    assert m % tm == 0 and kd % tk == 0, (name, m, kd, tm, tk)
    nk = kd // tk
    dims = ((1,), (1,) if tb else (0,))
    if a_parts:
        per = a.shape[2] // tk
        arrays, specs = [a], [pl.BlockSpec((1, tm, tk), lambda i, k: (k // per, i, k % per))]
    else:
        arrays, specs = [a], [pl.BlockSpec((tm, tk), lambda i, k: (i, k))]
    arrays.append(b)
    specs.append(pl.BlockSpec((n, tk), lambda i, k: (0, k)) if tb else pl.BlockSpec((tk, n), lambda i, k: (k, 0)))
    for x in ins:
        arrays.append(x.arr)
        if isinstance(x, _Full):
            specs.append(pl.BlockSpec(x.arr.shape, lambda i, k: (0, 0)))
        else:
            specs.append(pl.BlockSpec((tm, x.w), lambda i, k, cb=x.cb, ridx=x.ridx: (ridx(i), cb)))
    out_specs, out_shapes = [], []
    for o in outs:
        if o.acc:
            out_specs.append(pl.BlockSpec((o.rows, o.w), lambda i, k, cb=o.cb: (0, cb)))
            out_shapes.append(jax.ShapeDtypeStruct((o.rows, o.cols), o.dtype))
        else:
            out_specs.append(pl.BlockSpec((tm, o.w), lambda i, k, cb=o.cb, roff=o.roff: (i + roff, cb)))
            out_shapes.append(jax.ShapeDtypeStruct((o.nrows or m, o.cols), o.dtype))
    n_vals = len(arrays)
    aliases = _alias_outs(arrays, specs, outs)
    n_in = len(arrays)

    def body(*refs):
        i, kk = pl.program_id(0), pl.program_id(1)
        a_ref, b_ref = refs[0], refs[1]
        acc_ref = refs[-1]
        part = _dot(a_ref[0] if a_parts else a_ref[...], b_ref[...], dims, hi)

        @pl.when(kk == 0)
        def _():
            acc_ref[...] = part

        @pl.when(kk > 0)
        def _():
            acc_ref[...] += part

        @pl.when(kk == nk - 1)
        def _():
            res = fn(i, acc_ref[...], *[r[...] for r in refs[2:n_vals]])
            for o, ref, val in zip(outs, refs[n_in:-1], res):
                if o.acc:
                    @pl.when(i == 0)
                    def _(ref=ref, val=val, o=o):
                        ref[...] = val.astype(o.dtype)

                    @pl.when(i > 0)
                    def _(ref=ref, val=val, o=o):
                        ref[...] += val.astype(o.dtype)
                else:
                    ref[...] = val.astype(o.dtype)

    return pl.pallas_call(
        body, name=name, grid=(m // tm, nk), in_specs=specs, out_specs=out_specs, out_shape=out_shapes,
        scratch_shapes=[pltpu.VMEM((tm, n), F32)], input_output_aliases=aliases,
        compiler_params=_cparams(("arbitrary", "arbitrary")),
    )(*arrays)


def _shift(xe, s, tile):
    if s == 0:
        return xe[HALO:HALO + tile]
    return pltpu.roll(xe, (-s) % xe.shape[0], 0)[HALO:HALO + tile]


def _silu(x):
    return x * jax.nn.sigmoid(x)


def _dsilu(x):
    s = jax.nn.sigmoid(x)
    return s * (1.0 + x * (1.0 - s))


def _heads(x, fn):
    return jnp.concatenate([fn(h, x[:, h * HD:(h + 1) * HD]) for h in range(x.shape[1] // HD)], axis=1)


def _colsum(x):
    return jnp.sum(x, axis=0, keepdims=True)


def _rowmean(x):
    return jnp.mean(x, axis=1, keepdims=True)


def _rowsum(x):
    return jnp.sum(x, axis=1, keepdims=True)


TILE = 256
CT = CTX // TILE


def _all_halo(n_tiles):
    return (lambda i: i >= CT + 1, lambda i: jnp.logical_and(i >= CT, i < n_tiles - 1))


def _lat_halo(n_tiles):
    return (lambda i: i >= 1, lambda i: i < n_tiles - 1)


def _rms_mod(x, nm, shift, scale):
    r = lax.rsqrt(_rowmean(x * x) + EPS)
    return (x * r * nm) * (1.0 + scale) + shift


def _rms_mod_bwd(dh, x, nm, scale):
    r = lax.rsqrt(_rowmean(x * x) + EPS)
    xn = x * r
    dz = dh * (1.0 + scale)
    dxn = dz * nm
    dx = r * (dxn - xn * _rowmean(dxn * xn))
    return dx, _colsum(dz * xn), _colsum(dh), _colsum(dh * (xn * nm))


def _norm_mod(x, ctx, nm, mod_c, mod_x):
    n = (x.shape[0] + ctx.shape[0]) // TILE

    def fn(i, j, c_, x_, nm_, mc, mx):
        m = jnp.where(i < CT, mc, mx)
        return (_rms_mod(jnp.where(i < CT, c_, x_), nm_, m[0:1], m[1:2]),)

    ins = [_In(ctx, ridx=lambda i: jnp.minimum(i, CT - 1)), _In(x, ridx=lambda i: jnp.maximum(i - CT, 0)),
           _Full(nm), _Full(mod_c), _Full(mod_x)]
    return _rowcall("norm_mod", fn, n, TILE, ins, [_Out(D, BF16)])[0]


def _norm_mod_bwd(dh, xs, dres, nm, mod, roff, n):
    ins = [_In(dh, roff=roff), _In(xs), _Full(nm), _Full(mod)] + ([] if dres is None else [_In(dres)])

    def fn(i, j, dh_, x, nm_, m, *rest):
        dx, dn, dsh, dsc = _rms_mod_bwd(dh_, x, nm_, m[1:2])
        if rest:
            return (dx + rest[0], dn, dsh, dsc)
        return (dn, dsh, dsc)

    accs = [_Out(D, acc=True), _Out(D, acc=True), _Out(D, acc=True)]
    return _rowcall("norm_mod_bwd", fn, n, TILE, ins, ([] if dres is None else [_Out(D)]) + accs)


DN_Q_SCALE = HD ** -0.5


def _conv_taps(xe, w, width, rows=None):
    r = width // 2
    acc = None
    for t in range(width):
        s = t - r
        if rows is None:
            sh = xe if s == 0 else pltpu.roll(xe, (-s) % xe.shape[0], 0)
        else:
            sh = _shift(xe, s, rows)
        term = sh * w[t:t + 1]
        acc = term if acc is None else acc + term
    return acc


def _rolled(xe, width):
    r = width // 2
    return [xe if t == r else pltpu.roll(xe, (r - t) % xe.shape[0], 0) for t in range(width)]


def _conv_bwd(rolled, w, c_grad, width):
    r = width // 2
    cc = c_grad[HALO:HALO + TILE]
    dx, dws = None, []
    for t in range(width):
        term = _shift(c_grad, r - t, TILE) * w[t:t + 1]
        dx = term if dx is None else dx + term
        dws.append(_colsum(cc * rolled[t][HALO:HALO + TILE]))
    return dx, jnp.concatenate(dws + [jnp.zeros((8 - width, cc.shape[1]), F32)], axis=0)


def _silu_both(x):
    s = jax.nn.sigmoid(x)
    return x * s, s * (1.0 + x * (1.0 - s))


def _l2n(x, scale):
    rn = lax.rsqrt(_rowsum(x * x) + EPS)
    return x * (rn * scale)


def _l2n_bwd(dy, x, scale):
    rn = lax.rsqrt(_rowsum(x * x) + EPS)
    xu = x * rn
    return (scale * rn) * (dy - xu * _rowsum(dy * xu))


def _softplus(x):
    return jnp.maximum(x, 0.0) + jnp.log(1.0 + jnp.exp(-jnp.abs(x)))


def _lane_mask(lo, hi_):
    lane = lax.broadcasted_iota(jnp.int32, (1, 128), 1)
    return jnp.logical_and(lane >= lo, lane < hi_).astype(F32)


def _dn_prep(p, conv_w, gprm):
    n = p.shape[0] // TILE
    halo = _all_halo(n)

    def fn(i, j, qe, ke, ve, ba, w, gp):
        cq = _conv_taps(qe, w[:, 0:D], 5, TILE)
        ck = _conv_taps(ke, w[:, D:2 * D], 5, TILE)
        cv = _conv_taps(ve, w[:, 2 * D:3 * D], 5, TILE)
        q = _heads(_silu(cq), lambda h, x: _l2n(x, DN_Q_SCALE))
        k = _heads(_silu(ck), lambda h, x: _l2n(x, 1.0))
        v = _silu(cv)
        beta = jax.nn.sigmoid(ba)
        g = -jnp.exp(gp[0:1]) * _softplus(ba + gp[1:2])
        m0, m1 = _lane_mask(0, 8), _lane_mask(8, 16)
        gb_f = beta * m0 + pltpu.roll(g, 128 - 8, 1) * m1
        gb_b = pltpu.roll(beta, 128 - 8, 1) * m0 + pltpu.roll(g, 128 - 16, 1) * m1
        return q, k, v, gb_f, gb_b

    ins = [_In(p, D, 0, halo=halo), _In(p, D, 1, halo=halo), _In(p, D, 2, halo=halo), _In(p, 128, C_BA // 128),
           _Full(conv_w), _Full(gprm)]
    return _rowcall("dn_prep", fn, n, TILE, ins, [_Out(D), _Out(D), _Out(D), _Out(128), _Out(128)])


def _dn_prep_bwd(p, conv_w, gprm, dq2, dk2, dv2, dgb2, dk_at, dv_at, dp):
    n = p.shape[0] // TILE
    halo = _all_halo(n)

    def branch(xe, w, dye, scale):
        rolled = _rolled(xe, 5)
        c = rolled[0] * w[0:1]
        for t in range(1, 5):
            c = c + rolled[t] * w[t:t + 1]
        sx, dsilu = _silu_both(c)
        if scale is None:
            dsx = dye
        else:
            dsx = jnp.concatenate([_l2n_bwd(dye[:, h * HD:(h + 1) * HD], sx[:, h * HD:(h + 1) * HD], scale)
                                   for h in range(NH)], axis=1)
        return _conv_bwd(rolled, w, dsx * dsilu, 5)

    def fn(i, j, qe, ke, ve, ba, w, gp, dq0, dq1, dk0, dk1, dv0, dv1, dg0, dg1, dka, dva):
        dxq, dwq = branch(qe, w[:, 0:D], dq0 + dq1, DN_Q_SCALE)
        dxk, dwk = branch(ke, w[:, D:2 * D], dk0 + dk1, 1.0)
        dxv, dwv = branch(ve, w[:, 2 * D:3 * D], dv0 + dv1, None)
        m0, m1 = _lane_mask(0, 8), _lane_mask(8, 16)
        dbeta = dg0 * m0 + pltpu.roll(dg1 * m0, 8, 1)
        dg = pltpu.roll(dg0 * m1, 8, 1) + pltpu.roll(dg1 * m1, 16, 1)
        beta = jax.nn.sigmoid(ba)
        ea = jnp.exp(gp[0:1])
        z = ba + gp[1:2]
        g = -ea * _softplus(z)
        mg = _lane_mask(16, 32)
        da = dg * (-ea) * jax.nn.sigmoid(z) * mg
        dba = dbeta * beta * (1.0 - beta) * _lane_mask(0, 16) + da
        dgp = jnp.concatenate([_colsum(dg * g * mg), _colsum(da)], axis=0)
        half = jnp.concatenate([dxq, dxk, dxv, dka.astype(F32), dva.astype(F32), dba, jnp.zeros((TILE, PH - C_PAD), F32)],
                               axis=1)
        return (half, jnp.concatenate([dwq, dwk, dwv], axis=1), dgp)

    ins = [_In(p, D, 0, halo=halo), _In(p, D, 1, halo=halo), _In(p, D, 2, halo=halo), _In(p, 128, C_BA // 128),
           _Full(conv_w), _Full(gprm),
           _In(dq2, halo=halo), _In(dq2, roff=n, halo=halo), _In(dk2, halo=halo), _In(dk2, roff=n, halo=halo),
           _In(dv2, halo=halo), _In(dv2, roff=n, halo=halo), _In(dgb2), _In(dgb2, roff=n), _In(dk_at), _In(dv_at)]
    return _rowcall("dn_prep_bwd", fn, n, TILE, ins,
                    [_Out(PW, BF16, w=PH, cb=0, into=dp), _Out(3 * D, acc=True, rows=8), _Out(128, acc=True, rows=2)])


def _hnorm(x, w):
    return x * lax.rsqrt(_rowmean(x * x) + EPS) * w


def _hnorm_bwd(dy, x, w):
    r = lax.rsqrt(_rowmean(x * x) + EPS)
    xh = x * r
    dxh = dy * w
    return r * (dxh - xh * _rowmean(dxh * xh)), _colsum(dy * xh)


def _dn_gate_mm(o2, p, dn_norm, w_bdn, n_all, hi):
    n = n_all - CT

    def fn(i, j, of, ob, gt, w, wb):
        o = of + ob
        y = _heads(o, lambda h, x: _hnorm(x, w)) * _silu(gt)
        return y, _dot(y, wb, NN, hi)

    ins = [_In(o2, roff=CT), _In(o2, roff=n_all + CT), _In(p, D, C_GT // D, roff=CT), _Full(dn_norm), _Full(w_bdn)]
    return _rowcall("dn_gate_mm", fn, n, TILE, ins, [_Out(D, BF16), _Out(D)])


def _mm_bdn_dx_gate(dz_dn, w_bdn, o2, p, dn_norm, n_all, dp, hi):
    def fn(i, dy_, of, ob, gt, w):
        o = of + ob
        sg, dsg = _silu_both(gt)
        dos, dw = [], jnp.zeros((1, HD), F32)
        yn = []
        for h in range(NH):
            sl = slice(h * HD, (h + 1) * HD)
            dx, dwh = _hnorm_bwd(dy_[:, sl] * sg[:, sl], o[:, sl], w)
            dos.append(dx)
            dw = dw + dwh
            yn.append(_hnorm(o[:, sl], w))
        dgt = dy_ * jnp.concatenate(yn, axis=1) * dsg
        return jnp.concatenate(dos, axis=1), dgt, dw

    ins = [_In(o2, roff=CT), _In(o2, roff=n_all + CT), _In(p, D, C_GT // D, roff=CT), _Full(dn_norm)]
    outs = [_Out(D), _Out(PW, BF16, w=D, cb=C_GT // D, roff=CT, nrows=p.shape[0], into=dp), _Out(HD, acc=True)]
    return _mm_ep("mm_bdn_dx_gate", dz_dn, w_bdn, True, TILE, D, fn, ins, outs, hi)


def _rope_shuffle(x):
    lane = lax.broadcasted_iota(jnp.int32, (1, HD), 1)
    return jnp.where((lane % 64) < 32, pltpu.roll(x, HD - 32, 1), pltpu.roll(x, 32, 1))


def _rope(x, cos, sin):
    return x * cos + _rope_shuffle(x) * sin


def _rope_bwd(dy, cos, sin):
    return dy * cos + _rope_shuffle(dy * sin)


def _attn_prep(p, w, cos, sin, width, cb, roff, n, name):
    def fn(i, j, x, w_, c, s):
        return (_heads(x, lambda h, xh: _rope(_hnorm(xh, w_), c, s)),)

    ins = [_In(p, width, cb, roff=roff), _Full(w), _In(cos), _In(sin)]
    return _rowcall(name, fn, n, TILE, ins, [_Out(width)])[0]


def _attn_prep_bwd(dy, p, w, cos, sin, width, cb, roff, n, name, dx_out):
    def fn(i, j, dy_, x, w_, c, s):
        dxs, dw = [], jnp.zeros((1, HD), F32)
        for h in range(width // HD):
            sl = slice(h * HD, (h + 1) * HD)
            dx, dwh = _hnorm_bwd(_rope_bwd(dy_[:, sl], c, s), x[:, sl], w_)
            dxs.append(dx)
            dw = dw + dwh
        return jnp.concatenate(dxs, axis=1), dw

    ins = [_In(dy), _In(p, width, cb, roff=roff), _Full(w), _In(cos), _In(sin)]
    return _rowcall(name, fn, n, TILE, ins, [dx_out, _Out(HD, acc=True)])


def _mm_bat_merge(o_at, w_bat, z_dn, p, hi):
    def fn(i, za, zd, gd, ga):
        return za, jax.nn.sigmoid(gd) * zd + jax.nn.sigmoid(ga) * za

    ins = [_In(z_dn), _In(p, D, C_MG // D, roff=CT), _In(p, D, C_MG // D + 1, roff=CT)]
    return _mm_ep("mm_bat_merge", o_at, w_bat, False, TILE, D, fn, ins, [_Out(D), _Out(D, BF16)], hi)


def _mm_out_dx_merge(dmo, w_out, z_dn, z_at, p, hi):
    def fn(i, dm_, zd, za, gd, ga):
        sd, sa = jax.nn.sigmoid(gd), jax.nn.sigmoid(ga)
        dg = jnp.concatenate([dm_ * zd * sd * (1.0 - sd), dm_ * za * sa * (1.0 - sa)], axis=1)
        return dm_ * sd, dm_ * sa, dg

    ins = [_In(z_dn), _In(z_at), _In(p, D, C_MG // D, roff=CT), _In(p, D, C_MG // D + 1, roff=CT)]
    outs = [_Out(D, BF16), _Out(D, BF16), _Out(PW, BF16, w=2 * D, cb=C_MG // (2 * D), roff=CT, nrows=p.shape[0])]
    return _mm_ep("mm_out_dx_merge", dmo, w_out, True, TILE, D, fn, ins, outs, hi)


def _mm_out_resid(merged, w_out, x, g_a, nf, mod_f, hi):
    def fn(i, mo_, x_, ga, nf_, m):
        x1 = x_ + ga * mo_
        return mo_, x1, _rms_mod(x1, nf_, m[0:1], m[1:2])

    ins = [_In(x), _Full(g_a), _Full(nf), _Full(mod_f)]
    return _mm_ep("mm_out_resid", merged, w_out, False, min(512, x.shape[0]), D, fn, ins, [_Out(D), _Out(D), _Out(D, BF16)], hi)


def _mm_up_dx_norm(du, ffn_up, dy, x1, mo, g_a, nf, mod_f, hi):
    def fn(i, dh_, dy_, x1_, mo_, ga, nf_, m):
        dx, dn, dsh, dsc = _rms_mod_bwd(dh_, x1_, nf_, m[1:2])
        dx1 = dy_ + dx
        return dx1, ga * dx1, dn, dsh, dsc, _colsum(dx1 * mo_)

    ins = [_In(dy), _In(x1), _In(mo), _Full(g_a), _Full(nf), _Full(mod_f)]
    accs = [_Out(D, acc=True) for _ in range(4)]
    return _mm_ep("mm_up_dx_norm", du, ffn_up, True, min(512, x1.shape[0]), 1408, fn, ins, [_Out(D), _Out(D, BF16)] + accs, hi)


def _mm_down_loss(a, ffn_down, x1, tgt, g_f, hi):
    def fn(i, f_, x1_, t, gf):
        e = x1_ + gf * f_ - t
        dy = e * (1.0 / D)
        loss = _colsum(_rowsum(e * e)) * (0.5 / D)
        return dy, gf * dy, _colsum(dy * f_), jnp.broadcast_to(loss, (1, 128))

    ins = [_In(x1), _In(tgt), _Full(g_f)]
    outs = [_Out(D), _Out(D, BF16), _Out(D, acc=True), _Out(128, acc=True)]
    return _mm_ep("mm_down_loss", a, ffn_down, False, min(512, x1.shape[0]), DFF, fn, ins, outs, hi)


FW = DFF // 2


def _ffn_act(u, conv_w, conv_b, n):
    halo = _lat_halo(n)

    def fn(i, j, ge, ve, wg, wv, bg, bv):
        cg = _conv_taps(ge, wg, 3, TILE) + bg
        cv = _conv_taps(ve, wv, 3, TILE) + bv
        return (_silu(cg) * cv,)

    ins = [_In(u, FW, 0, halo=halo), _In(u, FW, 2, halo=halo), _Full(conv_w, FW, 0), _Full(conv_w, FW, 2),
           _Full(conv_b, FW, 0), _Full(conv_b, FW, 2)]
    return _rowcall("ffn_act", fn, n, TILE, ins, [_Out(DFF, BF16, FW)], ncol=2)[0]


def _ffn_act_bwd(u, da, conv_w, conv_b, n):
    halo = _lat_halo(n)

    def fn(i, j, ge, ve, dae, wg, wv, bg, bv):
        rg, rv = _rolled(ge, 3), _rolled(ve, 3)
        cg = rg[0] * wg[0:1] + rg[1] * wg[1:2] + rg[2] * wg[2:3] + bg
        cv = rv[0] * wv[0:1] + rv[1] * wv[1:2] + rv[2] * wv[2:3] + bv
        sg, dsg = _silu_both(cg)
        dcg = dae * cv * dsg
        dcv = dae * sg
        dxg, dwg = _conv_bwd(rg, wg, dcg, 3)
        dxv, dwv = _conv_bwd(rv, wv, dcv, 3)
        return (dxg, dxv), dwg, dwv, _colsum(dcg[HALO:HALO + TILE]), _colsum(dcv[HALO:HALO + TILE])

    ins = [_In(u, FW, 0, halo=halo), _In(u, FW, 2, halo=halo), _In(da, FW, 0, halo=halo),
           _Full(conv_w, FW, 0), _Full(conv_w, FW, 2), _Full(conv_b, FW, 0), _Full(conv_b, FW, 2)]
    outs = [_Out(DFF, BF16, FW, stack=2), _Out(DFF, w=FW, acc=True, rows=8), _Out(DFF, w=FW, acc=True, rows=8),
            _Out(DFF, w=FW, acc=True), _Out(DFF, w=FW, acc=True)]
    return _rowcall("ffn_act_bwd", fn, n, TILE, ins, outs, ncol=2)


def _rope_tables(tl):
    rows = tl // GRID_W
    inv = np.float32(ROPE_BASE) ** (-np.arange(32, dtype=np.float32) / np.float32(32))
    ar = np.arange(rows, dtype=np.float32)[:, None] * inv
    ac = np.arange(GRID_W, dtype=np.float32)[:, None] * inv

    def table(r, c):
        full = (rows, GRID_W, HD // 2)
        return jnp.concatenate([jnp.broadcast_to(jnp.asarray(r)[:, None, :], full),
                                jnp.broadcast_to(jnp.asarray(c)[None, :, :], full)], axis=2).reshape(tl, HD)

    two = lambda a, b: np.concatenate([a, b], axis=1).astype(np.float32)
    cos = table(two(np.cos(ar), np.cos(ar)), two(np.cos(ac), np.cos(ac)))
    sin = table(two(-np.sin(ar), np.sin(ar)), two(-np.sin(ac), np.sin(ac)))
    return cos, sin


def _pad_w_in(w_in):
    return jnp.concatenate([w_in[:, 0:3072], w_in[:, 5152:5664], w_in[:, 4096:4128], jnp.zeros((D, 96 + C_GT - C_PAD), w_in.dtype),
                            w_in[:, 3072:4096], w_in[:, 4128:5152], w_in[:, 5664:7712]], axis=1)


def _unpad_w_in(g, axis=1):
    cut = lambda a, b: lax.slice_in_dim(g, a, b, axis=axis)
    return jnp.concatenate([cut(0, 3072), cut(C_GT, C_GT + D), cut(C_BA, C_BA + 32), cut(C_QAT, C_QAT + D),
                            cut(C_KAT, C_KAT + 512), cut(C_MG, C_MG + 2 * D)], axis=axis)


def _local_step(x, ctx, tgt, mod_x, mod_c, w, hi=False):
    tl = x.shape[0]
    t_all = tl + CTX
    n_all, n = t_all // TILE, tl // TILE
    tm_all = 1280 if t_all % 1280 == 0 else TILE
    tm_lat = 1024
    mm = functools.partial(_mm, hi=hi)
    sp = lambda m: [m[:, k * D:(k + 1) * D] for k in range(6)]
    sh_a, sc_a, g_a, sh_f, sc_f, g_f = sp(mod_x)
    sh_ac, sc_ac = sp(mod_c)[:2]
    mod_ax = jnp.concatenate([sh_a, sc_a], axis=0)
    mod_ac = jnp.concatenate([sh_ac, sc_ac], axis=0)
    mod_f = jnp.concatenate([sh_f, sc_f], axis=0)
    nm, nf = w["norm_mix"], w["norm_ffn"]
    cos, sin = _rope_tables(tl)
    cos_all = jnp.concatenate([jnp.ones((CTX, HD), F32), cos], axis=0)
    sin_all = jnp.concatenate([jnp.zeros((CTX, HD), F32), sin], axis=0)
    conv_dn = jnp.concatenate([w["dn_conv"], jnp.zeros((3, 3 * D), F32)], axis=0)
    gprm = jnp.concatenate([jnp.zeros((2, 16), F32),
                            jnp.concatenate([w["dn_a_log"].reshape(1, 16), w["dn_dt_bias"].reshape(1, 16)], axis=0),
                            jnp.zeros((2, 96), F32)], axis=1)
    conv_ff = jnp.concatenate([w["ffn_conv"], jnp.zeros((5, 2 * DFF), F32)], axis=0)
    sink = jnp.concatenate([w["attn_sink"].reshape(1, NH), jnp.zeros((1, 128 - NH), F32)], axis=1)
    nct = CTX // CH

    h = _norm_mod(x, ctx, nm, mod_ac, mod_ax)
    p = mm(h, w["w_in_p"], tm=tm_all, tn=2048, name="mm_in")
    q, k, v, gb_f, gb_b = _dn_prep(p, conv_dn, gprm)
    gb = jnp.stack([gb_f, gb_b])
    dn_u, dn_w, dn_qg, dn_kd, dn_pm, dn_t = _dn_intra_fwd(q, k, v, gb, nct, hi)
    o2, s_hist, dn_vn = _dn_seq_fwd(dn_u, dn_w, dn_qg, dn_kd, dn_pm, gb, nct, hi)
    o2 = o2.reshape(2 * t_all, D)
    y_dn, z_dn = _dn_gate_mm(o2, p, w["dn_norm"], w["w_branch_dn"], n_all, hi)
    qr = _attn_prep(p, w["q_norm"], cos, sin, D, C_QAT // D, CT, n, "attn_prep_q")
    kr = _attn_prep(p, w["k_norm"], cos_all, sin_all, KVH * HD, C_KAT // (KVH * HD), 0, n_all, "attn_prep_k")
    vv = p[:, C_VAT:C_VAT + KVH * HD]
    o_at, lse = _attn_fwd(qr, kr, vv, sink, hi)
    z_at, merged = _mm_bat_merge(o_at, w["w_branch_attn"], z_dn, p, hi)
    mo, x1, h2 = _mm_out_resid(merged, w["w_out"], x, g_a, nf, mod_f, hi)
    u = mm(h2, w["ffn_up"], tm=2 * tm_lat, tn=1408, name="mm_up")
    a = _ffn_act(u, conv_ff, w["ffn_conv_b"], n)
    dy, df, dg_f, loss = _mm_down_loss(a, w["ffn_down"], x1, tgt, g_f, hi)

    g = {}
    da = mm(df, w["ffn_down"], tb=True, tm=tm_lat, tn=1408, name="mm_down_dx")
    g["ffn_down"] = mm(a, df, ta=True, tm=1408, tn=1024, tk=tm_lat, name="mm_down_dw")
    du, dcw_g, dcw_v, dcb_g, dcb_v = _ffn_act_bwd(u, da, conv_ff, w["ffn_conv_b"], n)
    g["ffn_conv"] = jnp.concatenate([dcw_g, dcw_v], axis=1)[0:3]
    g["ffn_conv_b"] = jnp.concatenate([dcb_g, dcb_v], axis=1)
    g["ffn_up"] = mm(h2, du, ta=True, tm=1024, tn=1408, tk=tm_lat, name="mm_up_dw")
    dx1, dmo, g["norm_ffn"], dsh_f, dsc_f, dg_a = _mm_up_dx_norm(du, w["ffn_up"], dy, x1, mo, g_a, nf, mod_f, hi)
    g["w_out"] = mm(merged, dmo, ta=True, tm=1024, tk=tm_lat, name="mm_out_dw")
    dz_dn, dz_at, dmg = _mm_out_dx_merge(dmo, w["w_out"], z_dn, z_at, p, hi)
    g["w_branch_dn"] = mm(y_dn, dz_dn, ta=True, tm=1024, tk=tm_lat, name="mm_bdn_dw")
    do_at, delta = _mm_bat_dx_delta(dz_at, w["w_branch_attn"], o_at, hi)
    g["w_branch_attn"] = mm(o_at, dz_at, ta=True, tm=1024, tk=tm_lat, name="mm_bat_dw")

    do_dn, dp, g["dn_norm"] = _mm_bdn_dx_gate(dz_dn, w["w_branch_dn"], o2, p, w["dn_norm"], n_all, dmg, hi)
    do_all = do_dn
    dn_dvn, dn_dw, dn_dqg, dn_dkd, dn_del = _dn_seq_bwd(dn_w, dn_qg, dn_kd, dn_pm, dn_vn, s_hist, gb, do_all, nct, hi)
    dq2, dk2, dv2, dgb2 = _dn_intra_bwd(q, k, v, gb, dn_u, dn_w, dn_t, dn_vn, dn_dvn, dn_dw, dn_dqg, dn_dkd, dn_del,
                                        do_all, nct, hi)

    dqr, dk_lat, dv_lat, dkx, dvx, dsink = _attn_bwd(qr, kr, vv, sink, do_at, lse, delta, hi)
    g["attn_sink"] = dsink[:, 0:NH]
    q_out = _Out(PW, BF16, w=D, cb=C_QAT // D, roff=CT, nrows=t_all, into=dp)
    dp, g["q_norm"] = _attn_prep_bwd(dqr, p, w["q_norm"], cos, sin, D, C_QAT // D, CT, n, "attn_prep_q_bwd", q_out)
    dkr = jnp.concatenate([dkx, dk_lat], axis=0)
    dk_at, g["k_norm"] = _attn_prep_bwd(dkr, p, w["k_norm"], cos_all, sin_all, KVH * HD, C_KAT // (KVH * HD), 0, n_all,
                                        "attn_prep_k_bwd", _Out(KVH * HD, BF16))
    dv_at = jnp.concatenate([dvx, dv_lat], axis=0).astype(BF16)

    dp, dconv, dgprm = _dn_prep_bwd(p, conv_dn, gprm, dq2.reshape(2 * t_all, D), dk2.reshape(2 * t_all, D),
                                    dv2.reshape(2 * t_all, D), dgb2.reshape(2 * t_all, 128), dk_at, dv_at, dp)
    g["dn_conv"] = dconv[0:5]
    g["dn_a_log"] = dgprm[0, 16:32].reshape(2, NH)
    g["dn_dt_bias"] = dgprm[1, 16:32].reshape(2, NH)
    dp = lax.dynamic_update_slice(dp, jnp.zeros((CTX, PH), BF16), (0, PH))
    dh = mm(dp, w["w_in_p"], tb=True, tm=tm_all, tn=1024, tk=2048, name="mm_in_dx")
    g["w_in_p"] = mm(h, dp, ta=True, tm=1024, tn=2048, tk=tm_all, name="mm_in_dw")
    dnm_c, dsh_ac, dsc_ac = _norm_mod_bwd(dh, ctx, None, nm, mod_ac, 0, CT)
    grad_x, dnm_x, dsh_a, dsc_a = _norm_mod_bwd(dh, x, dx1, nm, mod_ax, CT, n)
    g["norm_mix"] = dnm_c + dnm_x
    dmod_x = jnp.concatenate([dsh_a, dsc_a, dg_a, dsh_f, dsc_f, dg_f], axis=1)
    dmod_c = jnp.concatenate([dsh_ac, dsc_ac, jnp.zeros((1, 4 * D), F32)], axis=1)
    return loss, grad_x, g, dmod_x, dmod_c


def _sum_slots(buf, n_slots, rows, tile, name, stride=1):
    nt = rows // tile

    def fn(i, j, *vals):
        acc = vals[0]
        for v in vals[1:]:
            acc = acc + v
        return (acc,)

    ins = [_In(buf, roff=k * stride * nt) for k in range(n_slots)]
    return _rowcall(name, fn, nt, tile, ins, [_Out(buf.shape[1])])[0]


ADAM_LR, ADAM_B1, ADAM_B2, ADAM_EPS, ADAM_WD, ADAM_STEP = 0.001, 0.9, 0.999, 1e-08, 0.01, 10


def _row_tile(rows, cols):
    for t in (512, 256, 128, 64, 32, 16, 8):
        if rows % t == 0 and t * cols * 4 * 14 <= 40 * 1024 * 1024:
            return t
    return rows


def _adamw(w, g, m, v, name):
    shape = w.shape
    cols = shape[-1]
    rows = max(1, math.prod(shape[:-1]))
    tile = _row_tile(rows, cols)
    c1 = 1.0 / (1.0 - ADAM_B1 ** ADAM_STEP)
    c2 = 1.0 / (1.0 - ADAM_B2 ** ADAM_STEP)

    def fn(i, j, w_, g_, m_, v_):
        mn = ADAM_B1 * m_ + (1.0 - ADAM_B1) * g_
        vn = ADAM_B2 * v_ + (1.0 - ADAM_B2) * (g_ * g_)
        delta = -ADAM_LR * ((mn * c1) / (jnp.sqrt(vn * c2) + ADAM_EPS) + ADAM_WD * w_)
        return delta, mn, vn

    r2 = lambda a: a.reshape(rows, cols)
    outs = _rowcall(name, fn, rows // tile, tile, [_In(r2(w)), _In(r2(g)), _In(r2(m)), _In(r2(v))],
                    [_Out(cols), _Out(cols), _Out(cols)])
    return [o.reshape(shape) for o in outs]


MESH = pl.DeviceIdType.MESH
ANY = pl.BlockSpec(memory_space=pl.ANY)


def _pos():
    return lax.axis_index("x"), lax.axis_index("y"), lax.axis_index("c")


def _all_gather_many(blks, name):
    na = len(blks)

    def body(*refs):
        x_refs, out_refs = refs[:na], refs[na:2 * na]
        send_sems, recv_sems, local_sems = refs[2 * na:]
        x, y, c = _pos()
        me, sibling = (x, y, c), (x, y, 1 - c)
        chips = [(1 - x, y), (x, 1 - y), (1 - x, 1 - y)]

        def rows(a, px, py, pc):
            m_per = blks[a].shape[0]
            return out_refs[a].at[pl.ds(pl.multiple_of((4 * px + 2 * py + pc) * m_per, 8), m_per), :]

        def copy(a, k, block, to, src=None):
            return pltpu.make_async_remote_copy(
                src_ref=rows(a, *block) if src is None else src, dst_ref=rows(a, *block),
                send_sem=send_sems.at[7 * a + k], recv_sem=recv_sems.at[7 * a + k], device_id=to, device_id_type=MESH)

        every = range(na)
        mine = [pltpu.make_async_copy(x_refs[a], rows(a, *me), local_sems.at[a]) for a in every]
        for cp in mine:
            cp.start()
        first = [copy(a, 0, me, sibling, src=x_refs[a]) for a in every]
        first += [copy(a, 1 + j, me, (*chip, c), src=x_refs[a]) for j, chip in enumerate(chips) for a in every]
        for cp in first:
            cp.start()
        passed = []
        for j, chip in enumerate(chips):
            for a in every:
                copy(a, 1 + j, (*chip, c), me).wait_recv()
                passed.append(copy(a, 4 + j, (*chip, c), sibling))
                passed[-1].start()
        for a in every:
            copy(a, 0, sibling, me).wait_recv()
        for j, chip in enumerate(chips):
            for a in every:
                copy(a, 4 + j, (*chip, 1 - c), me).wait_recv()
        for cp in first + passed:
            cp.wait_send()
        for cp in mine:
            cp.wait()

    return pl.pallas_call(
        body, name=name,
        out_shape=[jax.ShapeDtypeStruct((N_DEV * b.shape[0], b.shape[1]), b.dtype) for b in blks],
        in_specs=[ANY] * na, out_specs=[ANY] * na,
        scratch_shapes=[pltpu.SemaphoreType.DMA((7 * na,)), pltpu.SemaphoreType.DMA((7 * na,)), pltpu.SemaphoreType.DMA((na,))],
        compiler_params=pltpu.CompilerParams(has_side_effects=True),
    )(*blks)


def _all_gather(blk, name):
    return _all_gather_many([blk], name)[0]


def _flip(v, bit):
    return 1 - v if bit else v


D2D_STREAMS = 8
ICI_STREAMS = 2


def _sibling_exchange(src, seg_rows, n_seg, paired, name):
    n = src.shape[1]
    per_seg = D2D_STREAMS // n_seg
    per = seg_rows // per_seg
    assert per_seg * n_seg == D2D_STREAMS and per * per_seg == seg_rows and per % 16 == 0

    def body(x_ref, out_ref, send_sems, recv_sems):
        x, y, c = _pos()
        copies = []
        for s in range(n_seg):
            base = (2 * s + (1 - c)) * seg_rows if paired else s * seg_rows
            for j in range(per_seg):
                i = s * per_seg + j
                cp = pltpu.make_async_remote_copy(
                    src_ref=x_ref.at[pl.ds(pl.multiple_of(base + j * per, 16), per), :],
                    dst_ref=out_ref.at[pl.ds(s * seg_rows + j * per, per), :],
                    send_sem=send_sems.at[i], recv_sem=recv_sems.at[i], device_id=(x, y, 1 - c), device_id_type=MESH)
                cp.start()
                copies.append(cp)
        for cp in copies:
            cp.wait_recv()
        for cp in copies:
            cp.wait_send()

    return pl.pallas_call(
        body, name=name, out_shape=jax.ShapeDtypeStruct((n_seg * seg_rows, n), src.dtype),
        in_specs=[ANY], out_specs=ANY,
        scratch_shapes=[pltpu.SemaphoreType.DMA((D2D_STREAMS,)), pltpu.SemaphoreType.DMA((D2D_STREAMS,))],
        compiler_params=pltpu.CompilerParams(has_side_effects=True),
    )(src)


def _transpose_cast(x, dtype, name):
    r, c = x.shape
    tc = 512

    def body(x_ref, o_ref):
        o_ref[...] = x_ref[...].T.astype(o_ref.dtype)

    return pl.pallas_call(
        body, name=name, grid=(c // tc,),
        in_specs=[pl.BlockSpec((r, tc), lambda j: (0, j))], out_specs=pl.BlockSpec((tc, r), lambda j: (j, 0)),
        out_shape=jax.ShapeDtypeStruct((c, r), dtype), compiler_params=_cparams(("parallel",)),
    )(x)


def _chip_exchange(buf, rows, name):
    n = buf.shape[1]
    per = rows // ICI_STREAMS
    assert per * ICI_STREAMS == rows and per % 16 == 0

    def body(x_ref, out_ref, send_sems, recv_sems):
        x, y, c = _pos()
        copies = []
        for k in range(1, 4):
            px, py = _flip(x, k & 2), _flip(y, k & 1)
            for j in range(ICI_STREAMS):
                i = (k - 1) * ICI_STREAMS + j
                cp = pltpu.make_async_remote_copy(
                    src_ref=x_ref.at[pl.ds(pl.multiple_of((2 * px + py) * rows + j * per, 16), per), :],
                    dst_ref=out_ref.at[pl.ds((k - 1) * rows + j * per, per), :],
                    send_sem=send_sems.at[i], recv_sem=recv_sems.at[i], device_id=(px, py, c), device_id_type=MESH)
                cp.start()
                copies.append(cp)
        for cp in copies:
            cp.wait_recv()
        for cp in copies:
            cp.wait_send()

    return pl.pallas_call(
        body, name=name, out_shape=jax.ShapeDtypeStruct((3 * rows, n), buf.dtype),
        in_specs=[ANY], out_specs=ANY,
        scratch_shapes=[pltpu.SemaphoreType.DMA((3 * ICI_STREAMS,)), pltpu.SemaphoreType.DMA((3 * ICI_STREAMS,))],
        compiler_params=pltpu.CompilerParams(has_side_effects=True),
    )(buf)


def _add_rows(parts, rows, dtype, name):
    tile = 1024
    ins = [_In(a, roff=r0 // tile) for a, r0 in parts]

    def fn(i, j, *vals):
        acc = vals[0].astype(F32)
        for v_ in vals[1:]:
            acc = acc + v_.astype(F32)
        return (acc,)

    return _rowcall(name, fn, rows // tile, tile, ins, [_Out(parts[0][0].shape[1], dtype)])[0]


BIG = ("w_in", "w_branch_dn", "w_branch_attn", "w_out", "ffn_up", "ffn_down")
BIG_SHARD = {"w_in": (1024, 1928, True), "w_branch_dn": (256, 1024, False), "w_branch_attn": (256, 1024, False),
             "w_out": (256, 1024, False), "ffn_up": (1024, 1408, True), "ffn_down": (704, 1024, False)}
BIG_ROWS = {k: r * c // 2 // 128 for k, (r, c, _) in BIG_SHARD.items()}
PIECE = 19456
assert sum(BIG_ROWS.values()) <= PIECE


def _gather_weights(shards, ci):
    halves = []
    for k in BIG:
        r, c, _ = BIG_SHARD[k]
        halves.append(lax.dynamic_slice_in_dim(shards[k], ci * (r // 2), r // 2, axis=0).astype(BF16))
    out = {}
    for k, ag in zip(BIG, _all_gather_many(halves, "ag_weights")):
        r, c, by_col = BIG_SHARD[k]
        blk = ag.reshape(4, r, c)
        out[k] = jnp.transpose(blk, (1, 0, 2)).reshape(r, 4 * c) if by_col else blk.reshape(4 * r, c)
    return out


def _pack_pieces(full):
    parts = [full["w_in_t"].reshape(N_DEV, BIG_ROWS["w_in"], 128).astype(BF16)]
    for k in BIG[1:]:
        r, c, by_col = BIG_SHARD[k]
        a = full[k]
        if by_col:
            a = jnp.transpose(a.reshape(r, 4, c), (1, 0, 2))
        parts.append(a.reshape(N_DEV, BIG_ROWS[k], 128).astype(BF16))
    parts.append(jnp.zeros((N_DEV, PIECE - sum(BIG_ROWS.values()), 128), BF16))
    return jnp.concatenate(parts, axis=1).reshape(N_DEV * PIECE, 128)


def _reduce_scatter(pieces, ci, shard):
    half = N_DEV // 2 * PIECE
    theirs = _sibling_exchange(pieces, PIECE, N_DEV // 2, True, "rs_d2d")
    own = lax.dynamic_index_in_dim(pieces.reshape(N_DEV // 2, 2, PIECE, 128), ci, axis=1, keepdims=False).reshape(half, 128)
    part = _add_rows([(own, 0), (theirs, 0)], half, BF16, "rs_sum_chip")
    recv = _chip_exchange(part, PIECE, "rs_ici")
    own2 = lax.dynamic_slice_in_dim(part, shard * PIECE, PIECE, axis=0)
    mine = _add_rows([(own2, 0), (recv, 0), (recv, PIECE), (recv, 2 * PIECE)], PIECE, F32, "rs_sum_all")
    other = _sibling_exchange(mine, PIECE, 1, False, "rs_pair")
    return jnp.where(ci == 0, jnp.stack([mine, other]), jnp.stack([other, mine]))


def _unpack_shard(two):
    out, off = {}, 0
    for k in BIG:
        r, c, _ = BIG_SHARD[k]
        blk = two[:, off:off + BIG_ROWS[k]]
        out[k] = blk.reshape(c, r).T if k == "w_in" else blk.reshape(r, c)
        off += BIG_ROWS[k]
    return out


SMALL = (("dn_conv", 120), ("ffn_conv", 132), ("ffn_conv_b", 44), ("norm_mix", 8), ("norm_ffn", 8), ("dn_a_log", 1),
         ("dn_dt_bias", 1), ("dn_norm", 1), ("q_norm", 1), ("k_norm", 1), ("attn_sink", 1), ("dmod_c", 48), ("dmod_x", 48))
SMALL_ROWS = 416


def _rows128(a, rows):
    flat = a.reshape(-1)
    return jnp.concatenate([flat, jnp.zeros((rows * 128 - flat.shape[0],), F32)]).reshape(rows, 128)


def _pack_small(g):
    parts = [_rows128(g[k], r) for k, r in SMALL]
    parts.append(jnp.zeros((SMALL_ROWS - sum(r for _, r in SMALL), 128), F32))
    return jnp.concatenate(parts, axis=0)


def _unpack_small(buf, shapes):
    out, off = {}, 0
    for k, r in SMALL:
        n = math.prod(shapes[k])
        out[k] = buf[off:off + r].reshape(-1)[:n].reshape(shapes[k])
        off += r
    return out


WEIGHTS = ("c_ctx", "w_ada", "b_ada", "norm_mix", "norm_ffn", "w_in", "dn_conv", "dn_a_log", "dn_dt_bias", "dn_norm",
           "q_norm", "k_norm", "attn_sink", "w_branch_dn", "w_branch_attn", "w_out", "ffn_up", "ffn_conv", "ffn_conv_b",
           "ffn_down")


def kernel(x, c, ctx, c_ctx, w_ada, b_ada, norm_mix, norm_ffn, w_in, dn_conv, dn_a_log, dn_dt_bias, dn_norm, q_norm, k_norm, attn_sink, w_branch_dn, w_branch_attn, w_out, ffn_up, ffn_conv, ffn_conv_b, ffn_down, loss_target, m_c_ctx, m_w_ada, m_b_ada, m_norm_mix, m_norm_ffn, m_w_in, m_dn_conv, m_dn_a_log, m_dn_dt_bias, m_dn_norm, m_q_norm, m_k_norm, m_attn_sink, m_w_branch_dn, m_w_branch_attn, m_w_out, m_ffn_up, m_ffn_conv, m_ffn_conv_b, m_ffn_down, v_c_ctx, v_w_ada, v_b_ada, v_norm_mix, v_norm_ffn, v_w_in, v_dn_conv, v_dn_a_log, v_dn_dt_bias, v_dn_norm, v_q_norm, v_k_norm, v_attn_sink, v_w_branch_dn, v_w_branch_attn, v_w_out, v_ffn_up, v_ffn_conv, v_ffn_conv_b, v_ffn_down):
    args = dict(locals())
    xi, yi, ci = _pos()
    dev = 4 * xi + 2 * yi + ci
    shard = 2 * xi + yi
    chips = lambda a: a[0::2]

    blk = jnp.concatenate([_rows128(c, 8), _rows128(dn_conv, 30), _rows128(ffn_conv, 33), jnp.zeros((1, 128), F32)], axis=0)
    ag = _all_gather(blk, "ag_small_in").reshape(N_DEV, 72, 128)
    c_all = ag[:, 0:8].reshape(N_DEV, D)
    dn_conv_full = jnp.transpose(chips(ag)[:, 8:38].reshape(4, 5, 768), (1, 0, 2)).reshape(5, 3 * D)
    ffn_conv_full = jnp.transpose(chips(ag)[:, 38:71].reshape(4, 3, 1408), (1, 0, 2)).reshape(3, 2 * DFF)

    c16 = jnp.concatenate([c_all, c_ctx[None], jnp.zeros((7, D), F32)], axis=0)
    a16 = _rowcall("ada_silu", lambda i, j, v: (_silu(v),), 1, 16, [_In(c16)], [_Out(D)])[0]
    m_sh = _mm(a16, w_ada[0], tm=16, tn=512, tk=D, name="ada_fwd", hi=True)
    mod16 = chips(_all_gather(m_sh, "ag_mod").reshape(N_DEV, 16, 1536))
    mod16 = jnp.transpose(mod16, (1, 0, 2)).reshape(16, 6 * D) + b_ada
    mod_x = lax.dynamic_slice_in_dim(mod16, dev, 1, axis=0)
    mod_c = mod16[8:9]

    shards = {k: args[k][0] for k in BIG}
    wfull = _gather_weights(shards, ci)
    w = dict(wfull)
    w["w_in_p"] = _pad_w_in(wfull["w_in"])
    w.update(norm_mix=norm_mix, norm_ffn=norm_ffn, dn_conv=dn_conv_full, dn_a_log=dn_a_log[0], dn_dt_bias=dn_dt_bias[0],
             dn_norm=dn_norm, q_norm=q_norm, k_norm=k_norm, attn_sink=attn_sink, ffn_conv=ffn_conv_full, ffn_conv_b=ffn_conv_b)

    loss_part, grad_x, g, dmod_x, dmod_c = _local_step(x[0], ctx[0], loss_target[0], mod_x, mod_c, w)
    loss = lax.psum(loss_part[0, 0], ("x", "y", "c"))

    g["w_in_t"] = _unpad_w_in(_transpose_cast(g["w_in_p"], BF16, "w_in_grad_t"), axis=0)
    gshard = _unpack_shard(_reduce_scatter(_pack_pieces(g), ci, shard))

    g["dmod_c"], g["dmod_x"] = dmod_c, dmod_x
    ag_s = _all_gather(_pack_small(g), "ag_small_grads")
    shapes = {k: g[k].shape for k, _ in SMALL}
    gs = _unpack_small(_sum_slots(ag_s, N_DEV, SMALL_ROWS, SMALL_ROWS, "small_sum"), shapes)
    dx_all = ag_s.reshape(N_DEV, SMALL_ROWS, 128)[:, SMALL_ROWS - 50:SMALL_ROWS - 2].reshape(N_DEV, 6 * D)

    d16 = jnp.concatenate([dx_all, gs["dmod_c"], jnp.zeros((7, 6 * D), F32)], axis=0)
    d16_sh = lax.dynamic_slice_in_dim(d16, shard * 1536, 1536, axis=1)
    g_w_ada = _mm(a16, d16_sh, ta=True, tm=D, tn=512, tk=16, name="ada_dw", hi=True)
    g_b_ada = _rowcall("ada_db", lambda i, j, v: (_colsum(v),), 1, 16, [_In(d16)], [_Out(6 * D, acc=True)])[0]
    da_part = _mm(d16_sh, w_ada[0], tb=True, tm=16, tn=D, tk=512, name="ada_dx", hi=True)
    da_all = _all_gather(da_part, "ag_ada_dx")
    da16 = _sum_slots(da_all, 4, 16, 16, "ada_dx_sum", stride=2)
    dc16 = _rowcall("ada_dsilu", lambda i, j, d_, v: (d_ * _dsilu(v),), 1, 16, [_In(da16), _In(c16)], [_Out(D)])[0]

    grads = {
        "c_ctx": dc16[8], "w_ada": g_w_ada[None], "b_ada": g_b_ada, "norm_mix": gs["norm_mix"], "norm_ffn": gs["norm_ffn"],
        "w_in": gshard["w_in"][None],
        "dn_conv": lax.dynamic_slice_in_dim(gs["dn_conv"], shard * 768, 768, axis=1)[None],
        "dn_a_log": gs["dn_a_log"][None], "dn_dt_bias": gs["dn_dt_bias"][None], "dn_norm": gs["dn_norm"],
        "q_norm": gs["q_norm"], "k_norm": gs["k_norm"], "attn_sink": gs["attn_sink"],
        "w_branch_dn": gshard["w_branch_dn"][None], "w_branch_attn": gshard["w_branch_attn"][None],
        "w_out": gshard["w_out"][None], "ffn_up": gshard["ffn_up"][None],
        "ffn_conv": lax.dynamic_slice_in_dim(gs["ffn_conv"], shard * 1408, 1408, axis=1)[None],
        "ffn_conv_b": gs["ffn_conv_b"], "ffn_down": gshard["ffn_down"][None],
    }
    deltas, new_m, new_v = [], [], []
    for k in WEIGHTS:
        d_, m_, v_ = _adamw(args[k], grads[k], args["m_" + k], args["v_" + k], "adamw_" + k)
        deltas.append(d_)
        new_m.append(m_)
        new_v.append(v_)
    return (loss, grad_x[None], *[grads[k] for k in WEIGHTS], *deltas, *new_m, *new_v)
```

```python
import functools
import math

import numpy as np
import jax
import jax.numpy as jnp
from jax import lax
from jax.experimental import pallas as pl
from jax.experimental.pallas import tpu as pltpu

F32 = jnp.float32
BF16 = jnp.bfloat16
HI = lax.Precision.HIGHEST

D = 1024
NH = 8
HD = 128
CH = 64
CTX = 256
AB = 128
KVH = 2
GRP = 4
DFF = 2816
EPS = 1e-6
GRID_W = 64
ROPE_BASE = 10000.0
N_DEV = 8
VMEM_LIMIT = 56 * 1024 * 1024

C_QKV, C_KAT, C_VAT, C_BA, C_PAD, C_GT, C_QAT, C_MG = 0, 3072, 3328, 3584, 3712, 4096, 5120, 6144
PW = 8192
PH = PW // 2


def _cparams(sem=None, **kw):
    return pltpu.CompilerParams(dimension_semantics=sem, vmem_limit_bytes=VMEM_LIMIT, **kw)


def _dot(a, b, dims, hi):
    if hi:
        return lax.dot_general(a.astype(F32), b.astype(F32), (dims, ((), ())), precision=HI, preferred_element_type=F32)
    return lax.dot_general(a.astype(BF16), b.astype(BF16), (dims, ((), ())), preferred_element_type=F32)


NN = ((1,), (0,))
NT = ((1,), (1,))
TN = ((0,), (0,))


def _dn_masks():
    i = np.arange(CH)
    lo_incl = (i[:, None] >= i[None, :]).astype(np.float32)
    lo_strict = (i[:, None] > i[None, :]).astype(np.float32)
    return jnp.asarray(np.stack([np.stack([lo_incl, lo_strict]), np.stack([lo_incl.T, lo_strict.T])]))


def _dn_chunk_index(d, i, n_ctx_chunks, n_chunks):
    fwd = i
    bwd = jnp.where(i < n_ctx_chunks, n_ctx_chunks - 1 - i, n_chunks - 1 + n_ctx_chunks - i)
    return jnp.where(d == 0, fwd, bwd)


BNN = ((2,), (1,))
BNT = ((2,), (2,))
BTN = ((1,), (1,))


def _bdot(a, b, dims, hi):
    dn = (dims, ((0,), (0,)))
    if hi:
        return lax.dot_general(a.astype(F32), b.astype(F32), dn, precision=HI, preferred_element_type=F32)
    return lax.dot_general(a.astype(BF16), b.astype(BF16), dn, preferred_element_type=F32)


def _bdot3(a, b, dims, hi):
    if hi:
        return _bdot(a, b, dims, True)
    ah, bh = a.astype(BF16), b.astype(BF16)
    al, bl = (a - ah.astype(F32)).astype(BF16), (b - bh.astype(F32)).astype(BF16)
    dn = (dims, ((0,), (0,)))
    d = lambda x_, y_: lax.dot_general(x_, y_, dn, preferred_element_type=F32)
    return d(ah, bh) + d(ah, bl) + d(al, bh)


DN_CB = 4
DN_SEQ_CB = 4


def _dn_heads(ref, cb=1):
    return jnp.stack([ref[t * CH:(t + 1) * CH, h * HD:(h + 1) * HD] for t in range(cb) for h in range(NH)])


def _dn_scalars(gb, mi, cb=1):
    beta, gc, gcr, gt = [], [], [], []
    for t in range(cb):
        g1 = gb[t * CH:(t + 1) * CH]
        gcum, gcum_t, gtot = _dn_gcum(g1, mi)
        beta += [g1[:, h:h + 1] for h in range(NH)]
        gc += [gcum[:, NH + h:NH + h + 1] for h in range(NH)]
        gcr += [gcum_t[NH + h:NH + h + 1, :] for h in range(NH)]
        gt += [gtot[:, NH + h:NH + h + 1] for h in range(NH)]
    return jnp.stack(beta), jnp.stack(gc), jnp.stack(gcr), jnp.stack(gt)


DN_NEWTON = 1


def _dn_inverse(a, hi):
    eye = (lax.broadcasted_iota(jnp.int32, (CH, CH), 0) == lax.broadcasted_iota(jnp.int32, (CH, CH), 1)).astype(F32)
    x = -a
    t = eye + x
    p = x
    if hi:
        for _ in range(5):
            p = _bdot(p, p, BNN, True)
            t = t + _bdot(t, p, BNN, True)
        return t
    for _ in range(5):
        p = _bdot(p, p, BNN, False)
        t = t + _bdot(t, p, BNN, False)
    for _ in range(DN_NEWTON):
        r = eye - t - _bdot3(a, t, BNN, False)
        t = t + _bdot(t, r, BNN, False)
    return t


def _dn_total(gb):
    gtot = jnp.sum(gb, axis=0, keepdims=True)
    return jnp.stack([gtot[:, NH + h:NH + h + 1] for h in range(NH)])


def _dn_gcum(gb, mi):
    gcum = _dot(mi, gb, NN, True)
    gtot = jnp.sum(gb, axis=0, keepdims=True)
    return gcum, gcum.T, gtot


def _dn_specs(n_ctx_chunks, n_chunks, reverse, cb):
    assert n_ctx_chunks % cb == 0 and n_chunks % cb == 0

    def grp(d, i):
        first = n_chunks - 1 - cb * i if reverse else cb * i
        return _dn_chunk_index(d, first, n_ctx_chunks, n_chunks) // cb

    def slot(d, t):
        ascending = (d == 1) if reverse else (d == 0)
        return jnp.where(ascending, t, cb - 1 - t)

    ctx_groups = n_ctx_chunks // cb
    tok_lat = pl.BlockSpec((cb * CH, D), lambda d, i: (jnp.maximum(grp(d, i) - ctx_groups, 0), 0))
    is_ctx = lambda d, i: grp(d, i) < ctx_groups
    tok_d = pl.BlockSpec((1, cb * CH, D), lambda d, i: (d, grp(d, i), 0))
    gbs = pl.BlockSpec((1, cb * CH, 128), lambda d, i: (d, grp(d, i), 0))

    def per_chunk(*tail):
        return pl.BlockSpec((1, cb) + tail, lambda d, i: (d, grp(d, i)) + (0,) * len(tail))

    return tok_lat, is_ctx, tok_d, gbs, per_chunk, slot


def _dn_group_specs(cb):
    tok = pl.BlockSpec((cb * CH, D), lambda d, i: (i, 0))
    tok_d = pl.BlockSpec((1, cb * CH, D), lambda d, i: (d, i, 0))
    gbs = pl.BlockSpec((1, cb * CH, 128), lambda d, i: (d, i, 0))
    msk = pl.BlockSpec((1, 2, CH, CH), lambda d, i: (d, 0, 0, 0))

    def per_chunk(*tail):
        return pl.BlockSpec((1, cb) + tail, lambda d, i: (d, i) + (0,) * len(tail))

    return tok, tok_d, gbs, msk, per_chunk


def _dn_intra_fwd(q, k, v, gb, n_ctx_chunks, hi):
    t_all = q.shape[0]
    n_chunks = t_all // CH
    masks = _dn_masks()

    cb = DN_CB

    def put(ref, val):
        for t_ in range(cb):
            ref[0, t_] = val[t_ * NH:(t_ + 1) * NH].astype(ref.dtype)

    def body(q_ref, k_ref, v_ref, gb_ref, m_ref, u_ref, w_ref, qg_ref, kd_ref, pm_ref, t_ref):
        mi, ms = m_ref[0, 0], m_ref[0, 1]
        beta, gc, gcr, gt = _dn_scalars(gb_ref[0], mi, cb)
        q_, k_, v_ = _dn_heads(q_ref, cb), _dn_heads(k_ref, cb), _dn_heads(v_ref, cb)
        decay = jnp.exp(jnp.where(mi > 0, gc - gcr, 0.0)) * mi
        e = jnp.exp(gc)
        a = ms * (beta * _bdot(k_, k_, BNT, hi) * decay)
        t = _dn_inverse(a, hi)
        uw =_bdot(t, jnp.concatenate([beta * v_, (beta * e) * k_], axis=2), BNN, hi)
        put(u_ref, uw[:, :, :HD])
        put(w_ref, uw[:, :, HD:])
        put(qg_ref, e * q_)
        put(kd_ref, jnp.exp(gt - gc) * k_)
        put(pm_ref, _bdot(q_, k_, BNT, hi) * decay)
        put(t_ref, t)

    tok, _, gbs, msk, per_chunk = _dn_group_specs(cb)
    big = lambda dt: jax.ShapeDtypeStruct((2, n_chunks, NH, CH, HD), dt)
    sq = jax.ShapeDtypeStruct((2, n_chunks, NH, CH, CH), BF16)
    return pl.pallas_call(
        body, name="dn_intra_fwd", grid=(2, n_chunks // cb),
        in_specs=[tok, tok, tok, gbs, msk],
        out_specs=[per_chunk(NH, CH, HD)] * 4 + [per_chunk(NH, CH, CH)] * 2,
        out_shape=[big(BF16), big(BF16), big(BF16), big(BF16), sq, sq],
        compiler_params=_cparams(("parallel", "parallel")),
    )(q, k, v, gb, masks)


def _dn_seq_fwd(u, w, qg, kd, pm, gb, n_ctx_chunks, hi):
    n_chunks = u.shape[1]
    t_all = n_chunks * CH

    cb = DN_SEQ_CB
    _, _, tok_d, gbs, per_chunk, slot = _dn_specs(n_ctx_chunks, n_chunks, False, cb)

    def body(u_ref, w_ref, qg_ref, kd_ref, pm_ref, gb_ref, o_ref, sh_ref, vn_ref, s_scr):
        @pl.when(pl.program_id(1) == 0)
        def _():
            s_scr[...] = jnp.zeros_like(s_scr)

        for t in range(cb):
            j = slot(pl.program_id(0), t)
            rows = pl.ds(pl.multiple_of(j * CH, CH), CH)
            s = s_scr[...]
            sh_ref[0, j] = s.astype(sh_ref.dtype)
            vn = u_ref[0, j] - _bdot(w_ref[0, j], s, BNN, hi)
            o = _bdot(qg_ref[0, j], s, BNN, hi) + _bdot(pm_ref[0, j], vn, BNN, hi)
            s_scr[...] = jnp.exp(_dn_total(gb_ref[0, rows, :])) * s + _bdot(kd_ref[0, j], vn, BTN, hi)
            vn_ref[0, j] = vn.astype(vn_ref.dtype)
            for h in range(NH):
                o_ref[0, rows, h * HD:(h + 1) * HD] = o[h]

    big = per_chunk(NH, CH, HD)
    return pl.pallas_call(
        body, name="dn_seq_fwd", grid=(2, n_chunks // cb),
        in_specs=[big, big, big, big, per_chunk(NH, CH, CH), gbs],
        out_specs=[tok_d, per_chunk(NH, HD, HD), big],
        out_shape=[jax.ShapeDtypeStruct((2, t_all, D), F32), jax.ShapeDtypeStruct((2, n_chunks, NH, HD, HD), BF16),
                   jax.ShapeDtypeStruct((2, n_chunks, NH, CH, HD), BF16)],
        scratch_shapes=[pltpu.VMEM((NH, HD, HD), F32)],
        compiler_params=_cparams(("parallel", "arbitrary")),
    )(u, w, qg, kd, pm, gb)


def _dn_seq_bwd(w, qg, kd, pm, vn, s_hist, gb, do, n_ctx_chunks, hi):
    n_chunks = w.shape[1]

    cb = DN_SEQ_CB
    tok_lat, is_ctx, _, gbs, per_chunk, slot = _dn_specs(n_ctx_chunks, n_chunks, True, cb)

    def body(w_ref, qg_ref, kd_ref, pm_ref, vn_ref, sh_ref, gb_ref, do_ref, dvn_ref, dw_ref, dqg_ref, dkd_ref, del_ref, ds_scr):
        @pl.when(pl.program_id(1) == 0)
        def _():
            ds_scr[...] = jnp.zeros_like(ds_scr)

        for t in range(cb):
            j = slot(pl.program_id(0), t)
            rows = pl.ds(pl.multiple_of(j * CH, CH), CH)
            dsn = ds_scr[...]
            s = sh_ref[0, j]
            do_ = jnp.stack([do_ref[rows, h * HD:(h + 1) * HD] for h in range(NH)])
            do_ = jnp.where(is_ctx(pl.program_id(0), pl.program_id(1)), 0.0, do_)
            dvn =_bdot(pm_ref[0, j], do_, BTN, hi) + _bdot(kd_ref[0, j], dsn, BNN, hi)
            ds_scr[...] = (_bdot(qg_ref[0, j], do_, BTN, hi) + jnp.exp(_dn_total(gb_ref[0, rows, :])) * dsn
                           - _bdot(w_ref[0, j], dvn, BTN, hi))
            dvn_ref[0, j] = dvn.astype(dvn_ref.dtype)
            dw_ref[0, j] = (-_bdot(dvn, s, BNT, hi)).astype(dw_ref.dtype)
            dqg_ref[0, j] = _bdot(do_, s, BNT, hi).astype(dqg_ref.dtype)
            dkd_ref[0, j] = _bdot(vn_ref[0, j], dsn, BNT, hi).astype(dkd_ref.dtype)
            del_ref[0, j] = jnp.broadcast_to(jnp.sum(jnp.sum(s * dsn, axis=2, keepdims=True), axis=1, keepdims=True),
                                             (NH, 1, 128))

    big = per_chunk(NH, CH, HD)
    shp = lambda dt: jax.ShapeDtypeStruct((2, n_chunks, NH, CH, HD), dt)
    return pl.pallas_call(
        body, name="dn_seq_bwd", grid=(2, n_chunks // cb),
        in_specs=[big, big, big, per_chunk(NH, CH, CH), big, per_chunk(NH, HD, HD), gbs, tok_lat],
        out_specs=[big, big, big, big, per_chunk(NH, 1, 128)],
        out_shape=[shp(BF16), shp(BF16), shp(BF16), shp(BF16), jax.ShapeDtypeStruct((2, n_chunks, NH, 1, 128), F32)],
        scratch_shapes=[pltpu.VMEM((NH, HD, HD), F32)],
        compiler_params=_cparams(("parallel", "arbitrary")),
    )(w, qg, kd, pm, vn, s_hist, gb, do)


def _dn_intra_bwd(q, k, v, gb, u, w, t, vn, dvn, dw, dqg, dkd, de_last, do, n_ctx_chunks, hi):
    t_all = q.shape[0]
    n_chunks = t_all // CH
    masks = _dn_masks()

    cb = DN_CB
    assert n_ctx_chunks % cb == 0
    ctx_groups = n_ctx_chunks // cb

    def body(q_ref, k_ref, v_ref, gb_ref, m_ref, u_ref, w_ref, t_ref, vn_ref, dvn_ref, dw_ref, dqg_ref, dkd_ref, del_ref,
             do_ref, dq_ref, dk_ref, dv_ref, dgb_ref):
        mi, ms = m_ref[0, 0], m_ref[0, 1]
        beta, gc, gcr, gt = _dn_scalars(gb_ref[0], mi, cb)
        q_, k_, v_ = _dn_heads(q_ref, cb), _dn_heads(k_ref, cb), _dn_heads(v_ref, cb)
        do_ = jnp.where(pl.program_id(1) < ctx_groups, 0.0, _dn_heads(do_ref, cb))
        get = lambda ref: jnp.concatenate([ref[0, t_] for t_ in range(cb)], axis=0)
        decay = jnp.exp(jnp.where(mi > 0, gc - gcr, 0.0)) * mi
        e = jnp.exp(gc)
        e_last = jnp.exp(gt)
        kdfac = jnp.exp(gt - gc)
        kk = _bdot(k_, k_, BNT, hi)
        a = ms * (beta * kk * decay)
        pm = _bdot(q_, k_, BNT, hi) * decay
        kd = kdfac * k_
        dqg, dkd = get(dqg_ref), get(dkd_ref)
        dpm = _bdot(do_, get(vn_ref), BNT, hi)
        dvbkb = _bdot(get(t_ref), jnp.concatenate([get(dvn_ref), get(dw_ref)], axis=2), BTN, hi)
        dvb, dkb = dvbkb[:, :, :HD], dvbkb[:, :, HD:]
        da = -ms * _bdot(dvbkb, jnp.concatenate([get(u_ref), get(w_ref).astype(F32)], axis=2), BNT, hi)
        dqk = dpm * decay
        gm = dpm * pm + da * a
        dgc = (jnp.sum(gm, axis=2, keepdims=True)
               - _bdot3(gm, jnp.ones((cb * NH, CH, 128), F32), BTN, hi)[:, :, 0:1])
        dkk = da * (beta * decay)
        dbeta = jnp.sum(da * kk * decay, axis=2, keepdims=True)
        dk = _bdot(dkk, k_, BNN, hi) + _bdot(dkk, k_, BTN, hi) + _bdot(dqk, q_, BTN, hi)
        dq = _bdot(dqk, k_, BNN, hi) + e * dqg
        de = jnp.sum(dqg * q_, axis=2, keepdims=True)
        dv = beta * dvb
        dbeta = dbeta + jnp.sum(dvb * v_, axis=2, keepdims=True)
        skb = jnp.sum(dkb * k_, axis=2, keepdims=True)
        dk = dk + (beta * e) * dkb + kdfac * dkd
        dbeta = dbeta + e * skb
        de = de + beta * skb
        skd = jnp.sum(dkd * kd, axis=2, keepdims=True)
        dgc = dgc - skd + de * e
        dgtot = jnp.sum(skd, axis=1, keepdims=True) + get(del_ref)[:, :, 0:1] * e_last
        lane = lax.broadcasted_iota(jnp.int32, (1, 128), 1)
        for t_ in range(cb):
            rows = slice(t_ * CH, (t_ + 1) * CH)
            dbeta_all = jnp.zeros((CH, 128), F32)
            dgc_all = jnp.zeros((CH, 128), F32)
            dgtot_all = jnp.zeros((1, 128), F32)
            for h in range(NH):
                sl = slice(h * HD, (h + 1) * HD)
                b = t_ * NH + h
                dq_ref[0, rows, sl] = dq[b]
                dk_ref[0, rows, sl] = dk[b]
                dv_ref[0, rows, sl] = dv[b]
                hot_b = (lane == h).astype(F32)
                hot_g = (lane == NH + h).astype(F32)
                dbeta_all = dbeta_all + dbeta[b] * hot_b
                dgc_all = dgc_all + dgc[b] * hot_g
                dgtot_all = dgtot_all + dgtot[b] * hot_g
            dgb_ref[0, rows, :] = dbeta_all + _dot(mi, dgc_all, TN, True) + dgtot_all

    tok, tok_d, gbs, msk, per_chunk = _dn_group_specs(cb)
    tok_lat = pl.BlockSpec((cb * CH, D), lambda d, i: (jnp.maximum(i - ctx_groups, 0), 0))
    big = per_chunk(NH, CH, HD)
    return pl.pallas_call(
        body, name="dn_intra_bwd", grid=(2, n_chunks // cb),
        in_specs=[tok, tok, tok, gbs, msk, big, big, per_chunk(NH, CH, CH), big, big, big, big, big,
                  per_chunk(NH, 1, 128), tok_lat],
        out_specs=[tok_d, tok_d, tok_d, gbs],
        out_shape=[jax.ShapeDtypeStruct((2, t_all, D), F32)] * 3 + [jax.ShapeDtypeStruct((2, t_all, 128), F32)],
        compiler_params=_cparams(("parallel", "parallel")),
    )(q, k, v, gb, masks, u, w, t, vn, dvn, dw, dqg, dkd, de_last, do)


ATT_SCALE = HD ** -0.5
NEG = -1e30


def _att_stack(ref, kvh):
    return jnp.concatenate([ref[:, (kvh * GRP + g) * HD:(kvh * GRP + g + 1) * HD] for g in range(GRP)], axis=0)


def _att_col(ref, kvh):
    return jnp.concatenate([ref[:, kvh * GRP + g:kvh * GRP + g + 1] for g in range(GRP)], axis=0)


def _att_sink(sink_ref, kvh):
    return jnp.concatenate([jnp.broadcast_to(sink_ref[:, kvh * GRP + g:kvh * GRP + g + 1], (AB, 1)) for g in range(GRP)],
                           axis=0)


def _att_mask(i, nb):
    r = lax.broadcasted_iota(jnp.int32, (AB, AB), 0)
    c = lax.broadcasted_iota(jnp.int32, (AB, AB), 1)
    okp = jnp.logical_and(c >= r, i > 0)
    okn = jnp.logical_and(c <= r, i < nb - 1)
    return jnp.concatenate([okp] * GRP, axis=0), jnp.concatenate([okn] * GRP, axis=0)


def _att_masked(s, mask):
    mp, mn = mask
    return jnp.concatenate([jnp.where(mp, s[:, 0:AB], NEG), s[:, AB:2 * AB], jnp.where(mn, s[:, 2 * AB:3 * AB], NEG),
                            s[:, 3 * AB:]], axis=1)


def _att_kspecs(nb):
    nc = CTX // AB
    return [pl.BlockSpec((AB, KVH * HD), lambda i: (jnp.maximum(i - 1, 0) + nc, 0)),
            pl.BlockSpec((AB, KVH * HD), lambda i: (i + nc, 0)),
            pl.BlockSpec((AB, KVH * HD), lambda i: (jnp.minimum(i + 1, nb - 1) + nc, 0)),
            pl.BlockSpec((CTX, KVH * HD), lambda i: (0, 0))]


def _attn_fwd(qr, kr, vv, sink, hi):
    tl = qr.shape[0]
    nb = tl // AB

    def body(q_ref, kp_ref, kc_ref, kn_ref, kx_ref, vp_ref, vc_ref, vn_ref, vx_ref, sink_ref, o_ref, lse_ref):
        i = pl.program_id(0)
        mask = _att_mask(i, nb)
        lane = lax.broadcasted_iota(jnp.int32, (1, 128), 1)
        lse_all = jnp.zeros((AB, 128), F32)
        for kvh in range(KVH):
            ksl = slice(kvh * HD, (kvh + 1) * HD)
            kall = jnp.concatenate([kp_ref[:, ksl], kc_ref[:, ksl], kn_ref[:, ksl], kx_ref[:, ksl]], axis=0)
            vall = jnp.concatenate([vp_ref[:, ksl], vc_ref[:, ksl], vn_ref[:, ksl], vx_ref[:, ksl]], axis=0)
            s = _dot(_att_stack(q_ref, kvh), kall, NT, hi) * ATT_SCALE
            s = _att_masked(s, mask)
            sk = _att_sink(sink_ref, kvh)
            m = jnp.maximum(jnp.max(s, axis=1, keepdims=True), sk)
            p = jnp.exp(s - m)
            l = jnp.sum(p, axis=1, keepdims=True) + jnp.exp(sk - m)
            o = _dot(p, vall, NN, hi) / l
            lse = m + jnp.log(l)
            for g in range(GRP):
                h = kvh * GRP + g
                o_ref[:, h * HD:(h + 1) * HD] = o[g * AB:(g + 1) * AB]
                lse_all = lse_all + lse[g * AB:(g + 1) * AB] * (lane == h).astype(F32)
        lse_ref[...] = lse_all

    ks = _att_kspecs(nb)
    return pl.pallas_call(
        body, name="attn_fwd", grid=(nb,),
        in_specs=[pl.BlockSpec((AB, D), lambda i: (i, 0))] + ks + ks + [pl.BlockSpec((1, 128), lambda i: (0, 0))],
        out_specs=[pl.BlockSpec((AB, D), lambda i: (i, 0)), pl.BlockSpec((AB, 128), lambda i: (i, 0))],
        out_shape=[jax.ShapeDtypeStruct((tl, D), F32), jax.ShapeDtypeStruct((tl, 128), F32)],
        compiler_params=_cparams(("parallel",)),
    )(qr, kr, kr, kr, kr, vv, vv, vv, vv, sink)


def _mm_bat_dx_delta(dz_at, w_bat, o, hi):
    def fn(i, do_, o_):
        lane = lax.broadcasted_iota(jnp.int32, (1, 128), 1)
        acc = jnp.zeros((do_.shape[0], 128), F32)
        for h in range(NH):
            sl = slice(h * HD, (h + 1) * HD)
            acc = acc + jnp.sum(o_[:, sl] * do_[:, sl], axis=1, keepdims=True) * (lane == h).astype(F32)
        return do_, acc

    return _mm_ep("mm_bat_dx_delta", dz_at, w_bat, True, min(512, o.shape[0]), D, fn, [_In(o)], [_Out(D), _Out(128)], hi)


def _attn_bwd(qr, kr, vv, sink, do, lse, delta, hi):
    tl = qr.shape[0]
    nb = tl // AB
    nc = CTX // AB

    def body(q_ref, kp_ref, kc_ref, kn_ref, kx_ref, vp_ref, vc_ref, vn_ref, vx_ref, sink_ref, do_ref, lse_ref, dl_ref,
             dq_ref, dk_ref, dv_ref, dkx_ref, dvx_ref, dsink_ref, dk_acc, dv_acc):
        i = pl.program_id(0)

        @pl.when(i == 0)
        def _():
            dkx_ref[...] = jnp.zeros_like(dkx_ref)
            dvx_ref[...] = jnp.zeros_like(dvx_ref)
            dsink_ref[...] = jnp.zeros_like(dsink_ref)
            dk_acc[...] = jnp.zeros_like(dk_acc)
            dv_acc[...] = jnp.zeros_like(dv_acc)

        @pl.when(i < nb)
        def _():
            mask = _att_mask(i, nb)
            lane = lax.broadcasted_iota(jnp.int32, (1, 128), 1)
            s_prev, s_cur, s_next = (i + 2) % 3, i % 3, (i + 1) % 3
            dsink = jnp.zeros((1, 128), F32)
            for kvh in range(KVH):
                ksl = slice(kvh * HD, (kvh + 1) * HD)
                kall = jnp.concatenate([kp_ref[:, ksl], kc_ref[:, ksl], kn_ref[:, ksl], kx_ref[:, ksl]], axis=0)
                vall = jnp.concatenate([vp_ref[:, ksl], vc_ref[:, ksl], vn_ref[:, ksl], vx_ref[:, ksl]], axis=0)
                qs = _att_stack(q_ref, kvh)
                dos = _att_stack(do_ref, kvh)
                lse_s = _att_col(lse_ref, kvh)
                dl_s = _att_col(dl_ref, kvh)
                s = _dot(qs, kall, NT, hi) * ATT_SCALE
                p = jnp.exp(_att_masked(s, mask) - lse_s)
                dp = _dot(dos, vall, NT, hi)
                ds = p * (dp - dl_s)
                dq = _dot(ds, kall, NN, hi) * ATT_SCALE
                dk_all = _dot(ds, qs, TN, hi) * ATT_SCALE
                dv_all = _dot(p, dos, TN, hi)
                dkx_ref[:, ksl] += dk_all[3 * AB:]
                dvx_ref[:, ksl] += dv_all[3 * AB:]
                dk_acc[s_prev, :, ksl] += dk_all[0:AB]
                dv_acc[s_prev, :, ksl] += dv_all[0:AB]
                dk_acc[s_cur, :, ksl] += dk_all[AB:2 * AB]
                dv_acc[s_cur, :, ksl] += dv_all[AB:2 * AB]
                dk_acc[s_next, :, ksl] = dk_all[2 * AB:3 * AB]
                dv_acc[s_next, :, ksl] = dv_all[2 * AB:3 * AB]
                psink = jnp.exp(_att_sink(sink_ref, kvh) - lse_s) * dl_s
                for g in range(GRP):
                    h = kvh * GRP + g
                    dq_ref[:, h * HD:(h + 1) * HD] = dq[g * AB:(g + 1) * AB]
                    dsink = dsink - jnp.sum(psink[g * AB:(g + 1) * AB], axis=0, keepdims=True) * (lane == h).astype(F32)
            dsink_ref[...] += dsink

        @pl.when(i >= 1)
        def _():
            dk_ref[...] = dk_acc[(i + 2) % 3]
            dv_ref[...] = dv_acc[(i + 2) % 3]

    blk = lambda i: jnp.minimum(i, nb - 1)
    row = pl.BlockSpec((AB, D), lambda i: (blk(i), 0))
    col = pl.BlockSpec((AB, 128), lambda i: (blk(i), 0))
    ks = [pl.BlockSpec((AB, KVH * HD), lambda i: (jnp.maximum(blk(i) - 1, 0) + nc, 0)),
          pl.BlockSpec((AB, KVH * HD), lambda i: (blk(i) + nc, 0)),
          pl.BlockSpec((AB, KVH * HD), lambda i: (jnp.minimum(i + 1, nb - 1) + nc, 0)),
          pl.BlockSpec((CTX, KVH * HD), lambda i: (0, 0))]
    kv_out = pl.BlockSpec((AB, KVH * HD), lambda i: (jnp.maximum(i - 1, 0), 0))
    ctx_out = pl.BlockSpec((CTX, KVH * HD), lambda i: (0, 0))
    return pl.pallas_call(
        body, name="attn_bwd", grid=(nb + 1,),
        in_specs=[row] + ks + ks + [pl.BlockSpec((1, 128), lambda i: (0, 0)), row, col, col],
        out_specs=[row, kv_out, kv_out, ctx_out, ctx_out, pl.BlockSpec((1, 128), lambda i: (0, 0))],
        out_shape=[jax.ShapeDtypeStruct((tl, D), F32), jax.ShapeDtypeStruct((tl, KVH * HD), F32),
                   jax.ShapeDtypeStruct((tl, KVH * HD), F32), jax.ShapeDtypeStruct((CTX, KVH * HD), F32),
                   jax.ShapeDtypeStruct((CTX, KVH * HD), F32), jax.ShapeDtypeStruct((1, 128), F32)],
        scratch_shapes=[pltpu.VMEM((3, AB, KVH * HD), F32), pltpu.VMEM((3, AB, KVH * HD), F32)],
        compiler_params=_cparams(("arbitrary",)),
    )(qr, kr, kr, kr, kr, vv, vv, vv, vv, sink, do, lse, delta)


def _mm(a, b, ta=False, tb=False, out_dtype=F32, tm=512, tn=1024, tk=1024, name="mm", hi=False):
    a_parts = a.shape[0] if a.ndim == 3 else 0
    b_parts = b.shape[0] if b.ndim == 3 else 0
    assert not (a_parts and ta) and not (b_parts and tb)
    if a_parts:
        m, kd = a.shape[1], a_parts * a.shape[2]
    else:
        m, kd = (a.shape[1], a.shape[0]) if ta else a.shape
    n = b_parts * b.shape[2] if b_parts else (b.shape[0] if tb else b.shape[1])
    tm, tn, tk = min(tm, m), min(tn, n), min(tk, kd)
    assert m % tm == 0 and n % tn == 0 and kd % tk == 0, (name, m, n, kd, tm, tn, tk)
    nk = kd // tk
    dims = ((0,) if ta else (1,), (1,) if tb else (0,))

    def body(a_ref, b_ref, o_ref, *scr):
        part = _dot(a_ref[0] if a_parts else a_ref[...], b_ref[0] if b_parts else b_ref[...], dims, hi)
        if nk == 1:
            o_ref[...] = part.astype(out_dtype)
        else:
            acc = scr[0]
            kk = pl.program_id(2)

            @pl.when(kk == 0)
            def _():
                acc[...] = part

            @pl.when(kk > 0)
            def _():
                acc[...] += part

            @pl.when(kk == nk - 1)
            def _():
                o_ref[...] = acc[...].astype(out_dtype)

    a_spec = pl.BlockSpec((tk, tm), lambda i, j, k: (k, i)) if ta else pl.BlockSpec((tm, tk), lambda i, j, k: (i, k))
    b_spec = pl.BlockSpec((tn, tk), lambda i, j, k: (j, k)) if tb else pl.BlockSpec((tk, tn), lambda i, j, k: (k, j))
    if a_parts:
        per = a.shape[2] // tk
        assert per * tk == a.shape[2]
        a_spec = pl.BlockSpec((1, tm, tk), lambda i, j, k: (k // per, i, k % per))
    if b_parts:
        per_n = b.shape[2] // tn
        assert per_n * tn == b.shape[2]
        b_spec = pl.BlockSpec((1, tk, tn), lambda i, j, k: (j // per_n, k, j % per_n))
    return pl.pallas_call(
        body, name=name, grid=(m // tm, n // tn, nk),
        in_specs=[a_spec, b_spec],
        out_specs=pl.BlockSpec((tm, tn), lambda i, j, k: (i, j)),
        out_shape=jax.ShapeDtypeStruct((m, n), out_dtype),
        scratch_shapes=[] if nk == 1 else [pltpu.VMEM((tm, tn), F32)],
        compiler_params=_cparams(("parallel", "parallel", "arbitrary")),
    )(a, b)


HALO = 8


class _In:
    def __init__(self, arr, w=None, cb=0, roff=0, halo=None, ridx=None):
        self.arr, self.w, self.cb, self.roff, self.halo = arr, w or arr.shape[1], cb, roff, halo
        self.ridx = ridx or (lambda i, roff=roff: i + roff)


class _Full:
    def __init__(self, arr, w=None, cb=0):
        self.arr, self.w, self.cb = arr, w, cb


class _Out:
    def __init__(self, cols, dtype=F32, w=None, cb=0, acc=False, rows=1, roff=0, nrows=None, stack=0, into=None):
        self.cols, self.dtype, self.w, self.cb, self.acc, self.rows, self.roff, self.nrows, self.stack = (
            cols, dtype, w or cols, cb, acc, rows, roff, nrows, stack)
        self.into = into


def _alias_outs(arrays, specs, outs):
    aliases = {}
    for k, o in enumerate(outs):
        if o.into is not None:
            aliases[len(arrays)] = k
            arrays.append(o.into)
            specs.append(pl.BlockSpec(memory_space=pl.ANY))
    return aliases


def _rowcall(name, fn, nrow_tiles, tile, ins, outs, ncol=1):
    arrays, specs, kinds = [], [], []
    for x in ins:
        if isinstance(x, _Full):
            arrays.append(x.arr)
            if x.w is None:
                specs.append(pl.BlockSpec(x.arr.shape, lambda j, i: (0, 0)))
            else:
                specs.append(pl.BlockSpec((x.arr.shape[0], x.w), lambda j, i, cb=x.cb: (0, cb + j)))
            kinds.append("full")
            continue
        w, cb, roff = x.w, x.cb, x.roff
        cur = pl.BlockSpec((tile, w), lambda j, i, cb=cb, ridx=x.ridx: (ridx(i), cb + j))
        if x.halo is None:
            arrays.append(x.arr)
            specs.append(cur)
            kinds.append("tile")
        else:
            r8 = tile // HALO
            last = x.arr.shape[0] // HALO - 1
            prev = pl.BlockSpec((HALO, w), lambda j, i, cb=cb, roff=roff, r8=r8: (jnp.maximum((i + roff) * r8 - 1, 0), cb + j))
            nxt = pl.BlockSpec((HALO, w), lambda j, i, cb=cb, roff=roff, r8=r8, last=last:
                               (jnp.minimum((i + roff + 1) * r8, last), cb + j))
            arrays += [x.arr, x.arr, x.arr]
            specs += [prev, cur, nxt]
            kinds.append(("halo", x.halo))
    out_specs, out_shapes = [], []
    for o in outs:
        if o.acc:
            out_specs.append(pl.BlockSpec((o.rows, o.w), lambda j, i, cb=o.cb: (0, cb + j)))
            out_shapes.append(jax.ShapeDtypeStruct((o.rows, o.cols), o.dtype))
        elif o.stack:
            out_specs.append(pl.BlockSpec((o.stack, tile, o.w), lambda j, i, cb=o.cb: (0, i, cb + j)))
            out_shapes.append(jax.ShapeDtypeStruct((o.stack, nrow_tiles * tile, o.cols), o.dtype))
        else:
            out_specs.append(pl.BlockSpec((tile, o.w), lambda j, i, cb=o.cb, roff=o.roff: (i + roff, cb + j)))
            out_shapes.append(jax.ShapeDtypeStruct(((o.nrows or nrow_tiles * tile), o.cols), o.dtype))
    aliases = _alias_outs(arrays, specs, outs)
    n_in = len(arrays)

    def body(*refs):
        j = pl.program_id(0)
        i = pl.program_id(1)
        vals, r = [], 0
        for kind in kinds:
            if kind in ("full", "tile"):
                vals.append(refs[r][...])
                r += 1
            else:
                pok, nok = kind[1]
                p, c, n = refs[r][...], refs[r + 1][...], refs[r + 2][...]
                p = jnp.where(pok(i), p, jnp.zeros_like(p))
                n = jnp.where(nok(i), n, jnp.zeros_like(n))
                vals.append(jnp.concatenate([p, c, n], axis=0))
                r += 3
        res = fn(i, j, *vals)
        for o, ref, val in zip(outs, refs[n_in:], res):
            if o.acc:
                @pl.when(i == 0)
                def _(ref=ref, val=val, o=o):
                    ref[...] = val.astype(o.dtype)

                @pl.when(i > 0)
                def _(ref=ref, val=val, o=o):
                    ref[...] += val.astype(o.dtype)
            elif o.stack:
                for s_ in range(o.stack):
                    ref[s_] = val[s_].astype(o.dtype)
            else:
                ref[...] = val.astype(o.dtype)

    return pl.pallas_call(
        body, name=name, grid=(ncol, nrow_tiles), in_specs=specs, out_specs=out_specs, out_shape=out_shapes,
        input_output_aliases=aliases, compiler_params=_cparams(("parallel", "arbitrary")),
    )(*arrays)


def _mm_ep(name, a, b, tb, tm, tk, fn, ins, outs, hi=False):
    a_parts = a.shape[0] if a.ndim == 3 else 0
    m, kd = (a.shape[1], a_parts * a.shape[2]) if a_parts else a.shape
    n = b.shape[0] if tb else b.shape[1]
    tk = min(tk, kd)
    assert m % tm == 0 and kd % tk == 0, (name, m, kd, tm, tk)
    nk = kd // tk
    dims = ((1,), (1,) if tb else (0,))
    if a_parts:
        per = a.shape[2] // tk
        arrays, specs = [a], [pl.BlockSpec((1, tm, tk), lambda i, k: (k // per, i, k % per))]
    else:
        arrays, specs = [a], [pl.BlockSpec((tm, tk), lambda i, k: (i, k))]
    arrays.append(b)
    specs.append(pl.BlockSpec((n, tk), lambda i, k: (0, k)) if tb else pl.BlockSpec((tk, n), lambda i, k: (k, 0)))
    for x in ins:
        arrays.append(x.arr)
        if isinstance(x, _Full):
            specs.append(pl.BlockSpec(x.arr.shape, lambda i, k: (0, 0)))
        else:
            specs.append(pl.BlockSpec((tm, x.w), lambda i, k, cb=x.cb, ridx=x.ridx: (ridx(i), cb)))
    out_specs, out_shapes = [], []
    for o in outs:
        if o.acc:
            out_specs.append(pl.BlockSpec((o.rows, o.w), lambda i, k, cb=o.cb: (0, cb)))
            out_shapes.append(jax.ShapeDtypeStruct((o.rows, o.cols), o.dtype))
        else:
            out_specs.append(pl.BlockSpec((tm, o.w), lambda i, k, cb=o.cb, roff=o.roff: (i + roff, cb)))
            out_shapes.append(jax.ShapeDtypeStruct((o.nrows or m, o.cols), o.dtype))
    n_vals = len(arrays)
    aliases = _alias_outs(arrays, specs, outs)
    n_in = len(arrays)

    def body(*refs):
        i, kk = pl.program_id(0), pl.program_id(1)
        a_ref, b_ref = refs[0], refs[1]
        acc_ref = refs[-1]
        part = _dot(a_ref[0] if a_parts else a_ref[...], b_ref[...], dims, hi)

        @pl.when(kk == 0)
        def _():
            acc_ref[...] = part

        @pl.when(kk > 0)
        def _():
            acc_ref[...] += part

        @pl.when(kk == nk - 1)
        def _():
            res = fn(i, acc_ref[...], *[r[...] for r in refs[2:n_vals]])
            for o, ref, val in zip(outs, refs[n_in:-1], res):
                if o.acc:
                    @pl.when(i == 0)
                    def _(ref=ref, val=val, o=o):
                        ref[...] = val.astype(o.dtype)

                    @pl.when(i > 0)
                    def _(ref=ref, val=val, o=o):
                        ref[...] += val.astype(o.dtype)
                else:
                    ref[...] = val.astype(o.dtype)

    return pl.pallas_call(
        body, name=name, grid=(m // tm, nk), in_specs=specs, out_specs=out_specs, out_shape=out_shapes,
        scratch_shapes=[pltpu.VMEM((tm, n), F32)], input_output_aliases=aliases,
        compiler_params=_cparams(("arbitrary", "arbitrary")),
    )(*arrays)


def _shift(xe, s, tile):
    if s == 0:
        return xe[HALO:HALO + tile]
    return pltpu.roll(xe, (-s) % xe.shape[0], 0)[HALO:HALO + tile]


def _silu(x):
    return x * jax.nn.sigmoid(x)


def _dsilu(x):
    s = jax.nn.sigmoid(x)
    return s * (1.0 + x * (1.0 - s))


def _heads(x, fn):
    return jnp.concatenate([fn(h, x[:, h * HD:(h + 1) * HD]) for h in range(x.shape[1] // HD)], axis=1)


def _colsum(x):
    return jnp.sum(x, axis=0, keepdims=True)


def _rowmean(x):
    return jnp.mean(x, axis=1, keepdims=True)


def _rowsum(x):
    return jnp.sum(x, axis=1, keepdims=True)


TILE = 256
CT = CTX // TILE


def _all_halo(n_tiles):
    return (lambda i: i >= CT + 1, lambda i: jnp.logical_and(i >= CT, i < n_tiles - 1))


def _lat_halo(n_tiles):
    return (lambda i: i >= 1, lambda i: i < n_tiles - 1)


def _rms_mod(x, nm, shift, scale):
    r = lax.rsqrt(_rowmean(x * x) + EPS)
    return (x * r * nm) * (1.0 + scale) + shift


def _rms_mod_bwd(dh, x, nm, scale):
    r = lax.rsqrt(_rowmean(x * x) + EPS)
    xn = x * r
    dz = dh * (1.0 + scale)
    dxn = dz * nm
    dx = r * (dxn - xn * _rowmean(dxn * xn))
    return dx, _colsum(dz * xn), _colsum(dh), _colsum(dh * (xn * nm))


def _norm_mod(x, ctx, nm, mod_c, mod_x):
    n = (x.shape[0] + ctx.shape[0]) // TILE

    def fn(i, j, c_, x_, nm_, mc, mx):
        m = jnp.where(i < CT, mc, mx)
        return (_rms_mod(jnp.where(i < CT, c_, x_), nm_, m[0:1], m[1:2]),)

    ins = [_In(ctx, ridx=lambda i: jnp.minimum(i, CT - 1)), _In(x, ridx=lambda i: jnp.maximum(i - CT, 0)),
           _Full(nm), _Full(mod_c), _Full(mod_x)]
    return _rowcall("norm_mod", fn, n, TILE, ins, [_Out(D, BF16)])[0]


def _norm_mod_bwd(dh, xs, dres, nm, mod, roff, n):
    ins = [_In(dh, roff=roff), _In(xs), _Full(nm), _Full(mod)] + ([] if dres is None else [_In(dres)])

    def fn(i, j, dh_, x, nm_, m, *rest):
        dx, dn, dsh, dsc = _rms_mod_bwd(dh_, x, nm_, m[1:2])
        if rest:
            return (dx + rest[0], dn, dsh, dsc)
        return (dn, dsh, dsc)

    accs = [_Out(D, acc=True), _Out(D, acc=True), _Out(D, acc=True)]
    return _rowcall("norm_mod_bwd", fn, n, TILE, ins, ([] if dres is None else [_Out(D)]) + accs)


DN_Q_SCALE = HD ** -0.5


def _conv_taps(xe, w, width, rows=None):
    r = width // 2
    acc = None
    for t in range(width):
        s = t - r
        if rows is None:
            sh = xe if s == 0 else pltpu.roll(xe, (-s) % xe.shape[0], 0)
        else:
            sh = _shift(xe, s, rows)
        term = sh * w[t:t + 1]
        acc = term if acc is None else acc + term
    return acc


def _rolled(xe, width):
    r = width // 2
    return [xe if t == r else pltpu.roll(xe, (r - t) % xe.shape[0], 0) for t in range(width)]


def _conv_bwd(rolled, w, c_grad, width):
    r = width // 2
    cc = c_grad[HALO:HALO + TILE]
    dx, dws = None, []
    for t in range(width):
        term = _shift(c_grad, r - t, TILE) * w[t:t + 1]
        dx = term if dx is None else dx + term
        dws.append(_colsum(cc * rolled[t][HALO:HALO + TILE]))
    return dx, jnp.concatenate(dws + [jnp.zeros((8 - width, cc.shape[1]), F32)], axis=0)


def _silu_both(x):
    s = jax.nn.sigmoid(x)
    return x * s, s * (1.0 + x * (1.0 - s))


def _l2n(x, scale):
    rn = lax.rsqrt(_rowsum(x * x) + EPS)
    return x * (rn * scale)


def _l2n_bwd(dy, x, scale):
    rn = lax.rsqrt(_rowsum(x * x) + EPS)
    xu = x * rn
    return (scale * rn) * (dy - xu * _rowsum(dy * xu))


def _softplus(x):
    return jnp.maximum(x, 0.0) + jnp.log(1.0 + jnp.exp(-jnp.abs(x)))


def _lane_mask(lo, hi_):
    lane = lax.broadcasted_iota(jnp.int32, (1, 128), 1)
    return jnp.logical_and(lane >= lo, lane < hi_).astype(F32)


def _dn_prep(p, conv_w, gprm):
    n = p.shape[0] // TILE
    halo = _all_halo(n)

    def fn(i, j, qe, ke, ve, ba, w, gp):
        cq = _conv_taps(qe, w[:, 0:D], 5, TILE)
        ck = _conv_taps(ke, w[:, D:2 * D], 5, TILE)
        cv = _conv_taps(ve, w[:, 2 * D:3 * D], 5, TILE)
        q = _heads(_silu(cq), lambda h, x: _l2n(x, DN_Q_SCALE))
        k = _heads(_silu(ck), lambda h, x: _l2n(x, 1.0))
        v = _silu(cv)
        beta = jax.nn.sigmoid(ba)
        g = -jnp.exp(gp[0:1]) * _softplus(ba + gp[1:2])
        m0, m1 = _lane_mask(0, 8), _lane_mask(8, 16)
        gb_f = beta * m0 + pltpu.roll(g, 128 - 8, 1) * m1
        gb_b = pltpu.roll(beta, 128 - 8, 1) * m0 + pltpu.roll(g, 128 - 16, 1) * m1
        return q, k, v, gb_f, gb_b

    ins = [_In(p, D, 0, halo=halo), _In(p, D, 1, halo=halo), _In(p, D, 2, halo=halo), _In(p, 128, C_BA // 128),
           _Full(conv_w), _Full(gprm)]
    return _rowcall("dn_prep", fn, n, TILE, ins, [_Out(D), _Out(D), _Out(D), _Out(128), _Out(128)])


def _dn_prep_bwd(p, conv_w, gprm, dq2, dk2, dv2, dgb2, dk_at, dv_at, dp):
    n = p.shape[0] // TILE
    halo = _all_halo(n)

    def branch(xe, w, dye, scale):
        rolled = _rolled(xe, 5)
        c = rolled[0] * w[0:1]
        for t in range(1, 5):
            c = c + rolled[t] * w[t:t + 1]
        sx, dsilu = _silu_both(c)
        if scale is None:
            dsx = dye
        else:
            dsx = jnp.concatenate([_l2n_bwd(dye[:, h * HD:(h + 1) * HD], sx[:, h * HD:(h + 1) * HD], scale)
                                   for h in range(NH)], axis=1)
        return _conv_bwd(rolled, w, dsx * dsilu, 5)

    def fn(i, j, qe, ke, ve, ba, w, gp, dq0, dq1, dk0, dk1, dv0, dv1, dg0, dg1, dka, dva):
        dxq, dwq = branch(qe, w[:, 0:D], dq0 + dq1, DN_Q_SCALE)
        dxk, dwk = branch(ke, w[:, D:2 * D], dk0 + dk1, 1.0)
        dxv, dwv = branch(ve, w[:, 2 * D:3 * D], dv0 + dv1, None)
        m0, m1 = _lane_mask(0, 8), _lane_mask(8, 16)
        dbeta = dg0 * m0 + pltpu.roll(dg1 * m0, 8, 1)
        dg = pltpu.roll(dg0 * m1, 8, 1) + pltpu.roll(dg1 * m1, 16, 1)
        beta = jax.nn.sigmoid(ba)
        ea = jnp.exp(gp[0:1])
        z = ba + gp[1:2]
        g = -ea * _softplus(z)
        mg = _lane_mask(16, 32)
        da = dg * (-ea) * jax.nn.sigmoid(z) * mg
        dba = dbeta * beta * (1.0 - beta) * _lane_mask(0, 16) + da
        dgp = jnp.concatenate([_colsum(dg * g * mg), _colsum(da)], axis=0)
        half = jnp.concatenate([dxq, dxk, dxv, dka.astype(F32), dva.astype(F32), dba, jnp.zeros((TILE, PH - C_PAD), F32)],
                               axis=1)
        return (half, jnp.concatenate([dwq, dwk, dwv], axis=1), dgp)

    ins = [_In(p, D, 0, halo=halo), _In(p, D, 1, halo=halo), _In(p, D, 2, halo=halo), _In(p, 128, C_BA // 128),
           _Full(conv_w), _Full(gprm),
           _In(dq2, halo=halo), _In(dq2, roff=n, halo=halo), _In(dk2, halo=halo), _In(dk2, roff=n, halo=halo),
           _In(dv2, halo=halo), _In(dv2, roff=n, halo=halo), _In(dgb2), _In(dgb2, roff=n), _In(dk_at), _In(dv_at)]
    return _rowcall("dn_prep_bwd", fn, n, TILE, ins,
                    [_Out(PW, BF16, w=PH, cb=0, into=dp), _Out(3 * D, acc=True, rows=8), _Out(128, acc=True, rows=2)])


def _hnorm(x, w):
    return x * lax.rsqrt(_rowmean(x * x) + EPS) * w


def _hnorm_bwd(dy, x, w):
    r = lax.rsqrt(_rowmean(x * x) + EPS)
    xh = x * r
    dxh = dy * w
    return r * (dxh - xh * _rowmean(dxh * xh)), _colsum(dy * xh)


def _dn_gate_mm(o2, p, dn_norm, w_bdn, n_all, hi):
    n = n_all - CT

    def fn(i, j, of, ob, gt, w, wb):
        o = of + ob
        y = _heads(o, lambda h, x: _hnorm(x, w)) * _silu(gt)
        return y, _dot(y, wb, NN, hi)

    ins = [_In(o2, roff=CT), _In(o2, roff=n_all + CT), _In(p, D, C_GT // D, roff=CT), _Full(dn_norm), _Full(w_bdn)]
    return _rowcall("dn_gate_mm", fn, n, TILE, ins, [_Out(D, BF16), _Out(D)])


def _mm_bdn_dx_gate(dz_dn, w_bdn, o2, p, dn_norm, n_all, dp, hi):
    def fn(i, dy_, of, ob, gt, w):
        o = of + ob
        sg, dsg = _silu_both(gt)
        dos, dw = [], jnp.zeros((1, HD), F32)
        yn = []
        for h in range(NH):
            sl = slice(h * HD, (h + 1) * HD)
            dx, dwh = _hnorm_bwd(dy_[:, sl] * sg[:, sl], o[:, sl], w)
            dos.append(dx)
            dw = dw + dwh
            yn.append(_hnorm(o[:, sl], w))
        dgt = dy_ * jnp.concatenate(yn, axis=1) * dsg
        return jnp.concatenate(dos, axis=1), dgt, dw

    ins = [_In(o2, roff=CT), _In(o2, roff=n_all + CT), _In(p, D, C_GT // D, roff=CT), _Full(dn_norm)]
    outs = [_Out(D), _Out(PW, BF16, w=D, cb=C_GT // D, roff=CT, nrows=p.shape[0], into=dp), _Out(HD, acc=True)]
    return _mm_ep("mm_bdn_dx_gate", dz_dn, w_bdn, True, TILE, D, fn, ins, outs, hi)


def _rope_shuffle(x):
    lane = lax.broadcasted_iota(jnp.int32, (1, HD), 1)
    return jnp.where((lane % 64) < 32, pltpu.roll(x, HD - 32, 1), pltpu.roll(x, 32, 1))


def _rope(x, cos, sin):
    return x * cos + _rope_shuffle(x) * sin


def _rope_bwd(dy, cos, sin):
    return dy * cos + _rope_shuffle(dy * sin)


def _attn_prep(p, w, cos, sin, width, cb, roff, n, name):
    def fn(i, j, x, w_, c, s):
        return (_heads(x, lambda h, xh: _rope(_hnorm(xh, w_), c, s)),)

    ins = [_In(p, width, cb, roff=roff), _Full(w), _In(cos), _In(sin)]
    return _rowcall(name, fn, n, TILE, ins, [_Out(width)])[0]


def _attn_prep_bwd(dy, p, w, cos, sin, width, cb, roff, n, name, dx_out):
    def fn(i, j, dy_, x, w_, c, s):
        dxs, dw = [], jnp.zeros((1, HD), F32)
        for h in range(width // HD):
            sl = slice(h * HD, (h + 1) * HD)
            dx, dwh = _hnorm_bwd(_rope_bwd(dy_[:, sl], c, s), x[:, sl], w_)
            dxs.append(dx)
            dw = dw + dwh
        return jnp.concatenate(dxs, axis=1), dw

    ins = [_In(dy), _In(p, width, cb, roff=roff), _Full(w), _In(cos), _In(sin)]
    return _rowcall(name, fn, n, TILE, ins, [dx_out, _Out(HD, acc=True)])


def _mm_bat_merge(o_at, w_bat, z_dn, p, hi):
    def fn(i, za, zd, gd, ga):
        return za, jax.nn.sigmoid(gd) * zd + jax.nn.sigmoid(ga) * za

    ins = [_In(z_dn), _In(p, D, C_MG // D, roff=CT), _In(p, D, C_MG // D + 1, roff=CT)]
    return _mm_ep("mm_bat_merge", o_at, w_bat, False, TILE, D, fn, ins, [_Out(D), _Out(D, BF16)], hi)


def _mm_out_dx_merge(dmo, w_out, z_dn, z_at, p, hi):
    def fn(i, dm_, zd, za, gd, ga):
        sd, sa = jax.nn.sigmoid(gd), jax.nn.sigmoid(ga)
        dg = jnp.concatenate([dm_ * zd * sd * (1.0 - sd), dm_ * za * sa * (1.0 - sa)], axis=1)
        return dm_ * sd, dm_ * sa, dg

    ins = [_In(z_dn), _In(z_at), _In(p, D, C_MG // D, roff=CT), _In(p, D, C_MG // D + 1, roff=CT)]
    outs = [_Out(D, BF16), _Out(D, BF16), _Out(PW, BF16, w=2 * D, cb=C_MG // (2 * D), roff=CT, nrows=p.shape[0])]
    return _mm_ep("mm_out_dx_merge", dmo, w_out, True, TILE, D, fn, ins, outs, hi)


def _mm_out_resid(merged, w_out, x, g_a, nf, mod_f, hi):
    def fn(i, mo_, x_, ga, nf_, m):
        x1 = x_ + ga * mo_
        return mo_, x1, _rms_mod(x1, nf_, m[0:1], m[1:2])

    ins = [_In(x), _Full(g_a), _Full(nf), _Full(mod_f)]
    return _mm_ep("mm_out_resid", merged, w_out, False, min(512, x.shape[0]), D, fn, ins, [_Out(D), _Out(D), _Out(D, BF16)], hi)


def _mm_up_dx_norm(du, ffn_up, dy, x1, mo, g_a, nf, mod_f, hi):
    def fn(i, dh_, dy_, x1_, mo_, ga, nf_, m):
        dx, dn, dsh, dsc = _rms_mod_bwd(dh_, x1_, nf_, m[1:2])
        dx1 = dy_ + dx
        return dx1, ga * dx1, dn, dsh, dsc, _colsum(dx1 * mo_)

    ins = [_In(dy), _In(x1), _In(mo), _Full(g_a), _Full(nf), _Full(mod_f)]
    accs = [_Out(D, acc=True) for _ in range(4)]
    return _mm_ep("mm_up_dx_norm", du, ffn_up, True, min(512, x1.shape[0]), 1408, fn, ins, [_Out(D), _Out(D, BF16)] + accs, hi)


def _mm_down_loss(a, ffn_down, x1, tgt, g_f, hi):
    def fn(i, f_, x1_, t, gf):
        e = x1_ + gf * f_ - t
        dy = e * (1.0 / D)
        loss = _colsum(_rowsum(e * e)) * (0.5 / D)
        return dy, gf * dy, _colsum(dy * f_), jnp.broadcast_to(loss, (1, 128))

    ins = [_In(x1), _In(tgt), _Full(g_f)]
    outs = [_Out(D), _Out(D, BF16), _Out(D, acc=True), _Out(128, acc=True)]
    return _mm_ep("mm_down_loss", a, ffn_down, False, min(512, x1.shape[0]), DFF, fn, ins, outs, hi)


FW = DFF // 2


def _ffn_act(u, conv_w, conv_b, n):
    halo = _lat_halo(n)

    def fn(i, j, ge, ve, wg, wv, bg, bv):
        cg = _conv_taps(ge, wg, 3, TILE) + bg
        cv = _conv_taps(ve, wv, 3, TILE) + bv
        return (_silu(cg) * cv,)

    ins = [_In(u, FW, 0, halo=halo), _In(u, FW, 2, halo=halo), _Full(conv_w, FW, 0), _Full(conv_w, FW, 2),
           _Full(conv_b, FW, 0), _Full(conv_b, FW, 2)]
    return _rowcall("ffn_act", fn, n, TILE, ins, [_Out(DFF, BF16, FW)], ncol=2)[0]


def _ffn_act_bwd(u, da, conv_w, conv_b, n):
    halo = _lat_halo(n)

    def fn(i, j, ge, ve, dae, wg, wv, bg, bv):
        rg, rv = _rolled(ge, 3), _rolled(ve, 3)
        cg = rg[0] * wg[0:1] + rg[1] * wg[1:2] + rg[2] * wg[2:3] + bg
        cv = rv[0] * wv[0:1] + rv[1] * wv[1:2] + rv[2] * wv[2:3] + bv
        sg, dsg = _silu_both(cg)
        dcg = dae * cv * dsg
        dcv = dae * sg
        dxg, dwg = _conv_bwd(rg, wg, dcg, 3)
        dxv, dwv = _conv_bwd(rv, wv, dcv, 3)
        return (dxg, dxv), dwg, dwv, _colsum(dcg[HALO:HALO + TILE]), _colsum(dcv[HALO:HALO + TILE])

    ins = [_In(u, FW, 0, halo=halo), _In(u, FW, 2, halo=halo), _In(da, FW, 0, halo=halo),
           _Full(conv_w, FW, 0), _Full(conv_w, FW, 2), _Full(conv_b, FW, 0), _Full(conv_b, FW, 2)]
    outs = [_Out(DFF, BF16, FW, stack=2), _Out(DFF, w=FW, acc=True, rows=8), _Out(DFF, w=FW, acc=True, rows=8),
            _Out(DFF, w=FW, acc=True), _Out(DFF, w=FW, acc=True)]
    return _rowcall("ffn_act_bwd", fn, n, TILE, ins, outs, ncol=2)


def _rope_tables(tl):
    rows = tl // GRID_W
    inv = np.float32(ROPE_BASE) ** (-np.arange(32, dtype=np.float32) / np.float32(32))
    ar = np.arange(rows, dtype=np.float32)[:, None] * inv
    ac = np.arange(GRID_W, dtype=np.float32)[:, None] * inv

    def table(r, c):
        full = (rows, GRID_W, HD // 2)
        return jnp.concatenate([jnp.broadcast_to(jnp.asarray(r)[:, None, :], full),
                                jnp.broadcast_to(jnp.asarray(c)[None, :, :], full)], axis=2).reshape(tl, HD)

    two = lambda a, b: np.concatenate([a, b], axis=1).astype(np.float32)
    cos = table(two(np.cos(ar), np.cos(ar)), two(np.cos(ac), np.cos(ac)))
    sin = table(two(-np.sin(ar), np.sin(ar)), two(-np.sin(ac), np.sin(ac)))
    return cos, sin


def _pad_w_in(w_in):
    return jnp.concatenate([w_in[:, 0:3072], w_in[:, 5152:5664], w_in[:, 4096:4128], jnp.zeros((D, 96 + C_GT - C_PAD), w_in.dtype),
                            w_in[:, 3072:4096], w_in[:, 4128:5152], w_in[:, 5664:7712]], axis=1)


def _unpad_w_in(g, axis=1):
    cut = lambda a, b: lax.slice_in_dim(g, a, b, axis=axis)
    return jnp.concatenate([cut(0, 3072), cut(C_GT, C_GT + D), cut(C_BA, C_BA + 32), cut(C_QAT, C_QAT + D),
                            cut(C_KAT, C_KAT + 512), cut(C_MG, C_MG + 2 * D)], axis=axis)


def _local_step(x, ctx, tgt, mod_x, mod_c, w, hi=False):
    tl = x.shape[0]
    t_all = tl + CTX
    n_all, n = t_all // TILE, tl // TILE
    tm_all = 1280 if t_all % 1280 == 0 else TILE
    tm_lat = 1024
    mm = functools.partial(_mm, hi=hi)
    sp = lambda m: [m[:, k * D:(k + 1) * D] for k in range(6)]
    sh_a, sc_a, g_a, sh_f, sc_f, g_f = sp(mod_x)
    sh_ac, sc_ac = sp(mod_c)[:2]
    mod_ax = jnp.concatenate([sh_a, sc_a], axis=0)
    mod_ac = jnp.concatenate([sh_ac, sc_ac], axis=0)
    mod_f = jnp.concatenate([sh_f, sc_f], axis=0)
    nm, nf = w["norm_mix"], w["norm_ffn"]
    cos, sin = _rope_tables(tl)
    cos_all = jnp.concatenate([jnp.ones((CTX, HD), F32), cos], axis=0)
    sin_all = jnp.concatenate([jnp.zeros((CTX, HD), F32), sin], axis=0)
    conv_dn = jnp.concatenate([w["dn_conv"], jnp.zeros((3, 3 * D), F32)], axis=0)
    gprm = jnp.concatenate([jnp.zeros((2, 16), F32),
                            jnp.concatenate([w["dn_a_log"].reshape(1, 16), w["dn_dt_bias"].reshape(1, 16)], axis=0),
                            jnp.zeros((2, 96), F32)], axis=1)
    conv_ff = jnp.concatenate([w["ffn_conv"], jnp.zeros((5, 2 * DFF), F32)], axis=0)
    sink = jnp.concatenate([w["attn_sink"].reshape(1, NH), jnp.zeros((1, 128 - NH), F32)], axis=1)
    nct = CTX // CH

    h = _norm_mod(x, ctx, nm, mod_ac, mod_ax)
    p = mm(h, w["w_in_p"], tm=tm_all, tn=2048, name="mm_in")
    q, k, v, gb_f, gb_b = _dn_prep(p, conv_dn, gprm)
    gb = jnp.stack([gb_f, gb_b])
    dn_u, dn_w, dn_qg, dn_kd, dn_pm, dn_t = _dn_intra_fwd(q, k, v, gb, nct, hi)
    o2, s_hist, dn_vn = _dn_seq_fwd(dn_u, dn_w, dn_qg, dn_kd, dn_pm, gb, nct, hi)
    o2 = o2.reshape(2 * t_all, D)
    y_dn, z_dn = _dn_gate_mm(o2, p, w["dn_norm"], w["w_branch_dn"], n_all, hi)
    qr = _attn_prep(p, w["q_norm"], cos, sin, D, C_QAT // D, CT, n, "attn_prep_q")
    kr = _attn_prep(p, w["k_norm"], cos_all, sin_all, KVH * HD, C_KAT // (KVH * HD), 0, n_all, "attn_prep_k")
    vv = p[:, C_VAT:C_VAT + KVH * HD]
    o_at, lse = _attn_fwd(qr, kr, vv, sink, hi)
    z_at, merged = _mm_bat_merge(o_at, w["w_branch_attn"], z_dn, p, hi)
    mo, x1, h2 = _mm_out_resid(merged, w["w_out"], x, g_a, nf, mod_f, hi)
    u = mm(h2, w["ffn_up"], tm=2 * tm_lat, tn=1408, name="mm_up")
    a = _ffn_act(u, conv_ff, w["ffn_conv_b"], n)
    dy, df, dg_f, loss = _mm_down_loss(a, w["ffn_down"], x1, tgt, g_f, hi)

    g = {}
    da = mm(df, w["ffn_down"], tb=True, tm=tm_lat, tn=1408, name="mm_down_dx")
    g["ffn_down"] = mm(a, df, ta=True, tm=1408, tn=1024, tk=2 * tm_lat, name="mm_down_dw")
    du, dcw_g, dcw_v, dcb_g, dcb_v = _ffn_act_bwd(u, da, conv_ff, w["ffn_conv_b"], n)
    g["ffn_conv"] = jnp.concatenate([dcw_g, dcw_v], axis=1)[0:3]
    g["ffn_conv_b"] = jnp.concatenate([dcb_g, dcb_v], axis=1)
    g["ffn_up"] = mm(h2, du, ta=True, tm=1024, tn=1408, tk=2 * tm_lat, name="mm_up_dw")
    dx1, dmo, g["norm_ffn"], dsh_f, dsc_f, dg_a = _mm_up_dx_norm(du, w["ffn_up"], dy, x1, mo, g_a, nf, mod_f, hi)
    g["w_out"] = mm(merged, dmo, ta=True, tm=1024, tk=2 * tm_lat, name="mm_out_dw")
    dz_dn, dz_at, dmg = _mm_out_dx_merge(dmo, w["w_out"], z_dn, z_at, p, hi)
    g["w_branch_dn"] = mm(y_dn, dz_dn, ta=True, tm=1024, tk=2 * tm_lat, name="mm_bdn_dw")
    do_at, delta = _mm_bat_dx_delta(dz_at, w["w_branch_attn"], o_at, hi)
    g["w_branch_attn"] = mm(o_at, dz_at, ta=True, tm=1024, tk=2 * tm_lat, name="mm_bat_dw")

    do_dn, dp, g["dn_norm"] = _mm_bdn_dx_gate(dz_dn, w["w_branch_dn"], o2, p, w["dn_norm"], n_all, dmg, hi)
    do_all = do_dn
    dn_dvn, dn_dw, dn_dqg, dn_dkd, dn_del = _dn_seq_bwd(dn_w, dn_qg, dn_kd, dn_pm, dn_vn, s_hist, gb, do_all, nct, hi)
    dq2, dk2, dv2, dgb2 = _dn_intra_bwd(q, k, v, gb, dn_u, dn_w, dn_t, dn_vn, dn_dvn, dn_dw, dn_dqg, dn_dkd, dn_del,
                                        do_all, nct, hi)

    dqr, dk_lat, dv_lat, dkx, dvx, dsink = _attn_bwd(qr, kr, vv, sink, do_at, lse, delta, hi)
    g["attn_sink"] = dsink[:, 0:NH]
    q_out = _Out(PW, BF16, w=D, cb=C_QAT // D, roff=CT, nrows=t_all, into=dp)
    dp, g["q_norm"] = _attn_prep_bwd(dqr, p, w["q_norm"], cos, sin, D, C_QAT // D, CT, n, "attn_prep_q_bwd", q_out)
    dkr = jnp.concatenate([dkx, dk_lat], axis=0)
    dk_at, g["k_norm"] = _attn_prep_bwd(dkr, p, w["k_norm"], cos_all, sin_all, KVH * HD, C_KAT // (KVH * HD), 0, n_all,
                                        "attn_prep_k_bwd", _Out(KVH * HD, BF16))
    dv_at = jnp.concatenate([dvx, dv_lat], axis=0).astype(BF16)

    dp, dconv, dgprm = _dn_prep_bwd(p, conv_dn, gprm, dq2.reshape(2 * t_all, D), dk2.reshape(2 * t_all, D),
                                    dv2.reshape(2 * t_all, D), dgb2.reshape(2 * t_all, 128), dk_at, dv_at, dp)
    g["dn_conv"] = dconv[0:5]
    g["dn_a_log"] = dgprm[0, 16:32].reshape(2, NH)
    g["dn_dt_bias"] = dgprm[1, 16:32].reshape(2, NH)
    dp = lax.dynamic_update_slice(dp, jnp.zeros((CTX, PH), BF16), (0, PH))
    dh = mm(dp, w["w_in_p"], tb=True, tm=tm_all, tn=1024, tk=2048, name="mm_in_dx")
    g["w_in_p"] = mm(h, dp, ta=True, tm=1024, tn=2048, tk=tm_all, name="mm_in_dw")
    dnm_c, dsh_ac, dsc_ac = _norm_mod_bwd(dh, ctx, None, nm, mod_ac, 0, CT)
    grad_x, dnm_x, dsh_a, dsc_a = _norm_mod_bwd(dh, x, dx1, nm, mod_ax, CT, n)
    g["norm_mix"] = dnm_c + dnm_x
    dmod_x = jnp.concatenate([dsh_a, dsc_a, dg_a, dsh_f, dsc_f, dg_f], axis=1)
    dmod_c = jnp.concatenate([dsh_ac, dsc_ac, jnp.zeros((1, 4 * D), F32)], axis=1)
    return loss, grad_x, g, dmod_x, dmod_c


def _sum_slots(buf, n_slots, rows, tile, name, stride=1):
    nt = rows // tile

    def fn(i, j, *vals):
        acc = vals[0]
        for v in vals[1:]:
            acc = acc + v
        return (acc,)

    ins = [_In(buf, roff=k * stride * nt) for k in range(n_slots)]
    return _rowcall(name, fn, nt, tile, ins, [_Out(buf.shape[1])])[0]


ADAM_LR, ADAM_B1, ADAM_B2, ADAM_EPS, ADAM_WD, ADAM_STEP = 0.001, 0.9, 0.999, 1e-08, 0.01, 10


def _row_tile(rows, cols):
    for t in (512, 256, 128, 64, 32, 16, 8):
        if rows % t == 0 and t * cols * 4 * 14 <= 40 * 1024 * 1024:
            return t
    return rows


def _adamw(w, g, m, v, name):
    shape = w.shape
    cols = shape[-1]
    rows = max(1, math.prod(shape[:-1]))
    tile = _row_tile(rows, cols)
    c1 = 1.0 / (1.0 - ADAM_B1 ** ADAM_STEP)
    c2 = 1.0 / (1.0 - ADAM_B2 ** ADAM_STEP)

    def fn(i, j, w_, g_, m_, v_):
        mn = ADAM_B1 * m_ + (1.0 - ADAM_B1) * g_
        vn = ADAM_B2 * v_ + (1.0 - ADAM_B2) * (g_ * g_)
        delta = -ADAM_LR * ((mn * c1) / (jnp.sqrt(vn * c2) + ADAM_EPS) + ADAM_WD * w_)
        return delta, mn, vn

    r2 = lambda a: a.reshape(rows, cols)
    outs = _rowcall(name, fn, rows // tile, tile, [_In(r2(w)), _In(r2(g)), _In(r2(m)), _In(r2(v))],
                    [_Out(cols), _Out(cols), _Out(cols)])
    return [o.reshape(shape) for o in outs]


MESH = pl.DeviceIdType.MESH
ANY = pl.BlockSpec(memory_space=pl.ANY)


def _pos():
    return lax.axis_index("x"), lax.axis_index("y"), lax.axis_index("c")


def _all_gather_many(blks, name):
    na = len(blks)

    def body(*refs):
        x_refs, out_refs = refs[:na], refs[na:2 * na]
        send_sems, recv_sems, local_sems = refs[2 * na:]
        x, y, c = _pos()
        me, sibling = (x, y, c), (x, y, 1 - c)
        chips = [(1 - x, y), (x, 1 - y), (1 - x, 1 - y)]

        def rows(a, px, py, pc):
            m_per = blks[a].shape[0]
            return out_refs[a].at[pl.ds(pl.multiple_of((4 * px + 2 * py + pc) * m_per, 8), m_per), :]

        def copy(a, k, block, to, src=None):
            return pltpu.make_async_remote_copy(
                src_ref=rows(a, *block) if src is None else src, dst_ref=rows(a, *block),
                send_sem=send_sems.at[7 * a + k], recv_sem=recv_sems.at[7 * a + k], device_id=to, device_id_type=MESH)

        every = range(na)
        mine = [pltpu.make_async_copy(x_refs[a], rows(a, *me), local_sems.at[a]) for a in every]
        for cp in mine:
            cp.start()
        first = [copy(a, 0, me, sibling, src=x_refs[a]) for a in every]
        first += [copy(a, 1 + j, me, (*chip, c), src=x_refs[a]) for j, chip in enumerate(chips) for a in every]
        for cp in first:
            cp.start()
        passed = []
        for j, chip in enumerate(chips):
            for a in every:
                copy(a, 1 + j, (*chip, c), me).wait_recv()
                passed.append(copy(a, 4 + j, (*chip, c), sibling))
                passed[-1].start()
        for a in every:
            copy(a, 0, sibling, me).wait_recv()
        for j, chip in enumerate(chips):
            for a in every:
                copy(a, 4 + j, (*chip, 1 - c), me).wait_recv()
        for cp in first + passed:
            cp.wait_send()
        for cp in mine:
            cp.wait()

    return pl.pallas_call(
        body, name=name,
        out_shape=[jax.ShapeDtypeStruct((N_DEV * b.shape[0], b.shape[1]), b.dtype) for b in blks],
        in_specs=[ANY] * na, out_specs=[ANY] * na,
        scratch_shapes=[pltpu.SemaphoreType.DMA((7 * na,)), pltpu.SemaphoreType.DMA((7 * na,)), pltpu.SemaphoreType.DMA((na,))],
        compiler_params=pltpu.CompilerParams(has_side_effects=True),
    )(*blks)


def _all_gather(blk, name):
    return _all_gather_many([blk], name)[0]


def _flip(v, bit):
    return 1 - v if bit else v


D2D_STREAMS = 8
ICI_STREAMS = 2


def _sibling_exchange(src, seg_rows, n_seg, paired, name):
    n = src.shape[1]
    per_seg = D2D_STREAMS // n_seg
    per = seg_rows // per_seg
    assert per_seg * n_seg == D2D_STREAMS and per * per_seg == seg_rows and per % 16 == 0

    def body(x_ref, out_ref, send_sems, recv_sems):
        x, y, c = _pos()
        copies = []
        for s in range(n_seg):
            base = (2 * s + (1 - c)) * seg_rows if paired else s * seg_rows
            for j in range(per_seg):
                i = s * per_seg + j
                cp = pltpu.make_async_remote_copy(
                    src_ref=x_ref.at[pl.ds(pl.multiple_of(base + j * per, 16), per), :],
                    dst_ref=out_ref.at[pl.ds(s * seg_rows + j * per, per), :],
                    send_sem=send_sems.at[i], recv_sem=recv_sems.at[i], device_id=(x, y, 1 - c), device_id_type=MESH)
                cp.start()
                copies.append(cp)
        for cp in copies:
            cp.wait_recv()
        for cp in copies:
            cp.wait_send()

    return pl.pallas_call(
        body, name=name, out_shape=jax.ShapeDtypeStruct((n_seg * seg_rows, n), src.dtype),
        in_specs=[ANY], out_specs=ANY,
        scratch_shapes=[pltpu.SemaphoreType.DMA((D2D_STREAMS,)), pltpu.SemaphoreType.DMA((D2D_STREAMS,))],
        compiler_params=pltpu.CompilerParams(has_side_effects=True),
    )(src)


def _transpose_cast(x, dtype, name):
    r, c = x.shape
    tc = 512

    def body(x_ref, o_ref):
        o_ref[...] = x_ref[...].T.astype(o_ref.dtype)

    return pl.pallas_call(
        body, name=name, grid=(c // tc,),
        in_specs=[pl.BlockSpec((r, tc), lambda j: (0, j))], out_specs=pl.BlockSpec((tc, r), lambda j: (j, 0)),
        out_shape=jax.ShapeDtypeStruct((c, r), dtype), compiler_params=_cparams(("parallel",)),
    )(x)


def _chip_exchange(buf, rows, name):
    n = buf.shape[1]
    per = rows // ICI_STREAMS
    assert per * ICI_STREAMS == rows and per % 16 == 0

    def body(x_ref, out_ref, send_sems, recv_sems):
        x, y, c = _pos()
        copies = []
        for k in range(1, 4):
            px, py = _flip(x, k & 2), _flip(y, k & 1)
            for j in range(ICI_STREAMS):
                i = (k - 1) * ICI_STREAMS + j
                cp = pltpu.make_async_remote_copy(
                    src_ref=x_ref.at[pl.ds(pl.multiple_of((2 * px + py) * rows + j * per, 16), per), :],
                    dst_ref=out_ref.at[pl.ds((k - 1) * rows + j * per, per), :],
                    send_sem=send_sems.at[i], recv_sem=recv_sems.at[i], device_id=(px, py, c), device_id_type=MESH)
                cp.start()
                copies.append(cp)
        for cp in copies:
            cp.wait_recv()
        for cp in copies:
            cp.wait_send()

    return pl.pallas_call(
        body, name=name, out_shape=jax.ShapeDtypeStruct((3 * rows, n), buf.dtype),
        in_specs=[ANY], out_specs=ANY,
        scratch_shapes=[pltpu.SemaphoreType.DMA((3 * ICI_STREAMS,)), pltpu.SemaphoreType.DMA((3 * ICI_STREAMS,))],
        compiler_params=pltpu.CompilerParams(has_side_effects=True),
    )(buf)


def _add_rows(parts, rows, dtype, name):
    tile = 1024
    ins = [_In(a, roff=r0 // tile) for a, r0 in parts]

    def fn(i, j, *vals):
        acc = vals[0].astype(F32)
        for v_ in vals[1:]:
            acc = acc + v_.astype(F32)
        return (acc,)

    return _rowcall(name, fn, rows // tile, tile, ins, [_Out(parts[0][0].shape[1], dtype)])[0]


BIG = ("w_in", "w_branch_dn", "w_branch_attn", "w_out", "ffn_up", "ffn_down")
BIG_SHARD = {"w_in": (1024, 1928, True), "w_branch_dn": (256, 1024, False), "w_branch_attn": (256, 1024, False),
             "w_out": (256, 1024, False), "ffn_up": (1024, 1408, True), "ffn_down": (704, 1024, False)}
BIG_ROWS = {k: r * c // 2 // 128 for k, (r, c, _) in BIG_SHARD.items()}
PIECE = 19456
assert sum(BIG_ROWS.values()) <= PIECE


def _gather_weights(shards, ci):
    halves = []
    for k in BIG:
        r, c, _ = BIG_SHARD[k]
        halves.append(lax.dynamic_slice_in_dim(shards[k], ci * (r // 2), r // 2, axis=0).astype(BF16))
    out = {}
    for k, ag in zip(BIG, _all_gather_many(halves, "ag_weights")):
        r, c, by_col = BIG_SHARD[k]
        blk = ag.reshape(4, r, c)
        out[k] = jnp.transpose(blk, (1, 0, 2)).reshape(r, 4 * c) if by_col else blk.reshape(4 * r, c)
    return out


def _pack_pieces(full):
    parts = [full["w_in_t"].reshape(N_DEV, BIG_ROWS["w_in"], 128).astype(BF16)]
    for k in BIG[1:]:
        r, c, by_col = BIG_SHARD[k]
        a = full[k]
        if by_col:
            a = jnp.transpose(a.reshape(r, 4, c), (1, 0, 2))
        parts.append(a.reshape(N_DEV, BIG_ROWS[k], 128).astype(BF16))
    parts.append(jnp.zeros((N_DEV, PIECE - sum(BIG_ROWS.values()), 128), BF16))
    return jnp.concatenate(parts, axis=1).reshape(N_DEV * PIECE, 128)


def _reduce_scatter(pieces, ci, shard):
    half = N_DEV // 2 * PIECE
    theirs = _sibling_exchange(pieces, PIECE, N_DEV // 2, True, "rs_d2d")
    own = lax.dynamic_index_in_dim(pieces.reshape(N_DEV // 2, 2, PIECE, 128), ci, axis=1, keepdims=False).reshape(half, 128)
    part = _add_rows([(own, 0), (theirs, 0)], half, BF16, "rs_sum_chip")
    recv = _chip_exchange(part, PIECE, "rs_ici")
    own2 = lax.dynamic_slice_in_dim(part, shard * PIECE, PIECE, axis=0)
    mine = _add_rows([(own2, 0), (recv, 0), (recv, PIECE), (recv, 2 * PIECE)], PIECE, F32, "rs_sum_all")
    other = _sibling_exchange(mine, PIECE, 1, False, "rs_pair")
    return jnp.where(ci == 0, jnp.stack([mine, other]), jnp.stack([other, mine]))


def _unpack_shard(two):
    out, off = {}, 0
    for k in BIG:
        r, c, _ = BIG_SHARD[k]
        blk = two[:, off:off + BIG_ROWS[k]]
        out[k] = blk.reshape(c, r).T if k == "w_in" else blk.reshape(r, c)
        off += BIG_ROWS[k]
    return out


SMALL = (("dn_conv", 120), ("ffn_conv", 132), ("ffn_conv_b", 44), ("norm_mix", 8), ("norm_ffn", 8), ("dn_a_log", 1),
         ("dn_dt_bias", 1), ("dn_norm", 1), ("q_norm", 1), ("k_norm", 1), ("attn_sink", 1), ("dmod_c", 48), ("dmod_x", 48))
SMALL_ROWS = 416


def _rows128(a, rows):
    flat = a.reshape(-1)
    return jnp.concatenate([flat, jnp.zeros((rows * 128 - flat.shape[0],), F32)]).reshape(rows, 128)


def _pack_small(g):
    parts = [_rows128(g[k], r) for k, r in SMALL]
    parts.append(jnp.zeros((SMALL_ROWS - sum(r for _, r in SMALL), 128), F32))
    return jnp.concatenate(parts, axis=0)


def _unpack_small(buf, shapes):
    out, off = {}, 0
    for k, r in SMALL:
        n = math.prod(shapes[k])
        out[k] = buf[off:off + r].reshape(-1)[:n].reshape(shapes[k])
        off += r
    return out


WEIGHTS = ("c_ctx", "w_ada", "b_ada", "norm_mix", "norm_ffn", "w_in", "dn_conv", "dn_a_log", "dn_dt_bias", "dn_norm",
           "q_norm", "k_norm", "attn_sink", "w_branch_dn", "w_branch_attn", "w_out", "ffn_up", "ffn_conv", "ffn_conv_b",
           "ffn_down")


def kernel(x, c, ctx, c_ctx, w_ada, b_ada, norm_mix, norm_ffn, w_in, dn_conv, dn_a_log, dn_dt_bias, dn_norm, q_norm, k_norm, attn_sink, w_branch_dn, w_branch_attn, w_out, ffn_up, ffn_conv, ffn_conv_b, ffn_down, loss_target, m_c_ctx, m_w_ada, m_b_ada, m_norm_mix, m_norm_ffn, m_w_in, m_dn_conv, m_dn_a_log, m_dn_dt_bias, m_dn_norm, m_q_norm, m_k_norm, m_attn_sink, m_w_branch_dn, m_w_branch_attn, m_w_out, m_ffn_up, m_ffn_conv, m_ffn_conv_b, m_ffn_down, v_c_ctx, v_w_ada, v_b_ada, v_norm_mix, v_norm_ffn, v_w_in, v_dn_conv, v_dn_a_log, v_dn_dt_bias, v_dn_norm, v_q_norm, v_k_norm, v_attn_sink, v_w_branch_dn, v_w_branch_attn, v_w_out, v_ffn_up, v_ffn_conv, v_ffn_conv_b, v_ffn_down):
    args = dict(locals())
    xi, yi, ci = _pos()
    dev = 4 * xi + 2 * yi + ci
    shard = 2 * xi + yi
    chips = lambda a: a[0::2]

    blk = jnp.concatenate([_rows128(c, 8), _rows128(dn_conv, 30), _rows128(ffn_conv, 33), jnp.zeros((1, 128), F32)], axis=0)
    ag = _all_gather(blk, "ag_small_in").reshape(N_DEV, 72, 128)
    c_all = ag[:, 0:8].reshape(N_DEV, D)
    dn_conv_full = jnp.transpose(chips(ag)[:, 8:38].reshape(4, 5, 768), (1, 0, 2)).reshape(5, 3 * D)
    ffn_conv_full = jnp.transpose(chips(ag)[:, 38:71].reshape(4, 3, 1408), (1, 0, 2)).reshape(3, 2 * DFF)

    c16 = jnp.concatenate([c_all, c_ctx[None], jnp.zeros((7, D), F32)], axis=0)
    a16 = _rowcall("ada_silu", lambda i, j, v: (_silu(v),), 1, 16, [_In(c16)], [_Out(D)])[0]
    m_sh = _mm(a16, w_ada[0], tm=16, tn=512, tk=D, name="ada_fwd", hi=True)
    mod16 = chips(_all_gather(m_sh, "ag_mod").reshape(N_DEV, 16, 1536))
    mod16 = jnp.transpose(mod16, (1, 0, 2)).reshape(16, 6 * D) + b_ada
    mod_x = lax.dynamic_slice_in_dim(mod16, dev, 1, axis=0)
    mod_c = mod16[8:9]

    shards = {k: args[k][0] for k in BIG}
    wfull = _gather_weights(shards, ci)
    w = dict(wfull)
    w["w_in_p"] = _pad_w_in(wfull["w_in"])
    w.update(norm_mix=norm_mix, norm_ffn=norm_ffn, dn_conv=dn_conv_full, dn_a_log=dn_a_log[0], dn_dt_bias=dn_dt_bias[0],
             dn_norm=dn_norm, q_norm=q_norm, k_norm=k_norm, attn_sink=attn_sink, ffn_conv=ffn_conv_full, ffn_conv_b=ffn_conv_b)

    loss_part, grad_x, g, dmod_x, dmod_c = _local_step(x[0], ctx[0], loss_target[0], mod_x, mod_c, w)
    loss = lax.psum(loss_part[0, 0], ("x", "y", "c"))

    g["w_in_t"] = _unpad_w_in(_transpose_cast(g["w_in_p"], BF16, "w_in_grad_t"), axis=0)
    gshard = _unpack_shard(_reduce_scatter(_pack_pieces(g), ci, shard))

    g["dmod_c"], g["dmod_x"] = dmod_c, dmod_x
    ag_s = _all_gather(_pack_small(g), "ag_small_grads")
    shapes = {k: g[k].shape for k, _ in SMALL}
    gs = _unpack_small(_sum_slots(ag_s, N_DEV, SMALL_ROWS, SMALL_ROWS, "small_sum"), shapes)
    dx_all = ag_s.reshape(N_DEV, SMALL_ROWS, 128)[:, SMALL_ROWS - 50:SMALL_ROWS - 2].reshape(N_DEV, 6 * D)

    d16 = jnp.concatenate([dx_all, gs["dmod_c"], jnp.zeros((7, 6 * D), F32)], axis=0)
    d16_sh = lax.dynamic_slice_in_dim(d16, shard * 1536, 1536, axis=1)
    g_w_ada = _mm(a16, d16_sh, ta=True, tm=D, tn=512, tk=16, name="ada_dw", hi=True)
    g_b_ada = _rowcall("ada_db", lambda i, j, v: (_colsum(v),), 1, 16, [_In(d16)], [_Out(6 * D, acc=True)])[0]
    da_part = _mm(d16_sh, w_ada[0], tb=True, tm=16, tn=D, tk=512, name="ada_dx", hi=True)
    da_all = _all_gather(da_part, "ag_ada_dx")
    da16 = _sum_slots(da_all, 4, 16, 16, "ada_dx_sum", stride=2)
    dc16 = _rowcall("ada_dsilu", lambda i, j, d_, v: (d_ * _dsilu(v),), 1, 16, [_In(da16), _In(c16)], [_Out(D)])[0]

    grads = {
        "c_ctx": dc16[8], "w_ada": g_w_ada[None], "b_ada": g_b_ada, "norm_mix": gs["norm_mix"], "norm_ffn": gs["norm_ffn"],
        "w_in": gshard["w_in"][None],
        "dn_conv": lax.dynamic_slice_in_dim(gs["dn_conv"], shard * 768, 768, axis=1)[None],
        "dn_a_log": gs["dn_a_log"][None], "dn_dt_bias": gs["dn_dt_bias"][None], "dn_norm": gs["dn_norm"],
        "q_norm": gs["q_norm"], "k_norm": gs["k_norm"], "attn_sink": gs["attn_sink"],
        "w_branch_dn": gshard["w_branch_dn"][None], "w_branch_attn": gshard["w_branch_attn"][None],
        "w_out": gshard["w_out"][None], "ffn_up": gshard["ffn_up"][None],
        "ffn_conv": lax.dynamic_slice_in_dim(gs["ffn_conv"], shard * 1408, 1408, axis=1)[None],
        "ffn_conv_b": gs["ffn_conv_b"], "ffn_down": gshard["ffn_down"][None],
    }
    deltas, new_m, new_v = [], [], []
    for k in WEIGHTS:
        d_, m_, v_ = _adamw(args[k], grads[k], args["m_" + k], args["v_" + k], "adamw_" + k)
        deltas.append(d_)
        new_m.append(m_)
        new_v.append(v_)
    return (loss, grad_x[None], *[grads[k] for k in WEIGHTS], *deltas, *new_m, *new_v)
```

```python
import functools
import math

import numpy as np
import jax
import jax.numpy as jnp
from jax import lax
from jax.experimental import pallas as pl
from jax.experimental.pallas import tpu as pltpu

F32 = jnp.float32
BF16 = jnp.bfloat16
HI = lax.Precision.HIGHEST

D = 1024
NH = 8
HD = 128
CH = 64
CTX = 256
AB = 128
KVH = 2
GRP = 4
DFF = 2816
EPS = 1e-6
GRID_W = 64
ROPE_BASE = 10000.0
N_DEV = 8
VMEM_LIMIT = 56 * 1024 * 1024

C_QKV, C_KAT, C_VAT, C_BA, C_PAD, C_GT, C_QAT, C_MG = 0, 3072, 3328, 3584, 3712, 4096, 5120, 6144
PW = 8192
PH = PW // 2


def _cparams(sem=None, **kw):
    return pltpu.CompilerParams(dimension_semantics=sem, vmem_limit_bytes=VMEM_LIMIT, **kw)


def _dot(a, b, dims, hi):
    if hi:
        return lax.dot_general(a.astype(F32), b.astype(F32), (dims, ((), ())), precision=HI, preferred_element_type=F32)
    return lax.dot_general(a.astype(BF16), b.astype(BF16), (dims, ((), ())), preferred_element_type=F32)


NN = ((1,), (0,))
NT = ((1,), (1,))
TN = ((0,), (0,))


def _dn_masks():
    i = np.arange(CH)
    lo_incl = (i[:, None] >= i[None, :]).astype(np.float32)
    lo_strict = (i[:, None] > i[None, :]).astype(np.float32)
    return jnp.asarray(np.stack([np.stack([lo_incl, lo_strict]), np.stack([lo_incl.T, lo_strict.T])]))


def _dn_chunk_index(d, i, n_ctx_chunks, n_chunks):
    fwd = i
    bwd = jnp.where(i < n_ctx_chunks, n_ctx_chunks - 1 - i, n_chunks - 1 + n_ctx_chunks - i)
    return jnp.where(d == 0, fwd, bwd)


BNN = ((2,), (1,))
BNT = ((2,), (2,))
BTN = ((1,), (1,))


def _bdot(a, b, dims, hi):
    dn = (dims, ((0,), (0,)))
    if hi:
        return lax.dot_general(a.astype(F32), b.astype(F32), dn, precision=HI, preferred_element_type=F32)
    return lax.dot_general(a.astype(BF16), b.astype(BF16), dn, preferred_element_type=F32)


def _bdot3(a, b, dims, hi):
    if hi:
        return _bdot(a, b, dims, True)
    ah, bh = a.astype(BF16), b.astype(BF16)
    al, bl = (a - ah.astype(F32)).astype(BF16), (b - bh.astype(F32)).astype(BF16)
    dn = (dims, ((0,), (0,)))
    d = lambda x_, y_: lax.dot_general(x_, y_, dn, preferred_element_type=F32)
    return d(ah, bh) + d(ah, bl) + d(al, bh)


DN_CB = 4
DN_SEQ_CB = 4


def _dn_heads(ref, cb=1):
    return jnp.stack([ref[t * CH:(t + 1) * CH, h * HD:(h + 1) * HD] for t in range(cb) for h in range(NH)])


def _dn_scalars(gb, mi, cb=1):
    beta, gc, gcr, gt = [], [], [], []
    for t in range(cb):
        g1 = gb[t * CH:(t + 1) * CH]
        gcum, gcum_t, gtot = _dn_gcum(g1, mi)
        beta += [g1[:, h:h + 1] for h in range(NH)]
        gc += [gcum[:, NH + h:NH + h + 1] for h in range(NH)]
        gcr += [gcum_t[NH + h:NH + h + 1, :] for h in range(NH)]
        gt += [gtot[:, NH + h:NH + h + 1] for h in range(NH)]
    return jnp.stack(beta), jnp.stack(gc), jnp.stack(gcr), jnp.stack(gt)


DN_NEWTON = 1


def _dn_inverse(a, hi):
    eye = (lax.broadcasted_iota(jnp.int32, (CH, CH), 0) == lax.broadcasted_iota(jnp.int32, (CH, CH), 1)).astype(F32)
    x = -a
    t = eye + x
    p = x
    if hi:
        for _ in range(5):
            p = _bdot(p, p, BNN, True)
            t = t + _bdot(t, p, BNN, True)
        return t
    for _ in range(5):
        p = _bdot(p, p, BNN, False)
        t = t + _bdot(t, p, BNN, False)
    for _ in range(DN_NEWTON):
        r = eye - t - _bdot3(a, t, BNN, False)
        t = t + _bdot(t, r, BNN, False)
    return t


def _dn_total(gb):
    gtot = jnp.sum(gb, axis=0, keepdims=True)
    return jnp.stack([gtot[:, NH + h:NH + h + 1] for h in range(NH)])


def _dn_gcum(gb, mi):
    gcum = _dot(mi, gb, NN, True)
    gtot = jnp.sum(gb, axis=0, keepdims=True)
    return gcum, gcum.T, gtot


def _dn_specs(n_ctx_chunks, n_chunks, reverse, cb):
    assert n_ctx_chunks % cb == 0 and n_chunks % cb == 0

    def grp(d, i):
        first = n_chunks - 1 - cb * i if reverse else cb * i
        return _dn_chunk_index(d, first, n_ctx_chunks, n_chunks) // cb

    def slot(d, t):
        ascending = (d == 1) if reverse else (d == 0)
        return jnp.where(ascending, t, cb - 1 - t)

    ctx_groups = n_ctx_chunks // cb
    tok_lat = pl.BlockSpec((cb * CH, D), lambda d, i: (jnp.maximum(grp(d, i) - ctx_groups, 0), 0))
    is_ctx = lambda d, i: grp(d, i) < ctx_groups
    tok_d = pl.BlockSpec((1, cb * CH, D), lambda d, i: (d, grp(d, i), 0))
    gbs = pl.BlockSpec((1, cb * CH, 128), lambda d, i: (d, grp(d, i), 0))

    def per_chunk(*tail):
        return pl.BlockSpec((1, cb) + tail, lambda d, i: (d, grp(d, i)) + (0,) * len(tail))

    return tok_lat, is_ctx, tok_d, gbs, per_chunk, slot


def _dn_group_specs(cb):
    tok = pl.BlockSpec((cb * CH, D), lambda d, i: (i, 0))
    tok_d = pl.BlockSpec((1, cb * CH, D), lambda d, i: (d, i, 0))
    gbs = pl.BlockSpec((1, cb * CH, 128), lambda d, i: (d, i, 0))
    msk = pl.BlockSpec((1, 2, CH, CH), lambda d, i: (d, 0, 0, 0))

    def per_chunk(*tail):
        return pl.BlockSpec((1, cb) + tail, lambda d, i: (d, i) + (0,) * len(tail))

    return tok, tok_d, gbs, msk, per_chunk


def _dn_intra_fwd(q, k, v, gb, n_ctx_chunks, hi):
    t_all = q.shape[0]
    n_chunks = t_all // CH
    masks = _dn_masks()

    cb = DN_CB

    def put(ref, val):
        for t_ in range(cb):
            ref[0, t_] = val[t_ * NH:(t_ + 1) * NH].astype(ref.dtype)

    def body(q_ref, k_ref, v_ref, gb_ref, m_ref, u_ref, w_ref, qg_ref, kd_ref, pm_ref, t_ref):
        mi, ms = m_ref[0, 0], m_ref[0, 1]
        beta, gc, gcr, gt = _dn_scalars(gb_ref[0], mi, cb)
        q_, k_, v_ = _dn_heads(q_ref, cb), _dn_heads(k_ref, cb), _dn_heads(v_ref, cb)
        decay = jnp.exp(jnp.where(mi > 0, gc - gcr, 0.0)) * mi
        e = jnp.exp(gc)
        a = ms * (beta * _bdot(k_, k_, BNT, hi) * decay)
        t = _dn_inverse(a, hi)
        uw =_bdot(t, jnp.concatenate([beta * v_, (beta * e) * k_], axis=2), BNN, hi)
        put(u_ref, uw[:, :, :HD])
        put(w_ref, uw[:, :, HD:])
        put(qg_ref, e * q_)
        put(kd_ref, jnp.exp(gt - gc) * k_)
        put(pm_ref, _bdot(q_, k_, BNT, hi) * decay)
        put(t_ref, t)

    tok, _, gbs, msk, per_chunk = _dn_group_specs(cb)
    big = lambda dt: jax.ShapeDtypeStruct((2, n_chunks, NH, CH, HD), dt)
    sq = jax.ShapeDtypeStruct((2, n_chunks, NH, CH, CH), BF16)
    return pl.pallas_call(
        body, name="dn_intra_fwd", grid=(2, n_chunks // cb),
        in_specs=[tok, tok, tok, gbs, msk],
        out_specs=[per_chunk(NH, CH, HD)] * 4 + [per_chunk(NH, CH, CH)] * 2,
        out_shape=[big(BF16), big(BF16), big(BF16), big(BF16), sq, sq],
        compiler_params=_cparams(("parallel", "parallel")),
    )(q, k, v, gb, masks)


def _dn_seq_fwd(u, w, qg, kd, pm, gb, n_ctx_chunks, hi):
    n_chunks = u.shape[1]
    t_all = n_chunks * CH

    cb = DN_SEQ_CB
    _, _, tok_d, gbs, per_chunk, slot = _dn_specs(n_ctx_chunks, n_chunks, False, cb)

    def body(u_ref, w_ref, qg_ref, kd_ref, pm_ref, gb_ref, o_ref, sh_ref, vn_ref, s_scr):
        @pl.when(pl.program_id(1) == 0)
        def _():
            s_scr[...] = jnp.zeros_like(s_scr)

        for t in range(cb):
            j = slot(pl.program_id(0), t)
            rows = pl.ds(pl.multiple_of(j * CH, CH), CH)
            s = s_scr[...]
            sh_ref[0, j] = s.astype(sh_ref.dtype)
            vn = u_ref[0, j] - _bdot(w_ref[0, j], s, BNN, hi)
            o = _bdot(qg_ref[0, j], s, BNN, hi) + _bdot(pm_ref[0, j], vn, BNN, hi)
            s_scr[...] = jnp.exp(_dn_total(gb_ref[0, rows, :])) * s + _bdot(kd_ref[0, j], vn, BTN, hi)
            vn_ref[0, j] = vn.astype(vn_ref.dtype)
            for h in range(NH):
                o_ref[0, rows, h * HD:(h + 1) * HD] = o[h]

    big = per_chunk(NH, CH, HD)
    return pl.pallas_call(
        body, name="dn_seq_fwd", grid=(2, n_chunks // cb),
        in_specs=[big, big, big, big, per_chunk(NH, CH, CH), gbs],
        out_specs=[tok_d, per_chunk(NH, HD, HD), big],
        out_shape=[jax.ShapeDtypeStruct((2, t_all, D), F32), jax.ShapeDtypeStruct((2, n_chunks, NH, HD, HD), BF16),
                   jax.ShapeDtypeStruct((2, n_chunks, NH, CH, HD), BF16)],
        scratch_shapes=[pltpu.VMEM((NH, HD, HD), F32)],
        compiler_params=_cparams(("parallel", "arbitrary")),
    )(u, w, qg, kd, pm, gb)


def _dn_seq_bwd(w, qg, kd, pm, vn, s_hist, gb, do, n_ctx_chunks, hi):
    n_chunks = w.shape[1]

    cb = DN_SEQ_CB
    tok_lat, is_ctx, _, gbs, per_chunk, slot = _dn_specs(n_ctx_chunks, n_chunks, True, cb)

    def body(w_ref, qg_ref, kd_ref, pm_ref, vn_ref, sh_ref, gb_ref, do_ref, dvn_ref, dw_ref, dqg_ref, dkd_ref, del_ref, ds_scr):
        @pl.when(pl.program_id(1) == 0)
        def _():
            ds_scr[...] = jnp.zeros_like(ds_scr)

        for t in range(cb):
            j = slot(pl.program_id(0), t)
            rows = pl.ds(pl.multiple_of(j * CH, CH), CH)
            dsn = ds_scr[...]
            s = sh_ref[0, j]
            do_ = jnp.stack([do_ref[rows, h * HD:(h + 1) * HD] for h in range(NH)])
            do_ = jnp.where(is_ctx(pl.program_id(0), pl.program_id(1)), 0.0, do_)
            dvn =_bdot(pm_ref[0, j], do_, BTN, hi) + _bdot(kd_ref[0, j], dsn, BNN, hi)
            ds_scr[...] = (_bdot(qg_ref[0, j], do_, BTN, hi) + jnp.exp(_dn_total(gb_ref[0, rows, :])) * dsn
                           - _bdot(w_ref[0, j], dvn, BTN, hi))
            dvn_ref[0, j] = dvn.astype(dvn_ref.dtype)
            dw_ref[0, j] = (-_bdot(dvn, s, BNT, hi)).astype(dw_ref.dtype)
            dqg_ref[0, j] = _bdot(do_, s, BNT, hi).astype(dqg_ref.dtype)
            dkd_ref[0, j] = _bdot(vn_ref[0, j], dsn, BNT, hi).astype(dkd_ref.dtype)
            del_ref[0, j] = jnp.broadcast_to(jnp.sum(jnp.sum(s * dsn, axis=2, keepdims=True), axis=1, keepdims=True),
                                             (NH, 1, 128))

    big = per_chunk(NH, CH, HD)
    shp = lambda dt: jax.ShapeDtypeStruct((2, n_chunks, NH, CH, HD), dt)
    return pl.pallas_call(
        body, name="dn_seq_bwd", grid=(2, n_chunks // cb),
        in_specs=[big, big, big, per_chunk(NH, CH, CH), big, per_chunk(NH, HD, HD), gbs, tok_lat],
        out_specs=[big, big, big, big, per_chunk(NH, 1, 128)],
        out_shape=[shp(BF16), shp(BF16), shp(BF16), shp(BF16), jax.ShapeDtypeStruct((2, n_chunks, NH, 1, 128), F32)],
        scratch_shapes=[pltpu.VMEM((NH, HD, HD), F32)],
        compiler_params=_cparams(("parallel", "arbitrary")),
    )(w, qg, kd, pm, vn, s_hist, gb, do)


def _dn_intra_bwd(q, k, v, gb, u, w, t, vn, dvn, dw, dqg, dkd, de_last, do, n_ctx_chunks, hi):
    t_all = q.shape[0]
    n_chunks = t_all // CH
    masks = _dn_masks()

    cb = DN_CB
    assert n_ctx_chunks % cb == 0
    ctx_groups = n_ctx_chunks // cb

    def body(q_ref, k_ref, v_ref, gb_ref, m_ref, u_ref, w_ref, t_ref, vn_ref, dvn_ref, dw_ref, dqg_ref, dkd_ref, del_ref,
             do_ref, dq_ref, dk_ref, dv_ref, dgb_ref):
        mi, ms = m_ref[0, 0], m_ref[0, 1]
        beta, gc, gcr, gt = _dn_scalars(gb_ref[0], mi, cb)
        q_, k_, v_ = _dn_heads(q_ref, cb), _dn_heads(k_ref, cb), _dn_heads(v_ref, cb)
        do_ = jnp.where(pl.program_id(1) < ctx_groups, 0.0, _dn_heads(do_ref, cb))
        get = lambda ref: jnp.concatenate([ref[0, t_] for t_ in range(cb)], axis=0)
        decay = jnp.exp(jnp.where(mi > 0, gc - gcr, 0.0)) * mi
        e = jnp.exp(gc)
        e_last = jnp.exp(gt)
        kdfac = jnp.exp(gt - gc)
        kk = _bdot(k_, k_, BNT, hi)
        a = ms * (beta * kk * decay)
        pm = _bdot(q_, k_, BNT, hi) * decay
        kd = kdfac * k_
        dqg, dkd = get(dqg_ref), get(dkd_ref)
        dpm = _bdot(do_, get(vn_ref), BNT, hi)
        dvbkb = _bdot(get(t_ref), jnp.concatenate([get(dvn_ref), get(dw_ref)], axis=2), BTN, hi)
        dvb, dkb = dvbkb[:, :, :HD], dvbkb[:, :, HD:]
        da = -ms * _bdot(dvbkb, jnp.concatenate([get(u_ref), get(w_ref).astype(F32)], axis=2), BNT, hi)
        dqk = dpm * decay
        gm = dpm * pm + da * a
        dgc = (jnp.sum(gm, axis=2, keepdims=True)
               - _bdot3(gm, jnp.ones((cb * NH, CH, 128), F32), BTN, hi)[:, :, 0:1])
        dkk = da * (beta * decay)
        dbeta = jnp.sum(da * kk * decay, axis=2, keepdims=True)
        dk = _bdot(dkk, k_, BNN, hi) + _bdot(dkk, k_, BTN, hi) + _bdot(dqk, q_, BTN, hi)
        dq = _bdot(dqk, k_, BNN, hi) + e * dqg
        de = jnp.sum(dqg * q_, axis=2, keepdims=True)
        dv = beta * dvb
        dbeta = dbeta + jnp.sum(dvb * v_, axis=2, keepdims=True)
        skb = jnp.sum(dkb * k_, axis=2, keepdims=True)
        dk = dk + (beta * e) * dkb + kdfac * dkd
        dbeta = dbeta + e * skb
        de = de + beta * skb
        skd = jnp.sum(dkd * kd, axis=2, keepdims=True)
        dgc = dgc - skd + de * e
        dgtot = jnp.sum(skd, axis=1, keepdims=True) + get(del_ref)[:, :, 0:1] * e_last
        lane = lax.broadcasted_iota(jnp.int32, (1, 128), 1)
        for t_ in range(cb):
            rows = slice(t_ * CH, (t_ + 1) * CH)
            dbeta_all = jnp.zeros((CH, 128), F32)
            dgc_all = jnp.zeros((CH, 128), F32)
            dgtot_all = jnp.zeros((1, 128), F32)
            for h in range(NH):
                sl = slice(h * HD, (h + 1) * HD)
                b = t_ * NH + h
                dq_ref[0, rows, sl] = dq[b]
                dk_ref[0, rows, sl] = dk[b]
                dv_ref[0, rows, sl] = dv[b]
                hot_b = (lane == h).astype(F32)
                hot_g = (lane == NH + h).astype(F32)
                dbeta_all = dbeta_all + dbeta[b] * hot_b
                dgc_all = dgc_all + dgc[b] * hot_g
                dgtot_all = dgtot_all + dgtot[b] * hot_g
            dgb_ref[0, rows, :] = dbeta_all + _dot(mi, dgc_all, TN, True) + dgtot_all

    tok, tok_d, gbs, msk, per_chunk = _dn_group_specs(cb)
    tok_lat = pl.BlockSpec((cb * CH, D), lambda d, i: (jnp.maximum(i - ctx_groups, 0), 0))
    big = per_chunk(NH, CH, HD)
    return pl.pallas_call(
        body, name="dn_intra_bwd", grid=(2, n_chunks // cb),
        in_specs=[tok, tok, tok, gbs, msk, big, big, per_chunk(NH, CH, CH), big, big, big, big, big,
                  per_chunk(NH, 1, 128), tok_lat],
        out_specs=[tok_d, tok_d, tok_d, gbs],
        out_shape=[jax.ShapeDtypeStruct((2, t_all, D), F32)] * 3 + [jax.ShapeDtypeStruct((2, t_all, 128), F32)],
        compiler_params=_cparams(("parallel", "parallel")),
    )(q, k, v, gb, masks, u, w, t, vn, dvn, dw, dqg, dkd, de_last, do)


ATT_SCALE = HD ** -0.5
NEG = -1e30


def _att_stack(ref, kvh):
    return jnp.concatenate([ref[:, (kvh * GRP + g) * HD:(kvh * GRP + g + 1) * HD] for g in range(GRP)], axis=0)


def _att_col(ref, kvh):
    return jnp.concatenate([ref[:, kvh * GRP + g:kvh * GRP + g + 1] for g in range(GRP)], axis=0)


def _att_sink(sink_ref, kvh):
    return jnp.concatenate([jnp.broadcast_to(sink_ref[:, kvh * GRP + g:kvh * GRP + g + 1], (AB, 1)) for g in range(GRP)],
                           axis=0)


def _att_mask(i, nb):
    r = lax.broadcasted_iota(jnp.int32, (AB, AB), 0)
    c = lax.broadcasted_iota(jnp.int32, (AB, AB), 1)
    okp = jnp.logical_and(c >= r, i > 0)
    okn = jnp.logical_and(c <= r, i < nb - 1)
    return jnp.concatenate([okp] * GRP, axis=0), jnp.concatenate([okn] * GRP, axis=0)


def _att_masked(s, mask):
    mp, mn = mask
    return jnp.concatenate([jnp.where(mp, s[:, 0:AB], NEG), s[:, AB:2 * AB], jnp.where(mn, s[:, 2 * AB:3 * AB], NEG),
                            s[:, 3 * AB:]], axis=1)


def _att_kspecs(nb):
    nc = CTX // AB
    return [pl.BlockSpec((AB, KVH * HD), lambda i: (jnp.maximum(i - 1, 0) + nc, 0)),
            pl.BlockSpec((AB, KVH * HD), lambda i: (i + nc, 0)),
            pl.BlockSpec((AB, KVH * HD), lambda i: (jnp.minimum(i + 1, nb - 1) + nc, 0)),
            pl.BlockSpec((CTX, KVH * HD), lambda i: (0, 0))]


def _attn_fwd(qr, kr, vv, sink, hi):
    tl = qr.shape[0]
    nb = tl // AB

    def body(q_ref, kp_ref, kc_ref, kn_ref, kx_ref, vp_ref, vc_ref, vn_ref, vx_ref, sink_ref, o_ref, lse_ref):
        i = pl.program_id(0)
        mask = _att_mask(i, nb)
        lane = lax.broadcasted_iota(jnp.int32, (1, 128), 1)
        lse_all = jnp.zeros((AB, 128), F32)
        for kvh in range(KVH):
            ksl = slice(kvh * HD, (kvh + 1) * HD)
            kall = jnp.concatenate([kp_ref[:, ksl], kc_ref[:, ksl], kn_ref[:, ksl], kx_ref[:, ksl]], axis=0)
            vall = jnp.concatenate([vp_ref[:, ksl], vc_ref[:, ksl], vn_ref[:, ksl], vx_ref[:, ksl]], axis=0)
            s = _dot(_att_stack(q_ref, kvh), kall, NT, hi) * ATT_SCALE
            s = _att_masked(s, mask)
            sk = _att_sink(sink_ref, kvh)
            m = jnp.maximum(jnp.max(s, axis=1, keepdims=True), sk)
            p = jnp.exp(s - m)
            l = jnp.sum(p, axis=1, keepdims=True) + jnp.exp(sk - m)
            o = _dot(p, vall, NN, hi) / l
            lse = m + jnp.log(l)
            for g in range(GRP):
                h = kvh * GRP + g
                o_ref[:, h * HD:(h + 1) * HD] = o[g * AB:(g + 1) * AB]
                lse_all = lse_all + lse[g * AB:(g + 1) * AB] * (lane == h).astype(F32)
        lse_ref[...] = lse_all

    ks = _att_kspecs(nb)
    return pl.pallas_call(
        body, name="attn_fwd", grid=(nb,),
        in_specs=[pl.BlockSpec((AB, D), lambda i: (i, 0))] + ks + ks + [pl.BlockSpec((1, 128), lambda i: (0, 0))],
        out_specs=[pl.BlockSpec((AB, D), lambda i: (i, 0)), pl.BlockSpec((AB, 128), lambda i: (i, 0))],
        out_shape=[jax.ShapeDtypeStruct((tl, D), F32), jax.ShapeDtypeStruct((tl, 128), F32)],
        compiler_params=_cparams(("parallel",)),
    )(qr, kr, kr, kr, kr, vv, vv, vv, vv, sink)


def _mm_bat_dx_delta(dz_at, w_bat, o, hi):
    def fn(i, do_, o_):
        lane = lax.broadcasted_iota(jnp.int32, (1, 128), 1)
        acc = jnp.zeros((do_.shape[0], 128), F32)
        for h in range(NH):
            sl = slice(h * HD, (h + 1) * HD)
            acc = acc + jnp.sum(o_[:, sl] * do_[:, sl], axis=1, keepdims=True) * (lane == h).astype(F32)
        return do_, acc

    return _mm_ep("mm_bat_dx_delta", dz_at, w_bat, True, min(512, o.shape[0]), D, fn, [_In(o)], [_Out(D), _Out(128)], hi)


def _attn_bwd(qr, kr, vv, sink, do, lse, delta, hi):
    tl = qr.shape[0]
    nb = tl // AB
    nc = CTX // AB

    def body(q_ref, kp_ref, kc_ref, kn_ref, kx_ref, vp_ref, vc_ref, vn_ref, vx_ref, sink_ref, do_ref, lse_ref, dl_ref,
             dq_ref, dk_ref, dv_ref, dkx_ref, dvx_ref, dsink_ref, dk_acc, dv_acc):
        i = pl.program_id(0)

        @pl.when(i == 0)
        def _():
            dkx_ref[...] = jnp.zeros_like(dkx_ref)
            dvx_ref[...] = jnp.zeros_like(dvx_ref)
            dsink_ref[...] = jnp.zeros_like(dsink_ref)
            dk_acc[...] = jnp.zeros_like(dk_acc)
            dv_acc[...] = jnp.zeros_like(dv_acc)

        @pl.when(i < nb)
        def _():
            mask = _att_mask(i, nb)
            lane = lax.broadcasted_iota(jnp.int32, (1, 128), 1)
            s_prev, s_cur, s_next = (i + 2) % 3, i % 3, (i + 1) % 3
            dsink = jnp.zeros((1, 128), F32)
            for kvh in range(KVH):
                ksl = slice(kvh * HD, (kvh + 1) * HD)
                kall = jnp.concatenate([kp_ref[:, ksl], kc_ref[:, ksl], kn_ref[:, ksl], kx_ref[:, ksl]], axis=0)
                vall = jnp.concatenate([vp_ref[:, ksl], vc_ref[:, ksl], vn_ref[:, ksl], vx_ref[:, ksl]], axis=0)
                qs = _att_stack(q_ref, kvh)
                dos = _att_stack(do_ref, kvh)
                lse_s = _att_col(lse_ref, kvh)
                dl_s = _att_col(dl_ref, kvh)
                s = _dot(qs, kall, NT, hi) * ATT_SCALE
                p = jnp.exp(_att_masked(s, mask) - lse_s)
                dp = _dot(dos, vall, NT, hi)
                ds = p * (dp - dl_s)
                dq = _dot(ds, kall, NN, hi) * ATT_SCALE
                dk_all = _dot(ds, qs, TN, hi) * ATT_SCALE
                dv_all = _dot(p, dos, TN, hi)
                dkx_ref[:, ksl] += dk_all[3 * AB:]
                dvx_ref[:, ksl] += dv_all[3 * AB:]
                dk_acc[s_prev, :, ksl] += dk_all[0:AB]
                dv_acc[s_prev, :, ksl] += dv_all[0:AB]
                dk_acc[s_cur, :, ksl] += dk_all[AB:2 * AB]
                dv_acc[s_cur, :, ksl] += dv_all[AB:2 * AB]
                dk_acc[s_next, :, ksl] = dk_all[2 * AB:3 * AB]
                dv_acc[s_next, :, ksl] = dv_all[2 * AB:3 * AB]
                psink = jnp.exp(_att_sink(sink_ref, kvh) - lse_s) * dl_s
                for g in range(GRP):
                    h = kvh * GRP + g
                    dq_ref[:, h * HD:(h + 1) * HD] = dq[g * AB:(g + 1) * AB]
                    dsink = dsink - jnp.sum(psink[g * AB:(g + 1) * AB], axis=0, keepdims=True) * (lane == h).astype(F32)
            dsink_ref[...] += dsink

        @pl.when(i >= 1)
        def _():
            dk_ref[...] = dk_acc[(i + 2) % 3]
            dv_ref[...] = dv_acc[(i + 2) % 3]

    blk = lambda i: jnp.minimum(i, nb - 1)
    row = pl.BlockSpec((AB, D), lambda i: (blk(i), 0))
    col = pl.BlockSpec((AB, 128), lambda i: (blk(i), 0))
    ks = [pl.BlockSpec((AB, KVH * HD), lambda i: (jnp.maximum(blk(i) - 1, 0) + nc, 0)),
          pl.BlockSpec((AB, KVH * HD), lambda i: (blk(i) + nc, 0)),
          pl.BlockSpec((AB, KVH * HD), lambda i: (jnp.minimum(i + 1, nb - 1) + nc, 0)),
          pl.BlockSpec((CTX, KVH * HD), lambda i: (0, 0))]
    kv_out = pl.BlockSpec((AB, KVH * HD), lambda i: (jnp.maximum(i - 1, 0), 0))
    ctx_out = pl.BlockSpec((CTX, KVH * HD), lambda i: (0, 0))
    return pl.pallas_call(
        body, name="attn_bwd", grid=(nb + 1,),
        in_specs=[row] + ks + ks + [pl.BlockSpec((1, 128), lambda i: (0, 0)), row, col, col],
        out_specs=[row, kv_out, kv_out, ctx_out, ctx_out, pl.BlockSpec((1, 128), lambda i: (0, 0))],
        out_shape=[jax.ShapeDtypeStruct((tl, D), F32), jax.ShapeDtypeStruct((tl, KVH * HD), F32),
                   jax.ShapeDtypeStruct((tl, KVH * HD), F32), jax.ShapeDtypeStruct((CTX, KVH * HD), F32),
                   jax.ShapeDtypeStruct((CTX, KVH * HD), F32), jax.ShapeDtypeStruct((1, 128), F32)],
        scratch_shapes=[pltpu.VMEM((3, AB, KVH * HD), F32), pltpu.VMEM((3, AB, KVH * HD), F32)],
        compiler_params=_cparams(("arbitrary",)),
    )(qr, kr, kr, kr, kr, vv, vv, vv, vv, sink, do, lse, delta)


def _mm(a, b, ta=False, tb=False, out_dtype=F32, tm=512, tn=1024, tk=1024, name="mm", hi=False):
    a_parts = a.shape[0] if a.ndim == 3 else 0
    b_parts = b.shape[0] if b.ndim == 3 else 0
    assert not (a_parts and ta) and not (b_parts and tb)
    if a_parts:
        m, kd = a.shape[1], a_parts * a.shape[2]
    else:
        m, kd = (a.shape[1], a.shape[0]) if ta else a.shape
    n = b_parts * b.shape[2] if b_parts else (b.shape[0] if tb else b.shape[1])
    tm, tn, tk = min(tm, m), min(tn, n), min(tk, kd)
    assert m % tm == 0 and n % tn == 0 and kd % tk == 0, (name, m, n, kd, tm, tn, tk)
    nk = kd // tk
    dims = ((0,) if ta else (1,), (1,) if tb else (0,))

    def body(a_ref, b_ref, o_ref, *scr):
        part = _dot(a_ref[0] if a_parts else a_ref[...], b_ref[0] if b_parts else b_ref[...], dims, hi)
        if nk == 1:
            o_ref[...] = part.astype(out_dtype)
        else:
            acc = scr[0]
            kk = pl.program_id(2)

            @pl.when(kk == 0)
            def _():
                acc[...] = part

            @pl.when(kk > 0)
            def _():
                acc[...] += part

            @pl.when(kk == nk - 1)
            def _():
                o_ref[...] = acc[...].astype(out_dtype)

    a_spec = pl.BlockSpec((tk, tm), lambda i, j, k: (k, i)) if ta else pl.BlockSpec((tm, tk), lambda i, j, k: (i, k))
    b_spec = pl.BlockSpec((tn, tk), lambda i, j, k: (j, k)) if tb else pl.BlockSpec((tk, tn), lambda i, j, k: (k, j))
    if a_parts:
        per = a.shape[2] // tk
        assert per * tk == a.shape[2]
        a_spec = pl.BlockSpec((1, tm, tk), lambda i, j, k: (k // per, i, k % per))
    if b_parts:
        per_n = b.shape[2] // tn
        assert per_n * tn == b.shape[2]
        b_spec = pl.BlockSpec((1, tk, tn), lambda i, j, k: (j // per_n, k, j % per_n))
    return pl.pallas_call(
        body, name=name, grid=(m // tm, n // tn, nk),
        in_specs=[a_spec, b_spec],
        out_specs=pl.BlockSpec((tm, tn), lambda i, j, k: (i, j)),
        out_shape=jax.ShapeDtypeStruct((m, n), out_dtype),
        scratch_shapes=[] if nk == 1 else [pltpu.VMEM((tm, tn), F32)],
        compiler_params=_cparams(("parallel", "parallel", "arbitrary")),
    )(a, b)


HALO = 8


class _In:
    def __init__(self, arr, w=None, cb=0, roff=0, halo=None, ridx=None):
        self.arr, self.w, self.cb, self.roff, self.halo = arr, w or arr.shape[1], cb, roff, halo
        self.ridx = ridx or (lambda i, roff=roff: i + roff)


class _Full:
    def __init__(self, arr, w=None, cb=0):
        self.arr, self.w, self.cb = arr, w, cb


class _Out:
    def __init__(self, cols, dtype=F32, w=None, cb=0, acc=False, rows=1, roff=0, nrows=None, stack=0, into=None):
        self.cols, self.dtype, self.w, self.cb, self.acc, self.rows, self.roff, self.nrows, self.stack = (
            cols, dtype, w or cols, cb, acc, rows, roff, nrows, stack)
        self.into = into


def _alias_outs(arrays, specs, outs):
    aliases = {}
    for k, o in enumerate(outs):
        if o.into is not None:
            aliases[len(arrays)] = k
            arrays.append(o.into)
            specs.append(pl.BlockSpec(memory_space=pl.ANY))
    return aliases


def _rowcall(name, fn, nrow_tiles, tile, ins, outs, ncol=1):
    arrays, specs, kinds = [], [], []
    for x in ins:
        if isinstance(x, _Full):
            arrays.append(x.arr)
            if x.w is None:
                specs.append(pl.BlockSpec(x.arr.shape, lambda j, i: (0, 0)))
            else:
                specs.append(pl.BlockSpec((x.arr.shape[0], x.w), lambda j, i, cb=x.cb: (0, cb + j)))
            kinds.append("full")
            continue
        w, cb, roff = x.w, x.cb, x.roff
        cur = pl.BlockSpec((tile, w), lambda j, i, cb=cb, ridx=x.ridx: (ridx(i), cb + j))
        if x.halo is None:
            arrays.append(x.arr)
            specs.append(cur)
            kinds.append("tile")
        else:
            r8 = tile // HALO
            last = x.arr.shape[0] // HALO - 1
            prev = pl.BlockSpec((HALO, w), lambda j, i, cb=cb, roff=roff, r8=r8: (jnp.maximum((i + roff) * r8 - 1, 0), cb + j))
            nxt = pl.BlockSpec((HALO, w), lambda j, i, cb=cb, roff=roff, r8=r8, last=last:
                               (jnp.minimum((i + roff + 1) * r8, last), cb + j))
            arrays += [x.arr, x.arr, x.arr]
            specs += [prev, cur, nxt]
            kinds.append(("halo", x.halo))
    out_specs, out_shapes = [], []
    for o in outs:
        if o.acc:
            out_specs.append(pl.BlockSpec((o.rows, o.w), lambda j, i, cb=o.cb: (0, cb + j)))
            out_shapes.append(jax.ShapeDtypeStruct((o.rows, o.cols), o.dtype))
        elif o.stack:
            out_specs.append(pl.BlockSpec((o.stack, tile, o.w), lambda j, i, cb=o.cb: (0, i, cb + j)))
            out_shapes.append(jax.ShapeDtypeStruct((o.stack, nrow_tiles * tile, o.cols), o.dtype))
        else:
            out_specs.append(pl.BlockSpec((tile, o.w), lambda j, i, cb=o.cb, roff=o.roff: (i + roff, cb + j)))
            out_shapes.append(jax.ShapeDtypeStruct(((o.nrows or nrow_tiles * tile), o.cols), o.dtype))
    aliases = _alias_outs(arrays, specs, outs)
    n_in = len(arrays)

    def body(*refs):
        j = pl.program_id(0)
        i = pl.program_id(1)
        vals, r = [], 0
        for kind in kinds:
            if kind in ("full", "tile"):
                vals.append(refs[r][...])
                r += 1
            else:
                pok, nok = kind[1]
                p, c, n = refs[r][...], refs[r + 1][...], refs[r + 2][...]
                p = jnp.where(pok(i), p, jnp.zeros_like(p))
                n = jnp.where(nok(i), n, jnp.zeros_like(n))
                vals.append(jnp.concatenate([p, c, n], axis=0))
                r += 3
        res = fn(i, j, *vals)
        for o, ref, val in zip(outs, refs[n_in:], res):
            if o.acc:
                @pl.when(i == 0)
                def _(ref=ref, val=val, o=o):
                    ref[...] = val.astype(o.dtype)

                @pl.when(i > 0)
                def _(ref=ref, val=val, o=o):
                    ref[...] += val.astype(o.dtype)
            elif o.stack:
                for s_ in range(o.stack):
                    ref[s_] = val[s_].astype(o.dtype)
            else:
                ref[...] = val.astype(o.dtype)

    return pl.pallas_call(
        body, name=name, grid=(ncol, nrow_tiles), in_specs=specs, out_specs=out_specs, out_shape=out_shapes,
        input_output_aliases=aliases, compiler_params=_cparams(("parallel", "arbitrary")),
    )(*arrays)


def _mm_ep(name, a, b, tb, tm, tk, fn, ins, outs, hi=False):
    a_parts = a.shape[0] if a.ndim == 3 else 0
    m, kd = (a.shape[1], a_parts * a.shape[2]) if a_parts else a.shape
    n = b.shape[0] if tb else b.shape[1]
    tk = min(tk, kd)
    assert m % tm == 0 and kd % tk == 0, (name, m, kd, tm, tk)
    nk = kd // tk
    dims = ((1,), (1,) if tb else (0,))
    if a_parts:
        per = a.shape[2] // tk
        arrays, specs = [a], [pl.BlockSpec((1, tm, tk), lambda i, k: (k // per, i, k % per))]
    else:
        arrays, specs = [a], [pl.BlockSpec((tm, tk), lambda i, k: (i, k))]
    arrays.append(b)
    specs.append(pl.BlockSpec((n, tk), lambda i, k: (0, k)) if tb else pl.BlockSpec((tk, n), lambda i, k: (k, 0)))
    for x in ins:
        arrays.append(x.arr)
        if isinstance(x, _Full):
            specs.append(pl.BlockSpec(x.arr.shape, lambda i, k: (0, 0)))
        else:
            specs.append(pl.BlockSpec((tm, x.w), lambda i, k, cb=x.cb, ridx=x.ridx: (ridx(i), cb)))
    out_specs, out_shapes = [], []
    for o in outs:
        if o.acc:
            out_specs.append(pl.BlockSpec((o.rows, o.w), lambda i, k, cb=o.cb: (0, cb)))
            out_shapes.append(jax.ShapeDtypeStruct((o.rows, o.cols), o.dtype))
        else:
            out_specs.append(pl.BlockSpec((tm, o.w), lambda i, k, cb=o.cb, roff=o.roff: (i + roff, cb)))
            out_shapes.append(jax.ShapeDtypeStruct((o.nrows or m, o.cols), o.dtype))
    n_vals = len(arrays)
    aliases = _alias_outs(arrays, specs, outs)
    n_in = len(arrays)

    def body(*refs):
        i, kk = pl.program_id(0), pl.program_id(1)
        a_ref, b_ref = refs[0], refs[1]
        acc_ref = refs[-1]
        part = _dot(a_ref[0] if a_parts else a_ref[...], b_ref[...], dims, hi)

        @pl.when(kk == 0)
        def _():
            acc_ref[...] = part

        @pl.when(kk > 0)
        def _():
            acc_ref[...] += part

        @pl.when(kk == nk - 1)
        def _():
            res = fn(i, acc_ref[...], *[r[...] for r in refs[2:n_vals]])
            for o, ref, val in zip(outs, refs[n_in:-1], res):
                if o.acc:
                    @pl.when(i == 0)
                    def _(ref=ref, val=val, o=o):
                        ref[...] = val.astype(o.dtype)

                    @pl.when(i > 0)
                    def _(ref=ref, val=val, o=o):
                        ref[...] += val.astype(o.dtype)
                else:
                    ref[...] = val.astype(o.dtype)

    return pl.pallas_call(
        body, name=name, grid=(m // tm, nk), in_specs=specs, out_specs=out_specs, out_shape=out_shapes,
        scratch_shapes=[pltpu.VMEM((tm, n), F32)], input_output_aliases=aliases,
        compiler_params=_cparams(("arbitrary", "arbitrary")),
    )(*arrays)


def _shift(xe, s, tile):
    if s == 0:
        return xe[HALO:HALO + tile]
    return pltpu.roll(xe, (-s) % xe.shape[0], 0)[HALO:HALO + tile]


def _silu(x):
    return x * jax.nn.sigmoid(x)


def _dsilu(x):
    s = jax.nn.sigmoid(x)
    return s * (1.0 + x * (1.0 - s))


def _heads(x, fn):
    return jnp.concatenate([fn(h, x[:, h * HD:(h + 1) * HD]) for h in range(x.shape[1] // HD)], axis=1)


def _colsum(x):
    return jnp.sum(x, axis=0, keepdims=True)


def _rowmean(x):
    return jnp.mean(x, axis=1, keepdims=True)


def _rowsum(x):
    return jnp.sum(x, axis=1, keepdims=True)


TILE = 256
CT = CTX // TILE


def _all_halo(n_tiles):
    return (lambda i: i >= CT + 1, lambda i: jnp.logical_and(i >= CT, i < n_tiles - 1))


def _lat_halo(n_tiles):
    return (lambda i: i >= 1, lambda i: i < n_tiles - 1)


def _rms_mod(x, nm, shift, scale):
    r = lax.rsqrt(_rowmean(x * x) + EPS)
    return (x * r * nm) * (1.0 + scale) + shift


def _rms_mod_bwd(dh, x, nm, scale):
    r = lax.rsqrt(_rowmean(x * x) + EPS)
    xn = x * r
    dz = dh * (1.0 + scale)
    dxn = dz * nm
    dx = r * (dxn - xn * _rowmean(dxn * xn))
    return dx, _colsum(dz * xn), _colsum(dh), _colsum(dh * (xn * nm))


def _norm_mod(x, ctx, nm, mod_c, mod_x):
    n = (x.shape[0] + ctx.shape[0]) // TILE

    def fn(i, j, c_, x_, nm_, mc, mx):
        m = jnp.where(i < CT, mc, mx)
        return (_rms_mod(jnp.where(i < CT, c_, x_), nm_, m[0:1], m[1:2]),)

    ins = [_In(ctx, ridx=lambda i: jnp.minimum(i, CT - 1)), _In(x, ridx=lambda i: jnp.maximum(i - CT, 0)),
           _Full(nm), _Full(mod_c), _Full(mod_x)]
    return _rowcall("norm_mod", fn, n, TILE, ins, [_Out(D, BF16)])[0]


def _norm_mod_bwd(dh, xs, dres, nm, mod, roff, n):
    ins = [_In(dh, roff=roff), _In(xs), _Full(nm), _Full(mod)] + ([] if dres is None else [_In(dres)])

    def fn(i, j, dh_, x, nm_, m, *rest):
        dx, dn, dsh, dsc = _rms_mod_bwd(dh_, x, nm_, m[1:2])
        if rest:
            return (dx + rest[0], dn, dsh, dsc)
        return (dn, dsh, dsc)

    accs = [_Out(D, acc=True), _Out(D, acc=True), _Out(D, acc=True)]
    return _rowcall("norm_mod_bwd", fn, n, TILE, ins, ([] if dres is None else [_Out(D)]) + accs)


DN_Q_SCALE = HD ** -0.5


def _conv_taps(xe, w, width, rows=None):
    r = width // 2
    acc = None
    for t in range(width):
        s = t - r
        if rows is None:
            sh = xe if s == 0 else pltpu.roll(xe, (-s) % xe.shape[0], 0)
        else:
            sh = _shift(xe, s, rows)
        term = sh * w[t:t + 1]
        acc = term if acc is None else acc + term
    return acc


def _rolled(xe, width):
    r = width // 2
    return [xe if t == r else pltpu.roll(xe, (r - t) % xe.shape[0], 0) for t in range(width)]


def _conv_bwd(rolled, w, c_grad, width):
    r = width // 2
    cc = c_grad[HALO:HALO + TILE]
    dx, dws = None, []
    for t in range(width):
        term = _shift(c_grad, r - t, TILE) * w[t:t + 1]
        dx = term if dx is None else dx + term
        dws.append(_colsum(cc * rolled[t][HALO:HALO + TILE]))
    return dx, jnp.concatenate(dws + [jnp.zeros((8 - width, cc.shape[1]), F32)], axis=0)


def _silu_both(x):
    s = jax.nn.sigmoid(x)
    return x * s, s * (1.0 + x * (1.0 - s))


def _l2n(x, scale):
    rn = lax.rsqrt(_rowsum(x * x) + EPS)
    return x * (rn * scale)


def _l2n_bwd(dy, x, scale):
    rn = lax.rsqrt(_rowsum(x * x) + EPS)
    xu = x * rn
    return (scale * rn) * (dy - xu * _rowsum(dy * xu))


def _softplus(x):
    return jnp.maximum(x, 0.0) + jnp.log(1.0 + jnp.exp(-jnp.abs(x)))


def _lane_mask(lo, hi_):
    lane = lax.broadcasted_iota(jnp.int32, (1, 128), 1)
    return jnp.logical_and(lane >= lo, lane < hi_).astype(F32)


def _dn_prep(p, conv_w, gprm):
    n = p.shape[0] // TILE
    halo = _all_halo(n)

    def fn(i, j, qe, ke, ve, ba, w, gp):
        cq = _conv_taps(qe, w[:, 0:D], 5, TILE)
        ck = _conv_taps(ke, w[:, D:2 * D], 5, TILE)
        cv = _conv_taps(ve, w[:, 2 * D:3 * D], 5, TILE)
        q = _heads(_silu(cq), lambda h, x: _l2n(x, DN_Q_SCALE))
        k = _heads(_silu(ck), lambda h, x: _l2n(x, 1.0))
        v = _silu(cv)
        beta = jax.nn.sigmoid(ba)
        g = -jnp.exp(gp[0:1]) * _softplus(ba + gp[1:2])
        m0, m1 = _lane_mask(0, 8), _lane_mask(8, 16)
        gb_f = beta * m0 + pltpu.roll(g, 128 - 8, 1) * m1
        gb_b = pltpu.roll(beta, 128 - 8, 1) * m0 + pltpu.roll(g, 128 - 16, 1) * m1
        return q, k, v, gb_f, gb_b

    ins = [_In(p, D, 0, halo=halo), _In(p, D, 1, halo=halo), _In(p, D, 2, halo=halo), _In(p, 128, C_BA // 128),
           _Full(conv_w), _Full(gprm)]
    return _rowcall("dn_prep", fn, n, TILE, ins, [_Out(D), _Out(D), _Out(D), _Out(128), _Out(128)])


def _dn_prep_bwd(p, conv_w, gprm, dq2, dk2, dv2, dgb2, dk_at, dv_at, dp):
    n = p.shape[0] // TILE
    halo = _all_halo(n)

    def branch(xe, w, dye, scale):
        rolled = _rolled(xe, 5)
        c = rolled[0] * w[0:1]
        for t in range(1, 5):
            c = c + rolled[t] * w[t:t + 1]
        sx, dsilu = _silu_both(c)
        if scale is None:
            dsx = dye
        else:
            dsx = jnp.concatenate([_l2n_bwd(dye[:, h * HD:(h + 1) * HD], sx[:, h * HD:(h + 1) * HD], scale)
                                   for h in range(NH)], axis=1)
        return _conv_bwd(rolled, w, dsx * dsilu, 5)

    def fn(i, j, qe, ke, ve, ba, w, gp, dq0, dq1, dk0, dk1, dv0, dv1, dg0, dg1, dka, dva):
        dxq, dwq = branch(qe, w[:, 0:D], dq0 + dq1, DN_Q_SCALE)
        dxk, dwk = branch(ke, w[:, D:2 * D], dk0 + dk1, 1.0)
        dxv, dwv = branch(ve, w[:, 2 * D:3 * D], dv0 + dv1, None)
        m0, m1 = _lane_mask(0, 8), _lane_mask(8, 16)
        dbeta = dg0 * m0 + pltpu.roll(dg1 * m0, 8, 1)
        dg = pltpu.roll(dg0 * m1, 8, 1) + pltpu.roll(dg1 * m1, 16, 1)
        beta = jax.nn.sigmoid(ba)
        ea = jnp.exp(gp[0:1])
        z = ba + gp[1:2]
        g = -ea * _softplus(z)
        mg = _lane_mask(16, 32)
        da = dg * (-ea) * jax.nn.sigmoid(z) * mg
        dba = dbeta * beta * (1.0 - beta) * _lane_mask(0, 16) + da
        dgp = jnp.concatenate([_colsum(dg * g * mg), _colsum(da)], axis=0)
        half = jnp.concatenate([dxq, dxk, dxv, dka.astype(F32), dva.astype(F32), dba, jnp.zeros((TILE, PH - C_PAD), F32)],
                               axis=1)
        return (half, jnp.concatenate([dwq, dwk, dwv], axis=1), dgp)

    ins = [_In(p, D, 0, halo=halo), _In(p, D, 1, halo=halo), _In(p, D, 2, halo=halo), _In(p, 128, C_BA // 128),
           _Full(conv_w), _Full(gprm),
           _In(dq2, halo=halo), _In(dq2, roff=n, halo=halo), _In(dk2, halo=halo), _In(dk2, roff=n, halo=halo),
           _In(dv2, halo=halo), _In(dv2, roff=n, halo=halo), _In(dgb2), _In(dgb2, roff=n), _In(dk_at), _In(dv_at)]
    return _rowcall("dn_prep_bwd", fn, n, TILE, ins,
                    [_Out(PW, BF16, w=PH, cb=0, into=dp), _Out(3 * D, acc=True, rows=8), _Out(128, acc=True, rows=2)])


def _hnorm(x, w):
    return x * lax.rsqrt(_rowmean(x * x) + EPS) * w


def _hnorm_bwd(dy, x, w):
    r = lax.rsqrt(_rowmean(x * x) + EPS)
    xh = x * r
    dxh = dy * w
    return r * (dxh - xh * _rowmean(dxh * xh)), _colsum(dy * xh)


def _dn_gate_mm(o2, p, dn_norm, w_bdn, n_all, hi):
    n = n_all - CT

    def fn(i, j, of, ob, gt, w, wb):
        o = of + ob
        y = _heads(o, lambda h, x: _hnorm(x, w)) * _silu(gt)
        return y, _dot(y, wb, NN, hi)

    ins = [_In(o2, roff=CT), _In(o2, roff=n_all + CT), _In(p, D, C_GT // D, roff=CT), _Full(dn_norm), _Full(w_bdn)]
    return _rowcall("dn_gate_mm", fn, n, TILE, ins, [_Out(D, BF16), _Out(D)])


def _mm_bdn_dx_gate(dz_dn, w_bdn, o2, p, dn_norm, n_all, dp, hi):
    def fn(i, dy_, of, ob, gt, w):
        o = of + ob
        sg, dsg = _silu_both(gt)
        dos, dw = [], jnp.zeros((1, HD), F32)
        yn = []
        for h in range(NH):
            sl = slice(h * HD, (h + 1) * HD)
            dx, dwh = _hnorm_bwd(dy_[:, sl] * sg[:, sl], o[:, sl], w)
            dos.append(dx)
            dw = dw + dwh
            yn.append(_hnorm(o[:, sl], w))
        dgt = dy_ * jnp.concatenate(yn, axis=1) * dsg
        return jnp.concatenate(dos, axis=1), dgt, dw

    ins = [_In(o2, roff=CT), _In(o2, roff=n_all + CT), _In(p, D, C_GT // D, roff=CT), _Full(dn_norm)]
    outs = [_Out(D), _Out(PW, BF16, w=D, cb=C_GT // D, roff=CT, nrows=p.shape[0], into=dp), _Out(HD, acc=True)]
    return _mm_ep("mm_bdn_dx_gate", dz_dn, w_bdn, True, TILE, D, fn, ins, outs, hi)


def _rope_shuffle(x):
    lane = lax.broadcasted_iota(jnp.int32, (1, HD), 1)
    return jnp.where((lane % 64) < 32, pltpu.roll(x, HD - 32, 1), pltpu.roll(x, 32, 1))


def _rope(x, cos, sin):
    return x * cos + _rope_shuffle(x) * sin


def _rope_bwd(dy, cos, sin):
    return dy * cos + _rope_shuffle(dy * sin)


def _attn_prep(p, w, cos, sin, width, cb, roff, n, name):
    def fn(i, j, x, w_, c, s):
        return (_heads(x, lambda h, xh: _rope(_hnorm(xh, w_), c, s)),)

    ins = [_In(p, width, cb, roff=roff), _Full(w), _In(cos), _In(sin)]
    return _rowcall(name, fn, n, TILE, ins, [_Out(width)])[0]


def _attn_prep_bwd(dy, p, w, cos, sin, width, cb, roff, n, name, dx_out):
    def fn(i, j, dy_, x, w_, c, s):
        dxs, dw = [], jnp.zeros((1, HD), F32)
        for h in range(width // HD):
            sl = slice(h * HD, (h + 1) * HD)
            dx, dwh = _hnorm_bwd(_rope_bwd(dy_[:, sl], c, s), x[:, sl], w_)
            dxs.append(dx)
            dw = dw + dwh
        return jnp.concatenate(dxs, axis=1), dw

    ins = [_In(dy), _In(p, width, cb, roff=roff), _Full(w), _In(cos), _In(sin)]
    return _rowcall(name, fn, n, TILE, ins, [dx_out, _Out(HD, acc=True)])


def _mm_bat_merge(o_at, w_bat, z_dn, p, hi):
    def fn(i, za, zd, gd, ga):
        return za, jax.nn.sigmoid(gd) * zd + jax.nn.sigmoid(ga) * za

    ins = [_In(z_dn), _In(p, D, C_MG // D, roff=CT), _In(p, D, C_MG // D + 1, roff=CT)]
    return _mm_ep("mm_bat_merge", o_at, w_bat, False, TILE, D, fn, ins, [_Out(D), _Out(D, BF16)], hi)


def _mm_out_dx_merge(dmo, w_out, z_dn, z_at, p, hi):
    def fn(i, dm_, zd, za, gd, ga):
        sd, sa = jax.nn.sigmoid(gd), jax.nn.sigmoid(ga)
        dg = jnp.concatenate([dm_ * zd * sd * (1.0 - sd), dm_ * za * sa * (1.0 - sa)], axis=1)
        return dm_ * sd, dm_ * sa, dg

    ins = [_In(z_dn), _In(z_at), _In(p, D, C_MG // D, roff=CT), _In(p, D, C_MG // D + 1, roff=CT)]
    outs = [_Out(D, BF16), _Out(D, BF16), _Out(PW, BF16, w=2 * D, cb=C_MG // (2 * D), roff=CT, nrows=p.shape[0])]
    return _mm_ep("mm_out_dx_merge", dmo, w_out, True, TILE, D, fn, ins, outs, hi)


def _mm_out_resid(merged, w_out, x, g_a, nf, mod_f, hi):
    def fn(i, mo_, x_, ga, nf_, m):
        x1 = x_ + ga * mo_
        return mo_, x1, _rms_mod(x1, nf_, m[0:1], m[1:2])

    ins = [_In(x), _Full(g_a), _Full(nf), _Full(mod_f)]
    return _mm_ep("mm_out_resid", merged, w_out, False, min(512, x.shape[0]), D, fn, ins, [_Out(D), _Out(D), _Out(D, BF16)], hi)


def _mm_up_dx_norm(du, ffn_up, dy, x1, mo, g_a, nf, mod_f, hi):
    def fn(i, dh_, dy_, x1_, mo_, ga, nf_, m):
        dx, dn, dsh, dsc = _rms_mod_bwd(dh_, x1_, nf_, m[1:2])
        dx1 = dy_ + dx
        return dx1, ga * dx1, dn, dsh, dsc, _colsum(dx1 * mo_)

    ins = [_In(dy), _In(x1), _In(mo), _Full(g_a), _Full(nf), _Full(mod_f)]
    accs = [_Out(D, acc=True) for _ in range(4)]
    return _mm_ep("mm_up_dx_norm", du, ffn_up, True, min(512, x1.shape[0]), DFF, fn, ins, [_Out(D), _Out(D, BF16)] + accs, hi)


def _mm_down_loss(a, ffn_down, x1, tgt, g_f, hi):
    def fn(i, f_, x1_, t, gf):
        e = x1_ + gf * f_ - t
        dy = e * (1.0 / D)
        loss = _colsum(_rowsum(e * e)) * (0.5 / D)
        return dy, gf * dy, _colsum(dy * f_), jnp.broadcast_to(loss, (1, 128))

    ins = [_In(x1), _In(tgt), _Full(g_f)]
    outs = [_Out(D), _Out(D, BF16), _Out(D, acc=True), _Out(128, acc=True)]
    return _mm_ep("mm_down_loss", a, ffn_down, False, min(512, x1.shape[0]), DFF, fn, ins, outs, hi)


FW = DFF // 2


def _ffn_act(u, conv_w, conv_b, n):
    halo = _lat_halo(n)

    def fn(i, j, ge, ve, wg, wv, bg, bv):
        cg = _conv_taps(ge, wg, 3, TILE) + bg
        cv = _conv_taps(ve, wv, 3, TILE) + bv
        return (_silu(cg) * cv,)

    ins = [_In(u, FW, 0, halo=halo), _In(u, FW, 2, halo=halo), _Full(conv_w, FW, 0), _Full(conv_w, FW, 2),
           _Full(conv_b, FW, 0), _Full(conv_b, FW, 2)]
    return _rowcall("ffn_act", fn, n, TILE, ins, [_Out(DFF, BF16, FW)], ncol=2)[0]


def _ffn_act_bwd(u, da, conv_w, conv_b, n):
    halo = _lat_halo(n)

    def fn(i, j, ge, ve, dae, wg, wv, bg, bv):
        rg, rv = _rolled(ge, 3), _rolled(ve, 3)
        cg = rg[0] * wg[0:1] + rg[1] * wg[1:2] + rg[2] * wg[2:3] + bg
        cv = rv[0] * wv[0:1] + rv[1] * wv[1:2] + rv[2] * wv[2:3] + bv
        sg, dsg = _silu_both(cg)
        dcg = dae * cv * dsg
        dcv = dae * sg
        dxg, dwg = _conv_bwd(rg, wg, dcg, 3)
        dxv, dwv = _conv_bwd(rv, wv, dcv, 3)
        return (dxg, dxv), dwg, dwv, _colsum(dcg[HALO:HALO + TILE]), _colsum(dcv[HALO:HALO + TILE])

    ins = [_In(u, FW, 0, halo=halo), _In(u, FW, 2, halo=halo), _In(da, FW, 0, halo=halo),
           _Full(conv_w, FW, 0), _Full(conv_w, FW, 2), _Full(conv_b, FW, 0), _Full(conv_b, FW, 2)]
    outs = [_Out(DFF, BF16, FW, stack=2), _Out(DFF, w=FW, acc=True, rows=8), _Out(DFF, w=FW, acc=True, rows=8),
            _Out(DFF, w=FW, acc=True), _Out(DFF, w=FW, acc=True)]
    return _rowcall("ffn_act_bwd", fn, n, TILE, ins, outs, ncol=2)


def _rope_tables(tl):
    rows = tl // GRID_W
    inv = np.float32(ROPE_BASE) ** (-np.arange(32, dtype=np.float32) / np.float32(32))
    ar = np.arange(rows, dtype=np.float32)[:, None] * inv
    ac = np.arange(GRID_W, dtype=np.float32)[:, None] * inv

    def table(r, c):
        full = (rows, GRID_W, HD // 2)
        return jnp.concatenate([jnp.broadcast_to(jnp.asarray(r)[:, None, :], full),
                                jnp.broadcast_to(jnp.asarray(c)[None, :, :], full)], axis=2).reshape(tl, HD)

    two = lambda a, b: np.concatenate([a, b], axis=1).astype(np.float32)
    cos = table(two(np.cos(ar), np.cos(ar)), two(np.cos(ac), np.cos(ac)))
    sin = table(two(-np.sin(ar), np.sin(ar)), two(-np.sin(ac), np.sin(ac)))
    return cos, sin


def _pad_w_in(w_in):
    return jnp.concatenate([w_in[:, 0:3072], w_in[:, 5152:5664], w_in[:, 4096:4128], jnp.zeros((D, 96 + C_GT - C_PAD), w_in.dtype),
                            w_in[:, 3072:4096], w_in[:, 4128:5152], w_in[:, 5664:7712]], axis=1)


def _unpad_w_in(g, axis=1):
    cut = lambda a, b: lax.slice_in_dim(g, a, b, axis=axis)
    return jnp.concatenate([cut(0, 3072), cut(C_GT, C_GT + D), cut(C_BA, C_BA + 32), cut(C_QAT, C_QAT + D),
                            cut(C_KAT, C_KAT + 512), cut(C_MG, C_MG + 2 * D)], axis=axis)


def _local_step(x, ctx, tgt, mod_x, mod_c, w, hi=False):
    tl = x.shape[0]
    t_all = tl + CTX
    n_all, n = t_all // TILE, tl // TILE
    tm_all = 1280 if t_all % 1280 == 0 else TILE
    tm_lat = 1024
    mm = functools.partial(_mm, hi=hi)
    sp = lambda m: [m[:, k * D:(k + 1) * D] for k in range(6)]
    sh_a, sc_a, g_a, sh_f, sc_f, g_f = sp(mod_x)
    sh_ac, sc_ac = sp(mod_c)[:2]
    mod_ax = jnp.concatenate([sh_a, sc_a], axis=0)
    mod_ac = jnp.concatenate([sh_ac, sc_ac], axis=0)
    mod_f = jnp.concatenate([sh_f, sc_f], axis=0)
    nm, nf = w["norm_mix"], w["norm_ffn"]
    cos, sin = _rope_tables(tl)
    cos_all = jnp.concatenate([jnp.ones((CTX, HD), F32), cos], axis=0)
    sin_all = jnp.concatenate([jnp.zeros((CTX, HD), F32), sin], axis=0)
    conv_dn = jnp.concatenate([w["dn_conv"], jnp.zeros((3, 3 * D), F32)], axis=0)
    gprm = jnp.concatenate([jnp.zeros((2, 16), F32),
                            jnp.concatenate([w["dn_a_log"].reshape(1, 16), w["dn_dt_bias"].reshape(1, 16)], axis=0),
                            jnp.zeros((2, 96), F32)], axis=1)
    conv_ff = jnp.concatenate([w["ffn_conv"], jnp.zeros((5, 2 * DFF), F32)], axis=0)
    sink = jnp.concatenate([w["attn_sink"].reshape(1, NH), jnp.zeros((1, 128 - NH), F32)], axis=1)
    nct = CTX // CH

    h = _norm_mod(x, ctx, nm, mod_ac, mod_ax)
    p = mm(h, w["w_in_p"], tm=tm_all, tn=2048, name="mm_in")
    q, k, v, gb_f, gb_b = _dn_prep(p, conv_dn, gprm)
    gb = jnp.stack([gb_f, gb_b])
    dn_u, dn_w, dn_qg, dn_kd, dn_pm, dn_t = _dn_intra_fwd(q, k, v, gb, nct, hi)
    o2, s_hist, dn_vn = _dn_seq_fwd(dn_u, dn_w, dn_qg, dn_kd, dn_pm, gb, nct, hi)
    o2 = o2.reshape(2 * t_all, D)
    y_dn, z_dn = _dn_gate_mm(o2, p, w["dn_norm"], w["w_branch_dn"], n_all, hi)
    qr = _attn_prep(p, w["q_norm"], cos, sin, D, C_QAT // D, CT, n, "attn_prep_q")
    kr = _attn_prep(p, w["k_norm"], cos_all, sin_all, KVH * HD, C_KAT // (KVH * HD), 0, n_all, "attn_prep_k")
    vv = p[:, C_VAT:C_VAT + KVH * HD]
    o_at, lse = _attn_fwd(qr, kr, vv, sink, hi)
    z_at, merged = _mm_bat_merge(o_at, w["w_branch_attn"], z_dn, p, hi)
    mo, x1, h2 = _mm_out_resid(merged, w["w_out"], x, g_a, nf, mod_f, hi)
    u = mm(h2, w["ffn_up"], tm=2 * tm_lat, tn=1408, name="mm_up")
    a = _ffn_act(u, conv_ff, w["ffn_conv_b"], n)
    dy, df, dg_f, loss = _mm_down_loss(a, w["ffn_down"], x1, tgt, g_f, hi)

    g = {}
    da = mm(df, w["ffn_down"], tb=True, tm=tm_lat, tn=DFF, name="mm_down_dx")
    g["ffn_down"] = mm(a, df, ta=True, tm=1408, tn=1024, tk=2 * tm_lat, name="mm_down_dw")
    du, dcw_g, dcw_v, dcb_g, dcb_v = _ffn_act_bwd(u, da, conv_ff, w["ffn_conv_b"], n)
    g["ffn_conv"] = jnp.concatenate([dcw_g, dcw_v], axis=1)[0:3]
    g["ffn_conv_b"] = jnp.concatenate([dcb_g, dcb_v], axis=1)
    g["ffn_up"] = mm(h2, du, ta=True, tm=1024, tn=1408, tk=2 * tm_lat, name="mm_up_dw")
    dx1, dmo, g["norm_ffn"], dsh_f, dsc_f, dg_a = _mm_up_dx_norm(du, w["ffn_up"], dy, x1, mo, g_a, nf, mod_f, hi)
    g["w_out"] = mm(merged, dmo, ta=True, tm=1024, tk=2 * tm_lat, name="mm_out_dw")
    dz_dn, dz_at, dmg = _mm_out_dx_merge(dmo, w["w_out"], z_dn, z_at, p, hi)
    g["w_branch_dn"] = mm(y_dn, dz_dn, ta=True, tm=1024, tk=2 * tm_lat, name="mm_bdn_dw")
    do_at, delta = _mm_bat_dx_delta(dz_at, w["w_branch_attn"], o_at, hi)
    g["w_branch_attn"] = mm(o_at, dz_at, ta=True, tm=1024, tk=2 * tm_lat, name="mm_bat_dw")

    do_dn, dp, g["dn_norm"] = _mm_bdn_dx_gate(dz_dn, w["w_branch_dn"], o2, p, w["dn_norm"], n_all, dmg, hi)
    do_all = do_dn
    dn_dvn, dn_dw, dn_dqg, dn_dkd, dn_del = _dn_seq_bwd(dn_w, dn_qg, dn_kd, dn_pm, dn_vn, s_hist, gb, do_all, nct, hi)
    dq2, dk2, dv2, dgb2 = _dn_intra_bwd(q, k, v, gb, dn_u, dn_w, dn_t, dn_vn, dn_dvn, dn_dw, dn_dqg, dn_dkd, dn_del,
                                        do_all, nct, hi)

    dqr, dk_lat, dv_lat, dkx, dvx, dsink = _attn_bwd(qr, kr, vv, sink, do_at, lse, delta, hi)
    g["attn_sink"] = dsink[:, 0:NH]
    q_out = _Out(PW, BF16, w=D, cb=C_QAT // D, roff=CT, nrows=t_all, into=dp)
    dp, g["q_norm"] = _attn_prep_bwd(dqr, p, w["q_norm"], cos, sin, D, C_QAT // D, CT, n, "attn_prep_q_bwd", q_out)
    dkr = jnp.concatenate([dkx, dk_lat], axis=0)
    dk_at, g["k_norm"] = _attn_prep_bwd(dkr, p, w["k_norm"], cos_all, sin_all, KVH * HD, C_KAT // (KVH * HD), 0, n_all,
                                        "attn_prep_k_bwd", _Out(KVH * HD, BF16))
    dv_at = jnp.concatenate([dvx, dv_lat], axis=0).astype(BF16)

    dp, dconv, dgprm = _dn_prep_bwd(p, conv_dn, gprm, dq2.reshape(2 * t_all, D), dk2.reshape(2 * t_all, D),
                                    dv2.reshape(2 * t_all, D), dgb2.reshape(2 * t_all, 128), dk_at, dv_at, dp)
    g["dn_conv"] = dconv[0:5]
    g["dn_a_log"] = dgprm[0, 16:32].reshape(2, NH)
    g["dn_dt_bias"] = dgprm[1, 16:32].reshape(2, NH)
    dp = lax.dynamic_update_slice(dp, jnp.zeros((CTX, PH), BF16), (0, PH))
    dh = mm(dp, w["w_in_p"], tb=True, tm=tm_all, tn=1024, tk=2048, name="mm_in_dx")
    g["w_in_p"] = mm(h, dp, ta=True, tm=1024, tn=2048, tk=tm_all, name="mm_in_dw")
    dnm_c, dsh_ac, dsc_ac = _norm_mod_bwd(dh, ctx, None, nm, mod_ac, 0, CT)
    grad_x, dnm_x, dsh_a, dsc_a = _norm_mod_bwd(dh, x, dx1, nm, mod_ax, CT, n)
    g["norm_mix"] = dnm_c + dnm_x
    dmod_x = jnp.concatenate([dsh_a, dsc_a, dg_a, dsh_f, dsc_f, dg_f], axis=1)
    dmod_c = jnp.concatenate([dsh_ac, dsc_ac, jnp.zeros((1, 4 * D), F32)], axis=1)
    return loss, grad_x, g, dmod_x, dmod_c


def _sum_slots(buf, n_slots, rows, tile, name, stride=1):
    nt = rows // tile

    def fn(i, j, *vals):
        acc = vals[0]
        for v in vals[1:]:
            acc = acc + v
        return (acc,)

    ins = [_In(buf, roff=k * stride * nt) for k in range(n_slots)]
    return _rowcall(name, fn, nt, tile, ins, [_Out(buf.shape[1])])[0]


ADAM_LR, ADAM_B1, ADAM_B2, ADAM_EPS, ADAM_WD, ADAM_STEP = 0.001, 0.9, 0.999, 1e-08, 0.01, 10


def _row_tile(rows, cols):
    for t in (512, 256, 128, 64, 32, 16, 8):
        if rows % t == 0 and t * cols * 4 * 14 <= 40 * 1024 * 1024:
            return t
    return rows


def _adamw(w, g, m, v, name):
    shape = w.shape
    cols = shape[-1]
    rows = max(1, math.prod(shape[:-1]))
    tile = _row_tile(rows, cols)
    c1 = 1.0 / (1.0 - ADAM_B1 ** ADAM_STEP)
    c2 = 1.0 / (1.0 - ADAM_B2 ** ADAM_STEP)

    def fn(i, j, w_, g_, m_, v_):
        mn = ADAM_B1 * m_ + (1.0 - ADAM_B1) * g_
        vn = ADAM_B2 * v_ + (1.0 - ADAM_B2) * (g_ * g_)
        delta = -ADAM_LR * ((mn * c1) / (jnp.sqrt(vn * c2) + ADAM_EPS) + ADAM_WD * w_)
        return delta, mn, vn

    r2 = lambda a: a.reshape(rows, cols)
    outs = _rowcall(name, fn, rows // tile, tile, [_In(r2(w)), _In(r2(g)), _In(r2(m)), _In(r2(v))],
                    [_Out(cols), _Out(cols), _Out(cols)])
    return [o.reshape(shape) for o in outs]


MESH = pl.DeviceIdType.MESH
ANY = pl.BlockSpec(memory_space=pl.ANY)


def _pos():
    return lax.axis_index("x"), lax.axis_index("y"), lax.axis_index("c")


def _all_gather_many(blks, name):
    na = len(blks)

    def body(*refs):
        x_refs, out_refs = refs[:na], refs[na:2 * na]
        send_sems, recv_sems, local_sems = refs[2 * na:]
        x, y, c = _pos()
        me, sibling = (x, y, c), (x, y, 1 - c)
        chips = [(1 - x, y), (x, 1 - y), (1 - x, 1 - y)]

        def rows(a, px, py, pc):
            m_per = blks[a].shape[0]
            return out_refs[a].at[pl.ds(pl.multiple_of((4 * px + 2 * py + pc) * m_per, 8), m_per), :]

        def copy(a, k, block, to, src=None):
            return pltpu.make_async_remote_copy(
                src_ref=rows(a, *block) if src is None else src, dst_ref=rows(a, *block),
                send_sem=send_sems.at[7 * a + k], recv_sem=recv_sems.at[7 * a + k], device_id=to, device_id_type=MESH)

        every = range(na)
        mine = [pltpu.make_async_copy(x_refs[a], rows(a, *me), local_sems.at[a]) for a in every]
        for cp in mine:
            cp.start()
        first = [copy(a, 0, me, sibling, src=x_refs[a]) for a in every]
        first += [copy(a, 1 + j, me, (*chip, c), src=x_refs[a]) for j, chip in enumerate(chips) for a in every]
        for cp in first:
            cp.start()
        passed = []
        for j, chip in enumerate(chips):
            for a in every:
                copy(a, 1 + j, (*chip, c), me).wait_recv()
                passed.append(copy(a, 4 + j, (*chip, c), sibling))
                passed[-1].start()
        for a in every:
            copy(a, 0, sibling, me).wait_recv()
        for j, chip in enumerate(chips):
            for a in every:
                copy(a, 4 + j, (*chip, 1 - c), me).wait_recv()
        for cp in first + passed:
            cp.wait_send()
        for cp in mine:
            cp.wait()

    return pl.pallas_call(
        body, name=name,
        out_shape=[jax.ShapeDtypeStruct((N_DEV * b.shape[0], b.shape[1]), b.dtype) for b in blks],
        in_specs=[ANY] * na, out_specs=[ANY] * na,
        scratch_shapes=[pltpu.SemaphoreType.DMA((7 * na,)), pltpu.SemaphoreType.DMA((7 * na,)), pltpu.SemaphoreType.DMA((na,))],
        compiler_params=pltpu.CompilerParams(has_side_effects=True),
    )(*blks)


def _all_gather(blk, name):
    return _all_gather_many([blk], name)[0]


def _flip(v, bit):
    return 1 - v if bit else v


D2D_STREAMS = 8
ICI_STREAMS = 2


def _sibling_exchange(src, seg_rows, n_seg, paired, name):
    n = src.shape[1]
    per_seg = D2D_STREAMS // n_seg
    per = seg_rows // per_seg
    assert per_seg * n_seg == D2D_STREAMS and per * per_seg == seg_rows and per % 16 == 0

    def body(x_ref, out_ref, send_sems, recv_sems):
        x, y, c = _pos()
        copies = []
        for s in range(n_seg):
            base = (2 * s + (1 - c)) * seg_rows if paired else s * seg_rows
            for j in range(per_seg):
                i = s * per_seg + j
                cp = pltpu.make_async_remote_copy(
                    src_ref=x_ref.at[pl.ds(pl.multiple_of(base + j * per, 16), per), :],
                    dst_ref=out_ref.at[pl.ds(s * seg_rows + j * per, per), :],
                    send_sem=send_sems.at[i], recv_sem=recv_sems.at[i], device_id=(x, y, 1 - c), device_id_type=MESH)
                cp.start()
                copies.append(cp)
        for cp in copies:
            cp.wait_recv()
        for cp in copies:
            cp.wait_send()

    return pl.pallas_call(
        body, name=name, out_shape=jax.ShapeDtypeStruct((n_seg * seg_rows, n), src.dtype),
        in_specs=[ANY], out_specs=ANY,
        scratch_shapes=[pltpu.SemaphoreType.DMA((D2D_STREAMS,)), pltpu.SemaphoreType.DMA((D2D_STREAMS,))],
        compiler_params=pltpu.CompilerParams(has_side_effects=True),
    )(src)


def _transpose_cast(x, dtype, name):
    r, c = x.shape
    tc = 512

    def body(x_ref, o_ref):
        o_ref[...] = x_ref[...].T.astype(o_ref.dtype)

    return pl.pallas_call(
        body, name=name, grid=(c // tc,),
        in_specs=[pl.BlockSpec((r, tc), lambda j: (0, j))], out_specs=pl.BlockSpec((tc, r), lambda j: (j, 0)),
        out_shape=jax.ShapeDtypeStruct((c, r), dtype), compiler_params=_cparams(("parallel",)),
    )(x)


def _chip_exchange(buf, rows, name):
    n = buf.shape[1]
    per = rows // ICI_STREAMS
    assert per * ICI_STREAMS == rows and per % 16 == 0

    def body(x_ref, out_ref, send_sems, recv_sems):
        x, y, c = _pos()
        copies = []
        for k in range(1, 4):
            px, py = _flip(x, k & 2), _flip(y, k & 1)
            for j in range(ICI_STREAMS):
                i = (k - 1) * ICI_STREAMS + j
                cp = pltpu.make_async_remote_copy(
                    src_ref=x_ref.at[pl.ds(pl.multiple_of((2 * px + py) * rows + j * per, 16), per), :],
                    dst_ref=out_ref.at[pl.ds((k - 1) * rows + j * per, per), :],
                    send_sem=send_sems.at[i], recv_sem=recv_sems.at[i], device_id=(px, py, c), device_id_type=MESH)
                cp.start()
                copies.append(cp)
        for cp in copies:
            cp.wait_recv()
        for cp in copies:
            cp.wait_send()

    return pl.pallas_call(
        body, name=name, out_shape=jax.ShapeDtypeStruct((3 * rows, n), buf.dtype),
        in_specs=[ANY], out_specs=ANY,
        scratch_shapes=[pltpu.SemaphoreType.DMA((3 * ICI_STREAMS,)), pltpu.SemaphoreType.DMA((3 * ICI_STREAMS,))],
        compiler_params=pltpu.CompilerParams(has_side_effects=True),
    )(buf)


def _add_rows(parts, rows, dtype, name):
    tile = 1024
    ins = [_In(a, roff=r0 // tile) for a, r0 in parts]

    def fn(i, j, *vals):
        acc = vals[0].astype(F32)
        for v_ in vals[1:]:
            acc = acc + v_.astype(F32)
        return (acc,)

    return _rowcall(name, fn, rows // tile, tile, ins, [_Out(parts[0][0].shape[1], dtype)])[0]


BIG = ("w_in", "w_branch_dn", "w_branch_attn", "w_out", "ffn_up", "ffn_down")
BIG_SHARD = {"w_in": (1024, 1928, True), "w_branch_dn": (256, 1024, False), "w_branch_attn": (256, 1024, False),
             "w_out": (256, 1024, False), "ffn_up": (1024, 1408, True), "ffn_down": (704, 1024, False)}
BIG_ROWS = {k: r * c // 2 // 128 for k, (r, c, _) in BIG_SHARD.items()}
PIECE = 19456
assert sum(BIG_ROWS.values()) <= PIECE


def _gather_weights(shards, ci):
    halves = []
    for k in BIG:
        r, c, _ = BIG_SHARD[k]
        halves.append(lax.dynamic_slice_in_dim(shards[k], ci * (r // 2), r // 2, axis=0).astype(BF16))
    out = {}
    for k, ag in zip(BIG, _all_gather_many(halves, "ag_weights")):
        r, c, by_col = BIG_SHARD[k]
        blk = ag.reshape(4, r, c)
        out[k] = jnp.transpose(blk, (1, 0, 2)).reshape(r, 4 * c) if by_col else blk.reshape(4 * r, c)
    return out


def _pack_pieces(full):
    parts = [full["w_in_t"].reshape(N_DEV, BIG_ROWS["w_in"], 128).astype(BF16)]
    for k in BIG[1:]:
        r, c, by_col = BIG_SHARD[k]
        a = full[k]
        if by_col:
            a = jnp.transpose(a.reshape(r, 4, c), (1, 0, 2))
        parts.append(a.reshape(N_DEV, BIG_ROWS[k], 128).astype(BF16))
    parts.append(jnp.zeros((N_DEV, PIECE - sum(BIG_ROWS.values()), 128), BF16))
    return jnp.concatenate(parts, axis=1).reshape(N_DEV * PIECE, 128)


def _reduce_scatter(pieces, ci, shard):
    half = N_DEV // 2 * PIECE
    theirs = _sibling_exchange(pieces, PIECE, N_DEV // 2, True, "rs_d2d")
    own = lax.dynamic_index_in_dim(pieces.reshape(N_DEV // 2, 2, PIECE, 128), ci, axis=1, keepdims=False).reshape(half, 128)
    part = _add_rows([(own, 0), (theirs, 0)], half, BF16, "rs_sum_chip")
    recv = _chip_exchange(part, PIECE, "rs_ici")
    own2 = lax.dynamic_slice_in_dim(part, shard * PIECE, PIECE, axis=0)
    mine = _add_rows([(own2, 0), (recv, 0), (recv, PIECE), (recv, 2 * PIECE)], PIECE, F32, "rs_sum_all")
    other = _sibling_exchange(mine, PIECE, 1, False, "rs_pair")
    return jnp.where(ci == 0, jnp.stack([mine, other]), jnp.stack([other, mine]))


def _unpack_shard(two):
    out, off = {}, 0
    for k in BIG:
        r, c, _ = BIG_SHARD[k]
        blk = two[:, off:off + BIG_ROWS[k]]
        out[k] = blk.reshape(c, r).T if k == "w_in" else blk.reshape(r, c)
        off += BIG_ROWS[k]
    return out


SMALL = (("dn_conv", 120), ("ffn_conv", 132), ("ffn_conv_b", 44), ("norm_mix", 8), ("norm_ffn", 8), ("dn_a_log", 1),
         ("dn_dt_bias", 1), ("dn_norm", 1), ("q_norm", 1), ("k_norm", 1), ("attn_sink", 1), ("dmod_c", 48), ("dmod_x", 48))
SMALL_ROWS = 416


def _rows128(a, rows):
    flat = a.reshape(-1)
    return jnp.concatenate([flat, jnp.zeros((rows * 128 - flat.shape[0],), F32)]).reshape(rows, 128)


def _pack_small(g):
    parts = [_rows128(g[k], r) for k, r in SMALL]
    parts.append(jnp.zeros((SMALL_ROWS - sum(r for _, r in SMALL), 128), F32))
    return jnp.concatenate(parts, axis=0)


def _unpack_small(buf, shapes):
    out, off = {}, 0
    for k, r in SMALL:
        n = math.prod(shapes[k])
        out[k] = buf[off:off + r].reshape(-1)[:n].reshape(shapes[k])
        off += r
    return out


WEIGHTS = ("c_ctx", "w_ada", "b_ada", "norm_mix", "norm_ffn", "w_in", "dn_conv", "dn_a_log", "dn_dt_bias", "dn_norm",
           "q_norm", "k_norm", "attn_sink", "w_branch_dn", "w_branch_attn", "w_out", "ffn_up", "ffn_conv", "ffn_conv_b",
           "ffn_down")


def kernel(x, c, ctx, c_ctx, w_ada, b_ada, norm_mix, norm_ffn, w_in, dn_conv, dn_a_log, dn_dt_bias, dn_norm, q_norm, k_norm, attn_sink, w_branch_dn, w_branch_attn, w_out, ffn_up, ffn_conv, ffn_conv_b, ffn_down, loss_target, m_c_ctx, m_w_ada, m_b_ada, m_norm_mix, m_norm_ffn, m_w_in, m_dn_conv, m_dn_a_log, m_dn_dt_bias, m_dn_norm, m_q_norm, m_k_norm, m_attn_sink, m_w_branch_dn, m_w_branch_attn, m_w_out, m_ffn_up, m_ffn_conv, m_ffn_conv_b, m_ffn_down, v_c_ctx, v_w_ada, v_b_ada, v_norm_mix, v_norm_ffn, v_w_in, v_dn_conv, v_dn_a_log, v_dn_dt_bias, v_dn_norm, v_q_norm, v_k_norm, v_attn_sink, v_w_branch_dn, v_w_branch_attn, v_w_out, v_ffn_up, v_ffn_conv, v_ffn_conv_b, v_ffn_down):
    args = dict(locals())
    xi, yi, ci = _pos()
    dev = 4 * xi + 2 * yi + ci
    shard = 2 * xi + yi
    chips = lambda a: a[0::2]

    blk = jnp.concatenate([_rows128(c, 8), _rows128(dn_conv, 30), _rows128(ffn_conv, 33), jnp.zeros((1, 128), F32)], axis=0)
    ag = _all_gather(blk, "ag_small_in").reshape(N_DEV, 72, 128)
    c_all = ag[:, 0:8].reshape(N_DEV, D)
    dn_conv_full = jnp.transpose(chips(ag)[:, 8:38].reshape(4, 5, 768), (1, 0, 2)).reshape(5, 3 * D)
    ffn_conv_full = jnp.transpose(chips(ag)[:, 38:71].reshape(4, 3, 1408), (1, 0, 2)).reshape(3, 2 * DFF)

    c16 = jnp.concatenate([c_all, c_ctx[None], jnp.zeros((7, D), F32)], axis=0)
    a16 = _rowcall("ada_silu", lambda i, j, v: (_silu(v),), 1, 16, [_In(c16)], [_Out(D)])[0]
    m_sh = _mm(a16, w_ada[0], tm=16, tn=512, tk=D, name="ada_fwd", hi=True)
    mod16 = chips(_all_gather(m_sh, "ag_mod").reshape(N_DEV, 16, 1536))
    mod16 = jnp.transpose(mod16, (1, 0, 2)).reshape(16, 6 * D) + b_ada
    mod_x = lax.dynamic_slice_in_dim(mod16, dev, 1, axis=0)
    mod_c = mod16[8:9]

    shards = {k: args[k][0] for k in BIG}
    wfull = _gather_weights(shards, ci)
    w = dict(wfull)
    w["w_in_p"] = _pad_w_in(wfull["w_in"])
    w.update(norm_mix=norm_mix, norm_ffn=norm_ffn, dn_conv=dn_conv_full, dn_a_log=dn_a_log[0], dn_dt_bias=dn_dt_bias[0],
             dn_norm=dn_norm, q_norm=q_norm, k_norm=k_norm, attn_sink=attn_sink, ffn_conv=ffn_conv_full, ffn_conv_b=ffn_conv_b)

    loss_part, grad_x, g, dmod_x, dmod_c = _local_step(x[0], ctx[0], loss_target[0], mod_x, mod_c, w)
    loss = lax.psum(loss_part[0, 0], ("x", "y", "c"))

    g["w_in_t"] = _unpad_w_in(_transpose_cast(g["w_in_p"], BF16, "w_in_grad_t"), axis=0)
    gshard = _unpack_shard(_reduce_scatter(_pack_pieces(g), ci, shard))

    g["dmod_c"], g["dmod_x"] = dmod_c, dmod_x
    ag_s = _all_gather(_pack_small(g), "ag_small_grads")
    shapes = {k: g[k].shape for k, _ in SMALL}
    gs = _unpack_small(_sum_slots(ag_s, N_DEV, SMALL_ROWS, SMALL_ROWS, "small_sum"), shapes)
    dx_all = ag_s.reshape(N_DEV, SMALL_ROWS, 128)[:, SMALL_ROWS - 50:SMALL_ROWS - 2].reshape(N_DEV, 6 * D)

    d16 = jnp.concatenate([dx_all, gs["dmod_c"], jnp.zeros((7, 6 * D), F32)], axis=0)
    d16_sh = lax.dynamic_slice_in_dim(d16, shard * 1536, 1536, axis=1)
    g_w_ada = _mm(a16, d16_sh, ta=True, tm=D, tn=512, tk=16, name="ada_dw", hi=True)
    g_b_ada = _rowcall("ada_db", lambda i, j, v: (_colsum(v),), 1, 16, [_In(d16)], [_Out(6 * D, acc=True)])[0]
    da_part = _mm(d16_sh, w_ada[0], tb=True, tm=16, tn=D, tk=512, name="ada_dx", hi=True)
    da_all = _all_gather(da_part, "ag_ada_dx")
    da16 = _sum_slots(da_all, 4, 16, 16, "ada_dx_sum", stride=2)
    dc16 = _rowcall("ada_dsilu", lambda i, j, d_, v: (d_ * _dsilu(v),), 1, 16, [_In(da16), _In(c16)], [_Out(D)])[0]

    grads = {
        "c_ctx": dc16[8], "w_ada": g_w_ada[None], "b_ada": g_b_ada, "norm_mix": gs["norm_mix"], "norm_ffn": gs["norm_ffn"],
        "w_in": gshard["w_in"][None],
        "dn_conv": lax.dynamic_slice_in_dim(gs["dn_conv"], shard * 768, 768, axis=1)[None],
        "dn_a_log": gs["dn_a_log"][None], "dn_dt_bias": gs["dn_dt_bias"][None], "dn_norm": gs["dn_norm"],
        "q_norm": gs["q_norm"], "k_norm": gs["k_norm"], "attn_sink": gs["attn_sink"],
        "w_branch_dn": gshard["w_branch_dn"][None], "w_branch_attn": gshard["w_branch_attn"][None],
        "w_out": gshard["w_out"][None], "ffn_up": gshard["ffn_up"][None],
        "ffn_conv": lax.dynamic_slice_in_dim(gs["ffn_conv"], shard * 1408, 1408, axis=1)[None],
        "ffn_conv_b": gs["ffn_conv_b"], "ffn_down": gshard["ffn_down"][None],
    }
    deltas, new_m, new_v = [], [], []
    for k in WEIGHTS:
        d_, m_, v_ = _adamw(args[k], grads[k], args["m_" + k], args["v_" + k], "adamw_" + k)
        deltas.append(d_)
        new_m.append(m_)
        new_v.append(v_)
    return (loss, grad_x[None], *[grads[k] for k in WEIGHTS], *deltas, *new_m, *new_v)
```

```python
import functools
import math

import numpy as np
import jax
import jax.numpy as jnp
from jax import lax
from jax.experimental import pallas as pl
from jax.experimental.pallas import tpu as pltpu

F32 = jnp.float32
BF16 = jnp.bfloat16
HI = lax.Precision.HIGHEST

D = 1024
NH = 8
HD = 128
CH = 64
CTX = 256
AB = 128
KVH = 2
GRP = 4
DFF = 2816
EPS = 1e-6
GRID_W = 64
ROPE_BASE = 10000.0
N_DEV = 8
VMEM_LIMIT = 56 * 1024 * 1024

C_QKV, C_KAT, C_VAT, C_BA, C_PAD, C_GT, C_QAT, C_MG = 0, 3072, 3328, 3584, 3712, 4096, 5120, 6144
PW = 8192
PH = PW // 2


def _cparams(sem=None, **kw):
    return pltpu.CompilerParams(dimension_semantics=sem, vmem_limit_bytes=VMEM_LIMIT, **kw)


def _dot(a, b, dims, hi):
    if hi:
        return lax.dot_general(a.astype(F32), b.astype(F32), (dims, ((), ())), precision=HI, preferred_element_type=F32)
    return lax.dot_general(a.astype(BF16), b.astype(BF16), (dims, ((), ())), preferred_element_type=F32)


NN = ((1,), (0,))
NT = ((1,), (1,))
TN = ((0,), (0,))


def _dn_masks():
    i = np.arange(CH)
    lo_incl = (i[:, None] >= i[None, :]).astype(np.float32)
    lo_strict = (i[:, None] > i[None, :]).astype(np.float32)
    return jnp.asarray(np.stack([np.stack([lo_incl, lo_strict]), np.stack([lo_incl.T, lo_strict.T])]))


def _dn_chunk_index(d, i, n_ctx_chunks, n_chunks):
    fwd = i
    bwd = jnp.where(i < n_ctx_chunks, n_ctx_chunks - 1 - i, n_chunks - 1 + n_ctx_chunks - i)
    return jnp.where(d == 0, fwd, bwd)


BNN = ((2,), (1,))
BNT = ((2,), (2,))
BTN = ((1,), (1,))


def _bdot(a, b, dims, hi):
    dn = (dims, ((0,), (0,)))
    if hi:
        return lax.dot_general(a.astype(F32), b.astype(F32), dn, precision=HI, preferred_element_type=F32)
    return lax.dot_general(a.astype(BF16), b.astype(BF16), dn, preferred_element_type=F32)


def _bdot3(a, b, dims, hi):
    if hi:
        return _bdot(a, b, dims, True)
    ah, bh = a.astype(BF16), b.astype(BF16)
    al, bl = (a - ah.astype(F32)).astype(BF16), (b - bh.astype(F32)).astype(BF16)
    dn = (dims, ((0,), (0,)))
    d = lambda x_, y_: lax.dot_general(x_, y_, dn, preferred_element_type=F32)
    return d(ah, bh) + d(ah, bl) + d(al, bh)


DN_CB = 4
DN_SEQ_CB = 2


def _dn_heads(ref, cb=1):
    return jnp.stack([ref[t * CH:(t + 1) * CH, h * HD:(h + 1) * HD] for t in range(cb) for h in range(NH)])


def _dn_scalars(gb, mi, cb=1):
    beta, gc, gcr, gt = [], [], [], []
    for t in range(cb):
        g1 = gb[t * CH:(t + 1) * CH]
        gcum, gcum_t, gtot = _dn_gcum(g1, mi)
        beta += [g1[:, h:h + 1] for h in range(NH)]
        gc += [gcum[:, NH + h:NH + h + 1] for h in range(NH)]
        gcr += [gcum_t[NH + h:NH + h + 1, :] for h in range(NH)]
        gt += [gtot[:, NH + h:NH + h + 1] for h in range(NH)]
    return jnp.stack(beta), jnp.stack(gc), jnp.stack(gcr), jnp.stack(gt)


DN_NEWTON = 1


def _dn_inverse(a, hi):
    eye = (lax.broadcasted_iota(jnp.int32, (CH, CH), 0) == lax.broadcasted_iota(jnp.int32, (CH, CH), 1)).astype(F32)
    x = -a
    t = eye + x
    p = x
    if hi:
        for _ in range(5):
            p = _bdot(p, p, BNN, True)
            t = t + _bdot(t, p, BNN, True)
        return t
    for _ in range(5):
        p = _bdot(p, p, BNN, False)
        t = t + _bdot(t, p, BNN, False)
    for _ in range(DN_NEWTON):
        r = eye - t - _bdot3(a, t, BNN, False)
        t = t + _bdot(t, r, BNN, False)
    return t


def _dn_total(gb):
    gtot = jnp.sum(gb, axis=0, keepdims=True)
    return jnp.stack([gtot[:, NH + h:NH + h + 1] for h in range(NH)])


def _dn_gcum(gb, mi):
    gcum = _dot(mi, gb, NN, True)
    gtot = jnp.sum(gb, axis=0, keepdims=True)
    return gcum, gcum.T, gtot


def _dn_specs(n_ctx_chunks, n_chunks, reverse, cb):
    assert n_ctx_chunks % cb == 0 and n_chunks % cb == 0

    def grp(d, i):
        first = n_chunks - 1 - cb * i if reverse else cb * i
        return _dn_chunk_index(d, first, n_ctx_chunks, n_chunks) // cb

    def slot(d, t):
        ascending = (d == 1) if reverse else (d == 0)
        return jnp.where(ascending, t, cb - 1 - t)

    ctx_groups = n_ctx_chunks // cb
    tok_lat = pl.BlockSpec((cb * CH, D), lambda d, i: (jnp.maximum(grp(d, i) - ctx_groups, 0), 0))
    is_ctx = lambda d, i: grp(d, i) < ctx_groups
    tok_d = pl.BlockSpec((1, cb * CH, D), lambda d, i: (d, grp(d, i), 0))
    gbs = pl.BlockSpec((1, cb * CH, 128), lambda d, i: (d, grp(d, i), 0))

    def per_chunk(*tail):
        return pl.BlockSpec((1, cb) + tail, lambda d, i: (d, grp(d, i)) + (0,) * len(tail))

    return tok_lat, is_ctx, tok_d, gbs, per_chunk, slot


def _dn_group_specs(cb):
    tok = pl.BlockSpec((cb * CH, D), lambda d, i: (i, 0))
    tok_d = pl.BlockSpec((1, cb * CH, D), lambda d, i: (d, i, 0))
    gbs = pl.BlockSpec((1, cb * CH, 128), lambda d, i: (d, i, 0))
    msk = pl.BlockSpec((1, 2, CH, CH), lambda d, i: (d, 0, 0, 0))

    def per_chunk(*tail):
        return pl.BlockSpec((1, cb) + tail, lambda d, i: (d, i) + (0,) * len(tail))

    return tok, tok_d, gbs, msk, per_chunk


def _dn_intra_fwd(q, k, v, gb, n_ctx_chunks, hi):
    t_all = q.shape[0]
    n_chunks = t_all // CH
    masks = _dn_masks()

    cb = DN_CB

    def put(ref, val):
        for t_ in range(cb):
            ref[0, t_] = val[t_ * NH:(t_ + 1) * NH].astype(ref.dtype)

    def body(q_ref, k_ref, v_ref, gb_ref, m_ref, u_ref, w_ref, qg_ref, kd_ref, pm_ref, t_ref):
        mi, ms = m_ref[0, 0], m_ref[0, 1]
        beta, gc, gcr, gt = _dn_scalars(gb_ref[0], mi, cb)
        q_, k_, v_ = _dn_heads(q_ref, cb), _dn_heads(k_ref, cb), _dn_heads(v_ref, cb)
        decay = jnp.exp(jnp.where(mi > 0, gc - gcr, 0.0)) * mi
        e = jnp.exp(gc)
        a = ms * (beta * _bdot(k_, k_, BNT, hi) * decay)
        t = _dn_inverse(a, hi)
        uw =_bdot(t, jnp.concatenate([beta * v_, (beta * e) * k_], axis=2), BNN, hi)
        put(u_ref, uw[:, :, :HD])
        put(w_ref, uw[:, :, HD:])
        put(qg_ref, e * q_)
        put(kd_ref, jnp.exp(gt - gc) * k_)
        put(pm_ref, _bdot(q_, k_, BNT, hi) * decay)
        put(t_ref, t)

    tok, _, gbs, msk, per_chunk = _dn_group_specs(cb)
    big = lambda dt: jax.ShapeDtypeStruct((2, n_chunks, NH, CH, HD), dt)
    sq = jax.ShapeDtypeStruct((2, n_chunks, NH, CH, CH), BF16)
    return pl.pallas_call(
        body, name="dn_intra_fwd", grid=(2, n_chunks // cb),
        in_specs=[tok, tok, tok, gbs, msk],
        out_specs=[per_chunk(NH, CH, HD)] * 4 + [per_chunk(NH, CH, CH)] * 2,
        out_shape=[big(BF16), big(BF16), big(BF16), big(BF16), sq, sq],
        compiler_params=_cparams(("parallel", "parallel")),
    )(q, k, v, gb, masks)


def _dn_seq_fwd(u, w, qg, kd, pm, gb, n_ctx_chunks, hi):
    n_chunks = u.shape[1]
    t_all = n_chunks * CH

    cb = DN_SEQ_CB
    _, _, tok_d, gbs, per_chunk, slot = _dn_specs(n_ctx_chunks, n_chunks, False, cb)

    def body(u_ref, w_ref, qg_ref, kd_ref, pm_ref, gb_ref, o_ref, sh_ref, vn_ref, s_scr):
        @pl.when(pl.program_id(1) == 0)
        def _():
            s_scr[...] = jnp.zeros_like(s_scr)

        for t in range(cb):
            j = slot(pl.program_id(0), t)
            rows = pl.ds(pl.multiple_of(j * CH, CH), CH)
            s = s_scr[...]
            sh_ref[0, j] = s.astype(sh_ref.dtype)
            vn = u_ref[0, j] - _bdot(w_ref[0, j], s, BNN, hi)
            o = _bdot(qg_ref[0, j], s, BNN, hi) + _bdot(pm_ref[0, j], vn, BNN, hi)
            s_scr[...] = jnp.exp(_dn_total(gb_ref[0, rows, :])) * s + _bdot(kd_ref[0, j], vn, BTN, hi)
            vn_ref[0, j] = vn.astype(vn_ref.dtype)
            for h in range(NH):
                o_ref[0, rows, h * HD:(h + 1) * HD] = o[h]

    big = per_chunk(NH, CH, HD)
    return pl.pallas_call(
        body, name="dn_seq_fwd", grid=(2, n_chunks // cb),
        in_specs=[big, big, big, big, per_chunk(NH, CH, CH), gbs],
        out_specs=[tok_d, per_chunk(NH, HD, HD), big],
        out_shape=[jax.ShapeDtypeStruct((2, t_all, D), F32), jax.ShapeDtypeStruct((2, n_chunks, NH, HD, HD), BF16),
                   jax.ShapeDtypeStruct((2, n_chunks, NH, CH, HD), BF16)],
        scratch_shapes=[pltpu.VMEM((NH, HD, HD), F32)],
        compiler_params=_cparams(("parallel", "arbitrary")),
    )(u, w, qg, kd, pm, gb)


def _dn_seq_bwd(w, qg, kd, pm, vn, s_hist, gb, do, n_ctx_chunks, hi):
    n_chunks = w.shape[1]

    cb = DN_SEQ_CB
    tok_lat, is_ctx, _, gbs, per_chunk, slot = _dn_specs(n_ctx_chunks, n_chunks, True, cb)

    def body(w_ref, qg_ref, kd_ref, pm_ref, vn_ref, sh_ref, gb_ref, do_ref, dvn_ref, dw_ref, dqg_ref, dkd_ref, del_ref, ds_scr):
        @pl.when(pl.program_id(1) == 0)
        def _():
            ds_scr[...] = jnp.zeros_like(ds_scr)

        for t in range(cb):
            j = slot(pl.program_id(0), t)
            rows = pl.ds(pl.multiple_of(j * CH, CH), CH)
            dsn = ds_scr[...]
            s = sh_ref[0, j]
            do_ = jnp.stack([do_ref[rows, h * HD:(h + 1) * HD] for h in range(NH)])
            do_ = jnp.where(is_ctx(pl.program_id(0), pl.program_id(1)), 0.0, do_)
            dvn =_bdot(pm_ref[0, j], do_, BTN, hi) + _bdot(kd_ref[0, j], dsn, BNN, hi)
            ds_scr[...] = (_bdot(qg_ref[0, j], do_, BTN, hi) + jnp.exp(_dn_total(gb_ref[0, rows, :])) * dsn
                           - _bdot(w_ref[0, j], dvn, BTN, hi))
            dvn_ref[0, j] = dvn.astype(dvn_ref.dtype)
            dw_ref[0, j] = (-_bdot(dvn, s, BNT, hi)).astype(dw_ref.dtype)
            dqg_ref[0, j] = _bdot(do_, s, BNT, hi).astype(dqg_ref.dtype)
            dkd_ref[0, j] = _bdot(vn_ref[0, j], dsn, BNT, hi).astype(dkd_ref.dtype)
            del_ref[0, j] = jnp.broadcast_to(jnp.sum(jnp.sum(s * dsn, axis=2, keepdims=True), axis=1, keepdims=True),
                                             (NH, 1, 128))

    big = per_chunk(NH, CH, HD)
    shp = lambda dt: jax.ShapeDtypeStruct((2, n_chunks, NH, CH, HD), dt)
    return pl.pallas_call(
        body, name="dn_seq_bwd", grid=(2, n_chunks // cb),
        in_specs=[big, big, big, per_chunk(NH, CH, CH), big, per_chunk(NH, HD, HD), gbs, tok_lat],
        out_specs=[big, big, big, big, per_chunk(NH, 1, 128)],
        out_shape=[shp(BF16), shp(BF16), shp(BF16), shp(BF16), jax.ShapeDtypeStruct((2, n_chunks, NH, 1, 128), F32)],
        scratch_shapes=[pltpu.VMEM((NH, HD, HD), F32)],
        compiler_params=_cparams(("parallel", "arbitrary")),
    )(w, qg, kd, pm, vn, s_hist, gb, do)


def _dn_intra_bwd(q, k, v, gb, u, w, t, vn, dvn, dw, dqg, dkd, de_last, do, n_ctx_chunks, hi):
    t_all = q.shape[0]
    n_chunks = t_all // CH
    masks = _dn_masks()

    cb = DN_CB
    assert n_ctx_chunks % cb == 0
    ctx_groups = n_ctx_chunks // cb

    def body(q_ref, k_ref, v_ref, gb_ref, m_ref, u_ref, w_ref, t_ref, vn_ref, dvn_ref, dw_ref, dqg_ref, dkd_ref, del_ref,
             do_ref, dq_ref, dk_ref, dv_ref, dgb_ref):
        mi, ms = m_ref[0, 0], m_ref[0, 1]
        beta, gc, gcr, gt = _dn_scalars(gb_ref[0], mi, cb)
        q_, k_, v_ = _dn_heads(q_ref, cb), _dn_heads(k_ref, cb), _dn_heads(v_ref, cb)
        do_ = jnp.where(pl.program_id(1) < ctx_groups, 0.0, _dn_heads(do_ref, cb))
        get = lambda ref: jnp.concatenate([ref[0, t_] for t_ in range(cb)], axis=0)
        decay = jnp.exp(jnp.where(mi > 0, gc - gcr, 0.0)) * mi
        e = jnp.exp(gc)
        e_last = jnp.exp(gt)
        kdfac = jnp.exp(gt - gc)
        kk = _bdot(k_, k_, BNT, hi)
        a = ms * (beta * kk * decay)
        pm = _bdot(q_, k_, BNT, hi) * decay
        kd = kdfac * k_
        dqg, dkd = get(dqg_ref), get(dkd_ref)
        dpm = _bdot(do_, get(vn_ref), BNT, hi)
        dvbkb = _bdot(get(t_ref), jnp.concatenate([get(dvn_ref), get(dw_ref)], axis=2), BTN, hi)
        dvb, dkb = dvbkb[:, :, :HD], dvbkb[:, :, HD:]
        da = -ms * _bdot(dvbkb, jnp.concatenate([get(u_ref), get(w_ref).astype(F32)], axis=2), BNT, hi)
        dqk = dpm * decay
        gm = dpm * pm + da * a
        dgc = (jnp.sum(gm, axis=2, keepdims=True)
               - _bdot3(gm, jnp.ones((cb * NH, CH, 128), F32), BTN, hi)[:, :, 0:1])
        dkk = da * (beta * decay)
        dbeta = jnp.sum(da * kk * decay, axis=2, keepdims=True)
        dk = _bdot(dkk, k_, BNN, hi) + _bdot(dkk, k_, BTN, hi) + _bdot(dqk, q_, BTN, hi)
        dq = _bdot(dqk, k_, BNN, hi) + e * dqg
        de = jnp.sum(dqg * q_, axis=2, keepdims=True)
        dv = beta * dvb
        dbeta = dbeta + jnp.sum(dvb * v_, axis=2, keepdims=True)
        skb = jnp.sum(dkb * k_, axis=2, keepdims=True)
        dk = dk + (beta * e) * dkb + kdfac * dkd
        dbeta = dbeta + e * skb
        de = de + beta * skb
        skd = jnp.sum(dkd * kd, axis=2, keepdims=True)
        dgc = dgc - skd + de * e
        dgtot = jnp.sum(skd, axis=1, keepdims=True) + get(del_ref)[:, :, 0:1] * e_last
        lane = lax.broadcasted_iota(jnp.int32, (1, 128), 1)
        for t_ in range(cb):
            rows = slice(t_ * CH, (t_ + 1) * CH)
            dbeta_all = jnp.zeros((CH, 128), F32)
            dgc_all = jnp.zeros((CH, 128), F32)
            dgtot_all = jnp.zeros((1, 128), F32)
            for h in range(NH):
                sl = slice(h * HD, (h + 1) * HD)
                b = t_ * NH + h
                dq_ref[0, rows, sl] = dq[b]
                dk_ref[0, rows, sl] = dk[b]
                dv_ref[0, rows, sl] = dv[b]
                hot_b = (lane == h).astype(F32)
                hot_g = (lane == NH + h).astype(F32)
                dbeta_all = dbeta_all + dbeta[b] * hot_b
                dgc_all = dgc_all + dgc[b] * hot_g
                dgtot_all = dgtot_all + dgtot[b] * hot_g
            dgb_ref[0, rows, :] = dbeta_all + _dot(mi, dgc_all, TN, True) + dgtot_all

    tok, tok_d, gbs, msk, per_chunk = _dn_group_specs(cb)
    tok_lat = pl.BlockSpec((cb * CH, D), lambda d, i: (jnp.maximum(i - ctx_groups, 0), 0))
    big = per_chunk(NH, CH, HD)
    return pl.pallas_call(
        body, name="dn_intra_bwd", grid=(2, n_chunks // cb),
        in_specs=[tok, tok, tok, gbs, msk, big, big, per_chunk(NH, CH, CH), big, big, big, big, big,
                  per_chunk(NH, 1, 128), tok_lat],
        out_specs=[tok_d, tok_d, tok_d, gbs],
        out_shape=[jax.ShapeDtypeStruct((2, t_all, D), F32)] * 3 + [jax.ShapeDtypeStruct((2, t_all, 128), F32)],
        compiler_params=_cparams(("parallel", "parallel")),
    )(q, k, v, gb, masks, u, w, t, vn, dvn, dw, dqg, dkd, de_last, do)


ATT_SCALE = HD ** -0.5
NEG = -1e30


def _att_stack(ref, kvh):
    return jnp.concatenate([ref[:, (kvh * GRP + g) * HD:(kvh * GRP + g + 1) * HD] for g in range(GRP)], axis=0)


def _att_col(ref, kvh):
    return jnp.concatenate([ref[:, kvh * GRP + g:kvh * GRP + g + 1] for g in range(GRP)], axis=0)


def _att_sink(sink_ref, kvh):
    return jnp.concatenate([jnp.broadcast_to(sink_ref[:, kvh * GRP + g:kvh * GRP + g + 1], (AB, 1)) for g in range(GRP)],
                           axis=0)


def _att_mask(i, nb):
    r = lax.broadcasted_iota(jnp.int32, (AB, AB), 0)
    c = lax.broadcasted_iota(jnp.int32, (AB, AB), 1)
    okp = jnp.logical_and(c >= r, i > 0)
    okn = jnp.logical_and(c <= r, i < nb - 1)
    return jnp.concatenate([okp] * GRP, axis=0), jnp.concatenate([okn] * GRP, axis=0)


def _att_masked(s, mask):
    mp, mn = mask
    return jnp.concatenate([jnp.where(mp, s[:, 0:AB], NEG), s[:, AB:2 * AB], jnp.where(mn, s[:, 2 * AB:3 * AB], NEG),
                            s[:, 3 * AB:]], axis=1)


def _att_kspecs(nb):
    nc = CTX // AB
    return [pl.BlockSpec((AB, KVH * HD), lambda i: (jnp.maximum(i - 1, 0) + nc, 0)),
            pl.BlockSpec((AB, KVH * HD), lambda i: (i + nc, 0)),
            pl.BlockSpec((AB, KVH * HD), lambda i: (jnp.minimum(i + 1, nb - 1) + nc, 0)),
            pl.BlockSpec((CTX, KVH * HD), lambda i: (0, 0))]


def _attn_fwd(qr, kr, vv, sink, hi):
    tl = qr.shape[0]
    nb = tl // AB

    def body(q_ref, kp_ref, kc_ref, kn_ref, kx_ref, vp_ref, vc_ref, vn_ref, vx_ref, sink_ref, o_ref, lse_ref):
        i = pl.program_id(0)
        mask = _att_mask(i, nb)
        lane = lax.broadcasted_iota(jnp.int32, (1, 128), 1)
        lse_all = jnp.zeros((AB, 128), F32)
        for kvh in range(KVH):
            ksl = slice(kvh * HD, (kvh + 1) * HD)
            kall = jnp.concatenate([kp_ref[:, ksl], kc_ref[:, ksl], kn_ref[:, ksl], kx_ref[:, ksl]], axis=0)
            vall = jnp.concatenate([vp_ref[:, ksl], vc_ref[:, ksl], vn_ref[:, ksl], vx_ref[:, ksl]], axis=0)
            s = _dot(_att_stack(q_ref, kvh), kall, NT, hi) * ATT_SCALE
            s = _att_masked(s, mask)
            sk = _att_sink(sink_ref, kvh)
            m = jnp.maximum(jnp.max(s, axis=1, keepdims=True), sk)
            p = jnp.exp(s - m)
            l = jnp.sum(p, axis=1, keepdims=True) + jnp.exp(sk - m)
            o = _dot(p, vall, NN, hi) / l
            lse = m + jnp.log(l)
            for g in range(GRP):
                h = kvh * GRP + g
                o_ref[:, h * HD:(h + 1) * HD] = o[g * AB:(g + 1) * AB]
                lse_all = lse_all + lse[g * AB:(g + 1) * AB] * (lane == h).astype(F32)
        lse_ref[...] = lse_all

    ks = _att_kspecs(nb)
    return pl.pallas_call(
        body, name="attn_fwd", grid=(nb,),
        in_specs=[pl.BlockSpec((AB, D), lambda i: (i, 0))] + ks + ks + [pl.BlockSpec((1, 128), lambda i: (0, 0))],
        out_specs=[pl.BlockSpec((AB, D), lambda i: (i, 0)), pl.BlockSpec((AB, 128), lambda i: (i, 0))],
        out_shape=[jax.ShapeDtypeStruct((tl, D), F32), jax.ShapeDtypeStruct((tl, 128), F32)],
        compiler_params=_cparams(("parallel",)),
    )(qr, kr, kr, kr, kr, vv, vv, vv, vv, sink)


def _mm_bat_dx_delta(dz_at, w_bat, o, hi):
    def fn(i, do_, o_):
        lane = lax.broadcasted_iota(jnp.int32, (1, 128), 1)
        acc = jnp.zeros((do_.shape[0], 128), F32)
        for h in range(NH):
            sl = slice(h * HD, (h + 1) * HD)
            acc = acc + jnp.sum(o_[:, sl] * do_[:, sl], axis=1, keepdims=True) * (lane == h).astype(F32)
        return do_, acc

    return _mm_ep("mm_bat_dx_delta", dz_at, w_bat, True, min(512, o.shape[0]), D, fn, [_In(o)], [_Out(D), _Out(128)], hi)


def _attn_bwd(qr, kr, vv, sink, do, lse, delta, hi):
    tl = qr.shape[0]
    nb = tl // AB
    nc = CTX // AB

    def body(q_ref, kp_ref, kc_ref, kn_ref, kx_ref, vp_ref, vc_ref, vn_ref, vx_ref, sink_ref, do_ref, lse_ref, dl_ref,
             dq_ref, dk_ref, dv_ref, dkx_ref, dvx_ref, dsink_ref, dk_acc, dv_acc):
        i = pl.program_id(0)

        @pl.when(i == 0)
        def _():
            dkx_ref[...] = jnp.zeros_like(dkx_ref)
            dvx_ref[...] = jnp.zeros_like(dvx_ref)
            dsink_ref[...] = jnp.zeros_like(dsink_ref)
            dk_acc[...] = jnp.zeros_like(dk_acc)
            dv_acc[...] = jnp.zeros_like(dv_acc)

        @pl.when(i < nb)
        def _():
            mask = _att_mask(i, nb)
            lane = lax.broadcasted_iota(jnp.int32, (1, 128), 1)
            s_prev, s_cur, s_next = (i + 2) % 3, i % 3, (i + 1) % 3
            dsink = jnp.zeros((1, 128), F32)
            for kvh in range(KVH):
                ksl = slice(kvh * HD, (kvh + 1) * HD)
                kall = jnp.concatenate([kp_ref[:, ksl], kc_ref[:, ksl], kn_ref[:, ksl], kx_ref[:, ksl]], axis=0)
                vall = jnp.concatenate([vp_ref[:, ksl], vc_ref[:, ksl], vn_ref[:, ksl], vx_ref[:, ksl]], axis=0)
                qs = _att_stack(q_ref, kvh)
                dos = _att_stack(do_ref, kvh)
                lse_s = _att_col(lse_ref, kvh)
                dl_s = _att_col(dl_ref, kvh)
                s = _dot(qs, kall, NT, hi) * ATT_SCALE
                p = jnp.exp(_att_masked(s, mask) - lse_s)
                dp = _dot(dos, vall, NT, hi)
                ds = p * (dp - dl_s)
                dq = _dot(ds, kall, NN, hi) * ATT_SCALE
                dk_all = _dot(ds, qs, TN, hi) * ATT_SCALE
                dv_all = _dot(p, dos, TN, hi)
                dkx_ref[:, ksl] += dk_all[3 * AB:]
                dvx_ref[:, ksl] += dv_all[3 * AB:]
                dk_acc[s_prev, :, ksl] += dk_all[0:AB]
                dv_acc[s_prev, :, ksl] += dv_all[0:AB]
                dk_acc[s_cur, :, ksl] += dk_all[AB:2 * AB]
                dv_acc[s_cur, :, ksl] += dv_all[AB:2 * AB]
                dk_acc[s_next, :, ksl] = dk_all[2 * AB:3 * AB]
                dv_acc[s_next, :, ksl] = dv_all[2 * AB:3 * AB]
                psink = jnp.exp(_att_sink(sink_ref, kvh) - lse_s) * dl_s
                for g in range(GRP):
                    h = kvh * GRP + g
                    dq_ref[:, h * HD:(h + 1) * HD] = dq[g * AB:(g + 1) * AB]
                    dsink = dsink - jnp.sum(psink[g * AB:(g + 1) * AB], axis=0, keepdims=True) * (lane == h).astype(F32)
            dsink_ref[...] += dsink

        @pl.when(i >= 1)
        def _():
            dk_ref[...] = dk_acc[(i + 2) % 3]
            dv_ref[...] = dv_acc[(i + 2) % 3]

    blk = lambda i: jnp.minimum(i, nb - 1)
    row = pl.BlockSpec((AB, D), lambda i: (blk(i), 0))
    col = pl.BlockSpec((AB, 128), lambda i: (blk(i), 0))
    ks = [pl.BlockSpec((AB, KVH * HD), lambda i: (jnp.maximum(blk(i) - 1, 0) + nc, 0)),
          pl.BlockSpec((AB, KVH * HD), lambda i: (blk(i) + nc, 0)),
          pl.BlockSpec((AB, KVH * HD), lambda i: (jnp.minimum(i + 1, nb - 1) + nc, 0)),
          pl.BlockSpec((CTX, KVH * HD), lambda i: (0, 0))]
    kv_out = pl.BlockSpec((AB, KVH * HD), lambda i: (jnp.maximum(i - 1, 0), 0))
    ctx_out = pl.BlockSpec((CTX, KVH * HD), lambda i: (0, 0))
    return pl.pallas_call(
        body, name="attn_bwd", grid=(nb + 1,),
        in_specs=[row] + ks + ks + [pl.BlockSpec((1, 128), lambda i: (0, 0)), row, col, col],
        out_specs=[row, kv_out, kv_out, ctx_out, ctx_out, pl.BlockSpec((1, 128), lambda i: (0, 0))],
        out_shape=[jax.ShapeDtypeStruct((tl, D), F32), jax.ShapeDtypeStruct((tl, KVH * HD), F32),
                   jax.ShapeDtypeStruct((tl, KVH * HD), F32), jax.ShapeDtypeStruct((CTX, KVH * HD), F32),
                   jax.ShapeDtypeStruct((CTX, KVH * HD), F32), jax.ShapeDtypeStruct((1, 128), F32)],
        scratch_shapes=[pltpu.VMEM((3, AB, KVH * HD), F32), pltpu.VMEM((3, AB, KVH * HD), F32)],
        compiler_params=_cparams(("arbitrary",)),
    )(qr, kr, kr, kr, kr, vv, vv, vv, vv, sink, do, lse, delta)


def _mm(a, b, ta=False, tb=False, out_dtype=F32, tm=512, tn=1024, tk=1024, name="mm", hi=False):
    a_parts = a.shape[0] if a.ndim == 3 else 0
    b_parts = b.shape[0] if b.ndim == 3 else 0
    assert not (a_parts and ta) and not (b_parts and tb)
    if a_parts:
        m, kd = a.shape[1], a_parts * a.shape[2]
    else:
        m, kd = (a.shape[1], a.shape[0]) if ta else a.shape
    n = b_parts * b.shape[2] if b_parts else (b.shape[0] if tb else b.shape[1])
    tm, tn, tk = min(tm, m), min(tn, n), min(tk, kd)
    assert m % tm == 0 and n % tn == 0 and kd % tk == 0, (name, m, n, kd, tm, tn, tk)
    nk = kd // tk
    dims = ((0,) if ta else (1,), (1,) if tb else (0,))

    def body(a_ref, b_ref, o_ref, *scr):
        part = _dot(a_ref[0] if a_parts else a_ref[...], b_ref[0] if b_parts else b_ref[...], dims, hi)
        if nk == 1:
            o_ref[...] = part.astype(out_dtype)
        else:
            acc = scr[0]
            kk = pl.program_id(2)

            @pl.when(kk == 0)
            def _():
                acc[...] = part

            @pl.when(kk > 0)
            def _():
                acc[...] += part

            @pl.when(kk == nk - 1)
            def _():
                o_ref[...] = acc[...].astype(out_dtype)

    a_spec = pl.BlockSpec((tk, tm), lambda i, j, k: (k, i)) if ta else pl.BlockSpec((tm, tk), lambda i, j, k: (i, k))
    b_spec = pl.BlockSpec((tn, tk), lambda i, j, k: (j, k)) if tb else pl.BlockSpec((tk, tn), lambda i, j, k: (k, j))
    if a_parts:
        per = a.shape[2] // tk
        assert per * tk == a.shape[2]
        a_spec = pl.BlockSpec((1, tm, tk), lambda i, j, k: (k // per, i, k % per))
    if b_parts:
        per_n = b.shape[2] // tn
        assert per_n * tn == b.shape[2]
        b_spec = pl.BlockSpec((1, tk, tn), lambda i, j, k: (j // per_n, k, j % per_n))
    return pl.pallas_call(
        body, name=name, grid=(m // tm, n // tn, nk),
        in_specs=[a_spec, b_spec],
        out_specs=pl.BlockSpec((tm, tn), lambda i, j, k: (i, j)),
        out_shape=jax.ShapeDtypeStruct((m, n), out_dtype),
        scratch_shapes=[] if nk == 1 else [pltpu.VMEM((tm, tn), F32)],
        compiler_params=_cparams(("parallel", "parallel", "arbitrary")),
    )(a, b)


HALO = 8


class _In:
    def __init__(self, arr, w=None, cb=0, roff=0, halo=None, ridx=None):
        self.arr, self.w, self.cb, self.roff, self.halo = arr, w or arr.shape[1], cb, roff, halo
        self.ridx = ridx or (lambda i, roff=roff: i + roff)


class _Full:
    def __init__(self, arr, w=None, cb=0):
        self.arr, self.w, self.cb = arr, w, cb


class _Out:
    def __init__(self, cols, dtype=F32, w=None, cb=0, acc=False, rows=1, roff=0, nrows=None, stack=0, into=None):
        self.cols, self.dtype, self.w, self.cb, self.acc, self.rows, self.roff, self.nrows, self.stack = (
            cols, dtype, w or cols, cb, acc, rows, roff, nrows, stack)
        self.into = into


def _alias_outs(arrays, specs, outs):
    aliases = {}
    for k, o in enumerate(outs):
        if o.into is not None:
            aliases[len(arrays)] = k
            arrays.append(o.into)
            specs.append(pl.BlockSpec(memory_space=pl.ANY))
    return aliases


def _rowcall(name, fn, nrow_tiles, tile, ins, outs, ncol=1):
    arrays, specs, kinds = [], [], []
    for x in ins:
        if isinstance(x, _Full):
            arrays.append(x.arr)
            if x.w is None:
                specs.append(pl.BlockSpec(x.arr.shape, lambda j, i: (0, 0)))
            else:
                specs.append(pl.BlockSpec((x.arr.shape[0], x.w), lambda j, i, cb=x.cb: (0, cb + j)))
            kinds.append("full")
            continue
        w, cb, roff = x.w, x.cb, x.roff
        cur = pl.BlockSpec((tile, w), lambda j, i, cb=cb, ridx=x.ridx: (ridx(i), cb + j))
        if x.halo is None:
            arrays.append(x.arr)
            specs.append(cur)
            kinds.append("tile")
        else:
            r8 = tile // HALO
            last = x.arr.shape[0] // HALO - 1
            prev = pl.BlockSpec((HALO, w), lambda j, i, cb=cb, roff=roff, r8=r8: (jnp.maximum((i + roff) * r8 - 1, 0), cb + j))
            nxt = pl.BlockSpec((HALO, w), lambda j, i, cb=cb, roff=roff, r8=r8, last=last:
                               (jnp.minimum((i + roff + 1) * r8, last), cb + j))
            arrays += [x.arr, x.arr, x.arr]
            specs += [prev, cur, nxt]
            kinds.append(("halo", x.halo))
    out_specs, out_shapes = [], []
    for o in outs:
        if o.acc:
            out_specs.append(pl.BlockSpec((o.rows, o.w), lambda j, i, cb=o.cb: (0, cb + j)))
            out_shapes.append(jax.ShapeDtypeStruct((o.rows, o.cols), o.dtype))
        elif o.stack:
            out_specs.append(pl.BlockSpec((o.stack, tile, o.w), lambda j, i, cb=o.cb: (0, i, cb + j)))
            out_shapes.append(jax.ShapeDtypeStruct((o.stack, nrow_tiles * tile, o.cols), o.dtype))
        else:
            out_specs.append(pl.BlockSpec((tile, o.w), lambda j, i, cb=o.cb, roff=o.roff: (i + roff, cb + j)))
            out_shapes.append(jax.ShapeDtypeStruct(((o.nrows or nrow_tiles * tile), o.cols), o.dtype))
    aliases = _alias_outs(arrays, specs, outs)
    n_in = len(arrays)

    def body(*refs):
        j = pl.program_id(0)
        i = pl.program_id(1)
        vals, r = [], 0
        for kind in kinds:
            if kind in ("full", "tile"):
                vals.append(refs[r][...])
                r += 1
            else:
                pok, nok = kind[1]
                p, c, n = refs[r][...], refs[r + 1][...], refs[r + 2][...]
                p = jnp.where(pok(i), p, jnp.zeros_like(p))
                n = jnp.where(nok(i), n, jnp.zeros_like(n))
                vals.append(jnp.concatenate([p, c, n], axis=0))
                r += 3
        res = fn(i, j, *vals)
        for o, ref, val in zip(outs, refs[n_in:], res):
            if o.acc:
                @pl.when(i == 0)
                def _(ref=ref, val=val, o=o):
                    ref[...] = val.astype(o.dtype)

                @pl.when(i > 0)
                def _(ref=ref, val=val, o=o):
                    ref[...] += val.astype(o.dtype)
            elif o.stack:
                for s_ in range(o.stack):
                    ref[s_] = val[s_].astype(o.dtype)
            else:
                ref[...] = val.astype(o.dtype)

    return pl.pallas_call(
        body, name=name, grid=(ncol, nrow_tiles), in_specs=specs, out_specs=out_specs, out_shape=out_shapes,
        input_output_aliases=aliases, compiler_params=_cparams(("parallel", "arbitrary")),
    )(*arrays)


def _mm_ep(name, a, b, tb, tm, tk, fn, ins, outs, hi=False):
    a_parts = a.shape[0] if a.ndim == 3 else 0
    m, kd = (a.shape[1], a_parts * a.shape[2]) if a_parts else a.shape
    n = b.shape[0] if tb else b.shape[1]
    tk = min(tk, kd)
    assert m % tm == 0 and kd % tk == 0, (name, m, kd, tm, tk)
    nk = kd // tk
    dims = ((1,), (1,) if tb else (0,))
    if a_parts:
        per = a.shape[2] // tk
        arrays, specs = [a], [pl.BlockSpec((1, tm, tk), lambda i, k: (k // per, i, k % per))]
    else:
        arrays, specs = [a], [pl.BlockSpec((tm, tk), lambda i, k: (i, k))]
    arrays.append(b)
    specs.append(pl.BlockSpec((n, tk), lambda i, k: (0, k)) if tb else pl.BlockSpec((tk, n), lambda i, k: (k, 0)))
    for x in ins:
        arrays.append(x.arr)
        if isinstance(x, _Full):
            specs.append(pl.BlockSpec(x.arr.shape, lambda i, k: (0, 0)))
        else:
            specs.append(pl.BlockSpec((tm, x.w), lambda i, k, cb=x.cb, ridx=x.ridx: (ridx(i), cb)))
    out_specs, out_shapes = [], []
    for o in outs:
        if o.acc:
            out_specs.append(pl.BlockSpec((o.rows, o.w), lambda i, k, cb=o.cb: (0, cb)))
            out_shapes.append(jax.ShapeDtypeStruct((o.rows, o.cols), o.dtype))
        else:
            out_specs.append(pl.BlockSpec((tm, o.w), lambda i, k, cb=o.cb, roff=o.roff: (i + roff, cb)))
            out_shapes.append(jax.ShapeDtypeStruct((o.nrows or m, o.cols), o.dtype))
    n_vals = len(arrays)
    aliases = _alias_outs(arrays, specs, outs)
    n_in = len(arrays)

    def body(*refs):
        i, kk = pl.program_id(0), pl.program_id(1)
        a_ref, b_ref = refs[0], refs[1]
        acc_ref = refs[-1]
        part = _dot(a_ref[0] if a_parts else a_ref[...], b_ref[...], dims, hi)

        @pl.when(kk == 0)
        def _():
            acc_ref[...] = part

        @pl.when(kk > 0)
        def _():
            acc_ref[...] += part

        @pl.when(kk == nk - 1)
        def _():
            res = fn(i, acc_ref[...], *[r[...] for r in refs[2:n_vals]])
            for o, ref, val in zip(outs, refs[n_in:-1], res):
                if o.acc:
                    @pl.when(i == 0)
                    def _(ref=ref, val=val, o=o):
                        ref[...] = val.astype(o.dtype)

                    @pl.when(i > 0)
                    def _(ref=ref, val=val, o=o):
                        ref[...] += val.astype(o.dtype)
                else:
                    ref[...] = val.astype(o.dtype)

    return pl.pallas_call(
        body, name=name, grid=(m // tm, nk), in_specs=specs, out_specs=out_specs, out_shape=out_shapes,
        scratch_shapes=[pltpu.VMEM((tm, n), F32)], input_output_aliases=aliases,
        compiler_params=_cparams(("arbitrary", "arbitrary")),
    )(*arrays)


def _shift(xe, s, tile):
    if s == 0:
        return xe[HALO:HALO + tile]
    return pltpu.roll(xe, (-s) % xe.shape[0], 0)[HALO:HALO + tile]


def _silu(x):
    return x * jax.nn.sigmoid(x)


def _dsilu(x):
    s = jax.nn.sigmoid(x)
    return s * (1.0 + x * (1.0 - s))


def _heads(x, fn):
    return jnp.concatenate([fn(h, x[:, h * HD:(h + 1) * HD]) for h in range(x.shape[1] // HD)], axis=1)


def _colsum(x):
    return jnp.sum(x, axis=0, keepdims=True)


def _rowmean(x):
    return jnp.mean(x, axis=1, keepdims=True)


def _rowsum(x):
    return jnp.sum(x, axis=1, keepdims=True)


TILE = 256
CT = CTX // TILE


def _all_halo(n_tiles):
    return (lambda i: i >= CT + 1, lambda i: jnp.logical_and(i >= CT, i < n_tiles - 1))


def _lat_halo(n_tiles):
    return (lambda i: i >= 1, lambda i: i < n_tiles - 1)


def _rms_mod(x, nm, shift, scale):
    r = lax.rsqrt(_rowmean(x * x) + EPS)
    return (x * r * nm) * (1.0 + scale) + shift


def _rms_mod_bwd(dh, x, nm, scale):
    r = lax.rsqrt(_rowmean(x * x) + EPS)
    xn = x * r
    dz = dh * (1.0 + scale)
    dxn = dz * nm
    dx = r * (dxn - xn * _rowmean(dxn * xn))
    return dx, _colsum(dz * xn), _colsum(dh), _colsum(dh * (xn * nm))


def _norm_mod(x, ctx, nm, mod_c, mod_x):
    n = (x.shape[0] + ctx.shape[0]) // TILE

    def fn(i, j, c_, x_, nm_, mc, mx):
        m = jnp.where(i < CT, mc, mx)
        return (_rms_mod(jnp.where(i < CT, c_, x_), nm_, m[0:1], m[1:2]),)

    ins = [_In(ctx, ridx=lambda i: jnp.minimum(i, CT - 1)), _In(x, ridx=lambda i: jnp.maximum(i - CT, 0)),
           _Full(nm), _Full(mod_c), _Full(mod_x)]
    return _rowcall("norm_mod", fn, n, TILE, ins, [_Out(D, BF16)])[0]


def _norm_mod_bwd(dh, xs, dres, nm, mod, roff, n):
    ins = [_In(dh, roff=roff), _In(xs), _Full(nm), _Full(mod)] + ([] if dres is None else [_In(dres)])

    def fn(i, j, dh_, x, nm_, m, *rest):
        dx, dn, dsh, dsc = _rms_mod_bwd(dh_, x, nm_, m[1:2])
        if rest:
            return (dx + rest[0], dn, dsh, dsc)
        return (dn, dsh, dsc)

    accs = [_Out(D, acc=True), _Out(D, acc=True), _Out(D, acc=True)]
    return _rowcall("norm_mod_bwd", fn, n, TILE, ins, ([] if dres is None else [_Out(D)]) + accs)


DN_Q_SCALE = HD ** -0.5


def _conv_taps(xe, w, width, rows=None):
    r = width // 2
    acc = None
    for t in range(width):
        s = t - r
        if rows is None:
            sh = xe if s == 0 else pltpu.roll(xe, (-s) % xe.shape[0], 0)
        else:
            sh = _shift(xe, s, rows)
        term = sh * w[t:t + 1]
        acc = term if acc is None else acc + term
    return acc


def _rolled(xe, width):
    r = width // 2
    return [xe if t == r else pltpu.roll(xe, (r - t) % xe.shape[0], 0) for t in range(width)]


def _conv_bwd(rolled, w, c_grad, width):
    r = width // 2
    cc = c_grad[HALO:HALO + TILE]
    dx, dws = None, []
    for t in range(width):
        term = _shift(c_grad, r - t, TILE) * w[t:t + 1]
        dx = term if dx is None else dx + term
        dws.append(_colsum(cc * rolled[t][HALO:HALO + TILE]))
    return dx, jnp.concatenate(dws + [jnp.zeros((8 - width, cc.shape[1]), F32)], axis=0)


def _silu_both(x):
    s = jax.nn.sigmoid(x)
    return x * s, s * (1.0 + x * (1.0 - s))


def _l2n(x, scale):
    rn = lax.rsqrt(_rowsum(x * x) + EPS)
    return x * (rn * scale)


def _l2n_bwd(dy, x, scale):
    rn = lax.rsqrt(_rowsum(x * x) + EPS)
    xu = x * rn
    return (scale * rn) * (dy - xu * _rowsum(dy * xu))


def _softplus(x):
    return jnp.maximum(x, 0.0) + jnp.log(1.0 + jnp.exp(-jnp.abs(x)))


def _lane_mask(lo, hi_):
    lane = lax.broadcasted_iota(jnp.int32, (1, 128), 1)
    return jnp.logical_and(lane >= lo, lane < hi_).astype(F32)


def _dn_prep(p, conv_w, gprm):
    n = p.shape[0] // TILE
    halo = _all_halo(n)

    def fn(i, j, qe, ke, ve, ba, w, gp):
        cq = _conv_taps(qe, w[:, 0:D], 5, TILE)
        ck = _conv_taps(ke, w[:, D:2 * D], 5, TILE)
        cv = _conv_taps(ve, w[:, 2 * D:3 * D], 5, TILE)
        q = _heads(_silu(cq), lambda h, x: _l2n(x, DN_Q_SCALE))
        k = _heads(_silu(ck), lambda h, x: _l2n(x, 1.0))
        v = _silu(cv)
        beta = jax.nn.sigmoid(ba)
        g = -jnp.exp(gp[0:1]) * _softplus(ba + gp[1:2])
        m0, m1 = _lane_mask(0, 8), _lane_mask(8, 16)
        gb_f = beta * m0 + pltpu.roll(g, 128 - 8, 1) * m1
        gb_b = pltpu.roll(beta, 128 - 8, 1) * m0 + pltpu.roll(g, 128 - 16, 1) * m1
        return q, k, v, gb_f, gb_b

    ins = [_In(p, D, 0, halo=halo), _In(p, D, 1, halo=halo), _In(p, D, 2, halo=halo), _In(p, 128, C_BA // 128),
           _Full(conv_w), _Full(gprm)]
    return _rowcall("dn_prep", fn, n, TILE, ins, [_Out(D), _Out(D), _Out(D), _Out(128), _Out(128)])


def _dn_prep_bwd(p, conv_w, gprm, dq2, dk2, dv2, dgb2, dk_at, dv_at, dp):
    n = p.shape[0] // TILE
    halo = _all_halo(n)

    def branch(xe, w, dye, scale):
        rolled = _rolled(xe, 5)
        c = rolled[0] * w[0:1]
        for t in range(1, 5):
            c = c + rolled[t] * w[t:t + 1]
        sx, dsilu = _silu_both(c)
        if scale is None:
            dsx = dye
        else:
            dsx = jnp.concatenate([_l2n_bwd(dye[:, h * HD:(h + 1) * HD], sx[:, h * HD:(h + 1) * HD], scale)
                                   for h in range(NH)], axis=1)
        return _conv_bwd(rolled, w, dsx * dsilu, 5)

    def fn(i, j, qe, ke, ve, ba, w, gp, dq0, dq1, dk0, dk1, dv0, dv1, dg0, dg1, dka, dva):
        dxq, dwq = branch(qe, w[:, 0:D], dq0 + dq1, DN_Q_SCALE)
        dxk, dwk = branch(ke, w[:, D:2 * D], dk0 + dk1, 1.0)
        dxv, dwv = branch(ve, w[:, 2 * D:3 * D], dv0 + dv1, None)
        m0, m1 = _lane_mask(0, 8), _lane_mask(8, 16)
        dbeta = dg0 * m0 + pltpu.roll(dg1 * m0, 8, 1)
        dg = pltpu.roll(dg0 * m1, 8, 1) + pltpu.roll(dg1 * m1, 16, 1)
        beta = jax.nn.sigmoid(ba)
        ea = jnp.exp(gp[0:1])
        z = ba + gp[1:2]
        g = -ea * _softplus(z)
        mg = _lane_mask(16, 32)
        da = dg * (-ea) * jax.nn.sigmoid(z) * mg
        dba = dbeta * beta * (1.0 - beta) * _lane_mask(0, 16) + da
        dgp = jnp.concatenate([_colsum(dg * g * mg), _colsum(da)], axis=0)
        half = jnp.concatenate([dxq, dxk, dxv, dka.astype(F32), dva.astype(F32), dba, jnp.zeros((TILE, PH - C_PAD), F32)],
                               axis=1)
        return (half, jnp.concatenate([dwq, dwk, dwv], axis=1), dgp)

    ins = [_In(p, D, 0, halo=halo), _In(p, D, 1, halo=halo), _In(p, D, 2, halo=halo), _In(p, 128, C_BA // 128),
           _Full(conv_w), _Full(gprm),
           _In(dq2, halo=halo), _In(dq2, roff=n, halo=halo), _In(dk2, halo=halo), _In(dk2, roff=n, halo=halo),
           _In(dv2, halo=halo), _In(dv2, roff=n, halo=halo), _In(dgb2), _In(dgb2, roff=n), _In(dk_at), _In(dv_at)]
    return _rowcall("dn_prep_bwd", fn, n, TILE, ins,
                    [_Out(PW, BF16, w=PH, cb=0, into=dp), _Out(3 * D, acc=True, rows=8), _Out(128, acc=True, rows=2)])


def _hnorm(x, w):
    return x * lax.rsqrt(_rowmean(x * x) + EPS) * w


def _hnorm_bwd(dy, x, w):
    r = lax.rsqrt(_rowmean(x * x) + EPS)
    xh = x * r
    dxh = dy * w
    return r * (dxh - xh * _rowmean(dxh * xh)), _colsum(dy * xh)


def _dn_gate_mm(o2, p, dn_norm, w_bdn, n_all, hi):
    n = n_all - CT

    def fn(i, j, of, ob, gt, w, wb):
        o = of + ob
        y = _heads(o, lambda h, x: _hnorm(x, w)) * _silu(gt)
        return y, _dot(y, wb, NN, hi)

    ins = [_In(o2, roff=CT), _In(o2, roff=n_all + CT), _In(p, D, C_GT // D, roff=CT), _Full(dn_norm), _Full(w_bdn)]
    return _rowcall("dn_gate_mm", fn, n, TILE, ins, [_Out(D, BF16), _Out(D)])


def _mm_bdn_dx_gate(dz_dn, w_bdn, o2, p, dn_norm, n_all, dp, hi):
    def fn(i, dy_, of, ob, gt, w):
        o = of + ob
        sg, dsg = _silu_both(gt)
        dos, dw = [], jnp.zeros((1, HD), F32)
        yn = []
        for h in range(NH):
            sl = slice(h * HD, (h + 1) * HD)
            dx, dwh = _hnorm_bwd(dy_[:, sl] * sg[:, sl], o[:, sl], w)
            dos.append(dx)
            dw = dw + dwh
            yn.append(_hnorm(o[:, sl], w))
        dgt = dy_ * jnp.concatenate(yn, axis=1) * dsg
        return jnp.concatenate(dos, axis=1), dgt, dw

    ins = [_In(o2, roff=CT), _In(o2, roff=n_all + CT), _In(p, D, C_GT // D, roff=CT), _Full(dn_norm)]
    outs = [_Out(D), _Out(PW, BF16, w=D, cb=C_GT // D, roff=CT, nrows=p.shape[0], into=dp), _Out(HD, acc=True)]
    return _mm_ep("mm_bdn_dx_gate", dz_dn, w_bdn, True, TILE, D, fn, ins, outs, hi)


def _rope_shuffle(x):
    lane = lax.broadcasted_iota(jnp.int32, (1, HD), 1)
    return jnp.where((lane % 64) < 32, pltpu.roll(x, HD - 32, 1), pltpu.roll(x, 32, 1))


def _rope(x, cos, sin):
    return x * cos + _rope_shuffle(x) * sin


def _rope_bwd(dy, cos, sin):
    return dy * cos + _rope_shuffle(dy * sin)


def _attn_prep(p, w, cos, sin, width, cb, roff, n, name):
    def fn(i, j, x, w_, c, s):
        return (_heads(x, lambda h, xh: _rope(_hnorm(xh, w_), c, s)),)

    ins = [_In(p, width, cb, roff=roff), _Full(w), _In(cos), _In(sin)]
    return _rowcall(name, fn, n, TILE, ins, [_Out(width)])[0]


def _attn_prep_bwd(dy, p, w, cos, sin, width, cb, roff, n, name, dx_out):
    def fn(i, j, dy_, x, w_, c, s):
        dxs, dw = [], jnp.zeros((1, HD), F32)
        for h in range(width // HD):
            sl = slice(h * HD, (h + 1) * HD)
            dx, dwh = _hnorm_bwd(_rope_bwd(dy_[:, sl], c, s), x[:, sl], w_)
            dxs.append(dx)
            dw = dw + dwh
        return jnp.concatenate(dxs, axis=1), dw

    ins = [_In(dy), _In(p, width, cb, roff=roff), _Full(w), _In(cos), _In(sin)]
    return _rowcall(name, fn, n, TILE, ins, [dx_out, _Out(HD, acc=True)])


def _mm_bat_merge(o_at, w_bat, z_dn, p, hi):
    def fn(i, za, zd, gd, ga):
        return za, jax.nn.sigmoid(gd) * zd + jax.nn.sigmoid(ga) * za

    ins = [_In(z_dn), _In(p, D, C_MG // D, roff=CT), _In(p, D, C_MG // D + 1, roff=CT)]
    return _mm_ep("mm_bat_merge", o_at, w_bat, False, TILE, D, fn, ins, [_Out(D), _Out(D, BF16)], hi)


def _mm_out_dx_merge(dmo, w_out, z_dn, z_at, p, hi):
    def fn(i, dm_, zd, za, gd, ga):
        sd, sa = jax.nn.sigmoid(gd), jax.nn.sigmoid(ga)
        dg = jnp.concatenate([dm_ * zd * sd * (1.0 - sd), dm_ * za * sa * (1.0 - sa)], axis=1)
        return dm_ * sd, dm_ * sa, dg

    ins = [_In(z_dn), _In(z_at), _In(p, D, C_MG // D, roff=CT), _In(p, D, C_MG // D + 1, roff=CT)]
    outs = [_Out(D, BF16), _Out(D, BF16), _Out(PW, BF16, w=2 * D, cb=C_MG // (2 * D), roff=CT, nrows=p.shape[0])]
    return _mm_ep("mm_out_dx_merge", dmo, w_out, True, TILE, D, fn, ins, outs, hi)


def _mm_out_resid(merged, w_out, x, g_a, nf, mod_f, hi):
    def fn(i, mo_, x_, ga, nf_, m):
        x1 = x_ + ga * mo_
        return mo_, x1, _rms_mod(x1, nf_, m[0:1], m[1:2])

    ins = [_In(x), _Full(g_a), _Full(nf), _Full(mod_f)]
    return _mm_ep("mm_out_resid", merged, w_out, False, min(512, x.shape[0]), D, fn, ins, [_Out(D), _Out(D), _Out(D, BF16)], hi)


def _mm_up_dx_norm(du, ffn_up, dy, x1, mo, g_a, nf, mod_f, hi):
    def fn(i, dh_, dy_, x1_, mo_, ga, nf_, m):
        dx, dn, dsh, dsc = _rms_mod_bwd(dh_, x1_, nf_, m[1:2])
        dx1 = dy_ + dx
        return dx1, ga * dx1, dn, dsh, dsc, _colsum(dx1 * mo_)

    ins = [_In(dy), _In(x1), _In(mo), _Full(g_a), _Full(nf), _Full(mod_f)]
    accs = [_Out(D, acc=True) for _ in range(4)]
    return _mm_ep("mm_up_dx_norm", du, ffn_up, True, min(512, x1.shape[0]), DFF, fn, ins, [_Out(D), _Out(D, BF16)] + accs, hi)


def _mm_down_loss(a, ffn_down, x1, tgt, g_f, hi):
    def fn(i, f_, x1_, t, gf):
        e = x1_ + gf * f_ - t
        dy = e * (1.0 / D)
        loss = _colsum(_rowsum(e * e)) * (0.5 / D)
        return dy, gf * dy, _colsum(dy * f_), jnp.broadcast_to(loss, (1, 128))

    ins = [_In(x1), _In(tgt), _Full(g_f)]
    outs = [_Out(D), _Out(D, BF16), _Out(D, acc=True), _Out(128, acc=True)]
    return _mm_ep("mm_down_loss", a, ffn_down, False, min(512, x1.shape[0]), DFF, fn, ins, outs, hi)


FW = DFF // 2


def _ffn_act(u, conv_w, conv_b, n):
    halo = _lat_halo(n)

    def fn(i, j, ge, ve, wg, wv, bg, bv):
        cg = _conv_taps(ge, wg, 3, TILE) + bg
        cv = _conv_taps(ve, wv, 3, TILE) + bv
        return (_silu(cg) * cv,)

    ins = [_In(u, FW, 0, halo=halo), _In(u, FW, 2, halo=halo), _Full(conv_w, FW, 0), _Full(conv_w, FW, 2),
           _Full(conv_b, FW, 0), _Full(conv_b, FW, 2)]
    return _rowcall("ffn_act", fn, n, TILE, ins, [_Out(DFF, BF16, FW)], ncol=2)[0]


def _ffn_act_bwd(u, da, conv_w, conv_b, n):
    halo = _lat_halo(n)

    def fn(i, j, ge, ve, dae, wg, wv, bg, bv):
        rg, rv = _rolled(ge, 3), _rolled(ve, 3)
        cg = rg[0] * wg[0:1] + rg[1] * wg[1:2] + rg[2] * wg[2:3] + bg
        cv = rv[0] * wv[0:1] + rv[1] * wv[1:2] + rv[2] * wv[2:3] + bv
        sg, dsg = _silu_both(cg)
        dcg = dae * cv * dsg
        dcv = dae * sg
        dxg, dwg = _conv_bwd(rg, wg, dcg, 3)
        dxv, dwv = _conv_bwd(rv, wv, dcv, 3)
        return (dxg, dxv), dwg, dwv, _colsum(dcg[HALO:HALO + TILE]), _colsum(dcv[HALO:HALO + TILE])

    ins = [_In(u, FW, 0, halo=halo), _In(u, FW, 2, halo=halo), _In(da, FW, 0, halo=halo),
           _Full(conv_w, FW, 0), _Full(conv_w, FW, 2), _Full(conv_b, FW, 0), _Full(conv_b, FW, 2)]
    outs = [_Out(DFF, BF16, FW, stack=2), _Out(DFF, w=FW, acc=True, rows=8), _Out(DFF, w=FW, acc=True, rows=8),
            _Out(DFF, w=FW, acc=True), _Out(DFF, w=FW, acc=True)]
    return _rowcall("ffn_act_bwd", fn, n, TILE, ins, outs, ncol=2)


def _rope_tables(tl):
    rows = tl // GRID_W
    inv = np.float32(ROPE_BASE) ** (-np.arange(32, dtype=np.float32) / np.float32(32))
    ar = np.arange(rows, dtype=np.float32)[:, None] * inv
    ac = np.arange(GRID_W, dtype=np.float32)[:, None] * inv

    def table(r, c):
        full = (rows, GRID_W, HD // 2)
        return jnp.concatenate([jnp.broadcast_to(jnp.asarray(r)[:, None, :], full),
                                jnp.broadcast_to(jnp.asarray(c)[None, :, :], full)], axis=2).reshape(tl, HD)

    two = lambda a, b: np.concatenate([a, b], axis=1).astype(np.float32)
    cos = table(two(np.cos(ar), np.cos(ar)), two(np.cos(ac), np.cos(ac)))
    sin = table(two(-np.sin(ar), np.sin(ar)), two(-np.sin(ac), np.sin(ac)))
    return cos, sin


def _pad_w_in(w_in):
    return jnp.concatenate([w_in[:, 0:3072], w_in[:, 5152:5664], w_in[:, 4096:4128], jnp.zeros((D, 96 + C_GT - C_PAD), w_in.dtype),
                            w_in[:, 3072:4096], w_in[:, 4128:5152], w_in[:, 5664:7712]], axis=1)


def _unpad_w_in(g, axis=1):
    cut = lambda a, b: lax.slice_in_dim(g, a, b, axis=axis)
    return jnp.concatenate([cut(0, 3072), cut(C_GT, C_GT + D), cut(C_BA, C_BA + 32), cut(C_QAT, C_QAT + D),
                            cut(C_KAT, C_KAT + 512), cut(C_MG, C_MG + 2 * D)], axis=axis)


def _local_step(x, ctx, tgt, mod_x, mod_c, w, hi=False):
    tl = x.shape[0]
    t_all = tl + CTX
    n_all, n = t_all // TILE, tl // TILE
    tm_all = 1280 if t_all % 1280 == 0 else TILE
    tm_lat = 1024
    mm = functools.partial(_mm, hi=hi)
    sp = lambda m: [m[:, k * D:(k + 1) * D] for k in range(6)]
    sh_a, sc_a, g_a, sh_f, sc_f, g_f = sp(mod_x)
    sh_ac, sc_ac = sp(mod_c)[:2]
    mod_ax = jnp.concatenate([sh_a, sc_a], axis=0)
    mod_ac = jnp.concatenate([sh_ac, sc_ac], axis=0)
    mod_f = jnp.concatenate([sh_f, sc_f], axis=0)
    nm, nf = w["norm_mix"], w["norm_ffn"]
    cos, sin = _rope_tables(tl)
    cos_all = jnp.concatenate([jnp.ones((CTX, HD), F32), cos], axis=0)
    sin_all = jnp.concatenate([jnp.zeros((CTX, HD), F32), sin], axis=0)
    conv_dn = jnp.concatenate([w["dn_conv"], jnp.zeros((3, 3 * D), F32)], axis=0)
    gprm = jnp.concatenate([jnp.zeros((2, 16), F32),
                            jnp.concatenate([w["dn_a_log"].reshape(1, 16), w["dn_dt_bias"].reshape(1, 16)], axis=0),
                            jnp.zeros((2, 96), F32)], axis=1)
    conv_ff = jnp.concatenate([w["ffn_conv"], jnp.zeros((5, 2 * DFF), F32)], axis=0)
    sink = jnp.concatenate([w["attn_sink"].reshape(1, NH), jnp.zeros((1, 128 - NH), F32)], axis=1)
    nct = CTX // CH

    h = _norm_mod(x, ctx, nm, mod_ac, mod_ax)
    p = mm(h, w["w_in_p"], tm=tm_all, tn=2048, name="mm_in")
    q, k, v, gb_f, gb_b = _dn_prep(p, conv_dn, gprm)
    gb = jnp.stack([gb_f, gb_b])
    dn_u, dn_w, dn_qg, dn_kd, dn_pm, dn_t = _dn_intra_fwd(q, k, v, gb, nct, hi)
    o2, s_hist, dn_vn = _dn_seq_fwd(dn_u, dn_w, dn_qg, dn_kd, dn_pm, gb, nct, hi)
    o2 = o2.reshape(2 * t_all, D)
    y_dn, z_dn = _dn_gate_mm(o2, p, w["dn_norm"], w["w_branch_dn"], n_all, hi)
    qr = _attn_prep(p, w["q_norm"], cos, sin, D, C_QAT // D, CT, n, "attn_prep_q")
    kr = _attn_prep(p, w["k_norm"], cos_all, sin_all, KVH * HD, C_KAT // (KVH * HD), 0, n_all, "attn_prep_k")
    vv = p[:, C_VAT:C_VAT + KVH * HD]
    o_at, lse = _attn_fwd(qr, kr, vv, sink, hi)
    z_at, merged = _mm_bat_merge(o_at, w["w_branch_attn"], z_dn, p, hi)
    mo, x1, h2 = _mm_out_resid(merged, w["w_out"], x, g_a, nf, mod_f, hi)
    u = mm(h2, w["ffn_up"], tm=2 * tm_lat, tn=1408, name="mm_up")
    a = _ffn_act(u, conv_ff, w["ffn_conv_b"], n)
    dy, df, dg_f, loss = _mm_down_loss(a, w["ffn_down"], x1, tgt, g_f, hi)

    g = {}
    da = mm(df, w["ffn_down"], tb=True, tm=tm_lat, tn=DFF, name="mm_down_dx")
    g["ffn_down"] = mm(a, df, ta=True, tm=1408, tn=1024, tk=2 * tm_lat, name="mm_down_dw")
    du, dcw_g, dcw_v, dcb_g, dcb_v = _ffn_act_bwd(u, da, conv_ff, w["ffn_conv_b"], n)
    g["ffn_conv"] = jnp.concatenate([dcw_g, dcw_v], axis=1)[0:3]
    g["ffn_conv_b"] = jnp.concatenate([dcb_g, dcb_v], axis=1)
    g["ffn_up"] = mm(h2, du, ta=True, tm=1024, tn=1408, tk=2 * tm_lat, name="mm_up_dw")
    dx1, dmo, g["norm_ffn"], dsh_f, dsc_f, dg_a = _mm_up_dx_norm(du, w["ffn_up"], dy, x1, mo, g_a, nf, mod_f, hi)
    g["w_out"] = mm(merged, dmo, ta=True, tm=1024, tk=2 * tm_lat, name="mm_out_dw")
    dz_dn, dz_at, dmg = _mm_out_dx_merge(dmo, w["w_out"], z_dn, z_at, p, hi)
    g["w_branch_dn"] = mm(y_dn, dz_dn, ta=True, tm=1024, tk=2 * tm_lat, name="mm_bdn_dw")
    do_at, delta = _mm_bat_dx_delta(dz_at, w["w_branch_attn"], o_at, hi)
    g["w_branch_attn"] = mm(o_at, dz_at, ta=True, tm=1024, tk=2 * tm_lat, name="mm_bat_dw")

    do_dn, dp, g["dn_norm"] = _mm_bdn_dx_gate(dz_dn, w["w_branch_dn"], o2, p, w["dn_norm"], n_all, dmg, hi)
    do_all = do_dn
    dn_dvn, dn_dw, dn_dqg, dn_dkd, dn_del = _dn_seq_bwd(dn_w, dn_qg, dn_kd, dn_pm, dn_vn, s_hist, gb, do_all, nct, hi)
    dq2, dk2, dv2, dgb2 = _dn_intra_bwd(q, k, v, gb, dn_u, dn_w, dn_t, dn_vn, dn_dvn, dn_dw, dn_dqg, dn_dkd, dn_del,
                                        do_all, nct, hi)

    dqr, dk_lat, dv_lat, dkx, dvx, dsink = _attn_bwd(qr, kr, vv, sink, do_at, lse, delta, hi)
    g["attn_sink"] = dsink[:, 0:NH]
    q_out = _Out(PW, BF16, w=D, cb=C_QAT // D, roff=CT, nrows=t_all, into=dp)
    dp, g["q_norm"] = _attn_prep_bwd(dqr, p, w["q_norm"], cos, sin, D, C_QAT // D, CT, n, "attn_prep_q_bwd", q_out)
    dkr = jnp.concatenate([dkx, dk_lat], axis=0)
    dk_at, g["k_norm"] = _attn_prep_bwd(dkr, p, w["k_norm"], cos_all, sin_all, KVH * HD, C_KAT // (KVH * HD), 0, n_all,
                                        "attn_prep_k_bwd", _Out(KVH * HD, BF16))
    dv_at = jnp.concatenate([dvx, dv_lat], axis=0).astype(BF16)

    dp, dconv, dgprm = _dn_prep_bwd(p, conv_dn, gprm, dq2.reshape(2 * t_all, D), dk2.reshape(2 * t_all, D),
                                    dv2.reshape(2 * t_all, D), dgb2.reshape(2 * t_all, 128), dk_at, dv_at, dp)
    g["dn_conv"] = dconv[0:5]
    g["dn_a_log"] = dgprm[0, 16:32].reshape(2, NH)
    g["dn_dt_bias"] = dgprm[1, 16:32].reshape(2, NH)
    dp = lax.dynamic_update_slice(dp, jnp.zeros((CTX, PH), BF16), (0, PH))
    dh = mm(dp, w["w_in_p"], tb=True, tm=tm_all, tn=1024, tk=2048, name="mm_in_dx")
    g["w_in_p"] = mm(h, dp, ta=True, tm=1024, tn=2048, tk=tm_all, name="mm_in_dw")
    dnm_c, dsh_ac, dsc_ac = _norm_mod_bwd(dh, ctx, None, nm, mod_ac, 0, CT)
    grad_x, dnm_x, dsh_a, dsc_a = _norm_mod_bwd(dh, x, dx1, nm, mod_ax, CT, n)
    g["norm_mix"] = dnm_c + dnm_x
    dmod_x = jnp.concatenate([dsh_a, dsc_a, dg_a, dsh_f, dsc_f, dg_f], axis=1)
    dmod_c = jnp.concatenate([dsh_ac, dsc_ac, jnp.zeros((1, 4 * D), F32)], axis=1)
    return loss, grad_x, g, dmod_x, dmod_c


def _sum_slots(buf, n_slots, rows, tile, name, stride=1):
    nt = rows // tile

    def fn(i, j, *vals):
        acc = vals[0]
        for v in vals[1:]:
            acc = acc + v
        return (acc,)

    ins = [_In(buf, roff=k * stride * nt) for k in range(n_slots)]
    return _rowcall(name, fn, nt, tile, ins, [_Out(buf.shape[1])])[0]


ADAM_LR, ADAM_B1, ADAM_B2, ADAM_EPS, ADAM_WD, ADAM_STEP = 0.001, 0.9, 0.999, 1e-08, 0.01, 10


def _row_tile(rows, cols):
    for t in (512, 256, 128, 64, 32, 16, 8):
        if rows % t == 0 and t * cols * 4 * 14 <= 40 * 1024 * 1024:
            return t
    return rows


def _adamw(w, g, m, v, name):
    shape = w.shape
    cols = shape[-1]
    rows = max(1, math.prod(shape[:-1]))
    tile = _row_tile(rows, cols)
    c1 = 1.0 / (1.0 - ADAM_B1 ** ADAM_STEP)
    c2 = 1.0 / (1.0 - ADAM_B2 ** ADAM_STEP)

    def fn(i, j, w_, g_, m_, v_):
        mn = ADAM_B1 * m_ + (1.0 - ADAM_B1) * g_
        vn = ADAM_B2 * v_ + (1.0 - ADAM_B2) * (g_ * g_)
        delta = -ADAM_LR * ((mn * c1) / (jnp.sqrt(vn * c2) + ADAM_EPS) + ADAM_WD * w_)
        return delta, mn, vn

    r2 = lambda a: a.reshape(rows, cols)
    outs = _rowcall(name, fn, rows // tile, tile, [_In(r2(w)), _In(r2(g)), _In(r2(m)), _In(r2(v))],
                    [_Out(cols), _Out(cols), _Out(cols)])
    return [o.reshape(shape) for o in outs]


MESH = pl.DeviceIdType.MESH
ANY = pl.BlockSpec(memory_space=pl.ANY)


def _pos():
    return lax.axis_index("x"), lax.axis_index("y"), lax.axis_index("c")


def _all_gather_many(blks, name):
    na = len(blks)

    def body(*refs):
        x_refs, out_refs = refs[:na], refs[na:2 * na]
        send_sems, recv_sems, local_sems = refs[2 * na:]
        x, y, c = _pos()
        me, sibling = (x, y, c), (x, y, 1 - c)
        chips = [(1 - x, y), (x, 1 - y), (1 - x, 1 - y)]

        def rows(a, px, py, pc):
            m_per = blks[a].shape[0]
            return out_refs[a].at[pl.ds(pl.multiple_of((4 * px + 2 * py + pc) * m_per, 8), m_per), :]

        def copy(a, k, block, to, src=None):
            return pltpu.make_async_remote_copy(
                src_ref=rows(a, *block) if src is None else src, dst_ref=rows(a, *block),
                send_sem=send_sems.at[7 * a + k], recv_sem=recv_sems.at[7 * a + k], device_id=to, device_id_type=MESH)

        every = range(na)
        mine = [pltpu.make_async_copy(x_refs[a], rows(a, *me), local_sems.at[a]) for a in every]
        for cp in mine:
            cp.start()
        first = [copy(a, 0, me, sibling, src=x_refs[a]) for a in every]
        first += [copy(a, 1 + j, me, (*chip, c), src=x_refs[a]) for j, chip in enumerate(chips) for a in every]
        for cp in first:
            cp.start()
        passed = []
        for j, chip in enumerate(chips):
            for a in every:
                copy(a, 1 + j, (*chip, c), me).wait_recv()
                passed.append(copy(a, 4 + j, (*chip, c), sibling))
                passed[-1].start()
        for a in every:
            copy(a, 0, sibling, me).wait_recv()
        for j, chip in enumerate(chips):
            for a in every:
                copy(a, 4 + j, (*chip, 1 - c), me).wait_recv()
        for cp in first + passed:
            cp.wait_send()
        for cp in mine:
            cp.wait()

    return pl.pallas_call(
        body, name=name,
        out_shape=[jax.ShapeDtypeStruct((N_DEV * b.shape[0], b.shape[1]), b.dtype) for b in blks],
        in_specs=[ANY] * na, out_specs=[ANY] * na,
        scratch_shapes=[pltpu.SemaphoreType.DMA((7 * na,)), pltpu.SemaphoreType.DMA((7 * na,)), pltpu.SemaphoreType.DMA((na,))],
        compiler_params=pltpu.CompilerParams(has_side_effects=True),
    )(*blks)


def _all_gather(blk, name):
    return _all_gather_many([blk], name)[0]


def _flip(v, bit):
    return 1 - v if bit else v


D2D_STREAMS = 8
ICI_STREAMS = 2


def _sibling_exchange(src, seg_rows, n_seg, paired, name):
    n = src.shape[1]
    per_seg = D2D_STREAMS // n_seg
    per = seg_rows // per_seg
    assert per_seg * n_seg == D2D_STREAMS and per * per_seg == seg_rows and per % 16 == 0

    def body(x_ref, out_ref, send_sems, recv_sems):
        x, y, c = _pos()
        copies = []
        for s in range(n_seg):
            base = (2 * s + (1 - c)) * seg_rows if paired else s * seg_rows
            for j in range(per_seg):
                i = s * per_seg + j
                cp = pltpu.make_async_remote_copy(
                    src_ref=x_ref.at[pl.ds(pl.multiple_of(base + j * per, 16), per), :],
                    dst_ref=out_ref.at[pl.ds(s * seg_rows + j * per, per), :],
                    send_sem=send_sems.at[i], recv_sem=recv_sems.at[i], device_id=(x, y, 1 - c), device_id_type=MESH)
                cp.start()
                copies.append(cp)
        for cp in copies:
            cp.wait_recv()
        for cp in copies:
            cp.wait_send()

    return pl.pallas_call(
        body, name=name, out_shape=jax.ShapeDtypeStruct((n_seg * seg_rows, n), src.dtype),
        in_specs=[ANY], out_specs=ANY,
        scratch_shapes=[pltpu.SemaphoreType.DMA((D2D_STREAMS,)), pltpu.SemaphoreType.DMA((D2D_STREAMS,))],
        compiler_params=pltpu.CompilerParams(has_side_effects=True),
    )(src)


def _transpose_cast(x, dtype, name):
    r, c = x.shape
    tc = 512

    def body(x_ref, o_ref):
        o_ref[...] = x_ref[...].T.astype(o_ref.dtype)

    return pl.pallas_call(
        body, name=name, grid=(c // tc,),
        in_specs=[pl.BlockSpec((r, tc), lambda j: (0, j))], out_specs=pl.BlockSpec((tc, r), lambda j: (j, 0)),
        out_shape=jax.ShapeDtypeStruct((c, r), dtype), compiler_params=_cparams(("parallel",)),
    )(x)


def _chip_exchange(buf, rows, name):
    n = buf.shape[1]
    per = rows // ICI_STREAMS
    assert per * ICI_STREAMS == rows and per % 16 == 0

    def body(x_ref, out_ref, send_sems, recv_sems):
        x, y, c = _pos()
        copies = []
        for k in range(1, 4):
            px, py = _flip(x, k & 2), _flip(y, k & 1)
            for j in range(ICI_STREAMS):
                i = (k - 1) * ICI_STREAMS + j
                cp = pltpu.make_async_remote_copy(
                    src_ref=x_ref.at[pl.ds(pl.multiple_of((2 * px + py) * rows + j * per, 16), per), :],
                    dst_ref=out_ref.at[pl.ds((k - 1) * rows + j * per, per), :],
                    send_sem=send_sems.at[i], recv_sem=recv_sems.at[i], device_id=(px, py, c), device_id_type=MESH)
                cp.start()
                copies.append(cp)
        for cp in copies:
            cp.wait_recv()
        for cp in copies:
            cp.wait_send()

    return pl.pallas_call(
        body, name=name, out_shape=jax.ShapeDtypeStruct((3 * rows, n), buf.dtype),
        in_specs=[ANY], out_specs=ANY,
        scratch_shapes=[pltpu.SemaphoreType.DMA((3 * ICI_STREAMS,)), pltpu.SemaphoreType.DMA((3 * ICI_STREAMS,))],
        compiler_params=pltpu.CompilerParams(has_side_effects=True),
    )(buf)


def _add_rows(parts, rows, dtype, name):
    tile = 1024
    ins = [_In(a, roff=r0 // tile) for a, r0 in parts]

    def fn(i, j, *vals):
        acc = vals[0].astype(F32)
        for v_ in vals[1:]:
            acc = acc + v_.astype(F32)
        return (acc,)

    return _rowcall(name, fn, rows // tile, tile, ins, [_Out(parts[0][0].shape[1], dtype)])[0]


BIG = ("w_in", "w_branch_dn", "w_branch_attn", "w_out", "ffn_up", "ffn_down")
BIG_SHARD = {"w_in": (1024, 1928, True), "w_branch_dn": (256, 1024, False), "w_branch_attn": (256, 1024, False),
             "w_out": (256, 1024, False), "ffn_up": (1024, 1408, True), "ffn_down": (704, 1024, False)}
BIG_ROWS = {k: r * c // 2 // 128 for k, (r, c, _) in BIG_SHARD.items()}
PIECE = 19456
assert sum(BIG_ROWS.values()) <= PIECE


def _gather_weights(shards, ci):
    halves = []
    for k in BIG:
        r, c, _ = BIG_SHARD[k]
        halves.append(lax.dynamic_slice_in_dim(shards[k], ci * (r // 2), r // 2, axis=0).astype(BF16))
    out = {}
    for k, ag in zip(BIG, _all_gather_many(halves, "ag_weights")):
        r, c, by_col = BIG_SHARD[k]
        blk = ag.reshape(4, r, c)
        out[k] = jnp.transpose(blk, (1, 0, 2)).reshape(r, 4 * c) if by_col else blk.reshape(4 * r, c)
    return out


def _pack_pieces(full):
    parts = [full["w_in_t"].reshape(N_DEV, BIG_ROWS["w_in"], 128).astype(BF16)]
    for k in BIG[1:]:
        r, c, by_col = BIG_SHARD[k]
        a = full[k]
        if by_col:
            a = jnp.transpose(a.reshape(r, 4, c), (1, 0, 2))
        parts.append(a.reshape(N_DEV, BIG_ROWS[k], 128).astype(BF16))
    parts.append(jnp.zeros((N_DEV, PIECE - sum(BIG_ROWS.values()), 128), BF16))
    return jnp.concatenate(parts, axis=1).reshape(N_DEV * PIECE, 128)


def _reduce_scatter(pieces, ci, shard):
    half = N_DEV // 2 * PIECE
    theirs = _sibling_exchange(pieces, PIECE, N_DEV // 2, True, "rs_d2d")
    own = lax.dynamic_index_in_dim(pieces.reshape(N_DEV // 2, 2, PIECE, 128), ci, axis=1, keepdims=False).reshape(half, 128)
    part = _add_rows([(own, 0), (theirs, 0)], half, BF16, "rs_sum_chip")
    recv = _chip_exchange(part, PIECE, "rs_ici")
    own2 = lax.dynamic_slice_in_dim(part, shard * PIECE, PIECE, axis=0)
    mine = _add_rows([(own2, 0), (recv, 0), (recv, PIECE), (recv, 2 * PIECE)], PIECE, F32, "rs_sum_all")
    other = _sibling_exchange(mine, PIECE, 1, False, "rs_pair")
    return jnp.where(ci == 0, jnp.stack([mine, other]), jnp.stack([other, mine]))


def _unpack_shard(two):
    out, off = {}, 0
    for k in BIG:
        r, c, _ = BIG_SHARD[k]
        blk = two[:, off:off + BIG_ROWS[k]]
        out[k] = blk.reshape(c, r).T if k == "w_in" else blk.reshape(r, c)
        off += BIG_ROWS[k]
    return out


SMALL = (("dn_conv", 120), ("ffn_conv", 132), ("ffn_conv_b", 44), ("norm_mix", 8), ("norm_ffn", 8), ("dn_a_log", 1),
         ("dn_dt_bias", 1), ("dn_norm", 1), ("q_norm", 1), ("k_norm", 1), ("attn_sink", 1), ("dmod_c", 48), ("dmod_x", 48))
SMALL_ROWS = 416


def _rows128(a, rows):
    flat = a.reshape(-1)
    return jnp.concatenate([flat, jnp.zeros((rows * 128 - flat.shape[0],), F32)]).reshape(rows, 128)


def _pack_small(g):
    parts = [_rows128(g[k], r) for k, r in SMALL]
    parts.append(jnp.zeros((SMALL_ROWS - sum(r for _, r in SMALL), 128), F32))
    return jnp.concatenate(parts, axis=0)


def _unpack_small(buf, shapes):
    out, off = {}, 0
    for k, r in SMALL:
        n = math.prod(shapes[k])
        out[k] = buf[off:off + r].reshape(-1)[:n].reshape(shapes[k])
        off += r
    return out


WEIGHTS = ("c_ctx", "w_ada", "b_ada", "norm_mix", "norm_ffn", "w_in", "dn_conv", "dn_a_log", "dn_dt_bias", "dn_norm",
           "q_norm", "k_norm", "attn_sink", "w_branch_dn", "w_branch_attn", "w_out", "ffn_up", "ffn_conv", "ffn_conv_b",
           "ffn_down")


def kernel(x, c, ctx, c_ctx, w_ada, b_ada, norm_mix, norm_ffn, w_in, dn_conv, dn_a_log, dn_dt_bias, dn_norm, q_norm, k_norm, attn_sink, w_branch_dn, w_branch_attn, w_out, ffn_up, ffn_conv, ffn_conv_b, ffn_down, loss_target, m_c_ctx, m_w_ada, m_b_ada, m_norm_mix, m_norm_ffn, m_w_in, m_dn_conv, m_dn_a_log, m_dn_dt_bias, m_dn_norm, m_q_norm, m_k_norm, m_attn_sink, m_w_branch_dn, m_w_branch_attn, m_w_out, m_ffn_up, m_ffn_conv, m_ffn_conv_b, m_ffn_down, v_c_ctx, v_w_ada, v_b_ada, v_norm_mix, v_norm_ffn, v_w_in, v_dn_conv, v_dn_a_log, v_dn_dt_bias, v_dn_norm, v_q_norm, v_k_norm, v_attn_sink, v_w_branch_dn, v_w_branch_attn, v_w_out, v_ffn_up, v_ffn_conv, v_ffn_conv_b, v_ffn_down):
    args = dict(locals())
    xi, yi, ci = _pos()
    dev = 4 * xi + 2 * yi + ci
    shard = 2 * xi + yi
    chips = lambda a: a[0::2]

    blk = jnp.concatenate([_rows128(c, 8), _rows128(dn_conv, 30), _rows128(ffn_conv, 33), jnp.zeros((1, 128), F32)], axis=0)
    ag = _all_gather(blk, "ag_small_in").reshape(N_DEV, 72, 128)
    c_all = ag[:, 0:8].reshape(N_DEV, D)
    dn_conv_full = jnp.transpose(chips(ag)[:, 8:38].reshape(4, 5, 768), (1, 0, 2)).reshape(5, 3 * D)
    ffn_conv_full = jnp.transpose(chips(ag)[:, 38:71].reshape(4, 3, 1408), (1, 0, 2)).reshape(3, 2 * DFF)

    c16 = jnp.concatenate([c_all, c_ctx[None], jnp.zeros((7, D), F32)], axis=0)
    a16 = _rowcall("ada_silu", lambda i, j, v: (_silu(v),), 1, 16, [_In(c16)], [_Out(D)])[0]
    m_sh = _mm(a16, w_ada[0], tm=16, tn=512, tk=D, name="ada_fwd", hi=True)
    mod16 = chips(_all_gather(m_sh, "ag_mod").reshape(N_DEV, 16, 1536))
    mod16 = jnp.transpose(mod16, (1, 0, 2)).reshape(16, 6 * D) + b_ada
    mod_x = lax.dynamic_slice_in_dim(mod16, dev, 1, axis=0)
    mod_c = mod16[8:9]

    shards = {k: args[k][0] for k in BIG}
    wfull = _gather_weights(shards, ci)
    w = dict(wfull)
    w["w_in_p"] = _pad_w_in(wfull["w_in"])
    w.update(norm_mix=norm_mix, norm_ffn=norm_ffn, dn_conv=dn_conv_full, dn_a_log=dn_a_log[0], dn_dt_bias=dn_dt_bias[0],
             dn_norm=dn_norm, q_norm=q_norm, k_norm=k_norm, attn_sink=attn_sink, ffn_conv=ffn_conv_full, ffn_conv_b=ffn_conv_b)

    loss_part, grad_x, g, dmod_x, dmod_c = _local_step(x[0], ctx[0], loss_target[0], mod_x, mod_c, w)
    loss = lax.psum(loss_part[0, 0], ("x", "y", "c"))

    g["w_in_t"] = _unpad_w_in(_transpose_cast(g["w_in_p"], BF16, "w_in_grad_t"), axis=0)
    gshard = _unpack_shard(_reduce_scatter(_pack_pieces(g), ci, shard))

    g["dmod_c"], g["dmod_x"] = dmod_c, dmod_x
    ag_s = _all_gather(_pack_small(g), "ag_small_grads")
    shapes = {k: g[k].shape for k, _ in SMALL}
    gs = _unpack_small(_sum_slots(ag_s, N_DEV, SMALL_ROWS, SMALL_ROWS, "small_sum"), shapes)
    dx_all = ag_s.reshape(N_DEV, SMALL_ROWS, 128)[:, SMALL_ROWS - 50:SMALL_ROWS - 2].reshape(N_DEV, 6 * D)

    d16 = jnp.concatenate([dx_all, gs["dmod_c"], jnp.zeros((7, 6 * D), F32)], axis=0)
    d16_sh = lax.dynamic_slice_in_dim(d16, shard * 1536, 1536, axis=1)
    g_w_ada = _mm(a16, d16_sh, ta=True, tm=D, tn=512, tk=16, name="ada_dw", hi=True)
    g_b_ada = _rowcall("ada_db", lambda i, j, v: (_colsum(v),), 1, 16, [_In(d16)], [_Out(6 * D, acc=True)])[0]
    da_part = _mm(d16_sh, w_ada[0], tb=True, tm=16, tn=D, tk=512, name="ada_dx", hi=True)
    da_all = _all_gather(da_part, "ag_ada_dx")
    da16 = _sum_slots(da_all, 4, 16, 16, "ada_dx_sum", stride=2)
    dc16 = _rowcall("ada_dsilu", lambda i, j, d_, v: (d_ * _dsilu(v),), 1, 16, [_In(da16), _In(c16)], [_Out(D)])[0]

    grads = {
        "c_ctx": dc16[8], "w_ada": g_w_ada[None], "b_ada": g_b_ada, "norm_mix": gs["norm_mix"], "norm_ffn": gs["norm_ffn"],
        "w_in": gshard["w_in"][None],
        "dn_conv": lax.dynamic_slice_in_dim(gs["dn_conv"], shard * 768, 768, axis=1)[None],
        "dn_a_log": gs["dn_a_log"][None], "dn_dt_bias": gs["dn_dt_bias"][None], "dn_norm": gs["dn_norm"],
        "q_norm": gs["q_norm"], "k_norm": gs["k_norm"], "attn_sink": gs["attn_sink"],
        "w_branch_dn": gshard["w_branch_dn"][None], "w_branch_attn": gshard["w_branch_attn"][None],
        "w_out": gshard["w_out"][None], "ffn_up": gshard["ffn_up"][None],
        "ffn_conv": lax.dynamic_slice_in_dim(gs["ffn_conv"], shard * 1408, 1408, axis=1)[None],
        "ffn_conv_b": gs["ffn_conv_b"], "ffn_down": gshard["ffn_down"][None],
    }
    deltas, new_m, new_v = [], [], []
    for k in WEIGHTS:
        d_, m_, v_ = _adamw(args[k], grads[k], args["m_" + k], args["v_" + k], "adamw_" + k)
        deltas.append(d_)
        new_m.append(m_)
        new_v.append(v_)
    return (loss, grad_x[None], *[grads[k] for k in WEIGHTS], *deltas, *new_m, *new_v)
```
